```python
import math
import jax, jax.numpy as jnp
from jax import lax
import numpy as np

D_MODEL = 1024
BATCH = 16
SEQ = 2048
DEPTH = 1

HEAD_DIM = 64
N_HEADS_FOX = 8
N_HEADS_SB = 8
D_FOX = N_HEADS_FOX * HEAD_DIM
D_SB = N_HEADS_SB * HEAD_DIM
D_FF = 4 * D_MODEL
D_PLE = 256
Q_BLOCK = 128
EPS = 1e-6
IN_SIZES = (D_FOX, D_FOX, D_FOX, N_HEADS_FOX, D_SB, D_SB, D_SB, D_MODEL, D_MODEL)
D_IN = 3 * D_FOX + N_HEADS_FOX + 3 * D_SB + 2 * D_MODEL

kernel_name = "fox_stickbreaking_gated_hybrid_block"


def _rmsnorm(x, g):
    xf = x.astype(jnp.float32)
    r = lax.rsqrt(jnp.mean(xf * xf, axis=-1, keepdims=True) + EPS)
    return (xf * r * g.astype(jnp.float32)).astype(x.dtype)


def _split_cols(u):
    offs = []
    o = 0
    for s in IN_SIZES[:-1]:
        o += s
        offs.append(o)
    return jnp.split(u, offs, axis=-1)


def _heads(u, n_heads):
    b, s, _ = u.shape
    return u.reshape(b, s, n_heads, HEAD_DIM).transpose(0, 2, 1, 3).astype(jnp.float32)


def _merge_heads(o, dtype):
    b, h, s, d = o.shape
    return o.transpose(0, 2, 1, 3).reshape(b, s, h * d).astype(dtype)


def _forgetting_attention(q, k, v, log_f):
    s_len = q.shape[2]
    scale = HEAD_DIM ** -0.5
    c = jnp.cumsum(log_f, axis=-1)
    outs = []
    for blk in range(s_len // Q_BLOCK):
        q0 = blk * Q_BLOCK
        k_end = q0 + Q_BLOCK
        qb = q[:, :, q0:k_end]
        kb = k[:, :, :k_end]
        vb = v[:, :, :k_end]
        logits = (jnp.einsum('bhqd,bhkd->bhqk', qb, kb) * scale
                  + c[:, :, q0:k_end, None] - c[:, :, None, :k_end])
        q_pos = q0 + jnp.arange(Q_BLOCK)
        k_pos = jnp.arange(k_end)
        causal = k_pos[None, :] <= q_pos[:, None]
        logits = jnp.where(causal, logits, -jnp.inf)
        probs = jax.nn.softmax(logits, axis=-1)
        outs.append(jnp.einsum('bhqk,bhkd->bhqd', probs, vb))
    return jnp.concatenate(outs, axis=2)


def _stick_breaking_attention(q, k, v):
    s_len = q.shape[2]
    scale = HEAD_DIM ** -0.5
    outs = []
    for blk in range(s_len // Q_BLOCK):
        q0 = blk * Q_BLOCK
        k_end = q0 + Q_BLOCK
        qb = q[:, :, q0:k_end]
        kb = k[:, :, :k_end]
        vb = v[:, :, :k_end]
        z = jnp.einsum('bhqd,bhkd->bhqk', qb, kb) * scale
        q_pos = q0 + jnp.arange(Q_BLOCK)
        k_pos = jnp.arange(k_end)
        strict = k_pos[None, :] < q_pos[:, None]
        log_1m_beta = jnp.where(strict, jax.nn.log_sigmoid(-z), 0.0)
        tail = lax.cumsum(log_1m_beta, axis=3, reverse=True) - log_1m_beta
        weights = jnp.where(strict, jnp.exp(jax.nn.log_sigmoid(z) + tail), 0.0)
        outs.append(jnp.einsum('bhqk,bhkd->bhqd', weights, vb))
    return jnp.concatenate(outs, axis=2)


def _fwd_setup_inputs(seed: int = 0) -> dict:
    key = jax.random.key(seed)
    ks = jax.random.split(key, 16)
    f32 = jnp.float32

    def w(k, shape, fan_in, gain=1.0):
        return jax.random.normal(k, shape, f32) * (gain * fan_in ** -0.5)

    def gain(k, shape):
        return 1.0 + 0.02 * jax.random.normal(k, shape, f32)

    return {
        "x": jax.random.normal(ks[0], (BATCH, SEQ, D_MODEL), f32),
        "p": jax.random.normal(ks[1], (DEPTH, BATCH, SEQ, D_PLE), f32),
        "g_mix": gain(ks[2], (DEPTH, D_MODEL)),
        "w_in": w(ks[3], (DEPTH, D_MODEL, D_IN), D_MODEL),
        "b_forget": jax.random.uniform(ks[4], (DEPTH, N_HEADS_FOX), f32, 1.0, 4.0),
        "b_gate": 0.01 * jax.random.normal(ks[5], (DEPTH, 2, D_MODEL), f32),
        "w_branch_fox": w(ks[6], (DEPTH, D_FOX, D_MODEL), D_FOX),
        "w_branch_sb": w(ks[7], (DEPTH, D_SB, D_MODEL), D_SB),
        "w_out": w(ks[8], (DEPTH, D_MODEL, D_MODEL), D_MODEL),
        "g_mlp": gain(ks[9], (DEPTH, D_MODEL)),
        "w_up": w(ks[10], (DEPTH, D_MODEL, D_FF), D_MODEL),
        "w_down": w(ks[11], (DEPTH, D_FF, D_MODEL), D_FF, gain=0.5),
        "g_ple": gain(ks[12], (DEPTH, D_MODEL)),
        "w_ple_gate": w(ks[13], (DEPTH, D_MODEL, D_MODEL), D_MODEL),
        "w_ple": w(ks[14], (DEPTH, D_PLE, D_MODEL), D_PLE),
        "g_final": gain(ks[15], (D_MODEL,)),
    }


def _fwd_reference(x, p, g_mix, w_in, b_forget, b_gate, w_branch_fox, w_branch_sb, w_out,
              g_mlp, w_up, w_down, g_ple, w_ple_gate, w_ple, g_final):
    b, s, _ = x.shape
    for i in range(DEPTH):
        h = _rmsnorm(x, g_mix[i])
        u = h @ w_in[i]
        q_a, k_a, v_a, f_a, q_b, k_b, v_b, gl_a, gl_b = _split_cols(u)
        log_f = jax.nn.log_sigmoid((f_a + b_forget[i]).astype(jnp.float32)).transpose(0, 2, 1)
        o_fox = _forgetting_attention(_heads(q_a, N_HEADS_FOX), _heads(k_a, N_HEADS_FOX),
                                      _heads(v_a, N_HEADS_FOX), log_f)
        o_sb = _stick_breaking_attention(_heads(q_b, N_HEADS_SB), _heads(k_b, N_HEADS_SB),
                                         _heads(v_b, N_HEADS_SB))
        o_fox = _merge_heads(o_fox, x.dtype) @ w_branch_fox[i]
        o_sb = _merge_heads(o_sb, x.dtype) @ w_branch_sb[i]
        merged = (jax.nn.sigmoid(gl_a + b_gate[i, 0]) * o_fox
                  + jax.nn.sigmoid(gl_b + b_gate[i, 1]) * o_sb)
        x = x + merged @ w_out[i]
        h = _rmsnorm(x, g_mlp[i])
        x = x + jnp.square(jax.nn.relu(h @ w_up[i])) @ w_down[i]
        h = _rmsnorm(x, g_ple[i])
        x = x + jax.nn.sigmoid(h @ w_ple_gate[i]) * (p[i] @ w_ple[i])
    return _rmsnorm(x, g_final)


import jax as _jax
import jax.numpy as _jnp

TWIN_FORMAT = 'train_step'
FWD_PARAMS = ['x', 'p', 'g_mix', 'w_in', 'b_forget', 'b_gate', 'w_branch_fox', 'w_branch_sb', 'w_out', 'g_mlp', 'w_up', 'w_down', 'g_ple', 'w_ple_gate', 'w_ple', 'g_final']
TWIN_WEIGHTS = ['g_mix', 'w_in', 'b_forget', 'b_gate', 'w_branch_fox', 'w_branch_sb', 'w_out', 'g_mlp', 'w_up', 'w_down', 'g_ple', 'w_ple_gate', 'w_ple', 'g_final']
TWIN_DIFF_INPUT = 'x'
TWIN_INPUTS = ['x', 'p', 'g_mix', 'w_in', 'b_forget', 'b_gate', 'w_branch_fox', 'w_branch_sb', 'w_out', 'g_mlp', 'w_up', 'w_down', 'g_ple', 'w_ple_gate', 'w_ple', 'g_final', 'loss_target', 'm_g_mix', 'm_w_in', 'm_b_forget', 'm_b_gate', 'm_w_branch_fox', 'm_w_branch_sb', 'm_w_out', 'm_g_mlp', 'm_w_up', 'm_w_down', 'm_g_ple', 'm_w_ple_gate', 'm_w_ple', 'm_g_final', 'v_g_mix', 'v_w_in', 'v_b_forget', 'v_b_gate', 'v_w_branch_fox', 'v_w_branch_sb', 'v_w_out', 'v_g_mlp', 'v_w_up', 'v_w_down', 'v_g_ple', 'v_w_ple_gate', 'v_w_ple', 'v_g_final']
TWIN_OUTPUTS = ['loss', 'grad_x', 'grad_g_mix', 'grad_w_in', 'grad_b_forget', 'grad_b_gate', 'grad_w_branch_fox', 'grad_w_branch_sb', 'grad_w_out', 'grad_g_mlp', 'grad_w_up', 'grad_w_down', 'grad_g_ple', 'grad_w_ple_gate', 'grad_w_ple', 'grad_g_final', 'delta_g_mix', 'delta_w_in', 'delta_b_forget', 'delta_b_gate', 'delta_w_branch_fox', 'delta_w_branch_sb', 'delta_w_out', 'delta_g_mlp', 'delta_w_up', 'delta_w_down', 'delta_g_ple', 'delta_w_ple_gate', 'delta_w_ple', 'delta_g_final', 'new_m_g_mix', 'new_m_w_in', 'new_m_b_forget', 'new_m_b_gate', 'new_m_w_branch_fox', 'new_m_w_branch_sb', 'new_m_w_out', 'new_m_g_mlp', 'new_m_w_up', 'new_m_w_down', 'new_m_g_ple', 'new_m_w_ple_gate', 'new_m_w_ple', 'new_m_g_final', 'new_v_g_mix', 'new_v_w_in', 'new_v_b_forget', 'new_v_b_gate', 'new_v_w_branch_fox', 'new_v_w_branch_sb', 'new_v_w_out', 'new_v_g_mlp', 'new_v_w_up', 'new_v_w_down', 'new_v_g_ple', 'new_v_w_ple_gate', 'new_v_w_ple', 'new_v_g_final']
TWIN_LEAF_KINDS = {'loss': 'loss', 'grad_x': 'grad_x', 'grad_g_mix': 'grad_w', 'grad_w_in': 'grad_w', 'grad_b_forget': 'grad_w', 'grad_b_gate': 'grad_w', 'grad_w_branch_fox': 'grad_w', 'grad_w_branch_sb': 'grad_w', 'grad_w_out': 'grad_w', 'grad_g_mlp': 'grad_w', 'grad_w_up': 'grad_w', 'grad_w_down': 'grad_w', 'grad_g_ple': 'grad_w', 'grad_w_ple_gate': 'grad_w', 'grad_w_ple': 'grad_w', 'grad_g_final': 'grad_w', 'delta_g_mix': 'delta_w', 'delta_w_in': 'delta_w', 'delta_b_forget': 'delta_w', 'delta_b_gate': 'delta_w', 'delta_w_branch_fox': 'delta_w', 'delta_w_branch_sb': 'delta_w', 'delta_w_out': 'delta_w', 'delta_g_mlp': 'delta_w', 'delta_w_up': 'delta_w', 'delta_w_down': 'delta_w', 'delta_g_ple': 'delta_w', 'delta_w_ple_gate': 'delta_w', 'delta_w_ple': 'delta_w', 'delta_g_final': 'delta_w', 'new_m_g_mix': 'new_m', 'new_m_w_in': 'new_m', 'new_m_b_forget': 'new_m', 'new_m_b_gate': 'new_m', 'new_m_w_branch_fox': 'new_m', 'new_m_w_branch_sb': 'new_m', 'new_m_w_out': 'new_m', 'new_m_g_mlp': 'new_m', 'new_m_w_up': 'new_m', 'new_m_w_down': 'new_m', 'new_m_g_ple': 'new_m', 'new_m_w_ple_gate': 'new_m', 'new_m_w_ple': 'new_m', 'new_m_g_final': 'new_m', 'new_v_g_mix': 'new_v', 'new_v_w_in': 'new_v', 'new_v_b_forget': 'new_v', 'new_v_b_gate': 'new_v', 'new_v_w_branch_fox': 'new_v', 'new_v_w_branch_sb': 'new_v', 'new_v_w_out': 'new_v', 'new_v_g_mlp': 'new_v', 'new_v_w_up': 'new_v', 'new_v_w_down': 'new_v', 'new_v_g_ple': 'new_v', 'new_v_w_ple_gate': 'new_v', 'new_v_w_ple': 'new_v', 'new_v_g_final': 'new_v'}


def _forward(args):
    return _fwd_reference(*[args[k] for k in FWD_PARAMS])


def _output_shape():
    out = _jax.eval_shape(lambda: _forward(_fwd_setup_inputs(0)))
    return out.shape, out.dtype

N_MICROBATCH = 1
ADAM_LR = 0.001
ADAM_B1 = 0.9
ADAM_B2 = 0.999
ADAM_EPS = 1e-08
ADAM_WD = 0.01
ADAM_STEP = 10
PER_EXAMPLE_BATCH_AXIS = {'x': 0, 'p': 1, 'loss_target': 0}
SHARED_INPUTS = []
_WEIGHT_DTYPES = {'g_mix': _jnp.float32, 'w_in': _jnp.float32, 'b_forget': _jnp.float32, 'b_gate': _jnp.float32, 'w_branch_fox': _jnp.float32, 'w_branch_sb': _jnp.float32, 'w_out': _jnp.float32, 'g_mlp': _jnp.float32, 'w_up': _jnp.float32, 'w_down': _jnp.float32, 'g_ple': _jnp.float32, 'w_ple_gate': _jnp.float32, 'w_ple': _jnp.float32, 'g_final': _jnp.float32}
MOMENT_SCALE = {'g_mix': 9.144785e-02, 'w_in': 3.873723e-02, 'b_forget': 2.406353e-01, 'b_gate': 1.840463e-02, 'w_branch_fox': 3.464486e-02, 'w_branch_sb': 5.564526e-02, 'w_out': 6.495346e-02, 'g_mlp': 9.789510e-02, 'w_up': 4.692222e-02, 'w_down': 1.664584e-01, 'g_ple': 2.929458e-02, 'w_ple_gate': 2.887043e-02, 'w_ple': 7.106577e-02, 'g_final': 3.197863e+01}


def _to_microbatches(a, axis):
    t = _jnp.moveaxis(a, axis, 0)
    t = t.reshape((N_MICROBATCH, t.shape[0] // N_MICROBATCH) + t.shape[1:])
    return _jnp.moveaxis(t, 1, axis + 1)


def setup_inputs(seed: int = 0) -> dict:
    inp = _fwd_setup_inputs(seed)
    key = _jax.random.fold_in(_jax.random.key(seed), 7919)
    shape, _ = _output_shape()
    out = dict(inp)
    out["loss_target"] = _jax.random.normal(_jax.random.fold_in(key, 0), shape, _jnp.float32)
    for i, name in enumerate(TWIN_WEIGHTS):
        w = inp[name].astype(_jnp.float32)
        if MOMENT_SCALE is None:
            s = _jnp.sqrt(_jnp.mean(_jnp.square(w)) + 1e-30)
        else:
            s = MOMENT_SCALE[name]
        km, kv = _jax.random.split(_jax.random.fold_in(key, i + 1))
        out[name] = w
        out["m_" + name] = s * _jax.random.normal(km, w.shape, _jnp.float32)
        out["v_" + name] = (s * s) * _jax.random.uniform(kv, w.shape, _jnp.float32, 0.5, 1.5)
    if N_MICROBATCH > 1:
        for name, axis in PER_EXAMPLE_BATCH_AXIS.items():
            out[name] = _to_microbatches(out[name], axis)
    return {'x': out['x'], 'p': out['p'], 'g_mix': out['g_mix'], 'w_in': out['w_in'], 'b_forget': out['b_forget'], 'b_gate': out['b_gate'], 'w_branch_fox': out['w_branch_fox'], 'w_branch_sb': out['w_branch_sb'], 'w_out': out['w_out'], 'g_mlp': out['g_mlp'], 'w_up': out['w_up'], 'w_down': out['w_down'], 'g_ple': out['g_ple'], 'w_ple_gate': out['w_ple_gate'], 'w_ple': out['w_ple'], 'g_final': out['g_final'], 'loss_target': out['loss_target'], 'm_g_mix': out['m_g_mix'], 'm_w_in': out['m_w_in'], 'm_b_forget': out['m_b_forget'], 'm_b_gate': out['m_b_gate'], 'm_w_branch_fox': out['m_w_branch_fox'], 'm_w_branch_sb': out['m_w_branch_sb'], 'm_w_out': out['m_w_out'], 'm_g_mlp': out['m_g_mlp'], 'm_w_up': out['m_w_up'], 'm_w_down': out['m_w_down'], 'm_g_ple': out['m_g_ple'], 'm_w_ple_gate': out['m_w_ple_gate'], 'm_w_ple': out['m_w_ple'], 'm_g_final': out['m_g_final'], 'v_g_mix': out['v_g_mix'], 'v_w_in': out['v_w_in'], 'v_b_forget': out['v_b_forget'], 'v_b_gate': out['v_b_gate'], 'v_w_branch_fox': out['v_w_branch_fox'], 'v_w_branch_sb': out['v_w_branch_sb'], 'v_w_out': out['v_w_out'], 'v_g_mlp': out['v_g_mlp'], 'v_w_up': out['v_w_up'], 'v_w_down': out['v_w_down'], 'v_g_ple': out['v_g_ple'], 'v_w_ple_gate': out['v_w_ple_gate'], 'v_w_ple': out['v_w_ple'], 'v_g_final': out['v_g_final']}


def _loss(weights, diff, rest, loss_target):
    with _jax.named_scope("forward"):
        args = {**rest, TWIN_DIFF_INPUT: diff, **{k: w.astype(_WEIGHT_DTYPES[k]) for k, w in weights.items()}}
        y = _forward(args)
    with _jax.named_scope("loss_head"):
        err = _jnp.square(y.astype(_jnp.float32) - loss_target)
        return 0.5 * _jnp.sum(_jnp.mean(err, axis=-1)) if err.ndim else 0.5 * err


def _adamw(w, g, m, v):
    m = ADAM_B1 * m + (1.0 - ADAM_B1) * g
    v = ADAM_B2 * v + (1.0 - ADAM_B2) * _jnp.square(g)
    m_hat = m / (1.0 - ADAM_B1 ** ADAM_STEP)
    v_hat = v / (1.0 - ADAM_B2 ** ADAM_STEP)
    delta = -ADAM_LR * (m_hat / (_jnp.sqrt(v_hat) + ADAM_EPS) + ADAM_WD * w)
    return delta, m, v


def reference(x, p, g_mix, w_in, b_forget, b_gate, w_branch_fox, w_branch_sb, w_out, g_mlp, w_up, w_down, g_ple, w_ple_gate, w_ple, g_final, loss_target, m_g_mix, m_w_in, m_b_forget, m_b_gate, m_w_branch_fox, m_w_branch_sb, m_w_out, m_g_mlp, m_w_up, m_w_down, m_g_ple, m_w_ple_gate, m_w_ple, m_g_final, v_g_mix, v_w_in, v_b_forget, v_b_gate, v_w_branch_fox, v_w_branch_sb, v_w_out, v_g_mlp, v_w_up, v_w_down, v_g_ple, v_w_ple_gate, v_w_ple, v_g_final):
    given = dict(x=x, p=p, g_mix=g_mix, w_in=w_in, b_forget=b_forget, b_gate=b_gate, w_branch_fox=w_branch_fox, w_branch_sb=w_branch_sb, w_out=w_out, g_mlp=g_mlp, w_up=w_up, w_down=w_down, g_ple=g_ple, w_ple_gate=w_ple_gate, w_ple=w_ple, g_final=g_final, loss_target=loss_target, m_g_mix=m_g_mix, m_w_in=m_w_in, m_b_forget=m_b_forget, m_b_gate=m_b_gate, m_w_branch_fox=m_w_branch_fox, m_w_branch_sb=m_w_branch_sb, m_w_out=m_w_out, m_g_mlp=m_g_mlp, m_w_up=m_w_up, m_w_down=m_w_down, m_g_ple=m_g_ple, m_w_ple_gate=m_w_ple_gate, m_w_ple=m_w_ple, m_g_final=m_g_final, v_g_mix=v_g_mix, v_w_in=v_w_in, v_b_forget=v_b_forget, v_b_gate=v_b_gate, v_w_branch_fox=v_w_branch_fox, v_w_branch_sb=v_w_branch_sb, v_w_out=v_w_out, v_g_mlp=v_g_mlp, v_w_up=v_w_up, v_w_down=v_w_down, v_g_ple=v_g_ple, v_w_ple_gate=v_w_ple_gate, v_w_ple=v_w_ple, v_g_final=v_g_final)
    weights = {n: given[n] for n in TWIN_WEIGHTS}
    shared = {n: given[n] for n in SHARED_INPUTS}
    per_example = {n: given[n] for n in ['x', 'p']}
    grad_fn = _jax.value_and_grad(_loss, argnums=(0, 1))

    def one_microbatch(ex, loss_target):
        ex = dict(ex)
        diff = ex.pop(TWIN_DIFF_INPUT)
        return grad_fn(weights, diff, {**shared, **ex}, loss_target)

    if N_MICROBATCH == 1:
        loss, (grad_w, grad_x) = one_microbatch(per_example, given["loss_target"])
    else:
        def body(carry, xs):
            loss_sum, grad_sum = carry
            l_k, (gw_k, gx_k) = one_microbatch(xs[0], xs[1])
            with _jax.named_scope("update"):
                return (loss_sum + l_k, _jax.tree.map(_jnp.add, grad_sum, gw_k)), gx_k

        init = (_jnp.zeros((), _jnp.float32), _jax.tree.map(_jnp.zeros_like, weights))
        (loss, grad_w), grad_x = _jax.lax.scan(body, init, (per_example, given["loss_target"]))
    with _jax.named_scope("update"):
        delta_w, new_m, new_v = {}, {}, {}
        for n in TWIN_WEIGHTS:
            delta_w[n], new_m[n], new_v[n] = _adamw(weights[n], grad_w[n], given["m_" + n], given["v_" + n])
    return (loss, grad_x, *[grad_w[n] for n in TWIN_WEIGHTS], *[delta_w[n] for n in TWIN_WEIGHTS],
            *[new_m[n] for n in TWIN_WEIGHTS], *[new_v[n] for n in TWIN_WEIGHTS])
```

```python
import functools

import jax
import jax.numpy as jnp
from jax import lax
from jax.experimental import pallas as pl
from jax.experimental.pallas import tpu as pltpu

F32 = jnp.float32
BF16 = jnp.bfloat16

HEAD_DIM = 64
N_HEADS = 8
D_BRANCH = N_HEADS * HEAD_DIM
EPS = 1e-6
ADAM_LR = 0.001
ADAM_B1 = 0.9
ADAM_B2 = 0.999
ADAM_EPS = 1e-08
ADAM_WD = 0.01
ADAM_STEP = 10

N_DEV = 8
LANES = 128
PACK_W = 1024
PACK_ROW_ALIGN = 64
TM = 256
TQ = 128
TK = 128
NEG = -1e30
VMEM_LIMIT = 56 * 1024 * 1024
MESH = pl.DeviceIdType.MESH


def _dot(a, b):
    return jnp.dot(a, b, preferred_element_type=F32)


def _dot_nt(a, b):
    return lax.dot_general(a, b, (((1,), (1,)), ((), ())), preferred_element_type=F32)


def _dot_tn(a, b):
    return lax.dot_general(a, b, (((0,), (0,)), ((), ())), preferred_element_type=F32)


def _sigmoid(x):
    return 1.0 / (1.0 + jnp.exp(-x))


def _softplus(x):
    return jnp.maximum(x, 0.0) + jnp.log(1.0 + jnp.exp(-jnp.abs(x)))


def _split2(x):
    hi = x.astype(BF16)
    lo = (x - hi.astype(F32)).astype(BF16)
    return hi, lo


def _split3(x):
    hi = x.astype(BF16)
    r = x - hi.astype(F32)
    mid = r.astype(BF16)
    lo = (r - mid.astype(F32)).astype(BF16)
    return hi, mid, lo


def _rows_dot_mask(x, mask_bf16):
    hi, lo = _split2(x)
    return _dot(hi, mask_bf16) + _dot(lo, mask_bf16)


def _tri(n, rel):
    r = lax.broadcasted_iota(jnp.int32, (n, n), 0)
    c = lax.broadcasted_iota(jnp.int32, (n, n), 1)
    return rel(r, c).astype(BF16)


def _rms(x):
    r = lax.rsqrt(jnp.mean(x * x, axis=-1, keepdims=True) + EPS)
    return x * r, r


def _rms_bwd(dh, xn, r, g):
    dxn = dh * g
    dx = r * (dxn - xn * jnp.mean(dxn * xn, axis=-1, keepdims=True))
    return dx, jnp.sum(dh * xn, axis=0, keepdims=True)


def _row_spec(tm, cols):
    return pl.BlockSpec((tm, cols), lambda i: (i, 0))


def _row3_spec(g, tm, cols):
    return pl.BlockSpec((g, tm, cols), lambda i: (0, i, 0))


def _const_spec(shape):
    nd = len(shape)
    return pl.BlockSpec(shape, lambda i: (0,) * nd, pipeline_mode=pl.Buffered(1))


def _acc_spec(shape):
    nd = len(shape)
    return pl.BlockSpec(shape, lambda i: (0,) * nd)


def _seq_params():
    return pltpu.CompilerParams(dimension_semantics=("arbitrary",), vmem_limit_bytes=VMEM_LIMIT)


def _mesh_pos():
    return lax.axis_index("x"), lax.axis_index("y"), lax.axis_index("c")


def _other_chips(x, y):
    return [(1 - x, y), (x, 1 - y), (1 - x, 1 - y)]


def _all_gather(pack):
    rows, width = pack.shape

    def body(x_ref, out_ref, send_sems, recv_sems, local_sem):
        x, y, c = _mesh_pos()
        me, sibling = (x, y, c), (x, y, 1 - c)
        chip = 2 * x + y
        chips = _other_chips(x, y)

        def copy(k, pc, pchip, to, src=None):
            slab = out_ref.at[pc, pchip]
            return pltpu.make_async_remote_copy(
                src_ref=slab if src is None else src, dst_ref=slab,
                send_sem=send_sems.at[k], recv_sem=recv_sems.at[k], device_id=to, device_id_type=MESH)

        mine = pltpu.make_async_copy(x_ref, out_ref.at[c, chip], local_sem)
        mine.start()
        first = [copy(0, c, chip, sibling, src=x_ref)]
        first += [copy(1 + j, c, chip, (cx, cy, c), src=x_ref) for j, (cx, cy) in enumerate(chips)]
        for cp in first:
            cp.start()
        passed = [copy(4 + j, c, 2 * cx + cy, sibling) for j, (cx, cy) in enumerate(chips)]
        for j, (cx, cy) in enumerate(chips):
            copy(1 + j, c, 2 * cx + cy, me).wait_recv()
            passed[j].start()
        copy(0, 1 - c, chip, me).wait_recv()
        for j, (cx, cy) in enumerate(chips):
            copy(4 + j, 1 - c, 2 * cx + cy, me).wait_recv()
        for cp in first + passed:
            cp.wait_send()
        mine.wait()

    return pl.pallas_call(
        body, name="all_gather_weights",
        out_shape=jax.ShapeDtypeStruct((2, 4, rows, width), pack.dtype),
        in_specs=[pl.BlockSpec(memory_space=pl.ANY)],
        out_specs=pl.BlockSpec(memory_space=pl.ANY),
        scratch_shapes=[pltpu.SemaphoreType.DMA((7,)), pltpu.SemaphoreType.DMA((7,)), pltpu.SemaphoreType.DMA],
    )(pack)


def _rs_core_pair(p_send):
    def body(p_ref, recv_ref, send_sem, recv_sem):
        x, y, c = _mesh_pos()
        cp = pltpu.make_async_remote_copy(
            src_ref=p_ref, dst_ref=recv_ref, send_sem=send_sem, recv_sem=recv_sem,
            device_id=(x, y, 1 - c), device_id_type=MESH)
        cp.start()
        cp.wait()

    return pl.pallas_call(
        body, name="reduce_scatter_core_pair",
        out_shape=jax.ShapeDtypeStruct(p_send.shape, p_send.dtype),
        in_specs=[pl.BlockSpec(memory_space=pl.ANY)],
        out_specs=pl.BlockSpec(memory_space=pl.ANY),
        scratch_shapes=[pltpu.SemaphoreType.DMA, pltpu.SemaphoreType.DMA],
    )(p_send)


def _rs_chips(chip_sums):
    def body(cs_ref, out_ref, send_sems, recv_sems, local_sem):
        x, y, c = _mesh_pos()
        chip = 2 * x + y
        chips = _other_chips(x, y)
        mine = pltpu.make_async_copy(cs_ref.at[chip], out_ref.at[chip], local_sem)
        mine.start()
        sends = []
        for j, (cx, cy) in enumerate(chips):
            sends.append(pltpu.make_async_remote_copy(
                src_ref=cs_ref.at[2 * cx + cy], dst_ref=out_ref.at[chip],
                send_sem=send_sems.at[j], recv_sem=recv_sems.at[j], device_id=(cx, cy, c), device_id_type=MESH))
            sends[-1].start()
        for j, (cx, cy) in enumerate(chips):
            pltpu.make_async_remote_copy(
                src_ref=cs_ref.at[chip], dst_ref=out_ref.at[2 * cx + cy],
                send_sem=send_sems.at[j], recv_sem=recv_sems.at[j], device_id=(x, y, c), device_id_type=MESH).wait_recv()
        for cp in sends:
            cp.wait_send()
        mine.wait()

    return pl.pallas_call(
        body, name="reduce_scatter_chips",
        out_shape=jax.ShapeDtypeStruct(chip_sums.shape, chip_sums.dtype),
        in_specs=[pl.BlockSpec(memory_space=pl.ANY)],
        out_specs=pl.BlockSpec(memory_space=pl.ANY),
        scratch_shapes=[pltpu.SemaphoreType.DMA((3,)), pltpu.SemaphoreType.DMA((3,)), pltpu.SemaphoreType.DMA],
    )(chip_sums)


def _all_reduce_small(vec):
    rows, cols = vec.shape

    def body(x_ref, land_ref, sum_ref, send_sems, recv_sems):
        x, y, c = _mesh_pos()
        me = 4 * x + 2 * y + c
        land_ref[me] = x_ref[...]
        flips = [(fx, fy, fc) for fx in (0, 1) for fy in (0, 1) for fc in (0, 1)][1:]

        def flipped(f):
            return tuple((1 - v) if b else v for v, b in zip((x, y, c), f))

        sends = []
        for k, f in enumerate(flips):
            sends.append(pltpu.make_async_remote_copy(
                src_ref=x_ref, dst_ref=land_ref.at[me], send_sem=send_sems.at[k], recv_sem=recv_sems.at[k],
                device_id=flipped(f), device_id_type=MESH))
            sends[-1].start()
        for k, f in enumerate(flips):
            px, py, pc = flipped(f)
            pltpu.make_async_remote_copy(
                src_ref=x_ref, dst_ref=land_ref.at[4 * px + 2 * py + pc], send_sem=send_sems.at[k],
                recv_sem=recv_sems.at[k], device_id=(x, y, c), device_id_type=MESH).wait_recv()
        for cp in sends:
            cp.wait_send()
        total = land_ref[0]
        for d in range(1, N_DEV):
            total = total + land_ref[d]
        sum_ref[...] = total

    vm = pl.BlockSpec(memory_space=pltpu.VMEM)
    return pl.pallas_call(
        body, name="all_reduce_small",
        out_shape=(jax.ShapeDtypeStruct((N_DEV, rows, cols), F32), jax.ShapeDtypeStruct((rows, cols), F32)),
        in_specs=[vm], out_specs=(vm, vm),
        scratch_shapes=[pltpu.SemaphoreType.DMA((7,)), pltpu.SemaphoreType.DMA((7,))],
    )(vec)[1]


def _pack_block_rows(rows):
    units = rows // PACK_ROW_ALIGN
    best = max(d for d in range(1, 9) if units % d == 0)
    return best * PACK_ROW_ALIGN


def _pair_add(keep, recv):
    n, rows, width = keep.shape
    br = _pack_block_rows(rows)

    def body(a_ref, b_ref, o_ref):
        o_ref[...] = (a_ref[...].astype(F32) + b_ref[...].astype(F32)).astype(BF16)

    spec = pl.BlockSpec((1, br, width), lambda j, i: (j, i, 0))
    return pl.pallas_call(
        body, name="pair_add", grid=(n, rows // br),
        out_shape=jax.ShapeDtypeStruct(keep.shape, BF16),
        in_specs=[spec, spec], out_specs=spec,
    )(keep, recv)


def _sum_chips(parts):
    n, rows, width = parts.shape
    br = _pack_block_rows(rows)

    def body(p_ref, o_ref):
        total = p_ref[0].astype(F32)
        for j in range(1, n):
            total = total + p_ref[j].astype(F32)
        o_ref[...] = total

    return pl.pallas_call(
        body, name="sum_chips", grid=(rows // br,),
        out_shape=jax.ShapeDtypeStruct((rows, width), F32),
        in_specs=[pl.BlockSpec((n, br, width), lambda i: (0, i, 0))],
        out_specs=pl.BlockSpec((br, width), lambda i: (i, 0)),
    )(parts)


def _adamw(name, w, g, m, v):
    rows, cols = w.shape
    tr = rows
    for t in (512, 256, 128, 64, 32, 16, 8):
        if rows % t == 0 and t * cols * 4 <= (1 << 20):
            tr = t
            break

    def body(w_ref, g_ref, m_ref, v_ref, d_ref, nm_ref, nv_ref):
        gg = g_ref[...]
        nm = ADAM_B1 * m_ref[...] + (1.0 - ADAM_B1) * gg
        nv = ADAM_B2 * v_ref[...] + (1.0 - ADAM_B2) * (gg * gg)
        m_hat = nm / (1.0 - ADAM_B1 ** ADAM_STEP)
        v_hat = nv / (1.0 - ADAM_B2 ** ADAM_STEP)
        d_ref[...] = -ADAM_LR * (m_hat / (jnp.sqrt(v_hat) + ADAM_EPS) + ADAM_WD * w_ref[...])
        nm_ref[...] = nm
        nv_ref[...] = nv

    spec = pl.BlockSpec((tr, cols), lambda i: (i, 0))
    shp = jax.ShapeDtypeStruct((rows, cols), F32)
    return pl.pallas_call(
        body, name=name, grid=(rows // tr,), out_shape=(shp, shp, shp),
        in_specs=[spec] * 4, out_specs=(spec,) * 3,
    )(w, g, m, v)


def _matmul_tn(name, a, b, relu2=False):
    squeeze = b.ndim == 2
    if squeeze:
        b = b[None]
    t_len, k_len = a.shape
    groups, _, n_len = b.shape
    tt = min(t_len, 512)
    tk = min(k_len, 512)
    tn = min(n_len, 1024)
    nt = t_len // tt

    def body(a_ref, b_ref, o_ref):
        @pl.when(pl.program_id(3) == 0)
        def _():
            o_ref[...] = jnp.zeros_like(o_ref)

        av = a_ref[...]
        if relu2:
            av = jnp.square(jnp.maximum(av.astype(F32), 0.0))
        o_ref[...] += _dot_tn(av.astype(BF16), b_ref[...].astype(BF16))

    out = pl.pallas_call(
        body, name=name, grid=(groups, k_len // tk, n_len // tn, nt),
        out_shape=jax.ShapeDtypeStruct((groups, k_len, n_len), F32),
        in_specs=[pl.BlockSpec((tt, tk), lambda g, i, j, t: (t, i)),
                  pl.BlockSpec((None, tt, tn), lambda g, i, j, t: (g, t, j))],
        out_specs=pl.BlockSpec((None, tk, tn), lambda g, i, j, t: (g, i, j)),
        compiler_params=pltpu.CompilerParams(
            dimension_semantics=("parallel", "parallel", "parallel", "arbitrary"), vmem_limit_bytes=VMEM_LIMIT),
    )(a, b)
    return out[0] if squeeze else out


def _inproj_fwd(x, g_mix, w_pad, bf_pad, seq):
    t_len, d = x.shape
    n_qkv = 6 * D_BRANCH
    tiles_per_seq = seq // TM

    def body(x_ref, g_ref, w_ref, bf_ref, qkv_ref, gl_ref, fpre_ref, c_ref, carry_ref):
        @pl.when(pl.program_id(0) % tiles_per_seq == 0)
        def _():
            carry_ref[...] = jnp.zeros_like(carry_ref)

        xn, _ = _rms(x_ref[...])
        h = (xn * g_ref[...]).astype(BF16)
        for j in range(6):
            qkv_ref[j] = _dot(h, w_ref[:, j * D_BRANCH:(j + 1) * D_BRANCH]).astype(BF16)
        for j in range(2):
            gl_ref[:, j * d:(j + 1) * d] = _dot(h, w_ref[:, n_qkv + j * d:n_qkv + (j + 1) * d]).astype(BF16)
        fpre = _dot(h, w_ref[:, n_qkv + 2 * d:]) + bf_ref[...]
        fpre_ref[...] = fpre
        logf = -_softplus(-fpre)
        lower = _tri(TM, lambda r, c: c <= r)
        hi, mid, lo = _split3(logf)
        c_val = carry_ref[...] + _dot(lower, hi) + _dot(lower, mid) + _dot(lower, lo)
        c_ref[...] = c_val
        carry_ref[...] = carry_ref[...] + jnp.sum(logf, axis=0, keepdims=True)

    return pl.pallas_call(
        body, name="inproj_fwd", grid=(t_len // TM,),
        out_shape=(jax.ShapeDtypeStruct((6, t_len, D_BRANCH), BF16), jax.ShapeDtypeStruct((t_len, 2 * d), BF16),
                   jax.ShapeDtypeStruct((t_len, LANES), F32), jax.ShapeDtypeStruct((t_len, LANES), F32)),
        in_specs=[_row_spec(TM, d), _const_spec((1, d)), _const_spec(w_pad.shape), _const_spec((1, LANES))],
        out_specs=(_row3_spec(6, TM, D_BRANCH), _row_spec(TM, 2 * d), _row_spec(TM, LANES), _row_spec(TM, LANES)),
        scratch_shapes=[pltpu.VMEM((1, LANES), F32)],
        compiler_params=_seq_params(),
    )(x, g_mix, w_pad, bf_pad)


def _head_spec(which, seq):
    return pl.BlockSpec((None, seq, 2 * HEAD_DIM), lambda b, hp: (which, b, hp))


def _pair_spec(seq):
    return pl.BlockSpec((seq, 2 * HEAD_DIM), lambda b, hp: (b, hp))


def _col_spec(seq):
    return pl.BlockSpec((1, 2, seq, 1), lambda b, hp: (b, hp, 0, 0))


def _rowblk_spec(seq):
    return pl.BlockSpec((1, 2, seq // TK, TK), lambda b, hp: (b, hp, 0, 0))


def _attn_params():
    return pltpu.CompilerParams(dimension_semantics=("parallel", "parallel"), vmem_limit_bytes=VMEM_LIMIT)


def _k_blocks(qi):
    return ((qi + 1) * TQ + TK - 1) // TK


def _positions(q0, k0):
    qpos = q0 + lax.broadcasted_iota(jnp.int32, (TQ, TK), 0)
    kpos = k0 + lax.broadcasted_iota(jnp.int32, (TQ, TK), 1)
    return qpos, kpos


def _fox_fwd(qkv, c_col, c_row, batch, seq):
    def body(q_ref, k_ref, v_ref, cc_ref, cr_ref, o_ref, lse_ref):
        def q_loop(qi, _):
            q0 = pl.multiple_of(qi * TQ, TQ)
            outs = []
            for hh in range(2):
                cols = slice(hh * HEAD_DIM, (hh + 1) * HEAD_DIM)
                q = q_ref[pl.ds(q0, TQ), cols]
                cc = cc_ref[0, hh, pl.ds(q0, TQ), :]

                def k_loop(kj, carry, cols=cols, q=q, cc=cc, hh=hh):
                    m, l, acc = carry
                    k0 = pl.multiple_of(kj * TK, TK)
                    k = k_ref[pl.ds(k0, TK), cols]
                    v = v_ref[pl.ds(k0, TK), cols]
                    s = _dot_nt(q, k) + cc - cr_ref[0, hh, pl.ds(kj, 1), :]
                    qpos, kpos = _positions(q0, k0)
                    s = jnp.where(kpos <= qpos, s, NEG)
                    m_new = jnp.maximum(m, jnp.max(s, axis=1, keepdims=True))
                    alpha = jnp.exp(m - m_new)
                    p = jnp.exp(s - m_new)
                    l = alpha * l + jnp.sum(p, axis=1, keepdims=True)
                    acc = alpha * acc + _dot(p.astype(BF16), v)
                    return m_new, l, acc

                init = (jnp.full((TQ, 1), NEG, F32), jnp.zeros((TQ, 1), F32), jnp.zeros((TQ, HEAD_DIM), F32))
                m, l, acc = lax.fori_loop(0, _k_blocks(qi), k_loop, init)
                outs.append(acc / l)
                lse_ref[0, hh, pl.ds(q0, TQ), :] = m + jnp.log(l)
            o_ref[pl.ds(q0, TQ), :] = jnp.concatenate(outs, axis=1).astype(BF16)
            return 0

        lax.fori_loop(0, seq // TQ, q_loop, 0)

    return pl.pallas_call(
        body, name="fox_fwd", grid=(batch, N_HEADS // 2),
        out_shape=(jax.ShapeDtypeStruct((batch * seq, D_BRANCH), BF16),
                   jax.ShapeDtypeStruct((batch, N_HEADS, seq, 1), F32)),
        in_specs=[_head_spec(0, seq), _head_spec(1, seq), _head_spec(2, seq), _col_spec(seq), _rowblk_spec(seq)],
        out_specs=(_pair_spec(seq), _col_spec(seq)),
        compiler_params=_attn_params(),
    )(qkv, qkv, qkv, c_col, c_row)


def _fox_bwd(qkv, o, do, lse, c_col, c_row, batch, seq):
    def body(q_ref, k_ref, v_ref, o_ref, do_ref, lse_ref, cc_ref, cr_ref, dqkv_ref, dc_ref, dcq_ref, dq_acc):
        dq_acc[...] = jnp.zeros_like(dq_acc)
        dcq_ref[...] = jnp.zeros_like(dcq_ref)

        def k_loop(kj, _):
            k0 = pl.multiple_of(kj * TK, TK)
            dks, dvs = [], []
            for hh in range(2):
                cols = slice(hh * HEAD_DIM, (hh + 1) * HEAD_DIM)
                k = k_ref[pl.ds(k0, TK), cols]
                v = v_ref[pl.ds(k0, TK), cols]
                cr = cr_ref[0, hh, pl.ds(kj, 1), :]

                def q_loop(qi, carry, cols=cols, k=k, v=v, cr=cr, hh=hh):
                    dk, dv, dcs = carry
                    q0 = pl.multiple_of(qi * TQ, TQ)
                    q = q_ref[pl.ds(q0, TQ), cols]
                    dout = do_ref[pl.ds(q0, TQ), cols]
                    delta = jnp.sum(dout.astype(F32) * o_ref[pl.ds(q0, TQ), cols].astype(F32), axis=1, keepdims=True)
                    s = _dot_nt(q, k) + cc_ref[0, hh, pl.ds(q0, TQ), :] - cr
                    qpos, kpos = _positions(q0, k0)
                    p = jnp.where(kpos <= qpos, jnp.exp(s - lse_ref[0, hh, pl.ds(q0, TQ), :]), 0.0)
                    dv = dv + _dot_tn(p.astype(BF16), dout)
                    ds = p * (_dot_nt(dout, v) - delta)
                    dsb = ds.astype(BF16)
                    dk = dk + _dot_tn(dsb, q)
                    dq_acc[hh, pl.ds(q0, TQ), :] += _dot(dsb, k)
                    dcq_ref[0, hh, pl.ds(q0, TQ), :] += jnp.sum(ds, axis=1, keepdims=True)
                    return dk, dv, dcs + jnp.sum(ds, axis=0, keepdims=True)

                init = (jnp.zeros((TK, HEAD_DIM), F32), jnp.zeros((TK, HEAD_DIM), F32), jnp.zeros((1, TK), F32))
                dk, dv, dcs = lax.fori_loop((kj * TK) // TQ, seq // TQ, q_loop, init)
                dks.append(dk)
                dvs.append(dv)
                dc_ref[0, hh, pl.ds(kj, 1), :] = dcs
            dqkv_ref[1, pl.ds(k0, TK), :] = jnp.concatenate(dks, axis=1).astype(BF16)
            dqkv_ref[2, pl.ds(k0, TK), :] = jnp.concatenate(dvs, axis=1).astype(BF16)
            return 0

        lax.fori_loop(0, seq // TK, k_loop, 0)
        dqkv_ref[0] = jnp.concatenate([dq_acc[0], dq_acc[1]], axis=1).astype(BF16)

    return pl.pallas_call(
        body, name="fox_bwd", grid=(batch, N_HEADS // 2),
        out_shape=(jax.ShapeDtypeStruct((3, batch * seq, D_BRANCH), BF16),
                   jax.ShapeDtypeStruct((batch, N_HEADS, seq // TK, TK), F32),
                   jax.ShapeDtypeStruct((batch, N_HEADS, seq, 1), F32)),
        in_specs=[_head_spec(0, seq), _head_spec(1, seq), _head_spec(2, seq), _pair_spec(seq), _pair_spec(seq),
                  _col_spec(seq), _col_spec(seq), _rowblk_spec(seq)],
        out_specs=(pl.BlockSpec((3, seq, 2 * HEAD_DIM), lambda b, hp: (0, b, hp)), _rowblk_spec(seq), _col_spec(seq)),
        scratch_shapes=[pltpu.VMEM((2, seq, HEAD_DIM), F32)],
        compiler_params=_attn_params(),
    )(qkv, qkv, qkv, o, do, lse, c_col, c_row)


def _sb_logits(q, k, q0, k0):
    z = _dot_nt(q, k)
    qpos, kpos = _positions(q0, k0)
    strict = kpos < qpos
    return z, strict, jnp.where(strict, -_softplus(z), 0.0)


def _sb_fwd(qkv, batch, seq):
    def body(q_ref, k_ref, v_ref, o_ref, lt_ref):
        later = _tri(TK, lambda r, c: r > c)

        def q_loop(qi, _):
            q0 = pl.multiple_of(qi * TQ, TQ)
            n_k = _k_blocks(qi)
            outs = []
            for hh in range(2):
                cols = slice(hh * HEAD_DIM, (hh + 1) * HEAD_DIM)
                q = q_ref[pl.ds(q0, TQ), cols]

                def k_loop(kk, carry, cols=cols, q=q):
                    run, acc = carry
                    k0 = pl.multiple_of((n_k - 1 - kk) * TK, TK)
                    z, strict, lg = _sb_logits(q, k_ref[pl.ds(k0, TK), cols], q0, k0)
                    tail = run + _rows_dot_mask(lg, later)
                    a = jnp.where(strict, jnp.exp(z + lg + tail), 0.0)
                    acc = acc + _dot(a.astype(BF16), v_ref[pl.ds(k0, TK), cols])
                    return run + jnp.sum(lg, axis=1, keepdims=True), acc

                run, acc = lax.fori_loop(0, n_k, k_loop, (jnp.zeros((TQ, 1), F32), jnp.zeros((TQ, HEAD_DIM), F32)))
                outs.append(acc)
                lt_ref[0, hh, pl.ds(q0, TQ), :] = run
            o_ref[pl.ds(q0, TQ), :] = jnp.concatenate(outs, axis=1).astype(BF16)
            return 0

        lax.fori_loop(0, seq // TQ, q_loop, 0)

    return pl.pallas_call(
        body, name="sb_fwd", grid=(batch, N_HEADS // 2),
        out_shape=(jax.ShapeDtypeStruct((batch * seq, D_BRANCH), BF16),
                   jax.ShapeDtypeStruct((batch, N_HEADS, seq, 1), F32)),
        in_specs=[_head_spec(3, seq), _head_spec(4, seq), _head_spec(5, seq)],
        out_specs=(_pair_spec(seq), _col_spec(seq)),
        compiler_params=_attn_params(),
    )(qkv, qkv, qkv)


def _sb_bwd(qkv, do, ltot, batch, seq):
    def body(q_ref, k_ref, v_ref, do_ref, lt_ref, dqkv_ref, dk_acc, dv_acc):
        dk_acc[...] = jnp.zeros_like(dk_acc)
        dv_acc[...] = jnp.zeros_like(dv_acc)
        upto = _tri(TK, lambda r, c: r <= c)
        before = _tri(TK, lambda r, c: r < c)

        def q_loop(qi, _):
            q0 = pl.multiple_of(qi * TQ, TQ)
            dqs = []
            for hh in range(2):
                cols = slice(hh * HEAD_DIM, (hh + 1) * HEAD_DIM)
                q = q_ref[pl.ds(q0, TQ), cols]
                dout = do_ref[pl.ds(q0, TQ), cols]
                lt = lt_ref[0, hh, pl.ds(q0, TQ), :]

                def k_loop(kj, carry, cols=cols, q=q, dout=dout, lt=lt, hh=hh):
                    lsum, gsum, dq = carry
                    k0 = pl.multiple_of(kj * TK, TK)
                    k = k_ref[pl.ds(k0, TK), cols]
                    z, strict, lg = _sb_logits(q, k, q0, k0)
                    tail = lt - lsum - _rows_dot_mask(lg, upto)
                    a = jnp.where(strict, jnp.exp(z + lg + tail), 0.0)
                    g = _dot_nt(dout, v_ref[pl.ds(k0, TK), cols]) * a
                    one_minus_beta = jnp.exp(lg)
                    dz = jnp.where(strict, g * one_minus_beta
                                   - (1.0 - one_minus_beta) * (gsum + _rows_dot_mask(g, before)), 0.0)
                    dzb = dz.astype(BF16)
                    dk_acc[hh, pl.ds(k0, TK), :] += _dot_tn(dzb, q)
                    dv_acc[hh, pl.ds(k0, TK), :] += _dot_tn(a.astype(BF16), dout)
                    return (lsum + jnp.sum(lg, axis=1, keepdims=True), gsum + jnp.sum(g, axis=1, keepdims=True),
                            dq + _dot(dzb, k))

                init = (jnp.zeros((TQ, 1), F32), jnp.zeros((TQ, 1), F32), jnp.zeros((TQ, HEAD_DIM), F32))
                dqs.append(lax.fori_loop(0, _k_blocks(qi), k_loop, init)[2])
            dqkv_ref[0, pl.ds(q0, TQ), :] = jnp.concatenate(dqs, axis=1).astype(BF16)
            return 0

        lax.fori_loop(0, seq // TQ, q_loop, 0)
        dqkv_ref[1] = jnp.concatenate([dk_acc[0], dk_acc[1]], axis=1).astype(BF16)
        dqkv_ref[2] = jnp.concatenate([dv_acc[0], dv_acc[1]], axis=1).astype(BF16)

    return pl.pallas_call(
        body, name="sb_bwd", grid=(batch, N_HEADS // 2),
        out_shape=jax.ShapeDtypeStruct((3, batch * seq, D_BRANCH), BF16),
        in_specs=[_head_spec(3, seq), _head_spec(4, seq), _head_spec(5, seq), _pair_spec(seq), _col_spec(seq)],
        out_specs=pl.BlockSpec((3, seq, 2 * HEAD_DIM), lambda b, hp: (0, b, hp)),
        scratch_shapes=[pltpu.VMEM((2, seq, HEAD_DIM), F32), pltpu.VMEM((2, seq, HEAD_DIM), F32)],
        compiler_params=_attn_params(),
    )(qkv, qkv, qkv, do, ltot)


def _forget_bwd(dcq_tok, dck_tok, fpre, batch, seq):
    t_len = batch * seq
    tiles = seq // TM

    def rev(i):
        return ((i // tiles) * tiles + (tiles - 1 - i % tiles), 0)

    def body(dcq_ref, dck_ref, f_ref, df_ref, db_ref, carry_ref):
        i = pl.program_id(0)

        @pl.when(i == 0)
        def _():
            db_ref[...] = jnp.zeros_like(db_ref)

        @pl.when(i % tiles == 0)
        def _():
            carry_ref[...] = jnp.zeros_like(carry_ref)

        dc = dcq_ref[...] - dck_ref[...]
        upper = _tri(TM, lambda r, c: c >= r)
        hi, mid, lo = _split3(dc)
        dlogf = carry_ref[...] + _dot(upper, hi) + _dot(upper, mid) + _dot(upper, lo)
        carry_ref[...] = carry_ref[...] + jnp.sum(dc, axis=0, keepdims=True)
        df = dlogf * _sigmoid(-f_ref[...])
        df_ref[...] = df.astype(BF16)
        db_ref[...] += jnp.sum(df, axis=0, keepdims=True)

    return pl.pallas_call(
        body, name="forget_bwd", grid=(t_len // TM,),
        out_shape=(jax.ShapeDtypeStruct((t_len, LANES), BF16), jax.ShapeDtypeStruct((1, LANES), F32)),
        in_specs=[pl.BlockSpec((TM, LANES), rev)] * 3,
        out_specs=(pl.BlockSpec((TM, LANES), rev), _acc_spec((1, LANES))),
        scratch_shapes=[pltpu.VMEM((1, LANES), F32)],
        compiler_params=_seq_params(),
    )(dcq_tok, dck_tok, fpre)


def _mix_fwd(o_fox, o_sb, gl, x, w_bf, w_bs, w_out, b_gate):
    t_len, d = x.shape

    def body(of_ref, os_ref, gl_ref, x_ref, wbf_ref, wbs_ref, wo_ref, bg_ref, x1_ref):
        br_f = _dot(of_ref[...], wbf_ref[...])
        br_s = _dot(os_ref[...], wbs_ref[...])
        ga = _sigmoid(gl_ref[:, :d].astype(F32) + bg_ref[0:1, :])
        gb = _sigmoid(gl_ref[:, d:].astype(F32) + bg_ref[1:2, :])
        merged = ga * br_f + gb * br_s
        x1_ref[...] = x_ref[...] + _dot(merged.astype(BF16), wo_ref[...])

    return pl.pallas_call(
        body, name="mix_fwd", grid=(t_len // TM,),
        out_shape=jax.ShapeDtypeStruct((t_len, d), F32),
        in_specs=[_row_spec(TM, D_BRANCH), _row_spec(TM, D_BRANCH), _row_spec(TM, 2 * d), _row_spec(TM, d),
                  _const_spec(w_bf.shape), _const_spec(w_bs.shape), _const_spec(w_out.shape), _const_spec(b_gate.shape)],
        out_specs=_row_spec(TM, d),
        compiler_params=_seq_params(),
    )(o_fox, o_sb, gl, x, w_bf, w_bs, w_out, b_gate)


def _ff_chunk(d_ff):
    return min(d_ff, 1024)


def _mlp_fwd(x1, g_mlp, w_up, w_down):
    t_len, d = x1.shape
    d_ff = w_up.shape[1]
    ch = _ff_chunk(d_ff)

    def body(x1_ref, g_ref, wu_ref, wd_ref, a_ref, x2_ref):
        x1v = x1_ref[...]
        xn, _ = _rms(x1v)
        h = (xn * g_ref[...]).astype(BF16)
        acc = x1v
        for j in range(d_ff // ch):
            a = _dot(h, wu_ref[:, j * ch:(j + 1) * ch])
            a_ref[:, j * ch:(j + 1) * ch] = a.astype(BF16)
            acc = acc + _dot(jnp.square(jnp.maximum(a, 0.0)).astype(BF16), wd_ref[j * ch:(j + 1) * ch, :])
        x2_ref[...] = acc

    return pl.pallas_call(
        body, name="mlp_fwd", grid=(t_len // TM,),
        out_shape=(jax.ShapeDtypeStruct((t_len, d_ff), BF16), jax.ShapeDtypeStruct((t_len, d), F32)),
        in_specs=[_row_spec(TM, d), _const_spec((1, d)), _const_spec(w_up.shape), _const_spec(w_down.shape)],
        out_specs=(_row_spec(TM, d_ff), _row_spec(TM, d)),
        compiler_params=_seq_params(),
    )(x1, g_mlp, w_up, w_down)


def _head_fwd_bwd(x2, p, target, g_ple, g_final, w_pg, w_ple):
    t_len, d = x2.shape
    d_ple = p.shape[1]

    def body(x2_ref, p_ref, t_ref, gp_ref, gf_ref, wpg_ref, wple_ref,
             dx2_ref, h3_ref, dpre_ref, dpe_ref, loss_ref, dgp_ref, dgf_ref):
        @pl.when(pl.program_id(0) == 0)
        def _():
            loss_ref[...] = jnp.zeros_like(loss_ref)
            dgp_ref[...] = jnp.zeros_like(dgp_ref)
            dgf_ref[...] = jnp.zeros_like(dgf_ref)

        x2v = x2_ref[...]
        x2n, r3 = _rms(x2v)
        h3 = (x2n * gp_ref[...]).astype(BF16)
        h3_ref[...] = h3
        gate = _sigmoid(_dot(h3, wpg_ref[...]))
        pe = _dot(p_ref[...].astype(BF16), wple_ref[...])
        x3n, r4 = _rms(x2v + gate * pe)
        err = x3n * gf_ref[...] - t_ref[...]
        loss_ref[...] += jnp.full(loss_ref.shape, (0.5 / d) * jnp.sum(err * err), F32)
        dx3, dgf = _rms_bwd(err * (1.0 / d), x3n, r4, gf_ref[...])
        dgf_ref[...] += dgf
        dpe_ref[...] = (dx3 * gate).astype(BF16)
        dpre = (dx3 * pe * gate * (1.0 - gate)).astype(BF16)
        dpre_ref[...] = dpre
        dres, dgp = _rms_bwd(_dot_nt(dpre, wpg_ref[...]), x2n, r3, gp_ref[...])
        dgp_ref[...] += dgp
        dx2_ref[...] = dx3 + dres

    shp_b = jax.ShapeDtypeStruct((t_len, d), BF16)
    return pl.pallas_call(
        body, name="head_fwd_bwd", grid=(t_len // TM,),
        out_shape=(jax.ShapeDtypeStruct((t_len, d), F32), shp_b, shp_b, shp_b,
                   jax.ShapeDtypeStruct((1, LANES), F32), jax.ShapeDtypeStruct((1, d), F32),
                   jax.ShapeDtypeStruct((1, d), F32)),
        in_specs=[_row_spec(TM, d), _row_spec(TM, d_ple), _row_spec(TM, d), _const_spec((1, d)), _const_spec((1, d)),
                  _const_spec(w_pg.shape), _const_spec(w_ple.shape)],
        out_specs=(_row_spec(TM, d), _row_spec(TM, d), _row_spec(TM, d), _row_spec(TM, d),
                   _acc_spec((1, LANES)), _acc_spec((1, d)), _acc_spec((1, d))),
        compiler_params=_seq_params(),
    )(x2, p, target, g_ple, g_final, w_pg, w_ple)


def _mlp_bwd(dx2, a, x1, g_mlp, w_up, w_down):
    t_len, d = x1.shape
    d_ff = w_up.shape[1]
    ch = _ff_chunk(d_ff)

    def body(dx2_ref, a_ref, x1_ref, g_ref, wu_ref, wd_ref, dx1_ref, da_ref, h2_ref, dg_ref):
        @pl.when(pl.program_id(0) == 0)
        def _():
            dg_ref[...] = jnp.zeros_like(dg_ref)

        dx2v = dx2_ref[...]
        dx2b = dx2v.astype(BF16)
        xn, r = _rms(x1_ref[...])
        h2_ref[...] = (xn * g_ref[...]).astype(BF16)
        dh = jnp.zeros((TM, d), F32)
        for j in range(d_ff // ch):
            dact = _dot_nt(dx2b, wd_ref[j * ch:(j + 1) * ch, :])
            da = (dact * 2.0 * jnp.maximum(a_ref[:, j * ch:(j + 1) * ch].astype(F32), 0.0)).astype(BF16)
            da_ref[:, j * ch:(j + 1) * ch] = da
            dh = dh + _dot_nt(da, wu_ref[:, j * ch:(j + 1) * ch])
        dres, dg = _rms_bwd(dh, xn, r, g_ref[...])
        dg_ref[...] += dg
        dx1_ref[...] = dx2v + dres

    return pl.pallas_call(
        body, name="mlp_bwd", grid=(t_len // TM,),
        out_shape=(jax.ShapeDtypeStruct((t_len, d), F32), jax.ShapeDtypeStruct((t_len, d_ff), BF16),
                   jax.ShapeDtypeStruct((t_len, d), BF16), jax.ShapeDtypeStruct((1, d), F32)),
        in_specs=[_row_spec(TM, d), _row_spec(TM, d_ff), _row_spec(TM, d), _const_spec((1, d)),
                  _const_spec(w_up.shape), _const_spec(w_down.shape)],
        out_specs=(_row_spec(TM, d), _row_spec(TM, d_ff), _row_spec(TM, d), _acc_spec((1, d))),
        compiler_params=_seq_params(),
    )(dx2, a, x1, g_mlp, w_up, w_down)


def _mix_bwd(dx1, o_fox, o_sb, gl, w_bf, w_bs, w_out, b_gate):
    t_len, d = dx1.shape

    def body(dx1_ref, of_ref, os_ref, gl_ref, wbf_ref, wbs_ref, wo_ref, bg_ref,
             mg_ref, dbf_ref, dbs_ref, dgl_ref, dof_ref, dos_ref, dbg_ref):
        @pl.when(pl.program_id(0) == 0)
        def _():
            dbg_ref[...] = jnp.zeros_like(dbg_ref)

        dmerged = _dot_nt(dx1_ref[...].astype(BF16), wo_ref[...])
        br_f = _dot(of_ref[...], wbf_ref[...])
        br_s = _dot(os_ref[...], wbs_ref[...])
        ga = _sigmoid(gl_ref[:, :d].astype(F32) + bg_ref[0:1, :])
        gb = _sigmoid(gl_ref[:, d:].astype(F32) + bg_ref[1:2, :])
        mg_ref[...] = (ga * br_f + gb * br_s).astype(BF16)
        dbf = (dmerged * ga).astype(BF16)
        dbs = (dmerged * gb).astype(BF16)
        dbf_ref[...] = dbf
        dbs_ref[...] = dbs
        dla = dmerged * br_f * ga * (1.0 - ga)
        dlb = dmerged * br_s * gb * (1.0 - gb)
        dgl_ref[:, :d] = dla.astype(BF16)
        dgl_ref[:, d:] = dlb.astype(BF16)
        dbg_ref[0:1, :] += jnp.sum(dla, axis=0, keepdims=True)
        dbg_ref[1:2, :] += jnp.sum(dlb, axis=0, keepdims=True)
        dof_ref[...] = _dot_nt(dbf, wbf_ref[...]).astype(BF16)
        dos_ref[...] = _dot_nt(dbs, wbs_ref[...]).astype(BF16)

    shp_d = jax.ShapeDtypeStruct((t_len, d), BF16)
    shp_h = jax.ShapeDtypeStruct((t_len, D_BRANCH), BF16)
    return pl.pallas_call(
        body, name="mix_bwd", grid=(t_len // TM,),
        out_shape=(shp_d, shp_d, shp_d, jax.ShapeDtypeStruct((t_len, 2 * d), BF16), shp_h, shp_h,
                   jax.ShapeDtypeStruct((2, d), F32)),
        in_specs=[_row_spec(TM, d), _row_spec(TM, D_BRANCH), _row_spec(TM, D_BRANCH), _row_spec(TM, 2 * d),
                  _const_spec(w_bf.shape), _const_spec(w_bs.shape), _const_spec(w_out.shape), _const_spec(b_gate.shape)],
        out_specs=(_row_spec(TM, d), _row_spec(TM, d), _row_spec(TM, d), _row_spec(TM, 2 * d),
                   _row_spec(TM, D_BRANCH), _row_spec(TM, D_BRANCH), _acc_spec((2, d))),
        compiler_params=_seq_params(),
    )(dx1, o_fox, o_sb, gl, w_bf, w_bs, w_out, b_gate)


def _inproj_bwd(dqkv_a, dqkv_b, dgl, df, dx1, x, g_mix, w_pad):
    t_len, d = x.shape
    n_qkv = 6 * D_BRANCH

    def body(da_ref, db_ref, dgl_ref, df_ref, dx1_ref, x_ref, g_ref, w_ref, dx_ref, h1_ref, dg_ref):
        @pl.when(pl.program_id(0) == 0)
        def _():
            dg_ref[...] = jnp.zeros_like(dg_ref)

        xn, r = _rms(x_ref[...])
        h1_ref[...] = (xn * g_ref[...]).astype(BF16)
        dh = _dot_nt(df_ref[...], w_ref[:, n_qkv + 2 * d:])
        for j in range(3):
            dh = dh + _dot_nt(da_ref[j], w_ref[:, j * D_BRANCH:(j + 1) * D_BRANCH])
            dh = dh + _dot_nt(db_ref[j], w_ref[:, (3 + j) * D_BRANCH:(4 + j) * D_BRANCH])
        for j in range(2):
            dh = dh + _dot_nt(dgl_ref[:, j * d:(j + 1) * d], w_ref[:, n_qkv + j * d:n_qkv + (j + 1) * d])
        dres, dg = _rms_bwd(dh, xn, r, g_ref[...])
        dg_ref[...] += dg
        dx_ref[...] = dx1_ref[...] + dres

    return pl.pallas_call(
        body, name="inproj_bwd", grid=(t_len // TM,),
        out_shape=(jax.ShapeDtypeStruct((t_len, d), F32), jax.ShapeDtypeStruct((t_len, d), BF16),
                   jax.ShapeDtypeStruct((1, d), F32)),
        in_specs=[_row3_spec(3, TM, D_BRANCH), _row3_spec(3, TM, D_BRANCH), _row_spec(TM, 2 * d), _row_spec(TM, LANES),
                  _row_spec(TM, d), _row_spec(TM, d), _const_spec((1, d)), _const_spec(w_pad.shape)],
        out_specs=(_row_spec(TM, d), _row_spec(TM, d), _acc_spec((1, d))),
        compiler_params=_seq_params(),
    )(dqkv_a, dqkv_b, dgl, df, dx1, x, g_mix, w_pad)


def _pack_rows(n_elems):
    rows = -(-n_elems // PACK_W)
    return -(-rows // PACK_ROW_ALIGN) * PACK_ROW_ALIGN


def _pack(parts, lead=()):
    flat = jnp.concatenate([a.reshape(lead + (-1,)) for a in parts], axis=-1)
    rows = _pack_rows(flat.shape[-1])
    flat = jnp.pad(flat, [(0, 0)] * len(lead) + [(0, rows * PACK_W - flat.shape[-1])])
    return flat.reshape(lead + (rows, PACK_W))


def _unpack(packed, shapes, lead=()):
    flat = packed.reshape(lead + (-1,))
    out, off = [], 0
    for shp in shapes:
        n = 1
        for s in shp:
            n *= s
        out.append(lax.slice_in_dim(flat, off, off + n, axis=len(lead)).reshape(lead + tuple(shp)))
        off += n
    return out


def _cols_to_slabs(full):
    r, c8 = full.shape
    return full.reshape(r, N_DEV, c8 // N_DEV).transpose(1, 0, 2)


def _slabs_to_cols(slabs):
    n, r, c = slabs.shape
    return slabs.transpose(1, 0, 2).reshape(r, n * c)


def _win_sizes(d):
    return (D_BRANCH, D_BRANCH, D_BRANCH, N_HEADS, D_BRANCH, D_BRANCH, D_BRANCH, d, d)


def _split_win(w, d):
    out, off = [], 0
    for s in _win_sizes(d):
        out.append(w[:, off:off + s])
        off += s
    return out


def _pad_win(w_full, d):
    qa, ka, va, fa, qb, kb, vb, ga, gb = _split_win(w_full, d)
    scale = HEAD_DIM ** -0.5
    fpad = jnp.pad(fa, ((0, 0), (0, LANES - N_HEADS)))
    return jnp.concatenate([qa * scale, ka, va, qb * scale, kb, vb, ga, gb, fpad], axis=1)


def _unpad_dwin(dqkv_a, dqkv_b, dgates, dforget, d):
    scale = HEAD_DIM ** -0.5
    return jnp.concatenate([dqkv_a[0] * scale, dqkv_a[1], dqkv_a[2], dforget[:, :N_HEADS],
                            dqkv_b[0] * scale, dqkv_b[1], dqkv_b[2], dgates], axis=1)


def kernel(x, p, g_mix, w_in, b_forget, b_gate, w_branch_fox, w_branch_sb, w_out, g_mlp, w_up, w_down, g_ple, w_ple_gate, w_ple, g_final, loss_target, m_g_mix, m_w_in, m_b_forget, m_b_gate, m_w_branch_fox, m_w_branch_sb, m_w_out, m_g_mlp, m_w_up, m_w_down, m_g_ple, m_w_ple_gate, m_w_ple, m_g_final, v_g_mix, v_w_in, v_b_forget, v_b_gate, v_w_branch_fox, v_w_branch_sb, v_w_out, v_g_mlp, v_w_up, v_w_down, v_g_ple, v_w_ple_gate, v_w_ple, v_g_final):
    batch, seq, d = x.shape
    t_len = batch * seq
    d_ple = p.shape[-1]
    d_ff = w_up.shape[-1] * N_DEV
    d_in = w_in.shape[-1] * N_DEV
    my_c = lax.axis_index("c")

    big = [w_in[0], w_branch_fox[0], w_branch_sb[0], w_out[0], w_up[0], w_down[0], w_ple_gate[0], w_ple[0]]
    big_shapes = [a.shape for a in big]
    bg_hi = b_gate[0].astype(BF16)
    bg_r = b_gate[0] - bg_hi.astype(F32)
    bg_mid = bg_r.astype(BF16)
    bg_lo = (bg_r - bg_mid.astype(F32)).astype(BF16)
    pack = _pack([a.astype(BF16) for a in big] + [bg_hi, bg_mid, bg_lo])
    gathered = _all_gather(pack)
    rows = gathered.shape[2]
    gathered = gathered.transpose(1, 0, 2, 3).reshape(N_DEV, rows, PACK_W)
    s_win, s_bf, s_bs, s_out, s_up, s_down, s_pg, s_ple, s_b0, s_b1, s_b2 = _unpack(
        gathered, big_shapes + [b_gate[0].shape] * 3, lead=(N_DEV,))
    w_pad = _pad_win(_slabs_to_cols(s_win), d)
    w_bf_full = _slabs_to_cols(s_bf)
    w_bs_full = _slabs_to_cols(s_bs)
    w_out_full = s_out.reshape(d, d)
    w_up_full = _slabs_to_cols(s_up)
    w_down_full = s_down.reshape(d_ff, d)
    w_pg_full = s_pg.reshape(d, d)
    w_ple_full = _slabs_to_cols(s_ple)
    bg_slabs = s_b0.astype(F32) + s_b1.astype(F32) + s_b2.astype(F32)
    b_gate_full = bg_slabs.transpose(1, 0, 2).reshape(2, d)
    bf_pad = jnp.pad(b_forget, ((0, 0), (0, LANES - N_HEADS)))

    x2d = x.reshape(t_len, d)
    p2d = p.reshape(t_len, d_ple)
    tgt2d = loss_target.reshape(t_len, d)
    qkv, gl, fpre, csum = _inproj_fwd(x2d, g_mix, w_pad, bf_pad, seq)
    c_heads = csum[:, :N_HEADS].reshape(batch, seq, N_HEADS).transpose(0, 2, 1)
    c_col = c_heads.reshape(batch, N_HEADS, seq, 1)
    c_row = c_heads.reshape(batch, N_HEADS, seq // TK, TK)
    o_fox, lse = _fox_fwd(qkv, c_col, c_row, batch, seq)
    o_sb, ltot = _sb_fwd(qkv, batch, seq)
    x1 = _mix_fwd(o_fox, o_sb, gl, x2d, w_bf_full, w_bs_full, w_out_full, b_gate_full)
    a_up, x2 = _mlp_fwd(x1, g_mlp, w_up_full, w_down_full)

    dx2, h3, dpre, dpe, loss_acc, dg_ple, dg_final = _head_fwd_bwd(
        x2, p2d, tgt2d, g_ple, g_final.reshape(1, d), w_pg_full, w_ple_full)
    dx1, da_up, h2, dg_mlp = _mlp_bwd(dx2, a_up, x1, g_mlp, w_up_full, w_down_full)
    merged, dbr_f, dbr_s, dgl, do_fox, do_sb, dbg = _mix_bwd(
        dx1, o_fox, o_sb, gl, w_bf_full, w_bs_full, w_out_full, b_gate_full)
    dqkv_a, dc_keys, dc_queries = _fox_bwd(qkv, o_fox, do_fox, lse, c_col, c_row, batch, seq)
    dqkv_b = _sb_bwd(qkv, do_sb, ltot, batch, seq)

    def heads_to_lanes(stat):
        tok = stat.reshape(batch, N_HEADS, seq).transpose(0, 2, 1).reshape(t_len, N_HEADS)
        return jnp.pad(tok, ((0, 0), (0, LANES - N_HEADS)))

    df, db_forget = _forget_bwd(heads_to_lanes(dc_queries), heads_to_lanes(dc_keys), fpre, batch, seq)
    grad_x, h1, dg_mix = _inproj_bwd(dqkv_a, dqkv_b, dgl, df, dx1, x2d, g_mix, w_pad)

    gw_ple = _matmul_tn("dw_ple", p2d, dpe)
    gw_pg = _matmul_tn("dw_ple_gate", h3, dpre)
    gw_down = _matmul_tn("dw_down", a_up, dx2, relu2=True)
    gw_up = _matmul_tn("dw_up", h2, da_up)
    gw_out = _matmul_tn("dw_out", merged, dx1)
    gw_bf = _matmul_tn("dw_branch_fox", o_fox, dbr_f)
    gw_bs = _matmul_tn("dw_branch_sb", o_sb, dbr_s)
    gw_in = _unpad_dwin(_matmul_tn("dw_in_fox", h1, dqkv_a), _matmul_tn("dw_in_sb", h1, dqkv_b),
                        _matmul_tn("dw_in_gates", h1, dgl), _matmul_tn("dw_in_forget", h1, df), d)

    dest = [_cols_to_slabs(gw_in), _cols_to_slabs(gw_bf), _cols_to_slabs(gw_bs), gw_out.reshape(N_DEV, d // N_DEV, d),
            _cols_to_slabs(gw_up), gw_down.reshape(N_DEV, d_ff // N_DEV, d), gw_pg.reshape(N_DEV, d // N_DEV, d),
            _cols_to_slabs(gw_ple)]
    partial = _pack([a.astype(BF16) for a in dest], lead=(N_DEV,))
    g_rows = partial.shape[1]
    partial = partial.reshape(4, 2, g_rows, PACK_W)
    keep = lax.dynamic_index_in_dim(partial, my_c, axis=1, keepdims=False)
    send = lax.dynamic_index_in_dim(partial, 1 - my_c, axis=1, keepdims=False)
    chip_sums = _pair_add(keep, _rs_core_pair(send))
    g_pack = _sum_chips(_rs_chips(chip_sums))
    g_big = _unpack(g_pack, big_shapes)

    small = jnp.concatenate([
        dg_mix, dg_mlp, dg_ple, dg_final, jnp.pad(db_forget[:, :N_HEADS], ((0, 0), (0, d - N_HEADS))), dbg,
        jnp.pad(loss_acc[:, :1], ((0, 0), (0, d - 1)))], axis=0)
    small = _all_reduce_small(small)
    g_g_mix, g_g_mlp, g_g_ple = small[0:1], small[1:2], small[2:3]
    g_g_final = small[3]
    g_b_forget = small[4:5, :N_HEADS]
    my_dev = 4 * lax.axis_index("x") + 2 * lax.axis_index("y") + my_c
    g_b_gate = lax.dynamic_slice_in_dim(small[5:7], my_dev * (d // N_DEV), d // N_DEV, axis=1)[None]
    loss = small[7, 0]

    grads = {
        "g_mix": g_g_mix, "w_in": g_big[0][None], "b_forget": g_b_forget, "b_gate": g_b_gate,
        "w_branch_fox": g_big[1][None], "w_branch_sb": g_big[2][None], "w_out": g_big[3][None], "g_mlp": g_g_mlp,
        "w_up": g_big[4][None], "w_down": g_big[5][None], "g_ple": g_g_ple, "w_ple_gate": g_big[6][None],
        "w_ple": g_big[7][None], "g_final": g_g_final,
    }
    weights = {"g_mix": g_mix, "w_in": w_in, "b_forget": b_forget, "b_gate": b_gate, "w_branch_fox": w_branch_fox,
               "w_branch_sb": w_branch_sb, "w_out": w_out, "g_mlp": g_mlp, "w_up": w_up, "w_down": w_down,
               "g_ple": g_ple, "w_ple_gate": w_ple_gate, "w_ple": w_ple, "g_final": g_final}
    m_in = {"g_mix": m_g_mix, "w_in": m_w_in, "b_forget": m_b_forget, "b_gate": m_b_gate,
            "w_branch_fox": m_w_branch_fox, "w_branch_sb": m_w_branch_sb, "w_out": m_w_out, "g_mlp": m_g_mlp,
            "w_up": m_w_up, "w_down": m_w_down, "g_ple": m_g_ple, "w_ple_gate": m_w_ple_gate, "w_ple": m_w_ple,
            "g_final": m_g_final}
    v_in = {"g_mix": v_g_mix, "w_in": v_w_in, "b_forget": v_b_forget, "b_gate": v_b_gate,
            "w_branch_fox": v_w_branch_fox, "w_branch_sb": v_w_branch_sb, "w_out": v_w_out, "g_mlp": v_g_mlp,
            "w_up": v_w_up, "w_down": v_w_down, "g_ple": v_g_ple, "w_ple_gate": v_w_ple_gate, "w_ple": v_w_ple,
            "g_final": v_g_final}
    names = list(weights)
    deltas, new_m, new_v = [], [], []
    for n in names:
        w = weights[n]
        shape2 = (-1, w.shape[-1])
        dlt, nm, nv = _adamw("adamw_" + n, w.reshape(shape2), grads[n].reshape(shape2), m_in[n].reshape(shape2),
                             v_in[n].reshape(shape2))
        deltas.append(dlt.reshape(w.shape))
        new_m.append(nm.reshape(w.shape))
        new_v.append(nv.reshape(w.shape))
    return (loss, grad_x.reshape(x.shape), *[grads[n] for n in names], *deltas, *new_m, *new_v)
```

```python
import jax
import jax.numpy as jnp
from jax import lax
from jax.experimental import pallas as pl
from jax.experimental.pallas import tpu as pltpu

F32 = jnp.float32
BF16 = jnp.bfloat16

HEAD_DIM = 64
N_HEADS = 8
D_BRANCH = N_HEADS * HEAD_DIM
EPS = 1e-6
ADAM_LR = 0.001
ADAM_B1 = 0.9
ADAM_B2 = 0.999
ADAM_EPS = 1e-08
ADAM_WD = 0.01
ADAM_STEP = 10

N_DEV = 8
LANES = 128
TM = 256
TQ = 256
TK = 256
NH = 4
NEG = -1e30
VMEM_LIMIT = 56 * 1024 * 1024
MESH = pl.DeviceIdType.MESH


def _dot(a, b):
    return jnp.dot(a, b, preferred_element_type=F32)


def _dot_nt(a, b):
    return lax.dot_general(a, b, (((1,), (1,)), ((), ())), preferred_element_type=F32)


def _dot_tn(a, b):
    return lax.dot_general(a, b, (((0,), (0,)), ((), ())), preferred_element_type=F32)


def _sigmoid(x):
    return 1.0 / (1.0 + jnp.exp(-x))


def _softplus(x):
    return jnp.maximum(x, 0.0) + jnp.log(1.0 + jnp.exp(-jnp.abs(x)))


def _split2(x):
    hi = x.astype(BF16)
    lo = (x - hi.astype(F32)).astype(BF16)
    return hi, lo


def _split3(x):
    hi = x.astype(BF16)
    r = x - hi.astype(F32)
    mid = r.astype(BF16)
    lo = (r - mid.astype(F32)).astype(BF16)
    return hi, mid, lo


def _rows_dot_mask(x, mask_bf16):
    hi, lo = _split2(x)
    return _dot(hi, mask_bf16) + _dot(lo, mask_bf16)


def _tri(n, rel):
    r = lax.broadcasted_iota(jnp.int32, (n, n), 0)
    c = lax.broadcasted_iota(jnp.int32, (n, n), 1)
    return rel(r, c).astype(BF16)


def _rms(x):
    r = lax.rsqrt(jnp.mean(x * x, axis=-1, keepdims=True) + EPS)
    return x * r, r


def _rms_bwd(dh, xn, r, g):
    dxn = dh * g
    dx = r * (dxn - xn * jnp.mean(dxn * xn, axis=-1, keepdims=True))
    return dx, jnp.sum(dh * xn, axis=0, keepdims=True)


def _row_spec(tm, cols):
    return pl.BlockSpec((tm, cols), lambda i: (i, 0))


def _row3_spec(g, tm, cols):
    return pl.BlockSpec((g, tm, cols), lambda i: (0, i, 0))


def _const_spec(shape):
    nd = len(shape)
    return pl.BlockSpec(shape, lambda i: (0,) * nd, pipeline_mode=pl.Buffered(1))


def _acc_spec(shape):
    nd = len(shape)
    return pl.BlockSpec(shape, lambda i: (0,) * nd)


def _seq_params():
    return pltpu.CompilerParams(dimension_semantics=("arbitrary",), vmem_limit_bytes=VMEM_LIMIT)


def _mesh_pos():
    return lax.axis_index("x"), lax.axis_index("y"), lax.axis_index("c")


def _other_chips(x, y):
    return [(1 - x, y), (x, 1 - y), (1 - x, 1 - y)]


def _hbm_specs(n):
    return [pl.BlockSpec(memory_space=pl.ANY)] * n


def _all_gather(shards):
    n = len(shards)

    def body(*refs):
        x_refs, out_refs = refs[:n], refs[n:2 * n]
        send_sems, recv_sems, local_sems = refs[2 * n:]
        x, y, c = _mesh_pos()
        me, sibling = (x, y, c), (x, y, 1 - c)
        chips = _other_chips(x, y)

        def index(px, py, pc):
            return 4 * px + 2 * py + pc

        def copy(a, k, block, to, src=None):
            slab = out_refs[a].at[index(*block)]
            return pltpu.make_async_remote_copy(
                src_ref=slab if src is None else src, dst_ref=slab,
                send_sem=send_sems.at[7 * a + k], recv_sem=recv_sems.at[7 * a + k], device_id=to, device_id_type=MESH)

        mine, first, passed = [], [], []
        for a in range(n):
            mine.append(pltpu.make_async_copy(x_refs[a], out_refs[a].at[index(*me)], local_sems.at[a]))
            mine[-1].start()
            first.append(copy(a, 0, me, sibling, src=x_refs[a]))
            first += [copy(a, 1 + j, me, (cx, cy, c), src=x_refs[a]) for j, (cx, cy) in enumerate(chips)]
        for cp in first:
            cp.start()
        for j, (cx, cy) in enumerate(chips):
            for a in range(n):
                copy(a, 1 + j, (cx, cy, c), me).wait_recv()
                passed.append(copy(a, 4 + j, (cx, cy, c), sibling))
                passed[-1].start()
        for a in range(n):
            copy(a, 0, sibling, me).wait_recv()
            for j, (cx, cy) in enumerate(chips):
                copy(a, 4 + j, (cx, cy, 1 - c), me).wait_recv()
        for cp in first + passed:
            cp.wait_send()
        for cp in mine:
            cp.wait()

    return pl.pallas_call(
        body, name="all_gather_weights",
        out_shape=[jax.ShapeDtypeStruct((N_DEV,) + s.shape, s.dtype) for s in shards],
        in_specs=_hbm_specs(n), out_specs=_hbm_specs(n),
        scratch_shapes=[pltpu.SemaphoreType.DMA((7 * n,)), pltpu.SemaphoreType.DMA((7 * n,)),
                        pltpu.SemaphoreType.DMA((n,))],
    )(*shards)


def _rs_core_pair(partials):
    n = len(partials)

    def body(*refs):
        p_refs, recv_refs = refs[:n], refs[n:2 * n]
        send_sems, recv_sems = refs[2 * n:]
        x, y, c = _mesh_pos()
        for a in range(n):
            for chip in range(4):
                pltpu.make_async_remote_copy(
                    src_ref=p_refs[a].at[2 * chip + (1 - c)], dst_ref=recv_refs[a].at[chip],
                    send_sem=send_sems.at[a], recv_sem=recv_sems.at[a],
                    device_id=(x, y, 1 - c), device_id_type=MESH).start()
        for a in range(n):
            pltpu.make_async_remote_copy(
                src_ref=recv_refs[a], dst_ref=recv_refs[a], send_sem=send_sems.at[a], recv_sem=recv_sems.at[a],
                device_id=(x, y, 1 - c), device_id_type=MESH).wait()

    return pl.pallas_call(
        body, name="reduce_scatter_core_pair",
        out_shape=[jax.ShapeDtypeStruct((4,) + s.shape[1:], s.dtype) for s in partials],
        in_specs=_hbm_specs(n), out_specs=_hbm_specs(n),
        scratch_shapes=[pltpu.SemaphoreType.DMA((n,)), pltpu.SemaphoreType.DMA((n,))],
    )(*partials)


def _rs_chips(chip_sums):
    n = len(chip_sums)

    def body(*refs):
        cs_refs, out_refs = refs[:n], refs[n:2 * n]
        send_sems, recv_sems, local_sems = refs[2 * n:]
        x, y, c = _mesh_pos()
        chip = 2 * x + y
        chips = _other_chips(x, y)
        mine, sends = [], []
        for a in range(n):
            mine.append(pltpu.make_async_copy(cs_refs[a].at[chip], out_refs[a].at[chip], local_sems.at[a]))
            mine[-1].start()
            for j, (cx, cy) in enumerate(chips):
                sends.append(pltpu.make_async_remote_copy(
                    src_ref=cs_refs[a].at[2 * cx + cy], dst_ref=out_refs[a].at[chip],
                    send_sem=send_sems.at[3 * a + j], recv_sem=recv_sems.at[3 * a + j],
                    device_id=(cx, cy, c), device_id_type=MESH))
                sends[-1].start()
        for a in range(n):
            for j, (cx, cy) in enumerate(chips):
                pltpu.make_async_remote_copy(
                    src_ref=cs_refs[a].at[chip], dst_ref=out_refs[a].at[2 * cx + cy],
                    send_sem=send_sems.at[3 * a + j], recv_sem=recv_sems.at[3 * a + j],
                    device_id=(x, y, c), device_id_type=MESH).wait_recv()
        for cp in sends:
            cp.wait_send()
        for cp in mine:
            cp.wait()

    return pl.pallas_call(
        body, name="reduce_scatter_chips",
        out_shape=[jax.ShapeDtypeStruct(s.shape, s.dtype) for s in chip_sums],
        in_specs=_hbm_specs(n), out_specs=_hbm_specs(n),
        scratch_shapes=[pltpu.SemaphoreType.DMA((3 * n,)), pltpu.SemaphoreType.DMA((3 * n,)),
                        pltpu.SemaphoreType.DMA((n,))],
    )(*chip_sums)


def _all_reduce_small(vec):
    rows, cols = vec.shape

    def body(x_ref, land_ref, sum_ref, send_sems, recv_sems):
        x, y, c = _mesh_pos()
        me = 4 * x + 2 * y + c
        land_ref[me] = x_ref[...]
        flips = [(fx, fy, fc) for fx in (0, 1) for fy in (0, 1) for fc in (0, 1)][1:]

        def flipped(f):
            return tuple((1 - v) if b else v for v, b in zip((x, y, c), f))

        sends = []
        for k, f in enumerate(flips):
            sends.append(pltpu.make_async_remote_copy(
                src_ref=x_ref, dst_ref=land_ref.at[me], send_sem=send_sems.at[k], recv_sem=recv_sems.at[k],
                device_id=flipped(f), device_id_type=MESH))
            sends[-1].start()
        for k, f in enumerate(flips):
            px, py, pc = flipped(f)
            pltpu.make_async_remote_copy(
                src_ref=x_ref, dst_ref=land_ref.at[4 * px + 2 * py + pc], send_sem=send_sems.at[k],
                recv_sem=recv_sems.at[k], device_id=(x, y, c), device_id_type=MESH).wait_recv()
        for cp in sends:
            cp.wait_send()
        total = land_ref[0]
        for d in range(1, N_DEV):
            total = total + land_ref[d]
        sum_ref[...] = total

    vm = pl.BlockSpec(memory_space=pltpu.VMEM)
    return pl.pallas_call(
        body, name="all_reduce_small",
        out_shape=(jax.ShapeDtypeStruct((N_DEV, rows, cols), F32), jax.ShapeDtypeStruct((rows, cols), F32)),
        in_specs=[vm], out_specs=(vm, vm),
        scratch_shapes=[pltpu.SemaphoreType.DMA((7,)), pltpu.SemaphoreType.DMA((7,))],
    )(vec)[1]


def _block_rows(rows, cols, itemsize, align, row_off=0):
    best = None
    for t in range(align, rows + 1, align):
        if rows % t == 0 and row_off % t == 0 and t * cols * itemsize <= (1 << 20):
            best = t
    return rows if best is None else best


def _pair_add(name, partial, recv, my_c):
    _, rows, cols = partial.shape
    br = _block_rows(rows, cols, 2, 16)

    def body(c_ref, a_ref, b_ref, o_ref):
        o_ref[...] = (a_ref[...].astype(F32) + b_ref[...].astype(F32)).astype(BF16)

    return pl.pallas_call(
        body, name=name,
        grid_spec=pltpu.PrefetchScalarGridSpec(
            num_scalar_prefetch=1, grid=(4, rows // br),
            in_specs=[pl.BlockSpec((None, None, br, cols), lambda j, i, c_ref: (j, c_ref[0], i, 0)),
                      pl.BlockSpec((None, br, cols), lambda j, i, c_ref: (j, i, 0))],
            out_specs=pl.BlockSpec((None, br, cols), lambda j, i, c_ref: (j, i, 0))),
        out_shape=jax.ShapeDtypeStruct((4, rows, cols), BF16),
    )(my_c.reshape(1).astype(jnp.int32), partial.reshape(4, 2, rows, cols), recv)


def _adam_update(w, g, m, v):
    nm = ADAM_B1 * m + (1.0 - ADAM_B1) * g
    nv = ADAM_B2 * v + (1.0 - ADAM_B2) * (g * g)
    m_hat = nm / (1.0 - ADAM_B1 ** ADAM_STEP)
    v_hat = nv / (1.0 - ADAM_B2 ** ADAM_STEP)
    return -ADAM_LR * (m_hat / (jnp.sqrt(v_hat) + ADAM_EPS) + ADAM_WD * w), nm, nv


def _adamw_parts(name, w, parts, row_off, m, v):
    rows, cols = w.shape
    tr = _block_rows(rows, cols, 4, 16, row_off)
    assert rows % tr == 0 and row_off % tr == 0
    off = row_off // tr

    def body(w_ref, p_ref, m_ref, v_ref, g_ref, d_ref, nm_ref, nv_ref):
        g = p_ref[0].astype(F32)
        for j in range(1, 4):
            g = g + p_ref[j].astype(F32)
        g_ref[...] = g
        d_ref[...], nm_ref[...], nv_ref[...] = _adam_update(w_ref[...], g, m_ref[...], v_ref[...])

    spec = pl.BlockSpec((tr, cols), lambda i: (i, 0))
    shp = jax.ShapeDtypeStruct((rows, cols), F32)
    return pl.pallas_call(
        body, name=name, grid=(rows // tr,), out_shape=(shp,) * 4,
        in_specs=[spec, pl.BlockSpec((4, tr, cols), lambda i: (0, off + i, 0)), spec, spec], out_specs=(spec,) * 4,
    )(w, parts, m, v)


def _adamw_small(tensors):
    n = len(tensors)

    def body(*refs):
        ins, outs = refs[:4 * n], refs[4 * n:]
        for t in range(n):
            w_ref, g_ref, m_ref, v_ref = ins[4 * t:4 * t + 4]
            d, nm, nv = _adam_update(w_ref[...], g_ref[...], m_ref[...], v_ref[...])
            outs[3 * t][...], outs[3 * t + 1][...], outs[3 * t + 2][...] = d, nm, nv

    vm = pl.BlockSpec(memory_space=pltpu.VMEM)
    out = pl.pallas_call(
        body, name="adamw_small",
        out_shape=[jax.ShapeDtypeStruct(t[0].shape, F32) for t in tensors for _ in range(3)],
        in_specs=[vm] * (4 * n), out_specs=[vm] * (3 * n),
    )(*[a for t in tensors for a in t])
    return [tuple(out[3 * t:3 * t + 3]) for t in range(n)]


def _matmul_tn(name, a, b, relu2=False):
    squeeze = b.ndim == 2
    if squeeze:
        b = b[None]
    t_len, k_len = a.shape
    groups, _, n_len = b.shape
    tt = min(t_len, 512)
    tk = min(k_len, 512)
    tn = min(n_len, 1024)
    nt = t_len // tt

    def body(a_ref, b_ref, o_ref):
        @pl.when(pl.program_id(3) == 0)
        def _():
            o_ref[...] = jnp.zeros_like(o_ref)

        av = a_ref[...]
        if relu2:
            av = jnp.square(jnp.maximum(av.astype(F32), 0.0))
        o_ref[...] += _dot_tn(av.astype(BF16), b_ref[...].astype(BF16))

    out = pl.pallas_call(
        body, name=name, grid=(groups, k_len // tk, n_len // tn, nt),
        out_shape=jax.ShapeDtypeStruct((groups, k_len, n_len), F32),
        in_specs=[pl.BlockSpec((tt, tk), lambda g, i, j, t: (t, i)),
                  pl.BlockSpec((None, tt, tn), lambda g, i, j, t: (g, t, j))],
        out_specs=pl.BlockSpec((None, tk, tn), lambda g, i, j, t: (g, i, j)),
        compiler_params=pltpu.CompilerParams(
            dimension_semantics=("parallel", "parallel", "parallel", "arbitrary"), vmem_limit_bytes=VMEM_LIMIT),
    )(a, b)
    return out[0] if squeeze else out


def _inproj_fwd(x, g_mix, w_pad, bf_pad, seq):
    t_len, d = x.shape
    n_qkv = 6 * D_BRANCH
    tiles_per_seq = seq // TM

    def body(x_ref, g_ref, w_ref, bf_ref, qkv_ref, gl_ref, fpre_ref, c_ref, carry_ref):
        @pl.when(pl.program_id(0) % tiles_per_seq == 0)
        def _():
            carry_ref[...] = jnp.zeros_like(carry_ref)

        xn, _ = _rms(x_ref[...])
        h = (xn * g_ref[...]).astype(BF16)
        for j in range(6):
            qkv_ref[j] = _dot(h, w_ref[:, j * D_BRANCH:(j + 1) * D_BRANCH]).astype(BF16)
        for j in range(2):
            gl_ref[:, j * d:(j + 1) * d] = _dot(h, w_ref[:, n_qkv + j * d:n_qkv + (j + 1) * d]).astype(BF16)
        fpre = _dot(h, w_ref[:, n_qkv + 2 * d:]) + bf_ref[...]
        fpre_ref[...] = fpre
        logf = -_softplus(-fpre)
        lower = _tri(TM, lambda r, c: c <= r)
        hi, mid, lo = _split3(logf)
        c_val = carry_ref[...] + _dot(lower, hi) + _dot(lower, mid) + _dot(lower, lo)
        c_ref[...] = c_val
        carry_ref[...] = carry_ref[...] + jnp.sum(logf, axis=0, keepdims=True)

    return pl.pallas_call(
        body, name="inproj_fwd", grid=(t_len // TM,),
        out_shape=(jax.ShapeDtypeStruct((6, t_len, D_BRANCH), BF16), jax.ShapeDtypeStruct((t_len, 2 * d), BF16),
                   jax.ShapeDtypeStruct((t_len, LANES), F32), jax.ShapeDtypeStruct((t_len, LANES), F32)),
        in_specs=[_row_spec(TM, d), _const_spec((1, d)), _const_spec(w_pad.shape), _const_spec((1, LANES))],
        out_specs=(_row3_spec(6, TM, D_BRANCH), _row_spec(TM, 2 * d), _row_spec(TM, LANES), _row_spec(TM, LANES)),
        scratch_shapes=[pltpu.VMEM((1, LANES), F32)],
        compiler_params=_seq_params(),
    )(x, g_mix, w_pad, bf_pad)


def _head_spec(which, seq):
    return pl.BlockSpec((None, seq, NH * HEAD_DIM), lambda b, g: (which, b, g))


def _group_spec(seq):
    return pl.BlockSpec((seq, NH * HEAD_DIM), lambda b, g: (b, g))


def _stat_spec(seq):
    return pl.BlockSpec((None, None, seq, LANES), lambda b, g: (b, g, 0, 0))


def _keyrow_spec(seq):
    return pl.BlockSpec((None, NH, seq // TK, TK), lambda b, g: (b, g, 0, 0))


def _attn_params():
    return pltpu.CompilerParams(dimension_semantics=("parallel", "parallel"), vmem_limit_bytes=VMEM_LIMIT)


def _hcols(hh):
    return slice(hh * HEAD_DIM, (hh + 1) * HEAD_DIM)


def _lane(hh):
    return slice(hh, hh + 1)


def _tile_mask(rel):
    r = lax.broadcasted_iota(jnp.int32, (TQ, TK), 0)
    c = lax.broadcasted_iota(jnp.int32, (TQ, TK), 1)
    return rel(r, c)


def _heads_cat(vals):
    return jnp.concatenate(vals, axis=1)


def _fox_fwd(qkv, c_stat, c_row, batch, seq):
    def body(q_ref, k_ref, v_ref, cc_ref, cr_ref, o_ref, lse_ref, m_s, l_s, acc_s):
        causal = _tile_mask(lambda r, c: c <= r)

        def tile(q0, k0, kj, masked):
            for hh in range(NH):
                cols = _hcols(hh)
                s = (_dot_nt(q_ref[pl.ds(q0, TQ), cols], k_ref[pl.ds(k0, TK), cols])
                     + cc_ref[pl.ds(q0, TQ), _lane(hh)] - cr_ref[hh, pl.ds(kj, 1), :])
                if masked:
                    s = jnp.where(causal, s, NEG)
                m_old = m_s[hh]
                m_new = jnp.maximum(m_old, jnp.max(s, axis=1, keepdims=True))
                alpha = jnp.exp(m_old - m_new)
                p = jnp.exp(s - m_new)
                l_s[hh] = alpha * l_s[hh] + jnp.sum(p, axis=1, keepdims=True)
                acc_s[hh] = alpha * acc_s[hh] + _dot(p.astype(BF16), v_ref[pl.ds(k0, TK), cols])
                m_s[hh] = m_new

        def q_loop(qi, _):
            q0 = pl.multiple_of(qi * TQ, TQ)
            m_s[...] = jnp.full(m_s.shape, NEG, F32)
            l_s[...] = jnp.zeros_like(l_s)
            acc_s[...] = jnp.zeros_like(acc_s)

            def k_loop(kj, _):
                tile(q0, pl.multiple_of(kj * TK, TK), kj, False)
                return 0

            lax.fori_loop(0, qi, k_loop, 0)
            tile(q0, q0, qi, True)
            o_ref[pl.ds(q0, TQ), :] = _heads_cat([acc_s[hh] / l_s[hh] for hh in range(NH)]).astype(BF16)
            for hh in range(NH):
                lse_ref[pl.ds(q0, TQ), _lane(hh)] = m_s[hh] + jnp.log(l_s[hh])
            return 0

        lax.fori_loop(0, seq // TQ, q_loop, 0)

    return pl.pallas_call(
        body, name="fox_fwd", grid=(batch, N_HEADS // NH),
        out_shape=(jax.ShapeDtypeStruct((batch * seq, D_BRANCH), BF16),
                   jax.ShapeDtypeStruct((batch, N_HEADS // NH, seq, LANES), F32)),
        in_specs=[_head_spec(0, seq), _head_spec(1, seq), _head_spec(2, seq), _stat_spec(seq), _keyrow_spec(seq)],
        out_specs=(_group_spec(seq), _stat_spec(seq)),
        scratch_shapes=[pltpu.VMEM((NH, TQ, 1), F32), pltpu.VMEM((NH, TQ, 1), F32), pltpu.VMEM((NH, TQ, HEAD_DIM), F32)],
        compiler_params=_attn_params(),
    )(qkv, qkv, qkv, c_stat, c_row)


def _fox_bwd(qkv, o, do, lse, c_stat, c_row, batch, seq):
    n_q = seq // TQ

    def body(q_ref, k_ref, v_ref, o_ref, do_ref, lse_ref, cc_ref, cr_ref, dqkv_ref, dck_ref, dcq_ref,
             bias_s, delta_s, dq_acc, dk_s, dv_s):
        causal = _tile_mask(lambda r, c: c <= r)
        dq_acc[...] = jnp.zeros_like(dq_acc)
        dcq_ref[...] = jnp.zeros_like(dcq_ref)

        def prep(qi, _):
            rows = pl.ds(pl.multiple_of(qi * TQ, TQ), TQ)
            bias_s[rows, :] = cc_ref[rows, :] - lse_ref[rows, :]
            for hh in range(NH):
                cols = _hcols(hh)
                delta_s[rows, _lane(hh)] = jnp.sum(
                    do_ref[rows, cols].astype(F32) * o_ref[rows, cols].astype(F32), axis=1, keepdims=True)
            return 0

        lax.fori_loop(0, n_q, prep, 0)

        def tile(q0, k0, kj, masked):
            rows, krows = pl.ds(q0, TQ), pl.ds(k0, TK)
            for hh in range(NH):
                cols = _hcols(hh)
                q, dout = q_ref[rows, cols], do_ref[rows, cols]
                k, v = k_ref[krows, cols], v_ref[krows, cols]
                p = jnp.exp(_dot_nt(q, k) + bias_s[rows, _lane(hh)] - cr_ref[hh, pl.ds(kj, 1), :])
                if masked:
                    p = jnp.where(causal, p, 0.0)
                dv_s[hh] += _dot_tn(p.astype(BF16), dout)
                ds = p * (_dot_nt(dout, v) - delta_s[rows, _lane(hh)])
                dsb = ds.astype(BF16)
                dk_s[hh] += _dot_tn(dsb, q)
                dq_acc[hh, rows, :] += _dot(dsb, k)
                dck_ref[hh, pl.ds(kj, 1), :] += jnp.sum(ds, axis=0, keepdims=True)
                dcq_ref[rows, _lane(hh)] += jnp.sum(ds, axis=1, keepdims=True)

        def k_loop(kj, _):
            k0 = pl.multiple_of(kj * TK, TK)
            dk_s[...] = jnp.zeros_like(dk_s)
            dv_s[...] = jnp.zeros_like(dv_s)
            dck_ref[:, pl.ds(kj, 1), :] = jnp.zeros((NH, 1, TK), F32)
            tile(k0, k0, kj, True)

            def q_loop(qi, _):
                tile(pl.multiple_of(qi * TQ, TQ), k0, kj, False)
                return 0

            lax.fori_loop(kj + 1, n_q, q_loop, 0)
            dqkv_ref[1, pl.ds(k0, TK), :] = _heads_cat([dk_s[hh] for hh in range(NH)]).astype(BF16)
            dqkv_ref[2, pl.ds(k0, TK), :] = _heads_cat([dv_s[hh] for hh in range(NH)]).astype(BF16)
            return 0

        lax.fori_loop(0, seq // TK, k_loop, 0)
        dqkv_ref[0] = _heads_cat([dq_acc[hh] for hh in range(NH)]).astype(BF16)

    return pl.pallas_call(
        body, name="fox_bwd", grid=(batch, N_HEADS // NH),
        out_shape=(jax.ShapeDtypeStruct((3, batch * seq, D_BRANCH), BF16),
                   jax.ShapeDtypeStruct((batch, N_HEADS, seq // TK, TK), F32),
                   jax.ShapeDtypeStruct((batch, N_HEADS // NH, seq, LANES), F32)),
        in_specs=[_head_spec(0, seq), _head_spec(1, seq), _head_spec(2, seq), _group_spec(seq), _group_spec(seq),
                  _stat_spec(seq), _stat_spec(seq), _keyrow_spec(seq)],
        out_specs=(pl.BlockSpec((3, seq, NH * HEAD_DIM), lambda b, g: (0, b, g)), _keyrow_spec(seq), _stat_spec(seq)),
        scratch_shapes=[pltpu.VMEM((seq, LANES), F32), pltpu.VMEM((seq, LANES), F32),
                        pltpu.VMEM((NH, seq, HEAD_DIM), F32), pltpu.VMEM((NH, TK, HEAD_DIM), F32),
                        pltpu.VMEM((NH, TK, HEAD_DIM), F32)],
        compiler_params=_attn_params(),
    )(qkv, qkv, qkv, o, do, lse, c_stat, c_row)


def _sb_fwd(qkv, batch, seq):
    def body(q_ref, k_ref, v_ref, o_ref, lt_ref, run_s, acc_s):
        strict = _tile_mask(lambda r, c: c < r)
        later = _tri(TK, lambda r, c: r > c)

        def tile(q0, k0, masked):
            for hh in range(NH):
                cols = _hcols(hh)
                z = _dot_nt(q_ref[pl.ds(q0, TQ), cols], k_ref[pl.ds(k0, TK), cols])
                lg = -_softplus(z)
                if masked:
                    lg = jnp.where(strict, lg, 0.0)
                a = jnp.exp(z + lg + run_s[hh] + _rows_dot_mask(lg, later))
                if masked:
                    a = jnp.where(strict, a, 0.0)
                acc_s[hh] += _dot(a.astype(BF16), v_ref[pl.ds(k0, TK), cols])
                run_s[hh] += jnp.sum(lg, axis=1, keepdims=True)

        def q_loop(qi, _):
            q0 = pl.multiple_of(qi * TQ, TQ)
            run_s[...] = jnp.zeros_like(run_s)
            acc_s[...] = jnp.zeros_like(acc_s)
            tile(q0, q0, True)

            def k_loop(kk, _):
                tile(q0, pl.multiple_of((qi - 1 - kk) * TK, TK), False)
                return 0

            lax.fori_loop(0, qi, k_loop, 0)
            o_ref[pl.ds(q0, TQ), :] = _heads_cat([acc_s[hh] for hh in range(NH)]).astype(BF16)
            for hh in range(NH):
                lt_ref[pl.ds(q0, TQ), _lane(hh)] = run_s[hh]
            return 0

        lax.fori_loop(0, seq // TQ, q_loop, 0)

    return pl.pallas_call(
        body, name="sb_fwd", grid=(batch, N_HEADS // NH),
        out_shape=(jax.ShapeDtypeStruct((batch * seq, D_BRANCH), BF16),
                   jax.ShapeDtypeStruct((batch, N_HEADS // NH, seq, LANES), F32)),
        in_specs=[_head_spec(3, seq), _head_spec(4, seq), _head_spec(5, seq)],
        out_specs=(_group_spec(seq), _stat_spec(seq)),
        scratch_shapes=[pltpu.VMEM((NH, TQ, 1), F32), pltpu.VMEM((NH, TQ, HEAD_DIM), F32)],
        compiler_params=_attn_params(),
    )(qkv, qkv, qkv)


def _sb_bwd(qkv, do, ltot, batch, seq):
    def body(q_ref, k_ref, v_ref, do_ref, lt_ref, dqkv_ref, dk_acc, dv_acc, ls_s, gs_s, dq_s):
        strict = _tile_mask(lambda r, c: c < r)
        upto = _tri(TK, lambda r, c: r <= c)
        before = _tri(TK, lambda r, c: r < c)
        dk_acc[...] = jnp.zeros_like(dk_acc)
        dv_acc[...] = jnp.zeros_like(dv_acc)

        def tile(q0, k0, masked):
            rows, krows = pl.ds(q0, TQ), pl.ds(k0, TK)
            for hh in range(NH):
                cols = _hcols(hh)
                q, dout = q_ref[rows, cols], do_ref[rows, cols]
                k = k_ref[krows, cols]
                z = _dot_nt(q, k)
                lg = -_softplus(z)
                if masked:
                    lg = jnp.where(strict, lg, 0.0)
                a = jnp.exp(z + lg + (lt_ref[rows, _lane(hh)] - ls_s[hh]) - _rows_dot_mask(lg, upto))
                if masked:
                    a = jnp.where(strict, a, 0.0)
                g = _dot_nt(dout, v_ref[krows, cols]) * a
                one_minus_beta = jnp.exp(lg)
                dz = g * one_minus_beta - (1.0 - one_minus_beta) * (gs_s[hh] + _rows_dot_mask(g, before))
                dzb = dz.astype(BF16)
                dk_acc[hh, krows, :] += _dot_tn(dzb, q)
                dv_acc[hh, krows, :] += _dot_tn(a.astype(BF16), dout)
                dq_s[hh] += _dot(dzb, k)
                ls_s[hh] += jnp.sum(lg, axis=1, keepdims=True)
                gs_s[hh] += jnp.sum(g, axis=1, keepdims=True)

        def q_loop(qi, _):
            q0 = pl.multiple_of(qi * TQ, TQ)
            ls_s[...] = jnp.zeros_like(ls_s)
            gs_s[...] = jnp.zeros_like(gs_s)
            dq_s[...] = jnp.zeros_like(dq_s)

            def k_loop(kj, _):
                tile(q0, pl.multiple_of(kj * TK, TK), False)
                return 0

            lax.fori_loop(0, qi, k_loop, 0)
            tile(q0, q0, True)
            dqkv_ref[0, pl.ds(q0, TQ), :] = _heads_cat([dq_s[hh] for hh in range(NH)]).astype(BF16)
            return 0

        lax.fori_loop(0, seq // TQ, q_loop, 0)
        dqkv_ref[1] = _heads_cat([dk_acc[hh] for hh in range(NH)]).astype(BF16)
        dqkv_ref[2] = _heads_cat([dv_acc[hh] for hh in range(NH)]).astype(BF16)

    return pl.pallas_call(
        body, name="sb_bwd", grid=(batch, N_HEADS // NH),
        out_shape=jax.ShapeDtypeStruct((3, batch * seq, D_BRANCH), BF16),
        in_specs=[_head_spec(3, seq), _head_spec(4, seq), _head_spec(5, seq), _group_spec(seq), _stat_spec(seq)],
        out_specs=pl.BlockSpec((3, seq, NH * HEAD_DIM), lambda b, g: (0, b, g)),
        scratch_shapes=[pltpu.VMEM((NH, seq, HEAD_DIM), F32), pltpu.VMEM((NH, seq, HEAD_DIM), F32),
                        pltpu.VMEM((NH, TQ, 1), F32), pltpu.VMEM((NH, TQ, 1), F32), pltpu.VMEM((NH, TQ, HEAD_DIM), F32)],
        compiler_params=_attn_params(),
    )(qkv, qkv, qkv, do, ltot)


def _forget_bwd(dcq_tok, dck_tok, fpre, batch, seq):
    t_len = batch * seq
    tiles = seq // TM

    def rev(i):
        return ((i // tiles) * tiles + (tiles - 1 - i % tiles), 0)

    def body(dcq_ref, dck_ref, f_ref, df_ref, db_ref, carry_ref):
        i = pl.program_id(0)

        @pl.when(i == 0)
        def _():
            db_ref[...] = jnp.zeros_like(db_ref)

        @pl.when(i % tiles == 0)
        def _():
            carry_ref[...] = jnp.zeros_like(carry_ref)

        dc = dcq_ref[...] - dck_ref[...]
        upper = _tri(TM, lambda r, c: c >= r)
        hi, mid, lo = _split3(dc)
        dlogf = carry_ref[...] + _dot(upper, hi) + _dot(upper, mid) + _dot(upper, lo)
        carry_ref[...] = carry_ref[...] + jnp.sum(dc, axis=0, keepdims=True)
        df = dlogf * _sigmoid(-f_ref[...])
        df_ref[...] = df.astype(BF16)
        db_ref[...] += jnp.sum(df, axis=0, keepdims=True)

    return pl.pallas_call(
        body, name="forget_bwd", grid=(t_len // TM,),
        out_shape=(jax.ShapeDtypeStruct((t_len, LANES), BF16), jax.ShapeDtypeStruct((1, LANES), F32)),
        in_specs=[pl.BlockSpec((TM, LANES), rev)] * 3,
        out_specs=(pl.BlockSpec((TM, LANES), rev), _acc_spec((1, LANES))),
        scratch_shapes=[pltpu.VMEM((1, LANES), F32)],
        compiler_params=_seq_params(),
    )(dcq_tok, dck_tok, fpre)


def _mix_fwd(o_fox, o_sb, gl, x, w_bf, w_bs, w_out, b_gate):
    t_len, d = x.shape

    def body(of_ref, os_ref, gl_ref, x_ref, wbf_ref, wbs_ref, wo_ref, bg_ref, x1_ref):
        br_f = _dot(of_ref[...], wbf_ref[...])
        br_s = _dot(os_ref[...], wbs_ref[...])
        ga = _sigmoid(gl_ref[:, :d].astype(F32) + bg_ref[0:1, :])
        gb = _sigmoid(gl_ref[:, d:].astype(F32) + bg_ref[1:2, :])
        merged = ga * br_f + gb * br_s
        x1_ref[...] = x_ref[...] + _dot(merged.astype(BF16), wo_ref[...])

    return pl.pallas_call(
        body, name="mix_fwd", grid=(t_len // TM,),
        out_shape=jax.ShapeDtypeStruct((t_len, d), F32),
        in_specs=[_row_spec(TM, D_BRANCH), _row_spec(TM, D_BRANCH), _row_spec(TM, 2 * d), _row_spec(TM, d),
                  _const_spec(w_bf.shape), _const_spec(w_bs.shape), _const_spec(w_out.shape), _const_spec(b_gate.shape)],
        out_specs=_row_spec(TM, d),
        compiler_params=_seq_params(),
    )(o_fox, o_sb, gl, x, w_bf, w_bs, w_out, b_gate)


def _ff_chunk(d_ff):
    return min(d_ff, 1024)


def _mlp_fwd(x1, g_mlp, w_up, w_down):
    t_len, d = x1.shape
    d_ff = w_up.shape[1]
    ch = _ff_chunk(d_ff)

    def body(x1_ref, g_ref, wu_ref, wd_ref, a_ref, x2_ref):
        x1v = x1_ref[...]
        xn, _ = _rms(x1v)
        h = (xn * g_ref[...]).astype(BF16)
        acc = x1v
        for j in range(d_ff // ch):
            a = _dot(h, wu_ref[:, j * ch:(j + 1) * ch])
            a_ref[:, j * ch:(j + 1) * ch] = a.astype(BF16)
            acc = acc + _dot(jnp.square(jnp.maximum(a, 0.0)).astype(BF16), wd_ref[j * ch:(j + 1) * ch, :])
        x2_ref[...] = acc

    return pl.pallas_call(
        body, name="mlp_fwd", grid=(t_len // TM,),
        out_shape=(jax.ShapeDtypeStruct((t_len, d_ff), BF16), jax.ShapeDtypeStruct((t_len, d), F32)),
        in_specs=[_row_spec(TM, d), _const_spec((1, d)), _const_spec(w_up.shape), _const_spec(w_down.shape)],
        out_specs=(_row_spec(TM, d_ff), _row_spec(TM, d)),
        compiler_params=_seq_params(),
    )(x1, g_mlp, w_up, w_down)


def _head_fwd_bwd(x2, p, target, g_ple, g_final, w_pg, w_ple):
    t_len, d = x2.shape
    d_ple = p.shape[1]

    def body(x2_ref, p_ref, t_ref, gp_ref, gf_ref, wpg_ref, wple_ref,
             dx2_ref, h3_ref, dpre_ref, dpe_ref, loss_ref, dgp_ref, dgf_ref):
        @pl.when(pl.program_id(0) == 0)
        def _():
            loss_ref[...] = jnp.zeros_like(loss_ref)
            dgp_ref[...] = jnp.zeros_like(dgp_ref)
            dgf_ref[...] = jnp.zeros_like(dgf_ref)

        x2v = x2_ref[...]
        x2n, r3 = _rms(x2v)
        h3 = (x2n * gp_ref[...]).astype(BF16)
        h3_ref[...] = h3
        gate = _sigmoid(_dot(h3, wpg_ref[...]))
        pe = _dot(p_ref[...].astype(BF16), wple_ref[...])
        x3n, r4 = _rms(x2v + gate * pe)
        err = x3n * gf_ref[...] - t_ref[...]
        loss_ref[...] += jnp.full(loss_ref.shape, (0.5 / d) * jnp.sum(err * err), F32)
        dx3, dgf = _rms_bwd(err * (1.0 / d), x3n, r4, gf_ref[...])
        dgf_ref[...] += dgf
        dpe_ref[...] = (dx3 * gate).astype(BF16)
        dpre = (dx3 * pe * gate * (1.0 - gate)).astype(BF16)
        dpre_ref[...] = dpre
        dres, dgp = _rms_bwd(_dot_nt(dpre, wpg_ref[...]), x2n, r3, gp_ref[...])
        dgp_ref[...] += dgp
        dx2_ref[...] = dx3 + dres

    shp_b = jax.ShapeDtypeStruct((t_len, d), BF16)
    return pl.pallas_call(
        body, name="head_fwd_bwd", grid=(t_len // TM,),
        out_shape=(jax.ShapeDtypeStruct((t_len, d), F32), shp_b, shp_b, shp_b,
                   jax.ShapeDtypeStruct((1, LANES), F32), jax.ShapeDtypeStruct((1, d), F32),
                   jax.ShapeDtypeStruct((1, d), F32)),
        in_specs=[_row_spec(TM, d), _row_spec(TM, d_ple), _row_spec(TM, d), _const_spec((1, d)), _const_spec((1, d)),
                  _const_spec(w_pg.shape), _const_spec(w_ple.shape)],
        out_specs=(_row_spec(TM, d), _row_spec(TM, d), _row_spec(TM, d), _row_spec(TM, d),
                   _acc_spec((1, LANES)), _acc_spec((1, d)), _acc_spec((1, d))),
        compiler_params=_seq_params(),
    )(x2, p, target, g_ple, g_final, w_pg, w_ple)


def _mlp_bwd(dx2, a, x1, g_mlp, w_up, w_down):
    t_len, d = x1.shape
    d_ff = w_up.shape[1]
    ch = _ff_chunk(d_ff)

    def body(dx2_ref, a_ref, x1_ref, g_ref, wu_ref, wd_ref, dx1_ref, da_ref, h2_ref, dg_ref):
        @pl.when(pl.program_id(0) == 0)
        def _():
            dg_ref[...] = jnp.zeros_like(dg_ref)

        dx2v = dx2_ref[...]
        dx2b = dx2v.astype(BF16)
        xn, r = _rms(x1_ref[...])
        h2_ref[...] = (xn * g_ref[...]).astype(BF16)
        dh = jnp.zeros((TM, d), F32)
        for j in range(d_ff // ch):
            dact = _dot_nt(dx2b, wd_ref[j * ch:(j + 1) * ch, :])
            da = (dact * 2.0 * jnp.maximum(a_ref[:, j * ch:(j + 1) * ch].astype(F32), 0.0)).astype(BF16)
            da_ref[:, j * ch:(j + 1) * ch] = da
            dh = dh + _dot_nt(da, wu_ref[:, j * ch:(j + 1) * ch])
        dres, dg = _rms_bwd(dh, xn, r, g_ref[...])
        dg_ref[...] += dg
        dx1_ref[...] = dx2v + dres

    return pl.pallas_call(
        body, name="mlp_bwd", grid=(t_len // TM,),
        out_shape=(jax.ShapeDtypeStruct((t_len, d), F32), jax.ShapeDtypeStruct((t_len, d_ff), BF16),
                   jax.ShapeDtypeStruct((t_len, d), BF16), jax.ShapeDtypeStruct((1, d), F32)),
        in_specs=[_row_spec(TM, d), _row_spec(TM, d_ff), _row_spec(TM, d), _const_spec((1, d)),
                  _const_spec(w_up.shape), _const_spec(w_down.shape)],
        out_specs=(_row_spec(TM, d), _row_spec(TM, d_ff), _row_spec(TM, d), _acc_spec((1, d))),
        compiler_params=_seq_params(),
    )(dx2, a, x1, g_mlp, w_up, w_down)


def _mix_bwd(dx1, o_fox, o_sb, gl, w_bf, w_bs, w_out, b_gate):
    t_len, d = dx1.shape

    def body(dx1_ref, of_ref, os_ref, gl_ref, wbf_ref, wbs_ref, wo_ref, bg_ref,
             mg_ref, dbf_ref, dbs_ref, dgl_ref, dof_ref, dos_ref, dbg_ref):
        @pl.when(pl.program_id(0) == 0)
        def _():
            dbg_ref[...] = jnp.zeros_like(dbg_ref)

        dmerged = _dot_nt(dx1_ref[...].astype(BF16), wo_ref[...])
        br_f = _dot(of_ref[...], wbf_ref[...])
        br_s = _dot(os_ref[...], wbs_ref[...])
        ga = _sigmoid(gl_ref[:, :d].astype(F32) + bg_ref[0:1, :])
        gb = _sigmoid(gl_ref[:, d:].astype(F32) + bg_ref[1:2, :])
        mg_ref[...] = (ga * br_f + gb * br_s).astype(BF16)
        dbf = (dmerged * ga).astype(BF16)
        dbs = (dmerged * gb).astype(BF16)
        dbf_ref[...] = dbf
        dbs_ref[...] = dbs
        dla = dmerged * br_f * ga * (1.0 - ga)
        dlb = dmerged * br_s * gb * (1.0 - gb)
        dgl_ref[:, :d] = dla.astype(BF16)
        dgl_ref[:, d:] = dlb.astype(BF16)
        dbg_ref[0:1, :] += jnp.sum(dla, axis=0, keepdims=True)
        dbg_ref[1:2, :] += jnp.sum(dlb, axis=0, keepdims=True)
        dof_ref[...] = _dot_nt(dbf, wbf_ref[...]).astype(BF16)
        dos_ref[...] = _dot_nt(dbs, wbs_ref[...]).astype(BF16)

    shp_d = jax.ShapeDtypeStruct((t_len, d), BF16)
    shp_h = jax.ShapeDtypeStruct((t_len, D_BRANCH), BF16)
    return pl.pallas_call(
        body, name="mix_bwd", grid=(t_len // TM,),
        out_shape=(shp_d, shp_d, shp_d, jax.ShapeDtypeStruct((t_len, 2 * d), BF16), shp_h, shp_h,
                   jax.ShapeDtypeStruct((2, d), F32)),
        in_specs=[_row_spec(TM, d), _row_spec(TM, D_BRANCH), _row_spec(TM, D_BRANCH), _row_spec(TM, 2 * d),
                  _const_spec(w_bf.shape), _const_spec(w_bs.shape), _const_spec(w_out.shape), _const_spec(b_gate.shape)],
        out_specs=(_row_spec(TM, d), _row_spec(TM, d), _row_spec(TM, d), _row_spec(TM, 2 * d),
                   _row_spec(TM, D_BRANCH), _row_spec(TM, D_BRANCH), _acc_spec((2, d))),
        compiler_params=_seq_params(),
    )(dx1, o_fox, o_sb, gl, w_bf, w_bs, w_out, b_gate)


def _inproj_bwd(dqkv_a, dqkv_b, dgl, df, dx1, x, g_mix, w_pad):
    t_len, d = x.shape
    n_qkv = 6 * D_BRANCH

    def body(da_ref, db_ref, dgl_ref, df_ref, dx1_ref, x_ref, g_ref, w_ref, dx_ref, h1_ref, dg_ref):
        @pl.when(pl.program_id(0) == 0)
        def _():
            dg_ref[...] = jnp.zeros_like(dg_ref)

        xn, r = _rms(x_ref[...])
        h1_ref[...] = (xn * g_ref[...]).astype(BF16)
        dh = _dot_nt(df_ref[...], w_ref[:, n_qkv + 2 * d:])
        for j in range(3):
            dh = dh + _dot_nt(da_ref[j], w_ref[:, j * D_BRANCH:(j + 1) * D_BRANCH])
            dh = dh + _dot_nt(db_ref[j], w_ref[:, (3 + j) * D_BRANCH:(4 + j) * D_BRANCH])
        for j in range(2):
            dh = dh + _dot_nt(dgl_ref[:, j * d:(j + 1) * d], w_ref[:, n_qkv + j * d:n_qkv + (j + 1) * d])
        dres, dg = _rms_bwd(dh, xn, r, g_ref[...])
        dg_ref[...] += dg
        dx_ref[...] = dx1_ref[...] + dres

    return pl.pallas_call(
        body, name="inproj_bwd", grid=(t_len // TM,),
        out_shape=(jax.ShapeDtypeStruct((t_len, d), F32), jax.ShapeDtypeStruct((t_len, d), BF16),
                   jax.ShapeDtypeStruct((1, d), F32)),
        in_specs=[_row3_spec(3, TM, D_BRANCH), _row3_spec(3, TM, D_BRANCH), _row_spec(TM, 2 * d), _row_spec(TM, LANES),
                  _row_spec(TM, d), _row_spec(TM, d), _const_spec((1, d)), _const_spec(w_pad.shape)],
        out_specs=(_row_spec(TM, d), _row_spec(TM, d), _acc_spec((1, d))),
        compiler_params=_seq_params(),
    )(dqkv_a, dqkv_b, dgl, df, dx1, x, g_mix, w_pad)


def _cols_to_slabs(full):
    r, c8 = full.shape
    return full.reshape(r, N_DEV, c8 // N_DEV).transpose(1, 0, 2)


def _slabs_to_cols(slabs):
    n, r, c = slabs.shape
    return slabs.transpose(1, 0, 2).reshape(r, n * c)


def _win_sizes(d):
    return (D_BRANCH, D_BRANCH, D_BRANCH, N_HEADS, D_BRANCH, D_BRANCH, D_BRANCH, d, d)


def _split_win(w, d):
    out, off = [], 0
    for s in _win_sizes(d):
        out.append(w[:, off:off + s])
        off += s
    return out


def _pad_win(w_full, d):
    qa, ka, va, fa, qb, kb, vb, ga, gb = _split_win(w_full, d)
    scale = HEAD_DIM ** -0.5
    fpad = jnp.pad(fa, ((0, 0), (0, LANES - N_HEADS)))
    return jnp.concatenate([qa * scale, ka, va, qb * scale, kb, vb, ga, gb, fpad], axis=1)


def _unpad_dwin(dqkv_a, dqkv_b, dgates, dforget, d):
    scale = HEAD_DIM ** -0.5
    return jnp.concatenate([dqkv_a[0] * scale, dqkv_a[1], dqkv_a[2], dforget[:, :N_HEADS],
                            dqkv_b[0] * scale, dqkv_b[1], dqkv_b[2], dgates], axis=1)


def _pad_rows(a, rows):
    return jnp.pad(a, [(0, 0)] * (a.ndim - 2) + [(0, rows - a.shape[-2]), (0, 0)])


def kernel(x, p, g_mix, w_in, b_forget, b_gate, w_branch_fox, w_branch_sb, w_out, g_mlp, w_up, w_down, g_ple, w_ple_gate, w_ple, g_final, loss_target, m_g_mix, m_w_in, m_b_forget, m_b_gate, m_w_branch_fox, m_w_branch_sb, m_w_out, m_g_mlp, m_w_up, m_w_down, m_g_ple, m_w_ple_gate, m_w_ple, m_g_final, v_g_mix, v_w_in, v_b_forget, v_b_gate, v_w_branch_fox, v_w_branch_sb, v_w_out, v_g_mlp, v_w_up, v_w_down, v_g_ple, v_w_ple_gate, v_w_ple, v_g_final):
    batch, seq, d = x.shape
    t_len = batch * seq
    d_ple = p.shape[-1]
    d_ff = w_up.shape[-1] * N_DEV
    dn = d // N_DEV
    fn = d_ff // N_DEV
    my_c = lax.axis_index("c")
    my_dev = 4 * lax.axis_index("x") + 2 * lax.axis_index("y") + my_c

    bg_hi = b_gate[0].astype(BF16)
    bg_r = b_gate[0] - bg_hi.astype(F32)
    bg_mid = bg_r.astype(BF16)
    bg_lo = (bg_r - bg_mid.astype(F32)).astype(BF16)
    narrow_rows = 2 * D_BRANCH + d_ple + 6
    narrow_rows_pad = -(-narrow_rows // 16) * 16
    wide = jnp.concatenate([w_out[0], w_down[0], w_ple_gate[0]], axis=0).astype(BF16)
    narrow = _pad_rows(jnp.concatenate(
        [w_branch_fox[0].astype(BF16), w_branch_sb[0].astype(BF16), w_ple[0].astype(BF16), bg_hi, bg_mid, bg_lo],
        axis=0), narrow_rows_pad)
    g_in, g_up, g_wide, g_narrow = _all_gather([w_in[0].astype(BF16), w_up[0].astype(BF16), wide, narrow])
    w_pad = _pad_win(_slabs_to_cols(g_in), d)
    w_up_full = _slabs_to_cols(g_up)
    w_out_full = g_wide[:, :dn].reshape(d, d)
    w_down_full = g_wide[:, dn:dn + fn].reshape(d_ff, d)
    w_pg_full = g_wide[:, dn + fn:].reshape(d, d)
    w_bf_full = _slabs_to_cols(g_narrow[:, :D_BRANCH])
    w_bs_full = _slabs_to_cols(g_narrow[:, D_BRANCH:2 * D_BRANCH])
    w_ple_full = _slabs_to_cols(g_narrow[:, 2 * D_BRANCH:2 * D_BRANCH + d_ple])
    bg_terms = g_narrow[:, 2 * D_BRANCH + d_ple:narrow_rows].astype(F32)
    b_gate_full = _slabs_to_cols(bg_terms[:, 0:2] + bg_terms[:, 2:4] + bg_terms[:, 4:6])
    bf_pad = jnp.pad(b_forget, ((0, 0), (0, LANES - N_HEADS)))

    x2d = x.reshape(t_len, d)
    p2d = p.reshape(t_len, d_ple)
    tgt2d = loss_target.reshape(t_len, d)
    qkv, gl, fpre, csum = _inproj_fwd(x2d, g_mix, w_pad, bf_pad, seq)
    c_heads = csum[:, :N_HEADS].reshape(batch, seq, N_HEADS)
    c_row = c_heads.transpose(0, 2, 1).reshape(batch, N_HEADS, seq // TK, TK)
    c_stat = jnp.pad(c_heads.reshape(batch, seq, N_HEADS // NH, NH).transpose(0, 2, 1, 3),
                     ((0, 0), (0, 0), (0, 0), (0, LANES - NH)))
    o_fox, lse = _fox_fwd(qkv, c_stat, c_row, batch, seq)
    o_sb, ltot = _sb_fwd(qkv, batch, seq)
    x1 = _mix_fwd(o_fox, o_sb, gl, x2d, w_bf_full, w_bs_full, w_out_full, b_gate_full)
    a_up, x2 = _mlp_fwd(x1, g_mlp, w_up_full, w_down_full)

    dx2, h3, dpre, dpe, loss_acc, dg_ple, dg_final = _head_fwd_bwd(
        x2, p2d, tgt2d, g_ple, g_final.reshape(1, d), w_pg_full, w_ple_full)
    dx1, da_up, h2, dg_mlp = _mlp_bwd(dx2, a_up, x1, g_mlp, w_up_full, w_down_full)
    merged, dbr_f, dbr_s, dgl, do_fox, do_sb, dbg = _mix_bwd(
        dx1, o_fox, o_sb, gl, w_bf_full, w_bs_full, w_out_full, b_gate_full)
    dqkv_a, dc_keys, dc_queries = _fox_bwd(qkv, o_fox, do_fox, lse, c_stat, c_row, batch, seq)
    dqkv_b = _sb_bwd(qkv, do_sb, ltot, batch, seq)
    dck_tok = dc_keys.reshape(batch, N_HEADS, seq).transpose(0, 2, 1).reshape(t_len, N_HEADS)
    dcq_tok = dc_queries[..., :NH].transpose(0, 2, 1, 3).reshape(t_len, N_HEADS)
    lane_pad = ((0, 0), (0, LANES - N_HEADS))
    df, db_forget = _forget_bwd(jnp.pad(dcq_tok, lane_pad), jnp.pad(dck_tok, lane_pad), fpre, batch, seq)
    grad_x, h1, dg_mix = _inproj_bwd(dqkv_a, dqkv_b, dgl, df, dx1, x2d, g_mix, w_pad)

    gw_ple = _matmul_tn("dw_ple", p2d, dpe)
    gw_pg = _matmul_tn("dw_ple_gate", h3, dpre)
    gw_down = _matmul_tn("dw_down", a_up, dx2, relu2=True)
    gw_up = _matmul_tn("dw_up", h2, da_up)
    gw_out = _matmul_tn("dw_out", merged, dx1)
    gw_bf = _matmul_tn("dw_branch_fox", o_fox, dbr_f)
    gw_bs = _matmul_tn("dw_branch_sb", o_sb, dbr_s)
    gw_in = _unpad_dwin(_matmul_tn("dw_in_fox", h1, dqkv_a), _matmul_tn("dw_in_sb", h1, dqkv_b),
                        _matmul_tn("dw_in_gates", h1, dgl), _matmul_tn("dw_in_forget", h1, df), d)

    part_in = _cols_to_slabs(gw_in).astype(BF16)
    part_up = _cols_to_slabs(gw_up).astype(BF16)
    part_wide = jnp.concatenate([gw_out.reshape(N_DEV, dn, d), gw_down.reshape(N_DEV, fn, d),
                                 gw_pg.reshape(N_DEV, dn, d)], axis=1).astype(BF16)
    part_narrow = _pad_rows(jnp.concatenate(
        [_cols_to_slabs(gw_bf), _cols_to_slabs(gw_bs), _cols_to_slabs(gw_ple)], axis=1).astype(BF16), narrow_rows_pad)
    partials = [part_in, part_up, part_wide, part_narrow]
    received = _rs_core_pair(partials)
    chip_sums = [_pair_add("pair_add_%d" % i, pt, rc, my_c) for i, (pt, rc) in enumerate(zip(partials, received))]
    s_in, s_up, s_wide, s_narrow = _rs_chips(chip_sums)

    small = jnp.concatenate([
        dg_mix, dg_mlp, dg_ple, dg_final, jnp.pad(db_forget[:, :N_HEADS], ((0, 0), (0, d - N_HEADS))), dbg,
        jnp.pad(loss_acc[:, :1], ((0, 0), (0, d - 1)))], axis=0)
    small = _all_reduce_small(small)
    loss = small[7, 0]
    small_grads = {
        "g_mix": small[0:1], "g_mlp": small[1:2], "g_ple": small[2:3], "g_final": small[3:4],
        "b_forget": small[4:5, :N_HEADS],
        "b_gate": lax.dynamic_slice_in_dim(small[5:7], my_dev * dn, dn, axis=1),
    }

    weights = {"g_mix": g_mix, "w_in": w_in, "b_forget": b_forget, "b_gate": b_gate, "w_branch_fox": w_branch_fox,
               "w_branch_sb": w_branch_sb, "w_out": w_out, "g_mlp": g_mlp, "w_up": w_up, "w_down": w_down,
               "g_ple": g_ple, "w_ple_gate": w_ple_gate, "w_ple": w_ple, "g_final": g_final}
    m_in = {"g_mix": m_g_mix, "w_in": m_w_in, "b_forget": m_b_forget, "b_gate": m_b_gate,
            "w_branch_fox": m_w_branch_fox, "w_branch_sb": m_w_branch_sb, "w_out": m_w_out, "g_mlp": m_g_mlp,
            "w_up": m_w_up, "w_down": m_w_down, "g_ple": m_g_ple, "w_ple_gate": m_w_ple_gate, "w_ple": m_w_ple,
            "g_final": m_g_final}
    v_in = {"g_mix": v_g_mix, "w_in": v_w_in, "b_forget": v_b_forget, "b_gate": v_b_gate,
            "w_branch_fox": v_w_branch_fox, "w_branch_sb": v_w_branch_sb, "w_out": v_w_out, "g_mlp": v_g_mlp,
            "w_up": v_w_up, "w_down": v_w_down, "g_ple": v_g_ple, "w_ple_gate": v_w_ple_gate, "w_ple": v_w_ple,
            "g_final": v_g_final}
    names = list(weights)

    def as2d(a):
        return a.reshape(-1, a.shape[-1])

    result = {}
    big = {"w_in": (s_in, 0), "w_up": (s_up, 0), "w_out": (s_wide, 0), "w_down": (s_wide, dn),
           "w_ple_gate": (s_wide, dn + fn), "w_branch_fox": (s_narrow, 0), "w_branch_sb": (s_narrow, D_BRANCH),
           "w_ple": (s_narrow, 2 * D_BRANCH)}
    for n, (parts, off) in big.items():
        result[n] = _adamw_parts("adamw_" + n, as2d(weights[n]), parts, off, as2d(m_in[n]), as2d(v_in[n]))
    small_names = list(small_grads)
    small_out = _adamw_small([(as2d(weights[n]), small_grads[n], as2d(m_in[n]), as2d(v_in[n])) for n in small_names])
    for n, (dlt, nm, nv) in zip(small_names, small_out):
        result[n] = (small_grads[n], dlt, nm, nv)
    outs = [[result[n][k].reshape(weights[n].shape) for n in names] for k in range(4)]
    return (loss, grad_x.reshape(x.shape), *outs[0], *outs[1], *outs[2], *outs[3])
```

```python
import jax
import jax.numpy as jnp
from jax import lax
from jax.experimental import pallas as pl
from jax.experimental.pallas import tpu as pltpu

F32 = jnp.float32
BF16 = jnp.bfloat16

HEAD_DIM = 64
N_HEADS = 8
D_BRANCH = N_HEADS * HEAD_DIM
EPS = 1e-6
ADAM_LR = 0.001
ADAM_B1 = 0.9
ADAM_B2 = 0.999
ADAM_EPS = 1e-08
ADAM_WD = 0.01
ADAM_STEP = 10

N_DEV = 8
LANES = 128
TM = 256
TQ = 256
TK = 256
NH = 4
HEAD_SLOT = 128
C_TERMS_Q = 64
C_ONES_K = 64
C_TERMS_K = 67
C_ONES_Q = 67
NEG = -1e30
VMEM_LIMIT = 56 * 1024 * 1024
MESH = pl.DeviceIdType.MESH


def _dot(a, b):
    return jnp.dot(a, b, preferred_element_type=F32)


def _dot_nt(a, b):
    return lax.dot_general(a, b, (((1,), (1,)), ((), ())), preferred_element_type=F32)


def _dot_tn(a, b):
    return lax.dot_general(a, b, (((0,), (0,)), ((), ())), preferred_element_type=F32)


def _sigmoid(x):
    return 1.0 / (1.0 + jnp.exp(-x))


def _softplus(x):
    return jnp.maximum(x, 0.0) + jnp.log(1.0 + jnp.exp(-jnp.abs(x)))


def _split2(x):
    hi = x.astype(BF16)
    lo = (x - hi.astype(F32)).astype(BF16)
    return hi, lo


def _split3(x):
    hi = x.astype(BF16)
    r = x - hi.astype(F32)
    mid = r.astype(BF16)
    lo = (r - mid.astype(F32)).astype(BF16)
    return hi, mid, lo


def _rows_dot_mask(x, mask_bf16):
    hi, lo = _split2(x)
    return _dot(hi, mask_bf16) + _dot(lo, mask_bf16)


def _tri(n, rel):
    r = lax.broadcasted_iota(jnp.int32, (n, n), 0)
    c = lax.broadcasted_iota(jnp.int32, (n, n), 1)
    return rel(r, c).astype(BF16)


def _rms(x):
    r = lax.rsqrt(jnp.mean(x * x, axis=-1, keepdims=True) + EPS)
    return x * r, r


def _rms_bwd(dh, xn, r, g):
    dxn = dh * g
    dx = r * (dxn - xn * jnp.mean(dxn * xn, axis=-1, keepdims=True))
    return dx, jnp.sum(dh * xn, axis=0, keepdims=True)


def _row_spec(tm, cols):
    return pl.BlockSpec((tm, cols), lambda i: (i, 0))


def _row3_spec(g, tm, cols):
    return pl.BlockSpec((g, tm, cols), lambda i: (0, i, 0))


def _const_spec(shape):
    nd = len(shape)
    return pl.BlockSpec(shape, lambda i: (0,) * nd, pipeline_mode=pl.Buffered(1))


def _acc_spec(shape):
    nd = len(shape)
    return pl.BlockSpec(shape, lambda i: (0,) * nd)


def _seq_params():
    return pltpu.CompilerParams(dimension_semantics=("arbitrary",), vmem_limit_bytes=VMEM_LIMIT)


def _mesh_pos():
    return lax.axis_index("x"), lax.axis_index("y"), lax.axis_index("c")


def _other_chips(x, y):
    return [(1 - x, y), (x, 1 - y), (1 - x, 1 - y)]


def _hbm_specs(n):
    return [pl.BlockSpec(memory_space=pl.ANY)] * n


def _all_gather(shards):
    n = len(shards)

    def body(*refs):
        x_refs, out_refs = refs[:n], refs[n:2 * n]
        send_sems, recv_sems, local_sems = refs[2 * n:]
        x, y, c = _mesh_pos()
        me, sibling = (x, y, c), (x, y, 1 - c)
        chips = _other_chips(x, y)

        def index(px, py, pc):
            return 4 * px + 2 * py + pc

        def copy(a, k, block, to, src=None):
            slab = out_refs[a].at[index(*block)]
            return pltpu.make_async_remote_copy(
                src_ref=slab if src is None else src, dst_ref=slab,
                send_sem=send_sems.at[7 * a + k], recv_sem=recv_sems.at[7 * a + k], device_id=to, device_id_type=MESH)

        mine, first, passed = [], [], []
        for a in range(n):
            mine.append(pltpu.make_async_copy(x_refs[a], out_refs[a].at[index(*me)], local_sems.at[a]))
            mine[-1].start()
            first.append(copy(a, 0, me, sibling, src=x_refs[a]))
            first += [copy(a, 1 + j, me, (cx, cy, c), src=x_refs[a]) for j, (cx, cy) in enumerate(chips)]
        for cp in first:
            cp.start()
        for j, (cx, cy) in enumerate(chips):
            for a in range(n):
                copy(a, 1 + j, (cx, cy, c), me).wait_recv()
                passed.append(copy(a, 4 + j, (cx, cy, c), sibling))
                passed[-1].start()
        for a in range(n):
            copy(a, 0, sibling, me).wait_recv()
            for j, (cx, cy) in enumerate(chips):
                copy(a, 4 + j, (cx, cy, 1 - c), me).wait_recv()
        for cp in first + passed:
            cp.wait_send()
        for cp in mine:
            cp.wait()

    return pl.pallas_call(
        body, name="all_gather_weights",
        out_shape=[jax.ShapeDtypeStruct((N_DEV,) + s.shape, s.dtype) for s in shards],
        in_specs=_hbm_specs(n), out_specs=_hbm_specs(n),
        scratch_shapes=[pltpu.SemaphoreType.DMA((7 * n,)), pltpu.SemaphoreType.DMA((7 * n,)),
                        pltpu.SemaphoreType.DMA((n,))],
    )(*shards)


def _rs_core_pair(partials):
    n = len(partials)

    def body(*refs):
        p_refs, recv_refs = refs[:n], refs[n:2 * n]
        send_sems, recv_sems = refs[2 * n:]
        x, y, c = _mesh_pos()
        for a in range(n):
            for chip in range(4):
                pltpu.make_async_remote_copy(
                    src_ref=p_refs[a].at[2 * chip + (1 - c)], dst_ref=recv_refs[a].at[chip],
                    send_sem=send_sems.at[a], recv_sem=recv_sems.at[a],
                    device_id=(x, y, 1 - c), device_id_type=MESH).start()
        for a in range(n):
            pltpu.make_async_remote_copy(
                src_ref=recv_refs[a], dst_ref=recv_refs[a], send_sem=send_sems.at[a], recv_sem=recv_sems.at[a],
                device_id=(x, y, 1 - c), device_id_type=MESH).wait()

    return pl.pallas_call(
        body, name="reduce_scatter_core_pair",
        out_shape=[jax.ShapeDtypeStruct((4,) + s.shape[1:], s.dtype) for s in partials],
        in_specs=_hbm_specs(n), out_specs=_hbm_specs(n),
        scratch_shapes=[pltpu.SemaphoreType.DMA((n,)), pltpu.SemaphoreType.DMA((n,))],
    )(*partials)


def _rs_chips(chip_sums):
    n = len(chip_sums)

    def body(*refs):
        cs_refs, out_refs = refs[:n], refs[n:2 * n]
        send_sems, recv_sems, local_sems = refs[2 * n:]
        x, y, c = _mesh_pos()
        chip = 2 * x + y
        chips = _other_chips(x, y)
        mine, sends = [], []
        for a in range(n):
            mine.append(pltpu.make_async_copy(cs_refs[a].at[chip], out_refs[a].at[chip], local_sems.at[a]))
            mine[-1].start()
            for j, (cx, cy) in enumerate(chips):
                sends.append(pltpu.make_async_remote_copy(
                    src_ref=cs_refs[a].at[2 * cx + cy], dst_ref=out_refs[a].at[chip],
                    send_sem=send_sems.at[3 * a + j], recv_sem=recv_sems.at[3 * a + j],
                    device_id=(cx, cy, c), device_id_type=MESH))
                sends[-1].start()
        for a in range(n):
            for j, (cx, cy) in enumerate(chips):
                pltpu.make_async_remote_copy(
                    src_ref=cs_refs[a].at[chip], dst_ref=out_refs[a].at[2 * cx + cy],
                    send_sem=send_sems.at[3 * a + j], recv_sem=recv_sems.at[3 * a + j],
                    device_id=(x, y, c), device_id_type=MESH).wait_recv()
        for cp in sends:
            cp.wait_send()
        for cp in mine:
            cp.wait()

    return pl.pallas_call(
        body, name="reduce_scatter_chips",
        out_shape=[jax.ShapeDtypeStruct(s.shape, s.dtype) for s in chip_sums],
        in_specs=_hbm_specs(n), out_specs=_hbm_specs(n),
        scratch_shapes=[pltpu.SemaphoreType.DMA((3 * n,)), pltpu.SemaphoreType.DMA((3 * n,)),
                        pltpu.SemaphoreType.DMA((n,))],
    )(*chip_sums)


def _all_reduce_small(vec):
    rows, cols = vec.shape

    def body(x_ref, land_ref, sum_ref, send_sems, recv_sems):
        x, y, c = _mesh_pos()
        me = 4 * x + 2 * y + c
        land_ref[me] = x_ref[...]
        flips = [(fx, fy, fc) for fx in (0, 1) for fy in (0, 1) for fc in (0, 1)][1:]

        def flipped(f):
            return tuple((1 - v) if b else v for v, b in zip((x, y, c), f))

        sends = []
        for k, f in enumerate(flips):
            sends.append(pltpu.make_async_remote_copy(
                src_ref=x_ref, dst_ref=land_ref.at[me], send_sem=send_sems.at[k], recv_sem=recv_sems.at[k],
                device_id=flipped(f), device_id_type=MESH))
            sends[-1].start()
        for k, f in enumerate(flips):
            px, py, pc = flipped(f)
            pltpu.make_async_remote_copy(
                src_ref=x_ref, dst_ref=land_ref.at[4 * px + 2 * py + pc], send_sem=send_sems.at[k],
                recv_sem=recv_sems.at[k], device_id=(x, y, c), device_id_type=MESH).wait_recv()
        for cp in sends:
            cp.wait_send()
        total = land_ref[0]
        for d in range(1, N_DEV):
            total = total + land_ref[d]
        sum_ref[...] = total

    vm = pl.BlockSpec(memory_space=pltpu.VMEM)
    return pl.pallas_call(
        body, name="all_reduce_small",
        out_shape=(jax.ShapeDtypeStruct((N_DEV, rows, cols), F32), jax.ShapeDtypeStruct((rows, cols), F32)),
        in_specs=[vm], out_specs=(vm, vm),
        scratch_shapes=[pltpu.SemaphoreType.DMA((7,)), pltpu.SemaphoreType.DMA((7,))],
    )(vec)[1]


def _block_rows(rows, cols, itemsize, align, row_off=0):
    best = None
    for t in range(align, rows + 1, align):
        if rows % t == 0 and row_off % t == 0 and t * cols * itemsize <= (1 << 20):
            best = t
    return rows if best is None else best


def _pair_add(name, partial, recv, my_c):
    _, rows, cols = partial.shape
    br = _block_rows(rows, cols, 2, 16)

    def body(c_ref, a_ref, b_ref, o_ref):
        o_ref[...] = (a_ref[...].astype(F32) + b_ref[...].astype(F32)).astype(BF16)

    return pl.pallas_call(
        body, name=name,
        grid_spec=pltpu.PrefetchScalarGridSpec(
            num_scalar_prefetch=1, grid=(4, rows // br),
            in_specs=[pl.BlockSpec((None, None, br, cols), lambda j, i, c_ref: (j, c_ref[0], i, 0)),
                      pl.BlockSpec((None, br, cols), lambda j, i, c_ref: (j, i, 0))],
            out_specs=pl.BlockSpec((None, br, cols), lambda j, i, c_ref: (j, i, 0))),
        out_shape=jax.ShapeDtypeStruct((4, rows, cols), BF16),
    )(my_c.reshape(1).astype(jnp.int32), partial.reshape(4, 2, rows, cols), recv)


def _adam_update(w, g, m, v):
    nm = ADAM_B1 * m + (1.0 - ADAM_B1) * g
    nv = ADAM_B2 * v + (1.0 - ADAM_B2) * (g * g)
    m_hat = nm / (1.0 - ADAM_B1 ** ADAM_STEP)
    v_hat = nv / (1.0 - ADAM_B2 ** ADAM_STEP)
    return -ADAM_LR * (m_hat / (jnp.sqrt(v_hat) + ADAM_EPS) + ADAM_WD * w), nm, nv


def _adamw_parts(name, w, parts, row_off, m, v):
    rows, cols = w.shape
    tr = _block_rows(rows, cols, 4, 16, row_off)
    assert rows % tr == 0 and row_off % tr == 0
    off = row_off // tr

    def body(w_ref, p_ref, m_ref, v_ref, g_ref, d_ref, nm_ref, nv_ref):
        g = p_ref[0].astype(F32)
        for j in range(1, 4):
            g = g + p_ref[j].astype(F32)
        g_ref[...] = g
        d_ref[...], nm_ref[...], nv_ref[...] = _adam_update(w_ref[...], g, m_ref[...], v_ref[...])

    spec = pl.BlockSpec((tr, cols), lambda i: (i, 0))
    shp = jax.ShapeDtypeStruct((rows, cols), F32)
    return pl.pallas_call(
        body, name=name, grid=(rows // tr,), out_shape=(shp,) * 4,
        in_specs=[spec, pl.BlockSpec((4, tr, cols), lambda i: (0, off + i, 0)), spec, spec], out_specs=(spec,) * 4,
    )(w, parts, m, v)


def _adamw_small(tensors):
    n = len(tensors)

    def body(*refs):
        ins, outs = refs[:4 * n], refs[4 * n:]
        for t in range(n):
            w_ref, g_ref, m_ref, v_ref = ins[4 * t:4 * t + 4]
            d, nm, nv = _adam_update(w_ref[...], g_ref[...], m_ref[...], v_ref[...])
            outs[3 * t][...], outs[3 * t + 1][...], outs[3 * t + 2][...] = d, nm, nv

    vm = pl.BlockSpec(memory_space=pltpu.VMEM)
    out = pl.pallas_call(
        body, name="adamw_small",
        out_shape=[jax.ShapeDtypeStruct(t[0].shape, F32) for t in tensors for _ in range(3)],
        in_specs=[vm] * (4 * n), out_specs=[vm] * (3 * n),
    )(*[a for t in tensors for a in t])
    return [tuple(out[3 * t:3 * t + 3]) for t in range(n)]


def _matmul_tn(name, a, b, relu2=False):
    squeeze = b.ndim == 2
    if squeeze:
        b = b[None]
    t_len, k_len = a.shape
    groups, _, n_len = b.shape
    tt = min(t_len, 512)
    tk = min(k_len, 512)
    tn = min(n_len, 1024)
    nt = t_len // tt

    def body(a_ref, b_ref, o_ref):
        @pl.when(pl.program_id(3) == 0)
        def _():
            o_ref[...] = jnp.zeros_like(o_ref)

        av = a_ref[...]
        if relu2:
            av = jnp.square(jnp.maximum(av.astype(F32), 0.0))
        o_ref[...] += _dot_tn(av.astype(BF16), b_ref[...].astype(BF16))

    out = pl.pallas_call(
        body, name=name, grid=(groups, k_len // tk, n_len // tn, nt),
        out_shape=jax.ShapeDtypeStruct((groups, k_len, n_len), F32),
        in_specs=[pl.BlockSpec((tt, tk), lambda g, i, j, t: (t, i)),
                  pl.BlockSpec((None, tt, tn), lambda g, i, j, t: (g, t, j))],
        out_specs=pl.BlockSpec((None, tk, tn), lambda g, i, j, t: (g, i, j)),
        compiler_params=pltpu.CompilerParams(
            dimension_semantics=("parallel", "parallel", "parallel", "arbitrary"), vmem_limit_bytes=VMEM_LIMIT),
    )(a, b)
    return out[0] if squeeze else out


def _pad_layout(d):
    names = ("qf", "kf", "vf", "qb", "kb", "vb", "gates", "forget")
    sizes = (N_HEADS * HEAD_SLOT, N_HEADS * HEAD_SLOT, D_BRANCH, D_BRANCH, D_BRANCH, D_BRANCH, 2 * d, LANES)
    out, off = {}, 0
    for n, s in zip(names, sizes):
        out[n] = (off, off + s)
        off += s
    return out, off


def _slot_rows(xt, extra):
    parts = []
    for h in range(N_HEADS):
        parts += [xt[h * HEAD_DIM:(h + 1) * HEAD_DIM, :], extra]
    return jnp.concatenate(parts, axis=0)


def _inproj_fwd(x, g_mix, w_pad, bf_pad, place_q, place_k, ones_q, ones_k, seq):
    t_len, d = x.shape
    lay, _ = _pad_layout(d)
    tiles_per_seq = seq // TM
    slot_w = N_HEADS * HEAD_SLOT

    def body(x_ref, g_ref, w_ref, bf_ref, pq_ref, pk_ref, oq_ref, ok_ref,
             qf_ref, kf_ref, kft_ref, vf_ref, vft_ref, qkvb_ref, kbt_ref, vbt_ref, gl_ref, fpre_ref, carry_ref):
        @pl.when(pl.program_id(0) % tiles_per_seq == 0)
        def _():
            carry_ref[...] = jnp.zeros_like(carry_ref)

        def proj(name):
            lo, hi = lay[name]
            return _dot(h, w_ref[:, lo:hi])

        xn, _ = _rms(x_ref[...])
        h = (xn * g_ref[...]).astype(BF16)
        fpre = proj("forget") + bf_ref[...]
        fpre_ref[...] = fpre
        logf = -_softplus(-fpre)
        lower = _tri(TM, lambda r, c: c <= r)
        hi, mid, lo = _split3(logf)
        c_val = carry_ref[...] + _dot(lower, hi) + _dot(lower, mid) + _dot(lower, lo)
        carry_ref[...] = carry_ref[...] + jnp.sum(logf, axis=0, keepdims=True)
        c3 = _split3(c_val)
        qf_ref[...] = (proj("qf") + sum(_dot(c3[j], pq_ref[j]) for j in range(3)) + oq_ref[...]).astype(BF16)
        kf = proj("kf") - sum(_dot(c3[j], pk_ref[j]) for j in range(3)) + ok_ref[...]
        kf_ref[...] = kf.astype(BF16)
        kft_ref[0] = kf.T.astype(BF16)
        row0 = (lax.broadcasted_iota(jnp.int32, (HEAD_DIM, TM), 0) == 0).astype(F32)
        zeros = jnp.zeros((HEAD_DIM, TM), F32)
        vf = proj("vf")
        vf_ref[...] = vf.astype(BF16)
        vft_ref[0] = _slot_rows(vf.T, row0).astype(BF16)
        qkvb_ref[0] = proj("qb").astype(BF16)
        kb = proj("kb")
        qkvb_ref[1] = kb.astype(BF16)
        kbt_ref[0] = _slot_rows(kb.T, zeros).astype(BF16)
        vb = proj("vb")
        qkvb_ref[2] = vb.astype(BF16)
        vbt_ref[0] = _slot_rows(vb.T, row0).astype(BF16)
        gl_ref[...] = proj("gates").astype(BF16)

    n_tiles = t_len // TM
    slot_shape = jax.ShapeDtypeStruct((t_len, slot_w), BF16)
    t_shape = jax.ShapeDtypeStruct((n_tiles, slot_w, TM), BF16)
    t_spec = pl.BlockSpec((1, slot_w, TM), lambda i: (i, 0, 0))
    return pl.pallas_call(
        body, name="inproj_fwd", grid=(n_tiles,),
        out_shape=(slot_shape, slot_shape, t_shape, jax.ShapeDtypeStruct((t_len, D_BRANCH), BF16), t_shape,
                   jax.ShapeDtypeStruct((3, t_len, D_BRANCH), BF16), t_shape, t_shape,
                   jax.ShapeDtypeStruct((t_len, 2 * d), BF16), jax.ShapeDtypeStruct((t_len, LANES), F32)),
        in_specs=[_row_spec(TM, d), _const_spec((1, d)), _const_spec(w_pad.shape), _const_spec((1, LANES)),
                  _const_spec(place_q.shape), _const_spec(place_k.shape), _const_spec((1, slot_w)),
                  _const_spec((1, slot_w))],
        out_specs=(_row_spec(TM, slot_w), _row_spec(TM, slot_w), t_spec, _row_spec(TM, D_BRANCH), t_spec,
                   _row3_spec(3, TM, D_BRANCH), t_spec, t_spec, _row_spec(TM, 2 * d), _row_spec(TM, LANES)),
        scratch_shapes=[pltpu.VMEM((1, LANES), F32)],
        compiler_params=_seq_params(),
    )(x, g_mix, w_pad, bf_pad, place_q, place_k, ones_q, ones_k)


def _slot_spec(seq):
    return pl.BlockSpec((seq, NH * HEAD_SLOT), lambda b, g: (b, g))


def _slot2_spec(seq):
    return pl.BlockSpec((2, seq, NH * HEAD_SLOT), lambda b, g: (0, b, g))


def _group_spec(seq):
    return pl.BlockSpec((seq, NH * HEAD_DIM), lambda b, g: (b, g))


def _group3_spec(which, seq):
    return pl.BlockSpec((None, seq, NH * HEAD_DIM), lambda b, g: (which, b, g))


def _tblock_spec(seq):
    return pl.BlockSpec((seq // TK, NH * HEAD_SLOT, TK), lambda b, g: (b, g, 0))


def _qrow_spec(seq):
    return pl.BlockSpec((None, NH, seq // TQ, TQ), lambda b, g: (b, g, 0, 0))


def _stat_spec(seq):
    return pl.BlockSpec((None, None, seq, LANES), lambda b, g: (b, g, 0, 0))


def _attn_params():
    return pltpu.CompilerParams(dimension_semantics=("parallel", "parallel"), vmem_limit_bytes=VMEM_LIMIT)


def _hcols(hh):
    return slice(hh * HEAD_DIM, (hh + 1) * HEAD_DIM)


def _hslot(hh):
    return slice(hh * HEAD_SLOT, (hh + 1) * HEAD_SLOT)


def _lane(hh):
    return slice(hh, hh + 1)


def _key_query_mask(rel):
    r = lax.broadcasted_iota(jnp.int32, (TK, TQ), 0)
    c = lax.broadcasted_iota(jnp.int32, (TK, TQ), 1)
    return rel(r, c)


def _heads_cat(vals):
    return jnp.concatenate(vals, axis=1)


def _untranspose(acc_t):
    return acc_t.T[:, :HEAD_DIM]


def _fox_fwd(qf, kf, vft, batch, seq):
    def body(q_ref, k_ref, vt_ref, o_ref, lse_ref, m_s, acc_s):
        causal = _key_query_mask(lambda r, c: r <= c)

        def tile(q0, kj, masked):
            krows = pl.ds(pl.multiple_of(kj * TK, TK), TK)
            for hh in range(NH):
                st = _dot_nt(k_ref[krows, _hslot(hh)], q_ref[pl.ds(q0, TQ), _hslot(hh)])
                if masked:
                    st = jnp.where(causal, st, NEG)
                m_old = m_s[hh]
                m_new = jnp.maximum(m_old, jnp.max(st, axis=0, keepdims=True))
                pt = jnp.exp(st - m_new)
                acc_s[hh] = jnp.exp(m_old - m_new) * acc_s[hh] + _dot(vt_ref[kj, _hslot(hh), :], pt.astype(BF16))
                m_s[hh] = m_new

        def q_loop(qi, _):
            q0 = pl.multiple_of(qi * TQ, TQ)
            m_s[...] = jnp.full(m_s.shape, NEG, F32)
            acc_s[...] = jnp.zeros_like(acc_s)

            def k_loop(kj, _):
                tile(q0, kj, False)
                return 0

            lax.fori_loop(0, qi, k_loop, 0)
            tile(q0, qi, True)
            outs = []
            for hh in range(NH):
                total = acc_s[hh, HEAD_DIM:HEAD_DIM + 1, :]
                outs.append(_untranspose(acc_s[hh] / total))
                lse_ref[hh, pl.ds(qi, 1), :] = m_s[hh] + jnp.log(total)
            o_ref[pl.ds(q0, TQ), :] = _heads_cat(outs).astype(BF16)
            return 0

        lax.fori_loop(0, seq // TQ, q_loop, 0)

    return pl.pallas_call(
        body, name="fox_fwd", grid=(batch, N_HEADS // NH),
        out_shape=(jax.ShapeDtypeStruct((batch * seq, D_BRANCH), BF16),
                   jax.ShapeDtypeStruct((batch, N_HEADS, seq // TQ, TQ), F32)),
        in_specs=[_slot_spec(seq), _slot_spec(seq), _tblock_spec(seq)],
        out_specs=(_group_spec(seq), _qrow_spec(seq)),
        scratch_shapes=[pltpu.VMEM((NH, 1, TQ), F32), pltpu.VMEM((NH, HEAD_SLOT, TQ), F32)],
        compiler_params=_attn_params(),
    )(qf, kf, vft)


def _fox_bwd(qf, kf, kft, vf, o, do, lse, batch, seq):
    n_q = seq // TQ

    def body(q_ref, k_ref, kt_ref, v_ref, o_ref, do_ref, lse_ref, dqk_ref, dv_ref, dcq_ref, dck_ref,
             delta_s, dqt_acc, dk_s, dv_s):
        causal = _key_query_mask(lambda r, c: r <= c)
        ones8 = jnp.ones((8, HEAD_DIM), BF16)
        dqt_acc[...] = jnp.zeros_like(dqt_acc)

        def prep(qi, _):
            rows = pl.ds(pl.multiple_of(qi * TQ, TQ), TQ)
            for hh in range(NH):
                hi, lo = _split2(do_ref[rows, _hcols(hh)].astype(F32) * o_ref[rows, _hcols(hh)].astype(F32))
                delta_s[hh, pl.ds(qi, 1), :] = (_dot_nt(ones8, hi) + _dot_nt(ones8, lo))[0:1, :]
            return 0

        lax.fori_loop(0, n_q, prep, 0)

        def tile(qi, kj, masked):
            rows = pl.ds(pl.multiple_of(qi * TQ, TQ), TQ)
            krows = pl.ds(pl.multiple_of(kj * TK, TK), TK)
            for hh in range(NH):
                q, dout = q_ref[rows, _hslot(hh)], do_ref[rows, _hcols(hh)]
                pt = jnp.exp(_dot_nt(k_ref[krows, _hslot(hh)], q) - lse_ref[hh, pl.ds(qi, 1), :])
                if masked:
                    pt = jnp.where(causal, pt, 0.0)
                dv_s[hh] += _dot(pt.astype(BF16), dout)
                dst = (pt * (_dot_nt(v_ref[krows, _hcols(hh)], dout) - delta_s[hh, pl.ds(qi, 1), :])).astype(BF16)
                dk_s[hh] += _dot(dst, q)
                dqt_acc[hh, qi] += _dot(kt_ref[kj, _hslot(hh), :], dst)

        def k_loop(kj, _):
            krows = pl.ds(pl.multiple_of(kj * TK, TK), TK)
            dk_s[...] = jnp.zeros_like(dk_s)
            dv_s[...] = jnp.zeros_like(dv_s)
            tile(kj, kj, True)

            def q_loop(qi, _):
                tile(qi, kj, False)
                return 0

            lax.fori_loop(kj + 1, n_q, q_loop, 0)
            dqk_ref[1, krows, :] = _heads_cat([dk_s[hh] for hh in range(NH)]).astype(BF16)
            dv_ref[krows, :] = _heads_cat([dv_s[hh] for hh in range(NH)]).astype(BF16)
            for hh in range(NH):
                dck_ref[krows, _lane(hh)] = dk_s[hh, :, C_ONES_Q:C_ONES_Q + 1]
            return 0

        lax.fori_loop(0, seq // TK, k_loop, 0)

        def finish(qi, _):
            rows = pl.ds(pl.multiple_of(qi * TQ, TQ), TQ)
            dqk_ref[0, rows, :] = _heads_cat([dqt_acc[hh, qi].T for hh in range(NH)]).astype(BF16)
            for hh in range(NH):
                dcq_ref[hh, pl.ds(qi, 1), :] = dqt_acc[hh, qi, C_ONES_K:C_ONES_K + 1, :]
            return 0

        lax.fori_loop(0, n_q, finish, 0)

    return pl.pallas_call(
        body, name="fox_bwd", grid=(batch, N_HEADS // NH),
        out_shape=(jax.ShapeDtypeStruct((2, batch * seq, N_HEADS * HEAD_SLOT), BF16),
                   jax.ShapeDtypeStruct((batch * seq, D_BRANCH), BF16),
                   jax.ShapeDtypeStruct((batch, N_HEADS, seq // TQ, TQ), F32),
                   jax.ShapeDtypeStruct((batch, N_HEADS // NH, seq, LANES), F32)),
        in_specs=[_slot_spec(seq), _slot_spec(seq), _tblock_spec(seq), _group_spec(seq), _group_spec(seq),
                  _group_spec(seq), _qrow_spec(seq)],
        out_specs=(_slot2_spec(seq), _group_spec(seq), _qrow_spec(seq), _stat_spec(seq)),
        scratch_shapes=[pltpu.VMEM((NH, n_q, TQ), F32), pltpu.VMEM((NH, n_q, HEAD_SLOT, TQ), F32),
                        pltpu.VMEM((NH, TK, HEAD_SLOT), F32), pltpu.VMEM((NH, TK, HEAD_DIM), F32)],
        compiler_params=_attn_params(),
    )(qf, kf, kft, vf, o, do, lse)


def _sb_fwd(qkvb, vbt, batch, seq):
    def body(q_ref, k_ref, vt_ref, o_ref, lt_ref, run_s, acc_s):
        strict = _key_query_mask(lambda r, c: r < c)
        later = _tri(TK, lambda r, c: c > r)

        def tile(q0, kj, masked):
            krows = pl.ds(pl.multiple_of(kj * TK, TK), TK)
            for hh in range(NH):
                zt = _dot_nt(k_ref[krows, _hcols(hh)], q_ref[pl.ds(q0, TQ), _hcols(hh)])
                lg = -_softplus(zt)
                if masked:
                    lg = jnp.where(strict, lg, 0.0)
                hi, lo = _split2(lg)
                at = jnp.exp(zt + lg + run_s[hh] + _dot(later, hi) + _dot(later, lo))
                if masked:
                    at = jnp.where(strict, at, 0.0)
                acc_s[hh] += _dot(vt_ref[kj, _hslot(hh), :], at.astype(BF16))
                run_s[hh] += jnp.sum(lg, axis=0, keepdims=True)

        def q_loop(qi, _):
            q0 = pl.multiple_of(qi * TQ, TQ)
            run_s[...] = jnp.zeros_like(run_s)
            acc_s[...] = jnp.zeros_like(acc_s)
            tile(q0, qi, True)

            def k_loop(kk, _):
                tile(q0, qi - 1 - kk, False)
                return 0

            lax.fori_loop(0, qi, k_loop, 0)
            o_ref[pl.ds(q0, TQ), :] = _heads_cat([_untranspose(acc_s[hh]) for hh in range(NH)]).astype(BF16)
            for hh in range(NH):
                lt_ref[hh, pl.ds(qi, 1), :] = run_s[hh]
            return 0

        lax.fori_loop(0, seq // TQ, q_loop, 0)

    return pl.pallas_call(
        body, name="sb_fwd", grid=(batch, N_HEADS // NH),
        out_shape=(jax.ShapeDtypeStruct((batch * seq, D_BRANCH), BF16),
                   jax.ShapeDtypeStruct((batch, N_HEADS, seq // TQ, TQ), F32)),
        in_specs=[_group3_spec(0, seq), _group3_spec(1, seq), _tblock_spec(seq)],
        out_specs=(_group_spec(seq), _qrow_spec(seq)),
        scratch_shapes=[pltpu.VMEM((NH, 1, TQ), F32), pltpu.VMEM((NH, HEAD_SLOT, TQ), F32)],
        compiler_params=_attn_params(),
    )(qkvb, qkvb, vbt)


def _sb_bwd(qkvb, kbt, do, ltot, batch, seq):
    def body(q_ref, k_ref, v_ref, kt_ref, do_ref, lt_ref, dqkv_ref, dk_acc, dv_acc, ls_s, gs_s, dqt_s):
        strict = _key_query_mask(lambda r, c: r < c)
        upto = _tri(TK, lambda r, c: c <= r)
        before = _tri(TK, lambda r, c: c < r)
        dk_acc[...] = jnp.zeros_like(dk_acc)
        dv_acc[...] = jnp.zeros_like(dv_acc)

        def tile(qi, kj, masked):
            rows = pl.ds(pl.multiple_of(qi * TQ, TQ), TQ)
            krows = pl.ds(pl.multiple_of(kj * TK, TK), TK)
            for hh in range(NH):
                q, dout = q_ref[rows, _hcols(hh)], do_ref[rows, _hcols(hh)]
                zt = _dot_nt(k_ref[krows, _hcols(hh)], q)
                lg = -_softplus(zt)
                if masked:
                    lg = jnp.where(strict, lg, 0.0)
                hi, lo = _split2(lg)
                at = jnp.exp(zt + lg + (lt_ref[hh, pl.ds(qi, 1), :] - ls_s[hh]) - _dot(upto, hi) - _dot(upto, lo))
                if masked:
                    at = jnp.where(strict, at, 0.0)
                gt = _dot_nt(v_ref[krows, _hcols(hh)], dout) * at
                u = gs_s[hh] + _dot(before, gt.astype(BF16))
                dzt = (jnp.exp(lg) * (gt + u) - u).astype(BF16)
                dk_acc[hh, krows, :] += _dot(dzt, q)
                dv_acc[hh, krows, :] += _dot(at.astype(BF16), dout)
                dqt_s[hh] += _dot(kt_ref[kj, _hslot(hh), :], dzt)
                ls_s[hh] += jnp.sum(lg, axis=0, keepdims=True)
                gs_s[hh] += jnp.sum(gt, axis=0, keepdims=True)

        def q_loop(qi, _):
            ls_s[...] = jnp.zeros_like(ls_s)
            gs_s[...] = jnp.zeros_like(gs_s)
            dqt_s[...] = jnp.zeros_like(dqt_s)

            def k_loop(kj, _):
                tile(qi, kj, False)
                return 0

            lax.fori_loop(0, qi, k_loop, 0)
            tile(qi, qi, True)
            dqkv_ref[0, pl.ds(pl.multiple_of(qi * TQ, TQ), TQ), :] = _heads_cat(
                [_untranspose(dqt_s[hh]) for hh in range(NH)]).astype(BF16)
            return 0

        lax.fori_loop(0, seq // TQ, q_loop, 0)
        dqkv_ref[1] = _heads_cat([dk_acc[hh] for hh in range(NH)]).astype(BF16)
        dqkv_ref[2] = _heads_cat([dv_acc[hh] for hh in range(NH)]).astype(BF16)

    return pl.pallas_call(
        body, name="sb_bwd", grid=(batch, N_HEADS // NH),
        out_shape=jax.ShapeDtypeStruct((3, batch * seq, D_BRANCH), BF16),
        in_specs=[_group3_spec(0, seq), _group3_spec(1, seq), _group3_spec(2, seq), _tblock_spec(seq),
                  _group_spec(seq), _qrow_spec(seq)],
        out_specs=pl.BlockSpec((3, seq, NH * HEAD_DIM), lambda b, g: (0, b, g)),
        scratch_shapes=[pltpu.VMEM((NH, seq, HEAD_DIM), F32), pltpu.VMEM((NH, seq, HEAD_DIM), F32),
                        pltpu.VMEM((NH, 1, TQ), F32), pltpu.VMEM((NH, 1, TQ), F32),
                        pltpu.VMEM((NH, HEAD_SLOT, TQ), F32)],
        compiler_params=_attn_params(),
    )(qkvb, qkvb, qkvb, kbt, do, ltot)


def _forget_bwd(dcq_tok, dck_tok, fpre, batch, seq):
    t_len = batch * seq
    tiles = seq // TM

    def rev(i):
        return ((i // tiles) * tiles + (tiles - 1 - i % tiles), 0)

    def body(dcq_ref, dck_ref, f_ref, df_ref, db_ref, carry_ref):
        i = pl.program_id(0)

        @pl.when(i == 0)
        def _():
            db_ref[...] = jnp.zeros_like(db_ref)

        @pl.when(i % tiles == 0)
        def _():
            carry_ref[...] = jnp.zeros_like(carry_ref)

        dc = dcq_ref[...] - dck_ref[...]
        upper = _tri(TM, lambda r, c: c >= r)
        hi, mid, lo = _split3(dc)
        dlogf = carry_ref[...] + _dot(upper, hi) + _dot(upper, mid) + _dot(upper, lo)
        carry_ref[...] = carry_ref[...] + jnp.sum(dc, axis=0, keepdims=True)
        df = dlogf * _sigmoid(-f_ref[...])
        df_ref[...] = df.astype(BF16)
        db_ref[...] += jnp.sum(df, axis=0, keepdims=True)

    return pl.pallas_call(
        body, name="forget_bwd", grid=(t_len // TM,),
        out_shape=(jax.ShapeDtypeStruct((t_len, LANES), BF16), jax.ShapeDtypeStruct((1, LANES), F32)),
        in_specs=[pl.BlockSpec((TM, LANES), rev)] * 3,
        out_specs=(pl.BlockSpec((TM, LANES), rev), _acc_spec((1, LANES))),
        scratch_shapes=[pltpu.VMEM((1, LANES), F32)],
        compiler_params=_seq_params(),
    )(dcq_tok, dck_tok, fpre)


def _mix_fwd(o_fox, o_sb, gl, x, w_bf, w_bs, w_out, b_gate):
    t_len, d = x.shape

    def body(of_ref, os_ref, gl_ref, x_ref, wbf_ref, wbs_ref, wo_ref, bg_ref, x1_ref):
        br_f = _dot(of_ref[...], wbf_ref[...])
        br_s = _dot(os_ref[...], wbs_ref[...])
        ga = _sigmoid(gl_ref[:, :d].astype(F32) + bg_ref[0:1, :])
        gb = _sigmoid(gl_ref[:, d:].astype(F32) + bg_ref[1:2, :])
        merged = ga * br_f + gb * br_s
        x1_ref[...] = x_ref[...] + _dot(merged.astype(BF16), wo_ref[...])

    return pl.pallas_call(
        body, name="mix_fwd", grid=(t_len // TM,),
        out_shape=jax.ShapeDtypeStruct((t_len, d), F32),
        in_specs=[_row_spec(TM, D_BRANCH), _row_spec(TM, D_BRANCH), _row_spec(TM, 2 * d), _row_spec(TM, d),
                  _const_spec(w_bf.shape), _const_spec(w_bs.shape), _const_spec(w_out.shape), _const_spec(b_gate.shape)],
        out_specs=_row_spec(TM, d),
        compiler_params=_seq_params(),
    )(o_fox, o_sb, gl, x, w_bf, w_bs, w_out, b_gate)


def _ff_chunk(d_ff):
    return min(d_ff, 1024)


def _mlp_fwd(x1, g_mlp, w_up, w_down):
    t_len, d = x1.shape
    d_ff = w_up.shape[1]
    ch = _ff_chunk(d_ff)

    def body(x1_ref, g_ref, wu_ref, wd_ref, a_ref, x2_ref):
        x1v = x1_ref[...]
        xn, _ = _rms(x1v)
        h = (xn * g_ref[...]).astype(BF16)
        acc = x1v
        for j in range(d_ff // ch):
            a = _dot(h, wu_ref[:, j * ch:(j + 1) * ch])
            a_ref[:, j * ch:(j + 1) * ch] = a.astype(BF16)
            acc = acc + _dot(jnp.square(jnp.maximum(a, 0.0)).astype(BF16), wd_ref[j * ch:(j + 1) * ch, :])
        x2_ref[...] = acc

    return pl.pallas_call(
        body, name="mlp_fwd", grid=(t_len // TM,),
        out_shape=(jax.ShapeDtypeStruct((t_len, d_ff), BF16), jax.ShapeDtypeStruct((t_len, d), F32)),
        in_specs=[_row_spec(TM, d), _const_spec((1, d)), _const_spec(w_up.shape), _const_spec(w_down.shape)],
        out_specs=(_row_spec(TM, d_ff), _row_spec(TM, d)),
        compiler_params=_seq_params(),
    )(x1, g_mlp, w_up, w_down)


def _head_fwd_bwd(x2, p, target, g_ple, g_final, w_pg, w_ple):
    t_len, d = x2.shape
    d_ple = p.shape[1]

    def body(x2_ref, p_ref, t_ref, gp_ref, gf_ref, wpg_ref, wple_ref,
             dx2_ref, h3_ref, dpre_ref, dpe_ref, loss_ref, dgp_ref, dgf_ref):
        @pl.when(pl.program_id(0) == 0)
        def _():
            loss_ref[...] = jnp.zeros_like(loss_ref)
            dgp_ref[...] = jnp.zeros_like(dgp_ref)
            dgf_ref[...] = jnp.zeros_like(dgf_ref)

        x2v = x2_ref[...]
        x2n, r3 = _rms(x2v)
        h3 = (x2n * gp_ref[...]).astype(BF16)
        h3_ref[...] = h3
        gate = _sigmoid(_dot(h3, wpg_ref[...]))
        pe = _dot(p_ref[...].astype(BF16), wple_ref[...])
        x3n, r4 = _rms(x2v + gate * pe)
        err = x3n * gf_ref[...] - t_ref[...]
        loss_ref[...] += jnp.full(loss_ref.shape, (0.5 / d) * jnp.sum(err * err), F32)
        dx3, dgf = _rms_bwd(err * (1.0 / d), x3n, r4, gf_ref[...])
        dgf_ref[...] += dgf
        dpe_ref[...] = (dx3 * gate).astype(BF16)
        dpre = (dx3 * pe * gate * (1.0 - gate)).astype(BF16)
        dpre_ref[...] = dpre
        dres, dgp = _rms_bwd(_dot_nt(dpre, wpg_ref[...]), x2n, r3, gp_ref[...])
        dgp_ref[...] += dgp
        dx2_ref[...] = dx3 + dres

    shp_b = jax.ShapeDtypeStruct((t_len, d), BF16)
    return pl.pallas_call(
        body, name="head_fwd_bwd", grid=(t_len // TM,),
        out_shape=(jax.ShapeDtypeStruct((t_len, d), F32), shp_b, shp_b, shp_b,
                   jax.ShapeDtypeStruct((1, LANES), F32), jax.ShapeDtypeStruct((1, d), F32),
                   jax.ShapeDtypeStruct((1, d), F32)),
        in_specs=[_row_spec(TM, d), _row_spec(TM, d_ple), _row_spec(TM, d), _const_spec((1, d)), _const_spec((1, d)),
                  _const_spec(w_pg.shape), _const_spec(w_ple.shape)],
        out_specs=(_row_spec(TM, d), _row_spec(TM, d), _row_spec(TM, d), _row_spec(TM, d),
                   _acc_spec((1, LANES)), _acc_spec((1, d)), _acc_spec((1, d))),
        compiler_params=_seq_params(),
    )(x2, p, target, g_ple, g_final, w_pg, w_ple)


def _mlp_bwd(dx2, a, x1, g_mlp, w_up, w_down):
    t_len, d = x1.shape
    d_ff = w_up.shape[1]
    ch = _ff_chunk(d_ff)

    def body(dx2_ref, a_ref, x1_ref, g_ref, wu_ref, wd_ref, dx1_ref, da_ref, h2_ref, dg_ref):
        @pl.when(pl.program_id(0) == 0)
        def _():
            dg_ref[...] = jnp.zeros_like(dg_ref)

        dx2v = dx2_ref[...]
        dx2b = dx2v.astype(BF16)
        xn, r = _rms(x1_ref[...])
        h2_ref[...] = (xn * g_ref[...]).astype(BF16)
        dh = jnp.zeros((TM, d), F32)
        for j in range(d_ff // ch):
            dact = _dot_nt(dx2b, wd_ref[j * ch:(j + 1) * ch, :])
            da = (dact * 2.0 * jnp.maximum(a_ref[:, j * ch:(j + 1) * ch].astype(F32), 0.0)).astype(BF16)
            da_ref[:, j * ch:(j + 1) * ch] = da
            dh = dh + _dot_nt(da, wu_ref[:, j * ch:(j + 1) * ch])
        dres, dg = _rms_bwd(dh, xn, r, g_ref[...])
        dg_ref[...] += dg
        dx1_ref[...] = dx2v + dres

    return pl.pallas_call(
        body, name="mlp_bwd", grid=(t_len // TM,),
        out_shape=(jax.ShapeDtypeStruct((t_len, d), F32), jax.ShapeDtypeStruct((t_len, d_ff), BF16),
                   jax.ShapeDtypeStruct((t_len, d), BF16), jax.ShapeDtypeStruct((1, d), F32)),
        in_specs=[_row_spec(TM, d), _row_spec(TM, d_ff), _row_spec(TM, d), _const_spec((1, d)),
                  _const_spec(w_up.shape), _const_spec(w_down.shape)],
        out_specs=(_row_spec(TM, d), _row_spec(TM, d_ff), _row_spec(TM, d), _acc_spec((1, d))),
        compiler_params=_seq_params(),
    )(dx2, a, x1, g_mlp, w_up, w_down)


def _mix_bwd(dx1, o_fox, o_sb, gl, w_bf, w_bs, w_out, b_gate):
    t_len, d = dx1.shape

    def body(dx1_ref, of_ref, os_ref, gl_ref, wbf_ref, wbs_ref, wo_ref, bg_ref,
             mg_ref, dbf_ref, dbs_ref, dgl_ref, dof_ref, dos_ref, dbg_ref):
        @pl.when(pl.program_id(0) == 0)
        def _():
            dbg_ref[...] = jnp.zeros_like(dbg_ref)

        dmerged = _dot_nt(dx1_ref[...].astype(BF16), wo_ref[...])
        br_f = _dot(of_ref[...], wbf_ref[...])
        br_s = _dot(os_ref[...], wbs_ref[...])
        ga = _sigmoid(gl_ref[:, :d].astype(F32) + bg_ref[0:1, :])
        gb = _sigmoid(gl_ref[:, d:].astype(F32) + bg_ref[1:2, :])
        mg_ref[...] = (ga * br_f + gb * br_s).astype(BF16)
        dbf = (dmerged * ga).astype(BF16)
        dbs = (dmerged * gb).astype(BF16)
        dbf_ref[...] = dbf
        dbs_ref[...] = dbs
        dla = dmerged * br_f * ga * (1.0 - ga)
        dlb = dmerged * br_s * gb * (1.0 - gb)
        dgl_ref[:, :d] = dla.astype(BF16)
        dgl_ref[:, d:] = dlb.astype(BF16)
        dbg_ref[0:1, :] += jnp.sum(dla, axis=0, keepdims=True)
        dbg_ref[1:2, :] += jnp.sum(dlb, axis=0, keepdims=True)
        dof_ref[...] = _dot_nt(dbf, wbf_ref[...]).astype(BF16)
        dos_ref[...] = _dot_nt(dbs, wbs_ref[...]).astype(BF16)

    shp_d = jax.ShapeDtypeStruct((t_len, d), BF16)
    shp_h = jax.ShapeDtypeStruct((t_len, D_BRANCH), BF16)
    return pl.pallas_call(
        body, name="mix_bwd", grid=(t_len // TM,),
        out_shape=(shp_d, shp_d, shp_d, jax.ShapeDtypeStruct((t_len, 2 * d), BF16), shp_h, shp_h,
                   jax.ShapeDtypeStruct((2, d), F32)),
        in_specs=[_row_spec(TM, d), _row_spec(TM, D_BRANCH), _row_spec(TM, D_BRANCH), _row_spec(TM, 2 * d),
                  _const_spec(w_bf.shape), _const_spec(w_bs.shape), _const_spec(w_out.shape), _const_spec(b_gate.shape)],
        out_specs=(_row_spec(TM, d), _row_spec(TM, d), _row_spec(TM, d), _row_spec(TM, 2 * d),
                   _row_spec(TM, D_BRANCH), _row_spec(TM, D_BRANCH), _acc_spec((2, d))),
        compiler_params=_seq_params(),
    )(dx1, o_fox, o_sb, gl, w_bf, w_bs, w_out, b_gate)


def _inproj_bwd(dqk_f, dv_f, dqkv_b, dgl, df, dx1, x, g_mix, w_pad):
    t_len, d = x.shape
    lay, _ = _pad_layout(d)
    slot_w = N_HEADS * HEAD_SLOT

    def body(dqk_ref, dvf_ref, db_ref, dgl_ref, df_ref, dx1_ref, x_ref, g_ref, w_ref, dx_ref, h1_ref, dg_ref):
        @pl.when(pl.program_id(0) == 0)
        def _():
            dg_ref[...] = jnp.zeros_like(dg_ref)

        def back(piece, name):
            lo, hi = lay[name]
            return _dot_nt(piece, w_ref[:, lo:hi])

        xn, r = _rms(x_ref[...])
        h1_ref[...] = (xn * g_ref[...]).astype(BF16)
        dh = (back(df_ref[...], "forget") + back(dgl_ref[...], "gates") + back(dqk_ref[0], "qf")
              + back(dqk_ref[1], "kf") + back(dvf_ref[...], "vf") + back(db_ref[0], "qb") + back(db_ref[1], "kb")
              + back(db_ref[2], "vb"))
        dres, dg = _rms_bwd(dh, xn, r, g_ref[...])
        dg_ref[...] += dg
        dx_ref[...] = dx1_ref[...] + dres

    return pl.pallas_call(
        body, name="inproj_bwd", grid=(t_len // TM,),
        out_shape=(jax.ShapeDtypeStruct((t_len, d), F32), jax.ShapeDtypeStruct((t_len, d), BF16),
                   jax.ShapeDtypeStruct((1, d), F32)),
        in_specs=[_row3_spec(2, TM, slot_w), _row_spec(TM, D_BRANCH), _row3_spec(3, TM, D_BRANCH),
                  _row_spec(TM, 2 * d), _row_spec(TM, LANES), _row_spec(TM, d), _row_spec(TM, d), _const_spec((1, d)),
                  _const_spec(w_pad.shape)],
        out_specs=(_row_spec(TM, d), _row_spec(TM, d), _acc_spec((1, d))),
        compiler_params=_seq_params(),
    )(dqk_f, dv_f, dqkv_b, dgl, df, dx1, x, g_mix, w_pad)


def _cols_to_slabs(full):
    r, c8 = full.shape
    return full.reshape(r, N_DEV, c8 // N_DEV).transpose(1, 0, 2)


def _slabs_to_cols(slabs):
    n, r, c = slabs.shape
    return slabs.transpose(1, 0, 2).reshape(r, n * c)


def _win_sizes(d):
    return (D_BRANCH, D_BRANCH, D_BRANCH, N_HEADS, D_BRANCH, D_BRANCH, D_BRANCH, d, d)


def _split_win(w, d):
    out, off = [], 0
    for s in _win_sizes(d):
        out.append(w[:, off:off + s])
        off += s
    return out


def _to_slots(w):
    r = w.shape[0]
    return jnp.pad(w.reshape(r, N_HEADS, HEAD_DIM), ((0, 0), (0, 0), (0, HEAD_SLOT - HEAD_DIM))).reshape(r, -1)


def _from_slots(w):
    r = w.shape[0]
    return w.reshape(r, N_HEADS, HEAD_SLOT)[:, :, :HEAD_DIM].reshape(r, N_HEADS * HEAD_DIM)


def _pad_win(w_full, d):
    qa, ka, va, fa, qb, kb, vb, ga, gb = _split_win(w_full, d)
    scale = HEAD_DIM ** -0.5
    fpad = jnp.pad(fa, ((0, 0), (0, LANES - N_HEADS)))
    return jnp.concatenate([_to_slots(qa * scale), _to_slots(ka), va, qb * scale, kb, vb, ga, gb, fpad], axis=1)


def _unpad_dwin(dqk_f, dv_f, dqkv_b, dgates, dforget, d):
    scale = HEAD_DIM ** -0.5
    return jnp.concatenate([_from_slots(dqk_f[0]) * scale, _from_slots(dqk_f[1]), dv_f, dforget[:, :N_HEADS],
                            dqkv_b[0] * scale, dqkv_b[1], dqkv_b[2], dgates], axis=1)


def _c_lane_constants():
    head = jnp.arange(LANES)[:, None]
    lane = jnp.arange(N_HEADS * HEAD_SLOT)[None, :]
    in_head = (lane // HEAD_SLOT == head) & (head < N_HEADS)

    def place(first):
        return jnp.stack([(in_head & (lane % HEAD_SLOT == first + j)) for j in range(3)]).astype(BF16)

    def ones(first):
        off = lane % HEAD_SLOT
        return ((off >= first) & (off < first + 3)).astype(F32)

    return place(C_TERMS_Q), place(C_TERMS_K), ones(C_ONES_Q), ones(C_ONES_K)


def _pad_rows(a, rows):
    return jnp.pad(a, [(0, 0)] * (a.ndim - 2) + [(0, rows - a.shape[-2]), (0, 0)])


def kernel(x, p, g_mix, w_in, b_forget, b_gate, w_branch_fox, w_branch_sb, w_out, g_mlp, w_up, w_down, g_ple, w_ple_gate, w_ple, g_final, loss_target, m_g_mix, m_w_in, m_b_forget, m_b_gate, m_w_branch_fox, m_w_branch_sb, m_w_out, m_g_mlp, m_w_up, m_w_down, m_g_ple, m_w_ple_gate, m_w_ple, m_g_final, v_g_mix, v_w_in, v_b_forget, v_b_gate, v_w_branch_fox, v_w_branch_sb, v_w_out, v_g_mlp, v_w_up, v_w_down, v_g_ple, v_w_ple_gate, v_w_ple, v_g_final):
    batch, seq, d = x.shape
    t_len = batch * seq
    d_ple = p.shape[-1]
    d_ff = w_up.shape[-1] * N_DEV
    dn = d // N_DEV
    fn = d_ff // N_DEV
    my_c = lax.axis_index("c")
    my_dev = 4 * lax.axis_index("x") + 2 * lax.axis_index("y") + my_c

    bg_hi = b_gate[0].astype(BF16)
    bg_r = b_gate[0] - bg_hi.astype(F32)
    bg_mid = bg_r.astype(BF16)
    bg_lo = (bg_r - bg_mid.astype(F32)).astype(BF16)
    narrow_rows = 2 * D_BRANCH + d_ple + 6
    narrow_rows_pad = -(-narrow_rows // 16) * 16
    wide = jnp.concatenate([w_out[0], w_down[0], w_ple_gate[0]], axis=0).astype(BF16)
    narrow = _pad_rows(jnp.concatenate(
        [w_branch_fox[0].astype(BF16), w_branch_sb[0].astype(BF16), w_ple[0].astype(BF16), bg_hi, bg_mid, bg_lo],
        axis=0), narrow_rows_pad)
    g_in, g_up, g_wide, g_narrow = _all_gather([w_in[0].astype(BF16), w_up[0].astype(BF16), wide, narrow])
    w_pad = _pad_win(_slabs_to_cols(g_in), d)
    w_up_full = _slabs_to_cols(g_up)
    w_out_full = g_wide[:, :dn].reshape(d, d)
    w_down_full = g_wide[:, dn:dn + fn].reshape(d_ff, d)
    w_pg_full = g_wide[:, dn + fn:].reshape(d, d)
    w_bf_full = _slabs_to_cols(g_narrow[:, :D_BRANCH])
    w_bs_full = _slabs_to_cols(g_narrow[:, D_BRANCH:2 * D_BRANCH])
    w_ple_full = _slabs_to_cols(g_narrow[:, 2 * D_BRANCH:2 * D_BRANCH + d_ple])
    bg_terms = g_narrow[:, 2 * D_BRANCH + d_ple:narrow_rows].astype(F32)
    b_gate_full = _slabs_to_cols(bg_terms[:, 0:2] + bg_terms[:, 2:4] + bg_terms[:, 4:6])
    bf_pad = jnp.pad(b_forget, ((0, 0), (0, LANES - N_HEADS)))
    place_q, place_k, ones_q, ones_k = _c_lane_constants()

    x2d = x.reshape(t_len, d)
    p2d = p.reshape(t_len, d_ple)
    tgt2d = loss_target.reshape(t_len, d)
    qf, kf, kft, vf, vft, qkvb, kbt, vbt, gl, fpre = _inproj_fwd(
        x2d, g_mix, w_pad, bf_pad, place_q, place_k, ones_q, ones_k, seq)
    o_fox, lse = _fox_fwd(qf, kf, vft, batch, seq)
    o_sb, ltot = _sb_fwd(qkvb, vbt, batch, seq)
    x1 = _mix_fwd(o_fox, o_sb, gl, x2d, w_bf_full, w_bs_full, w_out_full, b_gate_full)
    a_up, x2 = _mlp_fwd(x1, g_mlp, w_up_full, w_down_full)

    dx2, h3, dpre, dpe, loss_acc, dg_ple, dg_final = _head_fwd_bwd(
        x2, p2d, tgt2d, g_ple, g_final.reshape(1, d), w_pg_full, w_ple_full)
    dx1, da_up, h2, dg_mlp = _mlp_bwd(dx2, a_up, x1, g_mlp, w_up_full, w_down_full)
    merged, dbr_f, dbr_s, dgl, do_fox, do_sb, dbg = _mix_bwd(
        dx1, o_fox, o_sb, gl, w_bf_full, w_bs_full, w_out_full, b_gate_full)
    dqk_f, dv_f, dc_queries, dc_keys = _fox_bwd(qf, kf, kft, vf, o_fox, do_fox, lse, batch, seq)
    dqkv_b = _sb_bwd(qkvb, kbt, do_sb, ltot, batch, seq)
    dcq_tok = dc_queries.reshape(batch, N_HEADS, seq).transpose(0, 2, 1).reshape(t_len, N_HEADS)
    dck_tok = dc_keys[..., :NH].transpose(0, 2, 1, 3).reshape(t_len, N_HEADS)
    lane_pad = ((0, 0), (0, LANES - N_HEADS))
    df, db_forget = _forget_bwd(jnp.pad(dcq_tok, lane_pad), jnp.pad(dck_tok, lane_pad), fpre, batch, seq)
    grad_x, h1, dg_mix = _inproj_bwd(dqk_f, dv_f, dqkv_b, dgl, df, dx1, x2d, g_mix, w_pad)

    gw_ple = _matmul_tn("dw_ple", p2d, dpe)
    gw_pg = _matmul_tn("dw_ple_gate", h3, dpre)
    gw_down = _matmul_tn("dw_down", a_up, dx2, relu2=True)
    gw_up = _matmul_tn("dw_up", h2, da_up)
    gw_out = _matmul_tn("dw_out", merged, dx1)
    gw_bf = _matmul_tn("dw_branch_fox", o_fox, dbr_f)
    gw_bs = _matmul_tn("dw_branch_sb", o_sb, dbr_s)
    gw_in = _unpad_dwin(_matmul_tn("dw_in_fox_qk", h1, dqk_f), _matmul_tn("dw_in_fox_v", h1, dv_f),
                        _matmul_tn("dw_in_sb", h1, dqkv_b), _matmul_tn("dw_in_gates", h1, dgl),
                        _matmul_tn("dw_in_forget", h1, df), d)

    part_in = _cols_to_slabs(gw_in).astype(BF16)
    part_up = _cols_to_slabs(gw_up).astype(BF16)
    part_wide = jnp.concatenate([gw_out.reshape(N_DEV, dn, d), gw_down.reshape(N_DEV, fn, d),
                                 gw_pg.reshape(N_DEV, dn, d)], axis=1).astype(BF16)
    part_narrow = _pad_rows(jnp.concatenate(
        [_cols_to_slabs(gw_bf), _cols_to_slabs(gw_bs), _cols_to_slabs(gw_ple)], axis=1).astype(BF16), narrow_rows_pad)
    partials = [part_in, part_up, part_wide, part_narrow]
    received = _rs_core_pair(partials)
    chip_sums = [_pair_add("pair_add_%d" % i, pt, rc, my_c) for i, (pt, rc) in enumerate(zip(partials, received))]
    s_in, s_up, s_wide, s_narrow = _rs_chips(chip_sums)

    small = jnp.concatenate([
        dg_mix, dg_mlp, dg_ple, dg_final, jnp.pad(db_forget[:, :N_HEADS], ((0, 0), (0, d - N_HEADS))), dbg,
        jnp.pad(loss_acc[:, :1], ((0, 0), (0, d - 1)))], axis=0)
    small = _all_reduce_small(small)
    loss = small[7, 0]
    small_grads = {
        "g_mix": small[0:1], "g_mlp": small[1:2], "g_ple": small[2:3], "g_final": small[3:4],
        "b_forget": small[4:5, :N_HEADS],
        "b_gate": lax.dynamic_slice_in_dim(small[5:7], my_dev * dn, dn, axis=1),
    }

    weights = {"g_mix": g_mix, "w_in": w_in, "b_forget": b_forget, "b_gate": b_gate, "w_branch_fox": w_branch_fox,
               "w_branch_sb": w_branch_sb, "w_out": w_out, "g_mlp": g_mlp, "w_up": w_up, "w_down": w_down,
               "g_ple": g_ple, "w_ple_gate": w_ple_gate, "w_ple": w_ple, "g_final": g_final}
    m_in = {"g_mix": m_g_mix, "w_in": m_w_in, "b_forget": m_b_forget, "b_gate": m_b_gate,
            "w_branch_fox": m_w_branch_fox, "w_branch_sb": m_w_branch_sb, "w_out": m_w_out, "g_mlp": m_g_mlp,
            "w_up": m_w_up, "w_down": m_w_down, "g_ple": m_g_ple, "w_ple_gate": m_w_ple_gate, "w_ple": m_w_ple,
            "g_final": m_g_final}
    v_in = {"g_mix": v_g_mix, "w_in": v_w_in, "b_forget": v_b_forget, "b_gate": v_b_gate,
            "w_branch_fox": v_w_branch_fox, "w_branch_sb": v_w_branch_sb, "w_out": v_w_out, "g_mlp": v_g_mlp,
            "w_up": v_w_up, "w_down": v_w_down, "g_ple": v_g_ple, "w_ple_gate": v_w_ple_gate, "w_ple": v_w_ple,
            "g_final": v_g_final}
    names = list(weights)

    def as2d(a):
        return a.reshape(-1, a.shape[-1])

    result = {}
    big = {"w_in": (s_in, 0), "w_up": (s_up, 0), "w_out": (s_wide, 0), "w_down": (s_wide, dn),
           "w_ple_gate": (s_wide, dn + fn), "w_branch_fox": (s_narrow, 0), "w_branch_sb": (s_narrow, D_BRANCH),
           "w_ple": (s_narrow, 2 * D_BRANCH)}
    for n, (parts, off) in big.items():
        result[n] = _adamw_parts("adamw_" + n, as2d(weights[n]), parts, off, as2d(m_in[n]), as2d(v_in[n]))
    small_names = list(small_grads)
    small_out = _adamw_small([(as2d(weights[n]), small_grads[n], as2d(m_in[n]), as2d(v_in[n])) for n in small_names])
    for n, (dlt, nm, nv) in zip(small_names, small_out):
        result[n] = (small_grads[n], dlt, nm, nv)
    outs = [[result[n][k].reshape(weights[n].shape) for n in names] for k in range(4)]
    return (loss, grad_x.reshape(x.shape), *outs[0], *outs[1], *outs[2], *outs[3])
```

```python
import jax
import jax.numpy as jnp
from jax import lax
from jax.experimental import pallas as pl
from jax.experimental.pallas import tpu as pltpu

F32 = jnp.float32
BF16 = jnp.bfloat16

HEAD_DIM = 64
N_HEADS = 8
D_BRANCH = N_HEADS * HEAD_DIM
EPS = 1e-6
ADAM_LR = 0.001
ADAM_B1 = 0.9
ADAM_B2 = 0.999
ADAM_EPS = 1e-08
ADAM_WD = 0.01
ADAM_STEP = 10

N_DEV = 8
LANES = 128
TM = 256
TQ = 256
TK = 256
NH = 4
HEAD_SLOT = 128
C_TERMS_Q = 64
C_ONES_K = 64
C_TERMS_K = 67
C_ONES_Q = 67
NEG = -1e30
VMEM_LIMIT = 56 * 1024 * 1024
MESH = pl.DeviceIdType.MESH


def _dot(a, b):
    return jnp.dot(a, b, preferred_element_type=F32)


def _dot_nt(a, b):
    return lax.dot_general(a, b, (((1,), (1,)), ((), ())), preferred_element_type=F32)


def _dot_tn(a, b):
    return lax.dot_general(a, b, (((0,), (0,)), ((), ())), preferred_element_type=F32)


def _sigmoid(x):
    return 1.0 / (1.0 + jnp.exp(-x))


def _softplus(x):
    return jnp.maximum(x, 0.0) + jnp.log(1.0 + jnp.exp(-jnp.abs(x)))


def _split2(x):
    hi = x.astype(BF16)
    lo = (x - hi.astype(F32)).astype(BF16)
    return hi, lo


def _split3(x):
    hi = x.astype(BF16)
    r = x - hi.astype(F32)
    mid = r.astype(BF16)
    lo = (r - mid.astype(F32)).astype(BF16)
    return hi, mid, lo


def _rows_dot_mask(x, mask_bf16):
    hi, lo = _split2(x)
    return _dot(hi, mask_bf16) + _dot(lo, mask_bf16)


def _tri(n, rel):
    r = lax.broadcasted_iota(jnp.int32, (n, n), 0)
    c = lax.broadcasted_iota(jnp.int32, (n, n), 1)
    return rel(r, c).astype(BF16)


def _rms(x):
    r = lax.rsqrt(jnp.mean(x * x, axis=-1, keepdims=True) + EPS)
    return x * r, r


def _rms_bwd(dh, xn, r, g):
    dxn = dh * g
    dx = r * (dxn - xn * jnp.mean(dxn * xn, axis=-1, keepdims=True))
    return dx, jnp.sum(dh * xn, axis=0, keepdims=True)


def _row_spec(tm, cols):
    return pl.BlockSpec((tm, cols), lambda i: (i, 0))


def _row3_spec(g, tm, cols):
    return pl.BlockSpec((g, tm, cols), lambda i: (0, i, 0))


def _const_spec(shape):
    nd = len(shape)
    return pl.BlockSpec(shape, lambda i: (0,) * nd, pipeline_mode=pl.Buffered(1))


def _acc_spec(shape):
    nd = len(shape)
    return pl.BlockSpec(shape, lambda i: (0,) * nd)


def _seq_params():
    return pltpu.CompilerParams(dimension_semantics=("arbitrary",), vmem_limit_bytes=VMEM_LIMIT)


def _mesh_pos():
    return lax.axis_index("x"), lax.axis_index("y"), lax.axis_index("c")


def _other_chips(x, y):
    return [(1 - x, y), (x, 1 - y), (1 - x, 1 - y)]


def _hbm_specs(n):
    return [pl.BlockSpec(memory_space=pl.ANY)] * n


def _all_gather(shards):
    n = len(shards)

    def body(*refs):
        x_refs, out_refs = refs[:n], refs[n:2 * n]
        send_sems, recv_sems, local_sems = refs[2 * n:]
        x, y, c = _mesh_pos()
        me, sibling = (x, y, c), (x, y, 1 - c)
        chips = _other_chips(x, y)

        def index(px, py, pc):
            return 4 * px + 2 * py + pc

        def copy(a, k, block, to, src=None):
            slab = out_refs[a].at[index(*block)]
            return pltpu.make_async_remote_copy(
                src_ref=slab if src is None else src, dst_ref=slab,
                send_sem=send_sems.at[7 * a + k], recv_sem=recv_sems.at[7 * a + k], device_id=to, device_id_type=MESH)

        mine, first, passed = [], [], []
        for a in range(n):
            mine.append(pltpu.make_async_copy(x_refs[a], out_refs[a].at[index(*me)], local_sems.at[a]))
            mine[-1].start()
            first.append(copy(a, 0, me, sibling, src=x_refs[a]))
            first += [copy(a, 1 + j, me, (cx, cy, c), src=x_refs[a]) for j, (cx, cy) in enumerate(chips)]
        for cp in first:
            cp.start()
        for j, (cx, cy) in enumerate(chips):
            for a in range(n):
                copy(a, 1 + j, (cx, cy, c), me).wait_recv()
                passed.append(copy(a, 4 + j, (cx, cy, c), sibling))
                passed[-1].start()
        for a in range(n):
            copy(a, 0, sibling, me).wait_recv()
            for j, (cx, cy) in enumerate(chips):
                copy(a, 4 + j, (cx, cy, 1 - c), me).wait_recv()
        for cp in first + passed:
            cp.wait_send()
        for cp in mine:
            cp.wait()

    return pl.pallas_call(
        body, name="all_gather_weights",
        out_shape=[jax.ShapeDtypeStruct((N_DEV,) + s.shape, s.dtype) for s in shards],
        in_specs=_hbm_specs(n), out_specs=_hbm_specs(n),
        scratch_shapes=[pltpu.SemaphoreType.DMA((7 * n,)), pltpu.SemaphoreType.DMA((7 * n,)),
                        pltpu.SemaphoreType.DMA((n,))],
    )(*shards)


def _rs_core_pair(partials):
    n = len(partials)

    def body(*refs):
        p_refs, recv_refs = refs[:n], refs[n:2 * n]
        send_sems, recv_sems = refs[2 * n:]
        x, y, c = _mesh_pos()
        for a in range(n):
            for chip in range(4):
                pltpu.make_async_remote_copy(
                    src_ref=p_refs[a].at[2 * chip + (1 - c)], dst_ref=recv_refs[a].at[chip],
                    send_sem=send_sems.at[a], recv_sem=recv_sems.at[a],
                    device_id=(x, y, 1 - c), device_id_type=MESH).start()
        for a in range(n):
            pltpu.make_async_remote_copy(
                src_ref=recv_refs[a], dst_ref=recv_refs[a], send_sem=send_sems.at[a], recv_sem=recv_sems.at[a],
                device_id=(x, y, 1 - c), device_id_type=MESH).wait()

    return pl.pallas_call(
        body, name="reduce_scatter_core_pair",
        out_shape=[jax.ShapeDtypeStruct((4,) + s.shape[1:], s.dtype) for s in partials],
        in_specs=_hbm_specs(n), out_specs=_hbm_specs(n),
        scratch_shapes=[pltpu.SemaphoreType.DMA((n,)), pltpu.SemaphoreType.DMA((n,))],
    )(*partials)


def _rs_chips(chip_sums):
    n = len(chip_sums)

    def body(*refs):
        cs_refs, out_refs = refs[:n], refs[n:2 * n]
        send_sems, recv_sems, local_sems = refs[2 * n:]
        x, y, c = _mesh_pos()
        chip = 2 * x + y
        chips = _other_chips(x, y)
        mine, sends = [], []
        for a in range(n):
            mine.append(pltpu.make_async_copy(cs_refs[a].at[chip], out_refs[a].at[chip], local_sems.at[a]))
            mine[-1].start()
            for j, (cx, cy) in enumerate(chips):
                sends.append(pltpu.make_async_remote_copy(
                    src_ref=cs_refs[a].at[2 * cx + cy], dst_ref=out_refs[a].at[chip],
                    send_sem=send_sems.at[3 * a + j], recv_sem=recv_sems.at[3 * a + j],
                    device_id=(cx, cy, c), device_id_type=MESH))
                sends[-1].start()
        for a in range(n):
            for j, (cx, cy) in enumerate(chips):
                pltpu.make_async_remote_copy(
                    src_ref=cs_refs[a].at[chip], dst_ref=out_refs[a].at[2 * cx + cy],
                    send_sem=send_sems.at[3 * a + j], recv_sem=recv_sems.at[3 * a + j],
                    device_id=(x, y, c), device_id_type=MESH).wait_recv()
        for cp in sends:
            cp.wait_send()
        for cp in mine:
            cp.wait()

    return pl.pallas_call(
        body, name="reduce_scatter_chips",
        out_shape=[jax.ShapeDtypeStruct(s.shape, s.dtype) for s in chip_sums],
        in_specs=_hbm_specs(n), out_specs=_hbm_specs(n),
        scratch_shapes=[pltpu.SemaphoreType.DMA((3 * n,)), pltpu.SemaphoreType.DMA((3 * n,)),
                        pltpu.SemaphoreType.DMA((n,))],
    )(*chip_sums)


def _all_reduce_small(vec):
    rows, cols = vec.shape

    def body(x_ref, land_ref, sum_ref, send_sems, recv_sems):
        x, y, c = _mesh_pos()
        me = 4 * x + 2 * y + c
        land_ref[me] = x_ref[...]
        flips = [(fx, fy, fc) for fx in (0, 1) for fy in (0, 1) for fc in (0, 1)][1:]

        def flipped(f):
            return tuple((1 - v) if b else v for v, b in zip((x, y, c), f))

        sends = []
        for k, f in enumerate(flips):
            sends.append(pltpu.make_async_remote_copy(
                src_ref=x_ref, dst_ref=land_ref.at[me], send_sem=send_sems.at[k], recv_sem=recv_sems.at[k],
                device_id=flipped(f), device_id_type=MESH))
            sends[-1].start()
        for k, f in enumerate(flips):
            px, py, pc = flipped(f)
            pltpu.make_async_remote_copy(
                src_ref=x_ref, dst_ref=land_ref.at[4 * px + 2 * py + pc], send_sem=send_sems.at[k],
                recv_sem=recv_sems.at[k], device_id=(x, y, c), device_id_type=MESH).wait_recv()
        for cp in sends:
            cp.wait_send()
        total = land_ref[0]
        for d in range(1, N_DEV):
            total = total + land_ref[d]
        sum_ref[...] = total

    vm = pl.BlockSpec(memory_space=pltpu.VMEM)
    return pl.pallas_call(
        body, name="all_reduce_small",
        out_shape=(jax.ShapeDtypeStruct((N_DEV, rows, cols), F32), jax.ShapeDtypeStruct((rows, cols), F32)),
        in_specs=[vm], out_specs=(vm, vm),
        scratch_shapes=[pltpu.SemaphoreType.DMA((7,)), pltpu.SemaphoreType.DMA((7,))],
    )(vec)[1]


def _block_rows(rows, cols, itemsize, align, row_off=0):
    best = None
    for t in range(align, rows + 1, align):
        if rows % t == 0 and row_off % t == 0 and t * cols * itemsize <= (1 << 20):
            best = t
    return rows if best is None else best


def _pair_add(name, partial, recv, my_c):
    _, rows, cols = partial.shape
    br = _block_rows(rows, cols, 2, 16)

    def body(c_ref, a_ref, b_ref, o_ref):
        o_ref[...] = (a_ref[...].astype(F32) + b_ref[...].astype(F32)).astype(BF16)

    return pl.pallas_call(
        body, name=name,
        grid_spec=pltpu.PrefetchScalarGridSpec(
            num_scalar_prefetch=1, grid=(4, rows // br),
            in_specs=[pl.BlockSpec((None, None, br, cols), lambda j, i, c_ref: (j, c_ref[0], i, 0)),
                      pl.BlockSpec((None, br, cols), lambda j, i, c_ref: (j, i, 0))],
            out_specs=pl.BlockSpec((None, br, cols), lambda j, i, c_ref: (j, i, 0))),
        out_shape=jax.ShapeDtypeStruct((4, rows, cols), BF16),
    )(my_c.reshape(1).astype(jnp.int32), partial.reshape(4, 2, rows, cols), recv)


def _adam_update(w, g, m, v):
    nm = ADAM_B1 * m + (1.0 - ADAM_B1) * g
    nv = ADAM_B2 * v + (1.0 - ADAM_B2) * (g * g)
    m_hat = nm / (1.0 - ADAM_B1 ** ADAM_STEP)
    v_hat = nv / (1.0 - ADAM_B2 ** ADAM_STEP)
    return -ADAM_LR * (m_hat / (jnp.sqrt(v_hat) + ADAM_EPS) + ADAM_WD * w), nm, nv


def _adamw_parts(name, w, parts, row_off, m, v):
    rows, cols = w.shape
    tr = _block_rows(rows, cols, 4, 16, row_off)
    assert rows % tr == 0 and row_off % tr == 0
    off = row_off // tr

    def body(w_ref, p_ref, m_ref, v_ref, g_ref, d_ref, nm_ref, nv_ref):
        g = p_ref[0].astype(F32)
        for j in range(1, 4):
            g = g + p_ref[j].astype(F32)
        g_ref[...] = g
        d_ref[...], nm_ref[...], nv_ref[...] = _adam_update(w_ref[...], g, m_ref[...], v_ref[...])

    spec = pl.BlockSpec((tr, cols), lambda i: (i, 0))
    shp = jax.ShapeDtypeStruct((rows, cols), F32)
    return pl.pallas_call(
        body, name=name, grid=(rows // tr,), out_shape=(shp,) * 4,
        in_specs=[spec, pl.BlockSpec((4, tr, cols), lambda i: (0, off + i, 0)), spec, spec], out_specs=(spec,) * 4,
    )(w, parts, m, v)


def _adamw_small(tensors):
    n = len(tensors)

    def body(*refs):
        ins, outs = refs[:4 * n], refs[4 * n:]
        for t in range(n):
            w_ref, g_ref, m_ref, v_ref = ins[4 * t:4 * t + 4]
            d, nm, nv = _adam_update(w_ref[...], g_ref[...], m_ref[...], v_ref[...])
            outs[3 * t][...], outs[3 * t + 1][...], outs[3 * t + 2][...] = d, nm, nv

    vm = pl.BlockSpec(memory_space=pltpu.VMEM)
    out = pl.pallas_call(
        body, name="adamw_small",
        out_shape=[jax.ShapeDtypeStruct(t[0].shape, F32) for t in tensors for _ in range(3)],
        in_specs=[vm] * (4 * n), out_specs=[vm] * (3 * n),
    )(*[a for t in tensors for a in t])
    return [tuple(out[3 * t:3 * t + 3]) for t in range(n)]


def _matmul_tn(name, a, b, relu2=False):
    squeeze = b.ndim == 2
    if squeeze:
        b = b[None]
    t_len, k_len = a.shape
    groups, _, n_len = b.shape
    tt = min(t_len, 512)
    tk = min(k_len, 512)
    tn = min(n_len, 1024)
    nt = t_len // tt

    def body(a_ref, b_ref, o_ref):
        @pl.when(pl.program_id(3) == 0)
        def _():
            o_ref[...] = jnp.zeros_like(o_ref)

        av = a_ref[...]
        if relu2:
            av = jnp.square(jnp.maximum(av.astype(F32), 0.0))
        o_ref[...] += _dot_tn(av.astype(BF16), b_ref[...].astype(BF16))

    out = pl.pallas_call(
        body, name=name, grid=(groups, k_len // tk, n_len // tn, nt),
        out_shape=jax.ShapeDtypeStruct((groups, k_len, n_len), F32),
        in_specs=[pl.BlockSpec((tt, tk), lambda g, i, j, t: (t, i)),
                  pl.BlockSpec((None, tt, tn), lambda g, i, j, t: (g, t, j))],
        out_specs=pl.BlockSpec((None, tk, tn), lambda g, i, j, t: (g, i, j)),
        compiler_params=pltpu.CompilerParams(
            dimension_semantics=("parallel", "parallel", "parallel", "arbitrary"), vmem_limit_bytes=VMEM_LIMIT),
    )(a, b)
    return out[0] if squeeze else out


def _pad_layout(d):
    names = ("qf", "kf", "vf", "qb", "kb", "vb", "gates", "forget")
    sizes = (N_HEADS * HEAD_SLOT, N_HEADS * HEAD_SLOT, D_BRANCH, D_BRANCH, D_BRANCH, D_BRANCH, 2 * d, LANES)
    out, off = {}, 0
    for n, s in zip(names, sizes):
        out[n] = (off, off + s)
        off += s
    return out, off


def _slot_rows(xt, extra):
    parts = []
    for h in range(N_HEADS):
        parts += [xt[h * HEAD_DIM:(h + 1) * HEAD_DIM, :], extra]
    return jnp.concatenate(parts, axis=0)


def _inproj_fwd(x, g_mix, w_pad, bf_pad, place_q, place_k, ones_q, ones_k, seq):
    t_len, d = x.shape
    lay, _ = _pad_layout(d)
    tiles_per_seq = seq // TM
    slot_w = N_HEADS * HEAD_SLOT

    def body(x_ref, g_ref, w_ref, bf_ref, pq_ref, pk_ref, oq_ref, ok_ref,
             qf_ref, kf_ref, kft_ref, vf_ref, vft_ref, qkvb_ref, kbt_ref, vbt_ref, gl_ref, fpre_ref, carry_ref):
        @pl.when(pl.program_id(0) % tiles_per_seq == 0)
        def _():
            carry_ref[...] = jnp.zeros_like(carry_ref)

        def proj(name):
            lo, hi = lay[name]
            return _dot(h, w_ref[:, lo:hi])

        xn, _ = _rms(x_ref[...])
        h = (xn * g_ref[...]).astype(BF16)
        fpre = proj("forget") + bf_ref[...]
        fpre_ref[...] = fpre
        logf = -_softplus(-fpre)
        lower = _tri(TM, lambda r, c: c <= r)
        hi, mid, lo = _split3(logf)
        c_val = carry_ref[...] + _dot(lower, hi) + _dot(lower, mid) + _dot(lower, lo)
        carry_ref[...] = carry_ref[...] + jnp.sum(logf, axis=0, keepdims=True)
        c3 = _split3(c_val)
        qf_ref[...] = (proj("qf") + sum(_dot(c3[j], pq_ref[j]) for j in range(3)) + oq_ref[...]).astype(BF16)
        kf = proj("kf") - sum(_dot(c3[j], pk_ref[j]) for j in range(3)) + ok_ref[...]
        kf_ref[...] = kf.astype(BF16)
        kft_ref[0] = kf.T.astype(BF16)
        row0 = (lax.broadcasted_iota(jnp.int32, (HEAD_DIM, TM), 0) == 0).astype(F32)
        zeros = jnp.zeros((HEAD_DIM, TM), F32)
        vf = proj("vf")
        vf_ref[...] = vf.astype(BF16)
        vft_ref[0] = _slot_rows(vf.T, row0).astype(BF16)
        qkvb_ref[0] = proj("qb").astype(BF16)
        kb = proj("kb")
        qkvb_ref[1] = kb.astype(BF16)
        kbt_ref[0] = _slot_rows(kb.T, zeros).astype(BF16)
        vb = proj("vb")
        qkvb_ref[2] = vb.astype(BF16)
        vbt_ref[0] = _slot_rows(vb.T, row0).astype(BF16)
        gl_ref[...] = proj("gates").astype(BF16)

    n_tiles = t_len // TM
    slot_shape = jax.ShapeDtypeStruct((t_len, slot_w), BF16)
    t_shape = jax.ShapeDtypeStruct((n_tiles, slot_w, TM), BF16)
    t_spec = pl.BlockSpec((1, slot_w, TM), lambda i: (i, 0, 0))
    return pl.pallas_call(
        body, name="inproj_fwd", grid=(n_tiles,),
        out_shape=(slot_shape, slot_shape, t_shape, jax.ShapeDtypeStruct((t_len, D_BRANCH), BF16), t_shape,
                   jax.ShapeDtypeStruct((3, t_len, D_BRANCH), BF16), t_shape, t_shape,
                   jax.ShapeDtypeStruct((t_len, 2 * d), BF16), jax.ShapeDtypeStruct((t_len, LANES), F32)),
        in_specs=[_row_spec(TM, d), _const_spec((1, d)), _const_spec(w_pad.shape), _const_spec((1, LANES)),
                  _const_spec(place_q.shape), _const_spec(place_k.shape), _const_spec((1, slot_w)),
                  _const_spec((1, slot_w))],
        out_specs=(_row_spec(TM, slot_w), _row_spec(TM, slot_w), t_spec, _row_spec(TM, D_BRANCH), t_spec,
                   _row3_spec(3, TM, D_BRANCH), t_spec, t_spec, _row_spec(TM, 2 * d), _row_spec(TM, LANES)),
        scratch_shapes=[pltpu.VMEM((1, LANES), F32)],
        compiler_params=_seq_params(),
    )(x, g_mix, w_pad, bf_pad, place_q, place_k, ones_q, ones_k)


def _slot_spec(seq):
    return pl.BlockSpec((seq, NH * HEAD_SLOT), lambda b, g: (b, g))


def _slot2_spec(seq):
    return pl.BlockSpec((2, seq, NH * HEAD_SLOT), lambda b, g: (0, b, g))


def _group_spec(seq):
    return pl.BlockSpec((seq, NH * HEAD_DIM), lambda b, g: (b, g))


def _group3_spec(which, seq):
    return pl.BlockSpec((None, seq, NH * HEAD_DIM), lambda b, g: (which, b, g))


def _tblock_spec(seq):
    return pl.BlockSpec((seq // TK, NH * HEAD_SLOT, TK), lambda b, g: (b, g, 0))


def _qrow_spec(seq):
    return pl.BlockSpec((None, NH, seq // TQ, TQ), lambda b, g: (b, g, 0, 0))


def _stat_spec(seq):
    return pl.BlockSpec((None, None, seq, LANES), lambda b, g: (b, g, 0, 0))


def _attn_params():
    return pltpu.CompilerParams(dimension_semantics=("parallel", "parallel"), vmem_limit_bytes=VMEM_LIMIT)


def _hcols(hh):
    return slice(hh * HEAD_DIM, (hh + 1) * HEAD_DIM)


def _hslot(hh):
    return slice(hh * HEAD_SLOT, (hh + 1) * HEAD_SLOT)


def _lane(hh):
    return slice(hh, hh + 1)


def _key_query_mask(rel):
    r = lax.broadcasted_iota(jnp.int32, (TK, TQ), 0)
    c = lax.broadcasted_iota(jnp.int32, (TK, TQ), 1)
    return rel(r, c)


def _heads_cat(vals):
    return jnp.concatenate(vals, axis=1)


def _untranspose(acc_t):
    return acc_t.T[:, :HEAD_DIM]


def _fox_fwd(qf, kf, vft, batch, seq):
    def body(q_ref, k_ref, vt_ref, o_ref, lse_ref, m_s, acc_s):
        causal = _key_query_mask(lambda r, c: r <= c)

        def tile(q0, kj, masked):
            krows = pl.ds(pl.multiple_of(kj * TK, TK), TK)
            heads = range(NH)
            sts = [_dot_nt(k_ref[krows, _hslot(hh)], q_ref[pl.ds(q0, TQ), _hslot(hh)]) for hh in heads]
            if masked:
                sts = [jnp.where(causal, st, NEG) for st in sts]
            m_olds = [m_s[hh] for hh in heads]
            m_news = [jnp.maximum(m_olds[hh], jnp.max(sts[hh], axis=0, keepdims=True)) for hh in heads]
            pts = [jnp.exp(sts[hh] - m_news[hh]).astype(BF16) for hh in heads]
            pvs = [_dot(vt_ref[kj, _hslot(hh), :], pts[hh]) for hh in heads]
            for hh in heads:
                acc_s[hh] = jnp.exp(m_olds[hh] - m_news[hh]) * acc_s[hh] + pvs[hh]
                m_s[hh] = m_news[hh]

        def q_loop(qi, _):
            q0 = pl.multiple_of(qi * TQ, TQ)
            m_s[...] = jnp.full(m_s.shape, NEG, F32)
            acc_s[...] = jnp.zeros_like(acc_s)

            def k_loop(kj, _):
                tile(q0, kj, False)
                return 0

            lax.fori_loop(0, qi, k_loop, 0)
            tile(q0, qi, True)
            outs = []
            for hh in range(NH):
                total = acc_s[hh, HEAD_DIM:HEAD_DIM + 1, :]
                outs.append(_untranspose(acc_s[hh] / total))
                lse_ref[hh, pl.ds(qi, 1), :] = m_s[hh] + jnp.log(total)
            o_ref[pl.ds(q0, TQ), :] = _heads_cat(outs).astype(BF16)
            return 0

        lax.fori_loop(0, seq // TQ, q_loop, 0)

    return pl.pallas_call(
        body, name="fox_fwd", grid=(batch, N_HEADS // NH),
        out_shape=(jax.ShapeDtypeStruct((batch * seq, D_BRANCH), BF16),
                   jax.ShapeDtypeStruct((batch, N_HEADS, seq // TQ, TQ), F32)),
        in_specs=[_slot_spec(seq), _slot_spec(seq), _tblock_spec(seq)],
        out_specs=(_group_spec(seq), _qrow_spec(seq)),
        scratch_shapes=[pltpu.VMEM((NH, 1, TQ), F32), pltpu.VMEM((NH, HEAD_SLOT, TQ), F32)],
        compiler_params=_attn_params(),
    )(qf, kf, vft)


def _fox_bwd(qf, kf, kft, vf, o, do, lse, batch, seq):
    n_q = seq // TQ

    def body(q_ref, k_ref, kt_ref, v_ref, o_ref, do_ref, lse_ref, dqk_ref, dv_ref, dcq_ref, dck_ref,
             delta_s, dqt_acc, dk_s, dv_s):
        causal = _key_query_mask(lambda r, c: r <= c)
        ones8 = jnp.ones((8, HEAD_DIM), BF16)
        dqt_acc[...] = jnp.zeros_like(dqt_acc)

        def prep(qi, _):
            rows = pl.ds(pl.multiple_of(qi * TQ, TQ), TQ)
            for hh in range(NH):
                hi, lo = _split2(do_ref[rows, _hcols(hh)].astype(F32) * o_ref[rows, _hcols(hh)].astype(F32))
                delta_s[hh, pl.ds(qi, 1), :] = (_dot_nt(ones8, hi) + _dot_nt(ones8, lo))[0:1, :]
            return 0

        lax.fori_loop(0, n_q, prep, 0)

        def tile(qi, kj, masked):
            rows = pl.ds(pl.multiple_of(qi * TQ, TQ), TQ)
            krows = pl.ds(pl.multiple_of(kj * TK, TK), TK)
            heads = range(NH)
            qs = [q_ref[rows, _hslot(hh)] for hh in heads]
            douts = [do_ref[rows, _hcols(hh)] for hh in heads]
            sts = [_dot_nt(k_ref[krows, _hslot(hh)], qs[hh]) for hh in heads]
            dps = [_dot_nt(v_ref[krows, _hcols(hh)], douts[hh]) for hh in heads]
            pts = [jnp.exp(sts[hh] - lse_ref[hh, pl.ds(qi, 1), :]) for hh in heads]
            if masked:
                pts = [jnp.where(causal, pt, 0.0) for pt in pts]
            dsts = [(pts[hh] * (dps[hh] - delta_s[hh, pl.ds(qi, 1), :])).astype(BF16) for hh in heads]
            for hh in heads:
                dv_s[hh] += _dot(pts[hh].astype(BF16), douts[hh])
                dk_s[hh] += _dot(dsts[hh], qs[hh])
                dqt_acc[hh, qi] += _dot(kt_ref[kj, _hslot(hh), :], dsts[hh])

        def k_loop(kj, _):
            krows = pl.ds(pl.multiple_of(kj * TK, TK), TK)
            dk_s[...] = jnp.zeros_like(dk_s)
            dv_s[...] = jnp.zeros_like(dv_s)
            tile(kj, kj, True)

            def q_loop(qi, _):
                tile(qi, kj, False)
                return 0

            lax.fori_loop(kj + 1, n_q, q_loop, 0)
            dqk_ref[1, krows, :] = _heads_cat([dk_s[hh] for hh in range(NH)]).astype(BF16)
            dv_ref[krows, :] = _heads_cat([dv_s[hh] for hh in range(NH)]).astype(BF16)
            for hh in range(NH):
                dck_ref[krows, _lane(hh)] = dk_s[hh, :, C_ONES_Q:C_ONES_Q + 1]
            return 0

        lax.fori_loop(0, seq // TK, k_loop, 0)

        def finish(qi, _):
            rows = pl.ds(pl.multiple_of(qi * TQ, TQ), TQ)
            dqk_ref[0, rows, :] = _heads_cat([dqt_acc[hh, qi].T for hh in range(NH)]).astype(BF16)
            for hh in range(NH):
                dcq_ref[hh, pl.ds(qi, 1), :] = dqt_acc[hh, qi, C_ONES_K:C_ONES_K + 1, :]
            return 0

        lax.fori_loop(0, n_q, finish, 0)

    return pl.pallas_call(
        body, name="fox_bwd", grid=(batch, N_HEADS // NH),
        out_shape=(jax.ShapeDtypeStruct((2, batch * seq, N_HEADS * HEAD_SLOT), BF16),
                   jax.ShapeDtypeStruct((batch * seq, D_BRANCH), BF16),
                   jax.ShapeDtypeStruct((batch, N_HEADS, seq // TQ, TQ), F32),
                   jax.ShapeDtypeStruct((batch, N_HEADS // NH, seq, LANES), F32)),
        in_specs=[_slot_spec(seq), _slot_spec(seq), _tblock_spec(seq), _group_spec(seq), _group_spec(seq),
                  _group_spec(seq), _qrow_spec(seq)],
        out_specs=(_slot2_spec(seq), _group_spec(seq), _qrow_spec(seq), _stat_spec(seq)),
        scratch_shapes=[pltpu.VMEM((NH, n_q, TQ), F32), pltpu.VMEM((NH, n_q, HEAD_SLOT, TQ), F32),
                        pltpu.VMEM((NH, TK, HEAD_SLOT), F32), pltpu.VMEM((NH, TK, HEAD_DIM), F32)],
        compiler_params=_attn_params(),
    )(qf, kf, kft, vf, o, do, lse)


def _sb_fwd(qkvb, vbt, batch, seq):
    def body(q_ref, k_ref, vt_ref, o_ref, lt_ref, run_s, acc_s):
        strict = _key_query_mask(lambda r, c: r < c)
        later = _tri(TK, lambda r, c: c > r)

        def tile(q0, kj, masked):
            krows = pl.ds(pl.multiple_of(kj * TK, TK), TK)
            heads = range(NH)
            zts = [_dot_nt(k_ref[krows, _hcols(hh)], q_ref[pl.ds(q0, TQ), _hcols(hh)]) for hh in heads]
            lgs = [-_softplus(zt) for zt in zts]
            if masked:
                lgs = [jnp.where(strict, lg, 0.0) for lg in lgs]
            parts = [_split2(lg) for lg in lgs]
            sufs = [_dot(later, hi) + _dot(later, lo) for hi, lo in parts]
            ats = [jnp.exp(zts[hh] + lgs[hh] + run_s[hh] + sufs[hh]) for hh in heads]
            if masked:
                ats = [jnp.where(strict, at, 0.0) for at in ats]
            for hh in heads:
                acc_s[hh] += _dot(vt_ref[kj, _hslot(hh), :], ats[hh].astype(BF16))
                run_s[hh] += jnp.sum(lgs[hh], axis=0, keepdims=True)

        def q_loop(qi, _):
            q0 = pl.multiple_of(qi * TQ, TQ)
            run_s[...] = jnp.zeros_like(run_s)
            acc_s[...] = jnp.zeros_like(acc_s)
            tile(q0, qi, True)

            def k_loop(kk, _):
                tile(q0, qi - 1 - kk, False)
                return 0

            lax.fori_loop(0, qi, k_loop, 0)
            o_ref[pl.ds(q0, TQ), :] = _heads_cat([_untranspose(acc_s[hh]) for hh in range(NH)]).astype(BF16)
            for hh in range(NH):
                lt_ref[hh, pl.ds(qi, 1), :] = run_s[hh]
            return 0

        lax.fori_loop(0, seq // TQ, q_loop, 0)

    return pl.pallas_call(
        body, name="sb_fwd", grid=(batch, N_HEADS // NH),
        out_shape=(jax.ShapeDtypeStruct((batch * seq, D_BRANCH), BF16),
                   jax.ShapeDtypeStruct((batch, N_HEADS, seq // TQ, TQ), F32)),
        in_specs=[_group3_spec(0, seq), _group3_spec(1, seq), _tblock_spec(seq)],
        out_specs=(_group_spec(seq), _qrow_spec(seq)),
        scratch_shapes=[pltpu.VMEM((NH, 1, TQ), F32), pltpu.VMEM((NH, HEAD_SLOT, TQ), F32)],
        compiler_params=_attn_params(),
    )(qkvb, qkvb, vbt)


def _sb_bwd(qkvb, kbt, do, ltot, batch, seq):
    def body(q_ref, k_ref, v_ref, kt_ref, do_ref, lt_ref, dqkv_ref, dk_acc, dv_acc, ls_s, gs_s, dqt_s):
        strict = _key_query_mask(lambda r, c: r < c)
        upto = _tri(TK, lambda r, c: c <= r)
        before = _tri(TK, lambda r, c: c < r)
        dk_acc[...] = jnp.zeros_like(dk_acc)
        dv_acc[...] = jnp.zeros_like(dv_acc)

        def tile(qi, kj, masked):
            rows = pl.ds(pl.multiple_of(qi * TQ, TQ), TQ)
            krows = pl.ds(pl.multiple_of(kj * TK, TK), TK)
            heads = range(NH)
            qs = [q_ref[rows, _hcols(hh)] for hh in heads]
            douts = [do_ref[rows, _hcols(hh)] for hh in heads]
            zts = [_dot_nt(k_ref[krows, _hcols(hh)], qs[hh]) for hh in heads]
            das = [_dot_nt(v_ref[krows, _hcols(hh)], douts[hh]) for hh in heads]
            lgs = [-_softplus(zt) for zt in zts]
            if masked:
                lgs = [jnp.where(strict, lg, 0.0) for lg in lgs]
            parts = [_split2(lg) for lg in lgs]
            prefs = [_dot(upto, hi) + _dot(upto, lo) for hi, lo in parts]
            ats = [jnp.exp(zts[hh] + lgs[hh] + (lt_ref[hh, pl.ds(qi, 1), :] - ls_s[hh]) - prefs[hh]) for hh in heads]
            if masked:
                ats = [jnp.where(strict, at, 0.0) for at in ats]
            gts = [das[hh] * ats[hh] for hh in heads]
            us = [gs_s[hh] + _dot(before, gts[hh].astype(BF16)) for hh in heads]
            dzts = [(jnp.exp(lgs[hh]) * (gts[hh] + us[hh]) - us[hh]).astype(BF16) for hh in heads]
            for hh in heads:
                dk_acc[hh, krows, :] += _dot(dzts[hh], qs[hh])
                dv_acc[hh, krows, :] += _dot(ats[hh].astype(BF16), douts[hh])
                dqt_s[hh] += _dot(kt_ref[kj, _hslot(hh), :], dzts[hh])
                ls_s[hh] += jnp.sum(lgs[hh], axis=0, keepdims=True)
                gs_s[hh] += jnp.sum(gts[hh], axis=0, keepdims=True)

        def q_loop(qi, _):
            ls_s[...] = jnp.zeros_like(ls_s)
            gs_s[...] = jnp.zeros_like(gs_s)
            dqt_s[...] = jnp.zeros_like(dqt_s)

            def k_loop(kj, _):
                tile(qi, kj, False)
                return 0

            lax.fori_loop(0, qi, k_loop, 0)
            tile(qi, qi, True)
            dqkv_ref[0, pl.ds(pl.multiple_of(qi * TQ, TQ), TQ), :] = _heads_cat(
                [_untranspose(dqt_s[hh]) for hh in range(NH)]).astype(BF16)
            return 0

        lax.fori_loop(0, seq // TQ, q_loop, 0)
        dqkv_ref[1] = _heads_cat([dk_acc[hh] for hh in range(NH)]).astype(BF16)
        dqkv_ref[2] = _heads_cat([dv_acc[hh] for hh in range(NH)]).astype(BF16)

    return pl.pallas_call(
        body, name="sb_bwd", grid=(batch, N_HEADS // NH),
        out_shape=jax.ShapeDtypeStruct((3, batch * seq, D_BRANCH), BF16),
        in_specs=[_group3_spec(0, seq), _group3_spec(1, seq), _group3_spec(2, seq), _tblock_spec(seq),
                  _group_spec(seq), _qrow_spec(seq)],
        out_specs=pl.BlockSpec((3, seq, NH * HEAD_DIM), lambda b, g: (0, b, g)),
        scratch_shapes=[pltpu.VMEM((NH, seq, HEAD_DIM), F32), pltpu.VMEM((NH, seq, HEAD_DIM), F32),
                        pltpu.VMEM((NH, 1, TQ), F32), pltpu.VMEM((NH, 1, TQ), F32),
                        pltpu.VMEM((NH, HEAD_SLOT, TQ), F32)],
        compiler_params=_attn_params(),
    )(qkvb, qkvb, qkvb, kbt, do, ltot)


def _forget_bwd(dcq_tok, dck_tok, fpre, batch, seq):
    t_len = batch * seq
    tiles = seq // TM

    def rev(i):
        return ((i // tiles) * tiles + (tiles - 1 - i % tiles), 0)

    def body(dcq_ref, dck_ref, f_ref, df_ref, db_ref, carry_ref):
        i = pl.program_id(0)

        @pl.when(i == 0)
        def _():
            db_ref[...] = jnp.zeros_like(db_ref)

        @pl.when(i % tiles == 0)
        def _():
            carry_ref[...] = jnp.zeros_like(carry_ref)

        dc = dcq_ref[...] - dck_ref[...]
        upper = _tri(TM, lambda r, c: c >= r)
        hi, mid, lo = _split3(dc)
        dlogf = carry_ref[...] + _dot(upper, hi) + _dot(upper, mid) + _dot(upper, lo)
        carry_ref[...] = carry_ref[...] + jnp.sum(dc, axis=0, keepdims=True)
        df = dlogf * _sigmoid(-f_ref[...])
        df_ref[...] = df.astype(BF16)
        db_ref[...] += jnp.sum(df, axis=0, keepdims=True)

    return pl.pallas_call(
        body, name="forget_bwd", grid=(t_len // TM,),
        out_shape=(jax.ShapeDtypeStruct((t_len, LANES), BF16), jax.ShapeDtypeStruct((1, LANES), F32)),
        in_specs=[pl.BlockSpec((TM, LANES), rev)] * 3,
        out_specs=(pl.BlockSpec((TM, LANES), rev), _acc_spec((1, LANES))),
        scratch_shapes=[pltpu.VMEM((1, LANES), F32)],
        compiler_params=_seq_params(),
    )(dcq_tok, dck_tok, fpre)


def _mix_fwd(o_fox, o_sb, gl, x, w_bf, w_bs, w_out, b_gate):
    t_len, d = x.shape

    def body(of_ref, os_ref, gl_ref, x_ref, wbf_ref, wbs_ref, wo_ref, bg_ref, x1_ref):
        br_f = _dot(of_ref[...], wbf_ref[...])
        br_s = _dot(os_ref[...], wbs_ref[...])
        ga = _sigmoid(gl_ref[:, :d].astype(F32) + bg_ref[0:1, :])
        gb = _sigmoid(gl_ref[:, d:].astype(F32) + bg_ref[1:2, :])
        merged = ga * br_f + gb * br_s
        x1_ref[...] = x_ref[...] + _dot(merged.astype(BF16), wo_ref[...])

    return pl.pallas_call(
        body, name="mix_fwd", grid=(t_len // TM,),
        out_shape=jax.ShapeDtypeStruct((t_len, d), F32),
        in_specs=[_row_spec(TM, D_BRANCH), _row_spec(TM, D_BRANCH), _row_spec(TM, 2 * d), _row_spec(TM, d),
                  _const_spec(w_bf.shape), _const_spec(w_bs.shape), _const_spec(w_out.shape), _const_spec(b_gate.shape)],
        out_specs=_row_spec(TM, d),
        compiler_params=_seq_params(),
    )(o_fox, o_sb, gl, x, w_bf, w_bs, w_out, b_gate)


def _ff_chunk(d_ff):
    return min(d_ff, 1024)


def _mlp_fwd(x1, g_mlp, w_up, w_down):
    t_len, d = x1.shape
    d_ff = w_up.shape[1]
    ch = _ff_chunk(d_ff)

    def body(x1_ref, g_ref, wu_ref, wd_ref, a_ref, x2_ref):
        x1v = x1_ref[...]
        xn, _ = _rms(x1v)
        h = (xn * g_ref[...]).astype(BF16)
        acc = x1v
        for j in range(d_ff // ch):
            a = _dot(h, wu_ref[:, j * ch:(j + 1) * ch])
            a_ref[:, j * ch:(j + 1) * ch] = a.astype(BF16)
            acc = acc + _dot(jnp.square(jnp.maximum(a, 0.0)).astype(BF16), wd_ref[j * ch:(j + 1) * ch, :])
        x2_ref[...] = acc

    return pl.pallas_call(
        body, name="mlp_fwd", grid=(t_len // TM,),
        out_shape=(jax.ShapeDtypeStruct((t_len, d_ff), BF16), jax.ShapeDtypeStruct((t_len, d), F32)),
        in_specs=[_row_spec(TM, d), _const_spec((1, d)), _const_spec(w_up.shape), _const_spec(w_down.shape)],
        out_specs=(_row_spec(TM, d_ff), _row_spec(TM, d)),
        compiler_params=_seq_params(),
    )(x1, g_mlp, w_up, w_down)


def _head_fwd_bwd(x2, p, target, g_ple, g_final, w_pg, w_ple):
    t_len, d = x2.shape
    d_ple = p.shape[1]

    def body(x2_ref, p_ref, t_ref, gp_ref, gf_ref, wpg_ref, wple_ref,
             dx2_ref, h3_ref, dpre_ref, dpe_ref, loss_ref, dgp_ref, dgf_ref):
        @pl.when(pl.program_id(0) == 0)
        def _():
            loss_ref[...] = jnp.zeros_like(loss_ref)
            dgp_ref[...] = jnp.zeros_like(dgp_ref)
            dgf_ref[...] = jnp.zeros_like(dgf_ref)

        x2v = x2_ref[...]
        x2n, r3 = _rms(x2v)
        h3 = (x2n * gp_ref[...]).astype(BF16)
        h3_ref[...] = h3
        gate = _sigmoid(_dot(h3, wpg_ref[...]))
        pe = _dot(p_ref[...].astype(BF16), wple_ref[...])
        x3n, r4 = _rms(x2v + gate * pe)
        err = x3n * gf_ref[...] - t_ref[...]
        loss_ref[...] += jnp.full(loss_ref.shape, (0.5 / d) * jnp.sum(err * err), F32)
        dx3, dgf = _rms_bwd(err * (1.0 / d), x3n, r4, gf_ref[...])
        dgf_ref[...] += dgf
        dpe_ref[...] = (dx3 * gate).astype(BF16)
        dpre = (dx3 * pe * gate * (1.0 - gate)).astype(BF16)
        dpre_ref[...] = dpre
        dres, dgp = _rms_bwd(_dot_nt(dpre, wpg_ref[...]), x2n, r3, gp_ref[...])
        dgp_ref[...] += dgp
        dx2_ref[...] = dx3 + dres

    shp_b = jax.ShapeDtypeStruct((t_len, d), BF16)
    return pl.pallas_call(
        body, name="head_fwd_bwd", grid=(t_len // TM,),
        out_shape=(jax.ShapeDtypeStruct((t_len, d), F32), shp_b, shp_b, shp_b,
                   jax.ShapeDtypeStruct((1, LANES), F32), jax.ShapeDtypeStruct((1, d), F32),
                   jax.ShapeDtypeStruct((1, d), F32)),
        in_specs=[_row_spec(TM, d), _row_spec(TM, d_ple), _row_spec(TM, d), _const_spec((1, d)), _const_spec((1, d)),
                  _const_spec(w_pg.shape), _const_spec(w_ple.shape)],
        out_specs=(_row_spec(TM, d), _row_spec(TM, d), _row_spec(TM, d), _row_spec(TM, d),
                   _acc_spec((1, LANES)), _acc_spec((1, d)), _acc_spec((1, d))),
        compiler_params=_seq_params(),
    )(x2, p, target, g_ple, g_final, w_pg, w_ple)


def _mlp_bwd(dx2, a, x1, g_mlp, w_up, w_down):
    t_len, d = x1.shape
    d_ff = w_up.shape[1]
    ch = _ff_chunk(d_ff)

    def body(dx2_ref, a_ref, x1_ref, g_ref, wu_ref, wd_ref, dx1_ref, da_ref, h2_ref, dg_ref):
        @pl.when(pl.program_id(0) == 0)
        def _():
            dg_ref[...] = jnp.zeros_like(dg_ref)

        dx2v = dx2_ref[...]
        dx2b = dx2v.astype(BF16)
        xn, r = _rms(x1_ref[...])
        h2_ref[...] = (xn * g_ref[...]).astype(BF16)
        dh = jnp.zeros((TM, d), F32)
        for j in range(d_ff // ch):
            dact = _dot_nt(dx2b, wd_ref[j * ch:(j + 1) * ch, :])
            da = (dact * 2.0 * jnp.maximum(a_ref[:, j * ch:(j + 1) * ch].astype(F32), 0.0)).astype(BF16)
            da_ref[:, j * ch:(j + 1) * ch] = da
            dh = dh + _dot_nt(da, wu_ref[:, j * ch:(j + 1) * ch])
        dres, dg = _rms_bwd(dh, xn, r, g_ref[...])
        dg_ref[...] += dg
        dx1_ref[...] = dx2v + dres

    return pl.pallas_call(
        body, name="mlp_bwd", grid=(t_len // TM,),
        out_shape=(jax.ShapeDtypeStruct((t_len, d), F32), jax.ShapeDtypeStruct((t_len, d_ff), BF16),
                   jax.ShapeDtypeStruct((t_len, d), BF16), jax.ShapeDtypeStruct((1, d), F32)),
        in_specs=[_row_spec(TM, d), _row_spec(TM, d_ff), _row_spec(TM, d), _const_spec((1, d)),
                  _const_spec(w_up.shape), _const_spec(w_down.shape)],
        out_specs=(_row_spec(TM, d), _row_spec(TM, d_ff), _row_spec(TM, d), _acc_spec((1, d))),
        compiler_params=_seq_params(),
    )(dx2, a, x1, g_mlp, w_up, w_down)


def _mix_bwd(dx1, o_fox, o_sb, gl, w_bf, w_bs, w_out, b_gate):
    t_len, d = dx1.shape

    def body(dx1_ref, of_ref, os_ref, gl_ref, wbf_ref, wbs_ref, wo_ref, bg_ref,
             mg_ref, dbf_ref, dbs_ref, dgl_ref, dof_ref, dos_ref, dbg_ref):
        @pl.when(pl.program_id(0) == 0)
        def _():
            dbg_ref[...] = jnp.zeros_like(dbg_ref)

        dmerged = _dot_nt(dx1_ref[...].astype(BF16), wo_ref[...])
        br_f = _dot(of_ref[...], wbf_ref[...])
        br_s = _dot(os_ref[...], wbs_ref[...])
        ga = _sigmoid(gl_ref[:, :d].astype(F32) + bg_ref[0:1, :])
        gb = _sigmoid(gl_ref[:, d:].astype(F32) + bg_ref[1:2, :])
        mg_ref[...] = (ga * br_f + gb * br_s).astype(BF16)
        dbf = (dmerged * ga).astype(BF16)
        dbs = (dmerged * gb).astype(BF16)
        dbf_ref[...] = dbf
        dbs_ref[...] = dbs
        dla = dmerged * br_f * ga * (1.0 - ga)
        dlb = dmerged * br_s * gb * (1.0 - gb)
        dgl_ref[:, :d] = dla.astype(BF16)
        dgl_ref[:, d:] = dlb.astype(BF16)
        dbg_ref[0:1, :] += jnp.sum(dla, axis=0, keepdims=True)
        dbg_ref[1:2, :] += jnp.sum(dlb, axis=0, keepdims=True)
        dof_ref[...] = _dot_nt(dbf, wbf_ref[...]).astype(BF16)
        dos_ref[...] = _dot_nt(dbs, wbs_ref[...]).astype(BF16)

    shp_d = jax.ShapeDtypeStruct((t_len, d), BF16)
    shp_h = jax.ShapeDtypeStruct((t_len, D_BRANCH), BF16)
    return pl.pallas_call(
        body, name="mix_bwd", grid=(t_len // TM,),
        out_shape=(shp_d, shp_d, shp_d, jax.ShapeDtypeStruct((t_len, 2 * d), BF16), shp_h, shp_h,
                   jax.ShapeDtypeStruct((2, d), F32)),
        in_specs=[_row_spec(TM, d), _row_spec(TM, D_BRANCH), _row_spec(TM, D_BRANCH), _row_spec(TM, 2 * d),
                  _const_spec(w_bf.shape), _const_spec(w_bs.shape), _const_spec(w_out.shape), _const_spec(b_gate.shape)],
        out_specs=(_row_spec(TM, d), _row_spec(TM, d), _row_spec(TM, d), _row_spec(TM, 2 * d),
                   _row_spec(TM, D_BRANCH), _row_spec(TM, D_BRANCH), _acc_spec((2, d))),
        compiler_params=_seq_params(),
    )(dx1, o_fox, o_sb, gl, w_bf, w_bs, w_out, b_gate)


def _inproj_bwd(dqk_f, dv_f, dqkv_b, dgl, df, dx1, x, g_mix, w_pad):
    t_len, d = x.shape
    lay, _ = _pad_layout(d)
    slot_w = N_HEADS * HEAD_SLOT

    def body(dqk_ref, dvf_ref, db_ref, dgl_ref, df_ref, dx1_ref, x_ref, g_ref, w_ref, dx_ref, h1_ref, dg_ref):
        @pl.when(pl.program_id(0) == 0)
        def _():
            dg_ref[...] = jnp.zeros_like(dg_ref)

        def back(piece, name):
            lo, hi = lay[name]
            return _dot_nt(piece, w_ref[:, lo:hi])

        xn, r = _rms(x_ref[...])
        h1_ref[...] = (xn * g_ref[...]).astype(BF16)
        dh = (back(df_ref[...], "forget") + back(dgl_ref[...], "gates") + back(dqk_ref[0], "qf")
              + back(dqk_ref[1], "kf") + back(dvf_ref[...], "vf") + back(db_ref[0], "qb") + back(db_ref[1], "kb")
              + back(db_ref[2], "vb"))
        dres, dg = _rms_bwd(dh, xn, r, g_ref[...])
        dg_ref[...] += dg
        dx_ref[...] = dx1_ref[...] + dres

    return pl.pallas_call(
        body, name="inproj_bwd", grid=(t_len // TM,),
        out_shape=(jax.ShapeDtypeStruct((t_len, d), F32), jax.ShapeDtypeStruct((t_len, d), BF16),
                   jax.ShapeDtypeStruct((1, d), F32)),
        in_specs=[_row3_spec(2, TM, slot_w), _row_spec(TM, D_BRANCH), _row3_spec(3, TM, D_BRANCH),
                  _row_spec(TM, 2 * d), _row_spec(TM, LANES), _row_spec(TM, d), _row_spec(TM, d), _const_spec((1, d)),
                  _const_spec(w_pad.shape)],
        out_specs=(_row_spec(TM, d), _row_spec(TM, d), _acc_spec((1, d))),
        compiler_params=_seq_params(),
    )(dqk_f, dv_f, dqkv_b, dgl, df, dx1, x, g_mix, w_pad)


def _cols_to_slabs(full):
    r, c8 = full.shape
    return full.reshape(r, N_DEV, c8 // N_DEV).transpose(1, 0, 2)


def _slabs_to_cols(slabs):
    n, r, c = slabs.shape
    return slabs.transpose(1, 0, 2).reshape(r, n * c)


def _win_sizes(d):
    return (D_BRANCH, D_BRANCH, D_BRANCH, N_HEADS, D_BRANCH, D_BRANCH, D_BRANCH, d, d)


def _split_win(w, d):
    out, off = [], 0
    for s in _win_sizes(d):
        out.append(w[:, off:off + s])
        off += s
    return out


def _to_slots(w):
    r = w.shape[0]
    return jnp.pad(w.reshape(r, N_HEADS, HEAD_DIM), ((0, 0), (0, 0), (0, HEAD_SLOT - HEAD_DIM))).reshape(r, -1)


def _from_slots(w):
    r = w.shape[0]
    return w.reshape(r, N_HEADS, HEAD_SLOT)[:, :, :HEAD_DIM].reshape(r, N_HEADS * HEAD_DIM)


def _pad_win(w_full, d):
    qa, ka, va, fa, qb, kb, vb, ga, gb = _split_win(w_full, d)
    scale = HEAD_DIM ** -0.5
    fpad = jnp.pad(fa, ((0, 0), (0, LANES - N_HEADS)))
    return jnp.concatenate([_to_slots(qa * scale), _to_slots(ka), va, qb * scale, kb, vb, ga, gb, fpad], axis=1)


def _unpad_dwin(dqk_f, dv_f, dqkv_b, dgates, dforget, d):
    scale = HEAD_DIM ** -0.5
    return jnp.concatenate([_from_slots(dqk_f[0]) * scale, _from_slots(dqk_f[1]), dv_f, dforget[:, :N_HEADS],
                            dqkv_b[0] * scale, dqkv_b[1], dqkv_b[2], dgates], axis=1)


def _c_lane_constants():
    head = jnp.arange(LANES)[:, None]
    lane = jnp.arange(N_HEADS * HEAD_SLOT)[None, :]
    in_head = (lane // HEAD_SLOT == head) & (head < N_HEADS)

    def place(first):
        return jnp.stack([(in_head & (lane % HEAD_SLOT == first + j)) for j in range(3)]).astype(BF16)

    def ones(first):
        off = lane % HEAD_SLOT
        return ((off >= first) & (off < first + 3)).astype(F32)

    return place(C_TERMS_Q), place(C_TERMS_K), ones(C_ONES_Q), ones(C_ONES_K)


def _pad_rows(a, rows):
    return jnp.pad(a, [(0, 0)] * (a.ndim - 2) + [(0, rows - a.shape[-2]), (0, 0)])


def kernel(x, p, g_mix, w_in, b_forget, b_gate, w_branch_fox, w_branch_sb, w_out, g_mlp, w_up, w_down, g_ple, w_ple_gate, w_ple, g_final, loss_target, m_g_mix, m_w_in, m_b_forget, m_b_gate, m_w_branch_fox, m_w_branch_sb, m_w_out, m_g_mlp, m_w_up, m_w_down, m_g_ple, m_w_ple_gate, m_w_ple, m_g_final, v_g_mix, v_w_in, v_b_forget, v_b_gate, v_w_branch_fox, v_w_branch_sb, v_w_out, v_g_mlp, v_w_up, v_w_down, v_g_ple, v_w_ple_gate, v_w_ple, v_g_final):
    batch, seq, d = x.shape
    t_len = batch * seq
    d_ple = p.shape[-1]
    d_ff = w_up.shape[-1] * N_DEV
    dn = d // N_DEV
    fn = d_ff // N_DEV
    my_c = lax.axis_index("c")
    my_dev = 4 * lax.axis_index("x") + 2 * lax.axis_index("y") + my_c

    bg_hi = b_gate[0].astype(BF16)
    bg_r = b_gate[0] - bg_hi.astype(F32)
    bg_mid = bg_r.astype(BF16)
    bg_lo = (bg_r - bg_mid.astype(F32)).astype(BF16)
    narrow_rows = 2 * D_BRANCH + d_ple + 6
    narrow_rows_pad = -(-narrow_rows // 16) * 16
    wide = jnp.concatenate([w_out[0], w_down[0], w_ple_gate[0]], axis=0).astype(BF16)
    narrow = _pad_rows(jnp.concatenate(
        [w_branch_fox[0].astype(BF16), w_branch_sb[0].astype(BF16), w_ple[0].astype(BF16), bg_hi, bg_mid, bg_lo],
        axis=0), narrow_rows_pad)
    g_in, g_up, g_wide, g_narrow = _all_gather([w_in[0].astype(BF16), w_up[0].astype(BF16), wide, narrow])
    w_pad = _pad_win(_slabs_to_cols(g_in), d)
    w_up_full = _slabs_to_cols(g_up)
    w_out_full = g_wide[:, :dn].reshape(d, d)
    w_down_full = g_wide[:, dn:dn + fn].reshape(d_ff, d)
    w_pg_full = g_wide[:, dn + fn:].reshape(d, d)
    w_bf_full = _slabs_to_cols(g_narrow[:, :D_BRANCH])
    w_bs_full = _slabs_to_cols(g_narrow[:, D_BRANCH:2 * D_BRANCH])
    w_ple_full = _slabs_to_cols(g_narrow[:, 2 * D_BRANCH:2 * D_BRANCH + d_ple])
    bg_terms = g_narrow[:, 2 * D_BRANCH + d_ple:narrow_rows].astype(F32)
    b_gate_full = _slabs_to_cols(bg_terms[:, 0:2] + bg_terms[:, 2:4] + bg_terms[:, 4:6])
    bf_pad = jnp.pad(b_forget, ((0, 0), (0, LANES - N_HEADS)))
    place_q, place_k, ones_q, ones_k = _c_lane_constants()

    x2d = x.reshape(t_len, d)
    p2d = p.reshape(t_len, d_ple)
    tgt2d = loss_target.reshape(t_len, d)
    qf, kf, kft, vf, vft, qkvb, kbt, vbt, gl, fpre = _inproj_fwd(
        x2d, g_mix, w_pad, bf_pad, place_q, place_k, ones_q, ones_k, seq)
    o_fox, lse = _fox_fwd(qf, kf, vft, batch, seq)
    o_sb, ltot = _sb_fwd(qkvb, vbt, batch, seq)
    x1 = _mix_fwd(o_fox, o_sb, gl, x2d, w_bf_full, w_bs_full, w_out_full, b_gate_full)
    a_up, x2 = _mlp_fwd(x1, g_mlp, w_up_full, w_down_full)

    dx2, h3, dpre, dpe, loss_acc, dg_ple, dg_final = _head_fwd_bwd(
        x2, p2d, tgt2d, g_ple, g_final.reshape(1, d), w_pg_full, w_ple_full)
    dx1, da_up, h2, dg_mlp = _mlp_bwd(dx2, a_up, x1, g_mlp, w_up_full, w_down_full)
    merged, dbr_f, dbr_s, dgl, do_fox, do_sb, dbg = _mix_bwd(
        dx1, o_fox, o_sb, gl, w_bf_full, w_bs_full, w_out_full, b_gate_full)
    dqk_f, dv_f, dc_queries, dc_keys = _fox_bwd(qf, kf, kft, vf, o_fox, do_fox, lse, batch, seq)
    dqkv_b = _sb_bwd(qkvb, kbt, do_sb, ltot, batch, seq)
    dcq_tok = dc_queries.reshape(batch, N_HEADS, seq).transpose(0, 2, 1).reshape(t_len, N_HEADS)
    dck_tok = dc_keys[..., :NH].transpose(0, 2, 1, 3).reshape(t_len, N_HEADS)
    lane_pad = ((0, 0), (0, LANES - N_HEADS))
    df, db_forget = _forget_bwd(jnp.pad(dcq_tok, lane_pad), jnp.pad(dck_tok, lane_pad), fpre, batch, seq)
    grad_x, h1, dg_mix = _inproj_bwd(dqk_f, dv_f, dqkv_b, dgl, df, dx1, x2d, g_mix, w_pad)

    gw_ple = _matmul_tn("dw_ple", p2d, dpe)
    gw_pg = _matmul_tn("dw_ple_gate", h3, dpre)
    gw_down = _matmul_tn("dw_down", a_up, dx2, relu2=True)
    gw_up = _matmul_tn("dw_up", h2, da_up)
    gw_out = _matmul_tn("dw_out", merged, dx1)
    gw_bf = _matmul_tn("dw_branch_fox", o_fox, dbr_f)
    gw_bs = _matmul_tn("dw_branch_sb", o_sb, dbr_s)
    gw_in = _unpad_dwin(_matmul_tn("dw_in_fox_qk", h1, dqk_f), _matmul_tn("dw_in_fox_v", h1, dv_f),
                        _matmul_tn("dw_in_sb", h1, dqkv_b), _matmul_tn("dw_in_gates", h1, dgl),
                        _matmul_tn("dw_in_forget", h1, df), d)

    part_in = _cols_to_slabs(gw_in).astype(BF16)
    part_up = _cols_to_slabs(gw_up).astype(BF16)
    part_wide = jnp.concatenate([gw_out.reshape(N_DEV, dn, d), gw_down.reshape(N_DEV, fn, d),
                                 gw_pg.reshape(N_DEV, dn, d)], axis=1).astype(BF16)
    part_narrow = _pad_rows(jnp.concatenate(
        [_cols_to_slabs(gw_bf), _cols_to_slabs(gw_bs), _cols_to_slabs(gw_ple)], axis=1).astype(BF16), narrow_rows_pad)
    partials = [part_in, part_up, part_wide, part_narrow]
    received = _rs_core_pair(partials)
    chip_sums = [_pair_add("pair_add_%d" % i, pt, rc, my_c) for i, (pt, rc) in enumerate(zip(partials, received))]
    s_in, s_up, s_wide, s_narrow = _rs_chips(chip_sums)

    small = jnp.concatenate([
        dg_mix, dg_mlp, dg_ple, dg_final, jnp.pad(db_forget[:, :N_HEADS], ((0, 0), (0, d - N_HEADS))), dbg,
        jnp.pad(loss_acc[:, :1], ((0, 0), (0, d - 1)))], axis=0)
    small = _all_reduce_small(small)
    loss = small[7, 0]
    small_grads = {
        "g_mix": small[0:1], "g_mlp": small[1:2], "g_ple": small[2:3], "g_final": small[3:4],
        "b_forget": small[4:5, :N_HEADS],
        "b_gate": lax.dynamic_slice_in_dim(small[5:7], my_dev * dn, dn, axis=1),
    }

    weights = {"g_mix": g_mix, "w_in": w_in, "b_forget": b_forget, "b_gate": b_gate, "w_branch_fox": w_branch_fox,
               "w_branch_sb": w_branch_sb, "w_out": w_out, "g_mlp": g_mlp, "w_up": w_up, "w_down": w_down,
               "g_ple": g_ple, "w_ple_gate": w_ple_gate, "w_ple": w_ple, "g_final": g_final}
    m_in = {"g_mix": m_g_mix, "w_in": m_w_in, "b_forget": m_b_forget, "b_gate": m_b_gate,
            "w_branch_fox": m_w_branch_fox, "w_branch_sb": m_w_branch_sb, "w_out": m_w_out, "g_mlp": m_g_mlp,
            "w_up": m_w_up, "w_down": m_w_down, "g_ple": m_g_ple, "w_ple_gate": m_w_ple_gate, "w_ple": m_w_ple,
            "g_final": m_g_final}
    v_in = {"g_mix": v_g_mix, "w_in": v_w_in, "b_forget": v_b_forget, "b_gate": v_b_gate,
            "w_branch_fox": v_w_branch_fox, "w_branch_sb": v_w_branch_sb, "w_out": v_w_out, "g_mlp": v_g_mlp,
            "w_up": v_w_up, "w_down": v_w_down, "g_ple": v_g_ple, "w_ple_gate": v_w_ple_gate, "w_ple": v_w_ple,
            "g_final": v_g_final}
    names = list(weights)

    def as2d(a):
        return a.reshape(-1, a.shape[-1])

    result = {}
    big = {"w_in": (s_in, 0), "w_up": (s_up, 0), "w_out": (s_wide, 0), "w_down": (s_wide, dn),
           "w_ple_gate": (s_wide, dn + fn), "w_branch_fox": (s_narrow, 0), "w_branch_sb": (s_narrow, D_BRANCH),
           "w_ple": (s_narrow, 2 * D_BRANCH)}
    for n, (parts, off) in big.items():
        result[n] = _adamw_parts("adamw_" + n, as2d(weights[n]), parts, off, as2d(m_in[n]), as2d(v_in[n]))
    small_names = list(small_grads)
    small_out = _adamw_small([(as2d(weights[n]), small_grads[n], as2d(m_in[n]), as2d(v_in[n])) for n in small_names])
    for n, (dlt, nm, nv) in zip(small_names, small_out):
        result[n] = (small_grads[n], dlt, nm, nv)
    outs = [[result[n][k].reshape(weights[n].shape) for n in names] for k in range(4)]
    return (loss, grad_x.reshape(x.shape), *outs[0], *outs[1], *outs[2], *outs[3])
```

```python
import jax
import jax.numpy as jnp
from jax import lax
from jax.experimental import pallas as pl
from jax.experimental.pallas import tpu as pltpu

F32 = jnp.float32
BF16 = jnp.bfloat16

HEAD_DIM = 64
N_HEADS = 8
D_BRANCH = N_HEADS * HEAD_DIM
EPS = 1e-6
ADAM_LR = 0.001
ADAM_B1 = 0.9
ADAM_B2 = 0.999
ADAM_EPS = 1e-08
ADAM_WD = 0.01
ADAM_STEP = 10

N_DEV = 8
LANES = 128
TM = 256
TQ = 256
TK = 256
NH = 4
HEAD_SLOT = 128
C_TERMS_Q = 64
C_ONES_K = 64
C_TERMS_K = 67
C_ONES_Q = 67
NEG = -1e30
VMEM_LIMIT = 56 * 1024 * 1024
MESH = pl.DeviceIdType.MESH


def _dot(a, b):
    return jnp.dot(a, b, preferred_element_type=F32)


def _dot_nt(a, b):
    return lax.dot_general(a, b, (((1,), (1,)), ((), ())), preferred_element_type=F32)


def _dot_tn(a, b):
    return lax.dot_general(a, b, (((0,), (0,)), ((), ())), preferred_element_type=F32)


def _sigmoid(x):
    return 1.0 / (1.0 + jnp.exp(-x))


def _softplus(x):
    return jnp.maximum(x, 0.0) + jnp.log(1.0 + jnp.exp(-jnp.abs(x)))


def _split2(x):
    hi = x.astype(BF16)
    lo = (x - hi.astype(F32)).astype(BF16)
    return hi, lo


def _split3(x):
    hi = x.astype(BF16)
    r = x - hi.astype(F32)
    mid = r.astype(BF16)
    lo = (r - mid.astype(F32)).astype(BF16)
    return hi, mid, lo


def _rows_dot_mask(x, mask_bf16):
    hi, lo = _split2(x)
    return _dot(hi, mask_bf16) + _dot(lo, mask_bf16)


def _tri(n, rel):
    r = lax.broadcasted_iota(jnp.int32, (n, n), 0)
    c = lax.broadcasted_iota(jnp.int32, (n, n), 1)
    return rel(r, c).astype(BF16)


def _rms(x):
    r = lax.rsqrt(jnp.mean(x * x, axis=-1, keepdims=True) + EPS)
    return x * r, r


def _rms_bwd(dh, xn, r, g):
    dxn = dh * g
    dx = r * (dxn - xn * jnp.mean(dxn * xn, axis=-1, keepdims=True))
    return dx, jnp.sum(dh * xn, axis=0, keepdims=True)


def _row_spec(tm, cols):
    return pl.BlockSpec((tm, cols), lambda i: (i, 0))


def _row3_spec(g, tm, cols):
    return pl.BlockSpec((g, tm, cols), lambda i: (0, i, 0))


def _const_spec(shape):
    nd = len(shape)
    return pl.BlockSpec(shape, lambda i: (0,) * nd, pipeline_mode=pl.Buffered(1))


def _acc_spec(shape):
    nd = len(shape)
    return pl.BlockSpec(shape, lambda i: (0,) * nd)


def _seq_params():
    return pltpu.CompilerParams(dimension_semantics=("arbitrary",), vmem_limit_bytes=VMEM_LIMIT)


def _mesh_pos():
    return lax.axis_index("x"), lax.axis_index("y"), lax.axis_index("c")


def _other_chips(x, y):
    return [(1 - x, y), (x, 1 - y), (1 - x, 1 - y)]


def _hbm_specs(n):
    return [pl.BlockSpec(memory_space=pl.ANY)] * n


def _gather_plan(x_refs, out_refs, send_sems, recv_sems, local_sems):
    n = len(x_refs)
    x, y, c = _mesh_pos()
    me, sibling = (x, y, c), (x, y, 1 - c)
    chips = _other_chips(x, y)

    def index(px, py, pc):
        return 4 * px + 2 * py + pc

    def copy(a, k, block, to, src=None):
        slab = out_refs[a].at[index(*block)]
        return pltpu.make_async_remote_copy(
            src_ref=slab if src is None else src, dst_ref=slab,
            send_sem=send_sems.at[7 * a + k], recv_sem=recv_sems.at[7 * a + k], device_id=to, device_id_type=MESH)

    mine = [pltpu.make_async_copy(x_refs[a], out_refs[a].at[index(*me)], local_sems.at[a]) for a in range(n)]
    first = []
    for a in range(n):
        first.append(copy(a, 0, me, sibling, src=x_refs[a]))
        first += [copy(a, 1 + j, me, (cx, cy, c), src=x_refs[a]) for j, (cx, cy) in enumerate(chips)]

    def start():
        for cp in mine + first:
            cp.start()

    def finish():
        passed = []
        for j, (cx, cy) in enumerate(chips):
            for a in range(n):
                copy(a, 1 + j, (cx, cy, c), me).wait_recv()
                passed.append(copy(a, 4 + j, (cx, cy, c), sibling))
                passed[-1].start()
        for a in range(n):
            copy(a, 0, sibling, me).wait_recv()
            for j, (cx, cy) in enumerate(chips):
                copy(a, 4 + j, (cx, cy, 1 - c), me).wait_recv()
        for cp in first + passed:
            cp.wait_send()
        for cp in mine:
            cp.wait()

    return start, finish


def _gather_shapes(shards):
    return [jax.ShapeDtypeStruct((N_DEV,) + s.shape, s.dtype) for s in shards]


def _gather_sems(n):
    return [pltpu.SemaphoreType.DMA((7 * n,)), pltpu.SemaphoreType.DMA((7 * n,)), pltpu.SemaphoreType.DMA((n,))]


def _all_gather(shards):
    n = len(shards)

    def body(*refs):
        start, finish = _gather_plan(refs[:n], refs[n:2 * n], *refs[2 * n:])
        start()
        finish()

    return pl.pallas_call(
        body, name="all_gather_weights", out_shape=_gather_shapes(shards),
        in_specs=_hbm_specs(n), out_specs=_hbm_specs(n), scratch_shapes=_gather_sems(n),
    )(*shards)


def _rs_core_pair(name, partials):
    n = len(partials)

    def body(*refs):
        p_refs, recv_refs = refs[:n], refs[n:2 * n]
        send_sems, recv_sems = refs[2 * n:]
        x, y, c = _mesh_pos()
        for a in range(n):
            for chip in range(4):
                pltpu.make_async_remote_copy(
                    src_ref=p_refs[a].at[2 * chip + (1 - c)], dst_ref=recv_refs[a].at[chip],
                    send_sem=send_sems.at[a], recv_sem=recv_sems.at[a],
                    device_id=(x, y, 1 - c), device_id_type=MESH).start()
        for a in range(n):
            pltpu.make_async_remote_copy(
                src_ref=recv_refs[a], dst_ref=recv_refs[a], send_sem=send_sems.at[a], recv_sem=recv_sems.at[a],
                device_id=(x, y, 1 - c), device_id_type=MESH).wait()

    return pl.pallas_call(
        body, name=name,
        out_shape=[jax.ShapeDtypeStruct((4,) + s.shape[1:], s.dtype) for s in partials],
        in_specs=_hbm_specs(n), out_specs=_hbm_specs(n),
        scratch_shapes=[pltpu.SemaphoreType.DMA((n,)), pltpu.SemaphoreType.DMA((n,))],
    )(*partials)


def _chips_plan(cs_refs, out_refs, send_sems, recv_sems, local_sems):
    n = len(cs_refs)
    x, y, c = _mesh_pos()
    chip = 2 * x + y
    chips = _other_chips(x, y)
    mine = [pltpu.make_async_copy(cs_refs[a].at[chip], out_refs[a].at[chip], local_sems.at[a]) for a in range(n)]
    sends = [pltpu.make_async_remote_copy(
        src_ref=cs_refs[a].at[2 * cx + cy], dst_ref=out_refs[a].at[chip],
        send_sem=send_sems.at[3 * a + j], recv_sem=recv_sems.at[3 * a + j],
        device_id=(cx, cy, c), device_id_type=MESH) for a in range(n) for j, (cx, cy) in enumerate(chips)]

    def start():
        for cp in mine + sends:
            cp.start()

    def finish():
        for a in range(n):
            for j, (cx, cy) in enumerate(chips):
                pltpu.make_async_remote_copy(
                    src_ref=cs_refs[a].at[chip], dst_ref=out_refs[a].at[2 * cx + cy],
                    send_sem=send_sems.at[3 * a + j], recv_sem=recv_sems.at[3 * a + j],
                    device_id=(x, y, c), device_id_type=MESH).wait_recv()
        for cp in sends:
            cp.wait_send()
        for cp in mine:
            cp.wait()

    return start, finish


def _chips_sems(n):
    return [pltpu.SemaphoreType.DMA((3 * n,)), pltpu.SemaphoreType.DMA((3 * n,)), pltpu.SemaphoreType.DMA((n,))]


def _rs_chips(chip_sums):
    n = len(chip_sums)

    def body(*refs):
        start, finish = _chips_plan(refs[:n], refs[n:2 * n], *refs[2 * n:])
        start()
        finish()

    return pl.pallas_call(
        body, name="reduce_scatter_chips",
        out_shape=[jax.ShapeDtypeStruct(s.shape, s.dtype) for s in chip_sums],
        in_specs=_hbm_specs(n), out_specs=_hbm_specs(n), scratch_shapes=_chips_sems(n),
    )(*chip_sums)


def _all_reduce_small(vec):
    rows, cols = vec.shape

    def body(x_ref, land_ref, sum_ref, send_sems, recv_sems):
        x, y, c = _mesh_pos()
        me = 4 * x + 2 * y + c
        land_ref[me] = x_ref[...]
        flips = [(fx, fy, fc) for fx in (0, 1) for fy in (0, 1) for fc in (0, 1)][1:]

        def flipped(f):
            return tuple((1 - v) if b else v for v, b in zip((x, y, c), f))

        sends = []
        for k, f in enumerate(flips):
            sends.append(pltpu.make_async_remote_copy(
                src_ref=x_ref, dst_ref=land_ref.at[me], send_sem=send_sems.at[k], recv_sem=recv_sems.at[k],
                device_id=flipped(f), device_id_type=MESH))
            sends[-1].start()
        for k, f in enumerate(flips):
            px, py, pc = flipped(f)
            pltpu.make_async_remote_copy(
                src_ref=x_ref, dst_ref=land_ref.at[4 * px + 2 * py + pc], send_sem=send_sems.at[k],
                recv_sem=recv_sems.at[k], device_id=(x, y, c), device_id_type=MESH).wait_recv()
        for cp in sends:
            cp.wait_send()
        total = land_ref[0]
        for d in range(1, N_DEV):
            total = total + land_ref[d]
        sum_ref[...] = total

    vm = pl.BlockSpec(memory_space=pltpu.VMEM)
    return pl.pallas_call(
        body, name="all_reduce_small",
        out_shape=(jax.ShapeDtypeStruct((N_DEV, rows, cols), F32), jax.ShapeDtypeStruct((rows, cols), F32)),
        in_specs=[vm], out_specs=(vm, vm),
        scratch_shapes=[pltpu.SemaphoreType.DMA((7,)), pltpu.SemaphoreType.DMA((7,))],
    )(vec)[1]


def _block_rows(rows, cols, itemsize, align, row_off=0):
    best = None
    for t in range(align, rows + 1, align):
        if rows % t == 0 and row_off % t == 0 and t * cols * itemsize <= (1 << 20):
            best = t
    return rows if best is None else best


def _pair_add(name, partial, recv, my_c):
    _, rows, cols = partial.shape
    br = _block_rows(rows, cols, 2, 16)

    def body(c_ref, a_ref, b_ref, o_ref):
        o_ref[...] = (a_ref[...].astype(F32) + b_ref[...].astype(F32)).astype(BF16)

    return pl.pallas_call(
        body, name=name,
        grid_spec=pltpu.PrefetchScalarGridSpec(
            num_scalar_prefetch=1, grid=(4, rows // br),
            in_specs=[pl.BlockSpec((None, None, br, cols), lambda j, i, c_ref: (j, c_ref[0], i, 0)),
                      pl.BlockSpec((None, br, cols), lambda j, i, c_ref: (j, i, 0))],
            out_specs=pl.BlockSpec((None, br, cols), lambda j, i, c_ref: (j, i, 0))),
        out_shape=jax.ShapeDtypeStruct((4, rows, cols), BF16),
    )(my_c.reshape(1).astype(jnp.int32), partial.reshape(4, 2, rows, cols), recv)


def _adam_update(w, g, m, v):
    nm = ADAM_B1 * m + (1.0 - ADAM_B1) * g
    nv = ADAM_B2 * v + (1.0 - ADAM_B2) * (g * g)
    m_hat = nm / (1.0 - ADAM_B1 ** ADAM_STEP)
    v_hat = nv / (1.0 - ADAM_B2 ** ADAM_STEP)
    return -ADAM_LR * (m_hat / (jnp.sqrt(v_hat) + ADAM_EPS) + ADAM_WD * w), nm, nv


def _adamw_parts(name, w, parts, row_off, m, v):
    rows, cols = w.shape
    tr = _block_rows(rows, cols, 4, 16, row_off)
    assert rows % tr == 0 and row_off % tr == 0
    off = row_off // tr

    def body(w_ref, p_ref, m_ref, v_ref, g_ref, d_ref, nm_ref, nv_ref):
        g = p_ref[0].astype(F32)
        for j in range(1, 4):
            g = g + p_ref[j].astype(F32)
        g_ref[...] = g
        d_ref[...], nm_ref[...], nv_ref[...] = _adam_update(w_ref[...], g, m_ref[...], v_ref[...])

    spec = pl.BlockSpec((tr, cols), lambda i: (i, 0))
    shp = jax.ShapeDtypeStruct((rows, cols), F32)
    return pl.pallas_call(
        body, name=name, grid=(rows // tr,), out_shape=(shp,) * 4,
        in_specs=[spec, pl.BlockSpec((4, tr, cols), lambda i: (0, off + i, 0)), spec, spec], out_specs=(spec,) * 4,
    )(w, parts, m, v)


def _adamw_small(tensors):
    n = len(tensors)

    def body(*refs):
        ins, outs = refs[:4 * n], refs[4 * n:]
        for t in range(n):
            w_ref, g_ref, m_ref, v_ref = ins[4 * t:4 * t + 4]
            d, nm, nv = _adam_update(w_ref[...], g_ref[...], m_ref[...], v_ref[...])
            outs[3 * t][...], outs[3 * t + 1][...], outs[3 * t + 2][...] = d, nm, nv

    vm = pl.BlockSpec(memory_space=pltpu.VMEM)
    out = pl.pallas_call(
        body, name="adamw_small",
        out_shape=[jax.ShapeDtypeStruct(t[0].shape, F32) for t in tensors for _ in range(3)],
        in_specs=[vm] * (4 * n), out_specs=[vm] * (3 * n),
    )(*[a for t in tensors for a in t])
    return [tuple(out[3 * t:3 * t + 3]) for t in range(n)]


def _matmul_tn(name, a, b, relu2=False):
    squeeze = b.ndim == 2
    if squeeze:
        b = b[None]
    t_len, k_len = a.shape
    groups, _, n_len = b.shape
    tt = min(t_len, 512)
    tk = min(k_len, 512)
    tn = min(n_len, 1024)
    nt = t_len // tt

    def body(a_ref, b_ref, o_ref):
        @pl.when(pl.program_id(3) == 0)
        def _():
            o_ref[...] = jnp.zeros_like(o_ref)

        av = a_ref[...]
        if relu2:
            av = jnp.square(jnp.maximum(av.astype(F32), 0.0))
        o_ref[...] += _dot_tn(av.astype(BF16), b_ref[...].astype(BF16))

    out = pl.pallas_call(
        body, name=name, grid=(groups, k_len // tk, n_len // tn, nt),
        out_shape=jax.ShapeDtypeStruct((groups, k_len, n_len), F32),
        in_specs=[pl.BlockSpec((tt, tk), lambda g, i, j, t: (t, i)),
                  pl.BlockSpec((None, tt, tn), lambda g, i, j, t: (g, t, j))],
        out_specs=pl.BlockSpec((None, tk, tn), lambda g, i, j, t: (g, i, j)),
        compiler_params=pltpu.CompilerParams(
            dimension_semantics=("parallel", "parallel", "parallel", "arbitrary"), vmem_limit_bytes=VMEM_LIMIT),
    )(a, b)
    return out[0] if squeeze else out


def _pad_layout(d):
    names = ("qf", "kf", "vf", "qb", "kb", "vb", "gates", "forget")
    sizes = (N_HEADS * HEAD_SLOT, N_HEADS * HEAD_SLOT, D_BRANCH, D_BRANCH, D_BRANCH, D_BRANCH, 2 * d, LANES)
    out, off = {}, 0
    for n, s in zip(names, sizes):
        out[n] = (off, off + s)
        off += s
    return out, off


def _slot_rows(xt, extra):
    parts = []
    for h in range(N_HEADS):
        parts += [xt[h * HEAD_DIM:(h + 1) * HEAD_DIM, :], extra]
    return jnp.concatenate(parts, axis=0)


def _inproj_fwd(x, g_mix, w_pad, bf_pad, place_q, place_k, ones_q, ones_k, seq):
    t_len, d = x.shape
    lay, _ = _pad_layout(d)
    tiles_per_seq = seq // TM
    slot_w = N_HEADS * HEAD_SLOT

    def body(x_ref, g_ref, w_ref, bf_ref, pq_ref, pk_ref, oq_ref, ok_ref,
             qf_ref, kf_ref, kft_ref, vf_ref, vft_ref, qkvb_ref, kbt_ref, vbt_ref, gl_ref, fpre_ref, carry_ref):
        @pl.when(pl.program_id(0) % tiles_per_seq == 0)
        def _():
            carry_ref[...] = jnp.zeros_like(carry_ref)

        def proj(name):
            lo, hi = lay[name]
            return _dot(h, w_ref[:, lo:hi])

        xn, _ = _rms(x_ref[...])
        h = (xn * g_ref[...]).astype(BF16)
        fpre = proj("forget") + bf_ref[...]
        fpre_ref[...] = fpre
        logf = -_softplus(-fpre)
        lower = _tri(TM, lambda r, c: c <= r)
        hi, mid, lo = _split3(logf)
        c_val = carry_ref[...] + _dot(lower, hi) + _dot(lower, mid) + _dot(lower, lo)
        carry_ref[...] = carry_ref[...] + jnp.sum(logf, axis=0, keepdims=True)
        c3 = _split3(c_val)
        qf_ref[...] = (proj("qf") + sum(_dot(c3[j], pq_ref[j]) for j in range(3)) + oq_ref[...]).astype(BF16)
        kf = proj("kf") - sum(_dot(c3[j], pk_ref[j]) for j in range(3)) + ok_ref[...]
        kf_ref[...] = kf.astype(BF16)
        kft_ref[0] = kf.T.astype(BF16)
        row0 = (lax.broadcasted_iota(jnp.int32, (HEAD_DIM, TM), 0) == 0).astype(F32)
        zeros = jnp.zeros((HEAD_DIM, TM), F32)
        vf = proj("vf")
        vf_ref[...] = vf.astype(BF16)
        vft_ref[0] = _slot_rows(vf.T, row0).astype(BF16)
        qkvb_ref[0] = proj("qb").astype(BF16)
        kb = proj("kb")
        qkvb_ref[1] = kb.astype(BF16)
        kbt_ref[0] = _slot_rows(kb.T, zeros).astype(BF16)
        vb = proj("vb")
        qkvb_ref[2] = vb.astype(BF16)
        vbt_ref[0] = _slot_rows(vb.T, row0).astype(BF16)
        gl_ref[...] = proj("gates").astype(BF16)

    n_tiles = t_len // TM
    slot_shape = jax.ShapeDtypeStruct((t_len, slot_w), BF16)
    t_shape = jax.ShapeDtypeStruct((n_tiles, slot_w, TM), BF16)
    t_spec = pl.BlockSpec((1, slot_w, TM), lambda i: (i, 0, 0))
    return pl.pallas_call(
        body, name="inproj_fwd", grid=(n_tiles,),
        out_shape=(slot_shape, slot_shape, t_shape, jax.ShapeDtypeStruct((t_len, D_BRANCH), BF16), t_shape,
                   jax.ShapeDtypeStruct((3, t_len, D_BRANCH), BF16), t_shape, t_shape,
                   jax.ShapeDtypeStruct((t_len, 2 * d), BF16), jax.ShapeDtypeStruct((t_len, LANES), F32)),
        in_specs=[_row_spec(TM, d), _const_spec((1, d)), _const_spec(w_pad.shape), _const_spec((1, LANES)),
                  _const_spec(place_q.shape), _const_spec(place_k.shape), _const_spec((1, slot_w)),
                  _const_spec((1, slot_w))],
        out_specs=(_row_spec(TM, slot_w), _row_spec(TM, slot_w), t_spec, _row_spec(TM, D_BRANCH), t_spec,
                   _row3_spec(3, TM, D_BRANCH), t_spec, t_spec, _row_spec(TM, 2 * d), _row_spec(TM, LANES)),
        scratch_shapes=[pltpu.VMEM((1, LANES), F32)],
        compiler_params=_seq_params(),
    )(x, g_mix, w_pad, bf_pad, place_q, place_k, ones_q, ones_k)


def _slot_spec(seq):
    return pl.BlockSpec((seq, NH * HEAD_SLOT), lambda b, g: (b, g))


def _slot2_spec(seq):
    return pl.BlockSpec((2, seq, NH * HEAD_SLOT), lambda b, g: (0, b, g))


def _group_spec(seq):
    return pl.BlockSpec((seq, NH * HEAD_DIM), lambda b, g: (b, g))


def _group3_spec(which, seq):
    return pl.BlockSpec((None, seq, NH * HEAD_DIM), lambda b, g: (which, b, g))


def _tblock_spec(seq):
    return pl.BlockSpec((seq // TK, NH * HEAD_SLOT, TK), lambda b, g: (b, g, 0))


def _qrow_spec(seq):
    return pl.BlockSpec((None, NH, seq // TQ, TQ), lambda b, g: (b, g, 0, 0))


def _stat_spec(seq):
    return pl.BlockSpec((None, None, seq, LANES), lambda b, g: (b, g, 0, 0))


def _attn_params():
    return pltpu.CompilerParams(dimension_semantics=("parallel", "parallel"), vmem_limit_bytes=VMEM_LIMIT)


def _serial_attn_params():
    return pltpu.CompilerParams(dimension_semantics=("arbitrary", "arbitrary"), vmem_limit_bytes=VMEM_LIMIT)


def _hcols(hh):
    return slice(hh * HEAD_DIM, (hh + 1) * HEAD_DIM)


def _hslot(hh):
    return slice(hh * HEAD_SLOT, (hh + 1) * HEAD_SLOT)


def _lane(hh):
    return slice(hh, hh + 1)


def _key_query_mask(rel):
    r = lax.broadcasted_iota(jnp.int32, (TK, TQ), 0)
    c = lax.broadcasted_iota(jnp.int32, (TK, TQ), 1)
    return rel(r, c)


def _heads_cat(vals):
    return jnp.concatenate(vals, axis=1)


def _untranspose(acc_t):
    return acc_t.T[:, :HEAD_DIM]


def _fox_fwd(qf, kf, vft, batch, seq):
    def body(q_ref, k_ref, vt_ref, o_ref, lse_ref, m_s, acc_s):
        causal = _key_query_mask(lambda r, c: r <= c)

        def tile(q0, kj, masked):
            krows = pl.ds(pl.multiple_of(kj * TK, TK), TK)
            heads = range(NH)
            sts = [_dot_nt(k_ref[krows, _hslot(hh)], q_ref[pl.ds(q0, TQ), _hslot(hh)]) for hh in heads]
            if masked:
                sts = [jnp.where(causal, st, NEG) for st in sts]
            m_olds = [m_s[hh] for hh in heads]
            m_news = [jnp.maximum(m_olds[hh], jnp.max(sts[hh], axis=0, keepdims=True)) for hh in heads]
            pts = [jnp.exp(sts[hh] - m_news[hh]).astype(BF16) for hh in heads]
            pvs = [_dot(vt_ref[kj, _hslot(hh), :], pts[hh]) for hh in heads]
            for hh in heads:
                acc_s[hh] = jnp.exp(m_olds[hh] - m_news[hh]) * acc_s[hh] + pvs[hh]
                m_s[hh] = m_news[hh]

        def q_loop(qi, _):
            q0 = pl.multiple_of(qi * TQ, TQ)
            m_s[...] = jnp.full(m_s.shape, NEG, F32)
            acc_s[...] = jnp.zeros_like(acc_s)

            def k_loop(kj, _):
                tile(q0, kj, False)
                return 0

            lax.fori_loop(0, qi, k_loop, 0)
            tile(q0, qi, True)
            outs = []
            for hh in range(NH):
                total = acc_s[hh, HEAD_DIM:HEAD_DIM + 1, :]
                outs.append(_untranspose(acc_s[hh] / total))
                lse_ref[hh, pl.ds(qi, 1), :] = m_s[hh] + jnp.log(total)
            o_ref[pl.ds(q0, TQ), :] = _heads_cat(outs).astype(BF16)
            return 0

        lax.fori_loop(0, seq // TQ, q_loop, 0)

    return pl.pallas_call(
        body, name="fox_fwd", grid=(batch, N_HEADS // NH),
        out_shape=(jax.ShapeDtypeStruct((batch * seq, D_BRANCH), BF16),
                   jax.ShapeDtypeStruct((batch, N_HEADS, seq // TQ, TQ), F32)),
        in_specs=[_slot_spec(seq), _slot_spec(seq), _tblock_spec(seq)],
        out_specs=(_group_spec(seq), _qrow_spec(seq)),
        scratch_shapes=[pltpu.VMEM((NH, 1, TQ), F32), pltpu.VMEM((NH, HEAD_SLOT, TQ), F32)],
        compiler_params=_attn_params(),
    )(qf, kf, vft)


def _fox_bwd(qf, kf, kft, vf, o, do, lse, batch, seq):
    n_q = seq // TQ

    def body(q_ref, k_ref, kt_ref, v_ref, o_ref, do_ref, lse_ref, dqk_ref, dv_ref, dcq_ref, dck_ref,
             delta_s, dqt_acc, dk_s, dv_s):
        causal = _key_query_mask(lambda r, c: r <= c)
        ones8 = jnp.ones((8, HEAD_DIM), BF16)
        dqt_acc[...] = jnp.zeros_like(dqt_acc)

        def prep(qi, _):
            rows = pl.ds(pl.multiple_of(qi * TQ, TQ), TQ)
            for hh in range(NH):
                hi, lo = _split2(do_ref[rows, _hcols(hh)].astype(F32) * o_ref[rows, _hcols(hh)].astype(F32))
                delta_s[hh, pl.ds(qi, 1), :] = (_dot_nt(ones8, hi) + _dot_nt(ones8, lo))[0:1, :]
            return 0

        lax.fori_loop(0, n_q, prep, 0)

        def tile(qi, kj, masked):
            rows = pl.ds(pl.multiple_of(qi * TQ, TQ), TQ)
            krows = pl.ds(pl.multiple_of(kj * TK, TK), TK)
            heads = range(NH)
            qs = [q_ref[rows, _hslot(hh)] for hh in heads]
            douts = [do_ref[rows, _hcols(hh)] for hh in heads]
            sts = [_dot_nt(k_ref[krows, _hslot(hh)], qs[hh]) for hh in heads]
            dps = [_dot_nt(v_ref[krows, _hcols(hh)], douts[hh]) for hh in heads]
            pts = [jnp.exp(sts[hh] - lse_ref[hh, pl.ds(qi, 1), :]) for hh in heads]
            if masked:
                pts = [jnp.where(causal, pt, 0.0) for pt in pts]
            dsts = [(pts[hh] * (dps[hh] - delta_s[hh, pl.ds(qi, 1), :])).astype(BF16) for hh in heads]
            for hh in heads:
                dv_s[hh] += _dot(pts[hh].astype(BF16), douts[hh])
                dk_s[hh] += _dot(dsts[hh], qs[hh])
                dqt_acc[hh, qi] += _dot(kt_ref[kj, _hslot(hh), :], dsts[hh])

        def k_loop(kj, _):
            krows = pl.ds(pl.multiple_of(kj * TK, TK), TK)
            dk_s[...] = jnp.zeros_like(dk_s)
            dv_s[...] = jnp.zeros_like(dv_s)
            tile(kj, kj, True)

            def q_loop(qi, _):
                tile(qi, kj, False)
                return 0

            lax.fori_loop(kj + 1, n_q, q_loop, 0)
            dqk_ref[1, krows, :] = _heads_cat([dk_s[hh] for hh in range(NH)]).astype(BF16)
            dv_ref[krows, :] = _heads_cat([dv_s[hh] for hh in range(NH)]).astype(BF16)
            for hh in range(NH):
                dck_ref[krows, _lane(hh)] = dk_s[hh, :, C_ONES_Q:C_ONES_Q + 1]
            return 0

        lax.fori_loop(0, seq // TK, k_loop, 0)

        def finish(qi, _):
            rows = pl.ds(pl.multiple_of(qi * TQ, TQ), TQ)
            dqk_ref[0, rows, :] = _heads_cat([dqt_acc[hh, qi].T for hh in range(NH)]).astype(BF16)
            for hh in range(NH):
                dcq_ref[hh, pl.ds(qi, 1), :] = dqt_acc[hh, qi, C_ONES_K:C_ONES_K + 1, :]
            return 0

        lax.fori_loop(0, n_q, finish, 0)

    return pl.pallas_call(
        body, name="fox_bwd", grid=(batch, N_HEADS // NH),
        out_shape=(jax.ShapeDtypeStruct((2, batch * seq, N_HEADS * HEAD_SLOT), BF16),
                   jax.ShapeDtypeStruct((batch * seq, D_BRANCH), BF16),
                   jax.ShapeDtypeStruct((batch, N_HEADS, seq // TQ, TQ), F32),
                   jax.ShapeDtypeStruct((batch, N_HEADS // NH, seq, LANES), F32)),
        in_specs=[_slot_spec(seq), _slot_spec(seq), _tblock_spec(seq), _group_spec(seq), _group_spec(seq),
                  _group_spec(seq), _qrow_spec(seq)],
        out_specs=(_slot2_spec(seq), _group_spec(seq), _qrow_spec(seq), _stat_spec(seq)),
        scratch_shapes=[pltpu.VMEM((NH, n_q, TQ), F32), pltpu.VMEM((NH, n_q, HEAD_SLOT, TQ), F32),
                        pltpu.VMEM((NH, TK, HEAD_SLOT), F32), pltpu.VMEM((NH, TK, HEAD_DIM), F32)],
        compiler_params=_attn_params(),
    )(qf, kf, kft, vf, o, do, lse)


def _first_last_step():
    step = pl.program_id(0) * pl.num_programs(1) + pl.program_id(1)
    return step == 0, step == pl.num_programs(0) * pl.num_programs(1) - 1


def _sb_fwd(qkvb, vbt, batch, seq, shards):
    n = len(shards)

    def body(q_ref, k_ref, vt_ref, *rest):
        x_refs, (o_ref, lt_ref), out_refs = rest[:n], rest[n:n + 2], rest[n + 2:2 * n + 2]
        run_s, acc_s = rest[2 * n + 2:2 * n + 4]
        gather_start, gather_finish = _gather_plan(x_refs, out_refs, *rest[2 * n + 4:])
        first_step, last_step = _first_last_step()
        pl.when(first_step)(gather_start)
        strict = _key_query_mask(lambda r, c: r < c)
        later = _tri(TK, lambda r, c: c > r)

        def tile(q0, kj, masked):
            krows = pl.ds(pl.multiple_of(kj * TK, TK), TK)
            heads = range(NH)
            zts = [_dot_nt(k_ref[krows, _hcols(hh)], q_ref[pl.ds(q0, TQ), _hcols(hh)]) for hh in heads]
            lgs = [-_softplus(zt) for zt in zts]
            if masked:
                lgs = [jnp.where(strict, lg, 0.0) for lg in lgs]
            parts = [_split2(lg) for lg in lgs]
            sufs = [_dot(later, hi) + _dot(later, lo) for hi, lo in parts]
            ats = [jnp.exp(zts[hh] + lgs[hh] + run_s[hh] + sufs[hh]) for hh in heads]
            if masked:
                ats = [jnp.where(strict, at, 0.0) for at in ats]
            for hh in heads:
                acc_s[hh] += _dot(vt_ref[kj, _hslot(hh), :], ats[hh].astype(BF16))
                run_s[hh] += jnp.sum(lgs[hh], axis=0, keepdims=True)

        def q_loop(qi, _):
            q0 = pl.multiple_of(qi * TQ, TQ)
            run_s[...] = jnp.zeros_like(run_s)
            acc_s[...] = jnp.zeros_like(acc_s)
            tile(q0, qi, True)

            def k_loop(kk, _):
                tile(q0, qi - 1 - kk, False)
                return 0

            lax.fori_loop(0, qi, k_loop, 0)
            o_ref[pl.ds(q0, TQ), :] = _heads_cat([_untranspose(acc_s[hh]) for hh in range(NH)]).astype(BF16)
            for hh in range(NH):
                lt_ref[hh, pl.ds(qi, 1), :] = run_s[hh]
            return 0

        lax.fori_loop(0, seq // TQ, q_loop, 0)
        pl.when(last_step)(gather_finish)

    out = pl.pallas_call(
        body, name="sb_fwd", grid=(batch, N_HEADS // NH),
        out_shape=[jax.ShapeDtypeStruct((batch * seq, D_BRANCH), BF16),
                   jax.ShapeDtypeStruct((batch, N_HEADS, seq // TQ, TQ), F32)] + _gather_shapes(shards),
        in_specs=[_group3_spec(0, seq), _group3_spec(1, seq), _tblock_spec(seq)] + _hbm_specs(n),
        out_specs=[_group_spec(seq), _qrow_spec(seq)] + _hbm_specs(n),
        scratch_shapes=[pltpu.VMEM((NH, 1, TQ), F32), pltpu.VMEM((NH, HEAD_SLOT, TQ), F32)] + _gather_sems(n),
        compiler_params=_serial_attn_params(),
    )(qkvb, qkvb, vbt, *shards)
    return out[0], out[1], out[2:]


def _sb_bwd(qkvb, kbt, do, ltot, batch, seq, chip_sums):
    n = len(chip_sums)

    def body(q_ref, k_ref, v_ref, kt_ref, do_ref, lt_ref, *rest):
        cs_refs, dqkv_ref, out_refs = rest[:n], rest[n], rest[n + 1:2 * n + 1]
        dk_acc, dv_acc, ls_s, gs_s, dqt_s = rest[2 * n + 1:2 * n + 6]
        chips_start, chips_finish = _chips_plan(cs_refs, out_refs, *rest[2 * n + 6:])
        first_step, last_step = _first_last_step()
        pl.when(first_step)(chips_start)
        strict = _key_query_mask(lambda r, c: r < c)
        upto = _tri(TK, lambda r, c: c <= r)
        before = _tri(TK, lambda r, c: c < r)
        dk_acc[...] = jnp.zeros_like(dk_acc)
        dv_acc[...] = jnp.zeros_like(dv_acc)

        def tile(qi, kj, masked):
            rows = pl.ds(pl.multiple_of(qi * TQ, TQ), TQ)
            krows = pl.ds(pl.multiple_of(kj * TK, TK), TK)
            heads = range(NH)
            qs = [q_ref[rows, _hcols(hh)] for hh in heads]
            douts = [do_ref[rows, _hcols(hh)] for hh in heads]
            zts = [_dot_nt(k_ref[krows, _hcols(hh)], qs[hh]) for hh in heads]
            das = [_dot_nt(v_ref[krows, _hcols(hh)], douts[hh]) for hh in heads]
            lgs = [-_softplus(zt) for zt in zts]
            if masked:
                lgs = [jnp.where(strict, lg, 0.0) for lg in lgs]
            parts = [_split2(lg) for lg in lgs]
            prefs = [_dot(upto, hi) + _dot(upto, lo) for hi, lo in parts]
            ats = [jnp.exp(zts[hh] + lgs[hh] + (lt_ref[hh, pl.ds(qi, 1), :] - ls_s[hh]) - prefs[hh]) for hh in heads]
            if masked:
                ats = [jnp.where(strict, at, 0.0) for at in ats]
            gts = [das[hh] * ats[hh] for hh in heads]
            us = [gs_s[hh] + _dot(before, gts[hh].astype(BF16)) for hh in heads]
            dzts = [(jnp.exp(lgs[hh]) * (gts[hh] + us[hh]) - us[hh]).astype(BF16) for hh in heads]
            for hh in heads:
                dk_acc[hh, krows, :] += _dot(dzts[hh], qs[hh])
                dv_acc[hh, krows, :] += _dot(ats[hh].astype(BF16), douts[hh])
                dqt_s[hh] += _dot(kt_ref[kj, _hslot(hh), :], dzts[hh])
                ls_s[hh] += jnp.sum(lgs[hh], axis=0, keepdims=True)
                gs_s[hh] += jnp.sum(gts[hh], axis=0, keepdims=True)

        def q_loop(qi, _):
            ls_s[...] = jnp.zeros_like(ls_s)
            gs_s[...] = jnp.zeros_like(gs_s)
            dqt_s[...] = jnp.zeros_like(dqt_s)

            def k_loop(kj, _):
                tile(qi, kj, False)
                return 0

            lax.fori_loop(0, qi, k_loop, 0)
            tile(qi, qi, True)
            dqkv_ref[0, pl.ds(pl.multiple_of(qi * TQ, TQ), TQ), :] = _heads_cat(
                [_untranspose(dqt_s[hh]) for hh in range(NH)]).astype(BF16)
            return 0

        lax.fori_loop(0, seq // TQ, q_loop, 0)
        dqkv_ref[1] = _heads_cat([dk_acc[hh] for hh in range(NH)]).astype(BF16)
        dqkv_ref[2] = _heads_cat([dv_acc[hh] for hh in range(NH)]).astype(BF16)
        pl.when(last_step)(chips_finish)

    out = pl.pallas_call(
        body, name="sb_bwd", grid=(batch, N_HEADS // NH),
        out_shape=[jax.ShapeDtypeStruct((3, batch * seq, D_BRANCH), BF16)]
        + [jax.ShapeDtypeStruct(s.shape, s.dtype) for s in chip_sums],
        in_specs=[_group3_spec(0, seq), _group3_spec(1, seq), _group3_spec(2, seq), _tblock_spec(seq),
                  _group_spec(seq), _qrow_spec(seq)] + _hbm_specs(n),
        out_specs=[pl.BlockSpec((3, seq, NH * HEAD_DIM), lambda b, g: (0, b, g))] + _hbm_specs(n),
        scratch_shapes=[pltpu.VMEM((NH, seq, HEAD_DIM), F32), pltpu.VMEM((NH, seq, HEAD_DIM), F32),
                        pltpu.VMEM((NH, 1, TQ), F32), pltpu.VMEM((NH, 1, TQ), F32),
                        pltpu.VMEM((NH, HEAD_SLOT, TQ), F32)] + _chips_sems(n),
        compiler_params=_serial_attn_params(),
    )(qkvb, qkvb, qkvb, kbt, do, ltot, *chip_sums)
    return out[0], out[1:]


def _forget_bwd(dcq_tok, dck_tok, fpre, batch, seq):
    t_len = batch * seq
    tiles = seq // TM

    def rev(i):
        return ((i // tiles) * tiles + (tiles - 1 - i % tiles), 0)

    def body(dcq_ref, dck_ref, f_ref, df_ref, db_ref, carry_ref):
        i = pl.program_id(0)

        @pl.when(i == 0)
        def _():
            db_ref[...] = jnp.zeros_like(db_ref)

        @pl.when(i % tiles == 0)
        def _():
            carry_ref[...] = jnp.zeros_like(carry_ref)

        dc = dcq_ref[...] - dck_ref[...]
        upper = _tri(TM, lambda r, c: c >= r)
        hi, mid, lo = _split3(dc)
        dlogf = carry_ref[...] + _dot(upper, hi) + _dot(upper, mid) + _dot(upper, lo)
        carry_ref[...] = carry_ref[...] + jnp.sum(dc, axis=0, keepdims=True)
        df = dlogf * _sigmoid(-f_ref[...])
        df_ref[...] = df.astype(BF16)
        db_ref[...] += jnp.sum(df, axis=0, keepdims=True)

    return pl.pallas_call(
        body, name="forget_bwd", grid=(t_len // TM,),
        out_shape=(jax.ShapeDtypeStruct((t_len, LANES), BF16), jax.ShapeDtypeStruct((1, LANES), F32)),
        in_specs=[pl.BlockSpec((TM, LANES), rev)] * 3,
        out_specs=(pl.BlockSpec((TM, LANES), rev), _acc_spec((1, LANES))),
        scratch_shapes=[pltpu.VMEM((1, LANES), F32)],
        compiler_params=_seq_params(),
    )(dcq_tok, dck_tok, fpre)


def _mix_fwd(o_fox, o_sb, gl, x, w_bf, w_bs, w_out, b_gate):
    t_len, d = x.shape

    def body(of_ref, os_ref, gl_ref, x_ref, wbf_ref, wbs_ref, wo_ref, bg_ref, x1_ref):
        br_f = _dot(of_ref[...], wbf_ref[...])
        br_s = _dot(os_ref[...], wbs_ref[...])
        ga = _sigmoid(gl_ref[:, :d].astype(F32) + bg_ref[0:1, :])
        gb = _sigmoid(gl_ref[:, d:].astype(F32) + bg_ref[1:2, :])
        merged = ga * br_f + gb * br_s
        x1_ref[...] = x_ref[...] + _dot(merged.astype(BF16), wo_ref[...])

    return pl.pallas_call(
        body, name="mix_fwd", grid=(t_len // TM,),
        out_shape=jax.ShapeDtypeStruct((t_len, d), F32),
        in_specs=[_row_spec(TM, D_BRANCH), _row_spec(TM, D_BRANCH), _row_spec(TM, 2 * d), _row_spec(TM, d),
                  _const_spec(w_bf.shape), _const_spec(w_bs.shape), _const_spec(w_out.shape), _const_spec(b_gate.shape)],
        out_specs=_row_spec(TM, d),
        compiler_params=_seq_params(),
    )(o_fox, o_sb, gl, x, w_bf, w_bs, w_out, b_gate)


def _ff_chunk(d_ff):
    return min(d_ff, 1024)


def _mlp_fwd(x1, g_mlp, w_up, w_down):
    t_len, d = x1.shape
    d_ff = w_up.shape[1]
    ch = _ff_chunk(d_ff)

    def body(x1_ref, g_ref, wu_ref, wd_ref, a_ref, x2_ref):
        x1v = x1_ref[...]
        xn, _ = _rms(x1v)
        h = (xn * g_ref[...]).astype(BF16)
        acc = x1v
        for j in range(d_ff // ch):
            a = _dot(h, wu_ref[:, j * ch:(j + 1) * ch])
            a_ref[:, j * ch:(j + 1) * ch] = a.astype(BF16)
            acc = acc + _dot(jnp.square(jnp.maximum(a, 0.0)).astype(BF16), wd_ref[j * ch:(j + 1) * ch, :])
        x2_ref[...] = acc

    return pl.pallas_call(
        body, name="mlp_fwd", grid=(t_len // TM,),
        out_shape=(jax.ShapeDtypeStruct((t_len, d_ff), BF16), jax.ShapeDtypeStruct((t_len, d), F32)),
        in_specs=[_row_spec(TM, d), _const_spec((1, d)), _const_spec(w_up.shape), _const_spec(w_down.shape)],
        out_specs=(_row_spec(TM, d_ff), _row_spec(TM, d)),
        compiler_params=_seq_params(),
    )(x1, g_mlp, w_up, w_down)


def _head_fwd_bwd(x2, p, target, g_ple, g_final, w_pg, w_ple):
    t_len, d = x2.shape
    d_ple = p.shape[1]

    def body(x2_ref, p_ref, t_ref, gp_ref, gf_ref, wpg_ref, wple_ref,
             dx2_ref, h3_ref, dpre_ref, dpe_ref, loss_ref, dgp_ref, dgf_ref):
        @pl.when(pl.program_id(0) == 0)
        def _():
            loss_ref[...] = jnp.zeros_like(loss_ref)
            dgp_ref[...] = jnp.zeros_like(dgp_ref)
            dgf_ref[...] = jnp.zeros_like(dgf_ref)

        x2v = x2_ref[...]
        x2n, r3 = _rms(x2v)
        h3 = (x2n * gp_ref[...]).astype(BF16)
        h3_ref[...] = h3
        gate = _sigmoid(_dot(h3, wpg_ref[...]))
        pe = _dot(p_ref[...].astype(BF16), wple_ref[...])
        x3n, r4 = _rms(x2v + gate * pe)
        err = x3n * gf_ref[...] - t_ref[...]
        loss_ref[...] += jnp.full(loss_ref.shape, (0.5 / d) * jnp.sum(err * err), F32)
        dx3, dgf = _rms_bwd(err * (1.0 / d), x3n, r4, gf_ref[...])
        dgf_ref[...] += dgf
        dpe_ref[...] = (dx3 * gate).astype(BF16)
        dpre = (dx3 * pe * gate * (1.0 - gate)).astype(BF16)
        dpre_ref[...] = dpre
        dres, dgp = _rms_bwd(_dot_nt(dpre, wpg_ref[...]), x2n, r3, gp_ref[...])
        dgp_ref[...] += dgp
        dx2_ref[...] = dx3 + dres

    shp_b = jax.ShapeDtypeStruct((t_len, d), BF16)
    return pl.pallas_call(
        body, name="head_fwd_bwd", grid=(t_len // TM,),
        out_shape=(jax.ShapeDtypeStruct((t_len, d), F32), shp_b, shp_b, shp_b,
                   jax.ShapeDtypeStruct((1, LANES), F32), jax.ShapeDtypeStruct((1, d), F32),
                   jax.ShapeDtypeStruct((1, d), F32)),
        in_specs=[_row_spec(TM, d), _row_spec(TM, d_ple), _row_spec(TM, d), _const_spec((1, d)), _const_spec((1, d)),
                  _const_spec(w_pg.shape), _const_spec(w_ple.shape)],
        out_specs=(_row_spec(TM, d), _row_spec(TM, d), _row_spec(TM, d), _row_spec(TM, d),
                   _acc_spec((1, LANES)), _acc_spec((1, d)), _acc_spec((1, d))),
        compiler_params=_seq_params(),
    )(x2, p, target, g_ple, g_final, w_pg, w_ple)


def _mlp_bwd(dx2, a, x1, g_mlp, w_up, w_down):
    t_len, d = x1.shape
    d_ff = w_up.shape[1]
    ch = _ff_chunk(d_ff)

    def body(dx2_ref, a_ref, x1_ref, g_ref, wu_ref, wd_ref, dx1_ref, da_ref, h2_ref, dg_ref):
        @pl.when(pl.program_id(0) == 0)
        def _():
            dg_ref[...] = jnp.zeros_like(dg_ref)

        dx2v = dx2_ref[...]
        dx2b = dx2v.astype(BF16)
        xn, r = _rms(x1_ref[...])
        h2_ref[...] = (xn * g_ref[...]).astype(BF16)
        dh = jnp.zeros((TM, d), F32)
        for j in range(d_ff // ch):
            dact = _dot_nt(dx2b, wd_ref[j * ch:(j + 1) * ch, :])
            da = (dact * 2.0 * jnp.maximum(a_ref[:, j * ch:(j + 1) * ch].astype(F32), 0.0)).astype(BF16)
            da_ref[:, j * ch:(j + 1) * ch] = da
            dh = dh + _dot_nt(da, wu_ref[:, j * ch:(j + 1) * ch])
        dres, dg = _rms_bwd(dh, xn, r, g_ref[...])
        dg_ref[...] += dg
        dx1_ref[...] = dx2v + dres

    return pl.pallas_call(
        body, name="mlp_bwd", grid=(t_len // TM,),
        out_shape=(jax.ShapeDtypeStruct((t_len, d), F32), jax.ShapeDtypeStruct((t_len, d_ff), BF16),
                   jax.ShapeDtypeStruct((t_len, d), BF16), jax.ShapeDtypeStruct((1, d), F32)),
        in_specs=[_row_spec(TM, d), _row_spec(TM, d_ff), _row_spec(TM, d), _const_spec((1, d)),
                  _const_spec(w_up.shape), _const_spec(w_down.shape)],
        out_specs=(_row_spec(TM, d), _row_spec(TM, d_ff), _row_spec(TM, d), _acc_spec((1, d))),
        compiler_params=_seq_params(),
    )(dx2, a, x1, g_mlp, w_up, w_down)


def _mix_bwd(dx1, o_fox, o_sb, gl, w_bf, w_bs, w_out, b_gate):
    t_len, d = dx1.shape

    def body(dx1_ref, of_ref, os_ref, gl_ref, wbf_ref, wbs_ref, wo_ref, bg_ref,
             mg_ref, dbf_ref, dbs_ref, dgl_ref, dof_ref, dos_ref, dbg_ref):
        @pl.when(pl.program_id(0) == 0)
        def _():
            dbg_ref[...] = jnp.zeros_like(dbg_ref)

        dmerged = _dot_nt(dx1_ref[...].astype(BF16), wo_ref[...])
        br_f = _dot(of_ref[...], wbf_ref[...])
        br_s = _dot(os_ref[...], wbs_ref[...])
        ga = _sigmoid(gl_ref[:, :d].astype(F32) + bg_ref[0:1, :])
        gb = _sigmoid(gl_ref[:, d:].astype(F32) + bg_ref[1:2, :])
        mg_ref[...] = (ga * br_f + gb * br_s).astype(BF16)
        dbf = (dmerged * ga).astype(BF16)
        dbs = (dmerged * gb).astype(BF16)
        dbf_ref[...] = dbf
        dbs_ref[...] = dbs
        dla = dmerged * br_f * ga * (1.0 - ga)
        dlb = dmerged * br_s * gb * (1.0 - gb)
        dgl_ref[:, :d] = dla.astype(BF16)
        dgl_ref[:, d:] = dlb.astype(BF16)
        dbg_ref[0:1, :] += jnp.sum(dla, axis=0, keepdims=True)
        dbg_ref[1:2, :] += jnp.sum(dlb, axis=0, keepdims=True)
        dof_ref[...] = _dot_nt(dbf, wbf_ref[...]).astype(BF16)
        dos_ref[...] = _dot_nt(dbs, wbs_ref[...]).astype(BF16)

    shp_d = jax.ShapeDtypeStruct((t_len, d), BF16)
    shp_h = jax.ShapeDtypeStruct((t_len, D_BRANCH), BF16)
    return pl.pallas_call(
        body, name="mix_bwd", grid=(t_len // TM,),
        out_shape=(shp_d, shp_d, shp_d, jax.ShapeDtypeStruct((t_len, 2 * d), BF16), shp_h, shp_h,
                   jax.ShapeDtypeStruct((2, d), F32)),
        in_specs=[_row_spec(TM, d), _row_spec(TM, D_BRANCH), _row_spec(TM, D_BRANCH), _row_spec(TM, 2 * d),
                  _const_spec(w_bf.shape), _const_spec(w_bs.shape), _const_spec(w_out.shape), _const_spec(b_gate.shape)],
        out_specs=(_row_spec(TM, d), _row_spec(TM, d), _row_spec(TM, d), _row_spec(TM, 2 * d),
                   _row_spec(TM, D_BRANCH), _row_spec(TM, D_BRANCH), _acc_spec((2, d))),
        compiler_params=_seq_params(),
    )(dx1, o_fox, o_sb, gl, w_bf, w_bs, w_out, b_gate)


def _inproj_bwd(dqk_f, dv_f, dqkv_b, dgl, df, dx1, x, g_mix, w_pad):
    t_len, d = x.shape
    lay, _ = _pad_layout(d)
    slot_w = N_HEADS * HEAD_SLOT

    def body(dqk_ref, dvf_ref, db_ref, dgl_ref, df_ref, dx1_ref, x_ref, g_ref, w_ref, dx_ref, h1_ref, dg_ref):
        @pl.when(pl.program_id(0) == 0)
        def _():
            dg_ref[...] = jnp.zeros_like(dg_ref)

        def back(piece, name):
            lo, hi = lay[name]
            return _dot_nt(piece, w_ref[:, lo:hi])

        xn, r = _rms(x_ref[...])
        h1_ref[...] = (xn * g_ref[...]).astype(BF16)
        dh = (back(df_ref[...], "forget") + back(dgl_ref[...], "gates") + back(dqk_ref[0], "qf")
              + back(dqk_ref[1], "kf") + back(dvf_ref[...], "vf") + back(db_ref[0], "qb") + back(db_ref[1], "kb")
              + back(db_ref[2], "vb"))
        dres, dg = _rms_bwd(dh, xn, r, g_ref[...])
        dg_ref[...] += dg
        dx_ref[...] = dx1_ref[...] + dres

    return pl.pallas_call(
        body, name="inproj_bwd", grid=(t_len // TM,),
        out_shape=(jax.ShapeDtypeStruct((t_len, d), F32), jax.ShapeDtypeStruct((t_len, d), BF16),
                   jax.ShapeDtypeStruct((1, d), F32)),
        in_specs=[_row3_spec(2, TM, slot_w), _row_spec(TM, D_BRANCH), _row3_spec(3, TM, D_BRANCH),
                  _row_spec(TM, 2 * d), _row_spec(TM, LANES), _row_spec(TM, d), _row_spec(TM, d), _const_spec((1, d)),
                  _const_spec(w_pad.shape)],
        out_specs=(_row_spec(TM, d), _row_spec(TM, d), _acc_spec((1, d))),
        compiler_params=_seq_params(),
    )(dqk_f, dv_f, dqkv_b, dgl, df, dx1, x, g_mix, w_pad)


def _cols_to_slabs(full):
    r, c8 = full.shape
    return full.reshape(r, N_DEV, c8 // N_DEV).transpose(1, 0, 2)


def _slabs_to_cols(slabs):
    n, r, c = slabs.shape
    return slabs.transpose(1, 0, 2).reshape(r, n * c)


def _win_sizes(d):
    return (D_BRANCH, D_BRANCH, D_BRANCH, N_HEADS, D_BRANCH, D_BRANCH, D_BRANCH, d, d)


def _split_win(w, d):
    out, off = [], 0
    for s in _win_sizes(d):
        out.append(w[:, off:off + s])
        off += s
    return out


def _to_slots(w):
    r = w.shape[0]
    return jnp.pad(w.reshape(r, N_HEADS, HEAD_DIM), ((0, 0), (0, 0), (0, HEAD_SLOT - HEAD_DIM))).reshape(r, -1)


def _from_slots(w):
    r = w.shape[0]
    return w.reshape(r, N_HEADS, HEAD_SLOT)[:, :, :HEAD_DIM].reshape(r, N_HEADS * HEAD_DIM)


def _pad_win(w_full, d):
    qa, ka, va, fa, qb, kb, vb, ga, gb = _split_win(w_full, d)
    scale = HEAD_DIM ** -0.5
    fpad = jnp.pad(fa, ((0, 0), (0, LANES - N_HEADS)))
    return jnp.concatenate([_to_slots(qa * scale), _to_slots(ka), va, qb * scale, kb, vb, ga, gb, fpad], axis=1)


def _unpad_dwin(dqk_f, dv_f, dqkv_b, dgates, dforget, d):
    scale = HEAD_DIM ** -0.5
    return jnp.concatenate([_from_slots(dqk_f[0]) * scale, _from_slots(dqk_f[1]), dv_f, dforget[:, :N_HEADS],
                            dqkv_b[0] * scale, dqkv_b[1], dqkv_b[2], dgates], axis=1)


def _c_lane_constants():
    head = jnp.arange(LANES)[:, None]
    lane = jnp.arange(N_HEADS * HEAD_SLOT)[None, :]
    in_head = (lane // HEAD_SLOT == head) & (head < N_HEADS)

    def place(first):
        return jnp.stack([(in_head & (lane % HEAD_SLOT == first + j)) for j in range(3)]).astype(BF16)

    def ones(first):
        off = lane % HEAD_SLOT
        return ((off >= first) & (off < first + 3)).astype(F32)

    return place(C_TERMS_Q), place(C_TERMS_K), ones(C_ONES_Q), ones(C_ONES_K)


def _pad_rows(a, rows):
    return jnp.pad(a, [(0, 0)] * (a.ndim - 2) + [(0, rows - a.shape[-2]), (0, 0)])


def kernel(x, p, g_mix, w_in, b_forget, b_gate, w_branch_fox, w_branch_sb, w_out, g_mlp, w_up, w_down, g_ple, w_ple_gate, w_ple, g_final, loss_target, m_g_mix, m_w_in, m_b_forget, m_b_gate, m_w_branch_fox, m_w_branch_sb, m_w_out, m_g_mlp, m_w_up, m_w_down, m_g_ple, m_w_ple_gate, m_w_ple, m_g_final, v_g_mix, v_w_in, v_b_forget, v_b_gate, v_w_branch_fox, v_w_branch_sb, v_w_out, v_g_mlp, v_w_up, v_w_down, v_g_ple, v_w_ple_gate, v_w_ple, v_g_final):
    batch, seq, d = x.shape
    t_len = batch * seq
    d_ple = p.shape[-1]
    d_ff = w_up.shape[-1] * N_DEV
    dn = d // N_DEV
    fn = d_ff // N_DEV
    my_c = lax.axis_index("c")
    my_dev = 4 * lax.axis_index("x") + 2 * lax.axis_index("y") + my_c

    bg_hi = b_gate[0].astype(BF16)
    bg_r = b_gate[0] - bg_hi.astype(F32)
    bg_mid = bg_r.astype(BF16)
    bg_lo = (bg_r - bg_mid.astype(F32)).astype(BF16)
    narrow_rows = 2 * D_BRANCH + d_ple + 6
    narrow_rows_pad = -(-narrow_rows // 16) * 16
    wide = jnp.concatenate([w_out[0], w_down[0], w_ple_gate[0]], axis=0).astype(BF16)
    narrow = _pad_rows(jnp.concatenate(
        [w_branch_fox[0].astype(BF16), w_branch_sb[0].astype(BF16), w_ple[0].astype(BF16), bg_hi, bg_mid, bg_lo],
        axis=0), narrow_rows_pad)
    g_in, = _all_gather([w_in[0].astype(BF16)])
    w_pad = _pad_win(_slabs_to_cols(g_in), d)
    bf_pad = jnp.pad(b_forget, ((0, 0), (0, LANES - N_HEADS)))
    place_q, place_k, ones_q, ones_k = _c_lane_constants()

    x2d = x.reshape(t_len, d)
    p2d = p.reshape(t_len, d_ple)
    tgt2d = loss_target.reshape(t_len, d)
    qf, kf, kft, vf, vft, qkvb, kbt, vbt, gl, fpre = _inproj_fwd(
        x2d, g_mix, w_pad, bf_pad, place_q, place_k, ones_q, ones_k, seq)
    o_sb, ltot, (g_up, g_wide, g_narrow) = _sb_fwd(qkvb, vbt, batch, seq, [w_up[0].astype(BF16), wide, narrow])
    o_fox, lse = _fox_fwd(qf, kf, vft, batch, seq)
    w_up_full = _slabs_to_cols(g_up)
    w_out_full = g_wide[:, :dn].reshape(d, d)
    w_down_full = g_wide[:, dn:dn + fn].reshape(d_ff, d)
    w_pg_full = g_wide[:, dn + fn:].reshape(d, d)
    w_bf_full = _slabs_to_cols(g_narrow[:, :D_BRANCH])
    w_bs_full = _slabs_to_cols(g_narrow[:, D_BRANCH:2 * D_BRANCH])
    w_ple_full = _slabs_to_cols(g_narrow[:, 2 * D_BRANCH:2 * D_BRANCH + d_ple])
    bg_terms = g_narrow[:, 2 * D_BRANCH + d_ple:narrow_rows].astype(F32)
    b_gate_full = _slabs_to_cols(bg_terms[:, 0:2] + bg_terms[:, 2:4] + bg_terms[:, 4:6])
    x1 = _mix_fwd(o_fox, o_sb, gl, x2d, w_bf_full, w_bs_full, w_out_full, b_gate_full)
    a_up, x2 = _mlp_fwd(x1, g_mlp, w_up_full, w_down_full)

    dx2, h3, dpre, dpe, loss_acc, dg_ple, dg_final = _head_fwd_bwd(
        x2, p2d, tgt2d, g_ple, g_final.reshape(1, d), w_pg_full, w_ple_full)
    dx1, da_up, h2, dg_mlp = _mlp_bwd(dx2, a_up, x1, g_mlp, w_up_full, w_down_full)
    merged, dbr_f, dbr_s, dgl, do_fox, do_sb, dbg = _mix_bwd(
        dx1, o_fox, o_sb, gl, w_bf_full, w_bs_full, w_out_full, b_gate_full)

    gw_ple = _matmul_tn("dw_ple", p2d, dpe)
    gw_pg = _matmul_tn("dw_ple_gate", h3, dpre)
    gw_down = _matmul_tn("dw_down", a_up, dx2, relu2=True)
    gw_up = _matmul_tn("dw_up", h2, da_up)
    gw_out = _matmul_tn("dw_out", merged, dx1)
    gw_bf = _matmul_tn("dw_branch_fox", o_fox, dbr_f)
    gw_bs = _matmul_tn("dw_branch_sb", o_sb, dbr_s)
    part_up = _cols_to_slabs(gw_up).astype(BF16)
    part_wide = jnp.concatenate([gw_out.reshape(N_DEV, dn, d), gw_down.reshape(N_DEV, fn, d),
                                 gw_pg.reshape(N_DEV, dn, d)], axis=1).astype(BF16)
    part_narrow = _pad_rows(jnp.concatenate(
        [_cols_to_slabs(gw_bf), _cols_to_slabs(gw_bs), _cols_to_slabs(gw_ple)], axis=1).astype(BF16), narrow_rows_pad)
    early = [part_up, part_wide, part_narrow]
    early_sums = [_pair_add("pair_add_%d" % i, pt, rc, my_c)
                  for i, (pt, rc) in enumerate(zip(early, _rs_core_pair("reduce_scatter_core_pair_early", early)))]

    dqk_f, dv_f, dc_queries, dc_keys = _fox_bwd(qf, kf, kft, vf, o_fox, do_fox, lse, batch, seq)
    dqkv_b, (s_up, s_wide, s_narrow) = _sb_bwd(qkvb, kbt, do_sb, ltot, batch, seq, early_sums)
    dcq_tok = dc_queries.reshape(batch, N_HEADS, seq).transpose(0, 2, 1).reshape(t_len, N_HEADS)
    dck_tok = dc_keys[..., :NH].transpose(0, 2, 1, 3).reshape(t_len, N_HEADS)
    lane_pad = ((0, 0), (0, LANES - N_HEADS))
    df, db_forget = _forget_bwd(jnp.pad(dcq_tok, lane_pad), jnp.pad(dck_tok, lane_pad), fpre, batch, seq)
    grad_x, h1, dg_mix = _inproj_bwd(dqk_f, dv_f, dqkv_b, dgl, df, dx1, x2d, g_mix, w_pad)

    gw_in = _unpad_dwin(_matmul_tn("dw_in_fox_qk", h1, dqk_f), _matmul_tn("dw_in_fox_v", h1, dv_f),
                        _matmul_tn("dw_in_sb", h1, dqkv_b), _matmul_tn("dw_in_gates", h1, dgl),
                        _matmul_tn("dw_in_forget", h1, df), d)
    part_in = _cols_to_slabs(gw_in).astype(BF16)
    recv_in, = _rs_core_pair("reduce_scatter_core_pair_w_in", [part_in])
    s_in, = _rs_chips([_pair_add("pair_add_w_in", part_in, recv_in, my_c)])

    small = jnp.concatenate([
        dg_mix, dg_mlp, dg_ple, dg_final, jnp.pad(db_forget[:, :N_HEADS], ((0, 0), (0, d - N_HEADS))), dbg,
        jnp.pad(loss_acc[:, :1], ((0, 0), (0, d - 1)))], axis=0)
    small = _all_reduce_small(small)
    loss = small[7, 0]
    small_grads = {
        "g_mix": small[0:1], "g_mlp": small[1:2], "g_ple": small[2:3], "g_final": small[3:4],
        "b_forget": small[4:5, :N_HEADS],
        "b_gate": lax.dynamic_slice_in_dim(small[5:7], my_dev * dn, dn, axis=1),
    }

    weights = {"g_mix": g_mix, "w_in": w_in, "b_forget": b_forget, "b_gate": b_gate, "w_branch_fox": w_branch_fox,
               "w_branch_sb": w_branch_sb, "w_out": w_out, "g_mlp": g_mlp, "w_up": w_up, "w_down": w_down,
               "g_ple": g_ple, "w_ple_gate": w_ple_gate, "w_ple": w_ple, "g_final": g_final}
    m_in = {"g_mix": m_g_mix, "w_in": m_w_in, "b_forget": m_b_forget, "b_gate": m_b_gate,
            "w_branch_fox": m_w_branch_fox, "w_branch_sb": m_w_branch_sb, "w_out": m_w_out, "g_mlp": m_g_mlp,
            "w_up": m_w_up, "w_down": m_w_down, "g_ple": m_g_ple, "w_ple_gate": m_w_ple_gate, "w_ple": m_w_ple,
            "g_final": m_g_final}
    v_in = {"g_mix": v_g_mix, "w_in": v_w_in, "b_forget": v_b_forget, "b_gate": v_b_gate,
            "w_branch_fox": v_w_branch_fox, "w_branch_sb": v_w_branch_sb, "w_out": v_w_out, "g_mlp": v_g_mlp,
            "w_up": v_w_up, "w_down": v_w_down, "g_ple": v_g_ple, "w_ple_gate": v_w_ple_gate, "w_ple": v_w_ple,
            "g_final": v_g_final}
    names = list(weights)

    def as2d(a):
        return a.reshape(-1, a.shape[-1])

    result = {}
    big = {"w_in": (s_in, 0), "w_up": (s_up, 0), "w_out": (s_wide, 0), "w_down": (s_wide, dn),
           "w_ple_gate": (s_wide, dn + fn), "w_branch_fox": (s_narrow, 0), "w_branch_sb": (s_narrow, D_BRANCH),
           "w_ple": (s_narrow, 2 * D_BRANCH)}
    for n, (parts, off) in big.items():
        result[n] = _adamw_parts("adamw_" + n, as2d(weights[n]), parts, off, as2d(m_in[n]), as2d(v_in[n]))
    small_names = list(small_grads)
    small_out = _adamw_small([(as2d(weights[n]), small_grads[n], as2d(m_in[n]), as2d(v_in[n])) for n in small_names])
    for n, (dlt, nm, nv) in zip(small_names, small_out):
        result[n] = (small_grads[n], dlt, nm, nv)
    outs = [[result[n][k].reshape(weights[n].shape) for n in names] for k in range(4)]
    return (loss, grad_x.reshape(x.shape), *outs[0], *outs[1], *outs[2], *outs[3])
```

```python
import jax
import jax.numpy as jnp
from jax import lax
from jax.experimental import pallas as pl
from jax.experimental.pallas import tpu as pltpu

F32 = jnp.float32
BF16 = jnp.bfloat16

HEAD_DIM = 64
N_HEADS = 8
D_BRANCH = N_HEADS * HEAD_DIM
EPS = 1e-6
ADAM_LR = 0.001
ADAM_B1 = 0.9
ADAM_B2 = 0.999
ADAM_EPS = 1e-08
ADAM_WD = 0.01
ADAM_STEP = 10

N_DEV = 8
LANES = 128
TM = 256
TQ = 256
TK = 256
NH = 4
HEAD_SLOT = 128
C_TERMS_Q = 64
C_ONES_K = 64
C_TERMS_K = 67
C_ONES_Q = 67
NEG = -1e30
VMEM_LIMIT = 56 * 1024 * 1024
MESH = pl.DeviceIdType.MESH


def _dot(a, b):
    return jnp.dot(a, b, preferred_element_type=F32)


def _dot_nt(a, b):
    return lax.dot_general(a, b, (((1,), (1,)), ((), ())), preferred_element_type=F32)


def _dot_tn(a, b):
    return lax.dot_general(a, b, (((0,), (0,)), ((), ())), preferred_element_type=F32)


def _sigmoid(x):
    return 1.0 / (1.0 + jnp.exp(-x))


def _softplus(x):
    return jnp.maximum(x, 0.0) + jnp.log(1.0 + jnp.exp(-jnp.abs(x)))


def _split2(x):
    hi = x.astype(BF16)
    lo = (x - hi.astype(F32)).astype(BF16)
    return hi, lo


def _split3(x):
    hi = x.astype(BF16)
    r = x - hi.astype(F32)
    mid = r.astype(BF16)
    lo = (r - mid.astype(F32)).astype(BF16)
    return hi, mid, lo


def _rows_dot_mask(x, mask_bf16):
    hi, lo = _split2(x)
    return _dot(hi, mask_bf16) + _dot(lo, mask_bf16)


def _tri(n, rel):
    r = lax.broadcasted_iota(jnp.int32, (n, n), 0)
    c = lax.broadcasted_iota(jnp.int32, (n, n), 1)
    return rel(r, c).astype(BF16)


def _rms(x):
    r = lax.rsqrt(jnp.mean(x * x, axis=-1, keepdims=True) + EPS)
    return x * r, r


def _rms_bwd(dh, xn, r, g):
    dxn = dh * g
    dx = r * (dxn - xn * jnp.mean(dxn * xn, axis=-1, keepdims=True))
    return dx, jnp.sum(dh * xn, axis=0, keepdims=True)


def _row_spec(tm, cols):
    return pl.BlockSpec((tm, cols), lambda i: (i, 0))


def _row3_spec(g, tm, cols):
    return pl.BlockSpec((g, tm, cols), lambda i: (0, i, 0))


def _const_spec(shape):
    nd = len(shape)
    return pl.BlockSpec(shape, lambda i: (0,) * nd, pipeline_mode=pl.Buffered(1))


def _acc_spec(shape):
    nd = len(shape)
    return pl.BlockSpec(shape, lambda i: (0,) * nd)


def _seq_params():
    return pltpu.CompilerParams(dimension_semantics=("arbitrary",), vmem_limit_bytes=VMEM_LIMIT)


def _mesh_pos():
    return lax.axis_index("x"), lax.axis_index("y"), lax.axis_index("c")


def _other_chips(x, y):
    return [(1 - x, y), (x, 1 - y), (1 - x, 1 - y)]


def _hbm_specs(n):
    return [pl.BlockSpec(memory_space=pl.ANY)] * n


def _gather_plan(x_refs, out_refs, send_sems, recv_sems, local_sems):
    n = len(x_refs)
    x, y, c = _mesh_pos()
    me, sibling = (x, y, c), (x, y, 1 - c)
    chips = _other_chips(x, y)

    def index(px, py, pc):
        return 4 * px + 2 * py + pc

    def copy(a, k, block, to, src=None):
        slab = out_refs[a].at[index(*block)]
        return pltpu.make_async_remote_copy(
            src_ref=slab if src is None else src, dst_ref=slab,
            send_sem=send_sems.at[7 * a + k], recv_sem=recv_sems.at[7 * a + k], device_id=to, device_id_type=MESH)

    mine = [pltpu.make_async_copy(x_refs[a], out_refs[a].at[index(*me)], local_sems.at[a]) for a in range(n)]
    first = []
    for a in range(n):
        first.append(copy(a, 0, me, sibling, src=x_refs[a]))
        first += [copy(a, 1 + j, me, (cx, cy, c), src=x_refs[a]) for j, (cx, cy) in enumerate(chips)]

    def start():
        for cp in mine + first:
            cp.start()

    def finish():
        passed = []
        for j, (cx, cy) in enumerate(chips):
            for a in range(n):
                copy(a, 1 + j, (cx, cy, c), me).wait_recv()
                passed.append(copy(a, 4 + j, (cx, cy, c), sibling))
                passed[-1].start()
        for a in range(n):
            copy(a, 0, sibling, me).wait_recv()
            for j, (cx, cy) in enumerate(chips):
                copy(a, 4 + j, (cx, cy, 1 - c), me).wait_recv()
        for cp in first + passed:
            cp.wait_send()
        for cp in mine:
            cp.wait()

    return start, finish


def _gather_shapes(shards):
    return [jax.ShapeDtypeStruct((N_DEV,) + s.shape, s.dtype) for s in shards]


def _gather_sems(n):
    return [pltpu.SemaphoreType.DMA((7 * n,)), pltpu.SemaphoreType.DMA((7 * n,)), pltpu.SemaphoreType.DMA((n,))]


def _all_gather(shards):
    n = len(shards)

    def body(*refs):
        start, finish = _gather_plan(refs[:n], refs[n:2 * n], *refs[2 * n:])
        start()
        finish()

    return pl.pallas_call(
        body, name="all_gather_weights", out_shape=_gather_shapes(shards),
        in_specs=_hbm_specs(n), out_specs=_hbm_specs(n), scratch_shapes=_gather_sems(n),
    )(*shards)


def _rs_core_pair(name, partials):
    n = len(partials)

    def body(*refs):
        p_refs, recv_refs = refs[:n], refs[n:2 * n]
        send_sems, recv_sems = refs[2 * n:]
        x, y, c = _mesh_pos()
        for a in range(n):
            for chip in range(4):
                pltpu.make_async_remote_copy(
                    src_ref=p_refs[a].at[2 * chip + (1 - c)], dst_ref=recv_refs[a].at[chip],
                    send_sem=send_sems.at[a], recv_sem=recv_sems.at[a],
                    device_id=(x, y, 1 - c), device_id_type=MESH).start()
        for a in range(n):
            pltpu.make_async_remote_copy(
                src_ref=recv_refs[a], dst_ref=recv_refs[a], send_sem=send_sems.at[a], recv_sem=recv_sems.at[a],
                device_id=(x, y, 1 - c), device_id_type=MESH).wait()

    return pl.pallas_call(
        body, name=name,
        out_shape=[jax.ShapeDtypeStruct((4,) + s.shape[1:], s.dtype) for s in partials],
        in_specs=_hbm_specs(n), out_specs=_hbm_specs(n),
        scratch_shapes=[pltpu.SemaphoreType.DMA((n,)), pltpu.SemaphoreType.DMA((n,))],
    )(*partials)


def _chips_plan(cs_refs, out_refs, send_sems, recv_sems, local_sems):
    n = len(cs_refs)
    x, y, c = _mesh_pos()
    chip = 2 * x + y
    chips = _other_chips(x, y)
    mine = [pltpu.make_async_copy(cs_refs[a].at[chip], out_refs[a].at[chip], local_sems.at[a]) for a in range(n)]
    sends = [pltpu.make_async_remote_copy(
        src_ref=cs_refs[a].at[2 * cx + cy], dst_ref=out_refs[a].at[chip],
        send_sem=send_sems.at[3 * a + j], recv_sem=recv_sems.at[3 * a + j],
        device_id=(cx, cy, c), device_id_type=MESH) for a in range(n) for j, (cx, cy) in enumerate(chips)]

    def start():
        for cp in mine + sends:
            cp.start()

    def finish():
        for a in range(n):
            for j, (cx, cy) in enumerate(chips):
                pltpu.make_async_remote_copy(
                    src_ref=cs_refs[a].at[chip], dst_ref=out_refs[a].at[2 * cx + cy],
                    send_sem=send_sems.at[3 * a + j], recv_sem=recv_sems.at[3 * a + j],
                    device_id=(x, y, c), device_id_type=MESH).wait_recv()
        for cp in sends:
            cp.wait_send()
        for cp in mine:
            cp.wait()

    return start, finish


def _chips_sems(n):
    return [pltpu.SemaphoreType.DMA((3 * n,)), pltpu.SemaphoreType.DMA((3 * n,)), pltpu.SemaphoreType.DMA((n,))]


def _rs_chips(chip_sums):
    n = len(chip_sums)

    def body(*refs):
        start, finish = _chips_plan(refs[:n], refs[n:2 * n], *refs[2 * n:])
        start()
        finish()

    return pl.pallas_call(
        body, name="reduce_scatter_chips",
        out_shape=[jax.ShapeDtypeStruct(s.shape, s.dtype) for s in chip_sums],
        in_specs=_hbm_specs(n), out_specs=_hbm_specs(n), scratch_shapes=_chips_sems(n),
    )(*chip_sums)


def _all_reduce_small(vec):
    rows, cols = vec.shape

    def body(x_ref, land_ref, sum_ref, send_sems, recv_sems):
        x, y, c = _mesh_pos()
        me = 4 * x + 2 * y + c
        land_ref[me] = x_ref[...]
        flips = [(fx, fy, fc) for fx in (0, 1) for fy in (0, 1) for fc in (0, 1)][1:]

        def flipped(f):
            return tuple((1 - v) if b else v for v, b in zip((x, y, c), f))

        sends = []
        for k, f in enumerate(flips):
            sends.append(pltpu.make_async_remote_copy(
                src_ref=x_ref, dst_ref=land_ref.at[me], send_sem=send_sems.at[k], recv_sem=recv_sems.at[k],
                device_id=flipped(f), device_id_type=MESH))
            sends[-1].start()
        for k, f in enumerate(flips):
            px, py, pc = flipped(f)
            pltpu.make_async_remote_copy(
                src_ref=x_ref, dst_ref=land_ref.at[4 * px + 2 * py + pc], send_sem=send_sems.at[k],
                recv_sem=recv_sems.at[k], device_id=(x, y, c), device_id_type=MESH).wait_recv()
        for cp in sends:
            cp.wait_send()
        total = land_ref[0]
        for d in range(1, N_DEV):
            total = total + land_ref[d]
        sum_ref[...] = total

    vm = pl.BlockSpec(memory_space=pltpu.VMEM)
    return pl.pallas_call(
        body, name="all_reduce_small",
        out_shape=(jax.ShapeDtypeStruct((N_DEV, rows, cols), F32), jax.ShapeDtypeStruct((rows, cols), F32)),
        in_specs=[vm], out_specs=(vm, vm),
        scratch_shapes=[pltpu.SemaphoreType.DMA((7,)), pltpu.SemaphoreType.DMA((7,))],
    )(vec)[1]


def _block_rows(rows, cols, itemsize, align, row_off=0):
    best = None
    for t in range(align, rows + 1, align):
        if rows % t == 0 and row_off % t == 0 and t * cols * itemsize <= (1 << 20):
            best = t
    return rows if best is None else best


def _pair_add(name, partial, recv, my_c):
    _, rows, cols = partial.shape
    br = _block_rows(rows, cols, 2, 16)

    def body(c_ref, a_ref, b_ref, o_ref):
        o_ref[...] = (a_ref[...].astype(F32) + b_ref[...].astype(F32)).astype(BF16)

    return pl.pallas_call(
        body, name=name,
        grid_spec=pltpu.PrefetchScalarGridSpec(
            num_scalar_prefetch=1, grid=(4, rows // br),
            in_specs=[pl.BlockSpec((None, None, br, cols), lambda j, i, c_ref: (j, c_ref[0], i, 0)),
                      pl.BlockSpec((None, br, cols), lambda j, i, c_ref: (j, i, 0))],
            out_specs=pl.BlockSpec((None, br, cols), lambda j, i, c_ref: (j, i, 0))),
        out_shape=jax.ShapeDtypeStruct((4, rows, cols), BF16),
    )(my_c.reshape(1).astype(jnp.int32), partial.reshape(4, 2, rows, cols), recv)


def _adam_update(w, g, m, v):
    nm = ADAM_B1 * m + (1.0 - ADAM_B1) * g
    nv = ADAM_B2 * v + (1.0 - ADAM_B2) * (g * g)
    m_hat = nm / (1.0 - ADAM_B1 ** ADAM_STEP)
    v_hat = nv / (1.0 - ADAM_B2 ** ADAM_STEP)
    return -ADAM_LR * (m_hat / (jnp.sqrt(v_hat) + ADAM_EPS) + ADAM_WD * w), nm, nv


def _adamw_parts(name, w, parts, row_off, m, v):
    rows, cols = w.shape
    tr = _block_rows(rows, cols, 4, 16, row_off)
    assert rows % tr == 0 and row_off % tr == 0
    off = row_off // tr

    def body(w_ref, p_ref, m_ref, v_ref, g_ref, d_ref, nm_ref, nv_ref):
        g = p_ref[0].astype(F32)
        for j in range(1, 4):
            g = g + p_ref[j].astype(F32)
        g_ref[...] = g
        d_ref[...], nm_ref[...], nv_ref[...] = _adam_update(w_ref[...], g, m_ref[...], v_ref[...])

    spec = pl.BlockSpec((tr, cols), lambda i: (i, 0))
    shp = jax.ShapeDtypeStruct((rows, cols), F32)
    return pl.pallas_call(
        body, name=name, grid=(rows // tr,), out_shape=(shp,) * 4,
        in_specs=[spec, pl.BlockSpec((4, tr, cols), lambda i: (0, off + i, 0)), spec, spec], out_specs=(spec,) * 4,
    )(w, parts, m, v)


def _adamw_small(tensors):
    n = len(tensors)

    def body(*refs):
        ins, outs = refs[:4 * n], refs[4 * n:]
        for t in range(n):
            w_ref, g_ref, m_ref, v_ref = ins[4 * t:4 * t + 4]
            d, nm, nv = _adam_update(w_ref[...], g_ref[...], m_ref[...], v_ref[...])
            outs[3 * t][...], outs[3 * t + 1][...], outs[3 * t + 2][...] = d, nm, nv

    vm = pl.BlockSpec(memory_space=pltpu.VMEM)
    out = pl.pallas_call(
        body, name="adamw_small",
        out_shape=[jax.ShapeDtypeStruct(t[0].shape, F32) for t in tensors for _ in range(3)],
        in_specs=[vm] * (4 * n), out_specs=[vm] * (3 * n),
    )(*[a for t in tensors for a in t])
    return [tuple(out[3 * t:3 * t + 3]) for t in range(n)]


def _matmul_tn(name, a, b, relu2=False, slabs=False):
    squeeze = b.ndim == 2
    if squeeze:
        b = b[None]
    t_len, k_len = a.shape
    groups, _, n_len = b.shape
    tt = min(t_len, 512)
    tk = min(k_len, 1024)
    tn = n_len // N_DEV if slabs else min(n_len, 1024)
    nt = t_len // tt
    assert not slabs or (squeeze and tn <= 1024)

    def body(a_ref, b_ref, o_ref, acc_ref):
        @pl.when(pl.program_id(3) == 0)
        def _():
            acc_ref[...] = jnp.zeros_like(acc_ref)

        av = a_ref[...]
        if relu2:
            av = jnp.square(jnp.maximum(av.astype(F32), 0.0))
        acc_ref[...] += _dot_tn(av.astype(BF16), b_ref[...].astype(BF16))

        @pl.when(pl.program_id(3) == nt - 1)
        def _():
            o_ref[...] = acc_ref[...].astype(BF16)

    if slabs:
        out_shape = jax.ShapeDtypeStruct((N_DEV, k_len, tn), BF16)
        out_spec = pl.BlockSpec((None, tk, tn), lambda g, i, j, t: (j, i, 0))
    else:
        out_shape = jax.ShapeDtypeStruct((groups, k_len, n_len), BF16)
        out_spec = pl.BlockSpec((None, tk, tn), lambda g, i, j, t: (g, i, j))
    out = pl.pallas_call(
        body, name=name, grid=(groups, k_len // tk, n_len // tn, nt), out_shape=out_shape,
        in_specs=[pl.BlockSpec((tt, tk), lambda g, i, j, t: (t, i)),
                  pl.BlockSpec((None, tt, tn), lambda g, i, j, t: (g, t, j))],
        out_specs=out_spec,
        scratch_shapes=[pltpu.VMEM((tk, tn), F32)],
        compiler_params=pltpu.CompilerParams(
            dimension_semantics=("parallel", "parallel", "parallel", "arbitrary"), vmem_limit_bytes=VMEM_LIMIT),
    )(a, b)
    return out if slabs else (out[0] if squeeze else out)


def _pad_layout(d):
    names = ("qf", "kf", "vf", "qb", "kb", "vb", "gates", "forget")
    sizes = (N_HEADS * HEAD_SLOT, N_HEADS * HEAD_SLOT, D_BRANCH, D_BRANCH, D_BRANCH, D_BRANCH, 2 * d, LANES)
    out, off = {}, 0
    for n, s in zip(names, sizes):
        out[n] = (off, off + s)
        off += s
    return out, off


def _slot_rows(xt, extra):
    parts = []
    for h in range(N_HEADS):
        parts += [xt[h * HEAD_DIM:(h + 1) * HEAD_DIM, :], extra]
    return jnp.concatenate(parts, axis=0)


def _inproj_fwd(x, g_mix, w_pad, bf_pad, place_q, place_k, ones_q, ones_k, seq):
    t_len, d = x.shape
    lay, _ = _pad_layout(d)
    tiles_per_seq = seq // TM
    slot_w = N_HEADS * HEAD_SLOT

    def body(x_ref, g_ref, w_ref, bf_ref, pq_ref, pk_ref, oq_ref, ok_ref,
             qf_ref, kf_ref, kft_ref, vf_ref, vft_ref, qkvb_ref, kbt_ref, vbt_ref, gl_ref, fpre_ref, carry_ref):
        @pl.when(pl.program_id(0) % tiles_per_seq == 0)
        def _():
            carry_ref[...] = jnp.zeros_like(carry_ref)

        def proj(name):
            lo, hi = lay[name]
            return _dot(h, w_ref[:, lo:hi])

        xn, _ = _rms(x_ref[...])
        h = (xn * g_ref[...]).astype(BF16)
        fpre = proj("forget") + bf_ref[...]
        fpre_ref[...] = fpre
        logf = -_softplus(-fpre)
        lower = _tri(TM, lambda r, c: c <= r)
        hi, mid, lo = _split3(logf)
        c_val = carry_ref[...] + _dot(lower, hi) + _dot(lower, mid) + _dot(lower, lo)
        carry_ref[...] = carry_ref[...] + jnp.sum(logf, axis=0, keepdims=True)
        c3 = _split3(c_val)
        qf_ref[...] = (proj("qf") + sum(_dot(c3[j], pq_ref[j]) for j in range(3)) + oq_ref[...]).astype(BF16)
        kf = proj("kf") - sum(_dot(c3[j], pk_ref[j]) for j in range(3)) + ok_ref[...]
        kf_ref[...] = kf.astype(BF16)
        kft_ref[0] = kf.T.astype(BF16)
        row0 = (lax.broadcasted_iota(jnp.int32, (HEAD_DIM, TM), 0) == 0).astype(F32)
        zeros = jnp.zeros((HEAD_DIM, TM), F32)
        vf = proj("vf")
        vf_ref[...] = vf.astype(BF16)
        vft_ref[0] = _slot_rows(vf.T, row0).astype(BF16)
        qkvb_ref[0] = proj("qb").astype(BF16)
        kb = proj("kb")
        qkvb_ref[1] = kb.astype(BF16)
        kbt_ref[0] = _slot_rows(kb.T, zeros).astype(BF16)
        vb = proj("vb")
        qkvb_ref[2] = vb.astype(BF16)
        vbt_ref[0] = _slot_rows(vb.T, row0).astype(BF16)
        gl_ref[...] = proj("gates").astype(BF16)

    n_tiles = t_len // TM
    slot_shape = jax.ShapeDtypeStruct((t_len, slot_w), BF16)
    t_shape = jax.ShapeDtypeStruct((n_tiles, slot_w, TM), BF16)
    t_spec = pl.BlockSpec((1, slot_w, TM), lambda i: (i, 0, 0))
    return pl.pallas_call(
        body, name="inproj_fwd", grid=(n_tiles,),
        out_shape=(slot_shape, slot_shape, t_shape, jax.ShapeDtypeStruct((t_len, D_BRANCH), BF16), t_shape,
                   jax.ShapeDtypeStruct((3, t_len, D_BRANCH), BF16), t_shape, t_shape,
                   jax.ShapeDtypeStruct((t_len, 2 * d), BF16), jax.ShapeDtypeStruct((t_len, LANES), F32)),
        in_specs=[_row_spec(TM, d), _const_spec((1, d)), _const_spec(w_pad.shape), _const_spec((1, LANES)),
                  _const_spec(place_q.shape), _const_spec(place_k.shape), _const_spec((1, slot_w)),
                  _const_spec((1, slot_w))],
        out_specs=(_row_spec(TM, slot_w), _row_spec(TM, slot_w), t_spec, _row_spec(TM, D_BRANCH), t_spec,
                   _row3_spec(3, TM, D_BRANCH), t_spec, t_spec, _row_spec(TM, 2 * d), _row_spec(TM, LANES)),
        scratch_shapes=[pltpu.VMEM((1, LANES), F32)],
        compiler_params=_seq_params(),
    )(x, g_mix, w_pad, bf_pad, place_q, place_k, ones_q, ones_k)


def _slot_spec(seq):
    return pl.BlockSpec((seq, NH * HEAD_SLOT), lambda b, g: (b, g))


def _slot2_spec(seq):
    return pl.BlockSpec((2, seq, NH * HEAD_SLOT), lambda b, g: (0, b, g))


def _group_spec(seq):
    return pl.BlockSpec((seq, NH * HEAD_DIM), lambda b, g: (b, g))


def _group3_spec(which, seq):
    return pl.BlockSpec((None, seq, NH * HEAD_DIM), lambda b, g: (which, b, g))


def _tblock_spec(seq):
    return pl.BlockSpec((seq // TK, NH * HEAD_SLOT, TK), lambda b, g: (b, g, 0))


def _qrow_spec(seq):
    return pl.BlockSpec((None, NH, seq // TQ, TQ), lambda b, g: (b, g, 0, 0))


def _stat_spec(seq):
    return pl.BlockSpec((None, None, seq, LANES), lambda b, g: (b, g, 0, 0))


def _attn_params():
    return pltpu.CompilerParams(dimension_semantics=("parallel", "parallel"), vmem_limit_bytes=VMEM_LIMIT)


def _serial_attn_params():
    return pltpu.CompilerParams(dimension_semantics=("arbitrary", "arbitrary"), vmem_limit_bytes=VMEM_LIMIT)


def _hcols(hh):
    return slice(hh * HEAD_DIM, (hh + 1) * HEAD_DIM)


def _hslot(hh):
    return slice(hh * HEAD_SLOT, (hh + 1) * HEAD_SLOT)


def _lane(hh):
    return slice(hh, hh + 1)


def _key_query_mask(rel):
    r = lax.broadcasted_iota(jnp.int32, (TK, TQ), 0)
    c = lax.broadcasted_iota(jnp.int32, (TK, TQ), 1)
    return rel(r, c)


def _heads_cat(vals):
    return jnp.concatenate(vals, axis=1)


def _untranspose(acc_t):
    return acc_t.T[:, :HEAD_DIM]


def _fox_fwd(qf, kf, vft, batch, seq):
    def body(q_ref, k_ref, vt_ref, o_ref, lse_ref, m_s, acc_s):
        causal = _key_query_mask(lambda r, c: r <= c)

        def tile(q0, kj, masked):
            krows = pl.ds(pl.multiple_of(kj * TK, TK), TK)
            heads = range(NH)
            sts = [_dot_nt(k_ref[krows, _hslot(hh)], q_ref[pl.ds(q0, TQ), _hslot(hh)]) for hh in heads]
            if masked:
                sts = [jnp.where(causal, st, NEG) for st in sts]
            m_olds = [m_s[hh] for hh in heads]
            m_news = [jnp.maximum(m_olds[hh], jnp.max(sts[hh], axis=0, keepdims=True)) for hh in heads]
            pts = [jnp.exp(sts[hh] - m_news[hh]).astype(BF16) for hh in heads]
            pvs = [_dot(vt_ref[kj, _hslot(hh), :], pts[hh]) for hh in heads]
            for hh in heads:
                acc_s[hh] = jnp.exp(m_olds[hh] - m_news[hh]) * acc_s[hh] + pvs[hh]
                m_s[hh] = m_news[hh]

        def q_loop(qi, _):
            q0 = pl.multiple_of(qi * TQ, TQ)
            m_s[...] = jnp.full(m_s.shape, NEG, F32)
            acc_s[...] = jnp.zeros_like(acc_s)

            def k_loop(kj, _):
                tile(q0, kj, False)
                return 0

            lax.fori_loop(0, qi, k_loop, 0)
            tile(q0, qi, True)
            outs = []
            for hh in range(NH):
                total = acc_s[hh, HEAD_DIM:HEAD_DIM + 1, :]
                outs.append(_untranspose(acc_s[hh] / total))
                lse_ref[hh, pl.ds(qi, 1), :] = m_s[hh] + jnp.log(total)
            o_ref[pl.ds(q0, TQ), :] = _heads_cat(outs).astype(BF16)
            return 0

        lax.fori_loop(0, seq // TQ, q_loop, 0)

    return pl.pallas_call(
        body, name="fox_fwd", grid=(batch, N_HEADS // NH),
        out_shape=(jax.ShapeDtypeStruct((batch * seq, D_BRANCH), BF16),
                   jax.ShapeDtypeStruct((batch, N_HEADS, seq // TQ, TQ), F32)),
        in_specs=[_slot_spec(seq), _slot_spec(seq), _tblock_spec(seq)],
        out_specs=(_group_spec(seq), _qrow_spec(seq)),
        scratch_shapes=[pltpu.VMEM((NH, 1, TQ), F32), pltpu.VMEM((NH, HEAD_SLOT, TQ), F32)],
        compiler_params=_attn_params(),
    )(qf, kf, vft)


def _fox_bwd(qf, kf, kft, vf, o, do, lse, batch, seq):
    n_q = seq // TQ

    def body(q_ref, k_ref, kt_ref, v_ref, o_ref, do_ref, lse_ref, dqk_ref, dv_ref, dcq_ref, dck_ref,
             delta_s, dqt_acc, dk_s, dv_s):
        causal = _key_query_mask(lambda r, c: r <= c)
        ones8 = jnp.ones((8, HEAD_DIM), BF16)
        dqt_acc[...] = jnp.zeros_like(dqt_acc)

        def prep(qi, _):
            rows = pl.ds(pl.multiple_of(qi * TQ, TQ), TQ)
            for hh in range(NH):
                hi, lo = _split2(do_ref[rows, _hcols(hh)].astype(F32) * o_ref[rows, _hcols(hh)].astype(F32))
                delta_s[hh, pl.ds(qi, 1), :] = (_dot_nt(ones8, hi) + _dot_nt(ones8, lo))[0:1, :]
            return 0

        lax.fori_loop(0, n_q, prep, 0)

        def tile(qi, kj, masked):
            rows = pl.ds(pl.multiple_of(qi * TQ, TQ), TQ)
            krows = pl.ds(pl.multiple_of(kj * TK, TK), TK)
            heads = range(NH)
            qs = [q_ref[rows, _hslot(hh)] for hh in heads]
            douts = [do_ref[rows, _hcols(hh)] for hh in heads]
            sts = [_dot_nt(k_ref[krows, _hslot(hh)], qs[hh]) for hh in heads]
            dps = [_dot_nt(v_ref[krows, _hcols(hh)], douts[hh]) for hh in heads]
            pts = [jnp.exp(sts[hh] - lse_ref[hh, pl.ds(qi, 1), :]) for hh in heads]
            if masked:
                pts = [jnp.where(causal, pt, 0.0) for pt in pts]
            dsts = [(pts[hh] * (dps[hh] - delta_s[hh, pl.ds(qi, 1), :])).astype(BF16) for hh in heads]
            for hh in heads:
                dv_s[hh] += _dot(pts[hh].astype(BF16), douts[hh])
                dk_s[hh] += _dot(dsts[hh], qs[hh])
                dqt_acc[hh, qi] += _dot(kt_ref[kj, _hslot(hh), :], dsts[hh])

        def k_loop(kj, _):
            krows = pl.ds(pl.multiple_of(kj * TK, TK), TK)
            dk_s[...] = jnp.zeros_like(dk_s)
            dv_s[...] = jnp.zeros_like(dv_s)
            tile(kj, kj, True)

            def q_loop(qi, _):
                tile(qi, kj, False)
                return 0

            lax.fori_loop(kj + 1, n_q, q_loop, 0)
            dqk_ref[1, krows, :] = _heads_cat([dk_s[hh] for hh in range(NH)]).astype(BF16)
            dv_ref[krows, :] = _heads_cat([dv_s[hh] for hh in range(NH)]).astype(BF16)
            for hh in range(NH):
                dck_ref[krows, _lane(hh)] = dk_s[hh, :, C_ONES_Q:C_ONES_Q + 1]
            return 0

        lax.fori_loop(0, seq // TK, k_loop, 0)

        def finish(qi, _):
            rows = pl.ds(pl.multiple_of(qi * TQ, TQ), TQ)
            dqk_ref[0, rows, :] = _heads_cat([dqt_acc[hh, qi].T for hh in range(NH)]).astype(BF16)
            for hh in range(NH):
                dcq_ref[hh, pl.ds(qi, 1), :] = dqt_acc[hh, qi, C_ONES_K:C_ONES_K + 1, :]
            return 0

        lax.fori_loop(0, n_q, finish, 0)

    return pl.pallas_call(
        body, name="fox_bwd", grid=(batch, N_HEADS // NH),
        out_shape=(jax.ShapeDtypeStruct((2, batch * seq, N_HEADS * HEAD_SLOT), BF16),
                   jax.ShapeDtypeStruct((batch * seq, D_BRANCH), BF16),
                   jax.ShapeDtypeStruct((batch, N_HEADS, seq // TQ, TQ), F32),
                   jax.ShapeDtypeStruct((batch, N_HEADS // NH, seq, LANES), F32)),
        in_specs=[_slot_spec(seq), _slot_spec(seq), _tblock_spec(seq), _group_spec(seq), _group_spec(seq),
                  _group_spec(seq), _qrow_spec(seq)],
        out_specs=(_slot2_spec(seq), _group_spec(seq), _qrow_spec(seq), _stat_spec(seq)),
        scratch_shapes=[pltpu.VMEM((NH, n_q, TQ), F32), pltpu.VMEM((NH, n_q, HEAD_SLOT, TQ), F32),
                        pltpu.VMEM((NH, TK, HEAD_SLOT), F32), pltpu.VMEM((NH, TK, HEAD_DIM), F32)],
        compiler_params=_attn_params(),
    )(qf, kf, kft, vf, o, do, lse)


def _first_last_step():
    step = pl.program_id(0) * pl.num_programs(1) + pl.program_id(1)
    return step == 0, step == pl.num_programs(0) * pl.num_programs(1) - 1


def _sb_fwd(qkvb, vbt, batch, seq, shards):
    n = len(shards)

    def body(q_ref, k_ref, vt_ref, *rest):
        x_refs, (o_ref, lt_ref), out_refs = rest[:n], rest[n:n + 2], rest[n + 2:2 * n + 2]
        run_s, acc_s = rest[2 * n + 2:2 * n + 4]
        gather_start, gather_finish = _gather_plan(x_refs, out_refs, *rest[2 * n + 4:])
        first_step, last_step = _first_last_step()
        pl.when(first_step)(gather_start)
        strict = _key_query_mask(lambda r, c: r < c)
        later = _tri(TK, lambda r, c: c > r)

        def tile(q0, kj, masked):
            krows = pl.ds(pl.multiple_of(kj * TK, TK), TK)
            heads = range(NH)
            zts = [_dot_nt(k_ref[krows, _hcols(hh)], q_ref[pl.ds(q0, TQ), _hcols(hh)]) for hh in heads]
            lgs = [-_softplus(zt) for zt in zts]
            if masked:
                lgs = [jnp.where(strict, lg, 0.0) for lg in lgs]
            parts = [_split2(lg) for lg in lgs]
            sufs = [_dot(later, hi) + _dot(later, lo) for hi, lo in parts]
            ats = [jnp.exp(zts[hh] + lgs[hh] + run_s[hh] + sufs[hh]) for hh in heads]
            if masked:
                ats = [jnp.where(strict, at, 0.0) for at in ats]
            for hh in heads:
                acc_s[hh] += _dot(vt_ref[kj, _hslot(hh), :], ats[hh].astype(BF16))
                run_s[hh] += jnp.sum(lgs[hh], axis=0, keepdims=True)

        def q_loop(qi, _):
            q0 = pl.multiple_of(qi * TQ, TQ)
            run_s[...] = jnp.zeros_like(run_s)
            acc_s[...] = jnp.zeros_like(acc_s)
            tile(q0, qi, True)

            def k_loop(kk, _):
                tile(q0, qi - 1 - kk, False)
                return 0

            lax.fori_loop(0, qi, k_loop, 0)
            o_ref[pl.ds(q0, TQ), :] = _heads_cat([_untranspose(acc_s[hh]) for hh in range(NH)]).astype(BF16)
            for hh in range(NH):
                lt_ref[hh, pl.ds(qi, 1), :] = run_s[hh]
            return 0

        lax.fori_loop(0, seq // TQ, q_loop, 0)
        pl.when(last_step)(gather_finish)

    out = pl.pallas_call(
        body, name="sb_fwd", grid=(batch, N_HEADS // NH),
        out_shape=[jax.ShapeDtypeStruct((batch * seq, D_BRANCH), BF16),
                   jax.ShapeDtypeStruct((batch, N_HEADS, seq // TQ, TQ), F32)] + _gather_shapes(shards),
        in_specs=[_group3_spec(0, seq), _group3_spec(1, seq), _tblock_spec(seq)] + _hbm_specs(n),
        out_specs=[_group_spec(seq), _qrow_spec(seq)] + _hbm_specs(n),
        scratch_shapes=[pltpu.VMEM((NH, 1, TQ), F32), pltpu.VMEM((NH, HEAD_SLOT, TQ), F32)] + _gather_sems(n),
        compiler_params=_serial_attn_params(),
    )(qkvb, qkvb, vbt, *shards)
    return out[0], out[1], out[2:]


def _sb_bwd(qkvb, kbt, do, ltot, batch, seq, chip_sums):
    n = len(chip_sums)

    def body(q_ref, k_ref, v_ref, kt_ref, do_ref, lt_ref, *rest):
        cs_refs, dqkv_ref, out_refs = rest[:n], rest[n], rest[n + 1:2 * n + 1]
        dk_acc, dv_acc, ls_s, gs_s, dqt_s = rest[2 * n + 1:2 * n + 6]
        chips_start, chips_finish = _chips_plan(cs_refs, out_refs, *rest[2 * n + 6:])
        first_step, last_step = _first_last_step()
        pl.when(first_step)(chips_start)
        strict = _key_query_mask(lambda r, c: r < c)
        upto = _tri(TK, lambda r, c: c <= r)
        before = _tri(TK, lambda r, c: c < r)
        dk_acc[...] = jnp.zeros_like(dk_acc)
        dv_acc[...] = jnp.zeros_like(dv_acc)

        def tile(qi, kj, masked):
            rows = pl.ds(pl.multiple_of(qi * TQ, TQ), TQ)
            krows = pl.ds(pl.multiple_of(kj * TK, TK), TK)
            heads = range(NH)
            qs = [q_ref[rows, _hcols(hh)] for hh in heads]
            douts = [do_ref[rows, _hcols(hh)] for hh in heads]
            zts = [_dot_nt(k_ref[krows, _hcols(hh)], qs[hh]) for hh in heads]
            das = [_dot_nt(v_ref[krows, _hcols(hh)], douts[hh]) for hh in heads]
            lgs = [-_softplus(zt) for zt in zts]
            if masked:
                lgs = [jnp.where(strict, lg, 0.0) for lg in lgs]
            parts = [_split2(lg) for lg in lgs]
            prefs = [_dot(upto, hi) + _dot(upto, lo) for hi, lo in parts]
            ats = [jnp.exp(zts[hh] + lgs[hh] + (lt_ref[hh, pl.ds(qi, 1), :] - ls_s[hh]) - prefs[hh]) for hh in heads]
            if masked:
                ats = [jnp.where(strict, at, 0.0) for at in ats]
            gts = [das[hh] * ats[hh] for hh in heads]
            us = [gs_s[hh] + _dot(before, gts[hh].astype(BF16)) for hh in heads]
            dzts = [(jnp.exp(lgs[hh]) * (gts[hh] + us[hh]) - us[hh]).astype(BF16) for hh in heads]
            for hh in heads:
                dk_acc[hh, krows, :] += _dot(dzts[hh], qs[hh])
                dv_acc[hh, krows, :] += _dot(ats[hh].astype(BF16), douts[hh])
                dqt_s[hh] += _dot(kt_ref[kj, _hslot(hh), :], dzts[hh])
                ls_s[hh] += jnp.sum(lgs[hh], axis=0, keepdims=True)
                gs_s[hh] += jnp.sum(gts[hh], axis=0, keepdims=True)

        def q_loop(qi, _):
            ls_s[...] = jnp.zeros_like(ls_s)
            gs_s[...] = jnp.zeros_like(gs_s)
            dqt_s[...] = jnp.zeros_like(dqt_s)

            def k_loop(kj, _):
                tile(qi, kj, False)
                return 0

            lax.fori_loop(0, qi, k_loop, 0)
            tile(qi, qi, True)
            dqkv_ref[0, pl.ds(pl.multiple_of(qi * TQ, TQ), TQ), :] = _heads_cat(
                [_untranspose(dqt_s[hh]) for hh in range(NH)]).astype(BF16)
            return 0

        lax.fori_loop(0, seq // TQ, q_loop, 0)
        dqkv_ref[1] = _heads_cat([dk_acc[hh] for hh in range(NH)]).astype(BF16)
        dqkv_ref[2] = _heads_cat([dv_acc[hh] for hh in range(NH)]).astype(BF16)
        pl.when(last_step)(chips_finish)

    out = pl.pallas_call(
        body, name="sb_bwd", grid=(batch, N_HEADS // NH),
        out_shape=[jax.ShapeDtypeStruct((3, batch * seq, D_BRANCH), BF16)]
        + [jax.ShapeDtypeStruct(s.shape, s.dtype) for s in chip_sums],
        in_specs=[_group3_spec(0, seq), _group3_spec(1, seq), _group3_spec(2, seq), _tblock_spec(seq),
                  _group_spec(seq), _qrow_spec(seq)] + _hbm_specs(n),
        out_specs=[pl.BlockSpec((3, seq, NH * HEAD_DIM), lambda b, g: (0, b, g))] + _hbm_specs(n),
        scratch_shapes=[pltpu.VMEM((NH, seq, HEAD_DIM), F32), pltpu.VMEM((NH, seq, HEAD_DIM), F32),
                        pltpu.VMEM((NH, 1, TQ), F32), pltpu.VMEM((NH, 1, TQ), F32),
                        pltpu.VMEM((NH, HEAD_SLOT, TQ), F32)] + _chips_sems(n),
        compiler_params=_serial_attn_params(),
    )(qkvb, qkvb, qkvb, kbt, do, ltot, *chip_sums)
    return out[0], out[1:]


def _forget_bwd(dcq_tok, dck_tok, fpre, batch, seq):
    t_len = batch * seq
    tiles = seq // TM

    def rev(i):
        return ((i // tiles) * tiles + (tiles - 1 - i % tiles), 0)

    def body(dcq_ref, dck_ref, f_ref, df_ref, db_ref, carry_ref):
        i = pl.program_id(0)

        @pl.when(i == 0)
        def _():
            db_ref[...] = jnp.zeros_like(db_ref)

        @pl.when(i % tiles == 0)
        def _():
            carry_ref[...] = jnp.zeros_like(carry_ref)

        dc = dcq_ref[...] - dck_ref[...]
        upper = _tri(TM, lambda r, c: c >= r)
        hi, mid, lo = _split3(dc)
        dlogf = carry_ref[...] + _dot(upper, hi) + _dot(upper, mid) + _dot(upper, lo)
        carry_ref[...] = carry_ref[...] + jnp.sum(dc, axis=0, keepdims=True)
        df = dlogf * _sigmoid(-f_ref[...])
        df_ref[...] = df.astype(BF16)
        db_ref[...] += jnp.sum(df, axis=0, keepdims=True)

    return pl.pallas_call(
        body, name="forget_bwd", grid=(t_len // TM,),
        out_shape=(jax.ShapeDtypeStruct((t_len, LANES), BF16), jax.ShapeDtypeStruct((1, LANES), F32)),
        in_specs=[pl.BlockSpec((TM, LANES), rev)] * 3,
        out_specs=(pl.BlockSpec((TM, LANES), rev), _acc_spec((1, LANES))),
        scratch_shapes=[pltpu.VMEM((1, LANES), F32)],
        compiler_params=_seq_params(),
    )(dcq_tok, dck_tok, fpre)


def _mix_fwd(o_fox, o_sb, gl, x, w_bf, w_bs, w_out, b_gate):
    t_len, d = x.shape

    def body(of_ref, os_ref, gl_ref, x_ref, wbf_ref, wbs_ref, wo_ref, bg_ref, x1_ref):
        br_f = _dot(of_ref[...], wbf_ref[...])
        br_s = _dot(os_ref[...], wbs_ref[...])
        ga = _sigmoid(gl_ref[:, :d].astype(F32) + bg_ref[0:1, :])
        gb = _sigmoid(gl_ref[:, d:].astype(F32) + bg_ref[1:2, :])
        merged = ga * br_f + gb * br_s
        x1_ref[...] = x_ref[...] + _dot(merged.astype(BF16), wo_ref[...])

    return pl.pallas_call(
        body, name="mix_fwd", grid=(t_len // TM,),
        out_shape=jax.ShapeDtypeStruct((t_len, d), F32),
        in_specs=[_row_spec(TM, D_BRANCH), _row_spec(TM, D_BRANCH), _row_spec(TM, 2 * d), _row_spec(TM, d),
                  _const_spec(w_bf.shape), _const_spec(w_bs.shape), _const_spec(w_out.shape), _const_spec(b_gate.shape)],
        out_specs=_row_spec(TM, d),
        compiler_params=_seq_params(),
    )(o_fox, o_sb, gl, x, w_bf, w_bs, w_out, b_gate)


def _ff_chunk(d_ff):
    return min(d_ff, 1024)


def _mlp_fwd(x1, g_mlp, w_up, w_down):
    t_len, d = x1.shape
    d_ff = w_up.shape[1]
    ch = _ff_chunk(d_ff)

    def body(x1_ref, g_ref, wu_ref, wd_ref, a_ref, x2_ref):
        x1v = x1_ref[...]
        xn, _ = _rms(x1v)
        h = (xn * g_ref[...]).astype(BF16)
        acc = x1v
        for j in range(d_ff // ch):
            a = _dot(h, wu_ref[:, j * ch:(j + 1) * ch])
            a_ref[:, j * ch:(j + 1) * ch] = a.astype(BF16)
            acc = acc + _dot(jnp.square(jnp.maximum(a, 0.0)).astype(BF16), wd_ref[j * ch:(j + 1) * ch, :])
        x2_ref[...] = acc

    return pl.pallas_call(
        body, name="mlp_fwd", grid=(t_len // TM,),
        out_shape=(jax.ShapeDtypeStruct((t_len, d_ff), BF16), jax.ShapeDtypeStruct((t_len, d), F32)),
        in_specs=[_row_spec(TM, d), _const_spec((1, d)), _const_spec(w_up.shape), _const_spec(w_down.shape)],
        out_specs=(_row_spec(TM, d_ff), _row_spec(TM, d)),
        compiler_params=_seq_params(),
    )(x1, g_mlp, w_up, w_down)


def _head_fwd_bwd(x2, p, target, g_ple, g_final, w_pg, w_ple):
    t_len, d = x2.shape
    d_ple = p.shape[1]

    def body(x2_ref, p_ref, t_ref, gp_ref, gf_ref, wpg_ref, wple_ref,
             dx2_ref, h3_ref, dpre_ref, dpe_ref, loss_ref, dgp_ref, dgf_ref):
        @pl.when(pl.program_id(0) == 0)
        def _():
            loss_ref[...] = jnp.zeros_like(loss_ref)
            dgp_ref[...] = jnp.zeros_like(dgp_ref)
            dgf_ref[...] = jnp.zeros_like(dgf_ref)

        x2v = x2_ref[...]
        x2n, r3 = _rms(x2v)
        h3 = (x2n * gp_ref[...]).astype(BF16)
        h3_ref[...] = h3
        gate = _sigmoid(_dot(h3, wpg_ref[...]))
        pe = _dot(p_ref[...].astype(BF16), wple_ref[...])
        x3n, r4 = _rms(x2v + gate * pe)
        err = x3n * gf_ref[...] - t_ref[...]
        loss_ref[...] += jnp.full(loss_ref.shape, (0.5 / d) * jnp.sum(err * err), F32)
        dx3, dgf = _rms_bwd(err * (1.0 / d), x3n, r4, gf_ref[...])
        dgf_ref[...] += dgf
        dpe_ref[...] = (dx3 * gate).astype(BF16)
        dpre = (dx3 * pe * gate * (1.0 - gate)).astype(BF16)
        dpre_ref[...] = dpre
        dres, dgp = _rms_bwd(_dot_nt(dpre, wpg_ref[...]), x2n, r3, gp_ref[...])
        dgp_ref[...] += dgp
        dx2_ref[...] = dx3 + dres

    shp_b = jax.ShapeDtypeStruct((t_len, d), BF16)
    return pl.pallas_call(
        body, name="head_fwd_bwd", grid=(t_len // TM,),
        out_shape=(jax.ShapeDtypeStruct((t_len, d), F32), shp_b, shp_b, shp_b,
                   jax.ShapeDtypeStruct((1, LANES), F32), jax.ShapeDtypeStruct((1, d), F32),
                   jax.ShapeDtypeStruct((1, d), F32)),
        in_specs=[_row_spec(TM, d), _row_spec(TM, d_ple), _row_spec(TM, d), _const_spec((1, d)), _const_spec((1, d)),
                  _const_spec(w_pg.shape), _const_spec(w_ple.shape)],
        out_specs=(_row_spec(TM, d), _row_spec(TM, d), _row_spec(TM, d), _row_spec(TM, d),
                   _acc_spec((1, LANES)), _acc_spec((1, d)), _acc_spec((1, d))),
        compiler_params=_seq_params(),
    )(x2, p, target, g_ple, g_final, w_pg, w_ple)


def _mlp_bwd(dx2, a, x1, g_mlp, w_up, w_down):
    t_len, d = x1.shape
    d_ff = w_up.shape[1]
    ch = _ff_chunk(d_ff)

    def body(dx2_ref, a_ref, x1_ref, g_ref, wu_ref, wd_ref, dx1_ref, da_ref, h2_ref, dg_ref):
        @pl.when(pl.program_id(0) == 0)
        def _():
            dg_ref[...] = jnp.zeros_like(dg_ref)

        dx2v = dx2_ref[...]
        dx2b = dx2v.astype(BF16)
        xn, r = _rms(x1_ref[...])
        h2_ref[...] = (xn * g_ref[...]).astype(BF16)
        dh = jnp.zeros((TM, d), F32)
        for j in range(d_ff // ch):
            dact = _dot_nt(dx2b, wd_ref[j * ch:(j + 1) * ch, :])
            da = (dact * 2.0 * jnp.maximum(a_ref[:, j * ch:(j + 1) * ch].astype(F32), 0.0)).astype(BF16)
            da_ref[:, j * ch:(j + 1) * ch] = da
            dh = dh + _dot_nt(da, wu_ref[:, j * ch:(j + 1) * ch])
        dres, dg = _rms_bwd(dh, xn, r, g_ref[...])
        dg_ref[...] += dg
        dx1_ref[...] = dx2v + dres

    return pl.pallas_call(
        body, name="mlp_bwd", grid=(t_len // TM,),
        out_shape=(jax.ShapeDtypeStruct((t_len, d), F32), jax.ShapeDtypeStruct((t_len, d_ff), BF16),
                   jax.ShapeDtypeStruct((t_len, d), BF16), jax.ShapeDtypeStruct((1, d), F32)),
        in_specs=[_row_spec(TM, d), _row_spec(TM, d_ff), _row_spec(TM, d), _const_spec((1, d)),
                  _const_spec(w_up.shape), _const_spec(w_down.shape)],
        out_specs=(_row_spec(TM, d), _row_spec(TM, d_ff), _row_spec(TM, d), _acc_spec((1, d))),
        compiler_params=_seq_params(),
    )(dx2, a, x1, g_mlp, w_up, w_down)


def _mix_bwd(dx1, o_fox, o_sb, gl, w_bf, w_bs, w_out, b_gate):
    t_len, d = dx1.shape

    def body(dx1_ref, of_ref, os_ref, gl_ref, wbf_ref, wbs_ref, wo_ref, bg_ref,
             mg_ref, dbf_ref, dbs_ref, dgl_ref, dof_ref, dos_ref, dbg_ref):
        @pl.when(pl.program_id(0) == 0)
        def _():
            dbg_ref[...] = jnp.zeros_like(dbg_ref)

        dmerged = _dot_nt(dx1_ref[...].astype(BF16), wo_ref[...])
        br_f = _dot(of_ref[...], wbf_ref[...])
        br_s = _dot(os_ref[...], wbs_ref[...])
        ga = _sigmoid(gl_ref[:, :d].astype(F32) + bg_ref[0:1, :])
        gb = _sigmoid(gl_ref[:, d:].astype(F32) + bg_ref[1:2, :])
        mg_ref[...] = (ga * br_f + gb * br_s).astype(BF16)
        dbf = (dmerged * ga).astype(BF16)
        dbs = (dmerged * gb).astype(BF16)
        dbf_ref[...] = dbf
        dbs_ref[...] = dbs
        dla = dmerged * br_f * ga * (1.0 - ga)
        dlb = dmerged * br_s * gb * (1.0 - gb)
        dgl_ref[:, :d] = dla.astype(BF16)
        dgl_ref[:, d:] = dlb.astype(BF16)
        dbg_ref[0:1, :] += jnp.sum(dla, axis=0, keepdims=True)
        dbg_ref[1:2, :] += jnp.sum(dlb, axis=0, keepdims=True)
        dof_ref[...] = _dot_nt(dbf, wbf_ref[...]).astype(BF16)
        dos_ref[...] = _dot_nt(dbs, wbs_ref[...]).astype(BF16)

    shp_d = jax.ShapeDtypeStruct((t_len, d), BF16)
    shp_h = jax.ShapeDtypeStruct((t_len, D_BRANCH), BF16)
    return pl.pallas_call(
        body, name="mix_bwd", grid=(t_len // TM,),
        out_shape=(shp_d, shp_d, shp_d, jax.ShapeDtypeStruct((t_len, 2 * d), BF16), shp_h, shp_h,
                   jax.ShapeDtypeStruct((2, d), F32)),
        in_specs=[_row_spec(TM, d), _row_spec(TM, D_BRANCH), _row_spec(TM, D_BRANCH), _row_spec(TM, 2 * d),
                  _const_spec(w_bf.shape), _const_spec(w_bs.shape), _const_spec(w_out.shape), _const_spec(b_gate.shape)],
        out_specs=(_row_spec(TM, d), _row_spec(TM, d), _row_spec(TM, d), _row_spec(TM, 2 * d),
                   _row_spec(TM, D_BRANCH), _row_spec(TM, D_BRANCH), _acc_spec((2, d))),
        compiler_params=_seq_params(),
    )(dx1, o_fox, o_sb, gl, w_bf, w_bs, w_out, b_gate)


def _inproj_bwd(dqk_f, dv_f, dqkv_b, dgl, df, dx1, x, g_mix, w_pad):
    t_len, d = x.shape
    lay, _ = _pad_layout(d)
    slot_w = N_HEADS * HEAD_SLOT

    def body(dqk_ref, dvf_ref, db_ref, dgl_ref, df_ref, dx1_ref, x_ref, g_ref, w_ref, dx_ref, h1_ref, dg_ref):
        @pl.when(pl.program_id(0) == 0)
        def _():
            dg_ref[...] = jnp.zeros_like(dg_ref)

        def back(piece, name):
            lo, hi = lay[name]
            return _dot_nt(piece, w_ref[:, lo:hi])

        xn, r = _rms(x_ref[...])
        h1_ref[...] = (xn * g_ref[...]).astype(BF16)
        dh = (back(df_ref[...], "forget") + back(dgl_ref[...], "gates") + back(dqk_ref[0], "qf")
              + back(dqk_ref[1], "kf") + back(dvf_ref[...], "vf") + back(db_ref[0], "qb") + back(db_ref[1], "kb")
              + back(db_ref[2], "vb"))
        dres, dg = _rms_bwd(dh, xn, r, g_ref[...])
        dg_ref[...] += dg
        dx_ref[...] = dx1_ref[...] + dres

    return pl.pallas_call(
        body, name="inproj_bwd", grid=(t_len // TM,),
        out_shape=(jax.ShapeDtypeStruct((t_len, d), F32), jax.ShapeDtypeStruct((t_len, d), BF16),
                   jax.ShapeDtypeStruct((1, d), F32)),
        in_specs=[_row3_spec(2, TM, slot_w), _row_spec(TM, D_BRANCH), _row3_spec(3, TM, D_BRANCH),
                  _row_spec(TM, 2 * d), _row_spec(TM, LANES), _row_spec(TM, d), _row_spec(TM, d), _const_spec((1, d)),
                  _const_spec(w_pad.shape)],
        out_specs=(_row_spec(TM, d), _row_spec(TM, d), _acc_spec((1, d))),
        compiler_params=_seq_params(),
    )(dqk_f, dv_f, dqkv_b, dgl, df, dx1, x, g_mix, w_pad)


def _cols_to_slabs(full):
    r, c8 = full.shape
    return full.reshape(r, N_DEV, c8 // N_DEV).transpose(1, 0, 2)


def _slabs_to_cols(slabs):
    n, r, c = slabs.shape
    return slabs.transpose(1, 0, 2).reshape(r, n * c)


def _win_sizes(d):
    return (D_BRANCH, D_BRANCH, D_BRANCH, N_HEADS, D_BRANCH, D_BRANCH, D_BRANCH, d, d)


def _split_win(w, d):
    out, off = [], 0
    for s in _win_sizes(d):
        out.append(w[:, off:off + s])
        off += s
    return out


def _to_slots(w):
    r = w.shape[0]
    return jnp.pad(w.reshape(r, N_HEADS, HEAD_DIM), ((0, 0), (0, 0), (0, HEAD_SLOT - HEAD_DIM))).reshape(r, -1)


def _from_slots(w):
    r = w.shape[0]
    return w.reshape(r, N_HEADS, HEAD_SLOT)[:, :, :HEAD_DIM].reshape(r, N_HEADS * HEAD_DIM)


def _pad_win(w_full, d):
    qa, ka, va, fa, qb, kb, vb, ga, gb = _split_win(w_full, d)
    scale = HEAD_DIM ** -0.5
    fpad = jnp.pad(fa, ((0, 0), (0, LANES - N_HEADS)))
    return jnp.concatenate([_to_slots(qa * scale), _to_slots(ka), va, qb * scale, kb, vb, ga, gb, fpad], axis=1)


def _unpad_dwin(dqk_f, dv_f, dqkv_b, dgates, dforget, d):
    scale = HEAD_DIM ** -0.5
    return jnp.concatenate([_from_slots(dqk_f[0]) * scale, _from_slots(dqk_f[1]), dv_f, dforget[:, :N_HEADS],
                            dqkv_b[0] * scale, dqkv_b[1], dqkv_b[2], dgates], axis=1)


def _c_lane_constants():
    head = jnp.arange(LANES)[:, None]
    lane = jnp.arange(N_HEADS * HEAD_SLOT)[None, :]
    in_head = (lane // HEAD_SLOT == head) & (head < N_HEADS)

    def place(first):
        return jnp.stack([(in_head & (lane % HEAD_SLOT == first + j)) for j in range(3)]).astype(BF16)

    def ones(first):
        off = lane % HEAD_SLOT
        return ((off >= first) & (off < first + 3)).astype(F32)

    return place(C_TERMS_Q), place(C_TERMS_K), ones(C_ONES_Q), ones(C_ONES_K)


def _pad_rows(a, rows):
    return jnp.pad(a, [(0, 0)] * (a.ndim - 2) + [(0, rows - a.shape[-2]), (0, 0)])


def kernel(x, p, g_mix, w_in, b_forget, b_gate, w_branch_fox, w_branch_sb, w_out, g_mlp, w_up, w_down, g_ple, w_ple_gate, w_ple, g_final, loss_target, m_g_mix, m_w_in, m_b_forget, m_b_gate, m_w_branch_fox, m_w_branch_sb, m_w_out, m_g_mlp, m_w_up, m_w_down, m_g_ple, m_w_ple_gate, m_w_ple, m_g_final, v_g_mix, v_w_in, v_b_forget, v_b_gate, v_w_branch_fox, v_w_branch_sb, v_w_out, v_g_mlp, v_w_up, v_w_down, v_g_ple, v_w_ple_gate, v_w_ple, v_g_final):
    batch, seq, d = x.shape
    t_len = batch * seq
    d_ple = p.shape[-1]
    d_ff = w_up.shape[-1] * N_DEV
    dn = d // N_DEV
    fn = d_ff // N_DEV
    my_c = lax.axis_index("c")
    my_dev = 4 * lax.axis_index("x") + 2 * lax.axis_index("y") + my_c

    bg_hi = b_gate[0].astype(BF16)
    bg_r = b_gate[0] - bg_hi.astype(F32)
    bg_mid = bg_r.astype(BF16)
    bg_lo = (bg_r - bg_mid.astype(F32)).astype(BF16)
    narrow_rows = 2 * D_BRANCH + d_ple + 6
    narrow_rows_pad = -(-narrow_rows // 16) * 16
    wide = jnp.concatenate([w_out[0], w_down[0], w_ple_gate[0]], axis=0).astype(BF16)
    narrow = _pad_rows(jnp.concatenate(
        [w_branch_fox[0].astype(BF16), w_branch_sb[0].astype(BF16), w_ple[0].astype(BF16), bg_hi, bg_mid, bg_lo],
        axis=0), narrow_rows_pad)
    g_in, = _all_gather([w_in[0].astype(BF16)])
    w_pad = _pad_win(_slabs_to_cols(g_in), d)
    bf_pad = jnp.pad(b_forget, ((0, 0), (0, LANES - N_HEADS)))
    place_q, place_k, ones_q, ones_k = _c_lane_constants()

    x2d = x.reshape(t_len, d)
    p2d = p.reshape(t_len, d_ple)
    tgt2d = loss_target.reshape(t_len, d)
    qf, kf, kft, vf, vft, qkvb, kbt, vbt, gl, fpre = _inproj_fwd(
        x2d, g_mix, w_pad, bf_pad, place_q, place_k, ones_q, ones_k, seq)
    o_sb, ltot, (g_up, g_wide, g_narrow) = _sb_fwd(qkvb, vbt, batch, seq, [w_up[0].astype(BF16), wide, narrow])
    o_fox, lse = _fox_fwd(qf, kf, vft, batch, seq)
    w_up_full = _slabs_to_cols(g_up)
    w_out_full = g_wide[:, :dn].reshape(d, d)
    w_down_full = g_wide[:, dn:dn + fn].reshape(d_ff, d)
    w_pg_full = g_wide[:, dn + fn:].reshape(d, d)
    w_bf_full = _slabs_to_cols(g_narrow[:, :D_BRANCH])
    w_bs_full = _slabs_to_cols(g_narrow[:, D_BRANCH:2 * D_BRANCH])
    w_ple_full = _slabs_to_cols(g_narrow[:, 2 * D_BRANCH:2 * D_BRANCH + d_ple])
    bg_terms = g_narrow[:, 2 * D_BRANCH + d_ple:narrow_rows].astype(F32)
    b_gate_full = _slabs_to_cols(bg_terms[:, 0:2] + bg_terms[:, 2:4] + bg_terms[:, 4:6])
    x1 = _mix_fwd(o_fox, o_sb, gl, x2d, w_bf_full, w_bs_full, w_out_full, b_gate_full)
    a_up, x2 = _mlp_fwd(x1, g_mlp, w_up_full, w_down_full)

    dx2, h3, dpre, dpe, loss_acc, dg_ple, dg_final = _head_fwd_bwd(
        x2, p2d, tgt2d, g_ple, g_final.reshape(1, d), w_pg_full, w_ple_full)
    dx1, da_up, h2, dg_mlp = _mlp_bwd(dx2, a_up, x1, g_mlp, w_up_full, w_down_full)
    merged, dbr_f, dbr_s, dgl, do_fox, do_sb, dbg = _mix_bwd(
        dx1, o_fox, o_sb, gl, w_bf_full, w_bs_full, w_out_full, b_gate_full)

    def column_shards(name, lhs, rhs):
        if (rhs.shape[-1] // N_DEV) % LANES == 0:
            return _matmul_tn(name, lhs, rhs, slabs=True)
        return _cols_to_slabs(_matmul_tn(name, lhs, rhs))

    gw_pg = _matmul_tn("dw_ple_gate", h3, dpre)
    gw_down = _matmul_tn("dw_down", a_up, dx2, relu2=True)
    gw_out = _matmul_tn("dw_out", merged, dx1)
    part_up = column_shards("dw_up", h2, da_up)
    part_wide = jnp.concatenate([gw_out.reshape(N_DEV, dn, d), gw_down.reshape(N_DEV, fn, d),
                                 gw_pg.reshape(N_DEV, dn, d)], axis=1)
    part_narrow = _pad_rows(jnp.concatenate(
        [column_shards("dw_branch_fox", o_fox, dbr_f), column_shards("dw_branch_sb", o_sb, dbr_s),
         column_shards("dw_ple", p2d, dpe)], axis=1), narrow_rows_pad)
    early = [part_up, part_wide, part_narrow]
    early_sums = [_pair_add("pair_add_%d" % i, pt, rc, my_c)
                  for i, (pt, rc) in enumerate(zip(early, _rs_core_pair("reduce_scatter_core_pair_early", early)))]

    dqk_f, dv_f, dc_queries, dc_keys = _fox_bwd(qf, kf, kft, vf, o_fox, do_fox, lse, batch, seq)
    dqkv_b, (s_up, s_wide, s_narrow) = _sb_bwd(qkvb, kbt, do_sb, ltot, batch, seq, early_sums)
    dcq_tok = dc_queries.reshape(batch, N_HEADS, seq).transpose(0, 2, 1).reshape(t_len, N_HEADS)
    dck_tok = dc_keys[..., :NH].transpose(0, 2, 1, 3).reshape(t_len, N_HEADS)
    lane_pad = ((0, 0), (0, LANES - N_HEADS))
    df, db_forget = _forget_bwd(jnp.pad(dcq_tok, lane_pad), jnp.pad(dck_tok, lane_pad), fpre, batch, seq)
    grad_x, h1, dg_mix = _inproj_bwd(dqk_f, dv_f, dqkv_b, dgl, df, dx1, x2d, g_mix, w_pad)

    gw_in = _unpad_dwin(_matmul_tn("dw_in_fox_qk", h1, dqk_f), _matmul_tn("dw_in_fox_v", h1, dv_f),
                        _matmul_tn("dw_in_sb", h1, dqkv_b), _matmul_tn("dw_in_gates", h1, dgl),
                        _matmul_tn("dw_in_forget", h1, df), d)
    part_in = _cols_to_slabs(gw_in)
    recv_in, = _rs_core_pair("reduce_scatter_core_pair_w_in", [part_in])
    s_in, = _rs_chips([_pair_add("pair_add_w_in", part_in, recv_in, my_c)])

    small = jnp.concatenate([
        dg_mix, dg_mlp, dg_ple, dg_final, jnp.pad(db_forget[:, :N_HEADS], ((0, 0), (0, d - N_HEADS))), dbg,
        jnp.pad(loss_acc[:, :1], ((0, 0), (0, d - 1)))], axis=0)
    small = _all_reduce_small(small)
    loss = small[7, 0]
    small_grads = {
        "g_mix": small[0:1], "g_mlp": small[1:2], "g_ple": small[2:3], "g_final": small[3:4],
        "b_forget": small[4:5, :N_HEADS],
        "b_gate": lax.dynamic_slice_in_dim(small[5:7], my_dev * dn, dn, axis=1),
    }

    weights = {"g_mix": g_mix, "w_in": w_in, "b_forget": b_forget, "b_gate": b_gate, "w_branch_fox": w_branch_fox,
               "w_branch_sb": w_branch_sb, "w_out": w_out, "g_mlp": g_mlp, "w_up": w_up, "w_down": w_down,
               "g_ple": g_ple, "w_ple_gate": w_ple_gate, "w_ple": w_ple, "g_final": g_final}
    m_in = {"g_mix": m_g_mix, "w_in": m_w_in, "b_forget": m_b_forget, "b_gate": m_b_gate,
            "w_branch_fox": m_w_branch_fox, "w_branch_sb": m_w_branch_sb, "w_out": m_w_out, "g_mlp": m_g_mlp,
            "w_up": m_w_up, "w_down": m_w_down, "g_ple": m_g_ple, "w_ple_gate": m_w_ple_gate, "w_ple": m_w_ple,
            "g_final": m_g_final}
    v_in = {"g_mix": v_g_mix, "w_in": v_w_in, "b_forget": v_b_forget, "b_gate": v_b_gate,
            "w_branch_fox": v_w_branch_fox, "w_branch_sb": v_w_branch_sb, "w_out": v_w_out, "g_mlp": v_g_mlp,
            "w_up": v_w_up, "w_down": v_w_down, "g_ple": v_g_ple, "w_ple_gate": v_w_ple_gate, "w_ple": v_w_ple,
            "g_final": v_g_final}
    names = list(weights)

    def as2d(a):
        return a.reshape(-1, a.shape[-1])

    result = {}
    big = {"w_in": (s_in, 0), "w_up": (s_up, 0), "w_out": (s_wide, 0), "w_down": (s_wide, dn),
           "w_ple_gate": (s_wide, dn + fn), "w_branch_fox": (s_narrow, 0), "w_branch_sb": (s_narrow, D_BRANCH),
           "w_ple": (s_narrow, 2 * D_BRANCH)}
    for n, (parts, off) in big.items():
        result[n] = _adamw_parts("adamw_" + n, as2d(weights[n]), parts, off, as2d(m_in[n]), as2d(v_in[n]))
    small_names = list(small_grads)
    small_out = _adamw_small([(as2d(weights[n]), small_grads[n], as2d(m_in[n]), as2d(v_in[n])) for n in small_names])
    for n, (dlt, nm, nv) in zip(small_names, small_out):
        result[n] = (small_grads[n], dlt, nm, nv)
    outs = [[result[n][k].reshape(weights[n].shape) for n in names] for k in range(4)]
    return (loss, grad_x.reshape(x.shape), *outs[0], *outs[1], *outs[2], *outs[3])
```

```python
import jax
import jax.numpy as jnp
from jax import lax
from jax.experimental import pallas as pl
from jax.experimental.pallas import tpu as pltpu

F32 = jnp.float32
BF16 = jnp.bfloat16

HEAD_DIM = 64
N_HEADS = 8
D_BRANCH = N_HEADS * HEAD_DIM
EPS = 1e-6
ADAM_LR = 0.001
ADAM_B1 = 0.9
ADAM_B2 = 0.999
ADAM_EPS = 1e-08
ADAM_WD = 0.01
ADAM_STEP = 10

N_DEV = 8
LANES = 128
TM = 256
TQ = 256
TK = 256
NH = 4
HEAD_SLOT = 128
C_TERMS_Q = 64
C_ONES_K = 64
C_TERMS_K = 67
C_ONES_Q = 67
NEG = -1e30
VMEM_LIMIT = 56 * 1024 * 1024
MESH = pl.DeviceIdType.MESH


def _dot(a, b):
    return jnp.dot(a, b, preferred_element_type=F32)


def _dot_nt(a, b):
    return lax.dot_general(a, b, (((1,), (1,)), ((), ())), preferred_element_type=F32)


def _dot_tn(a, b):
    return lax.dot_general(a, b, (((0,), (0,)), ((), ())), preferred_element_type=F32)


def _sigmoid(x):
    return 1.0 / (1.0 + jnp.exp(-x))


def _softplus(x):
    return jnp.maximum(x, 0.0) + jnp.log(1.0 + jnp.exp(-jnp.abs(x)))


def _split2(x):
    hi = x.astype(BF16)
    lo = (x - hi.astype(F32)).astype(BF16)
    return hi, lo


def _split3(x):
    hi = x.astype(BF16)
    r = x - hi.astype(F32)
    mid = r.astype(BF16)
    lo = (r - mid.astype(F32)).astype(BF16)
    return hi, mid, lo


def _rows_dot_mask(x, mask_bf16):
    hi, lo = _split2(x)
    return _dot(hi, mask_bf16) + _dot(lo, mask_bf16)


def _tri(n, rel):
    r = lax.broadcasted_iota(jnp.int32, (n, n), 0)
    c = lax.broadcasted_iota(jnp.int32, (n, n), 1)
    return rel(r, c).astype(BF16)


def _rms(x):
    r = lax.rsqrt(jnp.mean(x * x, axis=-1, keepdims=True) + EPS)
    return x * r, r


def _rms_bwd(dh, xn, r, g):
    dxn = dh * g
    dx = r * (dxn - xn * jnp.mean(dxn * xn, axis=-1, keepdims=True))
    return dx, jnp.sum(dh * xn, axis=0, keepdims=True)


def _row_spec(tm, cols):
    return pl.BlockSpec((tm, cols), lambda i: (i, 0))


def _row3_spec(g, tm, cols):
    return pl.BlockSpec((g, tm, cols), lambda i: (0, i, 0))


def _col_spec(rows, tm):
    return pl.BlockSpec((rows, tm), lambda i: (0, i))


def _const_spec(shape):
    nd = len(shape)
    return pl.BlockSpec(shape, lambda i: (0,) * nd, pipeline_mode=pl.Buffered(1))


def _acc_spec(shape):
    nd = len(shape)
    return pl.BlockSpec(shape, lambda i: (0,) * nd)


def _seq_params():
    return pltpu.CompilerParams(dimension_semantics=("arbitrary",), vmem_limit_bytes=VMEM_LIMIT)


def _mesh_pos():
    return lax.axis_index("x"), lax.axis_index("y"), lax.axis_index("c")


def _other_chips(x, y):
    return [(1 - x, y), (x, 1 - y), (1 - x, 1 - y)]


def _hbm_specs(n):
    return [pl.BlockSpec(memory_space=pl.ANY)] * n


def _gather_plan(x_refs, out_refs, send_sems, recv_sems, local_sems):
    n = len(x_refs)
    x, y, c = _mesh_pos()
    me, sibling = (x, y, c), (x, y, 1 - c)
    chips = _other_chips(x, y)

    def index(px, py, pc):
        return 4 * px + 2 * py + pc

    def copy(a, k, block, to, src=None):
        slab = out_refs[a].at[index(*block)]
        return pltpu.make_async_remote_copy(
            src_ref=slab if src is None else src, dst_ref=slab,
            send_sem=send_sems.at[7 * a + k], recv_sem=recv_sems.at[7 * a + k], device_id=to, device_id_type=MESH)

    mine = [pltpu.make_async_copy(x_refs[a], out_refs[a].at[index(*me)], local_sems.at[a]) for a in range(n)]
    first = []
    for a in range(n):
        first.append(copy(a, 0, me, sibling, src=x_refs[a]))
        first += [copy(a, 1 + j, me, (cx, cy, c), src=x_refs[a]) for j, (cx, cy) in enumerate(chips)]

    def start():
        for cp in mine + first:
            cp.start()

    def finish():
        passed = []
        for j, (cx, cy) in enumerate(chips):
            for a in range(n):
                copy(a, 1 + j, (cx, cy, c), me).wait_recv()
                passed.append(copy(a, 4 + j, (cx, cy, c), sibling))
                passed[-1].start()
        for a in range(n):
            copy(a, 0, sibling, me).wait_recv()
            for j, (cx, cy) in enumerate(chips):
                copy(a, 4 + j, (cx, cy, 1 - c), me).wait_recv()
        for cp in first + passed:
            cp.wait_send()
        for cp in mine:
            cp.wait()

    return start, finish


def _gather_shapes(shards):
    return [jax.ShapeDtypeStruct((N_DEV,) + s.shape, s.dtype) for s in shards]


def _gather_sems(n):
    return [pltpu.SemaphoreType.DMA((7 * n,)), pltpu.SemaphoreType.DMA((7 * n,)), pltpu.SemaphoreType.DMA((n,))]


def _all_gather(shards):
    n = len(shards)

    def body(*refs):
        start, finish = _gather_plan(refs[:n], refs[n:2 * n], *refs[2 * n:])
        start()
        finish()

    return pl.pallas_call(
        body, name="all_gather_weights", out_shape=_gather_shapes(shards),
        in_specs=_hbm_specs(n), out_specs=_hbm_specs(n), scratch_shapes=_gather_sems(n),
    )(*shards)


def _rs_core_pair(name, partials):
    n = len(partials)

    def body(*refs):
        p_refs, recv_refs = refs[:n], refs[n:2 * n]
        send_sems, recv_sems = refs[2 * n:]
        x, y, c = _mesh_pos()
        for a in range(n):
            for chip in range(4):
                pltpu.make_async_remote_copy(
                    src_ref=p_refs[a].at[2 * chip + (1 - c)], dst_ref=recv_refs[a].at[chip],
                    send_sem=send_sems.at[a], recv_sem=recv_sems.at[a],
                    device_id=(x, y, 1 - c), device_id_type=MESH).start()
        for a in range(n):
            pltpu.make_async_remote_copy(
                src_ref=recv_refs[a], dst_ref=recv_refs[a], send_sem=send_sems.at[a], recv_sem=recv_sems.at[a],
                device_id=(x, y, 1 - c), device_id_type=MESH).wait()

    return pl.pallas_call(
        body, name=name,
        out_shape=[jax.ShapeDtypeStruct((4,) + s.shape[1:], s.dtype) for s in partials],
        in_specs=_hbm_specs(n), out_specs=_hbm_specs(n),
        scratch_shapes=[pltpu.SemaphoreType.DMA((n,)), pltpu.SemaphoreType.DMA((n,))],
    )(*partials)


def _chips_plan(cs_refs, out_refs, send_sems, recv_sems, local_sems):
    n = len(cs_refs)
    x, y, c = _mesh_pos()
    chip = 2 * x + y
    chips = _other_chips(x, y)
    mine = [pltpu.make_async_copy(cs_refs[a].at[chip], out_refs[a].at[chip], local_sems.at[a]) for a in range(n)]
    sends = [pltpu.make_async_remote_copy(
        src_ref=cs_refs[a].at[2 * cx + cy], dst_ref=out_refs[a].at[chip],
        send_sem=send_sems.at[3 * a + j], recv_sem=recv_sems.at[3 * a + j],
        device_id=(cx, cy, c), device_id_type=MESH) for a in range(n) for j, (cx, cy) in enumerate(chips)]

    def start():
        for cp in mine + sends:
            cp.start()

    def finish():
        for a in range(n):
            for j, (cx, cy) in enumerate(chips):
                pltpu.make_async_remote_copy(
                    src_ref=cs_refs[a].at[chip], dst_ref=out_refs[a].at[2 * cx + cy],
                    send_sem=send_sems.at[3 * a + j], recv_sem=recv_sems.at[3 * a + j],
                    device_id=(x, y, c), device_id_type=MESH).wait_recv()
        for cp in sends:
            cp.wait_send()
        for cp in mine:
            cp.wait()

    return start, finish


def _chips_sems(n):
    return [pltpu.SemaphoreType.DMA((3 * n,)), pltpu.SemaphoreType.DMA((3 * n,)), pltpu.SemaphoreType.DMA((n,))]


def _rs_chips(chip_sums):
    n = len(chip_sums)

    def body(*refs):
        start, finish = _chips_plan(refs[:n], refs[n:2 * n], *refs[2 * n:])
        start()
        finish()

    return pl.pallas_call(
        body, name="reduce_scatter_chips",
        out_shape=[jax.ShapeDtypeStruct(s.shape, s.dtype) for s in chip_sums],
        in_specs=_hbm_specs(n), out_specs=_hbm_specs(n), scratch_shapes=_chips_sems(n),
    )(*chip_sums)


def _all_reduce_small(vec):
    rows, cols = vec.shape

    def body(x_ref, land_ref, sum_ref, send_sems, recv_sems):
        x, y, c = _mesh_pos()
        me = 4 * x + 2 * y + c
        land_ref[me] = x_ref[...]
        flips = [(fx, fy, fc) for fx in (0, 1) for fy in (0, 1) for fc in (0, 1)][1:]

        def flipped(f):
            return tuple((1 - v) if b else v for v, b in zip((x, y, c), f))

        sends = []
        for k, f in enumerate(flips):
            sends.append(pltpu.make_async_remote_copy(
                src_ref=x_ref, dst_ref=land_ref.at[me], send_sem=send_sems.at[k], recv_sem=recv_sems.at[k],
                device_id=flipped(f), device_id_type=MESH))
            sends[-1].start()
        for k, f in enumerate(flips):
            px, py, pc = flipped(f)
            pltpu.make_async_remote_copy(
                src_ref=x_ref, dst_ref=land_ref.at[4 * px + 2 * py + pc], send_sem=send_sems.at[k],
                recv_sem=recv_sems.at[k], device_id=(x, y, c), device_id_type=MESH).wait_recv()
        for cp in sends:
            cp.wait_send()
        total = land_ref[0]
        for d in range(1, N_DEV):
            total = total + land_ref[d]
        sum_ref[...] = total

    vm = pl.BlockSpec(memory_space=pltpu.VMEM)
    return pl.pallas_call(
        body, name="all_reduce_small",
        out_shape=(jax.ShapeDtypeStruct((N_DEV, rows, cols), F32), jax.ShapeDtypeStruct((rows, cols), F32)),
        in_specs=[vm], out_specs=(vm, vm),
        scratch_shapes=[pltpu.SemaphoreType.DMA((7,)), pltpu.SemaphoreType.DMA((7,))],
    )(vec)[1]


def _block_rows(rows, cols, itemsize, align, row_off=0):
    best = None
    for t in range(align, rows + 1, align):
        if rows % t == 0 and row_off % t == 0 and t * cols * itemsize <= (1 << 20):
            best = t
    return rows if best is None else best


def _pair_add(name, partial, recv, my_c):
    _, rows, cols = partial.shape
    br = _block_rows(rows, cols, 2, 16)

    def body(c_ref, a_ref, b_ref, o_ref):
        o_ref[...] = (a_ref[...].astype(F32) + b_ref[...].astype(F32)).astype(BF16)

    return pl.pallas_call(
        body, name=name,
        grid_spec=pltpu.PrefetchScalarGridSpec(
            num_scalar_prefetch=1, grid=(4, rows // br),
            in_specs=[pl.BlockSpec((None, None, br, cols), lambda j, i, c_ref: (j, c_ref[0], i, 0)),
                      pl.BlockSpec((None, br, cols), lambda j, i, c_ref: (j, i, 0))],
            out_specs=pl.BlockSpec((None, br, cols), lambda j, i, c_ref: (j, i, 0))),
        out_shape=jax.ShapeDtypeStruct((4, rows, cols), BF16),
    )(my_c.reshape(1).astype(jnp.int32), partial.reshape(4, 2, rows, cols), recv)


def _adam_update(w, g, m, v):
    nm = ADAM_B1 * m + (1.0 - ADAM_B1) * g
    nv = ADAM_B2 * v + (1.0 - ADAM_B2) * (g * g)
    m_hat = nm / (1.0 - ADAM_B1 ** ADAM_STEP)
    v_hat = nv / (1.0 - ADAM_B2 ** ADAM_STEP)
    return -ADAM_LR * (m_hat / (jnp.sqrt(v_hat) + ADAM_EPS) + ADAM_WD * w), nm, nv


def _adamw_parts(name, w, parts, row_off, m, v):
    rows, cols = w.shape
    tr = _block_rows(rows, cols, 4, 16, row_off)
    assert rows % tr == 0 and row_off % tr == 0
    off = row_off // tr

    def body(w_ref, p_ref, m_ref, v_ref, g_ref, d_ref, nm_ref, nv_ref):
        g = p_ref[0].astype(F32)
        for j in range(1, 4):
            g = g + p_ref[j].astype(F32)
        g_ref[...] = g
        d_ref[...], nm_ref[...], nv_ref[...] = _adam_update(w_ref[...], g, m_ref[...], v_ref[...])

    spec = pl.BlockSpec((tr, cols), lambda i: (i, 0))
    shp = jax.ShapeDtypeStruct((rows, cols), F32)
    return pl.pallas_call(
        body, name=name, grid=(rows // tr,), out_shape=(shp,) * 4,
        in_specs=[spec, pl.BlockSpec((4, tr, cols), lambda i: (0, off + i, 0)), spec, spec], out_specs=(spec,) * 4,
    )(w, parts, m, v)


def _adamw_small(tensors):
    n = len(tensors)

    def body(*refs):
        ins, outs = refs[:4 * n], refs[4 * n:]
        for t in range(n):
            w_ref, g_ref, m_ref, v_ref = ins[4 * t:4 * t + 4]
            d, nm, nv = _adam_update(w_ref[...], g_ref[...], m_ref[...], v_ref[...])
            outs[3 * t][...], outs[3 * t + 1][...], outs[3 * t + 2][...] = d, nm, nv

    vm = pl.BlockSpec(memory_space=pltpu.VMEM)
    out = pl.pallas_call(
        body, name="adamw_small",
        out_shape=[jax.ShapeDtypeStruct(t[0].shape, F32) for t in tensors for _ in range(3)],
        in_specs=[vm] * (4 * n), out_specs=[vm] * (3 * n),
    )(*[a for t in tensors for a in t])
    return [tuple(out[3 * t:3 * t + 3]) for t in range(n)]


def _matmul_tn(name, a, b, relu2=False, slabs=False, lhs_t=False):
    squeeze = b.ndim == 2
    if squeeze:
        b = b[None]
    t_len, k_len = a.shape[::-1] if lhs_t else a.shape
    groups, _, n_len = b.shape
    tt = min(t_len, 512)
    tk = min(k_len, 1024)
    tn = n_len // N_DEV if slabs else min(n_len, 1024)
    nt = t_len // tt
    assert not slabs or (squeeze and tn <= 1024)

    def body(a_ref, b_ref, o_ref, acc_ref):
        @pl.when(pl.program_id(3) == 0)
        def _():
            acc_ref[...] = jnp.zeros_like(acc_ref)

        av = a_ref[...]
        if relu2:
            av = jnp.square(jnp.maximum(av.astype(F32), 0.0))
        product = _dot if lhs_t else _dot_tn
        acc_ref[...] += product(av.astype(BF16), b_ref[...].astype(BF16))

        @pl.when(pl.program_id(3) == nt - 1)
        def _():
            o_ref[...] = acc_ref[...].astype(BF16)

    if slabs:
        out_shape = jax.ShapeDtypeStruct((N_DEV, k_len, tn), BF16)
        out_spec = pl.BlockSpec((None, tk, tn), lambda g, i, j, t: (j, i, 0))
    else:
        out_shape = jax.ShapeDtypeStruct((groups, k_len, n_len), BF16)
        out_spec = pl.BlockSpec((None, tk, tn), lambda g, i, j, t: (g, i, j))
    out = pl.pallas_call(
        body, name=name, grid=(groups, k_len // tk, n_len // tn, nt), out_shape=out_shape,
        in_specs=[pl.BlockSpec((tk, tt), lambda g, i, j, t: (i, t)) if lhs_t
                  else pl.BlockSpec((tt, tk), lambda g, i, j, t: (t, i)),
                  pl.BlockSpec((None, tt, tn), lambda g, i, j, t: (g, t, j))],
        out_specs=out_spec,
        scratch_shapes=[pltpu.VMEM((tk, tn), F32)],
        compiler_params=pltpu.CompilerParams(
            dimension_semantics=("parallel", "parallel", "parallel", "arbitrary"), vmem_limit_bytes=VMEM_LIMIT),
    )(a, b)
    return out if slabs else (out[0] if squeeze else out)


def _pad_layout(d):
    names = ("qf", "kf", "vf", "qb", "kb", "vb", "gates", "forget")
    sizes = (N_HEADS * HEAD_SLOT, N_HEADS * HEAD_SLOT, D_BRANCH, D_BRANCH, D_BRANCH, D_BRANCH, 2 * d, LANES)
    out, off = {}, 0
    for n, s in zip(names, sizes):
        out[n] = (off, off + s)
        off += s
    return out, off


def _slot_rows(xt, extra):
    parts = []
    for h in range(N_HEADS):
        parts += [xt[h * HEAD_DIM:(h + 1) * HEAD_DIM, :], extra]
    return jnp.concatenate(parts, axis=0)


def _inproj_fwd(x, g_mix, w_pad, bf_pad, place_q, place_k, ones_q, ones_k, seq):
    t_len, d = x.shape
    lay, _ = _pad_layout(d)
    tiles_per_seq = seq // TM
    slot_w = N_HEADS * HEAD_SLOT

    def body(x_ref, g_ref, w_ref, bf_ref, pq_ref, pk_ref, oq_ref, ok_ref,
             qf_ref, kf_ref, kft_ref, vf_ref, vft_ref, qkvb_ref, kbt_ref, vbt_ref, gl_ref, fpre_ref, carry_ref):
        @pl.when(pl.program_id(0) % tiles_per_seq == 0)
        def _():
            carry_ref[...] = jnp.zeros_like(carry_ref)

        def proj(name):
            lo, hi = lay[name]
            return _dot(h, w_ref[:, lo:hi])

        xn, _ = _rms(x_ref[...])
        h = (xn * g_ref[...]).astype(BF16)
        fpre = proj("forget") + bf_ref[...]
        fpre_ref[...] = fpre
        logf = -_softplus(-fpre)
        lower = _tri(TM, lambda r, c: c <= r)
        hi, mid, lo = _split3(logf)
        c_val = carry_ref[...] + _dot(lower, hi) + _dot(lower, mid) + _dot(lower, lo)
        carry_ref[...] = carry_ref[...] + jnp.sum(logf, axis=0, keepdims=True)
        c3 = _split3(c_val)
        qf_ref[...] = (proj("qf") + sum(_dot(c3[j], pq_ref[j]) for j in range(3)) + oq_ref[...]).astype(BF16)
        kf = proj("kf") - sum(_dot(c3[j], pk_ref[j]) for j in range(3)) + ok_ref[...]
        kf_ref[...] = kf.astype(BF16)
        kft_ref[0] = kf.T.astype(BF16)
        row0 = (lax.broadcasted_iota(jnp.int32, (HEAD_DIM, TM), 0) == 0).astype(F32)
        zeros = jnp.zeros((HEAD_DIM, TM), F32)
        vf = proj("vf")
        vf_ref[...] = vf.astype(BF16)
        vft_ref[0] = _slot_rows(vf.T, row0).astype(BF16)
        qkvb_ref[0] = proj("qb").astype(BF16)
        kb = proj("kb")
        qkvb_ref[1] = kb.astype(BF16)
        kbt_ref[0] = _slot_rows(kb.T, zeros).astype(BF16)
        vb = proj("vb")
        qkvb_ref[2] = vb.astype(BF16)
        vbt_ref[0] = _slot_rows(vb.T, row0).astype(BF16)
        gl_ref[...] = proj("gates").astype(BF16)

    n_tiles = t_len // TM
    slot_shape = jax.ShapeDtypeStruct((t_len, slot_w), BF16)
    t_shape = jax.ShapeDtypeStruct((n_tiles, slot_w, TM), BF16)
    t_spec = pl.BlockSpec((1, slot_w, TM), lambda i: (i, 0, 0))
    return pl.pallas_call(
        body, name="inproj_fwd", grid=(n_tiles,),
        out_shape=(slot_shape, slot_shape, t_shape, jax.ShapeDtypeStruct((t_len, D_BRANCH), BF16), t_shape,
                   jax.ShapeDtypeStruct((3, t_len, D_BRANCH), BF16), t_shape, t_shape,
                   jax.ShapeDtypeStruct((t_len, 2 * d), BF16), jax.ShapeDtypeStruct((t_len, LANES), F32)),
        in_specs=[_row_spec(TM, d), _const_spec((1, d)), _const_spec(w_pad.shape), _const_spec((1, LANES)),
                  _const_spec(place_q.shape), _const_spec(place_k.shape), _const_spec((1, slot_w)),
                  _const_spec((1, slot_w))],
        out_specs=(_row_spec(TM, slot_w), _row_spec(TM, slot_w), t_spec, _row_spec(TM, D_BRANCH), t_spec,
                   _row3_spec(3, TM, D_BRANCH), t_spec, t_spec, _row_spec(TM, 2 * d), _row_spec(TM, LANES)),
        scratch_shapes=[pltpu.VMEM((1, LANES), F32)],
        compiler_params=_seq_params(),
    )(x, g_mix, w_pad, bf_pad, place_q, place_k, ones_q, ones_k)


def _slot_spec(seq):
    return pl.BlockSpec((seq, NH * HEAD_SLOT), lambda b, g: (b, g))


def _slot2_spec(seq):
    return pl.BlockSpec((2, seq, NH * HEAD_SLOT), lambda b, g: (0, b, g))


def _group_spec(seq):
    return pl.BlockSpec((seq, NH * HEAD_DIM), lambda b, g: (b, g))


def _group3_spec(which, seq):
    return pl.BlockSpec((None, seq, NH * HEAD_DIM), lambda b, g: (which, b, g))


def _tblock_spec(seq):
    return pl.BlockSpec((seq // TK, NH * HEAD_SLOT, TK), lambda b, g: (b, g, 0))


def _qrow_spec(seq):
    return pl.BlockSpec((None, NH, seq // TQ, TQ), lambda b, g: (b, g, 0, 0))


def _stat_spec(seq):
    return pl.BlockSpec((None, None, seq, LANES), lambda b, g: (b, g, 0, 0))


def _attn_params():
    return pltpu.CompilerParams(dimension_semantics=("parallel", "parallel"), vmem_limit_bytes=VMEM_LIMIT)


def _serial_attn_params():
    return pltpu.CompilerParams(dimension_semantics=("arbitrary", "arbitrary"), vmem_limit_bytes=VMEM_LIMIT)


def _hcols(hh):
    return slice(hh * HEAD_DIM, (hh + 1) * HEAD_DIM)


def _hslot(hh):
    return slice(hh * HEAD_SLOT, (hh + 1) * HEAD_SLOT)


def _lane(hh):
    return slice(hh, hh + 1)


def _key_query_mask(rel):
    r = lax.broadcasted_iota(jnp.int32, (TK, TQ), 0)
    c = lax.broadcasted_iota(jnp.int32, (TK, TQ), 1)
    return rel(r, c)


def _heads_cat(vals):
    return jnp.concatenate(vals, axis=1)


def _untranspose(acc_t):
    return acc_t.T[:, :HEAD_DIM]


def _fox_fwd(qf, kf, vft, batch, seq):
    def body(q_ref, k_ref, vt_ref, o_ref, lse_ref, m_s, acc_s):
        causal = _key_query_mask(lambda r, c: r <= c)

        def tile(q0, kj, masked):
            krows = pl.ds(pl.multiple_of(kj * TK, TK), TK)
            heads = range(NH)
            sts = [_dot_nt(k_ref[krows, _hslot(hh)], q_ref[pl.ds(q0, TQ), _hslot(hh)]) for hh in heads]
            if masked:
                sts = [jnp.where(causal, st, NEG) for st in sts]
            m_olds = [m_s[hh] for hh in heads]
            m_news = [jnp.maximum(m_olds[hh], jnp.max(sts[hh], axis=0, keepdims=True)) for hh in heads]
            pts = [jnp.exp(sts[hh] - m_news[hh]).astype(BF16) for hh in heads]
            pvs = [_dot(vt_ref[kj, _hslot(hh), :], pts[hh]) for hh in heads]
            for hh in heads:
                acc_s[hh] = jnp.exp(m_olds[hh] - m_news[hh]) * acc_s[hh] + pvs[hh]
                m_s[hh] = m_news[hh]

        def q_loop(qi, _):
            q0 = pl.multiple_of(qi * TQ, TQ)
            m_s[...] = jnp.full(m_s.shape, NEG, F32)
            acc_s[...] = jnp.zeros_like(acc_s)

            def k_loop(kj, _):
                tile(q0, kj, False)
                return 0

            lax.fori_loop(0, qi, k_loop, 0)
            tile(q0, qi, True)
            outs = []
            for hh in range(NH):
                total = acc_s[hh, HEAD_DIM:HEAD_DIM + 1, :]
                outs.append(_untranspose(acc_s[hh] / total))
                lse_ref[hh, pl.ds(qi, 1), :] = m_s[hh] + jnp.log(total)
            o_ref[pl.ds(q0, TQ), :] = _heads_cat(outs).astype(BF16)
            return 0

        lax.fori_loop(0, seq // TQ, q_loop, 0)

    return pl.pallas_call(
        body, name="fox_fwd", grid=(batch, N_HEADS // NH),
        out_shape=(jax.ShapeDtypeStruct((batch * seq, D_BRANCH), BF16),
                   jax.ShapeDtypeStruct((batch, N_HEADS, seq // TQ, TQ), F32)),
        in_specs=[_slot_spec(seq), _slot_spec(seq), _tblock_spec(seq)],
        out_specs=(_group_spec(seq), _qrow_spec(seq)),
        scratch_shapes=[pltpu.VMEM((NH, 1, TQ), F32), pltpu.VMEM((NH, HEAD_SLOT, TQ), F32)],
        compiler_params=_attn_params(),
    )(qf, kf, vft)


def _fox_bwd(qf, kf, kft, vf, o, do, lse, batch, seq):
    n_q = seq // TQ

    def body(q_ref, k_ref, kt_ref, v_ref, o_ref, do_ref, lse_ref, dqk_ref, dv_ref, dcq_ref, dck_ref,
             delta_s, dqt_acc, dk_s, dv_s):
        causal = _key_query_mask(lambda r, c: r <= c)
        ones8 = jnp.ones((8, HEAD_DIM), BF16)
        dqt_acc[...] = jnp.zeros_like(dqt_acc)

        def prep(qi, _):
            rows = pl.ds(pl.multiple_of(qi * TQ, TQ), TQ)
            for hh in range(NH):
                hi, lo = _split2(do_ref[rows, _hcols(hh)].astype(F32) * o_ref[rows, _hcols(hh)].astype(F32))
                delta_s[hh, pl.ds(qi, 1), :] = (_dot_nt(ones8, hi) + _dot_nt(ones8, lo))[0:1, :]
            return 0

        lax.fori_loop(0, n_q, prep, 0)

        def tile(qi, kj, masked):
            rows = pl.ds(pl.multiple_of(qi * TQ, TQ), TQ)
            krows = pl.ds(pl.multiple_of(kj * TK, TK), TK)
            heads = range(NH)
            qs = [q_ref[rows, _hslot(hh)] for hh in heads]
            douts = [do_ref[rows, _hcols(hh)] for hh in heads]
            sts = [_dot_nt(k_ref[krows, _hslot(hh)], qs[hh]) for hh in heads]
            dps = [_dot_nt(v_ref[krows, _hcols(hh)], douts[hh]) for hh in heads]
            pts = [jnp.exp(sts[hh] - lse_ref[hh, pl.ds(qi, 1), :]) for hh in heads]
            if masked:
                pts = [jnp.where(causal, pt, 0.0) for pt in pts]
            dsts = [(pts[hh] * (dps[hh] - delta_s[hh, pl.ds(qi, 1), :])).astype(BF16) for hh in heads]
            for hh in heads:
                dv_s[hh] += _dot(pts[hh].astype(BF16), douts[hh])
                dk_s[hh] += _dot(dsts[hh], qs[hh])
                dqt_acc[hh, qi] += _dot(kt_ref[kj, _hslot(hh), :], dsts[hh])

        def k_loop(kj, _):
            krows = pl.ds(pl.multiple_of(kj * TK, TK), TK)
            dk_s[...] = jnp.zeros_like(dk_s)
            dv_s[...] = jnp.zeros_like(dv_s)
            tile(kj, kj, True)

            def q_loop(qi, _):
                tile(qi, kj, False)
                return 0

            lax.fori_loop(kj + 1, n_q, q_loop, 0)
            dqk_ref[1, krows, :] = _heads_cat([dk_s[hh] for hh in range(NH)]).astype(BF16)
            dv_ref[krows, :] = _heads_cat([dv_s[hh] for hh in range(NH)]).astype(BF16)
            for hh in range(NH):
                dck_ref[krows, _lane(hh)] = dk_s[hh, :, C_ONES_Q:C_ONES_Q + 1]
            return 0

        lax.fori_loop(0, seq // TK, k_loop, 0)

        def finish(qi, _):
            rows = pl.ds(pl.multiple_of(qi * TQ, TQ), TQ)
            dqk_ref[0, rows, :] = _heads_cat([dqt_acc[hh, qi].T for hh in range(NH)]).astype(BF16)
            for hh in range(NH):
                dcq_ref[hh, pl.ds(qi, 1), :] = dqt_acc[hh, qi, C_ONES_K:C_ONES_K + 1, :]
            return 0

        lax.fori_loop(0, n_q, finish, 0)

    return pl.pallas_call(
        body, name="fox_bwd", grid=(batch, N_HEADS // NH),
        out_shape=(jax.ShapeDtypeStruct((2, batch * seq, N_HEADS * HEAD_SLOT), BF16),
                   jax.ShapeDtypeStruct((batch * seq, D_BRANCH), BF16),
                   jax.ShapeDtypeStruct((batch, N_HEADS, seq // TQ, TQ), F32),
                   jax.ShapeDtypeStruct((batch, N_HEADS // NH, seq, LANES), F32)),
        in_specs=[_slot_spec(seq), _slot_spec(seq), _tblock_spec(seq), _group_spec(seq), _group_spec(seq),
                  _group_spec(seq), _qrow_spec(seq)],
        out_specs=(_slot2_spec(seq), _group_spec(seq), _qrow_spec(seq), _stat_spec(seq)),
        scratch_shapes=[pltpu.VMEM((NH, n_q, TQ), F32), pltpu.VMEM((NH, n_q, HEAD_SLOT, TQ), F32),
                        pltpu.VMEM((NH, TK, HEAD_SLOT), F32), pltpu.VMEM((NH, TK, HEAD_DIM), F32)],
        compiler_params=_attn_params(),
    )(qf, kf, kft, vf, o, do, lse)


def _first_last_step():
    step = pl.program_id(0) * pl.num_programs(1) + pl.program_id(1)
    return step == 0, step == pl.num_programs(0) * pl.num_programs(1) - 1


def _sb_fwd(qkvb, vbt, batch, seq, shards):
    n = len(shards)

    def body(q_ref, k_ref, vt_ref, *rest):
        x_refs, (o_ref, lt_ref), out_refs = rest[:n], rest[n:n + 2], rest[n + 2:2 * n + 2]
        run_s, acc_s = rest[2 * n + 2:2 * n + 4]
        gather_start, gather_finish = _gather_plan(x_refs, out_refs, *rest[2 * n + 4:])
        first_step, last_step = _first_last_step()
        pl.when(first_step)(gather_start)
        strict = _key_query_mask(lambda r, c: r < c)
        later = _tri(TK, lambda r, c: c > r)

        def tile(q0, kj, masked):
            krows = pl.ds(pl.multiple_of(kj * TK, TK), TK)
            heads = range(NH)
            zts = [_dot_nt(k_ref[krows, _hcols(hh)], q_ref[pl.ds(q0, TQ), _hcols(hh)]) for hh in heads]
            lgs = [-_softplus(zt) for zt in zts]
            if masked:
                lgs = [jnp.where(strict, lg, 0.0) for lg in lgs]
            parts = [_split2(lg) for lg in lgs]
            sufs = [_dot(later, hi) + _dot(later, lo) for hi, lo in parts]
            ats = [jnp.exp(zts[hh] + lgs[hh] + run_s[hh] + sufs[hh]) for hh in heads]
            if masked:
                ats = [jnp.where(strict, at, 0.0) for at in ats]
            for hh in heads:
                acc_s[hh] += _dot(vt_ref[kj, _hslot(hh), :], ats[hh].astype(BF16))
                run_s[hh] += jnp.sum(lgs[hh], axis=0, keepdims=True)

        def q_loop(qi, _):
            q0 = pl.multiple_of(qi * TQ, TQ)
            run_s[...] = jnp.zeros_like(run_s)
            acc_s[...] = jnp.zeros_like(acc_s)
            tile(q0, qi, True)

            def k_loop(kk, _):
                tile(q0, qi - 1 - kk, False)
                return 0

            lax.fori_loop(0, qi, k_loop, 0)
            o_ref[pl.ds(q0, TQ), :] = _heads_cat([_untranspose(acc_s[hh]) for hh in range(NH)]).astype(BF16)
            for hh in range(NH):
                lt_ref[hh, pl.ds(qi, 1), :] = run_s[hh]
            return 0

        lax.fori_loop(0, seq // TQ, q_loop, 0)
        pl.when(last_step)(gather_finish)

    out = pl.pallas_call(
        body, name="sb_fwd", grid=(batch, N_HEADS // NH),
        out_shape=[jax.ShapeDtypeStruct((batch * seq, D_BRANCH), BF16),
                   jax.ShapeDtypeStruct((batch, N_HEADS, seq // TQ, TQ), F32)] + _gather_shapes(shards),
        in_specs=[_group3_spec(0, seq), _group3_spec(1, seq), _tblock_spec(seq)] + _hbm_specs(n),
        out_specs=[_group_spec(seq), _qrow_spec(seq)] + _hbm_specs(n),
        scratch_shapes=[pltpu.VMEM((NH, 1, TQ), F32), pltpu.VMEM((NH, HEAD_SLOT, TQ), F32)] + _gather_sems(n),
        compiler_params=_serial_attn_params(),
    )(qkvb, qkvb, vbt, *shards)
    return out[0], out[1], out[2:]


def _sb_bwd(qkvb, kbt, do, ltot, batch, seq, chip_sums):
    n = len(chip_sums)

    def body(q_ref, k_ref, v_ref, kt_ref, do_ref, lt_ref, *rest):
        cs_refs, dqkv_ref, out_refs = rest[:n], rest[n], rest[n + 1:2 * n + 1]
        dk_acc, dv_acc, ls_s, gs_s, dqt_s = rest[2 * n + 1:2 * n + 6]
        chips_start, chips_finish = _chips_plan(cs_refs, out_refs, *rest[2 * n + 6:])
        first_step, last_step = _first_last_step()
        pl.when(first_step)(chips_start)
        strict = _key_query_mask(lambda r, c: r < c)
        upto = _tri(TK, lambda r, c: c <= r)
        before = _tri(TK, lambda r, c: c < r)
        dk_acc[...] = jnp.zeros_like(dk_acc)
        dv_acc[...] = jnp.zeros_like(dv_acc)

        def tile(qi, kj, masked):
            rows = pl.ds(pl.multiple_of(qi * TQ, TQ), TQ)
            krows = pl.ds(pl.multiple_of(kj * TK, TK), TK)
            heads = range(NH)
            qs = [q_ref[rows, _hcols(hh)] for hh in heads]
            douts = [do_ref[rows, _hcols(hh)] for hh in heads]
            zts = [_dot_nt(k_ref[krows, _hcols(hh)], qs[hh]) for hh in heads]
            das = [_dot_nt(v_ref[krows, _hcols(hh)], douts[hh]) for hh in heads]
            lgs = [-_softplus(zt) for zt in zts]
            if masked:
                lgs = [jnp.where(strict, lg, 0.0) for lg in lgs]
            parts = [_split2(lg) for lg in lgs]
            prefs = [_dot(upto, hi) + _dot(upto, lo) for hi, lo in parts]
            ats = [jnp.exp(zts[hh] + lgs[hh] + (lt_ref[hh, pl.ds(qi, 1), :] - ls_s[hh]) - prefs[hh]) for hh in heads]
            if masked:
                ats = [jnp.where(strict, at, 0.0) for at in ats]
            gts = [das[hh] * ats[hh] for hh in heads]
            us = [gs_s[hh] + _dot(before, gts[hh].astype(BF16)) for hh in heads]
            dzts = [(jnp.exp(lgs[hh]) * (gts[hh] + us[hh]) - us[hh]).astype(BF16) for hh in heads]
            for hh in heads:
                dk_acc[hh, krows, :] += _dot(dzts[hh], qs[hh])
                dv_acc[hh, krows, :] += _dot(ats[hh].astype(BF16), douts[hh])
                dqt_s[hh] += _dot(kt_ref[kj, _hslot(hh), :], dzts[hh])
                ls_s[hh] += jnp.sum(lgs[hh], axis=0, keepdims=True)
                gs_s[hh] += jnp.sum(gts[hh], axis=0, keepdims=True)

        def q_loop(qi, _):
            ls_s[...] = jnp.zeros_like(ls_s)
            gs_s[...] = jnp.zeros_like(gs_s)
            dqt_s[...] = jnp.zeros_like(dqt_s)

            def k_loop(kj, _):
                tile(qi, kj, False)
                return 0

            lax.fori_loop(0, qi, k_loop, 0)
            tile(qi, qi, True)
            dqkv_ref[0, pl.ds(pl.multiple_of(qi * TQ, TQ), TQ), :] = _heads_cat(
                [_untranspose(dqt_s[hh]) for hh in range(NH)]).astype(BF16)
            return 0

        lax.fori_loop(0, seq // TQ, q_loop, 0)
        dqkv_ref[1] = _heads_cat([dk_acc[hh] for hh in range(NH)]).astype(BF16)
        dqkv_ref[2] = _heads_cat([dv_acc[hh] for hh in range(NH)]).astype(BF16)
        pl.when(last_step)(chips_finish)

    out = pl.pallas_call(
        body, name="sb_bwd", grid=(batch, N_HEADS // NH),
        out_shape=[jax.ShapeDtypeStruct((3, batch * seq, D_BRANCH), BF16)]
        + [jax.ShapeDtypeStruct(s.shape, s.dtype) for s in chip_sums],
        in_specs=[_group3_spec(0, seq), _group3_spec(1, seq), _group3_spec(2, seq), _tblock_spec(seq),
                  _group_spec(seq), _qrow_spec(seq)] + _hbm_specs(n),
        out_specs=[pl.BlockSpec((3, seq, NH * HEAD_DIM), lambda b, g: (0, b, g))] + _hbm_specs(n),
        scratch_shapes=[pltpu.VMEM((NH, seq, HEAD_DIM), F32), pltpu.VMEM((NH, seq, HEAD_DIM), F32),
                        pltpu.VMEM((NH, 1, TQ), F32), pltpu.VMEM((NH, 1, TQ), F32),
                        pltpu.VMEM((NH, HEAD_SLOT, TQ), F32)] + _chips_sems(n),
        compiler_params=_serial_attn_params(),
    )(qkvb, qkvb, qkvb, kbt, do, ltot, *chip_sums)
    return out[0], out[1:]


def _forget_bwd(dcq_tok, dck_tok, fpre, batch, seq):
    t_len = batch * seq
    tiles = seq // TM

    def rev(i):
        return ((i // tiles) * tiles + (tiles - 1 - i % tiles), 0)

    def body(dcq_ref, dck_ref, f_ref, df_ref, db_ref, carry_ref):
        i = pl.program_id(0)

        @pl.when(i == 0)
        def _():
            db_ref[...] = jnp.zeros_like(db_ref)

        @pl.when(i % tiles == 0)
        def _():
            carry_ref[...] = jnp.zeros_like(carry_ref)

        dc = dcq_ref[...] - dck_ref[...]
        upper = _tri(TM, lambda r, c: c >= r)
        hi, mid, lo = _split3(dc)
        dlogf = carry_ref[...] + _dot(upper, hi) + _dot(upper, mid) + _dot(upper, lo)
        carry_ref[...] = carry_ref[...] + jnp.sum(dc, axis=0, keepdims=True)
        df = dlogf * _sigmoid(-f_ref[...])
        df_ref[...] = df.astype(BF16)
        db_ref[...] += jnp.sum(df, axis=0, keepdims=True)

    return pl.pallas_call(
        body, name="forget_bwd", grid=(t_len // TM,),
        out_shape=(jax.ShapeDtypeStruct((t_len, LANES), BF16), jax.ShapeDtypeStruct((1, LANES), F32)),
        in_specs=[pl.BlockSpec((TM, LANES), rev)] * 3,
        out_specs=(pl.BlockSpec((TM, LANES), rev), _acc_spec((1, LANES))),
        scratch_shapes=[pltpu.VMEM((1, LANES), F32)],
        compiler_params=_seq_params(),
    )(dcq_tok, dck_tok, fpre)


def _mix_fwd(o_fox, o_sb, gl, x, w_bf, w_bs, w_out, b_gate):
    t_len, d = x.shape

    def body(of_ref, os_ref, gl_ref, x_ref, wbf_ref, wbs_ref, wo_ref, bg_ref, x1_ref):
        br_f = _dot(of_ref[...], wbf_ref[...])
        br_s = _dot(os_ref[...], wbs_ref[...])
        ga = _sigmoid(gl_ref[:, :d].astype(F32) + bg_ref[0:1, :])
        gb = _sigmoid(gl_ref[:, d:].astype(F32) + bg_ref[1:2, :])
        merged = ga * br_f + gb * br_s
        x1_ref[...] = x_ref[...] + _dot(merged.astype(BF16), wo_ref[...])

    return pl.pallas_call(
        body, name="mix_fwd", grid=(t_len // TM,),
        out_shape=jax.ShapeDtypeStruct((t_len, d), F32),
        in_specs=[_row_spec(TM, D_BRANCH), _row_spec(TM, D_BRANCH), _row_spec(TM, 2 * d), _row_spec(TM, d),
                  _const_spec(w_bf.shape), _const_spec(w_bs.shape), _const_spec(w_out.shape), _const_spec(b_gate.shape)],
        out_specs=_row_spec(TM, d),
        compiler_params=_seq_params(),
    )(o_fox, o_sb, gl, x, w_bf, w_bs, w_out, b_gate)


def _ff_chunk(d_ff):
    return min(d_ff, 1024)


def _mlp_fwd(x1, g_mlp, w_up, w_down):
    t_len, d = x1.shape
    d_ff = w_up.shape[1]
    ch = _ff_chunk(d_ff)

    def body(x1_ref, g_ref, wu_ref, wd_ref, a_ref, x2_ref):
        x1v = x1_ref[...]
        xn, _ = _rms(x1v)
        h = (xn * g_ref[...]).astype(BF16)
        acc = x1v
        for j in range(d_ff // ch):
            a = _dot(h, wu_ref[:, j * ch:(j + 1) * ch])
            a_ref[:, j * ch:(j + 1) * ch] = a.astype(BF16)
            acc = acc + _dot(jnp.square(jnp.maximum(a, 0.0)).astype(BF16), wd_ref[j * ch:(j + 1) * ch, :])
        x2_ref[...] = acc

    return pl.pallas_call(
        body, name="mlp_fwd", grid=(t_len // TM,),
        out_shape=(jax.ShapeDtypeStruct((t_len, d_ff), BF16), jax.ShapeDtypeStruct((t_len, d), F32)),
        in_specs=[_row_spec(TM, d), _const_spec((1, d)), _const_spec(w_up.shape), _const_spec(w_down.shape)],
        out_specs=(_row_spec(TM, d_ff), _row_spec(TM, d)),
        compiler_params=_seq_params(),
    )(x1, g_mlp, w_up, w_down)


def _head_fwd_bwd(x2, p, target, g_ple, g_final, w_pg, w_ple):
    t_len, d = x2.shape
    d_ple = p.shape[1]

    def body(x2_ref, p_ref, t_ref, gp_ref, gf_ref, wpg_ref, wple_ref,
             dx2_ref, h3_ref, dpre_ref, dpe_ref, loss_ref, dgp_ref, dgf_ref):
        @pl.when(pl.program_id(0) == 0)
        def _():
            loss_ref[...] = jnp.zeros_like(loss_ref)
            dgp_ref[...] = jnp.zeros_like(dgp_ref)
            dgf_ref[...] = jnp.zeros_like(dgf_ref)

        x2v = x2_ref[...]
        x2n, r3 = _rms(x2v)
        h3 = (x2n * gp_ref[...]).astype(BF16)
        h3_ref[...] = h3
        gate = _sigmoid(_dot(h3, wpg_ref[...]))
        pe = _dot(p_ref[...].astype(BF16), wple_ref[...])
        x3n, r4 = _rms(x2v + gate * pe)
        err = x3n * gf_ref[...] - t_ref[...]
        loss_ref[...] += jnp.full(loss_ref.shape, (0.5 / d) * jnp.sum(err * err), F32)
        dx3, dgf = _rms_bwd(err * (1.0 / d), x3n, r4, gf_ref[...])
        dgf_ref[...] += dgf
        dpe_ref[...] = (dx3 * gate).astype(BF16)
        dpre = (dx3 * pe * gate * (1.0 - gate)).astype(BF16)
        dpre_ref[...] = dpre
        dres, dgp = _rms_bwd(_dot_nt(dpre, wpg_ref[...]), x2n, r3, gp_ref[...])
        dgp_ref[...] += dgp
        dx2_ref[...] = dx3 + dres

    shp_b = jax.ShapeDtypeStruct((t_len, d), BF16)
    return pl.pallas_call(
        body, name="head_fwd_bwd", grid=(t_len // TM,),
        out_shape=(jax.ShapeDtypeStruct((t_len, d), F32), shp_b, shp_b, shp_b,
                   jax.ShapeDtypeStruct((1, LANES), F32), jax.ShapeDtypeStruct((1, d), F32),
                   jax.ShapeDtypeStruct((1, d), F32)),
        in_specs=[_row_spec(TM, d), _row_spec(TM, d_ple), _row_spec(TM, d), _const_spec((1, d)), _const_spec((1, d)),
                  _const_spec(w_pg.shape), _const_spec(w_ple.shape)],
        out_specs=(_row_spec(TM, d), _row_spec(TM, d), _row_spec(TM, d), _row_spec(TM, d),
                   _acc_spec((1, LANES)), _acc_spec((1, d)), _acc_spec((1, d))),
        compiler_params=_seq_params(),
    )(x2, p, target, g_ple, g_final, w_pg, w_ple)


def _mlp_bwd(dx2, a, x1, g_mlp, w_up, w_down):
    t_len, d = x1.shape
    d_ff = w_up.shape[1]
    ch = _ff_chunk(d_ff)

    def body(dx2_ref, a_ref, x1_ref, g_ref, wu_ref, wd_ref, dx1_ref, da_ref, h2_ref, dg_ref):
        @pl.when(pl.program_id(0) == 0)
        def _():
            dg_ref[...] = jnp.zeros_like(dg_ref)

        dx2v = dx2_ref[...]
        dx2b = dx2v.astype(BF16)
        xn, r = _rms(x1_ref[...])
        h2_ref[...] = (xn * g_ref[...]).T.astype(BF16)
        dh = jnp.zeros((TM, d), F32)
        for j in range(d_ff // ch):
            dact = _dot_nt(dx2b, wd_ref[j * ch:(j + 1) * ch, :])
            da = (dact * 2.0 * jnp.maximum(a_ref[:, j * ch:(j + 1) * ch].astype(F32), 0.0)).astype(BF16)
            da_ref[:, j * ch:(j + 1) * ch] = da
            dh = dh + _dot_nt(da, wu_ref[:, j * ch:(j + 1) * ch])
        dres, dg = _rms_bwd(dh, xn, r, g_ref[...])
        dg_ref[...] += dg
        dx1_ref[...] = dx2v + dres

    return pl.pallas_call(
        body, name="mlp_bwd", grid=(t_len // TM,),
        out_shape=(jax.ShapeDtypeStruct((t_len, d), F32), jax.ShapeDtypeStruct((t_len, d_ff), BF16),
                   jax.ShapeDtypeStruct((d, t_len), BF16), jax.ShapeDtypeStruct((1, d), F32)),
        in_specs=[_row_spec(TM, d), _row_spec(TM, d_ff), _row_spec(TM, d), _const_spec((1, d)),
                  _const_spec(w_up.shape), _const_spec(w_down.shape)],
        out_specs=(_row_spec(TM, d), _row_spec(TM, d_ff), _col_spec(d, TM), _acc_spec((1, d))),
        compiler_params=_seq_params(),
    )(dx2, a, x1, g_mlp, w_up, w_down)


def _mix_bwd(dx1, o_fox, o_sb, gl, w_bf, w_bs, w_out, b_gate):
    t_len, d = dx1.shape

    def body(dx1_ref, of_ref, os_ref, gl_ref, wbf_ref, wbs_ref, wo_ref, bg_ref,
             mg_ref, dbf_ref, dbs_ref, dgl_ref, dof_ref, dos_ref, dbg_ref):
        @pl.when(pl.program_id(0) == 0)
        def _():
            dbg_ref[...] = jnp.zeros_like(dbg_ref)

        dmerged = _dot_nt(dx1_ref[...].astype(BF16), wo_ref[...])
        br_f = _dot(of_ref[...], wbf_ref[...])
        br_s = _dot(os_ref[...], wbs_ref[...])
        ga = _sigmoid(gl_ref[:, :d].astype(F32) + bg_ref[0:1, :])
        gb = _sigmoid(gl_ref[:, d:].astype(F32) + bg_ref[1:2, :])
        mg_ref[...] = (ga * br_f + gb * br_s).astype(BF16)
        dbf = (dmerged * ga).astype(BF16)
        dbs = (dmerged * gb).astype(BF16)
        dbf_ref[...] = dbf
        dbs_ref[...] = dbs
        dla = dmerged * br_f * ga * (1.0 - ga)
        dlb = dmerged * br_s * gb * (1.0 - gb)
        dgl_ref[:, :d] = dla.astype(BF16)
        dgl_ref[:, d:] = dlb.astype(BF16)
        dbg_ref[0:1, :] += jnp.sum(dla, axis=0, keepdims=True)
        dbg_ref[1:2, :] += jnp.sum(dlb, axis=0, keepdims=True)
        dof_ref[...] = _dot_nt(dbf, wbf_ref[...]).astype(BF16)
        dos_ref[...] = _dot_nt(dbs, wbs_ref[...]).astype(BF16)

    shp_d = jax.ShapeDtypeStruct((t_len, d), BF16)
    shp_h = jax.ShapeDtypeStruct((t_len, D_BRANCH), BF16)
    return pl.pallas_call(
        body, name="mix_bwd", grid=(t_len // TM,),
        out_shape=(shp_d, shp_d, shp_d, jax.ShapeDtypeStruct((t_len, 2 * d), BF16), shp_h, shp_h,
                   jax.ShapeDtypeStruct((2, d), F32)),
        in_specs=[_row_spec(TM, d), _row_spec(TM, D_BRANCH), _row_spec(TM, D_BRANCH), _row_spec(TM, 2 * d),
                  _const_spec(w_bf.shape), _const_spec(w_bs.shape), _const_spec(w_out.shape), _const_spec(b_gate.shape)],
        out_specs=(_row_spec(TM, d), _row_spec(TM, d), _row_spec(TM, d), _row_spec(TM, 2 * d),
                   _row_spec(TM, D_BRANCH), _row_spec(TM, D_BRANCH), _acc_spec((2, d))),
        compiler_params=_seq_params(),
    )(dx1, o_fox, o_sb, gl, w_bf, w_bs, w_out, b_gate)


def _inproj_bwd(dqk_f, dv_f, dqkv_b, dgl, df, dx1, x, g_mix, w_pad):
    t_len, d = x.shape
    lay, _ = _pad_layout(d)
    slot_w = N_HEADS * HEAD_SLOT

    def body(dqk_ref, dvf_ref, db_ref, dgl_ref, df_ref, dx1_ref, x_ref, g_ref, w_ref, dx_ref, h1_ref, dg_ref):
        @pl.when(pl.program_id(0) == 0)
        def _():
            dg_ref[...] = jnp.zeros_like(dg_ref)

        def back(piece, name):
            lo, hi = lay[name]
            return _dot_nt(piece, w_ref[:, lo:hi])

        xn, r = _rms(x_ref[...])
        h1_ref[...] = (xn * g_ref[...]).T.astype(BF16)
        dh = (back(df_ref[...], "forget") + back(dgl_ref[...], "gates") + back(dqk_ref[0], "qf")
              + back(dqk_ref[1], "kf") + back(dvf_ref[...], "vf") + back(db_ref[0], "qb") + back(db_ref[1], "kb")
              + back(db_ref[2], "vb"))
        dres, dg = _rms_bwd(dh, xn, r, g_ref[...])
        dg_ref[...] += dg
        dx_ref[...] = dx1_ref[...] + dres

    return pl.pallas_call(
        body, name="inproj_bwd", grid=(t_len // TM,),
        out_shape=(jax.ShapeDtypeStruct((t_len, d), F32), jax.ShapeDtypeStruct((d, t_len), BF16),
                   jax.ShapeDtypeStruct((1, d), F32)),
        in_specs=[_row3_spec(2, TM, slot_w), _row_spec(TM, D_BRANCH), _row3_spec(3, TM, D_BRANCH),
                  _row_spec(TM, 2 * d), _row_spec(TM, LANES), _row_spec(TM, d), _row_spec(TM, d), _const_spec((1, d)),
                  _const_spec(w_pad.shape)],
        out_specs=(_row_spec(TM, d), _col_spec(d, TM), _acc_spec((1, d))),
        compiler_params=_seq_params(),
    )(dqk_f, dv_f, dqkv_b, dgl, df, dx1, x, g_mix, w_pad)


def _cols_to_slabs(full):
    r, c8 = full.shape
    return full.reshape(r, N_DEV, c8 // N_DEV).transpose(1, 0, 2)


def _slabs_to_cols(slabs):
    n, r, c = slabs.shape
    return slabs.transpose(1, 0, 2).reshape(r, n * c)


def _win_sizes(d):
    return (D_BRANCH, D_BRANCH, D_BRANCH, N_HEADS, D_BRANCH, D_BRANCH, D_BRANCH, d, d)


def _split_win(w, d):
    out, off = [], 0
    for s in _win_sizes(d):
        out.append(w[:, off:off + s])
        off += s
    return out


def _to_slots(w):
    r = w.shape[0]
    return jnp.pad(w.reshape(r, N_HEADS, HEAD_DIM), ((0, 0), (0, 0), (0, HEAD_SLOT - HEAD_DIM))).reshape(r, -1)


def _from_slots(w):
    r = w.shape[0]
    return w.reshape(r, N_HEADS, HEAD_SLOT)[:, :, :HEAD_DIM].reshape(r, N_HEADS * HEAD_DIM)


def _pad_win(w_full, d):
    qa, ka, va, fa, qb, kb, vb, ga, gb = _split_win(w_full, d)
    scale = HEAD_DIM ** -0.5
    fpad = jnp.pad(fa, ((0, 0), (0, LANES - N_HEADS)))
    return jnp.concatenate([_to_slots(qa * scale), _to_slots(ka), va, qb * scale, kb, vb, ga, gb, fpad], axis=1)


def _unpad_dwin(dqk_f, dv_f, dqkv_b, dgates, dforget, d):
    scale = HEAD_DIM ** -0.5
    return jnp.concatenate([_from_slots(dqk_f[0]) * scale, _from_slots(dqk_f[1]), dv_f, dforget[:, :N_HEADS],
                            dqkv_b[0] * scale, dqkv_b[1], dqkv_b[2], dgates], axis=1)


def _c_lane_constants():
    head = jnp.arange(LANES)[:, None]
    lane = jnp.arange(N_HEADS * HEAD_SLOT)[None, :]
    in_head = (lane // HEAD_SLOT == head) & (head < N_HEADS)

    def place(first):
        return jnp.stack([(in_head & (lane % HEAD_SLOT == first + j)) for j in range(3)]).astype(BF16)

    def ones(first):
        off = lane % HEAD_SLOT
        return ((off >= first) & (off < first + 3)).astype(F32)

    return place(C_TERMS_Q), place(C_TERMS_K), ones(C_ONES_Q), ones(C_ONES_K)


def _pad_rows(a, rows):
    return jnp.pad(a, [(0, 0)] * (a.ndim - 2) + [(0, rows - a.shape[-2]), (0, 0)])


def kernel(x, p, g_mix, w_in, b_forget, b_gate, w_branch_fox, w_branch_sb, w_out, g_mlp, w_up, w_down, g_ple, w_ple_gate, w_ple, g_final, loss_target, m_g_mix, m_w_in, m_b_forget, m_b_gate, m_w_branch_fox, m_w_branch_sb, m_w_out, m_g_mlp, m_w_up, m_w_down, m_g_ple, m_w_ple_gate, m_w_ple, m_g_final, v_g_mix, v_w_in, v_b_forget, v_b_gate, v_w_branch_fox, v_w_branch_sb, v_w_out, v_g_mlp, v_w_up, v_w_down, v_g_ple, v_w_ple_gate, v_w_ple, v_g_final):
    batch, seq, d = x.shape
    t_len = batch * seq
    d_ple = p.shape[-1]
    d_ff = w_up.shape[-1] * N_DEV
    dn = d // N_DEV
    fn = d_ff // N_DEV
    my_c = lax.axis_index("c")
    my_dev = 4 * lax.axis_index("x") + 2 * lax.axis_index("y") + my_c

    bg_hi = b_gate[0].astype(BF16)
    bg_r = b_gate[0] - bg_hi.astype(F32)
    bg_mid = bg_r.astype(BF16)
    bg_lo = (bg_r - bg_mid.astype(F32)).astype(BF16)
    narrow_rows = 2 * D_BRANCH + d_ple + 6
    narrow_rows_pad = -(-narrow_rows // 16) * 16
    wide = jnp.concatenate([w_out[0], w_down[0], w_ple_gate[0]], axis=0).astype(BF16)
    narrow = _pad_rows(jnp.concatenate(
        [w_branch_fox[0].astype(BF16), w_branch_sb[0].astype(BF16), w_ple[0].astype(BF16), bg_hi, bg_mid, bg_lo],
        axis=0), narrow_rows_pad)
    g_in, = _all_gather([w_in[0].astype(BF16)])
    w_pad = _pad_win(_slabs_to_cols(g_in), d)
    bf_pad = jnp.pad(b_forget, ((0, 0), (0, LANES - N_HEADS)))
    place_q, place_k, ones_q, ones_k = _c_lane_constants()

    x2d = x.reshape(t_len, d)
    p2d = p.reshape(t_len, d_ple)
    tgt2d = loss_target.reshape(t_len, d)
    qf, kf, kft, vf, vft, qkvb, kbt, vbt, gl, fpre = _inproj_fwd(
        x2d, g_mix, w_pad, bf_pad, place_q, place_k, ones_q, ones_k, seq)
    o_sb, ltot, (g_up, g_wide, g_narrow) = _sb_fwd(qkvb, vbt, batch, seq, [w_up[0].astype(BF16), wide, narrow])
    o_fox, lse = _fox_fwd(qf, kf, vft, batch, seq)
    w_up_full = _slabs_to_cols(g_up)
    w_out_full = g_wide[:, :dn].reshape(d, d)
    w_down_full = g_wide[:, dn:dn + fn].reshape(d_ff, d)
    w_pg_full = g_wide[:, dn + fn:].reshape(d, d)
    w_bf_full = _slabs_to_cols(g_narrow[:, :D_BRANCH])
    w_bs_full = _slabs_to_cols(g_narrow[:, D_BRANCH:2 * D_BRANCH])
    w_ple_full = _slabs_to_cols(g_narrow[:, 2 * D_BRANCH:2 * D_BRANCH + d_ple])
    bg_terms = g_narrow[:, 2 * D_BRANCH + d_ple:narrow_rows].astype(F32)
    b_gate_full = _slabs_to_cols(bg_terms[:, 0:2] + bg_terms[:, 2:4] + bg_terms[:, 4:6])
    x1 = _mix_fwd(o_fox, o_sb, gl, x2d, w_bf_full, w_bs_full, w_out_full, b_gate_full)
    a_up, x2 = _mlp_fwd(x1, g_mlp, w_up_full, w_down_full)

    dx2, h3, dpre, dpe, loss_acc, dg_ple, dg_final = _head_fwd_bwd(
        x2, p2d, tgt2d, g_ple, g_final.reshape(1, d), w_pg_full, w_ple_full)
    dx1, da_up, h2t, dg_mlp = _mlp_bwd(dx2, a_up, x1, g_mlp, w_up_full, w_down_full)
    merged, dbr_f, dbr_s, dgl, do_fox, do_sb, dbg = _mix_bwd(
        dx1, o_fox, o_sb, gl, w_bf_full, w_bs_full, w_out_full, b_gate_full)

    def column_shards(name, lhs, rhs, lhs_t=False):
        if (rhs.shape[-1] // N_DEV) % (4 * LANES) == 0:
            return _matmul_tn(name, lhs, rhs, slabs=True, lhs_t=lhs_t)
        return _cols_to_slabs(_matmul_tn(name, lhs, rhs, lhs_t=lhs_t))

    gw_pg = _matmul_tn("dw_ple_gate", h3, dpre)
    gw_down = _matmul_tn("dw_down", a_up, dx2, relu2=True)
    gw_out = _matmul_tn("dw_out", merged, dx1)
    part_up = column_shards("dw_up", h2t, da_up, lhs_t=True)
    part_wide = jnp.concatenate([gw_out.reshape(N_DEV, dn, d), gw_down.reshape(N_DEV, fn, d),
                                 gw_pg.reshape(N_DEV, dn, d)], axis=1)
    part_narrow = _pad_rows(jnp.concatenate(
        [column_shards("dw_branch_fox", o_fox, dbr_f), column_shards("dw_branch_sb", o_sb, dbr_s),
         column_shards("dw_ple", p2d, dpe)], axis=1), narrow_rows_pad)
    early = [part_up, part_wide, part_narrow]
    early_sums = [_pair_add("pair_add_%d" % i, pt, rc, my_c)
                  for i, (pt, rc) in enumerate(zip(early, _rs_core_pair("reduce_scatter_core_pair_early", early)))]

    dqk_f, dv_f, dc_queries, dc_keys = _fox_bwd(qf, kf, kft, vf, o_fox, do_fox, lse, batch, seq)
    dqkv_b, (s_up, s_wide, s_narrow) = _sb_bwd(qkvb, kbt, do_sb, ltot, batch, seq, early_sums)
    dcq_tok = dc_queries.reshape(batch, N_HEADS, seq).transpose(0, 2, 1).reshape(t_len, N_HEADS)
    dck_tok = dc_keys[..., :NH].transpose(0, 2, 1, 3).reshape(t_len, N_HEADS)
    lane_pad = ((0, 0), (0, LANES - N_HEADS))
    df, db_forget = _forget_bwd(jnp.pad(dcq_tok, lane_pad), jnp.pad(dck_tok, lane_pad), fpre, batch, seq)
    grad_x, h1t, dg_mix = _inproj_bwd(dqk_f, dv_f, dqkv_b, dgl, df, dx1, x2d, g_mix, w_pad)

    gw_in = _unpad_dwin(*[_matmul_tn("dw_in_" + tag, h1t, piece, lhs_t=True) for tag, piece in (
        ("fox_qk", dqk_f), ("fox_v", dv_f), ("sb", dqkv_b), ("gates", dgl), ("forget", df))], d)
    part_in = _cols_to_slabs(gw_in)
    recv_in, = _rs_core_pair("reduce_scatter_core_pair_w_in", [part_in])
    s_in, = _rs_chips([_pair_add("pair_add_w_in", part_in, recv_in, my_c)])

    small = jnp.concatenate([
        dg_mix, dg_mlp, dg_ple, dg_final, jnp.pad(db_forget[:, :N_HEADS], ((0, 0), (0, d - N_HEADS))), dbg,
        jnp.pad(loss_acc[:, :1], ((0, 0), (0, d - 1)))], axis=0)
    small = _all_reduce_small(small)
    loss = small[7, 0]
    small_grads = {
        "g_mix": small[0:1], "g_mlp": small[1:2], "g_ple": small[2:3], "g_final": small[3:4],
        "b_forget": small[4:5, :N_HEADS],
        "b_gate": lax.dynamic_slice_in_dim(small[5:7], my_dev * dn, dn, axis=1),
    }

    weights = {"g_mix": g_mix, "w_in": w_in, "b_forget": b_forget, "b_gate": b_gate, "w_branch_fox": w_branch_fox,
               "w_branch_sb": w_branch_sb, "w_out": w_out, "g_mlp": g_mlp, "w_up": w_up, "w_down": w_down,
               "g_ple": g_ple, "w_ple_gate": w_ple_gate, "w_ple": w_ple, "g_final": g_final}
    m_in = {"g_mix": m_g_mix, "w_in": m_w_in, "b_forget": m_b_forget, "b_gate": m_b_gate,
            "w_branch_fox": m_w_branch_fox, "w_branch_sb": m_w_branch_sb, "w_out": m_w_out, "g_mlp": m_g_mlp,
            "w_up": m_w_up, "w_down": m_w_down, "g_ple": m_g_ple, "w_ple_gate": m_w_ple_gate, "w_ple": m_w_ple,
            "g_final": m_g_final}
    v_in = {"g_mix": v_g_mix, "w_in": v_w_in, "b_forget": v_b_forget, "b_gate": v_b_gate,
            "w_branch_fox": v_w_branch_fox, "w_branch_sb": v_w_branch_sb, "w_out": v_w_out, "g_mlp": v_g_mlp,
            "w_up": v_w_up, "w_down": v_w_down, "g_ple": v_g_ple, "w_ple_gate": v_w_ple_gate, "w_ple": v_w_ple,
            "g_final": v_g_final}
    names = list(weights)

    def as2d(a):
        return a.reshape(-1, a.shape[-1])

    result = {}
    big = {"w_in": (s_in, 0), "w_up": (s_up, 0), "w_out": (s_wide, 0), "w_down": (s_wide, dn),
           "w_ple_gate": (s_wide, dn + fn), "w_branch_fox": (s_narrow, 0), "w_branch_sb": (s_narrow, D_BRANCH),
           "w_ple": (s_narrow, 2 * D_BRANCH)}
    for n, (parts, off) in big.items():
        result[n] = _adamw_parts("adamw_" + n, as2d(weights[n]), parts, off, as2d(m_in[n]), as2d(v_in[n]))
    small_names = list(small_grads)
    small_out = _adamw_small([(as2d(weights[n]), small_grads[n], as2d(m_in[n]), as2d(v_in[n])) for n in small_names])
    for n, (dlt, nm, nv) in zip(small_names, small_out):
        result[n] = (small_grads[n], dlt, nm, nv)
    outs = [[result[n][k].reshape(weights[n].shape) for n in names] for k in range(4)]
    return (loss, grad_x.reshape(x.shape), *outs[0], *outs[1], *outs[2], *outs[3])
```

```python
import jax
import jax.numpy as jnp
from jax import lax
from jax.experimental import pallas as pl
from jax.experimental.pallas import tpu as pltpu

F32 = jnp.float32
BF16 = jnp.bfloat16

HEAD_DIM = 64
N_HEADS = 8
D_BRANCH = N_HEADS * HEAD_DIM
EPS = 1e-6
ADAM_LR = 0.001
ADAM_B1 = 0.9
ADAM_B2 = 0.999
ADAM_EPS = 1e-08
ADAM_WD = 0.01
ADAM_STEP = 10

N_DEV = 8
LANES = 128
TM = 256
TQ = 256
TK = 256
NH = 4
HEAD_SLOT = 128
C_TERMS_Q = 64
C_ONES_K = 64
C_TERMS_K = 67
C_ONES_Q = 67
NEG = -1e30
VMEM_LIMIT = 56 * 1024 * 1024
MESH = pl.DeviceIdType.MESH


def _dot(a, b):
    return jnp.dot(a, b, preferred_element_type=F32)


def _dot_nt(a, b):
    return lax.dot_general(a, b, (((1,), (1,)), ((), ())), preferred_element_type=F32)


def _dot_tn(a, b):
    return lax.dot_general(a, b, (((0,), (0,)), ((), ())), preferred_element_type=F32)


def _sigmoid(x):
    return 1.0 / (1.0 + jnp.exp(-x))


def _softplus(x):
    return jnp.maximum(x, 0.0) + jnp.log(1.0 + jnp.exp(-jnp.abs(x)))


def _split2(x):
    hi = x.astype(BF16)
    lo = (x - hi.astype(F32)).astype(BF16)
    return hi, lo


def _split3(x):
    hi = x.astype(BF16)
    r = x - hi.astype(F32)
    mid = r.astype(BF16)
    lo = (r - mid.astype(F32)).astype(BF16)
    return hi, mid, lo


def _rows_dot_mask(x, mask_bf16):
    hi, lo = _split2(x)
    return _dot(hi, mask_bf16) + _dot(lo, mask_bf16)


def _tri(n, rel):
    r = lax.broadcasted_iota(jnp.int32, (n, n), 0)
    c = lax.broadcasted_iota(jnp.int32, (n, n), 1)
    return rel(r, c).astype(BF16)


def _rms(x):
    r = lax.rsqrt(jnp.mean(x * x, axis=-1, keepdims=True) + EPS)
    return x * r, r


def _rms_bwd(dh, xn, r, g):
    dxn = dh * g
    dx = r * (dxn - xn * jnp.mean(dxn * xn, axis=-1, keepdims=True))
    return dx, jnp.sum(dh * xn, axis=0, keepdims=True)


def _row_spec(tm, cols):
    return pl.BlockSpec((tm, cols), lambda i: (i, 0))


def _row3_spec(g, tm, cols):
    return pl.BlockSpec((g, tm, cols), lambda i: (0, i, 0))


def _col_spec(rows, tm):
    return pl.BlockSpec((rows, tm), lambda i: (0, i))


def _const_spec(shape):
    nd = len(shape)
    return pl.BlockSpec(shape, lambda i: (0,) * nd, pipeline_mode=pl.Buffered(1))


def _acc_spec(shape):
    nd = len(shape)
    return pl.BlockSpec(shape, lambda i: (0,) * nd)


def _seq_params():
    return pltpu.CompilerParams(dimension_semantics=("arbitrary",), vmem_limit_bytes=VMEM_LIMIT)


def _mesh_pos():
    return lax.axis_index("x"), lax.axis_index("y"), lax.axis_index("c")


def _other_chips(x, y):
    return [(1 - x, y), (x, 1 - y), (1 - x, 1 - y)]


def _hbm_specs(n):
    return [pl.BlockSpec(memory_space=pl.ANY)] * n


def _gather_plan(x_refs, out_refs, send_sems, recv_sems, local_sems):
    n = len(x_refs)
    x, y, c = _mesh_pos()
    me, sibling = (x, y, c), (x, y, 1 - c)
    chips = _other_chips(x, y)

    def index(px, py, pc):
        return 4 * px + 2 * py + pc

    def copy(a, k, block, to, src=None):
        slab = out_refs[a].at[index(*block)]
        return pltpu.make_async_remote_copy(
            src_ref=slab if src is None else src, dst_ref=slab,
            send_sem=send_sems.at[7 * a + k], recv_sem=recv_sems.at[7 * a + k], device_id=to, device_id_type=MESH)

    mine = [pltpu.make_async_copy(x_refs[a], out_refs[a].at[index(*me)], local_sems.at[a]) for a in range(n)]
    first = []
    for a in range(n):
        first.append(copy(a, 0, me, sibling, src=x_refs[a]))
        first += [copy(a, 1 + j, me, (cx, cy, c), src=x_refs[a]) for j, (cx, cy) in enumerate(chips)]

    def start():
        for cp in mine + first:
            cp.start()

    def finish():
        passed = []
        for j, (cx, cy) in enumerate(chips):
            for a in range(n):
                copy(a, 1 + j, (cx, cy, c), me).wait_recv()
                passed.append(copy(a, 4 + j, (cx, cy, c), sibling))
                passed[-1].start()
        for a in range(n):
            copy(a, 0, sibling, me).wait_recv()
            for j, (cx, cy) in enumerate(chips):
                copy(a, 4 + j, (cx, cy, 1 - c), me).wait_recv()
        for cp in first + passed:
            cp.wait_send()
        for cp in mine:
            cp.wait()

    return start, finish


def _gather_shapes(shards):
    return [jax.ShapeDtypeStruct((N_DEV,) + s.shape, s.dtype) for s in shards]


def _gather_sems(n):
    return [pltpu.SemaphoreType.DMA((7 * n,)), pltpu.SemaphoreType.DMA((7 * n,)), pltpu.SemaphoreType.DMA((n,))]


def _all_gather(shards):
    n = len(shards)

    def body(*refs):
        start, finish = _gather_plan(refs[:n], refs[n:2 * n], *refs[2 * n:])
        start()
        finish()

    return pl.pallas_call(
        body, name="all_gather_weights", out_shape=_gather_shapes(shards),
        in_specs=_hbm_specs(n), out_specs=_hbm_specs(n), scratch_shapes=_gather_sems(n),
    )(*shards)


def _rs_core_pair(name, partials):
    n = len(partials)

    def body(*refs):
        p_refs, recv_refs = refs[:n], refs[n:2 * n]
        send_sems, recv_sems = refs[2 * n:]
        x, y, c = _mesh_pos()
        for a in range(n):
            for chip in range(4):
                pltpu.make_async_remote_copy(
                    src_ref=p_refs[a].at[2 * chip + (1 - c)], dst_ref=recv_refs[a].at[chip],
                    send_sem=send_sems.at[a], recv_sem=recv_sems.at[a],
                    device_id=(x, y, 1 - c), device_id_type=MESH).start()
        for a in range(n):
            pltpu.make_async_remote_copy(
                src_ref=recv_refs[a], dst_ref=recv_refs[a], send_sem=send_sems.at[a], recv_sem=recv_sems.at[a],
                device_id=(x, y, 1 - c), device_id_type=MESH).wait()

    return pl.pallas_call(
        body, name=name,
        out_shape=[jax.ShapeDtypeStruct((4,) + s.shape[1:], s.dtype) for s in partials],
        in_specs=_hbm_specs(n), out_specs=_hbm_specs(n),
        scratch_shapes=[pltpu.SemaphoreType.DMA((n,)), pltpu.SemaphoreType.DMA((n,))],
    )(*partials)


def _chips_plan(cs_refs, out_refs, send_sems, recv_sems, local_sems):
    n = len(cs_refs)
    x, y, c = _mesh_pos()
    chip = 2 * x + y
    chips = _other_chips(x, y)
    mine = [pltpu.make_async_copy(cs_refs[a].at[chip], out_refs[a].at[chip], local_sems.at[a]) for a in range(n)]
    sends = [pltpu.make_async_remote_copy(
        src_ref=cs_refs[a].at[2 * cx + cy], dst_ref=out_refs[a].at[chip],
        send_sem=send_sems.at[3 * a + j], recv_sem=recv_sems.at[3 * a + j],
        device_id=(cx, cy, c), device_id_type=MESH) for a in range(n) for j, (cx, cy) in enumerate(chips)]

    def start():
        for cp in mine + sends:
            cp.start()

    def finish():
        for a in range(n):
            for j, (cx, cy) in enumerate(chips):
                pltpu.make_async_remote_copy(
                    src_ref=cs_refs[a].at[chip], dst_ref=out_refs[a].at[2 * cx + cy],
                    send_sem=send_sems.at[3 * a + j], recv_sem=recv_sems.at[3 * a + j],
                    device_id=(x, y, c), device_id_type=MESH).wait_recv()
        for cp in sends:
            cp.wait_send()
        for cp in mine:
            cp.wait()

    return start, finish


def _chips_sems(n):
    return [pltpu.SemaphoreType.DMA((3 * n,)), pltpu.SemaphoreType.DMA((3 * n,)), pltpu.SemaphoreType.DMA((n,))]


def _rs_chips(chip_sums):
    n = len(chip_sums)

    def body(*refs):
        start, finish = _chips_plan(refs[:n], refs[n:2 * n], *refs[2 * n:])
        start()
        finish()

    return pl.pallas_call(
        body, name="reduce_scatter_chips",
        out_shape=[jax.ShapeDtypeStruct(s.shape, s.dtype) for s in chip_sums],
        in_specs=_hbm_specs(n), out_specs=_hbm_specs(n), scratch_shapes=_chips_sems(n),
    )(*chip_sums)


def _all_reduce_small(vec):
    rows, cols = vec.shape

    def body(x_ref, land_ref, sum_ref, send_sems, recv_sems):
        x, y, c = _mesh_pos()
        me = 4 * x + 2 * y + c
        land_ref[me] = x_ref[...]
        flips = [(fx, fy, fc) for fx in (0, 1) for fy in (0, 1) for fc in (0, 1)][1:]

        def flipped(f):
            return tuple((1 - v) if b else v for v, b in zip((x, y, c), f))

        sends = []
        for k, f in enumerate(flips):
            sends.append(pltpu.make_async_remote_copy(
                src_ref=x_ref, dst_ref=land_ref.at[me], send_sem=send_sems.at[k], recv_sem=recv_sems.at[k],
                device_id=flipped(f), device_id_type=MESH))
            sends[-1].start()
        for k, f in enumerate(flips):
            px, py, pc = flipped(f)
            pltpu.make_async_remote_copy(
                src_ref=x_ref, dst_ref=land_ref.at[4 * px + 2 * py + pc], send_sem=send_sems.at[k],
                recv_sem=recv_sems.at[k], device_id=(x, y, c), device_id_type=MESH).wait_recv()
        for cp in sends:
            cp.wait_send()
        total = land_ref[0]
        for d in range(1, N_DEV):
            total = total + land_ref[d]
        sum_ref[...] = total

    vm = pl.BlockSpec(memory_space=pltpu.VMEM)
    return pl.pallas_call(
        body, name="all_reduce_small",
        out_shape=(jax.ShapeDtypeStruct((N_DEV, rows, cols), F32), jax.ShapeDtypeStruct((rows, cols), F32)),
        in_specs=[vm], out_specs=(vm, vm),
        scratch_shapes=[pltpu.SemaphoreType.DMA((7,)), pltpu.SemaphoreType.DMA((7,))],
    )(vec)[1]


def _block_rows(rows, cols, itemsize, align, row_off=0):
    best = None
    for t in range(align, rows + 1, align):
        if rows % t == 0 and row_off % t == 0 and t * cols * itemsize <= (1 << 20):
            best = t
    return rows if best is None else best


def _pair_add(name, partial, recv, my_c):
    _, rows, cols = partial.shape
    br = _block_rows(rows, cols, 2, 16)

    def body(c_ref, a_ref, b_ref, o_ref):
        o_ref[...] = (a_ref[...].astype(F32) + b_ref[...].astype(F32)).astype(BF16)

    return pl.pallas_call(
        body, name=name,
        grid_spec=pltpu.PrefetchScalarGridSpec(
            num_scalar_prefetch=1, grid=(4, rows // br),
            in_specs=[pl.BlockSpec((None, None, br, cols), lambda j, i, c_ref: (j, c_ref[0], i, 0)),
                      pl.BlockSpec((None, br, cols), lambda j, i, c_ref: (j, i, 0))],
            out_specs=pl.BlockSpec((None, br, cols), lambda j, i, c_ref: (j, i, 0))),
        out_shape=jax.ShapeDtypeStruct((4, rows, cols), BF16),
    )(my_c.reshape(1).astype(jnp.int32), partial.reshape(4, 2, rows, cols), recv)


def _adam_update(w, g, m, v):
    nm = ADAM_B1 * m + (1.0 - ADAM_B1) * g
    nv = ADAM_B2 * v + (1.0 - ADAM_B2) * (g * g)
    m_hat = nm / (1.0 - ADAM_B1 ** ADAM_STEP)
    v_hat = nv / (1.0 - ADAM_B2 ** ADAM_STEP)
    return -ADAM_LR * (m_hat / (jnp.sqrt(v_hat) + ADAM_EPS) + ADAM_WD * w), nm, nv


def _adamw_parts(name, w, parts, row_off, m, v):
    rows, cols = w.shape
    tr = _block_rows(rows, cols, 4, 16, row_off)
    tc = cols
    if tr == rows and rows % 16 != 0 and cols % (2 * LANES) == 0:
        tc = 2 * LANES
    assert rows % tr == 0 and row_off % tr == 0 and (tc == cols or row_off == 0)
    off = row_off // tr

    def body(w_ref, p_ref, m_ref, v_ref, g_ref, d_ref, nm_ref, nv_ref):
        g = p_ref[0].astype(F32)
        for j in range(1, 4):
            g = g + p_ref[j].astype(F32)
        g_ref[...] = g
        d_ref[...], nm_ref[...], nv_ref[...] = _adam_update(w_ref[...], g, m_ref[...], v_ref[...])

    spec = pl.BlockSpec((tr, tc), lambda i, j: (i, j))
    shp = jax.ShapeDtypeStruct((rows, cols), F32)
    return pl.pallas_call(
        body, name=name, grid=(rows // tr, cols // tc), out_shape=(shp,) * 4,
        in_specs=[spec, pl.BlockSpec((4, tr, tc), lambda i, j: (0, off + i, j)), spec, spec], out_specs=(spec,) * 4,
    )(w, parts, m, v)


def _adamw_small(tensors):
    n = len(tensors)

    def body(*refs):
        ins, outs = refs[:4 * n], refs[4 * n:]
        for t in range(n):
            w_ref, g_ref, m_ref, v_ref = ins[4 * t:4 * t + 4]
            d, nm, nv = _adam_update(w_ref[...], g_ref[...], m_ref[...], v_ref[...])
            outs[3 * t][...], outs[3 * t + 1][...], outs[3 * t + 2][...] = d, nm, nv

    vm = pl.BlockSpec(memory_space=pltpu.VMEM)
    out = pl.pallas_call(
        body, name="adamw_small",
        out_shape=[jax.ShapeDtypeStruct(t[0].shape, F32) for t in tensors for _ in range(3)],
        in_specs=[vm] * (4 * n), out_specs=[vm] * (3 * n),
    )(*[a for t in tensors for a in t])
    return [tuple(out[3 * t:3 * t + 3]) for t in range(n)]


def _matmul_tn(name, a, b, relu2=False, slabs=False, lhs_t=False):
    a_groups = a.shape[0] if a.ndim == 3 else 0
    b_groups = b.shape[0] if b.ndim == 3 else 0
    groups = max(a_groups, b_groups, 1)
    assert not (a_groups and b_groups) and not (a_groups and lhs_t)
    a3 = a if a_groups else a[None]
    b3 = b if b_groups else b[None]
    t_len, k_len = a3.shape[1:][::-1] if lhs_t else a3.shape[1:]
    n_len = b3.shape[2]
    tt = min(t_len, 512)
    tk = min(k_len, 1024)
    tn = n_len // N_DEV if slabs else min(n_len, 1024)
    nt = t_len // tt
    assert not slabs or (groups == 1 and tn <= 1024)

    def body(a_ref, b_ref, o_ref, acc_ref):
        @pl.when(pl.program_id(3) == 0)
        def _():
            acc_ref[...] = jnp.zeros_like(acc_ref)

        av = a_ref[...]
        if relu2:
            av = jnp.square(jnp.maximum(av.astype(F32), 0.0))
        product = _dot if lhs_t else _dot_tn
        acc_ref[...] += product(av.astype(BF16), b_ref[...].astype(BF16))

        @pl.when(pl.program_id(3) == nt - 1)
        def _():
            o_ref[...] = acc_ref[...].astype(BF16)

    def a_group(g):
        return g if a_groups else 0

    def b_group(g):
        return g if b_groups else 0

    if slabs:
        out_shape = jax.ShapeDtypeStruct((N_DEV, k_len, tn), BF16)
        out_spec = pl.BlockSpec((None, tk, tn), lambda g, i, j, t: (j, i, 0))
    else:
        out_shape = jax.ShapeDtypeStruct((groups, k_len, n_len), BF16)
        out_spec = pl.BlockSpec((None, tk, tn), lambda g, i, j, t: (g, i, j))
    out = pl.pallas_call(
        body, name=name, grid=(groups, k_len // tk, n_len // tn, nt), out_shape=out_shape,
        in_specs=[pl.BlockSpec((None, tk, tt), lambda g, i, j, t: (a_group(g), i, t)) if lhs_t
                  else pl.BlockSpec((None, tt, tk), lambda g, i, j, t: (a_group(g), t, i)),
                  pl.BlockSpec((None, tt, tn), lambda g, i, j, t: (b_group(g), t, j))],
        out_specs=out_spec,
        scratch_shapes=[pltpu.VMEM((tk, tn), F32)],
        compiler_params=pltpu.CompilerParams(
            dimension_semantics=("parallel", "parallel", "parallel", "arbitrary"), vmem_limit_bytes=VMEM_LIMIT),
    )(a3, b3)
    return out if (slabs or a_groups or b_groups) else out[0]


def _pad_layout(d):
    names = ("qf", "kf", "vf", "qb", "kb", "vb", "gates", "forget")
    sizes = (N_HEADS * HEAD_SLOT, N_HEADS * HEAD_SLOT, D_BRANCH, D_BRANCH, D_BRANCH, D_BRANCH, 2 * d, LANES)
    out, off = {}, 0
    for n, s in zip(names, sizes):
        out[n] = (off, off + s)
        off += s
    return out, off


def _slot_rows(xt, extra):
    parts = []
    for h in range(N_HEADS):
        parts += [xt[h * HEAD_DIM:(h + 1) * HEAD_DIM, :], extra]
    return jnp.concatenate(parts, axis=0)


def _inproj_fwd(x, g_mix, w_pad, bf_pad, place_q, place_k, ones_q, ones_k, seq):
    t_len, d = x.shape
    lay, _ = _pad_layout(d)
    tiles_per_seq = seq // TM
    slot_w = N_HEADS * HEAD_SLOT

    def body(x_ref, g_ref, w_ref, bf_ref, pq_ref, pk_ref, oq_ref, ok_ref,
             qf_ref, kf_ref, kft_ref, vf_ref, vft_ref, qkvb_ref, kbt_ref, vbt_ref, gl_ref, fpre_ref, carry_ref):
        @pl.when(pl.program_id(0) % tiles_per_seq == 0)
        def _():
            carry_ref[...] = jnp.zeros_like(carry_ref)

        def proj(name):
            lo, hi = lay[name]
            return _dot_nt(h, w_ref[lo:hi, :])

        xn, _ = _rms(x_ref[...])
        h = (xn * g_ref[...]).astype(BF16)
        fpre = proj("forget") + bf_ref[...]
        fpre_ref[...] = fpre
        logf = -_softplus(-fpre)
        lower = _tri(TM, lambda r, c: c <= r)
        hi, mid, lo = _split3(logf)
        c_val = carry_ref[...] + _dot(lower, hi) + _dot(lower, mid) + _dot(lower, lo)
        carry_ref[...] = carry_ref[...] + jnp.sum(logf, axis=0, keepdims=True)
        c3 = _split3(c_val)
        qf_ref[...] = (proj("qf") + sum(_dot(c3[j], pq_ref[j]) for j in range(3)) + oq_ref[...]).astype(BF16)
        kf = proj("kf") - sum(_dot(c3[j], pk_ref[j]) for j in range(3)) + ok_ref[...]
        kf_ref[...] = kf.astype(BF16)
        kft_ref[0] = kf.T.astype(BF16)
        row0 = (lax.broadcasted_iota(jnp.int32, (HEAD_DIM, TM), 0) == 0).astype(F32)
        zeros = jnp.zeros((HEAD_DIM, TM), F32)
        vf = proj("vf")
        vf_ref[...] = vf.astype(BF16)
        vft_ref[0] = _slot_rows(vf.T, row0).astype(BF16)
        qkvb_ref[0] = proj("qb").astype(BF16)
        kb = proj("kb")
        qkvb_ref[1] = kb.astype(BF16)
        kbt_ref[0] = _slot_rows(kb.T, zeros).astype(BF16)
        vb = proj("vb")
        qkvb_ref[2] = vb.astype(BF16)
        vbt_ref[0] = _slot_rows(vb.T, row0).astype(BF16)
        gl_ref[...] = proj("gates").astype(BF16)

    n_tiles = t_len // TM
    slot_shape = jax.ShapeDtypeStruct((t_len, slot_w), BF16)
    t_shape = jax.ShapeDtypeStruct((n_tiles, slot_w, TM), BF16)
    t_spec = pl.BlockSpec((1, slot_w, TM), lambda i: (i, 0, 0))
    return pl.pallas_call(
        body, name="inproj_fwd", grid=(n_tiles,),
        out_shape=(slot_shape, slot_shape, t_shape, jax.ShapeDtypeStruct((t_len, D_BRANCH), BF16), t_shape,
                   jax.ShapeDtypeStruct((3, t_len, D_BRANCH), BF16), t_shape, t_shape,
                   jax.ShapeDtypeStruct((t_len, 2 * d), BF16), jax.ShapeDtypeStruct((t_len, LANES), F32)),
        in_specs=[_row_spec(TM, d), _const_spec((1, d)), _const_spec(w_pad.shape), _const_spec((1, LANES)),
                  _const_spec(place_q.shape), _const_spec(place_k.shape), _const_spec((1, slot_w)),
                  _const_spec((1, slot_w))],
        out_specs=(_row_spec(TM, slot_w), _row_spec(TM, slot_w), t_spec, _row_spec(TM, D_BRANCH), t_spec,
                   _row3_spec(3, TM, D_BRANCH), t_spec, t_spec, _row_spec(TM, 2 * d), _row_spec(TM, LANES)),
        scratch_shapes=[pltpu.VMEM((1, LANES), F32)],
        compiler_params=_seq_params(),
    )(x, g_mix, w_pad, bf_pad, place_q, place_k, ones_q, ones_k)


def _slot_spec(seq):
    return pl.BlockSpec((seq, NH * HEAD_SLOT), lambda b, g: (b, g))


def _slot2_spec(seq):
    return pl.BlockSpec((2, seq, NH * HEAD_SLOT), lambda b, g: (0, b, g))


def _group_spec(seq):
    return pl.BlockSpec((seq, NH * HEAD_DIM), lambda b, g: (b, g))


def _group3_spec(which, seq):
    return pl.BlockSpec((None, seq, NH * HEAD_DIM), lambda b, g: (which, b, g))


def _tblock_spec(seq):
    return pl.BlockSpec((seq // TK, NH * HEAD_SLOT, TK), lambda b, g: (b, g, 0))


def _qrow_spec(seq):
    return pl.BlockSpec((None, NH, seq // TQ, TQ), lambda b, g: (b, g, 0, 0))


def _stat_spec(seq):
    return pl.BlockSpec((None, None, seq, LANES), lambda b, g: (b, g, 0, 0))


def _attn_params():
    return pltpu.CompilerParams(dimension_semantics=("parallel", "parallel"), vmem_limit_bytes=VMEM_LIMIT)


def _serial_attn_params():
    return pltpu.CompilerParams(dimension_semantics=("arbitrary", "arbitrary"), vmem_limit_bytes=VMEM_LIMIT)


def _hcols(hh):
    return slice(hh * HEAD_DIM, (hh + 1) * HEAD_DIM)


def _hslot(hh):
    return slice(hh * HEAD_SLOT, (hh + 1) * HEAD_SLOT)


def _lane(hh):
    return slice(hh, hh + 1)


def _key_query_mask(rel):
    r = lax.broadcasted_iota(jnp.int32, (TK, TQ), 0)
    c = lax.broadcasted_iota(jnp.int32, (TK, TQ), 1)
    return rel(r, c)


def _heads_cat(vals):
    return jnp.concatenate(vals, axis=1)


def _untranspose(acc_t):
    return acc_t.T[:, :HEAD_DIM]


def _fox_fwd(qf, kf, vft, batch, seq):
    def body(q_ref, k_ref, vt_ref, o_ref, lse_ref, m_s, acc_s):
        causal = _key_query_mask(lambda r, c: r <= c)

        def tile(q0, kj, masked):
            krows = pl.ds(pl.multiple_of(kj * TK, TK), TK)
            heads = range(NH)
            sts = [_dot_nt(k_ref[krows, _hslot(hh)], q_ref[pl.ds(q0, TQ), _hslot(hh)]) for hh in heads]
            if masked:
                sts = [jnp.where(causal, st, NEG) for st in sts]
            m_olds = [m_s[hh] for hh in heads]
            m_news = [jnp.maximum(m_olds[hh], jnp.max(sts[hh], axis=0, keepdims=True)) for hh in heads]
            pts = [jnp.exp(sts[hh] - m_news[hh]).astype(BF16) for hh in heads]
            pvs = [_dot(vt_ref[kj, _hslot(hh), :], pts[hh]) for hh in heads]
            for hh in heads:
                acc_s[hh] = jnp.exp(m_olds[hh] - m_news[hh]) * acc_s[hh] + pvs[hh]
                m_s[hh] = m_news[hh]

        def q_loop(qi, _):
            q0 = pl.multiple_of(qi * TQ, TQ)
            m_s[...] = jnp.full(m_s.shape, NEG, F32)
            acc_s[...] = jnp.zeros_like(acc_s)

            def k_loop(kj, _):
                tile(q0, kj, False)
                return 0

            lax.fori_loop(0, qi, k_loop, 0)
            tile(q0, qi, True)
            outs = []
            for hh in range(NH):
                total = acc_s[hh, HEAD_DIM:HEAD_DIM + 1, :]
                outs.append(_untranspose(acc_s[hh] / total))
                lse_ref[hh, pl.ds(qi, 1), :] = m_s[hh] + jnp.log(total)
            o_ref[pl.ds(q0, TQ), :] = _heads_cat(outs).astype(BF16)
            return 0

        lax.fori_loop(0, seq // TQ, q_loop, 0)

    return pl.pallas_call(
        body, name="fox_fwd", grid=(batch, N_HEADS // NH),
        out_shape=(jax.ShapeDtypeStruct((batch * seq, D_BRANCH), BF16),
                   jax.ShapeDtypeStruct((batch, N_HEADS, seq // TQ, TQ), F32)),
        in_specs=[_slot_spec(seq), _slot_spec(seq), _tblock_spec(seq)],
        out_specs=(_group_spec(seq), _qrow_spec(seq)),
        scratch_shapes=[pltpu.VMEM((NH, 1, TQ), F32), pltpu.VMEM((NH, HEAD_SLOT, TQ), F32)],
        compiler_params=_attn_params(),
    )(qf, kf, vft)


def _fox_bwd(qf, kf, kft, vf, o, do, lse, batch, seq):
    n_q = seq // TQ

    def body(q_ref, k_ref, kt_ref, v_ref, o_ref, do_ref, lse_ref, dqk_ref, dv_ref, dcq_ref, dck_ref,
             delta_s, dqt_acc, dk_s, dv_s):
        causal = _key_query_mask(lambda r, c: r <= c)
        ones8 = jnp.ones((8, HEAD_DIM), BF16)
        dqt_acc[...] = jnp.zeros_like(dqt_acc)

        def prep(qi, _):
            rows = pl.ds(pl.multiple_of(qi * TQ, TQ), TQ)
            for hh in range(NH):
                hi, lo = _split2(do_ref[rows, _hcols(hh)].astype(F32) * o_ref[rows, _hcols(hh)].astype(F32))
                delta_s[hh, pl.ds(qi, 1), :] = (_dot_nt(ones8, hi) + _dot_nt(ones8, lo))[0:1, :]
            return 0

        lax.fori_loop(0, n_q, prep, 0)

        def tile(qi, kj, masked):
            rows = pl.ds(pl.multiple_of(qi * TQ, TQ), TQ)
            krows = pl.ds(pl.multiple_of(kj * TK, TK), TK)
            heads = range(NH)
            qs = [q_ref[rows, _hslot(hh)] for hh in heads]
            douts = [do_ref[rows, _hcols(hh)] for hh in heads]
            sts = [_dot_nt(k_ref[krows, _hslot(hh)], qs[hh]) for hh in heads]
            dps = [_dot_nt(v_ref[krows, _hcols(hh)], douts[hh]) for hh in heads]
            pts = [jnp.exp(sts[hh] - lse_ref[hh, pl.ds(qi, 1), :]) for hh in heads]
            if masked:
                pts = [jnp.where(causal, pt, 0.0) for pt in pts]
            dsts = [(pts[hh] * (dps[hh] - delta_s[hh, pl.ds(qi, 1), :])).astype(BF16) for hh in heads]
            for hh in heads:
                dv_s[hh] += _dot(pts[hh].astype(BF16), douts[hh])
                dk_s[hh] += _dot(dsts[hh], qs[hh])
                dqt_acc[hh, qi] += _dot(kt_ref[kj, _hslot(hh), :], dsts[hh])

        def k_loop(kj, _):
            krows = pl.ds(pl.multiple_of(kj * TK, TK), TK)
            dk_s[...] = jnp.zeros_like(dk_s)
            dv_s[...] = jnp.zeros_like(dv_s)
            tile(kj, kj, True)

            def q_loop(qi, _):
                tile(qi, kj, False)
                return 0

            lax.fori_loop(kj + 1, n_q, q_loop, 0)
            dqk_ref[1, krows, :] = _heads_cat([dk_s[hh] for hh in range(NH)]).astype(BF16)
            dv_ref[krows, :] = _heads_cat([dv_s[hh] for hh in range(NH)]).astype(BF16)
            for hh in range(NH):
                dck_ref[krows, _lane(hh)] = dk_s[hh, :, C_ONES_Q:C_ONES_Q + 1]
            return 0

        lax.fori_loop(0, seq // TK, k_loop, 0)

        def finish(qi, _):
            rows = pl.ds(pl.multiple_of(qi * TQ, TQ), TQ)
            dqk_ref[0, rows, :] = _heads_cat([dqt_acc[hh, qi].T for hh in range(NH)]).astype(BF16)
            for hh in range(NH):
                dcq_ref[hh, pl.ds(qi, 1), :] = dqt_acc[hh, qi, C_ONES_K:C_ONES_K + 1, :]
            return 0

        lax.fori_loop(0, n_q, finish, 0)

    return pl.pallas_call(
        body, name="fox_bwd", grid=(batch, N_HEADS // NH),
        out_shape=(jax.ShapeDtypeStruct((2, batch * seq, N_HEADS * HEAD_SLOT), BF16),
                   jax.ShapeDtypeStruct((batch * seq, D_BRANCH), BF16),
                   jax.ShapeDtypeStruct((batch, N_HEADS, seq // TQ, TQ), F32),
                   jax.ShapeDtypeStruct((batch, N_HEADS // NH, seq, LANES), F32)),
        in_specs=[_slot_spec(seq), _slot_spec(seq), _tblock_spec(seq), _group_spec(seq), _group_spec(seq),
                  _group_spec(seq), _qrow_spec(seq)],
        out_specs=(_slot2_spec(seq), _group_spec(seq), _qrow_spec(seq), _stat_spec(seq)),
        scratch_shapes=[pltpu.VMEM((NH, n_q, TQ), F32), pltpu.VMEM((NH, n_q, HEAD_SLOT, TQ), F32),
                        pltpu.VMEM((NH, TK, HEAD_SLOT), F32), pltpu.VMEM((NH, TK, HEAD_DIM), F32)],
        compiler_params=_attn_params(),
    )(qf, kf, kft, vf, o, do, lse)


def _first_last_step():
    step = pl.program_id(0) * pl.num_programs(1) + pl.program_id(1)
    return step == 0, step == pl.num_programs(0) * pl.num_programs(1) - 1


def _sb_fwd(qkvb, vbt, batch, seq, shards):
    n = len(shards)

    def body(q_ref, k_ref, vt_ref, *rest):
        x_refs, (o_ref, lt_ref), out_refs = rest[:n], rest[n:n + 2], rest[n + 2:2 * n + 2]
        run_s, acc_s = rest[2 * n + 2:2 * n + 4]
        gather_start, gather_finish = _gather_plan(x_refs, out_refs, *rest[2 * n + 4:])
        first_step, last_step = _first_last_step()
        pl.when(first_step)(gather_start)
        strict = _key_query_mask(lambda r, c: r < c)
        later = _tri(TK, lambda r, c: c > r)

        def tile(q0, kj, masked):
            krows = pl.ds(pl.multiple_of(kj * TK, TK), TK)
            heads = range(NH)
            zts = [_dot_nt(k_ref[krows, _hcols(hh)], q_ref[pl.ds(q0, TQ), _hcols(hh)]) for hh in heads]
            lgs = [-_softplus(zt) for zt in zts]
            if masked:
                lgs = [jnp.where(strict, lg, 0.0) for lg in lgs]
            parts = [_split2(lg) for lg in lgs]
            sufs = [_dot(later, hi) + _dot(later, lo) for hi, lo in parts]
            ats = [jnp.exp(zts[hh] + lgs[hh] + run_s[hh] + sufs[hh]) for hh in heads]
            if masked:
                ats = [jnp.where(strict, at, 0.0) for at in ats]
            for hh in heads:
                acc_s[hh] += _dot(vt_ref[kj, _hslot(hh), :], ats[hh].astype(BF16))
                run_s[hh] += jnp.sum(lgs[hh], axis=0, keepdims=True)

        def q_loop(qi, _):
            q0 = pl.multiple_of(qi * TQ, TQ)
            run_s[...] = jnp.zeros_like(run_s)
            acc_s[...] = jnp.zeros_like(acc_s)
            tile(q0, qi, True)

            def k_loop(kk, _):
                tile(q0, qi - 1 - kk, False)
                return 0

            lax.fori_loop(0, qi, k_loop, 0)
            o_ref[pl.ds(q0, TQ), :] = _heads_cat([_untranspose(acc_s[hh]) for hh in range(NH)]).astype(BF16)
            for hh in range(NH):
                lt_ref[hh, pl.ds(qi, 1), :] = run_s[hh]
            return 0

        lax.fori_loop(0, seq // TQ, q_loop, 0)
        pl.when(last_step)(gather_finish)

    out = pl.pallas_call(
        body, name="sb_fwd", grid=(batch, N_HEADS // NH),
        out_shape=[jax.ShapeDtypeStruct((batch * seq, D_BRANCH), BF16),
                   jax.ShapeDtypeStruct((batch, N_HEADS, seq // TQ, TQ), F32)] + _gather_shapes(shards),
        in_specs=[_group3_spec(0, seq), _group3_spec(1, seq), _tblock_spec(seq)] + _hbm_specs(n),
        out_specs=[_group_spec(seq), _qrow_spec(seq)] + _hbm_specs(n),
        scratch_shapes=[pltpu.VMEM((NH, 1, TQ), F32), pltpu.VMEM((NH, HEAD_SLOT, TQ), F32)] + _gather_sems(n),
        compiler_params=_serial_attn_params(),
    )(qkvb, qkvb, vbt, *shards)
    return out[0], out[1], out[2:]


def _sb_bwd(qkvb, kbt, do, ltot, batch, seq, chip_sums):
    n = len(chip_sums)

    def body(q_ref, k_ref, v_ref, kt_ref, do_ref, lt_ref, *rest):
        cs_refs, dqkv_ref, out_refs = rest[:n], rest[n], rest[n + 1:2 * n + 1]
        dk_acc, dv_acc, ls_s, gs_s, dqt_s = rest[2 * n + 1:2 * n + 6]
        chips_start, chips_finish = _chips_plan(cs_refs, out_refs, *rest[2 * n + 6:])
        first_step, last_step = _first_last_step()
        pl.when(first_step)(chips_start)
        strict = _key_query_mask(lambda r, c: r < c)
        upto = _tri(TK, lambda r, c: c <= r)
        before = _tri(TK, lambda r, c: c < r)
        dk_acc[...] = jnp.zeros_like(dk_acc)
        dv_acc[...] = jnp.zeros_like(dv_acc)

        def tile(qi, kj, masked):
            rows = pl.ds(pl.multiple_of(qi * TQ, TQ), TQ)
            krows = pl.ds(pl.multiple_of(kj * TK, TK), TK)
            heads = range(NH)
            qs = [q_ref[rows, _hcols(hh)] for hh in heads]
            douts = [do_ref[rows, _hcols(hh)] for hh in heads]
            zts = [_dot_nt(k_ref[krows, _hcols(hh)], qs[hh]) for hh in heads]
            das = [_dot_nt(v_ref[krows, _hcols(hh)], douts[hh]) for hh in heads]
            lgs = [-_softplus(zt) for zt in zts]
            if masked:
                lgs = [jnp.where(strict, lg, 0.0) for lg in lgs]
            parts = [_split2(lg) for lg in lgs]
            prefs = [_dot(upto, hi) + _dot(upto, lo) for hi, lo in parts]
            ats = [jnp.exp(zts[hh] + lgs[hh] + (lt_ref[hh, pl.ds(qi, 1), :] - ls_s[hh]) - prefs[hh]) for hh in heads]
            if masked:
                ats = [jnp.where(strict, at, 0.0) for at in ats]
            gts = [das[hh] * ats[hh] for hh in heads]
            us = [gs_s[hh] + _dot(before, gts[hh].astype(BF16)) for hh in heads]
            dzts = [(jnp.exp(lgs[hh]) * (gts[hh] + us[hh]) - us[hh]).astype(BF16) for hh in heads]
            for hh in heads:
                dk_acc[hh, krows, :] += _dot(dzts[hh], qs[hh])
                dv_acc[hh, krows, :] += _dot(ats[hh].astype(BF16), douts[hh])
                dqt_s[hh] += _dot(kt_ref[kj, _hslot(hh), :], dzts[hh])
                ls_s[hh] += jnp.sum(lgs[hh], axis=0, keepdims=True)
                gs_s[hh] += jnp.sum(gts[hh], axis=0, keepdims=True)

        def q_loop(qi, _):
            ls_s[...] = jnp.zeros_like(ls_s)
            gs_s[...] = jnp.zeros_like(gs_s)
            dqt_s[...] = jnp.zeros_like(dqt_s)

            def k_loop(kj, _):
                tile(qi, kj, False)
                return 0

            lax.fori_loop(0, qi, k_loop, 0)
            tile(qi, qi, True)
            dqkv_ref[0, pl.ds(pl.multiple_of(qi * TQ, TQ), TQ), :] = _heads_cat(
                [_untranspose(dqt_s[hh]) for hh in range(NH)]).astype(BF16)
            return 0

        lax.fori_loop(0, seq // TQ, q_loop, 0)
        dqkv_ref[1] = _heads_cat([dk_acc[hh] for hh in range(NH)]).astype(BF16)
        dqkv_ref[2] = _heads_cat([dv_acc[hh] for hh in range(NH)]).astype(BF16)
        pl.when(last_step)(chips_finish)

    out = pl.pallas_call(
        body, name="sb_bwd", grid=(batch, N_HEADS // NH),
        out_shape=[jax.ShapeDtypeStruct((3, batch * seq, D_BRANCH), BF16)]
        + [jax.ShapeDtypeStruct(s.shape, s.dtype) for s in chip_sums],
        in_specs=[_group3_spec(0, seq), _group3_spec(1, seq), _group3_spec(2, seq), _tblock_spec(seq),
                  _group_spec(seq), _qrow_spec(seq)] + _hbm_specs(n),
        out_specs=[pl.BlockSpec((3, seq, NH * HEAD_DIM), lambda b, g: (0, b, g))] + _hbm_specs(n),
        scratch_shapes=[pltpu.VMEM((NH, seq, HEAD_DIM), F32), pltpu.VMEM((NH, seq, HEAD_DIM), F32),
                        pltpu.VMEM((NH, 1, TQ), F32), pltpu.VMEM((NH, 1, TQ), F32),
                        pltpu.VMEM((NH, HEAD_SLOT, TQ), F32)] + _chips_sems(n),
        compiler_params=_serial_attn_params(),
    )(qkvb, qkvb, qkvb, kbt, do, ltot, *chip_sums)
    return out[0], out[1:]


def _forget_bwd(dcq_tok, dck_tok, fpre, batch, seq):
    t_len = batch * seq
    tiles = seq // TM

    def rev(i):
        return ((i // tiles) * tiles + (tiles - 1 - i % tiles), 0)

    def body(dcq_ref, dck_ref, f_ref, df_ref, db_ref, carry_ref):
        i = pl.program_id(0)

        @pl.when(i == 0)
        def _():
            db_ref[...] = jnp.zeros_like(db_ref)

        @pl.when(i % tiles == 0)
        def _():
            carry_ref[...] = jnp.zeros_like(carry_ref)

        dc = dcq_ref[...] - dck_ref[...]
        upper = _tri(TM, lambda r, c: c >= r)
        hi, mid, lo = _split3(dc)
        dlogf = carry_ref[...] + _dot(upper, hi) + _dot(upper, mid) + _dot(upper, lo)
        carry_ref[...] = carry_ref[...] + jnp.sum(dc, axis=0, keepdims=True)
        df = dlogf * _sigmoid(-f_ref[...])
        df_ref[...] = df.astype(BF16)
        db_ref[...] += jnp.sum(df, axis=0, keepdims=True)

    return pl.pallas_call(
        body, name="forget_bwd", grid=(t_len // TM,),
        out_shape=(jax.ShapeDtypeStruct((t_len, LANES), BF16), jax.ShapeDtypeStruct((1, LANES), F32)),
        in_specs=[pl.BlockSpec((TM, LANES), rev)] * 3,
        out_specs=(pl.BlockSpec((TM, LANES), rev), _acc_spec((1, LANES))),
        scratch_shapes=[pltpu.VMEM((1, LANES), F32)],
        compiler_params=_seq_params(),
    )(dcq_tok, dck_tok, fpre)


def _mix_fwd(o_fox, o_sb, gl, x, w_bf, w_bs, w_out, b_gate):
    t_len, d = x.shape

    def body(of_ref, os_ref, gl_ref, x_ref, wbf_ref, wbs_ref, wo_ref, bg_ref, x1_ref):
        br_f = _dot(of_ref[...], wbf_ref[...])
        br_s = _dot(os_ref[...], wbs_ref[...])
        ga = _sigmoid(gl_ref[:, :d].astype(F32) + bg_ref[0:1, :])
        gb = _sigmoid(gl_ref[:, d:].astype(F32) + bg_ref[1:2, :])
        merged = ga * br_f + gb * br_s
        x1_ref[...] = x_ref[...] + _dot(merged.astype(BF16), wo_ref[...])

    return pl.pallas_call(
        body, name="mix_fwd", grid=(t_len // TM,),
        out_shape=jax.ShapeDtypeStruct((t_len, d), F32),
        in_specs=[_row_spec(TM, D_BRANCH), _row_spec(TM, D_BRANCH), _row_spec(TM, 2 * d), _row_spec(TM, d),
                  _const_spec(w_bf.shape), _const_spec(w_bs.shape), _const_spec(w_out.shape), _const_spec(b_gate.shape)],
        out_specs=_row_spec(TM, d),
        compiler_params=_seq_params(),
    )(o_fox, o_sb, gl, x, w_bf, w_bs, w_out, b_gate)


def _ff_chunk(d_ff):
    return min(d_ff, 1024)


def _mlp_fwd(x1, g_mlp, w_up, w_down):
    t_len, d = x1.shape
    d_ff = w_up.shape[1]
    ch = _ff_chunk(d_ff)

    def body(x1_ref, g_ref, wu_ref, wd_ref, a_ref, x2_ref):
        x1v = x1_ref[...]
        xn, _ = _rms(x1v)
        h = (xn * g_ref[...]).astype(BF16)
        acc = x1v
        for j in range(d_ff // ch):
            a = _dot(h, wu_ref[:, j * ch:(j + 1) * ch])
            a_ref[:, j * ch:(j + 1) * ch] = a.astype(BF16)
            acc = acc + _dot(jnp.square(jnp.maximum(a, 0.0)).astype(BF16), wd_ref[j * ch:(j + 1) * ch, :])
        x2_ref[...] = acc

    return pl.pallas_call(
        body, name="mlp_fwd", grid=(t_len // TM,),
        out_shape=(jax.ShapeDtypeStruct((t_len, d_ff), BF16), jax.ShapeDtypeStruct((t_len, d), F32)),
        in_specs=[_row_spec(TM, d), _const_spec((1, d)), _const_spec(w_up.shape), _const_spec(w_down.shape)],
        out_specs=(_row_spec(TM, d_ff), _row_spec(TM, d)),
        compiler_params=_seq_params(),
    )(x1, g_mlp, w_up, w_down)


def _head_fwd_bwd(x2, p, target, g_ple, g_final, w_pg, w_ple):
    t_len, d = x2.shape
    d_ple = p.shape[1]

    def body(x2_ref, p_ref, t_ref, gp_ref, gf_ref, wpg_ref, wple_ref,
             dx2_ref, h3_ref, dpre_ref, dpe_ref, loss_ref, dgp_ref, dgf_ref):
        @pl.when(pl.program_id(0) == 0)
        def _():
            loss_ref[...] = jnp.zeros_like(loss_ref)
            dgp_ref[...] = jnp.zeros_like(dgp_ref)
            dgf_ref[...] = jnp.zeros_like(dgf_ref)

        x2v = x2_ref[...]
        x2n, r3 = _rms(x2v)
        h3 = (x2n * gp_ref[...]).astype(BF16)
        h3_ref[...] = h3
        gate = _sigmoid(_dot(h3, wpg_ref[...]))
        pe = _dot(p_ref[...].astype(BF16), wple_ref[...])
        x3n, r4 = _rms(x2v + gate * pe)
        err = x3n * gf_ref[...] - t_ref[...]
        loss_ref[...] += jnp.full(loss_ref.shape, (0.5 / d) * jnp.sum(err * err), F32)
        dx3, dgf = _rms_bwd(err * (1.0 / d), x3n, r4, gf_ref[...])
        dgf_ref[...] += dgf
        dpe_ref[...] = (dx3 * gate).astype(BF16)
        dpre = (dx3 * pe * gate * (1.0 - gate)).astype(BF16)
        dpre_ref[...] = dpre
        dres, dgp = _rms_bwd(_dot_nt(dpre, wpg_ref[...]), x2n, r3, gp_ref[...])
        dgp_ref[...] += dgp
        dx2_ref[...] = dx3 + dres

    shp_b = jax.ShapeDtypeStruct((t_len, d), BF16)
    return pl.pallas_call(
        body, name="head_fwd_bwd", grid=(t_len // TM,),
        out_shape=(jax.ShapeDtypeStruct((t_len, d), F32), shp_b, shp_b, shp_b,
                   jax.ShapeDtypeStruct((1, LANES), F32), jax.ShapeDtypeStruct((1, d), F32),
                   jax.ShapeDtypeStruct((1, d), F32)),
        in_specs=[_row_spec(TM, d), _row_spec(TM, d_ple), _row_spec(TM, d), _const_spec((1, d)), _const_spec((1, d)),
                  _const_spec(w_pg.shape), _const_spec(w_ple.shape)],
        out_specs=(_row_spec(TM, d), _row_spec(TM, d), _row_spec(TM, d), _row_spec(TM, d),
                   _acc_spec((1, LANES)), _acc_spec((1, d)), _acc_spec((1, d))),
        compiler_params=_seq_params(),
    )(x2, p, target, g_ple, g_final, w_pg, w_ple)


def _mlp_bwd(dx2, a, x1, g_mlp, w_up, w_down):
    t_len, d = x1.shape
    d_ff = w_up.shape[1]
    ch = _ff_chunk(d_ff)

    def body(dx2_ref, a_ref, x1_ref, g_ref, wu_ref, wd_ref, dx1_ref, da_ref, h2_ref, dg_ref):
        @pl.when(pl.program_id(0) == 0)
        def _():
            dg_ref[...] = jnp.zeros_like(dg_ref)

        dx2v = dx2_ref[...]
        dx2b = dx2v.astype(BF16)
        xn, r = _rms(x1_ref[...])
        h2_ref[...] = (xn * g_ref[...]).T.astype(BF16)
        dh = jnp.zeros((TM, d), F32)
        for j in range(d_ff // ch):
            dact = _dot_nt(dx2b, wd_ref[j * ch:(j + 1) * ch, :])
            da = (dact * 2.0 * jnp.maximum(a_ref[:, j * ch:(j + 1) * ch].astype(F32), 0.0)).astype(BF16)
            da_ref[:, j * ch:(j + 1) * ch] = da
            dh = dh + _dot_nt(da, wu_ref[:, j * ch:(j + 1) * ch])
        dres, dg = _rms_bwd(dh, xn, r, g_ref[...])
        dg_ref[...] += dg
        dx1_ref[...] = dx2v + dres

    return pl.pallas_call(
        body, name="mlp_bwd", grid=(t_len // TM,),
        out_shape=(jax.ShapeDtypeStruct((t_len, d), F32), jax.ShapeDtypeStruct((t_len, d_ff), BF16),
                   jax.ShapeDtypeStruct((d, t_len), BF16), jax.ShapeDtypeStruct((1, d), F32)),
        in_specs=[_row_spec(TM, d), _row_spec(TM, d_ff), _row_spec(TM, d), _const_spec((1, d)),
                  _const_spec(w_up.shape), _const_spec(w_down.shape)],
        out_specs=(_row_spec(TM, d), _row_spec(TM, d_ff), _col_spec(d, TM), _acc_spec((1, d))),
        compiler_params=_seq_params(),
    )(dx2, a, x1, g_mlp, w_up, w_down)


def _mix_bwd(dx1, o_fox, o_sb, gl, w_bf, w_bs, w_out, b_gate):
    t_len, d = dx1.shape

    def body(dx1_ref, of_ref, os_ref, gl_ref, wbf_ref, wbs_ref, wo_ref, bg_ref,
             mg_ref, dbf_ref, dbs_ref, dgl_ref, dof_ref, dos_ref, dbg_ref):
        @pl.when(pl.program_id(0) == 0)
        def _():
            dbg_ref[...] = jnp.zeros_like(dbg_ref)

        dmerged = _dot_nt(dx1_ref[...].astype(BF16), wo_ref[...])
        br_f = _dot(of_ref[...], wbf_ref[...])
        br_s = _dot(os_ref[...], wbs_ref[...])
        ga = _sigmoid(gl_ref[:, :d].astype(F32) + bg_ref[0:1, :])
        gb = _sigmoid(gl_ref[:, d:].astype(F32) + bg_ref[1:2, :])
        mg_ref[...] = (ga * br_f + gb * br_s).astype(BF16)
        dbf = (dmerged * ga).astype(BF16)
        dbs = (dmerged * gb).astype(BF16)
        dbf_ref[...] = dbf
        dbs_ref[...] = dbs
        dla = dmerged * br_f * ga * (1.0 - ga)
        dlb = dmerged * br_s * gb * (1.0 - gb)
        dgl_ref[:, :d] = dla.astype(BF16)
        dgl_ref[:, d:] = dlb.astype(BF16)
        dbg_ref[0:1, :] += jnp.sum(dla, axis=0, keepdims=True)
        dbg_ref[1:2, :] += jnp.sum(dlb, axis=0, keepdims=True)
        dof_ref[...] = _dot_nt(dbf, wbf_ref[...]).astype(BF16)
        dos_ref[...] = _dot_nt(dbs, wbs_ref[...]).astype(BF16)

    shp_d = jax.ShapeDtypeStruct((t_len, d), BF16)
    shp_h = jax.ShapeDtypeStruct((t_len, D_BRANCH), BF16)
    return pl.pallas_call(
        body, name="mix_bwd", grid=(t_len // TM,),
        out_shape=(shp_d, shp_d, shp_d, jax.ShapeDtypeStruct((t_len, 2 * d), BF16), shp_h, shp_h,
                   jax.ShapeDtypeStruct((2, d), F32)),
        in_specs=[_row_spec(TM, d), _row_spec(TM, D_BRANCH), _row_spec(TM, D_BRANCH), _row_spec(TM, 2 * d),
                  _const_spec(w_bf.shape), _const_spec(w_bs.shape), _const_spec(w_out.shape), _const_spec(b_gate.shape)],
        out_specs=(_row_spec(TM, d), _row_spec(TM, d), _row_spec(TM, d), _row_spec(TM, 2 * d),
                   _row_spec(TM, D_BRANCH), _row_spec(TM, D_BRANCH), _acc_spec((2, d))),
        compiler_params=_seq_params(),
    )(dx1, o_fox, o_sb, gl, w_bf, w_bs, w_out, b_gate)


def _inproj_bwd(dqk_f, dv_f, dqkv_b, dgl, df, dx1, x, g_mix, w_pad):
    t_len, d = x.shape
    lay, _ = _pad_layout(d)
    slot_w = N_HEADS * HEAD_SLOT

    def body(dqk_ref, dvf_ref, db_ref, dgl_ref, df_ref, dx1_ref, x_ref, g_ref, w_ref, dx_ref, h1_ref, dg_ref):
        @pl.when(pl.program_id(0) == 0)
        def _():
            dg_ref[...] = jnp.zeros_like(dg_ref)

        def back(piece, name):
            lo, hi = lay[name]
            return _dot(piece, w_ref[lo:hi, :])

        xn, r = _rms(x_ref[...])
        h1_ref[...] = (xn * g_ref[...]).astype(BF16)
        dh = (back(df_ref[...], "forget") + back(dgl_ref[...], "gates") + back(dqk_ref[0], "qf")
              + back(dqk_ref[1], "kf") + back(dvf_ref[...], "vf") + back(db_ref[0], "qb") + back(db_ref[1], "kb")
              + back(db_ref[2], "vb"))
        dres, dg = _rms_bwd(dh, xn, r, g_ref[...])
        dg_ref[...] += dg
        dx_ref[...] = dx1_ref[...] + dres

    return pl.pallas_call(
        body, name="inproj_bwd", grid=(t_len // TM,),
        out_shape=(jax.ShapeDtypeStruct((t_len, d), F32), jax.ShapeDtypeStruct((t_len, d), BF16),
                   jax.ShapeDtypeStruct((1, d), F32)),
        in_specs=[_row3_spec(2, TM, slot_w), _row_spec(TM, D_BRANCH), _row3_spec(3, TM, D_BRANCH),
                  _row_spec(TM, 2 * d), _row_spec(TM, LANES), _row_spec(TM, d), _row_spec(TM, d), _const_spec((1, d)),
                  _const_spec(w_pad.shape)],
        out_specs=(_row_spec(TM, d), _row_spec(TM, d), _acc_spec((1, d))),
        compiler_params=_seq_params(),
    )(dqk_f, dv_f, dqkv_b, dgl, df, dx1, x, g_mix, w_pad)


def _cols_to_slabs(full):
    r, c8 = full.shape
    return full.reshape(r, N_DEV, c8 // N_DEV).transpose(1, 0, 2)


def _slabs_to_cols(slabs):
    n, r, c = slabs.shape
    return slabs.transpose(1, 0, 2).reshape(r, n * c)


def _win_sizes(d):
    return (D_BRANCH, D_BRANCH, D_BRANCH, N_HEADS, D_BRANCH, D_BRANCH, D_BRANCH, d, d)


def _split_win(w_t, d):
    out, off = [], 0
    for s in _win_sizes(d):
        out.append(w_t[off:off + s])
        off += s
    return out


def _to_slots(w_t):
    c = w_t.shape[1]
    return jnp.pad(w_t.reshape(N_HEADS, HEAD_DIM, c), ((0, 0), (0, HEAD_SLOT - HEAD_DIM), (0, 0))).reshape(-1, c)


def _from_slots(w_t):
    c = w_t.shape[1]
    return w_t.reshape(N_HEADS, HEAD_SLOT, c)[:, :HEAD_DIM].reshape(N_HEADS * HEAD_DIM, c)


def _pad_win(w_full_t, d):
    qa, ka, va, fa, qb, kb, vb, ga, gb = _split_win(w_full_t, d)
    scale = HEAD_DIM ** -0.5
    fpad = jnp.pad(fa, ((0, LANES - N_HEADS), (0, 0)))
    return jnp.concatenate([_to_slots(qa * scale), _to_slots(ka), va, qb * scale, kb, vb, ga, gb, fpad], axis=0)


def _unpad_dwin(dqk_f, dv_f, dqkv_b, dgates, dforget, d):
    scale = HEAD_DIM ** -0.5
    return jnp.concatenate([_from_slots(dqk_f[0]) * scale, _from_slots(dqk_f[1]), dv_f, dforget[:N_HEADS],
                            dqkv_b[0] * scale, dqkv_b[1], dqkv_b[2], dgates], axis=0)


def _c_lane_constants():
    head = jnp.arange(LANES)[:, None]
    lane = jnp.arange(N_HEADS * HEAD_SLOT)[None, :]
    in_head = (lane // HEAD_SLOT == head) & (head < N_HEADS)

    def place(first):
        return jnp.stack([(in_head & (lane % HEAD_SLOT == first + j)) for j in range(3)]).astype(BF16)

    def ones(first):
        off = lane % HEAD_SLOT
        return ((off >= first) & (off < first + 3)).astype(F32)

    return place(C_TERMS_Q), place(C_TERMS_K), ones(C_ONES_Q), ones(C_ONES_K)


def _pad_rows(a, rows):
    return jnp.pad(a, [(0, 0)] * (a.ndim - 2) + [(0, rows - a.shape[-2]), (0, 0)])


def kernel(x, p, g_mix, w_in, b_forget, b_gate, w_branch_fox, w_branch_sb, w_out, g_mlp, w_up, w_down, g_ple, w_ple_gate, w_ple, g_final, loss_target, m_g_mix, m_w_in, m_b_forget, m_b_gate, m_w_branch_fox, m_w_branch_sb, m_w_out, m_g_mlp, m_w_up, m_w_down, m_g_ple, m_w_ple_gate, m_w_ple, m_g_final, v_g_mix, v_w_in, v_b_forget, v_b_gate, v_w_branch_fox, v_w_branch_sb, v_w_out, v_g_mlp, v_w_up, v_w_down, v_g_ple, v_w_ple_gate, v_w_ple, v_g_final):
    batch, seq, d = x.shape
    t_len = batch * seq
    d_ple = p.shape[-1]
    d_ff = w_up.shape[-1] * N_DEV
    dn = d // N_DEV
    fn = d_ff // N_DEV
    my_c = lax.axis_index("c")
    my_dev = 4 * lax.axis_index("x") + 2 * lax.axis_index("y") + my_c

    bg_hi = b_gate[0].astype(BF16)
    bg_r = b_gate[0] - bg_hi.astype(F32)
    bg_mid = bg_r.astype(BF16)
    bg_lo = (bg_r - bg_mid.astype(F32)).astype(BF16)
    narrow_rows = 2 * D_BRANCH + d_ple + 6
    narrow_rows_pad = -(-narrow_rows // 16) * 16
    narrow = _pad_rows(jnp.concatenate(
        [w_branch_fox[0].astype(BF16), w_branch_sb[0].astype(BF16), w_ple[0].astype(BF16), bg_hi, bg_mid, bg_lo],
        axis=0), narrow_rows_pad)
    g_in, = _all_gather([w_in[0].T.astype(BF16)])
    w_pad = _pad_win(g_in.reshape(-1, d), d)
    bf_pad = jnp.pad(b_forget, ((0, 0), (0, LANES - N_HEADS)))
    place_q, place_k, ones_q, ones_k = _c_lane_constants()

    x2d = x.reshape(t_len, d)
    p2d = p.reshape(t_len, d_ple)
    tgt2d = loss_target.reshape(t_len, d)
    qf, kf, kft, vf, vft, qkvb, kbt, vbt, gl, fpre = _inproj_fwd(
        x2d, g_mix, w_pad, bf_pad, place_q, place_k, ones_q, ones_k, seq)
    o_sb, ltot, (g_up, g_out, g_down, g_pg, g_narrow) = _sb_fwd(qkvb, vbt, batch, seq, [
        w_up[0].astype(BF16), w_out[0].astype(BF16), w_down[0].astype(BF16), w_ple_gate[0].astype(BF16), narrow])
    o_fox, lse = _fox_fwd(qf, kf, vft, batch, seq)
    w_up_full = _slabs_to_cols(g_up)
    w_out_full = g_out.reshape(d, d)
    w_down_full = g_down.reshape(d_ff, d)
    w_pg_full = g_pg.reshape(d, d)
    w_bf_full = _slabs_to_cols(g_narrow[:, :D_BRANCH])
    w_bs_full = _slabs_to_cols(g_narrow[:, D_BRANCH:2 * D_BRANCH])
    w_ple_full = _slabs_to_cols(g_narrow[:, 2 * D_BRANCH:2 * D_BRANCH + d_ple])
    bg_terms = g_narrow[:, 2 * D_BRANCH + d_ple:narrow_rows].astype(F32)
    b_gate_full = _slabs_to_cols(bg_terms[:, 0:2] + bg_terms[:, 2:4] + bg_terms[:, 4:6])
    x1 = _mix_fwd(o_fox, o_sb, gl, x2d, w_bf_full, w_bs_full, w_out_full, b_gate_full)
    a_up, x2 = _mlp_fwd(x1, g_mlp, w_up_full, w_down_full)

    dx2, h3, dpre, dpe, loss_acc, dg_ple, dg_final = _head_fwd_bwd(
        x2, p2d, tgt2d, g_ple, g_final.reshape(1, d), w_pg_full, w_ple_full)
    dx1, da_up, h2t, dg_mlp = _mlp_bwd(dx2, a_up, x1, g_mlp, w_up_full, w_down_full)
    merged, dbr_f, dbr_s, dgl, do_fox, do_sb, dbg = _mix_bwd(
        dx1, o_fox, o_sb, gl, w_bf_full, w_bs_full, w_out_full, b_gate_full)

    def column_shards(name, lhs, rhs, lhs_t=False):
        if (rhs.shape[-1] // N_DEV) % (4 * LANES) == 0:
            return _matmul_tn(name, lhs, rhs, slabs=True, lhs_t=lhs_t)
        return _cols_to_slabs(_matmul_tn(name, lhs, rhs, lhs_t=lhs_t))

    part_up = column_shards("dw_up", h2t, da_up, lhs_t=True)
    part_out = _matmul_tn("dw_out", merged, dx1).reshape(N_DEV, dn, d)
    part_down = _matmul_tn("dw_down", a_up, dx2, relu2=True).reshape(N_DEV, fn, d)
    part_pg = _matmul_tn("dw_ple_gate", h3, dpre).reshape(N_DEV, dn, d)
    part_narrow = _pad_rows(jnp.concatenate(
        [column_shards("dw_branch_fox", o_fox, dbr_f), column_shards("dw_branch_sb", o_sb, dbr_s),
         column_shards("dw_ple", p2d, dpe)], axis=1), narrow_rows_pad)
    early = [part_up, part_out, part_down, part_pg, part_narrow]
    early_sums = [_pair_add("pair_add_%d" % i, pt, rc, my_c)
                  for i, (pt, rc) in enumerate(zip(early, _rs_core_pair("reduce_scatter_core_pair_early", early)))]

    dqk_f, dv_f, dc_queries, dc_keys = _fox_bwd(qf, kf, kft, vf, o_fox, do_fox, lse, batch, seq)
    dqkv_b, (s_up, s_out, s_down, s_pg, s_narrow) = _sb_bwd(qkvb, kbt, do_sb, ltot, batch, seq, early_sums)
    dcq_tok = dc_queries.reshape(batch, N_HEADS, seq).transpose(0, 2, 1).reshape(t_len, N_HEADS)
    dck_tok = dc_keys[..., :NH].transpose(0, 2, 1, 3).reshape(t_len, N_HEADS)
    lane_pad = ((0, 0), (0, LANES - N_HEADS))
    df, db_forget = _forget_bwd(jnp.pad(dcq_tok, lane_pad), jnp.pad(dck_tok, lane_pad), fpre, batch, seq)
    grad_x, h1, dg_mix = _inproj_bwd(dqk_f, dv_f, dqkv_b, dgl, df, dx1, x2d, g_mix, w_pad)

    gw_in = _unpad_dwin(*[_matmul_tn("dw_in_" + tag, piece, h1) for tag, piece in (
        ("fox_qk", dqk_f), ("fox_v", dv_f), ("sb", dqkv_b), ("gates", dgl), ("forget", df))], d)
    part_in = gw_in.reshape(N_DEV, -1, d)
    recv_in, = _rs_core_pair("reduce_scatter_core_pair_w_in", [part_in])
    s_in, = _rs_chips([_pair_add("pair_add_w_in", part_in, recv_in, my_c)])

    small = jnp.concatenate([
        dg_mix, dg_mlp, dg_ple, dg_final, jnp.pad(db_forget[:, :N_HEADS], ((0, 0), (0, d - N_HEADS))), dbg,
        jnp.pad(loss_acc[:, :1], ((0, 0), (0, d - 1)))], axis=0)
    small = _all_reduce_small(small)
    loss = small[7, 0]
    small_grads = {
        "g_mix": small[0:1], "g_mlp": small[1:2], "g_ple": small[2:3], "g_final": small[3:4],
        "b_forget": small[4:5, :N_HEADS],
        "b_gate": lax.dynamic_slice_in_dim(small[5:7], my_dev * dn, dn, axis=1),
    }

    weights = {"g_mix": g_mix, "w_in": w_in, "b_forget": b_forget, "b_gate": b_gate, "w_branch_fox": w_branch_fox,
               "w_branch_sb": w_branch_sb, "w_out": w_out, "g_mlp": g_mlp, "w_up": w_up, "w_down": w_down,
               "g_ple": g_ple, "w_ple_gate": w_ple_gate, "w_ple": w_ple, "g_final": g_final}
    m_in = {"g_mix": m_g_mix, "w_in": m_w_in, "b_forget": m_b_forget, "b_gate": m_b_gate,
            "w_branch_fox": m_w_branch_fox, "w_branch_sb": m_w_branch_sb, "w_out": m_w_out, "g_mlp": m_g_mlp,
            "w_up": m_w_up, "w_down": m_w_down, "g_ple": m_g_ple, "w_ple_gate": m_w_ple_gate, "w_ple": m_w_ple,
            "g_final": m_g_final}
    v_in = {"g_mix": v_g_mix, "w_in": v_w_in, "b_forget": v_b_forget, "b_gate": v_b_gate,
            "w_branch_fox": v_w_branch_fox, "w_branch_sb": v_w_branch_sb, "w_out": v_w_out, "g_mlp": v_g_mlp,
            "w_up": v_w_up, "w_down": v_w_down, "g_ple": v_g_ple, "w_ple_gate": v_w_ple_gate, "w_ple": v_w_ple,
            "g_final": v_g_final}
    names = list(weights)

    def as2d(a):
        return a.reshape(-1, a.shape[-1])

    result = {}
    big = {"w_up": (s_up, 0), "w_out": (s_out, 0), "w_down": (s_down, 0), "w_ple_gate": (s_pg, 0),
           "w_branch_fox": (s_narrow, 0), "w_branch_sb": (s_narrow, D_BRANCH), "w_ple": (s_narrow, 2 * D_BRANCH)}
    for n, (parts, off) in big.items():
        result[n] = _adamw_parts("adamw_" + n, as2d(weights[n]), parts, off, as2d(m_in[n]), as2d(v_in[n]))
    result["w_in"] = tuple(r.T for r in _adamw_parts("adamw_w_in", w_in[0].T, s_in, 0, m_w_in[0].T, v_w_in[0].T))
    small_names = list(small_grads)
    small_out = _adamw_small([(as2d(weights[n]), small_grads[n], as2d(m_in[n]), as2d(v_in[n])) for n in small_names])
    for n, (dlt, nm, nv) in zip(small_names, small_out):
        result[n] = (small_grads[n], dlt, nm, nv)
    outs = [[result[n][k].reshape(weights[n].shape) for n in names] for k in range(4)]
    return (loss, grad_x.reshape(x.shape), *outs[0], *outs[1], *outs[2], *outs[3])
```

```python
import jax
import jax.numpy as jnp
from jax import lax
from jax.experimental import pallas as pl
from jax.experimental.pallas import tpu as pltpu

F32 = jnp.float32
BF16 = jnp.bfloat16

HEAD_DIM = 64
N_HEADS = 8
D_BRANCH = N_HEADS * HEAD_DIM
EPS = 1e-6
ADAM_LR = 0.001
ADAM_B1 = 0.9
ADAM_B2 = 0.999
ADAM_EPS = 1e-08
ADAM_WD = 0.01
ADAM_STEP = 10

N_DEV = 8
LANES = 128
TM = 256
TQ = 256
TK = 256
NH = 4
HEAD_SLOT = 128
C_TERMS_Q = 64
C_ONES_K = 64
C_TERMS_K = 67
C_ONES_Q = 67
NEG = -1e30
VMEM_LIMIT = 56 * 1024 * 1024
MESH = pl.DeviceIdType.MESH


def _dot(a, b):
    return jnp.dot(a, b, preferred_element_type=F32)


def _dot_nt(a, b):
    return lax.dot_general(a, b, (((1,), (1,)), ((), ())), preferred_element_type=F32)


def _dot_tn(a, b):
    return lax.dot_general(a, b, (((0,), (0,)), ((), ())), preferred_element_type=F32)


def _sigmoid(x):
    return 1.0 / (1.0 + jnp.exp(-x))


def _softplus(x):
    return jnp.maximum(x, 0.0) + jnp.log(1.0 + jnp.exp(-jnp.abs(x)))


def _split2(x):
    hi = x.astype(BF16)
    lo = (x - hi.astype(F32)).astype(BF16)
    return hi, lo


def _split3(x):
    hi = x.astype(BF16)
    r = x - hi.astype(F32)
    mid = r.astype(BF16)
    lo = (r - mid.astype(F32)).astype(BF16)
    return hi, mid, lo


def _rows_dot_mask(x, mask_bf16):
    hi, lo = _split2(x)
    return _dot(hi, mask_bf16) + _dot(lo, mask_bf16)


def _tri(n, rel):
    r = lax.broadcasted_iota(jnp.int32, (n, n), 0)
    c = lax.broadcasted_iota(jnp.int32, (n, n), 1)
    return rel(r, c).astype(BF16)


def _rms(x):
    r = lax.rsqrt(jnp.mean(x * x, axis=-1, keepdims=True) + EPS)
    return x * r, r


def _rms_bwd(dh, xn, r, g):
    dxn = dh * g
    dx = r * (dxn - xn * jnp.mean(dxn * xn, axis=-1, keepdims=True))
    return dx, jnp.sum(dh * xn, axis=0, keepdims=True)


def _row_spec(tm, cols):
    return pl.BlockSpec((tm, cols), lambda i: (i, 0))


def _row3_spec(g, tm, cols):
    return pl.BlockSpec((g, tm, cols), lambda i: (0, i, 0))


def _col_spec(rows, tm):
    return pl.BlockSpec((rows, tm), lambda i: (0, i))


def _const_spec(shape):
    nd = len(shape)
    return pl.BlockSpec(shape, lambda i: (0,) * nd, pipeline_mode=pl.Buffered(1))


def _acc_spec(shape):
    nd = len(shape)
    return pl.BlockSpec(shape, lambda i: (0,) * nd)


def _seq_params():
    return pltpu.CompilerParams(dimension_semantics=("arbitrary",), vmem_limit_bytes=VMEM_LIMIT)


def _mesh_pos():
    return lax.axis_index("x"), lax.axis_index("y"), lax.axis_index("c")


def _other_chips(x, y):
    return [(1 - x, y), (x, 1 - y), (1 - x, 1 - y)]


def _hbm_specs(n):
    return [pl.BlockSpec(memory_space=pl.ANY)] * n


def _gather_plan(x_refs, out_refs, send_sems, recv_sems, local_sems):
    n = len(x_refs)
    x, y, c = _mesh_pos()
    me, sibling = (x, y, c), (x, y, 1 - c)
    chips = _other_chips(x, y)

    def index(px, py, pc):
        return 4 * px + 2 * py + pc

    def copy(a, k, block, to, src=None):
        slab = out_refs[a].at[index(*block)]
        return pltpu.make_async_remote_copy(
            src_ref=slab if src is None else src, dst_ref=slab,
            send_sem=send_sems.at[7 * a + k], recv_sem=recv_sems.at[7 * a + k], device_id=to, device_id_type=MESH)

    mine = [pltpu.make_async_copy(x_refs[a], out_refs[a].at[index(*me)], local_sems.at[a]) for a in range(n)]
    first = []
    for a in range(n):
        first.append(copy(a, 0, me, sibling, src=x_refs[a]))
        first += [copy(a, 1 + j, me, (cx, cy, c), src=x_refs[a]) for j, (cx, cy) in enumerate(chips)]

    def start():
        for cp in mine + first:
            cp.start()

    def finish():
        passed = []
        for j, (cx, cy) in enumerate(chips):
            for a in range(n):
                copy(a, 1 + j, (cx, cy, c), me).wait_recv()
                passed.append(copy(a, 4 + j, (cx, cy, c), sibling))
                passed[-1].start()
        for a in range(n):
            copy(a, 0, sibling, me).wait_recv()
            for j, (cx, cy) in enumerate(chips):
                copy(a, 4 + j, (cx, cy, 1 - c), me).wait_recv()
        for cp in first + passed:
            cp.wait_send()
        for cp in mine:
            cp.wait()

    return start, finish


def _gather_shapes(shards):
    return [jax.ShapeDtypeStruct((N_DEV,) + s.shape, s.dtype) for s in shards]


def _gather_sems(n):
    return [pltpu.SemaphoreType.DMA((7 * n,)), pltpu.SemaphoreType.DMA((7 * n,)), pltpu.SemaphoreType.DMA((n,))]


def _all_gather(shards):
    n = len(shards)

    def body(*refs):
        start, finish = _gather_plan(refs[:n], refs[n:2 * n], *refs[2 * n:])
        start()
        finish()

    return pl.pallas_call(
        body, name="all_gather_weights", out_shape=_gather_shapes(shards),
        in_specs=_hbm_specs(n), out_specs=_hbm_specs(n), scratch_shapes=_gather_sems(n),
    )(*shards)


def _rs_core_pair(name, partials):
    n = len(partials)

    def body(*refs):
        p_refs, recv_refs = refs[:n], refs[n:2 * n]
        send_sems, recv_sems = refs[2 * n:]
        x, y, c = _mesh_pos()
        for a in range(n):
            for chip in range(4):
                pltpu.make_async_remote_copy(
                    src_ref=p_refs[a].at[2 * chip + (1 - c)], dst_ref=recv_refs[a].at[chip],
                    send_sem=send_sems.at[a], recv_sem=recv_sems.at[a],
                    device_id=(x, y, 1 - c), device_id_type=MESH).start()
        for a in range(n):
            pltpu.make_async_remote_copy(
                src_ref=recv_refs[a], dst_ref=recv_refs[a], send_sem=send_sems.at[a], recv_sem=recv_sems.at[a],
                device_id=(x, y, 1 - c), device_id_type=MESH).wait()

    return pl.pallas_call(
        body, name=name,
        out_shape=[jax.ShapeDtypeStruct((4,) + s.shape[1:], s.dtype) for s in partials],
        in_specs=_hbm_specs(n), out_specs=_hbm_specs(n),
        scratch_shapes=[pltpu.SemaphoreType.DMA((n,)), pltpu.SemaphoreType.DMA((n,))],
    )(*partials)


def _chips_plan(cs_refs, out_refs, send_sems, recv_sems, local_sems):
    n = len(cs_refs)
    x, y, c = _mesh_pos()
    chip = 2 * x + y
    chips = _other_chips(x, y)
    mine = [pltpu.make_async_copy(cs_refs[a].at[chip], out_refs[a].at[chip], local_sems.at[a]) for a in range(n)]
    sends = [pltpu.make_async_remote_copy(
        src_ref=cs_refs[a].at[2 * cx + cy], dst_ref=out_refs[a].at[chip],
        send_sem=send_sems.at[3 * a + j], recv_sem=recv_sems.at[3 * a + j],
        device_id=(cx, cy, c), device_id_type=MESH) for a in range(n) for j, (cx, cy) in enumerate(chips)]

    def start():
        for cp in mine + sends:
            cp.start()

    def finish():
        for a in range(n):
            for j, (cx, cy) in enumerate(chips):
                pltpu.make_async_remote_copy(
                    src_ref=cs_refs[a].at[chip], dst_ref=out_refs[a].at[2 * cx + cy],
                    send_sem=send_sems.at[3 * a + j], recv_sem=recv_sems.at[3 * a + j],
                    device_id=(x, y, c), device_id_type=MESH).wait_recv()
        for cp in sends:
            cp.wait_send()
        for cp in mine:
            cp.wait()

    return start, finish


def _chips_sems(n):
    return [pltpu.SemaphoreType.DMA((3 * n,)), pltpu.SemaphoreType.DMA((3 * n,)), pltpu.SemaphoreType.DMA((n,))]


def _rs_chips(chip_sums):
    n = len(chip_sums)

    def body(*refs):
        start, finish = _chips_plan(refs[:n], refs[n:2 * n], *refs[2 * n:])
        start()
        finish()

    return pl.pallas_call(
        body, name="reduce_scatter_chips",
        out_shape=[jax.ShapeDtypeStruct(s.shape, s.dtype) for s in chip_sums],
        in_specs=_hbm_specs(n), out_specs=_hbm_specs(n), scratch_shapes=_chips_sems(n),
    )(*chip_sums)


def _all_reduce_small(vec):
    rows, cols = vec.shape

    def body(x_ref, land_ref, sum_ref, send_sems, recv_sems):
        x, y, c = _mesh_pos()
        me = 4 * x + 2 * y + c
        land_ref[me] = x_ref[...]
        flips = [(fx, fy, fc) for fx in (0, 1) for fy in (0, 1) for fc in (0, 1)][1:]

        def flipped(f):
            return tuple((1 - v) if b else v for v, b in zip((x, y, c), f))

        sends = []
        for k, f in enumerate(flips):
            sends.append(pltpu.make_async_remote_copy(
                src_ref=x_ref, dst_ref=land_ref.at[me], send_sem=send_sems.at[k], recv_sem=recv_sems.at[k],
                device_id=flipped(f), device_id_type=MESH))
            sends[-1].start()
        for k, f in enumerate(flips):
            px, py, pc = flipped(f)
            pltpu.make_async_remote_copy(
                src_ref=x_ref, dst_ref=land_ref.at[4 * px + 2 * py + pc], send_sem=send_sems.at[k],
                recv_sem=recv_sems.at[k], device_id=(x, y, c), device_id_type=MESH).wait_recv()
        for cp in sends:
            cp.wait_send()
        total = land_ref[0]
        for d in range(1, N_DEV):
            total = total + land_ref[d]
        sum_ref[...] = total

    vm = pl.BlockSpec(memory_space=pltpu.VMEM)
    return pl.pallas_call(
        body, name="all_reduce_small",
        out_shape=(jax.ShapeDtypeStruct((N_DEV, rows, cols), F32), jax.ShapeDtypeStruct((rows, cols), F32)),
        in_specs=[vm], out_specs=(vm, vm),
        scratch_shapes=[pltpu.SemaphoreType.DMA((7,)), pltpu.SemaphoreType.DMA((7,))],
    )(vec)[1]


def _block_rows(rows, cols, itemsize, align, row_off=0):
    best = None
    for t in range(align, rows + 1, align):
        if rows % t == 0 and row_off % t == 0 and t * cols * itemsize <= (1 << 20):
            best = t
    return rows if best is None else best


def _pair_add(name, partial, recv, my_c):
    _, rows, cols = partial.shape
    br = _block_rows(rows, cols, 2, 16)

    def body(c_ref, a_ref, b_ref, o_ref):
        o_ref[...] = (a_ref[...].astype(F32) + b_ref[...].astype(F32)).astype(BF16)

    return pl.pallas_call(
        body, name=name,
        grid_spec=pltpu.PrefetchScalarGridSpec(
            num_scalar_prefetch=1, grid=(4, rows // br),
            in_specs=[pl.BlockSpec((None, None, br, cols), lambda j, i, c_ref: (j, c_ref[0], i, 0)),
                      pl.BlockSpec((None, br, cols), lambda j, i, c_ref: (j, i, 0))],
            out_specs=pl.BlockSpec((None, br, cols), lambda j, i, c_ref: (j, i, 0))),
        out_shape=jax.ShapeDtypeStruct((4, rows, cols), BF16),
    )(my_c.reshape(1).astype(jnp.int32), partial.reshape(4, 2, rows, cols), recv)


def _adam_update(w, g, m, v):
    nm = ADAM_B1 * m + (1.0 - ADAM_B1) * g
    nv = ADAM_B2 * v + (1.0 - ADAM_B2) * (g * g)
    m_hat = nm / (1.0 - ADAM_B1 ** ADAM_STEP)
    v_hat = nv / (1.0 - ADAM_B2 ** ADAM_STEP)
    return -ADAM_LR * (m_hat / (jnp.sqrt(v_hat) + ADAM_EPS) + ADAM_WD * w), nm, nv


def _adamw_parts(name, w, parts, row_off, m, v):
    rows, cols = w.shape
    tr = _block_rows(rows, cols, 4, 16, row_off)
    tc = cols
    if tr == rows and rows % 16 != 0 and cols % (2 * LANES) == 0:
        tc = 2 * LANES
    assert rows % tr == 0 and row_off % tr == 0 and (tc == cols or row_off == 0)
    off = row_off // tr

    def body(w_ref, p_ref, m_ref, v_ref, g_ref, d_ref, nm_ref, nv_ref):
        g = p_ref[0].astype(F32)
        for j in range(1, 4):
            g = g + p_ref[j].astype(F32)
        g_ref[...] = g
        d_ref[...], nm_ref[...], nv_ref[...] = _adam_update(w_ref[...], g, m_ref[...], v_ref[...])

    spec = pl.BlockSpec((tr, tc), lambda i, j: (i, j))
    shp = jax.ShapeDtypeStruct((rows, cols), F32)
    return pl.pallas_call(
        body, name=name, grid=(rows // tr, cols // tc), out_shape=(shp,) * 4,
        in_specs=[spec, pl.BlockSpec((4, tr, tc), lambda i, j: (0, off + i, j)), spec, spec], out_specs=(spec,) * 4,
    )(w, parts, m, v)


def _adamw_small(tensors):
    n = len(tensors)

    def body(*refs):
        ins, outs = refs[:4 * n], refs[4 * n:]
        for t in range(n):
            w_ref, g_ref, m_ref, v_ref = ins[4 * t:4 * t + 4]
            d, nm, nv = _adam_update(w_ref[...], g_ref[...], m_ref[...], v_ref[...])
            outs[3 * t][...], outs[3 * t + 1][...], outs[3 * t + 2][...] = d, nm, nv

    vm = pl.BlockSpec(memory_space=pltpu.VMEM)
    out = pl.pallas_call(
        body, name="adamw_small",
        out_shape=[jax.ShapeDtypeStruct(t[0].shape, F32) for t in tensors for _ in range(3)],
        in_specs=[vm] * (4 * n), out_specs=[vm] * (3 * n),
    )(*[a for t in tensors for a in t])
    return [tuple(out[3 * t:3 * t + 3]) for t in range(n)]


def _matmul_tn(name, a, b, relu2=False, slabs=False, lhs_t=False):
    a_groups = a.shape[0] if a.ndim == 3 else 0
    b_groups = b.shape[0] if b.ndim == 3 else 0
    groups = max(a_groups, b_groups, 1)
    assert not (a_groups and b_groups) and not (a_groups and lhs_t)
    a3 = a if a_groups else a[None]
    b3 = b if b_groups else b[None]
    t_len, k_len = a3.shape[1:][::-1] if lhs_t else a3.shape[1:]
    n_len = b3.shape[2]
    tt = min(t_len, 512)
    tk = min(k_len, 1024)
    tn = n_len // N_DEV if slabs else min(n_len, 1024)
    nt = t_len // tt
    assert not slabs or (groups == 1 and tn <= 1024)

    def body(a_ref, b_ref, o_ref, acc_ref):
        @pl.when(pl.program_id(3) == 0)
        def _():
            acc_ref[...] = jnp.zeros_like(acc_ref)

        av = a_ref[...]
        if relu2:
            av = jnp.square(jnp.maximum(av.astype(F32), 0.0))
        product = _dot if lhs_t else _dot_tn
        acc_ref[...] += product(av.astype(BF16), b_ref[...].astype(BF16))

        @pl.when(pl.program_id(3) == nt - 1)
        def _():
            o_ref[...] = acc_ref[...].astype(BF16)

    def a_group(g):
        return g if a_groups else 0

    def b_group(g):
        return g if b_groups else 0

    if slabs:
        out_shape = jax.ShapeDtypeStruct((N_DEV, k_len, tn), BF16)
        out_spec = pl.BlockSpec((None, tk, tn), lambda g, i, j, t: (j, i, 0))
    else:
        out_shape = jax.ShapeDtypeStruct((groups, k_len, n_len), BF16)
        out_spec = pl.BlockSpec((None, tk, tn), lambda g, i, j, t: (g, i, j))
    out = pl.pallas_call(
        body, name=name, grid=(groups, k_len // tk, n_len // tn, nt), out_shape=out_shape,
        in_specs=[pl.BlockSpec((None, tk, tt), lambda g, i, j, t: (a_group(g), i, t)) if lhs_t
                  else pl.BlockSpec((None, tt, tk), lambda g, i, j, t: (a_group(g), t, i)),
                  pl.BlockSpec((None, tt, tn), lambda g, i, j, t: (b_group(g), t, j))],
        out_specs=out_spec,
        scratch_shapes=[pltpu.VMEM((tk, tn), F32)],
        compiler_params=pltpu.CompilerParams(
            dimension_semantics=("parallel", "parallel", "parallel", "arbitrary"), vmem_limit_bytes=VMEM_LIMIT),
    )(a3, b3)
    return out if (slabs or a_groups or b_groups) else out[0]


def _matmul_tn_once(name, lhs_list, rhs, relu2=False, slabs=False, lhs_t=False):
    t_len, n_len = rhs.shape
    tt = min(t_len, 256)
    nt = t_len // tt
    n_lhs = len(lhs_list)
    assert not (lhs_t or slabs) or (n_lhs == 1 and lhs_list[0].ndim == 2)
    k_shapes = [(a.shape[0], n_len) if lhs_t else a.shape[:-2] + (a.shape[-1], n_len) for a in lhs_list]
    tn = n_len // N_DEV

    def body(*refs):
        a_refs, b_ref = refs[:n_lhs], refs[n_lhs]
        o_refs, acc_refs = refs[n_lhs + 1:2 * n_lhs + 1], refs[2 * n_lhs + 1:]
        step = pl.program_id(0)

        @pl.when(step == 0)
        def _():
            for acc in acc_refs:
                acc[...] = jnp.zeros_like(acc)

        bv = b_ref[...].astype(BF16)

        def piece(av):
            if relu2:
                av = jnp.square(jnp.maximum(av.astype(F32), 0.0))
            return (_dot if lhs_t else _dot_tn)(av.astype(BF16), bv)

        for a_ref, acc in zip(a_refs, acc_refs):
            if len(acc.shape) == 3:
                for g in range(acc.shape[0]):
                    acc[g] += piece(a_ref[g])
            else:
                acc[...] += piece(a_ref[...])

        @pl.when(step == nt - 1)
        def _():
            for o_ref, acc in zip(o_refs, acc_refs):
                if slabs:
                    for j in range(N_DEV):
                        o_ref[j] = acc[:, j * tn:(j + 1) * tn].astype(BF16)
                else:
                    o_ref[...] = acc[...].astype(BF16)

    def lhs_spec(a):
        if lhs_t:
            return pl.BlockSpec((a.shape[0], tt), lambda t: (0, t))
        if a.ndim == 3:
            return pl.BlockSpec((a.shape[0], tt, a.shape[2]), lambda t: (0, t, 0))
        return pl.BlockSpec((tt, a.shape[1]), lambda t: (t, 0))

    out_shapes = [(N_DEV, k_shapes[0][0], tn)] if slabs else k_shapes
    return pl.pallas_call(
        body, name=name, grid=(nt,),
        out_shape=[jax.ShapeDtypeStruct(s, BF16) for s in out_shapes],
        in_specs=[lhs_spec(a) for a in lhs_list] + [pl.BlockSpec((tt, n_len), lambda t: (t, 0))],
        out_specs=[_acc_spec(s) for s in out_shapes],
        scratch_shapes=[pltpu.VMEM(s, F32) for s in k_shapes],
        compiler_params=_seq_params(),
    )(*lhs_list, rhs)


def _pad_layout(d):
    names = ("qf", "kf", "vf", "qb", "kb", "vb", "gates", "forget")
    sizes = (N_HEADS * HEAD_SLOT, N_HEADS * HEAD_SLOT, D_BRANCH, D_BRANCH, D_BRANCH, D_BRANCH, 2 * d, LANES)
    out, off = {}, 0
    for n, s in zip(names, sizes):
        out[n] = (off, off + s)
        off += s
    return out, off


def _slot_rows(xt, extra):
    parts = []
    for h in range(N_HEADS):
        parts += [xt[h * HEAD_DIM:(h + 1) * HEAD_DIM, :], extra]
    return jnp.concatenate(parts, axis=0)


def _inproj_fwd(x, g_mix, w_pad, bf_pad, place_q, place_k, ones_q, ones_k, seq):
    t_len, d = x.shape
    lay, _ = _pad_layout(d)
    tiles_per_seq = seq // TM
    slot_w = N_HEADS * HEAD_SLOT

    def body(x_ref, g_ref, w_ref, bf_ref, pq_ref, pk_ref, oq_ref, ok_ref,
             qf_ref, kf_ref, kft_ref, vf_ref, vft_ref, qkvb_ref, kbt_ref, vbt_ref, gl_ref, fpre_ref, carry_ref):
        @pl.when(pl.program_id(0) % tiles_per_seq == 0)
        def _():
            carry_ref[...] = jnp.zeros_like(carry_ref)

        def proj(name):
            lo, hi = lay[name]
            return _dot_nt(h, w_ref[lo:hi, :])

        xn, _ = _rms(x_ref[...])
        h = (xn * g_ref[...]).astype(BF16)
        fpre = proj("forget") + bf_ref[...]
        fpre_ref[...] = fpre
        logf = -_softplus(-fpre)
        lower = _tri(TM, lambda r, c: c <= r)
        hi, mid, lo = _split3(logf)
        c_val = carry_ref[...] + _dot(lower, hi) + _dot(lower, mid) + _dot(lower, lo)
        carry_ref[...] = carry_ref[...] + jnp.sum(logf, axis=0, keepdims=True)
        c3 = _split3(c_val)
        qf_ref[...] = (proj("qf") + sum(_dot(c3[j], pq_ref[j]) for j in range(3)) + oq_ref[...]).astype(BF16)
        kf = proj("kf") - sum(_dot(c3[j], pk_ref[j]) for j in range(3)) + ok_ref[...]
        kf_ref[...] = kf.astype(BF16)
        kft_ref[0] = kf.T.astype(BF16)
        row0 = (lax.broadcasted_iota(jnp.int32, (HEAD_DIM, TM), 0) == 0).astype(F32)
        zeros = jnp.zeros((HEAD_DIM, TM), F32)
        vf = proj("vf")
        vf_ref[...] = vf.astype(BF16)
        vft_ref[0] = _slot_rows(vf.T, row0).astype(BF16)
        qkvb_ref[0] = proj("qb").astype(BF16)
        kb = proj("kb")
        qkvb_ref[1] = kb.astype(BF16)
        kbt_ref[0] = _slot_rows(kb.T, zeros).astype(BF16)
        vb = proj("vb")
        qkvb_ref[2] = vb.astype(BF16)
        vbt_ref[0] = _slot_rows(vb.T, row0).astype(BF16)
        gl_ref[...] = proj("gates").astype(BF16)

    n_tiles = t_len // TM
    slot_shape = jax.ShapeDtypeStruct((t_len, slot_w), BF16)
    t_shape = jax.ShapeDtypeStruct((n_tiles, slot_w, TM), BF16)
    t_spec = pl.BlockSpec((1, slot_w, TM), lambda i: (i, 0, 0))
    return pl.pallas_call(
        body, name="inproj_fwd", grid=(n_tiles,),
        out_shape=(slot_shape, slot_shape, t_shape, jax.ShapeDtypeStruct((t_len, D_BRANCH), BF16), t_shape,
                   jax.ShapeDtypeStruct((3, t_len, D_BRANCH), BF16), t_shape, t_shape,
                   jax.ShapeDtypeStruct((t_len, 2 * d), BF16), jax.ShapeDtypeStruct((t_len, LANES), F32)),
        in_specs=[_row_spec(TM, d), _const_spec((1, d)), _const_spec(w_pad.shape), _const_spec((1, LANES)),
                  _const_spec(place_q.shape), _const_spec(place_k.shape), _const_spec((1, slot_w)),
                  _const_spec((1, slot_w))],
        out_specs=(_row_spec(TM, slot_w), _row_spec(TM, slot_w), t_spec, _row_spec(TM, D_BRANCH), t_spec,
                   _row3_spec(3, TM, D_BRANCH), t_spec, t_spec, _row_spec(TM, 2 * d), _row_spec(TM, LANES)),
        scratch_shapes=[pltpu.VMEM((1, LANES), F32)],
        compiler_params=_seq_params(),
    )(x, g_mix, w_pad, bf_pad, place_q, place_k, ones_q, ones_k)


def _slot_spec(seq):
    return pl.BlockSpec((seq, NH * HEAD_SLOT), lambda b, g: (b, g))


def _slot2_spec(seq):
    return pl.BlockSpec((2, seq, NH * HEAD_SLOT), lambda b, g: (0, b, g))


def _group_spec(seq):
    return pl.BlockSpec((seq, NH * HEAD_DIM), lambda b, g: (b, g))


def _group3_spec(which, seq):
    return pl.BlockSpec((None, seq, NH * HEAD_DIM), lambda b, g: (which, b, g))


def _tblock_spec(seq):
    return pl.BlockSpec((seq // TK, NH * HEAD_SLOT, TK), lambda b, g: (b, g, 0))


def _qrow_spec(seq):
    return pl.BlockSpec((None, NH, seq // TQ, TQ), lambda b, g: (b, g, 0, 0))


def _stat_spec(seq):
    return pl.BlockSpec((None, None, seq, LANES), lambda b, g: (b, g, 0, 0))


def _attn_params():
    return pltpu.CompilerParams(dimension_semantics=("parallel", "parallel"), vmem_limit_bytes=VMEM_LIMIT)


def _serial_attn_params():
    return pltpu.CompilerParams(dimension_semantics=("arbitrary", "arbitrary"), vmem_limit_bytes=VMEM_LIMIT)


def _hcols(hh):
    return slice(hh * HEAD_DIM, (hh + 1) * HEAD_DIM)


def _hslot(hh):
    return slice(hh * HEAD_SLOT, (hh + 1) * HEAD_SLOT)


def _lane(hh):
    return slice(hh, hh + 1)


def _key_query_mask(rel):
    r = lax.broadcasted_iota(jnp.int32, (TK, TQ), 0)
    c = lax.broadcasted_iota(jnp.int32, (TK, TQ), 1)
    return rel(r, c)


def _heads_cat(vals):
    return jnp.concatenate(vals, axis=1)


def _untranspose(acc_t):
    return acc_t.T[:, :HEAD_DIM]


def _fox_fwd(qf, kf, vft, batch, seq):
    def body(q_ref, k_ref, vt_ref, o_ref, lse_ref, m_s, acc_s):
        causal = _key_query_mask(lambda r, c: r <= c)

        def tile(q0, kj, masked):
            krows = pl.ds(pl.multiple_of(kj * TK, TK), TK)
            heads = range(NH)
            sts = [_dot_nt(k_ref[krows, _hslot(hh)], q_ref[pl.ds(q0, TQ), _hslot(hh)]) for hh in heads]
            if masked:
                sts = [jnp.where(causal, st, NEG) for st in sts]
            m_olds = [m_s[hh] for hh in heads]
            m_news = [jnp.maximum(m_olds[hh], jnp.max(sts[hh], axis=0, keepdims=True)) for hh in heads]
            pts = [jnp.exp(sts[hh] - m_news[hh]).astype(BF16) for hh in heads]
            pvs = [_dot(vt_ref[kj, _hslot(hh), :], pts[hh]) for hh in heads]
            for hh in heads:
                acc_s[hh] = jnp.exp(m_olds[hh] - m_news[hh]) * acc_s[hh] + pvs[hh]
                m_s[hh] = m_news[hh]

        def q_loop(qi, _):
            q0 = pl.multiple_of(qi * TQ, TQ)
            m_s[...] = jnp.full(m_s.shape, NEG, F32)
            acc_s[...] = jnp.zeros_like(acc_s)

            def k_loop(kj, _):
                tile(q0, kj, False)
                return 0

            lax.fori_loop(0, qi, k_loop, 0)
            tile(q0, qi, True)
            outs = []
            for hh in range(NH):
                total = acc_s[hh, HEAD_DIM:HEAD_DIM + 1, :]
                outs.append(_untranspose(acc_s[hh] / total))
                lse_ref[hh, pl.ds(qi, 1), :] = m_s[hh] + jnp.log(total)
            o_ref[pl.ds(q0, TQ), :] = _heads_cat(outs).astype(BF16)
            return 0

        lax.fori_loop(0, seq // TQ, q_loop, 0)

    return pl.pallas_call(
        body, name="fox_fwd", grid=(batch, N_HEADS // NH),
        out_shape=(jax.ShapeDtypeStruct((batch * seq, D_BRANCH), BF16),
                   jax.ShapeDtypeStruct((batch, N_HEADS, seq // TQ, TQ), F32)),
        in_specs=[_slot_spec(seq), _slot_spec(seq), _tblock_spec(seq)],
        out_specs=(_group_spec(seq), _qrow_spec(seq)),
        scratch_shapes=[pltpu.VMEM((NH, 1, TQ), F32), pltpu.VMEM((NH, HEAD_SLOT, TQ), F32)],
        compiler_params=_attn_params(),
    )(qf, kf, vft)


def _fox_bwd(qf, kf, kft, vf, o, do, lse, batch, seq):
    n_q = seq // TQ

    def body(q_ref, k_ref, kt_ref, v_ref, o_ref, do_ref, lse_ref, dqk_ref, dv_ref, dcq_ref, dck_ref,
             delta_s, dqt_acc, dk_s, dv_s):
        causal = _key_query_mask(lambda r, c: r <= c)
        ones8 = jnp.ones((8, HEAD_DIM), BF16)
        dqt_acc[...] = jnp.zeros_like(dqt_acc)

        def prep(qi, _):
            rows = pl.ds(pl.multiple_of(qi * TQ, TQ), TQ)
            for hh in range(NH):
                hi, lo = _split2(do_ref[rows, _hcols(hh)].astype(F32) * o_ref[rows, _hcols(hh)].astype(F32))
                delta_s[hh, pl.ds(qi, 1), :] = (_dot_nt(ones8, hi) + _dot_nt(ones8, lo))[0:1, :]
            return 0

        lax.fori_loop(0, n_q, prep, 0)

        def tile(qi, kj, masked):
            rows = pl.ds(pl.multiple_of(qi * TQ, TQ), TQ)
            krows = pl.ds(pl.multiple_of(kj * TK, TK), TK)
            heads = range(NH)
            qs = [q_ref[rows, _hslot(hh)] for hh in heads]
            douts = [do_ref[rows, _hcols(hh)] for hh in heads]
            sts = [_dot_nt(k_ref[krows, _hslot(hh)], qs[hh]) for hh in heads]
            dps = [_dot_nt(v_ref[krows, _hcols(hh)], douts[hh]) for hh in heads]
            pts = [jnp.exp(sts[hh] - lse_ref[hh, pl.ds(qi, 1), :]) for hh in heads]
            if masked:
                pts = [jnp.where(causal, pt, 0.0) for pt in pts]
            dsts = [(pts[hh] * (dps[hh] - delta_s[hh, pl.ds(qi, 1), :])).astype(BF16) for hh in heads]
            for hh in heads:
                dv_s[hh] += _dot(pts[hh].astype(BF16), douts[hh])
                dk_s[hh] += _dot(dsts[hh], qs[hh])
                dqt_acc[hh, qi] += _dot(kt_ref[kj, _hslot(hh), :], dsts[hh])

        def k_loop(kj, _):
            krows = pl.ds(pl.multiple_of(kj * TK, TK), TK)
            dk_s[...] = jnp.zeros_like(dk_s)
            dv_s[...] = jnp.zeros_like(dv_s)
            tile(kj, kj, True)

            def q_loop(qi, _):
                tile(qi, kj, False)
                return 0

            lax.fori_loop(kj + 1, n_q, q_loop, 0)
            dqk_ref[1, krows, :] = _heads_cat([dk_s[hh] for hh in range(NH)]).astype(BF16)
            dv_ref[krows, :] = _heads_cat([dv_s[hh] for hh in range(NH)]).astype(BF16)
            for hh in range(NH):
                dck_ref[krows, _lane(hh)] = dk_s[hh, :, C_ONES_Q:C_ONES_Q + 1]
            return 0

        lax.fori_loop(0, seq // TK, k_loop, 0)

        def finish(qi, _):
            rows = pl.ds(pl.multiple_of(qi * TQ, TQ), TQ)
            dqk_ref[0, rows, :] = _heads_cat([dqt_acc[hh, qi].T for hh in range(NH)]).astype(BF16)
            for hh in range(NH):
                dcq_ref[hh, pl.ds(qi, 1), :] = dqt_acc[hh, qi, C_ONES_K:C_ONES_K + 1, :]
            return 0

        lax.fori_loop(0, n_q, finish, 0)

    return pl.pallas_call(
        body, name="fox_bwd", grid=(batch, N_HEADS // NH),
        out_shape=(jax.ShapeDtypeStruct((2, batch * seq, N_HEADS * HEAD_SLOT), BF16),
                   jax.ShapeDtypeStruct((batch * seq, D_BRANCH), BF16),
                   jax.ShapeDtypeStruct((batch, N_HEADS, seq // TQ, TQ), F32),
                   jax.ShapeDtypeStruct((batch, N_HEADS // NH, seq, LANES), F32)),
        in_specs=[_slot_spec(seq), _slot_spec(seq), _tblock_spec(seq), _group_spec(seq), _group_spec(seq),
                  _group_spec(seq), _qrow_spec(seq)],
        out_specs=(_slot2_spec(seq), _group_spec(seq), _qrow_spec(seq), _stat_spec(seq)),
        scratch_shapes=[pltpu.VMEM((NH, n_q, TQ), F32), pltpu.VMEM((NH, n_q, HEAD_SLOT, TQ), F32),
                        pltpu.VMEM((NH, TK, HEAD_SLOT), F32), pltpu.VMEM((NH, TK, HEAD_DIM), F32)],
        compiler_params=_attn_params(),
    )(qf, kf, kft, vf, o, do, lse)


def _first_last_step():
    step = pl.program_id(0) * pl.num_programs(1) + pl.program_id(1)
    return step == 0, step == pl.num_programs(0) * pl.num_programs(1) - 1


def _sb_fwd(qkvb, vbt, batch, seq, shards):
    n = len(shards)

    def body(q_ref, k_ref, vt_ref, *rest):
        x_refs, (o_ref, lt_ref), out_refs = rest[:n], rest[n:n + 2], rest[n + 2:2 * n + 2]
        run_s, acc_s = rest[2 * n + 2:2 * n + 4]
        gather_start, gather_finish = _gather_plan(x_refs, out_refs, *rest[2 * n + 4:])
        first_step, last_step = _first_last_step()
        pl.when(first_step)(gather_start)
        strict = _key_query_mask(lambda r, c: r < c)
        later = _tri(TK, lambda r, c: c > r)

        def tile(q0, kj, masked):
            krows = pl.ds(pl.multiple_of(kj * TK, TK), TK)
            heads = range(NH)
            zts = [_dot_nt(k_ref[krows, _hcols(hh)], q_ref[pl.ds(q0, TQ), _hcols(hh)]) for hh in heads]
            lgs = [-_softplus(zt) for zt in zts]
            if masked:
                lgs = [jnp.where(strict, lg, 0.0) for lg in lgs]
            parts = [_split2(lg) for lg in lgs]
            sufs = [_dot(later, hi) + _dot(later, lo) for hi, lo in parts]
            ats = [jnp.exp(zts[hh] + lgs[hh] + run_s[hh] + sufs[hh]) for hh in heads]
            if masked:
                ats = [jnp.where(strict, at, 0.0) for at in ats]
            for hh in heads:
                acc_s[hh] += _dot(vt_ref[kj, _hslot(hh), :], ats[hh].astype(BF16))
                run_s[hh] += jnp.sum(lgs[hh], axis=0, keepdims=True)

        def q_loop(qi, _):
            q0 = pl.multiple_of(qi * TQ, TQ)
            run_s[...] = jnp.zeros_like(run_s)
            acc_s[...] = jnp.zeros_like(acc_s)
            tile(q0, qi, True)

            def k_loop(kk, _):
                tile(q0, qi - 1 - kk, False)
                return 0

            lax.fori_loop(0, qi, k_loop, 0)
            o_ref[pl.ds(q0, TQ), :] = _heads_cat([_untranspose(acc_s[hh]) for hh in range(NH)]).astype(BF16)
            for hh in range(NH):
                lt_ref[hh, pl.ds(qi, 1), :] = run_s[hh]
            return 0

        lax.fori_loop(0, seq // TQ, q_loop, 0)
        pl.when(last_step)(gather_finish)

    out = pl.pallas_call(
        body, name="sb_fwd", grid=(batch, N_HEADS // NH),
        out_shape=[jax.ShapeDtypeStruct((batch * seq, D_BRANCH), BF16),
                   jax.ShapeDtypeStruct((batch, N_HEADS, seq // TQ, TQ), F32)] + _gather_shapes(shards),
        in_specs=[_group3_spec(0, seq), _group3_spec(1, seq), _tblock_spec(seq)] + _hbm_specs(n),
        out_specs=[_group_spec(seq), _qrow_spec(seq)] + _hbm_specs(n),
        scratch_shapes=[pltpu.VMEM((NH, 1, TQ), F32), pltpu.VMEM((NH, HEAD_SLOT, TQ), F32)] + _gather_sems(n),
        compiler_params=_serial_attn_params(),
    )(qkvb, qkvb, vbt, *shards)
    return out[0], out[1], out[2:]


def _sb_bwd(qkvb, kbt, do, ltot, batch, seq, chip_sums):
    n = len(chip_sums)

    def body(q_ref, k_ref, v_ref, kt_ref, do_ref, lt_ref, *rest):
        cs_refs, dqkv_ref, out_refs = rest[:n], rest[n], rest[n + 1:2 * n + 1]
        dk_acc, dv_acc, ls_s, gs_s, dqt_s = rest[2 * n + 1:2 * n + 6]
        chips_start, chips_finish = _chips_plan(cs_refs, out_refs, *rest[2 * n + 6:])
        first_step, last_step = _first_last_step()
        pl.when(first_step)(chips_start)
        strict = _key_query_mask(lambda r, c: r < c)
        upto = _tri(TK, lambda r, c: c <= r)
        before = _tri(TK, lambda r, c: c < r)
        dk_acc[...] = jnp.zeros_like(dk_acc)
        dv_acc[...] = jnp.zeros_like(dv_acc)

        def tile(qi, kj, masked):
            rows = pl.ds(pl.multiple_of(qi * TQ, TQ), TQ)
            krows = pl.ds(pl.multiple_of(kj * TK, TK), TK)
            heads = range(NH)
            qs = [q_ref[rows, _hcols(hh)] for hh in heads]
            douts = [do_ref[rows, _hcols(hh)] for hh in heads]
            zts = [_dot_nt(k_ref[krows, _hcols(hh)], qs[hh]) for hh in heads]
            das = [_dot_nt(v_ref[krows, _hcols(hh)], douts[hh]) for hh in heads]
            lgs = [-_softplus(zt) for zt in zts]
            if masked:
                lgs = [jnp.where(strict, lg, 0.0) for lg in lgs]
            parts = [_split2(lg) for lg in lgs]
            prefs = [_dot(upto, hi) + _dot(upto, lo) for hi, lo in parts]
            ats = [jnp.exp(zts[hh] + lgs[hh] + (lt_ref[hh, pl.ds(qi, 1), :] - ls_s[hh]) - prefs[hh]) for hh in heads]
            if masked:
                ats = [jnp.where(strict, at, 0.0) for at in ats]
            gts = [das[hh] * ats[hh] for hh in heads]
            us = [gs_s[hh] + _dot(before, gts[hh].astype(BF16)) for hh in heads]
            dzts = [(jnp.exp(lgs[hh]) * (gts[hh] + us[hh]) - us[hh]).astype(BF16) for hh in heads]
            for hh in heads:
                dk_acc[hh, krows, :] += _dot(dzts[hh], qs[hh])
                dv_acc[hh, krows, :] += _dot(ats[hh].astype(BF16), douts[hh])
                dqt_s[hh] += _dot(kt_ref[kj, _hslot(hh), :], dzts[hh])
                ls_s[hh] += jnp.sum(lgs[hh], axis=0, keepdims=True)
                gs_s[hh] += jnp.sum(gts[hh], axis=0, keepdims=True)

        def q_loop(qi, _):
            ls_s[...] = jnp.zeros_like(ls_s)
            gs_s[...] = jnp.zeros_like(gs_s)
            dqt_s[...] = jnp.zeros_like(dqt_s)

            def k_loop(kj, _):
                tile(qi, kj, False)
                return 0

            lax.fori_loop(0, qi, k_loop, 0)
            tile(qi, qi, True)
            dqkv_ref[0, pl.ds(pl.multiple_of(qi * TQ, TQ), TQ), :] = _heads_cat(
                [_untranspose(dqt_s[hh]) for hh in range(NH)]).astype(BF16)
            return 0

        lax.fori_loop(0, seq // TQ, q_loop, 0)
        dqkv_ref[1] = _heads_cat([dk_acc[hh] for hh in range(NH)]).astype(BF16)
        dqkv_ref[2] = _heads_cat([dv_acc[hh] for hh in range(NH)]).astype(BF16)
        pl.when(last_step)(chips_finish)

    out = pl.pallas_call(
        body, name="sb_bwd", grid=(batch, N_HEADS // NH),
        out_shape=[jax.ShapeDtypeStruct((3, batch * seq, D_BRANCH), BF16)]
        + [jax.ShapeDtypeStruct(s.shape, s.dtype) for s in chip_sums],
        in_specs=[_group3_spec(0, seq), _group3_spec(1, seq), _group3_spec(2, seq), _tblock_spec(seq),
                  _group_spec(seq), _qrow_spec(seq)] + _hbm_specs(n),
        out_specs=[pl.BlockSpec((3, seq, NH * HEAD_DIM), lambda b, g: (0, b, g))] + _hbm_specs(n),
        scratch_shapes=[pltpu.VMEM((NH, seq, HEAD_DIM), F32), pltpu.VMEM((NH, seq, HEAD_DIM), F32),
                        pltpu.VMEM((NH, 1, TQ), F32), pltpu.VMEM((NH, 1, TQ), F32),
                        pltpu.VMEM((NH, HEAD_SLOT, TQ), F32)] + _chips_sems(n),
        compiler_params=_serial_attn_params(),
    )(qkvb, qkvb, qkvb, kbt, do, ltot, *chip_sums)
    return out[0], out[1:]


def _forget_bwd(dcq_tok, dck_tok, fpre, batch, seq):
    t_len = batch * seq
    tiles = seq // TM

    def rev(i):
        return ((i // tiles) * tiles + (tiles - 1 - i % tiles), 0)

    def body(dcq_ref, dck_ref, f_ref, df_ref, db_ref, carry_ref):
        i = pl.program_id(0)

        @pl.when(i == 0)
        def _():
            db_ref[...] = jnp.zeros_like(db_ref)

        @pl.when(i % tiles == 0)
        def _():
            carry_ref[...] = jnp.zeros_like(carry_ref)

        dc = dcq_ref[...] - dck_ref[...]
        upper = _tri(TM, lambda r, c: c >= r)
        hi, mid, lo = _split3(dc)
        dlogf = carry_ref[...] + _dot(upper, hi) + _dot(upper, mid) + _dot(upper, lo)
        carry_ref[...] = carry_ref[...] + jnp.sum(dc, axis=0, keepdims=True)
        df = dlogf * _sigmoid(-f_ref[...])
        df_ref[...] = df.astype(BF16)
        db_ref[...] += jnp.sum(df, axis=0, keepdims=True)

    return pl.pallas_call(
        body, name="forget_bwd", grid=(t_len // TM,),
        out_shape=(jax.ShapeDtypeStruct((t_len, LANES), BF16), jax.ShapeDtypeStruct((1, LANES), F32)),
        in_specs=[pl.BlockSpec((TM, LANES), rev)] * 3,
        out_specs=(pl.BlockSpec((TM, LANES), rev), _acc_spec((1, LANES))),
        scratch_shapes=[pltpu.VMEM((1, LANES), F32)],
        compiler_params=_seq_params(),
    )(dcq_tok, dck_tok, fpre)


def _mix_fwd(o_fox, o_sb, gl, x, w_bf, w_bs, w_out, b_gate):
    t_len, d = x.shape

    def body(of_ref, os_ref, gl_ref, x_ref, wbf_ref, wbs_ref, wo_ref, bg_ref, x1_ref):
        br_f = _dot(of_ref[...], wbf_ref[...])
        br_s = _dot(os_ref[...], wbs_ref[...])
        ga = _sigmoid(gl_ref[:, :d].astype(F32) + bg_ref[0:1, :])
        gb = _sigmoid(gl_ref[:, d:].astype(F32) + bg_ref[1:2, :])
        merged = ga * br_f + gb * br_s
        x1_ref[...] = x_ref[...] + _dot(merged.astype(BF16), wo_ref[...])

    return pl.pallas_call(
        body, name="mix_fwd", grid=(t_len // TM,),
        out_shape=jax.ShapeDtypeStruct((t_len, d), F32),
        in_specs=[_row_spec(TM, D_BRANCH), _row_spec(TM, D_BRANCH), _row_spec(TM, 2 * d), _row_spec(TM, d),
                  _const_spec(w_bf.shape), _const_spec(w_bs.shape), _const_spec(w_out.shape), _const_spec(b_gate.shape)],
        out_specs=_row_spec(TM, d),
        compiler_params=_seq_params(),
    )(o_fox, o_sb, gl, x, w_bf, w_bs, w_out, b_gate)


def _ff_chunk(d_ff):
    return min(d_ff, 1024)


def _mlp_fwd(x1, g_mlp, w_up, w_down):
    t_len, d = x1.shape
    d_ff = w_up.shape[1]
    ch = _ff_chunk(d_ff)

    def body(x1_ref, g_ref, wu_ref, wd_ref, a_ref, x2_ref):
        x1v = x1_ref[...]
        xn, _ = _rms(x1v)
        h = (xn * g_ref[...]).astype(BF16)
        acc = x1v
        for j in range(d_ff // ch):
            a = _dot(h, wu_ref[:, j * ch:(j + 1) * ch])
            a_ref[:, j * ch:(j + 1) * ch] = a.astype(BF16)
            acc = acc + _dot(jnp.square(jnp.maximum(a, 0.0)).astype(BF16), wd_ref[j * ch:(j + 1) * ch, :])
        x2_ref[...] = acc

    return pl.pallas_call(
        body, name="mlp_fwd", grid=(t_len // TM,),
        out_shape=(jax.ShapeDtypeStruct((t_len, d_ff), BF16), jax.ShapeDtypeStruct((t_len, d), F32)),
        in_specs=[_row_spec(TM, d), _const_spec((1, d)), _const_spec(w_up.shape), _const_spec(w_down.shape)],
        out_specs=(_row_spec(TM, d_ff), _row_spec(TM, d)),
        compiler_params=_seq_params(),
    )(x1, g_mlp, w_up, w_down)


def _head_fwd_bwd(x2, p, target, g_ple, g_final, w_pg, w_ple):
    t_len, d = x2.shape
    d_ple = p.shape[1]

    def body(x2_ref, p_ref, t_ref, gp_ref, gf_ref, wpg_ref, wple_ref,
             dx2_ref, h3_ref, dpre_ref, dpe_ref, loss_ref, dgp_ref, dgf_ref):
        @pl.when(pl.program_id(0) == 0)
        def _():
            loss_ref[...] = jnp.zeros_like(loss_ref)
            dgp_ref[...] = jnp.zeros_like(dgp_ref)
            dgf_ref[...] = jnp.zeros_like(dgf_ref)

        x2v = x2_ref[...]
        x2n, r3 = _rms(x2v)
        h3 = (x2n * gp_ref[...]).astype(BF16)
        h3_ref[...] = h3
        gate = _sigmoid(_dot(h3, wpg_ref[...]))
        pe = _dot(p_ref[...].astype(BF16), wple_ref[...])
        x3n, r4 = _rms(x2v + gate * pe)
        err = x3n * gf_ref[...] - t_ref[...]
        loss_ref[...] += jnp.full(loss_ref.shape, (0.5 / d) * jnp.sum(err * err), F32)
        dx3, dgf = _rms_bwd(err * (1.0 / d), x3n, r4, gf_ref[...])
        dgf_ref[...] += dgf
        dpe_ref[...] = (dx3 * gate).astype(BF16)
        dpre = (dx3 * pe * gate * (1.0 - gate)).astype(BF16)
        dpre_ref[...] = dpre
        dres, dgp = _rms_bwd(_dot_nt(dpre, wpg_ref[...]), x2n, r3, gp_ref[...])
        dgp_ref[...] += dgp
        dx2_ref[...] = dx3 + dres

    shp_b = jax.ShapeDtypeStruct((t_len, d), BF16)
    return pl.pallas_call(
        body, name="head_fwd_bwd", grid=(t_len // TM,),
        out_shape=(jax.ShapeDtypeStruct((t_len, d), F32), shp_b, shp_b, shp_b,
                   jax.ShapeDtypeStruct((1, LANES), F32), jax.ShapeDtypeStruct((1, d), F32),
                   jax.ShapeDtypeStruct((1, d), F32)),
        in_specs=[_row_spec(TM, d), _row_spec(TM, d_ple), _row_spec(TM, d), _const_spec((1, d)), _const_spec((1, d)),
                  _const_spec(w_pg.shape), _const_spec(w_ple.shape)],
        out_specs=(_row_spec(TM, d), _row_spec(TM, d), _row_spec(TM, d), _row_spec(TM, d),
                   _acc_spec((1, LANES)), _acc_spec((1, d)), _acc_spec((1, d))),
        compiler_params=_seq_params(),
    )(x2, p, target, g_ple, g_final, w_pg, w_ple)


def _mlp_bwd(dx2, a, x1, g_mlp, w_up, w_down):
    t_len, d = x1.shape
    d_ff = w_up.shape[1]
    ch = _ff_chunk(d_ff)

    def body(dx2_ref, a_ref, x1_ref, g_ref, wu_ref, wd_ref, dx1_ref, da_ref, h2_ref, dg_ref):
        @pl.when(pl.program_id(0) == 0)
        def _():
            dg_ref[...] = jnp.zeros_like(dg_ref)

        dx2v = dx2_ref[...]
        dx2b = dx2v.astype(BF16)
        xn, r = _rms(x1_ref[...])
        h2_ref[...] = (xn * g_ref[...]).T.astype(BF16)
        dh = jnp.zeros((TM, d), F32)
        for j in range(d_ff // ch):
            dact = _dot_nt(dx2b, wd_ref[j * ch:(j + 1) * ch, :])
            da = (dact * 2.0 * jnp.maximum(a_ref[:, j * ch:(j + 1) * ch].astype(F32), 0.0)).astype(BF16)
            da_ref[:, j * ch:(j + 1) * ch] = da
            dh = dh + _dot_nt(da, wu_ref[:, j * ch:(j + 1) * ch])
        dres, dg = _rms_bwd(dh, xn, r, g_ref[...])
        dg_ref[...] += dg
        dx1_ref[...] = dx2v + dres

    return pl.pallas_call(
        body, name="mlp_bwd", grid=(t_len // TM,),
        out_shape=(jax.ShapeDtypeStruct((t_len, d), F32), jax.ShapeDtypeStruct((t_len, d_ff), BF16),
                   jax.ShapeDtypeStruct((d, t_len), BF16), jax.ShapeDtypeStruct((1, d), F32)),
        in_specs=[_row_spec(TM, d), _row_spec(TM, d_ff), _row_spec(TM, d), _const_spec((1, d)),
                  _const_spec(w_up.shape), _const_spec(w_down.shape)],
        out_specs=(_row_spec(TM, d), _row_spec(TM, d_ff), _col_spec(d, TM), _acc_spec((1, d))),
        compiler_params=_seq_params(),
    )(dx2, a, x1, g_mlp, w_up, w_down)


def _mix_bwd(dx1, o_fox, o_sb, gl, w_bf, w_bs, w_out, b_gate):
    t_len, d = dx1.shape

    def body(dx1_ref, of_ref, os_ref, gl_ref, wbf_ref, wbs_ref, wo_ref, bg_ref,
             mg_ref, dbf_ref, dbs_ref, dgl_ref, dof_ref, dos_ref, dbg_ref):
        @pl.when(pl.program_id(0) == 0)
        def _():
            dbg_ref[...] = jnp.zeros_like(dbg_ref)

        dmerged = _dot_nt(dx1_ref[...].astype(BF16), wo_ref[...])
        br_f = _dot(of_ref[...], wbf_ref[...])
        br_s = _dot(os_ref[...], wbs_ref[...])
        ga = _sigmoid(gl_ref[:, :d].astype(F32) + bg_ref[0:1, :])
        gb = _sigmoid(gl_ref[:, d:].astype(F32) + bg_ref[1:2, :])
        mg_ref[...] = (ga * br_f + gb * br_s).astype(BF16)
        dbf = (dmerged * ga).astype(BF16)
        dbs = (dmerged * gb).astype(BF16)
        dbf_ref[...] = dbf
        dbs_ref[...] = dbs
        dla = dmerged * br_f * ga * (1.0 - ga)
        dlb = dmerged * br_s * gb * (1.0 - gb)
        dgl_ref[:, :d] = dla.astype(BF16)
        dgl_ref[:, d:] = dlb.astype(BF16)
        dbg_ref[0:1, :] += jnp.sum(dla, axis=0, keepdims=True)
        dbg_ref[1:2, :] += jnp.sum(dlb, axis=0, keepdims=True)
        dof_ref[...] = _dot_nt(dbf, wbf_ref[...]).astype(BF16)
        dos_ref[...] = _dot_nt(dbs, wbs_ref[...]).astype(BF16)

    shp_d = jax.ShapeDtypeStruct((t_len, d), BF16)
    shp_h = jax.ShapeDtypeStruct((t_len, D_BRANCH), BF16)
    return pl.pallas_call(
        body, name="mix_bwd", grid=(t_len // TM,),
        out_shape=(shp_d, shp_d, shp_d, jax.ShapeDtypeStruct((t_len, 2 * d), BF16), shp_h, shp_h,
                   jax.ShapeDtypeStruct((2, d), F32)),
        in_specs=[_row_spec(TM, d), _row_spec(TM, D_BRANCH), _row_spec(TM, D_BRANCH), _row_spec(TM, 2 * d),
                  _const_spec(w_bf.shape), _const_spec(w_bs.shape), _const_spec(w_out.shape), _const_spec(b_gate.shape)],
        out_specs=(_row_spec(TM, d), _row_spec(TM, d), _row_spec(TM, d), _row_spec(TM, 2 * d),
                   _row_spec(TM, D_BRANCH), _row_spec(TM, D_BRANCH), _acc_spec((2, d))),
        compiler_params=_seq_params(),
    )(dx1, o_fox, o_sb, gl, w_bf, w_bs, w_out, b_gate)


def _inproj_bwd(dqk_f, dv_f, dqkv_b, dgl, df, dx1, x, g_mix, w_pad):
    t_len, d = x.shape
    lay, _ = _pad_layout(d)
    slot_w = N_HEADS * HEAD_SLOT

    def body(dqk_ref, dvf_ref, db_ref, dgl_ref, df_ref, dx1_ref, x_ref, g_ref, w_ref, dx_ref, h1_ref, dg_ref):
        @pl.when(pl.program_id(0) == 0)
        def _():
            dg_ref[...] = jnp.zeros_like(dg_ref)

        def back(piece, name):
            lo, hi = lay[name]
            return _dot(piece, w_ref[lo:hi, :])

        xn, r = _rms(x_ref[...])
        h1_ref[...] = (xn * g_ref[...]).astype(BF16)
        dh = (back(df_ref[...], "forget") + back(dgl_ref[...], "gates") + back(dqk_ref[0], "qf")
              + back(dqk_ref[1], "kf") + back(dvf_ref[...], "vf") + back(db_ref[0], "qb") + back(db_ref[1], "kb")
              + back(db_ref[2], "vb"))
        dres, dg = _rms_bwd(dh, xn, r, g_ref[...])
        dg_ref[...] += dg
        dx_ref[...] = dx1_ref[...] + dres

    return pl.pallas_call(
        body, name="inproj_bwd", grid=(t_len // TM,),
        out_shape=(jax.ShapeDtypeStruct((t_len, d), F32), jax.ShapeDtypeStruct((t_len, d), BF16),
                   jax.ShapeDtypeStruct((1, d), F32)),
        in_specs=[_row3_spec(2, TM, slot_w), _row_spec(TM, D_BRANCH), _row3_spec(3, TM, D_BRANCH),
                  _row_spec(TM, 2 * d), _row_spec(TM, LANES), _row_spec(TM, d), _row_spec(TM, d), _const_spec((1, d)),
                  _const_spec(w_pad.shape)],
        out_specs=(_row_spec(TM, d), _row_spec(TM, d), _acc_spec((1, d))),
        compiler_params=_seq_params(),
    )(dqk_f, dv_f, dqkv_b, dgl, df, dx1, x, g_mix, w_pad)


def _cols_to_slabs(full):
    r, c8 = full.shape
    return full.reshape(r, N_DEV, c8 // N_DEV).transpose(1, 0, 2)


def _slabs_to_cols(slabs):
    n, r, c = slabs.shape
    return slabs.transpose(1, 0, 2).reshape(r, n * c)


def _win_sizes(d):
    return (D_BRANCH, D_BRANCH, D_BRANCH, N_HEADS, D_BRANCH, D_BRANCH, D_BRANCH, d, d)


def _split_win(w_t, d):
    out, off = [], 0
    for s in _win_sizes(d):
        out.append(w_t[off:off + s])
        off += s
    return out


def _to_slots(w_t):
    c = w_t.shape[1]
    return jnp.pad(w_t.reshape(N_HEADS, HEAD_DIM, c), ((0, 0), (0, HEAD_SLOT - HEAD_DIM), (0, 0))).reshape(-1, c)


def _from_slots(w_t):
    c = w_t.shape[1]
    return w_t.reshape(N_HEADS, HEAD_SLOT, c)[:, :HEAD_DIM].reshape(N_HEADS * HEAD_DIM, c)


def _pad_win(w_full_t, d):
    qa, ka, va, fa, qb, kb, vb, ga, gb = _split_win(w_full_t, d)
    scale = HEAD_DIM ** -0.5
    fpad = jnp.pad(fa, ((0, LANES - N_HEADS), (0, 0)))
    return jnp.concatenate([_to_slots(qa * scale), _to_slots(ka), va, qb * scale, kb, vb, ga, gb, fpad], axis=0)


def _unpad_dwin(dqk_f, dv_f, dqkv_b, dgates, dforget, d):
    scale = HEAD_DIM ** -0.5
    return jnp.concatenate([_from_slots(dqk_f[0]) * scale, _from_slots(dqk_f[1]), dv_f, dforget[:N_HEADS],
                            dqkv_b[0] * scale, dqkv_b[1], dqkv_b[2], dgates], axis=0)


def _c_lane_constants():
    head = jnp.arange(LANES)[:, None]
    lane = jnp.arange(N_HEADS * HEAD_SLOT)[None, :]
    in_head = (lane // HEAD_SLOT == head) & (head < N_HEADS)

    def place(first):
        return jnp.stack([(in_head & (lane % HEAD_SLOT == first + j)) for j in range(3)]).astype(BF16)

    def ones(first):
        off = lane % HEAD_SLOT
        return ((off >= first) & (off < first + 3)).astype(F32)

    return place(C_TERMS_Q), place(C_TERMS_K), ones(C_ONES_Q), ones(C_ONES_K)


def _pad_rows(a, rows):
    return jnp.pad(a, [(0, 0)] * (a.ndim - 2) + [(0, rows - a.shape[-2]), (0, 0)])


def kernel(x, p, g_mix, w_in, b_forget, b_gate, w_branch_fox, w_branch_sb, w_out, g_mlp, w_up, w_down, g_ple, w_ple_gate, w_ple, g_final, loss_target, m_g_mix, m_w_in, m_b_forget, m_b_gate, m_w_branch_fox, m_w_branch_sb, m_w_out, m_g_mlp, m_w_up, m_w_down, m_g_ple, m_w_ple_gate, m_w_ple, m_g_final, v_g_mix, v_w_in, v_b_forget, v_b_gate, v_w_branch_fox, v_w_branch_sb, v_w_out, v_g_mlp, v_w_up, v_w_down, v_g_ple, v_w_ple_gate, v_w_ple, v_g_final):
    batch, seq, d = x.shape
    t_len = batch * seq
    d_ple = p.shape[-1]
    d_ff = w_up.shape[-1] * N_DEV
    dn = d // N_DEV
    fn = d_ff // N_DEV
    my_c = lax.axis_index("c")
    my_dev = 4 * lax.axis_index("x") + 2 * lax.axis_index("y") + my_c

    bg_hi = b_gate[0].astype(BF16)
    bg_r = b_gate[0] - bg_hi.astype(F32)
    bg_mid = bg_r.astype(BF16)
    bg_lo = (bg_r - bg_mid.astype(F32)).astype(BF16)
    narrow_rows = 2 * D_BRANCH + d_ple + 6
    narrow_rows_pad = -(-narrow_rows // 16) * 16
    narrow = _pad_rows(jnp.concatenate(
        [w_branch_fox[0].astype(BF16), w_branch_sb[0].astype(BF16), w_ple[0].astype(BF16), bg_hi, bg_mid, bg_lo],
        axis=0), narrow_rows_pad)
    g_in, = _all_gather([w_in[0].T.astype(BF16)])
    w_pad = _pad_win(g_in.reshape(-1, d), d)
    bf_pad = jnp.pad(b_forget, ((0, 0), (0, LANES - N_HEADS)))
    place_q, place_k, ones_q, ones_k = _c_lane_constants()

    x2d = x.reshape(t_len, d)
    p2d = p.reshape(t_len, d_ple)
    tgt2d = loss_target.reshape(t_len, d)
    qf, kf, kft, vf, vft, qkvb, kbt, vbt, gl, fpre = _inproj_fwd(
        x2d, g_mix, w_pad, bf_pad, place_q, place_k, ones_q, ones_k, seq)
    o_sb, ltot, (g_up, g_out, g_down, g_pg, g_narrow) = _sb_fwd(qkvb, vbt, batch, seq, [
        w_up[0].astype(BF16), w_out[0].astype(BF16), w_down[0].astype(BF16), w_ple_gate[0].astype(BF16), narrow])
    o_fox, lse = _fox_fwd(qf, kf, vft, batch, seq)
    w_up_full = _slabs_to_cols(g_up)
    w_out_full = g_out.reshape(d, d)
    w_down_full = g_down.reshape(d_ff, d)
    w_pg_full = g_pg.reshape(d, d)
    w_bf_full = _slabs_to_cols(g_narrow[:, :D_BRANCH])
    w_bs_full = _slabs_to_cols(g_narrow[:, D_BRANCH:2 * D_BRANCH])
    w_ple_full = _slabs_to_cols(g_narrow[:, 2 * D_BRANCH:2 * D_BRANCH + d_ple])
    bg_terms = g_narrow[:, 2 * D_BRANCH + d_ple:narrow_rows].astype(F32)
    b_gate_full = _slabs_to_cols(bg_terms[:, 0:2] + bg_terms[:, 2:4] + bg_terms[:, 4:6])
    x1 = _mix_fwd(o_fox, o_sb, gl, x2d, w_bf_full, w_bs_full, w_out_full, b_gate_full)
    a_up, x2 = _mlp_fwd(x1, g_mlp, w_up_full, w_down_full)

    dx2, h3, dpre, dpe, loss_acc, dg_ple, dg_final = _head_fwd_bwd(
        x2, p2d, tgt2d, g_ple, g_final.reshape(1, d), w_pg_full, w_ple_full)
    dx1, da_up, h2t, dg_mlp = _mlp_bwd(dx2, a_up, x1, g_mlp, w_up_full, w_down_full)
    merged, dbr_f, dbr_s, dgl, do_fox, do_sb, dbg = _mix_bwd(
        dx1, o_fox, o_sb, gl, w_bf_full, w_bs_full, w_out_full, b_gate_full)

    def column_shards(name, lhs, rhs, lhs_t=False):
        if (rhs.shape[-1] // N_DEV) % (4 * LANES) == 0:
            return _matmul_tn(name, lhs, rhs, slabs=True, lhs_t=lhs_t)
        return _cols_to_slabs(_matmul_tn(name, lhs, rhs, lhs_t=lhs_t))

    if fn % (4 * LANES) == 0:
        part_up, = _matmul_tn_once("dw_up", [h2t], da_up, slabs=True, lhs_t=True)
    else:
        part_up = column_shards("dw_up", h2t, da_up, lhs_t=True)
    part_out = _matmul_tn("dw_out", merged, dx1).reshape(N_DEV, dn, d)
    part_down = _matmul_tn_once("dw_down", [a_up], dx2, relu2=True)[0].reshape(N_DEV, fn, d)
    part_pg = _matmul_tn("dw_ple_gate", h3, dpre).reshape(N_DEV, dn, d)
    part_narrow = _pad_rows(jnp.concatenate(
        [column_shards("dw_branch_fox", o_fox, dbr_f), column_shards("dw_branch_sb", o_sb, dbr_s),
         column_shards("dw_ple", p2d, dpe)], axis=1), narrow_rows_pad)
    early = [part_up, part_out, part_down, part_pg, part_narrow]
    early_sums = [_pair_add("pair_add_%d" % i, pt, rc, my_c)
                  for i, (pt, rc) in enumerate(zip(early, _rs_core_pair("reduce_scatter_core_pair_early", early)))]

    dqk_f, dv_f, dc_queries, dc_keys = _fox_bwd(qf, kf, kft, vf, o_fox, do_fox, lse, batch, seq)
    dqkv_b, (s_up, s_out, s_down, s_pg, s_narrow) = _sb_bwd(qkvb, kbt, do_sb, ltot, batch, seq, early_sums)
    dcq_tok = dc_queries.reshape(batch, N_HEADS, seq).transpose(0, 2, 1).reshape(t_len, N_HEADS)
    dck_tok = dc_keys[..., :NH].transpose(0, 2, 1, 3).reshape(t_len, N_HEADS)
    lane_pad = ((0, 0), (0, LANES - N_HEADS))
    df, db_forget = _forget_bwd(jnp.pad(dcq_tok, lane_pad), jnp.pad(dck_tok, lane_pad), fpre, batch, seq)
    grad_x, h1, dg_mix = _inproj_bwd(dqk_f, dv_f, dqkv_b, dgl, df, dx1, x2d, g_mix, w_pad)

    gw_in = _unpad_dwin(*_matmul_tn_once("dw_in_fox_qk", [dqk_f], h1),
                        *_matmul_tn_once("dw_in_rest", [dv_f, dqkv_b, dgl, df], h1), d)
    part_in = gw_in.reshape(N_DEV, -1, d)
    recv_in, = _rs_core_pair("reduce_scatter_core_pair_w_in", [part_in])
    s_in, = _rs_chips([_pair_add("pair_add_w_in", part_in, recv_in, my_c)])

    small = jnp.concatenate([
        dg_mix, dg_mlp, dg_ple, dg_final, jnp.pad(db_forget[:, :N_HEADS], ((0, 0), (0, d - N_HEADS))), dbg,
        jnp.pad(loss_acc[:, :1], ((0, 0), (0, d - 1)))], axis=0)
    small = _all_reduce_small(small)
    loss = small[7, 0]
    small_grads = {
        "g_mix": small[0:1], "g_mlp": small[1:2], "g_ple": small[2:3], "g_final": small[3:4],
        "b_forget": small[4:5, :N_HEADS],
        "b_gate": lax.dynamic_slice_in_dim(small[5:7], my_dev * dn, dn, axis=1),
    }

    weights = {"g_mix": g_mix, "w_in": w_in, "b_forget": b_forget, "b_gate": b_gate, "w_branch_fox": w_branch_fox,
               "w_branch_sb": w_branch_sb, "w_out": w_out, "g_mlp": g_mlp, "w_up": w_up, "w_down": w_down,
               "g_ple": g_ple, "w_ple_gate": w_ple_gate, "w_ple": w_ple, "g_final": g_final}
    m_in = {"g_mix": m_g_mix, "w_in": m_w_in, "b_forget": m_b_forget, "b_gate": m_b_gate,
            "w_branch_fox": m_w_branch_fox, "w_branch_sb": m_w_branch_sb, "w_out": m_w_out, "g_mlp": m_g_mlp,
            "w_up": m_w_up, "w_down": m_w_down, "g_ple": m_g_ple, "w_ple_gate": m_w_ple_gate, "w_ple": m_w_ple,
            "g_final": m_g_final}
    v_in = {"g_mix": v_g_mix, "w_in": v_w_in, "b_forget": v_b_forget, "b_gate": v_b_gate,
            "w_branch_fox": v_w_branch_fox, "w_branch_sb": v_w_branch_sb, "w_out": v_w_out, "g_mlp": v_g_mlp,
            "w_up": v_w_up, "w_down": v_w_down, "g_ple": v_g_ple, "w_ple_gate": v_w_ple_gate, "w_ple": v_w_ple,
            "g_final": v_g_final}
    names = list(weights)

    def as2d(a):
        return a.reshape(-1, a.shape[-1])

    result = {}
    big = {"w_up": (s_up, 0), "w_out": (s_out, 0), "w_down": (s_down, 0), "w_ple_gate": (s_pg, 0),
           "w_branch_fox": (s_narrow, 0), "w_branch_sb": (s_narrow, D_BRANCH), "w_ple": (s_narrow, 2 * D_BRANCH)}
    for n, (parts, off) in big.items():
        result[n] = _adamw_parts("adamw_" + n, as2d(weights[n]), parts, off, as2d(m_in[n]), as2d(v_in[n]))
    result["w_in"] = tuple(r.T for r in _adamw_parts("adamw_w_in", w_in[0].T, s_in, 0, m_w_in[0].T, v_w_in[0].T))
    small_names = list(small_grads)
    small_out = _adamw_small([(as2d(weights[n]), small_grads[n], as2d(m_in[n]), as2d(v_in[n])) for n in small_names])
    for n, (dlt, nm, nv) in zip(small_names, small_out):
        result[n] = (small_grads[n], dlt, nm, nv)
    outs = [[result[n][k].reshape(weights[n].shape) for n in names] for k in range(4)]
    return (loss, grad_x.reshape(x.shape), *outs[0], *outs[1], *outs[2], *outs[3])
```

```python
import jax
import jax.numpy as jnp
from jax import lax
from jax.experimental import pallas as pl
from jax.experimental.pallas import tpu as pltpu

F32 = jnp.float32
BF16 = jnp.bfloat16

HEAD_DIM = 64
N_HEADS = 8
D_BRANCH = N_HEADS * HEAD_DIM
EPS = 1e-6
ADAM_LR = 0.001
ADAM_B1 = 0.9
ADAM_B2 = 0.999
ADAM_EPS = 1e-08
ADAM_WD = 0.01
ADAM_STEP = 10

N_DEV = 8
LANES = 128
TM = 256
TQ = 256
TK = 256
NH = 4
HEAD_SLOT = 128
C_TERMS_Q = 64
C_ONES_K = 64
C_TERMS_K = 67
C_ONES_Q = 67
NEG = -1e30
VMEM_LIMIT = 56 * 1024 * 1024
MESH = pl.DeviceIdType.MESH


def _dot(a, b):
    return jnp.dot(a, b, preferred_element_type=F32)


def _dot_nt(a, b):
    return lax.dot_general(a, b, (((1,), (1,)), ((), ())), preferred_element_type=F32)


def _dot_tn(a, b):
    return lax.dot_general(a, b, (((0,), (0,)), ((), ())), preferred_element_type=F32)


def _sigmoid(x):
    return 1.0 / (1.0 + jnp.exp(-x))


def _softplus(x):
    return jnp.maximum(x, 0.0) + jnp.log(1.0 + jnp.exp(-jnp.abs(x)))


def _split2(x):
    hi = x.astype(BF16)
    lo = (x - hi.astype(F32)).astype(BF16)
    return hi, lo


def _split3(x):
    hi = x.astype(BF16)
    r = x - hi.astype(F32)
    mid = r.astype(BF16)
    lo = (r - mid.astype(F32)).astype(BF16)
    return hi, mid, lo


def _rows_dot_mask(x, mask_bf16):
    hi, lo = _split2(x)
    return _dot(hi, mask_bf16) + _dot(lo, mask_bf16)


def _tri(n, rel):
    r = lax.broadcasted_iota(jnp.int32, (n, n), 0)
    c = lax.broadcasted_iota(jnp.int32, (n, n), 1)
    return rel(r, c).astype(BF16)


def _rms(x):
    r = lax.rsqrt(jnp.mean(x * x, axis=-1, keepdims=True) + EPS)
    return x * r, r


def _rms_bwd(dh, xn, r, g):
    dxn = dh * g
    dx = r * (dxn - xn * jnp.mean(dxn * xn, axis=-1, keepdims=True))
    return dx, jnp.sum(dh * xn, axis=0, keepdims=True)


def _row_spec(tm, cols):
    return pl.BlockSpec((tm, cols), lambda i: (i, 0))


def _row3_spec(g, tm, cols):
    return pl.BlockSpec((g, tm, cols), lambda i: (0, i, 0))


def _col_spec(rows, tm):
    return pl.BlockSpec((rows, tm), lambda i: (0, i))


def _const_spec(shape):
    nd = len(shape)
    return pl.BlockSpec(shape, lambda i: (0,) * nd, pipeline_mode=pl.Buffered(1))


def _acc_spec(shape):
    nd = len(shape)
    return pl.BlockSpec(shape, lambda i: (0,) * nd)


def _seq_params():
    return pltpu.CompilerParams(dimension_semantics=("arbitrary",), vmem_limit_bytes=VMEM_LIMIT)


def _mesh_pos():
    return lax.axis_index("x"), lax.axis_index("y"), lax.axis_index("c")


def _other_chips(x, y):
    return [(1 - x, y), (x, 1 - y), (1 - x, 1 - y)]


def _hbm_specs(n):
    return [pl.BlockSpec(memory_space=pl.ANY)] * n


def _gather_plan(x_refs, out_refs, send_sems, recv_sems, local_sems):
    n = len(x_refs)
    x, y, c = _mesh_pos()
    me, sibling = (x, y, c), (x, y, 1 - c)
    chips = _other_chips(x, y)

    def index(px, py, pc):
        return 4 * px + 2 * py + pc

    def copy(a, k, block, to, src=None):
        slab = out_refs[a].at[index(*block)]
        return pltpu.make_async_remote_copy(
            src_ref=slab if src is None else src, dst_ref=slab,
            send_sem=send_sems.at[7 * a + k], recv_sem=recv_sems.at[7 * a + k], device_id=to, device_id_type=MESH)

    mine = [pltpu.make_async_copy(x_refs[a], out_refs[a].at[index(*me)], local_sems.at[a]) for a in range(n)]
    first = []
    for a in range(n):
        first.append(copy(a, 0, me, sibling, src=x_refs[a]))
        first += [copy(a, 1 + j, me, (cx, cy, c), src=x_refs[a]) for j, (cx, cy) in enumerate(chips)]

    def start():
        for cp in mine + first:
            cp.start()

    def finish():
        passed = []
        for j, (cx, cy) in enumerate(chips):
            for a in range(n):
                copy(a, 1 + j, (cx, cy, c), me).wait_recv()
                passed.append(copy(a, 4 + j, (cx, cy, c), sibling))
                passed[-1].start()
        for a in range(n):
            copy(a, 0, sibling, me).wait_recv()
            for j, (cx, cy) in enumerate(chips):
                copy(a, 4 + j, (cx, cy, 1 - c), me).wait_recv()
        for cp in first + passed:
            cp.wait_send()
        for cp in mine:
            cp.wait()

    return start, finish


def _gather_shapes(shards):
    return [jax.ShapeDtypeStruct((N_DEV,) + s.shape, s.dtype) for s in shards]


def _gather_sems(n):
    return [pltpu.SemaphoreType.DMA((7 * n,)), pltpu.SemaphoreType.DMA((7 * n,)), pltpu.SemaphoreType.DMA((n,))]


def _all_gather(shards):
    n = len(shards)

    def body(*refs):
        start, finish = _gather_plan(refs[:n], refs[n:2 * n], *refs[2 * n:])
        start()
        finish()

    return pl.pallas_call(
        body, name="all_gather_weights", out_shape=_gather_shapes(shards),
        in_specs=_hbm_specs(n), out_specs=_hbm_specs(n), scratch_shapes=_gather_sems(n),
    )(*shards)


def _pair_plan(p_refs, recv_refs, send_sems, recv_sems):
    n = len(p_refs)
    x, y, c = _mesh_pos()
    sibling = (x, y, 1 - c)

    def start():
        for a in range(n):
            for chip in range(4):
                pltpu.make_async_remote_copy(
                    src_ref=p_refs[a].at[2 * chip + (1 - c)], dst_ref=recv_refs[a].at[chip],
                    send_sem=send_sems.at[a], recv_sem=recv_sems.at[a], device_id=sibling, device_id_type=MESH).start()

    def finish():
        for a in range(n):
            pltpu.make_async_remote_copy(
                src_ref=recv_refs[a], dst_ref=recv_refs[a], send_sem=send_sems.at[a], recv_sem=recv_sems.at[a],
                device_id=sibling, device_id_type=MESH).wait()

    return start, finish


def _pair_shapes(partials):
    return [jax.ShapeDtypeStruct((4,) + s.shape[1:], s.dtype) for s in partials]


def _pair_sems(n):
    return [pltpu.SemaphoreType.DMA((n,)), pltpu.SemaphoreType.DMA((n,))]


def _rs_core_pair(name, partials):
    n = len(partials)

    def body(*refs):
        start, finish = _pair_plan(refs[:n], refs[n:2 * n], *refs[2 * n:])
        start()
        finish()

    return pl.pallas_call(
        body, name=name, out_shape=_pair_shapes(partials),
        in_specs=_hbm_specs(n), out_specs=_hbm_specs(n), scratch_shapes=_pair_sems(n),
    )(*partials)


def _chips_plan(cs_refs, out_refs, send_sems, recv_sems, local_sems):
    n = len(cs_refs)
    x, y, c = _mesh_pos()
    chip = 2 * x + y
    chips = _other_chips(x, y)
    mine = [pltpu.make_async_copy(cs_refs[a].at[chip], out_refs[a].at[chip], local_sems.at[a]) for a in range(n)]
    sends = [pltpu.make_async_remote_copy(
        src_ref=cs_refs[a].at[2 * cx + cy], dst_ref=out_refs[a].at[chip],
        send_sem=send_sems.at[3 * a + j], recv_sem=recv_sems.at[3 * a + j],
        device_id=(cx, cy, c), device_id_type=MESH) for a in range(n) for j, (cx, cy) in enumerate(chips)]

    def start():
        for cp in mine + sends:
            cp.start()

    def finish():
        for a in range(n):
            for j, (cx, cy) in enumerate(chips):
                pltpu.make_async_remote_copy(
                    src_ref=cs_refs[a].at[chip], dst_ref=out_refs[a].at[2 * cx + cy],
                    send_sem=send_sems.at[3 * a + j], recv_sem=recv_sems.at[3 * a + j],
                    device_id=(x, y, c), device_id_type=MESH).wait_recv()
        for cp in sends:
            cp.wait_send()
        for cp in mine:
            cp.wait()

    return start, finish


def _chips_sems(n):
    return [pltpu.SemaphoreType.DMA((3 * n,)), pltpu.SemaphoreType.DMA((3 * n,)), pltpu.SemaphoreType.DMA((n,))]


def _all_reduce_small(vec):
    rows, cols = vec.shape

    def body(x_ref, land_ref, sum_ref, send_sems, recv_sems):
        x, y, c = _mesh_pos()
        me = 4 * x + 2 * y + c
        land_ref[me] = x_ref[...]
        flips = [(fx, fy, fc) for fx in (0, 1) for fy in (0, 1) for fc in (0, 1)][1:]

        def flipped(f):
            return tuple((1 - v) if b else v for v, b in zip((x, y, c), f))

        sends = []
        for k, f in enumerate(flips):
            sends.append(pltpu.make_async_remote_copy(
                src_ref=x_ref, dst_ref=land_ref.at[me], send_sem=send_sems.at[k], recv_sem=recv_sems.at[k],
                device_id=flipped(f), device_id_type=MESH))
            sends[-1].start()
        for k, f in enumerate(flips):
            px, py, pc = flipped(f)
            pltpu.make_async_remote_copy(
                src_ref=x_ref, dst_ref=land_ref.at[4 * px + 2 * py + pc], send_sem=send_sems.at[k],
                recv_sem=recv_sems.at[k], device_id=(x, y, c), device_id_type=MESH).wait_recv()
        for cp in sends:
            cp.wait_send()
        total = land_ref[0]
        for d in range(1, N_DEV):
            total = total + land_ref[d]
        sum_ref[...] = total

    vm = pl.BlockSpec(memory_space=pltpu.VMEM)
    return pl.pallas_call(
        body, name="all_reduce_small",
        out_shape=(jax.ShapeDtypeStruct((N_DEV, rows, cols), F32), jax.ShapeDtypeStruct((rows, cols), F32)),
        in_specs=[vm], out_specs=(vm, vm),
        scratch_shapes=[pltpu.SemaphoreType.DMA((7,)), pltpu.SemaphoreType.DMA((7,))],
    )(vec)[1]


def _block_rows(rows, cols, itemsize, align, row_off=0):
    best = None
    for t in range(align, rows + 1, align):
        if rows % t == 0 and row_off % t == 0 and t * cols * itemsize <= (1 << 20):
            best = t
    return rows if best is None else best


def _pair_add(name, partial, recv, my_c):
    _, rows, cols = partial.shape
    br = _block_rows(rows, cols, 2, 16)

    def body(c_ref, a_ref, b_ref, o_ref):
        o_ref[...] = (a_ref[...].astype(F32) + b_ref[...].astype(F32)).astype(BF16)

    return pl.pallas_call(
        body, name=name,
        grid_spec=pltpu.PrefetchScalarGridSpec(
            num_scalar_prefetch=1, grid=(4, rows // br),
            in_specs=[pl.BlockSpec((None, None, br, cols), lambda j, i, c_ref: (j, c_ref[0], i, 0)),
                      pl.BlockSpec((None, br, cols), lambda j, i, c_ref: (j, i, 0))],
            out_specs=pl.BlockSpec((None, br, cols), lambda j, i, c_ref: (j, i, 0))),
        out_shape=jax.ShapeDtypeStruct((4, rows, cols), BF16),
    )(my_c.reshape(1).astype(jnp.int32), partial.reshape(4, 2, rows, cols), recv)


def _adam_update(w, g, m, v):
    nm = ADAM_B1 * m + (1.0 - ADAM_B1) * g
    nv = ADAM_B2 * v + (1.0 - ADAM_B2) * (g * g)
    m_hat = nm / (1.0 - ADAM_B1 ** ADAM_STEP)
    v_hat = nv / (1.0 - ADAM_B2 ** ADAM_STEP)
    return -ADAM_LR * (m_hat / (jnp.sqrt(v_hat) + ADAM_EPS) + ADAM_WD * w), nm, nv


def _adamw_parts(name, w, parts, row_off, m, v):
    rows, cols = w.shape
    tr = _block_rows(rows, cols, 4, 16, row_off)
    tc = cols
    if tr == rows and rows % 16 != 0 and cols % (2 * LANES) == 0:
        tc = 2 * LANES
    assert rows % tr == 0 and row_off % tr == 0 and (tc == cols or row_off == 0)
    off = row_off // tr

    def body(w_ref, p_ref, m_ref, v_ref, g_ref, d_ref, nm_ref, nv_ref):
        g = p_ref[0].astype(F32)
        for j in range(1, 4):
            g = g + p_ref[j].astype(F32)
        g_ref[...] = g
        d_ref[...], nm_ref[...], nv_ref[...] = _adam_update(w_ref[...], g, m_ref[...], v_ref[...])

    spec = pl.BlockSpec((tr, tc), lambda i, j: (i, j))
    shp = jax.ShapeDtypeStruct((rows, cols), F32)
    return pl.pallas_call(
        body, name=name, grid=(rows // tr, cols // tc), out_shape=(shp,) * 4,
        in_specs=[spec, pl.BlockSpec((4, tr, tc), lambda i, j: (0, off + i, j)), spec, spec], out_specs=(spec,) * 4,
    )(w, parts, m, v)


def _adamw_small(tensors):
    n = len(tensors)

    def body(*refs):
        ins, outs = refs[:4 * n], refs[4 * n:]
        for t in range(n):
            w_ref, g_ref, m_ref, v_ref = ins[4 * t:4 * t + 4]
            d, nm, nv = _adam_update(w_ref[...], g_ref[...], m_ref[...], v_ref[...])
            outs[3 * t][...], outs[3 * t + 1][...], outs[3 * t + 2][...] = d, nm, nv

    vm = pl.BlockSpec(memory_space=pltpu.VMEM)
    out = pl.pallas_call(
        body, name="adamw_small",
        out_shape=[jax.ShapeDtypeStruct(t[0].shape, F32) for t in tensors for _ in range(3)],
        in_specs=[vm] * (4 * n), out_specs=[vm] * (3 * n),
    )(*[a for t in tensors for a in t])
    return [tuple(out[3 * t:3 * t + 3]) for t in range(n)]


def _matmul_tn(name, a, b, relu2=False, slabs=False, lhs_t=False):
    a_groups = a.shape[0] if a.ndim == 3 else 0
    b_groups = b.shape[0] if b.ndim == 3 else 0
    groups = max(a_groups, b_groups, 1)
    assert not (a_groups and b_groups) and not (a_groups and lhs_t)
    a3 = a if a_groups else a[None]
    b3 = b if b_groups else b[None]
    t_len, k_len = a3.shape[1:][::-1] if lhs_t else a3.shape[1:]
    n_len = b3.shape[2]
    tt = min(t_len, 512)
    tk = min(k_len, 1024)
    tn = n_len // N_DEV if slabs else min(n_len, 1024)
    nt = t_len // tt
    assert not slabs or (groups == 1 and tn <= 1024)

    def body(a_ref, b_ref, o_ref, acc_ref):
        @pl.when(pl.program_id(3) == 0)
        def _():
            acc_ref[...] = jnp.zeros_like(acc_ref)

        av = a_ref[...]
        if relu2:
            av = jnp.square(jnp.maximum(av.astype(F32), 0.0))
        product = _dot if lhs_t else _dot_tn
        acc_ref[...] += product(av.astype(BF16), b_ref[...].astype(BF16))

        @pl.when(pl.program_id(3) == nt - 1)
        def _():
            o_ref[...] = acc_ref[...].astype(BF16)

    def a_group(g):
        return g if a_groups else 0

    def b_group(g):
        return g if b_groups else 0

    if slabs:
        out_shape = jax.ShapeDtypeStruct((N_DEV, k_len, tn), BF16)
        out_spec = pl.BlockSpec((None, tk, tn), lambda g, i, j, t: (j, i, 0))
    else:
        out_shape = jax.ShapeDtypeStruct((groups, k_len, n_len), BF16)
        out_spec = pl.BlockSpec((None, tk, tn), lambda g, i, j, t: (g, i, j))
    out = pl.pallas_call(
        body, name=name, grid=(groups, k_len // tk, n_len // tn, nt), out_shape=out_shape,
        in_specs=[pl.BlockSpec((None, tk, tt), lambda g, i, j, t: (a_group(g), i, t)) if lhs_t
                  else pl.BlockSpec((None, tt, tk), lambda g, i, j, t: (a_group(g), t, i)),
                  pl.BlockSpec((None, tt, tn), lambda g, i, j, t: (b_group(g), t, j))],
        out_specs=out_spec,
        scratch_shapes=[pltpu.VMEM((tk, tn), F32)],
        compiler_params=pltpu.CompilerParams(
            dimension_semantics=("parallel", "parallel", "parallel", "arbitrary"), vmem_limit_bytes=VMEM_LIMIT),
    )(a3, b3)
    return out if (slabs or a_groups or b_groups) else out[0]


def _matmul_tn_once(name, lhs_list, rhs, relu2=False, slabs=False, lhs_t=False):
    t_len, n_len = rhs.shape
    tt = min(t_len, 256)
    nt = t_len // tt
    n_lhs = len(lhs_list)
    assert not (lhs_t or slabs) or (n_lhs == 1 and lhs_list[0].ndim == 2)
    k_shapes = [(a.shape[0], n_len) if lhs_t else a.shape[:-2] + (a.shape[-1], n_len) for a in lhs_list]
    tn = n_len // N_DEV

    def body(*refs):
        a_refs, b_ref = refs[:n_lhs], refs[n_lhs]
        o_refs, acc_refs = refs[n_lhs + 1:2 * n_lhs + 1], refs[2 * n_lhs + 1:]
        step = pl.program_id(0)

        @pl.when(step == 0)
        def _():
            for acc in acc_refs:
                acc[...] = jnp.zeros_like(acc)

        bv = b_ref[...].astype(BF16)

        def piece(av):
            if relu2:
                av = jnp.square(jnp.maximum(av.astype(F32), 0.0))
            return (_dot if lhs_t else _dot_tn)(av.astype(BF16), bv)

        for a_ref, acc in zip(a_refs, acc_refs):
            if len(acc.shape) == 3:
                for g in range(acc.shape[0]):
                    acc[g] += piece(a_ref[g])
            else:
                acc[...] += piece(a_ref[...])

        @pl.when(step == nt - 1)
        def _():
            for o_ref, acc in zip(o_refs, acc_refs):
                if slabs:
                    for j in range(N_DEV):
                        o_ref[j] = acc[:, j * tn:(j + 1) * tn].astype(BF16)
                else:
                    o_ref[...] = acc[...].astype(BF16)

    def lhs_spec(a):
        if lhs_t:
            return pl.BlockSpec((a.shape[0], tt), lambda t: (0, t))
        if a.ndim == 3:
            return pl.BlockSpec((a.shape[0], tt, a.shape[2]), lambda t: (0, t, 0))
        return pl.BlockSpec((tt, a.shape[1]), lambda t: (t, 0))

    out_shapes = [(N_DEV, k_shapes[0][0], tn)] if slabs else k_shapes
    return pl.pallas_call(
        body, name=name, grid=(nt,),
        out_shape=[jax.ShapeDtypeStruct(s, BF16) for s in out_shapes],
        in_specs=[lhs_spec(a) for a in lhs_list] + [pl.BlockSpec((tt, n_len), lambda t: (t, 0))],
        out_specs=[_acc_spec(s) for s in out_shapes],
        scratch_shapes=[pltpu.VMEM(s, F32) for s in k_shapes],
        compiler_params=_seq_params(),
    )(*lhs_list, rhs)


def _pad_layout(d):
    names = ("qf", "kf", "vf", "qb", "kb", "vb", "gates", "forget")
    sizes = (N_HEADS * HEAD_SLOT, N_HEADS * HEAD_SLOT, D_BRANCH, D_BRANCH, D_BRANCH, D_BRANCH, 2 * d, LANES)
    out, off = {}, 0
    for n, s in zip(names, sizes):
        out[n] = (off, off + s)
        off += s
    return out, off


def _slot_rows(xt, extra):
    parts = []
    for h in range(N_HEADS):
        parts += [xt[h * HEAD_DIM:(h + 1) * HEAD_DIM, :], extra]
    return jnp.concatenate(parts, axis=0)


def _inproj_fwd(x, g_mix, w_pad, bf_pad, place_q, place_k, ones_q, ones_k, seq):
    t_len, d = x.shape
    lay, _ = _pad_layout(d)
    tiles_per_seq = seq // TM
    slot_w = N_HEADS * HEAD_SLOT

    def body(x_ref, g_ref, w_ref, bf_ref, pq_ref, pk_ref, oq_ref, ok_ref,
             qf_ref, kf_ref, kft_ref, vf_ref, vft_ref, qkvb_ref, kbt_ref, vbt_ref, gl_ref, fpre_ref, h_ref,
             carry_ref):
        @pl.when(pl.program_id(0) % tiles_per_seq == 0)
        def _():
            carry_ref[...] = jnp.zeros_like(carry_ref)

        def proj(name):
            lo, hi = lay[name]
            return _dot_nt(h, w_ref[lo:hi, :])

        xn, _ = _rms(x_ref[...])
        h = (xn * g_ref[...]).astype(BF16)
        fpre = proj("forget") + bf_ref[...]
        fpre_ref[...] = fpre
        logf = -_softplus(-fpre)
        lower = _tri(TM, lambda r, c: c <= r)
        hi, mid, lo = _split3(logf)
        c_val = carry_ref[...] + _dot(lower, hi) + _dot(lower, mid) + _dot(lower, lo)
        carry_ref[...] = carry_ref[...] + jnp.sum(logf, axis=0, keepdims=True)
        c3 = _split3(c_val)
        qf_ref[...] = (proj("qf") + sum(_dot(c3[j], pq_ref[j]) for j in range(3)) + oq_ref[...]).astype(BF16)
        kf = proj("kf") - sum(_dot(c3[j], pk_ref[j]) for j in range(3)) + ok_ref[...]
        kf_ref[...] = kf.astype(BF16)
        kft_ref[0] = kf.T.astype(BF16)
        row0 = (lax.broadcasted_iota(jnp.int32, (HEAD_DIM, TM), 0) == 0).astype(F32)
        zeros = jnp.zeros((HEAD_DIM, TM), F32)
        vf = proj("vf")
        vf_ref[...] = vf.astype(BF16)
        vft_ref[0] = _slot_rows(vf.T, row0).astype(BF16)
        qkvb_ref[0] = proj("qb").astype(BF16)
        kb = proj("kb")
        qkvb_ref[1] = kb.astype(BF16)
        kbt_ref[0] = _slot_rows(kb.T, zeros).astype(BF16)
        vb = proj("vb")
        qkvb_ref[2] = vb.astype(BF16)
        vbt_ref[0] = _slot_rows(vb.T, row0).astype(BF16)
        gl_ref[...] = proj("gates").astype(BF16)
        h_ref[...] = h

    n_tiles = t_len // TM
    slot_shape = jax.ShapeDtypeStruct((t_len, slot_w), BF16)
    t_shape = jax.ShapeDtypeStruct((n_tiles, slot_w, TM), BF16)
    t_spec = pl.BlockSpec((1, slot_w, TM), lambda i: (i, 0, 0))
    return pl.pallas_call(
        body, name="inproj_fwd", grid=(n_tiles,),
        out_shape=(slot_shape, slot_shape, t_shape, jax.ShapeDtypeStruct((t_len, D_BRANCH), BF16), t_shape,
                   jax.ShapeDtypeStruct((3, t_len, D_BRANCH), BF16), t_shape, t_shape,
                   jax.ShapeDtypeStruct((t_len, 2 * d), BF16), jax.ShapeDtypeStruct((t_len, LANES), F32),
                   jax.ShapeDtypeStruct((t_len, d), BF16)),
        in_specs=[_row_spec(TM, d), _const_spec((1, d)), _const_spec(w_pad.shape), _const_spec((1, LANES)),
                  _const_spec(place_q.shape), _const_spec(place_k.shape), _const_spec((1, slot_w)),
                  _const_spec((1, slot_w))],
        out_specs=(_row_spec(TM, slot_w), _row_spec(TM, slot_w), t_spec, _row_spec(TM, D_BRANCH), t_spec,
                   _row3_spec(3, TM, D_BRANCH), t_spec, t_spec, _row_spec(TM, 2 * d), _row_spec(TM, LANES),
                   _row_spec(TM, d)),
        scratch_shapes=[pltpu.VMEM((1, LANES), F32)],
        compiler_params=_seq_params(),
    )(x, g_mix, w_pad, bf_pad, place_q, place_k, ones_q, ones_k)


def _slot_spec(seq):
    return pl.BlockSpec((seq, NH * HEAD_SLOT), lambda b, g: (b, g))


def _slot2_spec(seq):
    return pl.BlockSpec((2, seq, NH * HEAD_SLOT), lambda b, g: (0, b, g))


def _group_spec(seq):
    return pl.BlockSpec((seq, NH * HEAD_DIM), lambda b, g: (b, g))


def _group3_spec(which, seq):
    return pl.BlockSpec((None, seq, NH * HEAD_DIM), lambda b, g: (which, b, g))


def _tblock_spec(seq):
    return pl.BlockSpec((seq // TK, NH * HEAD_SLOT, TK), lambda b, g: (b, g, 0))


def _qrow_spec(seq):
    return pl.BlockSpec((None, NH, seq // TQ, TQ), lambda b, g: (b, g, 0, 0))


def _stat_spec(seq):
    return pl.BlockSpec((None, None, seq, LANES), lambda b, g: (b, g, 0, 0))


def _attn_params():
    return pltpu.CompilerParams(dimension_semantics=("parallel", "parallel"), vmem_limit_bytes=VMEM_LIMIT)


def _serial_attn_params():
    return pltpu.CompilerParams(dimension_semantics=("arbitrary", "arbitrary"), vmem_limit_bytes=VMEM_LIMIT)


def _hcols(hh):
    return slice(hh * HEAD_DIM, (hh + 1) * HEAD_DIM)


def _hslot(hh):
    return slice(hh * HEAD_SLOT, (hh + 1) * HEAD_SLOT)


def _lane(hh):
    return slice(hh, hh + 1)


def _key_query_mask(rel):
    r = lax.broadcasted_iota(jnp.int32, (TK, TQ), 0)
    c = lax.broadcasted_iota(jnp.int32, (TK, TQ), 1)
    return rel(r, c)


def _heads_cat(vals):
    return jnp.concatenate(vals, axis=1)


def _untranspose(acc_t):
    return acc_t.T[:, :HEAD_DIM]


def _fox_fwd(qf, kf, vft, batch, seq):
    def body(q_ref, k_ref, vt_ref, o_ref, lse_ref, m_s, acc_s):
        causal = _key_query_mask(lambda r, c: r <= c)

        def tile(q0, kj, masked):
            krows = pl.ds(pl.multiple_of(kj * TK, TK), TK)
            heads = range(NH)
            sts = [_dot_nt(k_ref[krows, _hslot(hh)], q_ref[pl.ds(q0, TQ), _hslot(hh)]) for hh in heads]
            if masked:
                sts = [jnp.where(causal, st, NEG) for st in sts]
            m_olds = [m_s[hh] for hh in heads]
            m_news = [jnp.maximum(m_olds[hh], jnp.max(sts[hh], axis=0, keepdims=True)) for hh in heads]
            pts = [jnp.exp(sts[hh] - m_news[hh]).astype(BF16) for hh in heads]
            pvs = [_dot(vt_ref[kj, _hslot(hh), :], pts[hh]) for hh in heads]
            for hh in heads:
                acc_s[hh] = jnp.exp(m_olds[hh] - m_news[hh]) * acc_s[hh] + pvs[hh]
                m_s[hh] = m_news[hh]

        def q_loop(qi, _):
            q0 = pl.multiple_of(qi * TQ, TQ)
            m_s[...] = jnp.full(m_s.shape, NEG, F32)
            acc_s[...] = jnp.zeros_like(acc_s)

            def k_loop(kj, _):
                tile(q0, kj, False)
                return 0

            lax.fori_loop(0, qi, k_loop, 0)
            tile(q0, qi, True)
            outs = []
            for hh in range(NH):
                total = acc_s[hh, HEAD_DIM:HEAD_DIM + 1, :]
                outs.append(_untranspose(acc_s[hh] / total))
                lse_ref[hh, pl.ds(qi, 1), :] = m_s[hh] + jnp.log(total)
            o_ref[pl.ds(q0, TQ), :] = _heads_cat(outs).astype(BF16)
            return 0

        lax.fori_loop(0, seq // TQ, q_loop, 0)

    return pl.pallas_call(
        body, name="fox_fwd", grid=(batch, N_HEADS // NH),
        out_shape=(jax.ShapeDtypeStruct((batch * seq, D_BRANCH), BF16),
                   jax.ShapeDtypeStruct((batch, N_HEADS, seq // TQ, TQ), F32)),
        in_specs=[_slot_spec(seq), _slot_spec(seq), _tblock_spec(seq)],
        out_specs=(_group_spec(seq), _qrow_spec(seq)),
        scratch_shapes=[pltpu.VMEM((NH, 1, TQ), F32), pltpu.VMEM((NH, HEAD_SLOT, TQ), F32)],
        compiler_params=_attn_params(),
    )(qf, kf, vft)


def _fox_bwd(qf, kf, kft, vf, o, do, lse, batch, seq, partials):
    n_q = seq // TQ
    n = len(partials)

    def body(q_ref, k_ref, kt_ref, v_ref, o_ref, do_ref, lse_ref, *rest):
        p_refs, (dqk_ref, dv_ref, dcq_ref, dck_ref), recv_refs = rest[:n], rest[n:n + 4], rest[n + 4:2 * n + 4]
        delta_s, dqt_acc, dk_s, dv_s = rest[2 * n + 4:2 * n + 8]
        pair_start, pair_finish = _pair_plan(p_refs, recv_refs, *rest[2 * n + 8:])
        first_step, last_step = _first_last_step()
        pl.when(first_step)(pair_start)
        causal = _key_query_mask(lambda r, c: r <= c)
        ones8 = jnp.ones((8, HEAD_DIM), BF16)
        dqt_acc[...] = jnp.zeros_like(dqt_acc)

        def prep(qi, _):
            rows = pl.ds(pl.multiple_of(qi * TQ, TQ), TQ)
            for hh in range(NH):
                hi, lo = _split2(do_ref[rows, _hcols(hh)].astype(F32) * o_ref[rows, _hcols(hh)].astype(F32))
                delta_s[hh, pl.ds(qi, 1), :] = (_dot_nt(ones8, hi) + _dot_nt(ones8, lo))[0:1, :]
            return 0

        lax.fori_loop(0, n_q, prep, 0)

        def tile(qi, kj, masked):
            rows = pl.ds(pl.multiple_of(qi * TQ, TQ), TQ)
            krows = pl.ds(pl.multiple_of(kj * TK, TK), TK)
            heads = range(NH)
            qs = [q_ref[rows, _hslot(hh)] for hh in heads]
            douts = [do_ref[rows, _hcols(hh)] for hh in heads]
            sts = [_dot_nt(k_ref[krows, _hslot(hh)], qs[hh]) for hh in heads]
            dps = [_dot_nt(v_ref[krows, _hcols(hh)], douts[hh]) for hh in heads]
            pts = [jnp.exp(sts[hh] - lse_ref[hh, pl.ds(qi, 1), :]) for hh in heads]
            if masked:
                pts = [jnp.where(causal, pt, 0.0) for pt in pts]
            dsts = [(pts[hh] * (dps[hh] - delta_s[hh, pl.ds(qi, 1), :])).astype(BF16) for hh in heads]
            for hh in heads:
                dv_s[hh] += _dot(pts[hh].astype(BF16), douts[hh])
                dk_s[hh] += _dot(dsts[hh], qs[hh])
                dqt_acc[hh, qi] += _dot(kt_ref[kj, _hslot(hh), :], dsts[hh])

        def k_loop(kj, _):
            krows = pl.ds(pl.multiple_of(kj * TK, TK), TK)
            dk_s[...] = jnp.zeros_like(dk_s)
            dv_s[...] = jnp.zeros_like(dv_s)
            tile(kj, kj, True)

            def q_loop(qi, _):
                tile(qi, kj, False)
                return 0

            lax.fori_loop(kj + 1, n_q, q_loop, 0)
            dqk_ref[1, krows, :] = _heads_cat([dk_s[hh] for hh in range(NH)]).astype(BF16)
            dv_ref[krows, :] = _heads_cat([dv_s[hh] for hh in range(NH)]).astype(BF16)
            for hh in range(NH):
                dck_ref[krows, _lane(hh)] = dk_s[hh, :, C_ONES_Q:C_ONES_Q + 1]
            return 0

        lax.fori_loop(0, seq // TK, k_loop, 0)

        def finish(qi, _):
            rows = pl.ds(pl.multiple_of(qi * TQ, TQ), TQ)
            dqk_ref[0, rows, :] = _heads_cat([dqt_acc[hh, qi].T for hh in range(NH)]).astype(BF16)
            for hh in range(NH):
                dcq_ref[hh, pl.ds(qi, 1), :] = dqt_acc[hh, qi, C_ONES_K:C_ONES_K + 1, :]
            return 0

        lax.fori_loop(0, n_q, finish, 0)
        pl.when(last_step)(pair_finish)

    out = pl.pallas_call(
        body, name="fox_bwd", grid=(batch, N_HEADS // NH),
        out_shape=[jax.ShapeDtypeStruct((2, batch * seq, N_HEADS * HEAD_SLOT), BF16),
                   jax.ShapeDtypeStruct((batch * seq, D_BRANCH), BF16),
                   jax.ShapeDtypeStruct((batch, N_HEADS, seq // TQ, TQ), F32),
                   jax.ShapeDtypeStruct((batch, N_HEADS // NH, seq, LANES), F32)] + _pair_shapes(partials),
        in_specs=[_slot_spec(seq), _slot_spec(seq), _tblock_spec(seq), _group_spec(seq), _group_spec(seq),
                  _group_spec(seq), _qrow_spec(seq)] + _hbm_specs(n),
        out_specs=[_slot2_spec(seq), _group_spec(seq), _qrow_spec(seq), _stat_spec(seq)] + _hbm_specs(n),
        scratch_shapes=[pltpu.VMEM((NH, n_q, TQ), F32), pltpu.VMEM((NH, n_q, HEAD_SLOT, TQ), F32),
                        pltpu.VMEM((NH, TK, HEAD_SLOT), F32), pltpu.VMEM((NH, TK, HEAD_DIM), F32)] + _pair_sems(n),
        compiler_params=_serial_attn_params(),
    )(qf, kf, kft, vf, o, do, lse, *partials)
    return out[0], out[1], out[2], out[3], out[4:]


def _first_last_step():
    step = pl.program_id(0) * pl.num_programs(1) + pl.program_id(1)
    return step == 0, step == pl.num_programs(0) * pl.num_programs(1) - 1


def _sb_fwd(qkvb, vbt, batch, seq, shards):
    n = len(shards)

    def body(q_ref, k_ref, vt_ref, *rest):
        x_refs, (o_ref, lt_ref), out_refs = rest[:n], rest[n:n + 2], rest[n + 2:2 * n + 2]
        run_s, acc_s = rest[2 * n + 2:2 * n + 4]
        gather_start, gather_finish = _gather_plan(x_refs, out_refs, *rest[2 * n + 4:])
        first_step, last_step = _first_last_step()
        pl.when(first_step)(gather_start)
        strict = _key_query_mask(lambda r, c: r < c)
        later = _tri(TK, lambda r, c: c > r)

        def tile(q0, kj, masked):
            krows = pl.ds(pl.multiple_of(kj * TK, TK), TK)
            heads = range(NH)
            zts = [_dot_nt(k_ref[krows, _hcols(hh)], q_ref[pl.ds(q0, TQ), _hcols(hh)]) for hh in heads]
            lgs = [-_softplus(zt) for zt in zts]
            if masked:
                lgs = [jnp.where(strict, lg, 0.0) for lg in lgs]
            parts = [_split2(lg) for lg in lgs]
            sufs = [_dot(later, hi) + _dot(later, lo) for hi, lo in parts]
            ats = [jnp.exp(zts[hh] + lgs[hh] + run_s[hh] + sufs[hh]) for hh in heads]
            if masked:
                ats = [jnp.where(strict, at, 0.0) for at in ats]
            for hh in heads:
                acc_s[hh] += _dot(vt_ref[kj, _hslot(hh), :], ats[hh].astype(BF16))
                run_s[hh] += jnp.sum(lgs[hh], axis=0, keepdims=True)

        def q_loop(qi, _):
            q0 = pl.multiple_of(qi * TQ, TQ)
            run_s[...] = jnp.zeros_like(run_s)
            acc_s[...] = jnp.zeros_like(acc_s)
            tile(q0, qi, True)

            def k_loop(kk, _):
                tile(q0, qi - 1 - kk, False)
                return 0

            lax.fori_loop(0, qi, k_loop, 0)
            o_ref[pl.ds(q0, TQ), :] = _heads_cat([_untranspose(acc_s[hh]) for hh in range(NH)]).astype(BF16)
            for hh in range(NH):
                lt_ref[hh, pl.ds(qi, 1), :] = run_s[hh]
            return 0

        lax.fori_loop(0, seq // TQ, q_loop, 0)
        pl.when(last_step)(gather_finish)

    out = pl.pallas_call(
        body, name="sb_fwd", grid=(batch, N_HEADS // NH),
        out_shape=[jax.ShapeDtypeStruct((batch * seq, D_BRANCH), BF16),
                   jax.ShapeDtypeStruct((batch, N_HEADS, seq // TQ, TQ), F32)] + _gather_shapes(shards),
        in_specs=[_group3_spec(0, seq), _group3_spec(1, seq), _tblock_spec(seq)] + _hbm_specs(n),
        out_specs=[_group_spec(seq), _qrow_spec(seq)] + _hbm_specs(n),
        scratch_shapes=[pltpu.VMEM((NH, 1, TQ), F32), pltpu.VMEM((NH, HEAD_SLOT, TQ), F32)] + _gather_sems(n),
        compiler_params=_serial_attn_params(),
    )(qkvb, qkvb, vbt, *shards)
    return out[0], out[1], out[2:]


def _sb_bwd(qkvb, kbt, do, ltot, batch, seq, chip_sums):
    n = len(chip_sums)

    def body(q_ref, k_ref, v_ref, kt_ref, do_ref, lt_ref, *rest):
        cs_refs, dqkv_ref, out_refs = rest[:n], rest[n], rest[n + 1:2 * n + 1]
        dk_acc, dv_acc, ls_s, gs_s, dqt_s = rest[2 * n + 1:2 * n + 6]
        chips_start, chips_finish = _chips_plan(cs_refs, out_refs, *rest[2 * n + 6:])
        first_step, last_step = _first_last_step()
        pl.when(first_step)(chips_start)
        strict = _key_query_mask(lambda r, c: r < c)
        upto = _tri(TK, lambda r, c: c <= r)
        before = _tri(TK, lambda r, c: c < r)
        dk_acc[...] = jnp.zeros_like(dk_acc)
        dv_acc[...] = jnp.zeros_like(dv_acc)

        def tile(qi, kj, masked):
            rows = pl.ds(pl.multiple_of(qi * TQ, TQ), TQ)
            krows = pl.ds(pl.multiple_of(kj * TK, TK), TK)
            heads = range(NH)
            qs = [q_ref[rows, _hcols(hh)] for hh in heads]
            douts = [do_ref[rows, _hcols(hh)] for hh in heads]
            zts = [_dot_nt(k_ref[krows, _hcols(hh)], qs[hh]) for hh in heads]
            das = [_dot_nt(v_ref[krows, _hcols(hh)], douts[hh]) for hh in heads]
            lgs = [-_softplus(zt) for zt in zts]
            if masked:
                lgs = [jnp.where(strict, lg, 0.0) for lg in lgs]
            parts = [_split2(lg) for lg in lgs]
            prefs = [_dot(upto, hi) + _dot(upto, lo) for hi, lo in parts]
            ats = [jnp.exp(zts[hh] + lgs[hh] + (lt_ref[hh, pl.ds(qi, 1), :] - ls_s[hh]) - prefs[hh]) for hh in heads]
            if masked:
                ats = [jnp.where(strict, at, 0.0) for at in ats]
            gts = [das[hh] * ats[hh] for hh in heads]
            us = [gs_s[hh] + _dot(before, gts[hh].astype(BF16)) for hh in heads]
            dzts = [(jnp.exp(lgs[hh]) * (gts[hh] + us[hh]) - us[hh]).astype(BF16) for hh in heads]
            for hh in heads:
                dk_acc[hh, krows, :] += _dot(dzts[hh], qs[hh])
                dv_acc[hh, krows, :] += _dot(ats[hh].astype(BF16), douts[hh])
                dqt_s[hh] += _dot(kt_ref[kj, _hslot(hh), :], dzts[hh])
                ls_s[hh] += jnp.sum(lgs[hh], axis=0, keepdims=True)
                gs_s[hh] += jnp.sum(gts[hh], axis=0, keepdims=True)

        def q_loop(qi, _):
            ls_s[...] = jnp.zeros_like(ls_s)
            gs_s[...] = jnp.zeros_like(gs_s)
            dqt_s[...] = jnp.zeros_like(dqt_s)

            def k_loop(kj, _):
                tile(qi, kj, False)
                return 0

            lax.fori_loop(0, qi, k_loop, 0)
            tile(qi, qi, True)
            dqkv_ref[0, pl.ds(pl.multiple_of(qi * TQ, TQ), TQ), :] = _heads_cat(
                [_untranspose(dqt_s[hh]) for hh in range(NH)]).astype(BF16)
            return 0

        lax.fori_loop(0, seq // TQ, q_loop, 0)
        dqkv_ref[1] = _heads_cat([dk_acc[hh] for hh in range(NH)]).astype(BF16)
        dqkv_ref[2] = _heads_cat([dv_acc[hh] for hh in range(NH)]).astype(BF16)
        pl.when(last_step)(chips_finish)

    out = pl.pallas_call(
        body, name="sb_bwd", grid=(batch, N_HEADS // NH),
        out_shape=[jax.ShapeDtypeStruct((3, batch * seq, D_BRANCH), BF16)]
        + [jax.ShapeDtypeStruct(s.shape, s.dtype) for s in chip_sums],
        in_specs=[_group3_spec(0, seq), _group3_spec(1, seq), _group3_spec(2, seq), _tblock_spec(seq),
                  _group_spec(seq), _qrow_spec(seq)] + _hbm_specs(n),
        out_specs=[pl.BlockSpec((3, seq, NH * HEAD_DIM), lambda b, g: (0, b, g))] + _hbm_specs(n),
        scratch_shapes=[pltpu.VMEM((NH, seq, HEAD_DIM), F32), pltpu.VMEM((NH, seq, HEAD_DIM), F32),
                        pltpu.VMEM((NH, 1, TQ), F32), pltpu.VMEM((NH, 1, TQ), F32),
                        pltpu.VMEM((NH, HEAD_SLOT, TQ), F32)] + _chips_sems(n),
        compiler_params=_serial_attn_params(),
    )(qkvb, qkvb, qkvb, kbt, do, ltot, *chip_sums)
    return out[0], out[1:]


def _forget_bwd(dcq_tok, dck_tok, fpre, batch, seq):
    t_len = batch * seq
    tiles = seq // TM

    def rev(i):
        return ((i // tiles) * tiles + (tiles - 1 - i % tiles), 0)

    def body(dcq_ref, dck_ref, f_ref, df_ref, db_ref, carry_ref):
        i = pl.program_id(0)

        @pl.when(i == 0)
        def _():
            db_ref[...] = jnp.zeros_like(db_ref)

        @pl.when(i % tiles == 0)
        def _():
            carry_ref[...] = jnp.zeros_like(carry_ref)

        dc = dcq_ref[...] - dck_ref[...]
        upper = _tri(TM, lambda r, c: c >= r)
        hi, mid, lo = _split3(dc)
        dlogf = carry_ref[...] + _dot(upper, hi) + _dot(upper, mid) + _dot(upper, lo)
        carry_ref[...] = carry_ref[...] + jnp.sum(dc, axis=0, keepdims=True)
        df = dlogf * _sigmoid(-f_ref[...])
        df_ref[...] = df.astype(BF16)
        db_ref[...] += jnp.sum(df, axis=0, keepdims=True)

    return pl.pallas_call(
        body, name="forget_bwd", grid=(t_len // TM,),
        out_shape=(jax.ShapeDtypeStruct((t_len, LANES), BF16), jax.ShapeDtypeStruct((1, LANES), F32)),
        in_specs=[pl.BlockSpec((TM, LANES), rev)] * 3,
        out_specs=(pl.BlockSpec((TM, LANES), rev), _acc_spec((1, LANES))),
        scratch_shapes=[pltpu.VMEM((1, LANES), F32)],
        compiler_params=_seq_params(),
    )(dcq_tok, dck_tok, fpre)


def _mix_fwd(o_fox, o_sb, gl, x, w_bf, w_bs, w_out, b_gate):
    t_len, d = x.shape

    def body(of_ref, os_ref, gl_ref, x_ref, wbf_ref, wbs_ref, wo_ref, bg_ref, x1_ref):
        br_f = _dot(of_ref[...], wbf_ref[...])
        br_s = _dot(os_ref[...], wbs_ref[...])
        ga = _sigmoid(gl_ref[:, :d].astype(F32) + bg_ref[0:1, :])
        gb = _sigmoid(gl_ref[:, d:].astype(F32) + bg_ref[1:2, :])
        merged = ga * br_f + gb * br_s
        x1_ref[...] = x_ref[...] + _dot(merged.astype(BF16), wo_ref[...])

    return pl.pallas_call(
        body, name="mix_fwd", grid=(t_len // TM,),
        out_shape=jax.ShapeDtypeStruct((t_len, d), F32),
        in_specs=[_row_spec(TM, D_BRANCH), _row_spec(TM, D_BRANCH), _row_spec(TM, 2 * d), _row_spec(TM, d),
                  _const_spec(w_bf.shape), _const_spec(w_bs.shape), _const_spec(w_out.shape), _const_spec(b_gate.shape)],
        out_specs=_row_spec(TM, d),
        compiler_params=_seq_params(),
    )(o_fox, o_sb, gl, x, w_bf, w_bs, w_out, b_gate)


def _ff_chunk(d_ff):
    return min(d_ff, 1024)


def _mlp_fwd(x1, g_mlp, w_up, w_down):
    t_len, d = x1.shape
    d_ff = w_up.shape[1]
    ch = _ff_chunk(d_ff)

    def body(x1_ref, g_ref, wu_ref, wd_ref, a_ref, x2_ref):
        x1v = x1_ref[...]
        xn, _ = _rms(x1v)
        h = (xn * g_ref[...]).astype(BF16)
        acc = x1v
        for j in range(d_ff // ch):
            a = _dot(h, wu_ref[:, j * ch:(j + 1) * ch])
            a_ref[:, j * ch:(j + 1) * ch] = a.astype(BF16)
            acc = acc + _dot(jnp.square(jnp.maximum(a, 0.0)).astype(BF16), wd_ref[j * ch:(j + 1) * ch, :])
        x2_ref[...] = acc

    return pl.pallas_call(
        body, name="mlp_fwd", grid=(t_len // TM,),
        out_shape=(jax.ShapeDtypeStruct((t_len, d_ff), BF16), jax.ShapeDtypeStruct((t_len, d), F32)),
        in_specs=[_row_spec(TM, d), _const_spec((1, d)), _const_spec(w_up.shape), _const_spec(w_down.shape)],
        out_specs=(_row_spec(TM, d_ff), _row_spec(TM, d)),
        compiler_params=_seq_params(),
    )(x1, g_mlp, w_up, w_down)


def _head_fwd_bwd(x2, p, target, g_ple, g_final, w_pg, w_ple):
    t_len, d = x2.shape
    d_ple = p.shape[1]

    def body(x2_ref, p_ref, t_ref, gp_ref, gf_ref, wpg_ref, wple_ref,
             dx2_ref, h3_ref, dpre_ref, dpe_ref, loss_ref, dgp_ref, dgf_ref):
        @pl.when(pl.program_id(0) == 0)
        def _():
            loss_ref[...] = jnp.zeros_like(loss_ref)
            dgp_ref[...] = jnp.zeros_like(dgp_ref)
            dgf_ref[...] = jnp.zeros_like(dgf_ref)

        x2v = x2_ref[...]
        x2n, r3 = _rms(x2v)
        h3 = (x2n * gp_ref[...]).astype(BF16)
        h3_ref[...] = h3
        gate = _sigmoid(_dot(h3, wpg_ref[...]))
        pe = _dot(p_ref[...].astype(BF16), wple_ref[...])
        x3n, r4 = _rms(x2v + gate * pe)
        err = x3n * gf_ref[...] - t_ref[...]
        loss_ref[...] += jnp.full(loss_ref.shape, (0.5 / d) * jnp.sum(err * err), F32)
        dx3, dgf = _rms_bwd(err * (1.0 / d), x3n, r4, gf_ref[...])
        dgf_ref[...] += dgf
        dpe_ref[...] = (dx3 * gate).astype(BF16)
        dpre = (dx3 * pe * gate * (1.0 - gate)).astype(BF16)
        dpre_ref[...] = dpre
        dres, dgp = _rms_bwd(_dot_nt(dpre, wpg_ref[...]), x2n, r3, gp_ref[...])
        dgp_ref[...] += dgp
        dx2_ref[...] = dx3 + dres

    shp_b = jax.ShapeDtypeStruct((t_len, d), BF16)
    return pl.pallas_call(
        body, name="head_fwd_bwd", grid=(t_len // TM,),
        out_shape=(jax.ShapeDtypeStruct((t_len, d), F32), shp_b, shp_b, shp_b,
                   jax.ShapeDtypeStruct((1, LANES), F32), jax.ShapeDtypeStruct((1, d), F32),
                   jax.ShapeDtypeStruct((1, d), F32)),
        in_specs=[_row_spec(TM, d), _row_spec(TM, d_ple), _row_spec(TM, d), _const_spec((1, d)), _const_spec((1, d)),
                  _const_spec(w_pg.shape), _const_spec(w_ple.shape)],
        out_specs=(_row_spec(TM, d), _row_spec(TM, d), _row_spec(TM, d), _row_spec(TM, d),
                   _acc_spec((1, LANES)), _acc_spec((1, d)), _acc_spec((1, d))),
        compiler_params=_seq_params(),
    )(x2, p, target, g_ple, g_final, w_pg, w_ple)


def _mlp_bwd(dx2, a, x1, g_mlp, w_up, w_down):
    t_len, d = x1.shape
    d_ff = w_up.shape[1]
    ch = _ff_chunk(d_ff)

    def body(dx2_ref, a_ref, x1_ref, g_ref, wu_ref, wd_ref, dx1_ref, da_ref, h2_ref, dg_ref):
        @pl.when(pl.program_id(0) == 0)
        def _():
            dg_ref[...] = jnp.zeros_like(dg_ref)

        dx2v = dx2_ref[...]
        dx2b = dx2v.astype(BF16)
        xn, r = _rms(x1_ref[...])
        h2_ref[...] = (xn * g_ref[...]).T.astype(BF16)
        dh = jnp.zeros((TM, d), F32)
        for j in range(d_ff // ch):
            dact = _dot_nt(dx2b, wd_ref[j * ch:(j + 1) * ch, :])
            da = (dact * 2.0 * jnp.maximum(a_ref[:, j * ch:(j + 1) * ch].astype(F32), 0.0)).astype(BF16)
            da_ref[:, j * ch:(j + 1) * ch] = da
            dh = dh + _dot_nt(da, wu_ref[:, j * ch:(j + 1) * ch])
        dres, dg = _rms_bwd(dh, xn, r, g_ref[...])
        dg_ref[...] += dg
        dx1_ref[...] = dx2v + dres

    return pl.pallas_call(
        body, name="mlp_bwd", grid=(t_len // TM,),
        out_shape=(jax.ShapeDtypeStruct((t_len, d), F32), jax.ShapeDtypeStruct((t_len, d_ff), BF16),
                   jax.ShapeDtypeStruct((d, t_len), BF16), jax.ShapeDtypeStruct((1, d), F32)),
        in_specs=[_row_spec(TM, d), _row_spec(TM, d_ff), _row_spec(TM, d), _const_spec((1, d)),
                  _const_spec(w_up.shape), _const_spec(w_down.shape)],
        out_specs=(_row_spec(TM, d), _row_spec(TM, d_ff), _col_spec(d, TM), _acc_spec((1, d))),
        compiler_params=_seq_params(),
    )(dx2, a, x1, g_mlp, w_up, w_down)


def _mix_bwd(dx1, o_fox, o_sb, gl, w_bf, w_bs, w_out, b_gate):
    t_len, d = dx1.shape

    def body(dx1_ref, of_ref, os_ref, gl_ref, wbf_ref, wbs_ref, wo_ref, bg_ref,
             mg_ref, dbf_ref, dbs_ref, dgl_ref, dof_ref, dos_ref, dbg_ref):
        @pl.when(pl.program_id(0) == 0)
        def _():
            dbg_ref[...] = jnp.zeros_like(dbg_ref)

        dmerged = _dot_nt(dx1_ref[...].astype(BF16), wo_ref[...])
        br_f = _dot(of_ref[...], wbf_ref[...])
        br_s = _dot(os_ref[...], wbs_ref[...])
        ga = _sigmoid(gl_ref[:, :d].astype(F32) + bg_ref[0:1, :])
        gb = _sigmoid(gl_ref[:, d:].astype(F32) + bg_ref[1:2, :])
        mg_ref[...] = (ga * br_f + gb * br_s).astype(BF16)
        dbf = (dmerged * ga).astype(BF16)
        dbs = (dmerged * gb).astype(BF16)
        dbf_ref[...] = dbf
        dbs_ref[...] = dbs
        dla = dmerged * br_f * ga * (1.0 - ga)
        dlb = dmerged * br_s * gb * (1.0 - gb)
        dgl_ref[:, :d] = dla.astype(BF16)
        dgl_ref[:, d:] = dlb.astype(BF16)
        dbg_ref[0:1, :] += jnp.sum(dla, axis=0, keepdims=True)
        dbg_ref[1:2, :] += jnp.sum(dlb, axis=0, keepdims=True)
        dof_ref[...] = _dot_nt(dbf, wbf_ref[...]).astype(BF16)
        dos_ref[...] = _dot_nt(dbs, wbs_ref[...]).astype(BF16)

    shp_d = jax.ShapeDtypeStruct((t_len, d), BF16)
    shp_h = jax.ShapeDtypeStruct((t_len, D_BRANCH), BF16)
    return pl.pallas_call(
        body, name="mix_bwd", grid=(t_len // TM,),
        out_shape=(shp_d, shp_d, shp_d, jax.ShapeDtypeStruct((t_len, 2 * d), BF16), shp_h, shp_h,
                   jax.ShapeDtypeStruct((2, d), F32)),
        in_specs=[_row_spec(TM, d), _row_spec(TM, D_BRANCH), _row_spec(TM, D_BRANCH), _row_spec(TM, 2 * d),
                  _const_spec(w_bf.shape), _const_spec(w_bs.shape), _const_spec(w_out.shape), _const_spec(b_gate.shape)],
        out_specs=(_row_spec(TM, d), _row_spec(TM, d), _row_spec(TM, d), _row_spec(TM, 2 * d),
                   _row_spec(TM, D_BRANCH), _row_spec(TM, D_BRANCH), _acc_spec((2, d))),
        compiler_params=_seq_params(),
    )(dx1, o_fox, o_sb, gl, w_bf, w_bs, w_out, b_gate)


def _inproj_bwd(dqk_f, dv_f, dqkv_b, dgl, df, dx1, x, g_mix, w_pad, chip_sums):
    t_len, d = x.shape
    lay, _ = _pad_layout(d)
    slot_w = N_HEADS * HEAD_SLOT
    n = len(chip_sums)
    n_tiles = t_len // TM

    def body(dqk_ref, dvf_ref, db_ref, dgl_ref, df_ref, dx1_ref, x_ref, g_ref, w_ref, *rest):
        cs_refs, (dx_ref, dg_ref), out_refs = rest[:n], rest[n:n + 2], rest[n + 2:2 * n + 2]
        chips_start, chips_finish = _chips_plan(cs_refs, out_refs, *rest[2 * n + 2:])

        @pl.when(pl.program_id(0) == 0)
        def _():
            dg_ref[...] = jnp.zeros_like(dg_ref)
            chips_start()

        def back(piece, name):
            lo, hi = lay[name]
            return _dot(piece, w_ref[lo:hi, :])

        xn, r = _rms(x_ref[...])
        dh = (back(df_ref[...], "forget") + back(dgl_ref[...], "gates") + back(dqk_ref[0], "qf")
              + back(dqk_ref[1], "kf") + back(dvf_ref[...], "vf") + back(db_ref[0], "qb") + back(db_ref[1], "kb")
              + back(db_ref[2], "vb"))
        dres, dg = _rms_bwd(dh, xn, r, g_ref[...])
        dg_ref[...] += dg
        dx_ref[...] = dx1_ref[...] + dres
        pl.when(pl.program_id(0) == n_tiles - 1)(chips_finish)

    out = pl.pallas_call(
        body, name="inproj_bwd", grid=(n_tiles,),
        out_shape=[jax.ShapeDtypeStruct((t_len, d), F32), jax.ShapeDtypeStruct((1, d), F32)]
        + [jax.ShapeDtypeStruct(s.shape, s.dtype) for s in chip_sums],
        in_specs=[_row3_spec(2, TM, slot_w), _row_spec(TM, D_BRANCH), _row3_spec(3, TM, D_BRANCH),
                  _row_spec(TM, 2 * d), _row_spec(TM, LANES), _row_spec(TM, d), _row_spec(TM, d), _const_spec((1, d)),
                  _const_spec(w_pad.shape)] + _hbm_specs(n),
        out_specs=[_row_spec(TM, d), _acc_spec((1, d))] + _hbm_specs(n),
        scratch_shapes=_chips_sems(n),
        compiler_params=_seq_params(),
    )(dqk_f, dv_f, dqkv_b, dgl, df, dx1, x, g_mix, w_pad, *chip_sums)
    return out[0], out[1], out[2:]


def _cols_to_slabs(full):
    r, c8 = full.shape
    return full.reshape(r, N_DEV, c8 // N_DEV).transpose(1, 0, 2)


def _slabs_to_cols(slabs):
    n, r, c = slabs.shape
    return slabs.transpose(1, 0, 2).reshape(r, n * c)


def _win_sizes(d):
    return (D_BRANCH, D_BRANCH, D_BRANCH, N_HEADS, D_BRANCH, D_BRANCH, D_BRANCH, d, d)


def _split_win(w_t, d):
    out, off = [], 0
    for s in _win_sizes(d):
        out.append(w_t[off:off + s])
        off += s
    return out


def _to_slots(w_t):
    c = w_t.shape[1]
    return jnp.pad(w_t.reshape(N_HEADS, HEAD_DIM, c), ((0, 0), (0, HEAD_SLOT - HEAD_DIM), (0, 0))).reshape(-1, c)


def _from_slots(w_t):
    c = w_t.shape[1]
    return w_t.reshape(N_HEADS, HEAD_SLOT, c)[:, :HEAD_DIM].reshape(N_HEADS * HEAD_DIM, c)


def _pad_win(w_full_t, d):
    qa, ka, va, fa, qb, kb, vb, ga, gb = _split_win(w_full_t, d)
    scale = HEAD_DIM ** -0.5
    fpad = jnp.pad(fa, ((0, LANES - N_HEADS), (0, 0)))
    return jnp.concatenate([_to_slots(qa * scale), _to_slots(ka), va, qb * scale, kb, vb, ga, gb, fpad], axis=0)


def _unpad_dwin(dqk_f, dv_f, dqkv_b, dgates, dforget, d):
    scale = HEAD_DIM ** -0.5
    return jnp.concatenate([_from_slots(dqk_f[0]) * scale, _from_slots(dqk_f[1]), dv_f, dforget[:N_HEADS],
                            dqkv_b[0] * scale, dqkv_b[1], dqkv_b[2], dgates], axis=0)


def _c_lane_constants():
    head = jnp.arange(LANES)[:, None]
    lane = jnp.arange(N_HEADS * HEAD_SLOT)[None, :]
    in_head = (lane // HEAD_SLOT == head) & (head < N_HEADS)

    def place(first):
        return jnp.stack([(in_head & (lane % HEAD_SLOT == first + j)) for j in range(3)]).astype(BF16)

    def ones(first):
        off = lane % HEAD_SLOT
        return ((off >= first) & (off < first + 3)).astype(F32)

    return place(C_TERMS_Q), place(C_TERMS_K), ones(C_ONES_Q), ones(C_ONES_K)


def _pad_rows(a, rows):
    return jnp.pad(a, [(0, 0)] * (a.ndim - 2) + [(0, rows - a.shape[-2]), (0, 0)])


def kernel(x, p, g_mix, w_in, b_forget, b_gate, w_branch_fox, w_branch_sb, w_out, g_mlp, w_up, w_down, g_ple, w_ple_gate, w_ple, g_final, loss_target, m_g_mix, m_w_in, m_b_forget, m_b_gate, m_w_branch_fox, m_w_branch_sb, m_w_out, m_g_mlp, m_w_up, m_w_down, m_g_ple, m_w_ple_gate, m_w_ple, m_g_final, v_g_mix, v_w_in, v_b_forget, v_b_gate, v_w_branch_fox, v_w_branch_sb, v_w_out, v_g_mlp, v_w_up, v_w_down, v_g_ple, v_w_ple_gate, v_w_ple, v_g_final):
    batch, seq, d = x.shape
    t_len = batch * seq
    d_ple = p.shape[-1]
    d_ff = w_up.shape[-1] * N_DEV
    dn = d // N_DEV
    fn = d_ff // N_DEV
    my_c = lax.axis_index("c")
    my_dev = 4 * lax.axis_index("x") + 2 * lax.axis_index("y") + my_c

    bg_hi = b_gate[0].astype(BF16)
    bg_r = b_gate[0] - bg_hi.astype(F32)
    bg_mid = bg_r.astype(BF16)
    bg_lo = (bg_r - bg_mid.astype(F32)).astype(BF16)
    narrow_rows = 2 * D_BRANCH + d_ple + 6
    narrow_rows_pad = -(-narrow_rows // 16) * 16
    narrow = _pad_rows(jnp.concatenate(
        [w_branch_fox[0].astype(BF16), w_branch_sb[0].astype(BF16), w_ple[0].astype(BF16), bg_hi, bg_mid, bg_lo],
        axis=0), narrow_rows_pad)
    g_in, = _all_gather([w_in[0].T.astype(BF16)])
    w_pad = _pad_win(g_in.reshape(-1, d), d)
    bf_pad = jnp.pad(b_forget, ((0, 0), (0, LANES - N_HEADS)))
    place_q, place_k, ones_q, ones_k = _c_lane_constants()

    x2d = x.reshape(t_len, d)
    p2d = p.reshape(t_len, d_ple)
    tgt2d = loss_target.reshape(t_len, d)
    qf, kf, kft, vf, vft, qkvb, kbt, vbt, gl, fpre, h1 = _inproj_fwd(
        x2d, g_mix, w_pad, bf_pad, place_q, place_k, ones_q, ones_k, seq)
    o_sb, ltot, (g_up, g_out, g_down, g_pg, g_narrow) = _sb_fwd(qkvb, vbt, batch, seq, [
        w_up[0].astype(BF16), w_out[0].astype(BF16), w_down[0].astype(BF16), w_ple_gate[0].astype(BF16), narrow])
    o_fox, lse = _fox_fwd(qf, kf, vft, batch, seq)
    w_up_full = _slabs_to_cols(g_up)
    w_out_full = g_out.reshape(d, d)
    w_down_full = g_down.reshape(d_ff, d)
    w_pg_full = g_pg.reshape(d, d)
    w_bf_full = _slabs_to_cols(g_narrow[:, :D_BRANCH])
    w_bs_full = _slabs_to_cols(g_narrow[:, D_BRANCH:2 * D_BRANCH])
    w_ple_full = _slabs_to_cols(g_narrow[:, 2 * D_BRANCH:2 * D_BRANCH + d_ple])
    bg_terms = g_narrow[:, 2 * D_BRANCH + d_ple:narrow_rows].astype(F32)
    b_gate_full = _slabs_to_cols(bg_terms[:, 0:2] + bg_terms[:, 2:4] + bg_terms[:, 4:6])
    x1 = _mix_fwd(o_fox, o_sb, gl, x2d, w_bf_full, w_bs_full, w_out_full, b_gate_full)
    a_up, x2 = _mlp_fwd(x1, g_mlp, w_up_full, w_down_full)

    dx2, h3, dpre, dpe, loss_acc, dg_ple, dg_final = _head_fwd_bwd(
        x2, p2d, tgt2d, g_ple, g_final.reshape(1, d), w_pg_full, w_ple_full)
    dx1, da_up, h2t, dg_mlp = _mlp_bwd(dx2, a_up, x1, g_mlp, w_up_full, w_down_full)
    merged, dbr_f, dbr_s, dgl, do_fox, do_sb, dbg = _mix_bwd(
        dx1, o_fox, o_sb, gl, w_bf_full, w_bs_full, w_out_full, b_gate_full)

    def column_shards(name, lhs, rhs, lhs_t=False):
        if (rhs.shape[-1] // N_DEV) % (4 * LANES) == 0:
            return _matmul_tn(name, lhs, rhs, slabs=True, lhs_t=lhs_t)
        return _cols_to_slabs(_matmul_tn(name, lhs, rhs, lhs_t=lhs_t))

    if fn % (4 * LANES) == 0:
        part_up, = _matmul_tn_once("dw_up", [h2t], da_up, slabs=True, lhs_t=True)
    else:
        part_up = column_shards("dw_up", h2t, da_up, lhs_t=True)
    part_out = _matmul_tn("dw_out", merged, dx1).reshape(N_DEV, dn, d)
    part_down = _matmul_tn_once("dw_down", [a_up], dx2, relu2=True)[0].reshape(N_DEV, fn, d)
    part_pg = _matmul_tn("dw_ple_gate", h3, dpre).reshape(N_DEV, dn, d)
    part_narrow = _pad_rows(jnp.concatenate(
        [column_shards("dw_branch_fox", o_fox, dbr_f), column_shards("dw_branch_sb", o_sb, dbr_s),
         column_shards("dw_ple", p2d, dpe)], axis=1), narrow_rows_pad)
    early = [part_up, part_out, part_down, part_pg, part_narrow]

    dqk_f, dv_f, dc_queries, dc_keys, early_recv = _fox_bwd(
        qf, kf, kft, vf, o_fox, do_fox, lse, batch, seq, early)
    early_sums = [_pair_add("pair_add_%d" % i, pt, rc, my_c) for i, (pt, rc) in enumerate(zip(early, early_recv))]
    dqkv_b, (s_up, s_out, s_down, s_pg, s_narrow) = _sb_bwd(qkvb, kbt, do_sb, ltot, batch, seq, early_sums)
    dcq_tok = dc_queries.reshape(batch, N_HEADS, seq).transpose(0, 2, 1).reshape(t_len, N_HEADS)
    dck_tok = dc_keys[..., :NH].transpose(0, 2, 1, 3).reshape(t_len, N_HEADS)
    lane_pad = ((0, 0), (0, LANES - N_HEADS))
    df, db_forget = _forget_bwd(jnp.pad(dcq_tok, lane_pad), jnp.pad(dck_tok, lane_pad), fpre, batch, seq)

    gw_in = _unpad_dwin(*_matmul_tn_once("dw_in_fox_qk", [dqk_f], h1),
                        *_matmul_tn_once("dw_in_rest", [dv_f, dqkv_b, dgl, df], h1), d)
    part_in = gw_in.reshape(N_DEV, -1, d)
    recv_in, = _rs_core_pair("reduce_scatter_core_pair_w_in", [part_in])
    grad_x, dg_mix, (s_in,) = _inproj_bwd(dqk_f, dv_f, dqkv_b, dgl, df, dx1, x2d, g_mix, w_pad,
                                          [_pair_add("pair_add_w_in", part_in, recv_in, my_c)])

    small = jnp.concatenate([
        dg_mix, dg_mlp, dg_ple, dg_final, jnp.pad(db_forget[:, :N_HEADS], ((0, 0), (0, d - N_HEADS))), dbg,
        jnp.pad(loss_acc[:, :1], ((0, 0), (0, d - 1)))], axis=0)
    small = _all_reduce_small(small)
    loss = small[7, 0]
    small_grads = {
        "g_mix": small[0:1], "g_mlp": small[1:2], "g_ple": small[2:3], "g_final": small[3:4],
        "b_forget": small[4:5, :N_HEADS],
        "b_gate": lax.dynamic_slice_in_dim(small[5:7], my_dev * dn, dn, axis=1),
    }

    weights = {"g_mix": g_mix, "w_in": w_in, "b_forget": b_forget, "b_gate": b_gate, "w_branch_fox": w_branch_fox,
               "w_branch_sb": w_branch_sb, "w_out": w_out, "g_mlp": g_mlp, "w_up": w_up, "w_down": w_down,
               "g_ple": g_ple, "w_ple_gate": w_ple_gate, "w_ple": w_ple, "g_final": g_final}
    m_in = {"g_mix": m_g_mix, "w_in": m_w_in, "b_forget": m_b_forget, "b_gate": m_b_gate,
            "w_branch_fox": m_w_branch_fox, "w_branch_sb": m_w_branch_sb, "w_out": m_w_out, "g_mlp": m_g_mlp,
            "w_up": m_w_up, "w_down": m_w_down, "g_ple": m_g_ple, "w_ple_gate": m_w_ple_gate, "w_ple": m_w_ple,
            "g_final": m_g_final}
    v_in = {"g_mix": v_g_mix, "w_in": v_w_in, "b_forget": v_b_forget, "b_gate": v_b_gate,
            "w_branch_fox": v_w_branch_fox, "w_branch_sb": v_w_branch_sb, "w_out": v_w_out, "g_mlp": v_g_mlp,
            "w_up": v_w_up, "w_down": v_w_down, "g_ple": v_g_ple, "w_ple_gate": v_w_ple_gate, "w_ple": v_w_ple,
            "g_final": v_g_final}
    names = list(weights)

    def as2d(a):
        return a.reshape(-1, a.shape[-1])

    result = {}
    big = {"w_up": (s_up, 0), "w_out": (s_out, 0), "w_down": (s_down, 0), "w_ple_gate": (s_pg, 0),
           "w_branch_fox": (s_narrow, 0), "w_branch_sb": (s_narrow, D_BRANCH), "w_ple": (s_narrow, 2 * D_BRANCH)}
    for n, (parts, off) in big.items():
        result[n] = _adamw_parts("adamw_" + n, as2d(weights[n]), parts, off, as2d(m_in[n]), as2d(v_in[n]))
    result["w_in"] = tuple(r.T for r in _adamw_parts("adamw_w_in", w_in[0].T, s_in, 0, m_w_in[0].T, v_w_in[0].T))
    small_names = list(small_grads)
    small_out = _adamw_small([(as2d(weights[n]), small_grads[n], as2d(m_in[n]), as2d(v_in[n])) for n in small_names])
    for n, (dlt, nm, nv) in zip(small_names, small_out):
        result[n] = (small_grads[n], dlt, nm, nv)
    outs = [[result[n][k].reshape(weights[n].shape) for n in names] for k in range(4)]
    return (loss, grad_x.reshape(x.shape), *outs[0], *outs[1], *outs[2], *outs[3])
```

```python
import jax
import jax.numpy as jnp
from jax import lax
from jax.experimental import pallas as pl
from jax.experimental.pallas import tpu as pltpu

F32 = jnp.float32
BF16 = jnp.bfloat16

HEAD_DIM = 64
N_HEADS = 8
D_BRANCH = N_HEADS * HEAD_DIM
EPS = 1e-6
ADAM_LR = 0.001
ADAM_B1 = 0.9
ADAM_B2 = 0.999
ADAM_EPS = 1e-08
ADAM_WD = 0.01
ADAM_STEP = 10

N_DEV = 8
LANES = 128
TM = 256
TQ = 256
TK = 256
NH = 4
HEAD_SLOT = 128
C_TERMS_Q = 64
C_ONES_K = 64
C_TERMS_K = 67
C_ONES_Q = 67
NEG = -1e30
VMEM_LIMIT = 56 * 1024 * 1024
MESH = pl.DeviceIdType.MESH


def _dot(a, b):
    return jnp.dot(a, b, preferred_element_type=F32)


def _dot_nt(a, b):
    return lax.dot_general(a, b, (((1,), (1,)), ((), ())), preferred_element_type=F32)


def _dot_tn(a, b):
    return lax.dot_general(a, b, (((0,), (0,)), ((), ())), preferred_element_type=F32)


def _sigmoid(x):
    return 1.0 / (1.0 + jnp.exp(-x))


def _softplus(x):
    return jnp.maximum(x, 0.0) + jnp.log(1.0 + jnp.exp(-jnp.abs(x)))


def _split2(x):
    hi = x.astype(BF16)
    lo = (x - hi.astype(F32)).astype(BF16)
    return hi, lo


def _split3(x):
    hi = x.astype(BF16)
    r = x - hi.astype(F32)
    mid = r.astype(BF16)
    lo = (r - mid.astype(F32)).astype(BF16)
    return hi, mid, lo


def _rows_dot_mask(x, mask_bf16):
    hi, lo = _split2(x)
    return _dot(hi, mask_bf16) + _dot(lo, mask_bf16)


def _tri(n, rel):
    r = lax.broadcasted_iota(jnp.int32, (n, n), 0)
    c = lax.broadcasted_iota(jnp.int32, (n, n), 1)
    return rel(r, c).astype(BF16)


def _rms(x):
    r = lax.rsqrt(jnp.mean(x * x, axis=-1, keepdims=True) + EPS)
    return x * r, r


def _rms_bwd(dh, xn, r, g):
    dxn = dh * g
    dx = r * (dxn - xn * jnp.mean(dxn * xn, axis=-1, keepdims=True))
    return dx, jnp.sum(dh * xn, axis=0, keepdims=True)


def _row_spec(tm, cols):
    return pl.BlockSpec((tm, cols), lambda i: (i, 0))


def _row3_spec(g, tm, cols):
    return pl.BlockSpec((g, tm, cols), lambda i: (0, i, 0))


def _col_spec(rows, tm):
    return pl.BlockSpec((rows, tm), lambda i: (0, i))


def _const_spec(shape):
    nd = len(shape)
    return pl.BlockSpec(shape, lambda i: (0,) * nd, pipeline_mode=pl.Buffered(1))


def _acc_spec(shape):
    nd = len(shape)
    return pl.BlockSpec(shape, lambda i: (0,) * nd)


def _seq_params():
    return pltpu.CompilerParams(dimension_semantics=("arbitrary",), vmem_limit_bytes=VMEM_LIMIT)


def _mesh_pos():
    return lax.axis_index("x"), lax.axis_index("y"), lax.axis_index("c")


def _other_chips(x, y):
    return [(1 - x, y), (x, 1 - y), (1 - x, 1 - y)]


def _hbm_specs(n):
    return [pl.BlockSpec(memory_space=pl.ANY)] * n


def _gather_plan(x_refs, out_refs, send_sems, recv_sems, local_sems):
    n = len(x_refs)
    x, y, c = _mesh_pos()
    me, sibling = (x, y, c), (x, y, 1 - c)
    chips = _other_chips(x, y)

    def index(px, py, pc):
        return 4 * px + 2 * py + pc

    def copy(a, k, block, to, src=None):
        slab = out_refs[a].at[index(*block)]
        return pltpu.make_async_remote_copy(
            src_ref=slab if src is None else src, dst_ref=slab,
            send_sem=send_sems.at[7 * a + k], recv_sem=recv_sems.at[7 * a + k], device_id=to, device_id_type=MESH)

    mine = [pltpu.make_async_copy(x_refs[a], out_refs[a].at[index(*me)], local_sems.at[a]) for a in range(n)]
    first = []
    for a in range(n):
        first.append(copy(a, 0, me, sibling, src=x_refs[a]))
        first += [copy(a, 1 + j, me, (cx, cy, c), src=x_refs[a]) for j, (cx, cy) in enumerate(chips)]

    def start():
        for cp in mine + first:
            cp.start()

    def finish():
        passed = []
        for j, (cx, cy) in enumerate(chips):
            for a in range(n):
                copy(a, 1 + j, (cx, cy, c), me).wait_recv()
                passed.append(copy(a, 4 + j, (cx, cy, c), sibling))
                passed[-1].start()
        for a in range(n):
            copy(a, 0, sibling, me).wait_recv()
            for j, (cx, cy) in enumerate(chips):
                copy(a, 4 + j, (cx, cy, 1 - c), me).wait_recv()
        for cp in first + passed:
            cp.wait_send()
        for cp in mine:
            cp.wait()

    return start, finish


def _gather_shapes(shards):
    return [jax.ShapeDtypeStruct((N_DEV,) + s.shape, s.dtype) for s in shards]


def _gather_sems(n):
    return [pltpu.SemaphoreType.DMA((7 * n,)), pltpu.SemaphoreType.DMA((7 * n,)), pltpu.SemaphoreType.DMA((n,))]


def _all_gather(shards):
    n = len(shards)

    def body(*refs):
        start, finish = _gather_plan(refs[:n], refs[n:2 * n], *refs[2 * n:])
        start()
        finish()

    return pl.pallas_call(
        body, name="all_gather_weights", out_shape=_gather_shapes(shards),
        in_specs=_hbm_specs(n), out_specs=_hbm_specs(n), scratch_shapes=_gather_sems(n),
    )(*shards)


def _pair_plan(p_refs, recv_refs, send_sems, recv_sems):
    n = len(p_refs)
    x, y, c = _mesh_pos()
    sibling = (x, y, 1 - c)

    def start():
        for a in range(n):
            for chip in range(4):
                pltpu.make_async_remote_copy(
                    src_ref=p_refs[a].at[2 * chip + (1 - c)], dst_ref=recv_refs[a].at[chip],
                    send_sem=send_sems.at[a], recv_sem=recv_sems.at[a], device_id=sibling, device_id_type=MESH).start()

    def finish():
        for a in range(n):
            pltpu.make_async_remote_copy(
                src_ref=recv_refs[a], dst_ref=recv_refs[a], send_sem=send_sems.at[a], recv_sem=recv_sems.at[a],
                device_id=sibling, device_id_type=MESH).wait()

    return start, finish


def _pair_shapes(partials):
    return [jax.ShapeDtypeStruct((4,) + s.shape[1:], s.dtype) for s in partials]


def _pair_sems(n):
    return [pltpu.SemaphoreType.DMA((n,)), pltpu.SemaphoreType.DMA((n,))]


def _rs_core_pair(name, partials):
    n = len(partials)

    def body(*refs):
        start, finish = _pair_plan(refs[:n], refs[n:2 * n], *refs[2 * n:])
        start()
        finish()

    return pl.pallas_call(
        body, name=name, out_shape=_pair_shapes(partials),
        in_specs=_hbm_specs(n), out_specs=_hbm_specs(n), scratch_shapes=_pair_sems(n),
    )(*partials)


def _chips_plan(cs_refs, out_refs, send_sems, recv_sems, local_sems):
    n = len(cs_refs)
    x, y, c = _mesh_pos()
    chip = 2 * x + y
    chips = _other_chips(x, y)
    mine = [pltpu.make_async_copy(cs_refs[a].at[chip], out_refs[a].at[chip], local_sems.at[a]) for a in range(n)]
    sends = [pltpu.make_async_remote_copy(
        src_ref=cs_refs[a].at[2 * cx + cy], dst_ref=out_refs[a].at[chip],
        send_sem=send_sems.at[3 * a + j], recv_sem=recv_sems.at[3 * a + j],
        device_id=(cx, cy, c), device_id_type=MESH) for a in range(n) for j, (cx, cy) in enumerate(chips)]

    def start():
        for cp in mine + sends:
            cp.start()

    def finish():
        for a in range(n):
            for j, (cx, cy) in enumerate(chips):
                pltpu.make_async_remote_copy(
                    src_ref=cs_refs[a].at[chip], dst_ref=out_refs[a].at[2 * cx + cy],
                    send_sem=send_sems.at[3 * a + j], recv_sem=recv_sems.at[3 * a + j],
                    device_id=(x, y, c), device_id_type=MESH).wait_recv()
        for cp in sends:
            cp.wait_send()
        for cp in mine:
            cp.wait()

    return start, finish


def _chips_sems(n):
    return [pltpu.SemaphoreType.DMA((3 * n,)), pltpu.SemaphoreType.DMA((3 * n,)), pltpu.SemaphoreType.DMA((n,))]


def _all_reduce_small(vec):
    rows, cols = vec.shape

    def body(x_ref, land_ref, sum_ref, send_sems, recv_sems):
        x, y, c = _mesh_pos()
        me = 4 * x + 2 * y + c
        land_ref[me] = x_ref[...]
        flips = [(fx, fy, fc) for fx in (0, 1) for fy in (0, 1) for fc in (0, 1)][1:]

        def flipped(f):
            return tuple((1 - v) if b else v for v, b in zip((x, y, c), f))

        sends = []
        for k, f in enumerate(flips):
            sends.append(pltpu.make_async_remote_copy(
                src_ref=x_ref, dst_ref=land_ref.at[me], send_sem=send_sems.at[k], recv_sem=recv_sems.at[k],
                device_id=flipped(f), device_id_type=MESH))
            sends[-1].start()
        for k, f in enumerate(flips):
            px, py, pc = flipped(f)
            pltpu.make_async_remote_copy(
                src_ref=x_ref, dst_ref=land_ref.at[4 * px + 2 * py + pc], send_sem=send_sems.at[k],
                recv_sem=recv_sems.at[k], device_id=(x, y, c), device_id_type=MESH).wait_recv()
        for cp in sends:
            cp.wait_send()
        total = land_ref[0]
        for d in range(1, N_DEV):
            total = total + land_ref[d]
        sum_ref[...] = total

    vm = pl.BlockSpec(memory_space=pltpu.VMEM)
    return pl.pallas_call(
        body, name="all_reduce_small",
        out_shape=(jax.ShapeDtypeStruct((N_DEV, rows, cols), F32), jax.ShapeDtypeStruct((rows, cols), F32)),
        in_specs=[vm], out_specs=(vm, vm),
        scratch_shapes=[pltpu.SemaphoreType.DMA((7,)), pltpu.SemaphoreType.DMA((7,))],
    )(vec)[1]


def _block_rows(rows, cols, itemsize, align, row_off=0):
    best = None
    for t in range(align, rows + 1, align):
        if rows % t == 0 and row_off % t == 0 and t * cols * itemsize <= (1 << 20):
            best = t
    return rows if best is None else best


def _pair_add(name, partial, recv, my_c):
    _, rows, cols = partial.shape
    br = _block_rows(rows, cols, 2, 16)

    def body(c_ref, a_ref, b_ref, o_ref):
        o_ref[...] = (a_ref[...].astype(F32) + b_ref[...].astype(F32)).astype(BF16)

    return pl.pallas_call(
        body, name=name,
        grid_spec=pltpu.PrefetchScalarGridSpec(
            num_scalar_prefetch=1, grid=(4, rows // br),
            in_specs=[pl.BlockSpec((None, None, br, cols), lambda j, i, c_ref: (j, c_ref[0], i, 0)),
                      pl.BlockSpec((None, br, cols), lambda j, i, c_ref: (j, i, 0))],
            out_specs=pl.BlockSpec((None, br, cols), lambda j, i, c_ref: (j, i, 0))),
        out_shape=jax.ShapeDtypeStruct((4, rows, cols), BF16),
    )(my_c.reshape(1).astype(jnp.int32), partial.reshape(4, 2, rows, cols), recv)


def _adam_update(w, g, m, v):
    nm = ADAM_B1 * m + (1.0 - ADAM_B1) * g
    nv = ADAM_B2 * v + (1.0 - ADAM_B2) * (g * g)
    m_hat = nm / (1.0 - ADAM_B1 ** ADAM_STEP)
    v_hat = nv / (1.0 - ADAM_B2 ** ADAM_STEP)
    return -ADAM_LR * (m_hat / (jnp.sqrt(v_hat) + ADAM_EPS) + ADAM_WD * w), nm, nv


def _adamw_parts(name, w, parts, row_off, m, v):
    rows, cols = w.shape
    tr = _block_rows(rows, cols, 4, 16, row_off)
    tc = cols
    if tr == rows and rows % 16 != 0 and cols % (2 * LANES) == 0:
        tc = 2 * LANES
    assert rows % tr == 0 and row_off % tr == 0 and (tc == cols or row_off == 0)
    off = row_off // tr

    def body(w_ref, p_ref, m_ref, v_ref, g_ref, d_ref, nm_ref, nv_ref):
        g = p_ref[0].astype(F32)
        for j in range(1, 4):
            g = g + p_ref[j].astype(F32)
        g_ref[...] = g
        d_ref[...], nm_ref[...], nv_ref[...] = _adam_update(w_ref[...], g, m_ref[...], v_ref[...])

    spec = pl.BlockSpec((tr, tc), lambda i, j: (i, j))
    shp = jax.ShapeDtypeStruct((rows, cols), F32)
    return pl.pallas_call(
        body, name=name, grid=(rows // tr, cols // tc), out_shape=(shp,) * 4,
        in_specs=[spec, pl.BlockSpec((4, tr, tc), lambda i, j: (0, off + i, j)), spec, spec], out_specs=(spec,) * 4,
    )(w, parts, m, v)


def _adamw_small(tensors):
    n = len(tensors)

    def body(*refs):
        ins, outs = refs[:4 * n], refs[4 * n:]
        for t in range(n):
            w_ref, g_ref, m_ref, v_ref = ins[4 * t:4 * t + 4]
            d, nm, nv = _adam_update(w_ref[...], g_ref[...], m_ref[...], v_ref[...])
            outs[3 * t][...], outs[3 * t + 1][...], outs[3 * t + 2][...] = d, nm, nv

    vm = pl.BlockSpec(memory_space=pltpu.VMEM)
    out = pl.pallas_call(
        body, name="adamw_small",
        out_shape=[jax.ShapeDtypeStruct(t[0].shape, F32) for t in tensors for _ in range(3)],
        in_specs=[vm] * (4 * n), out_specs=[vm] * (3 * n),
    )(*[a for t in tensors for a in t])
    return [tuple(out[3 * t:3 * t + 3]) for t in range(n)]


def _matmul_tn(name, a, b, relu2=False, slabs=False, lhs_t=False):
    a_groups = a.shape[0] if a.ndim == 3 else 0
    b_groups = b.shape[0] if b.ndim == 3 else 0
    groups = max(a_groups, b_groups, 1)
    assert not (a_groups and b_groups) and not (a_groups and lhs_t)
    a3 = a if a_groups else a[None]
    b3 = b if b_groups else b[None]
    t_len, k_len = a3.shape[1:][::-1] if lhs_t else a3.shape[1:]
    n_len = b3.shape[2]
    tt = min(t_len, 512)
    tk = min(k_len, 1024)
    tn = n_len // N_DEV if slabs else min(n_len, 1024)
    nt = t_len // tt
    assert not slabs or (groups == 1 and tn <= 1024)

    def body(a_ref, b_ref, o_ref, acc_ref):
        @pl.when(pl.program_id(3) == 0)
        def _():
            acc_ref[...] = jnp.zeros_like(acc_ref)

        av = a_ref[...]
        if relu2:
            av = jnp.square(jnp.maximum(av.astype(F32), 0.0))
        product = _dot if lhs_t else _dot_tn
        acc_ref[...] += product(av.astype(BF16), b_ref[...].astype(BF16))

        @pl.when(pl.program_id(3) == nt - 1)
        def _():
            o_ref[...] = acc_ref[...].astype(BF16)

    def a_group(g):
        return g if a_groups else 0

    def b_group(g):
        return g if b_groups else 0

    if slabs:
        out_shape = jax.ShapeDtypeStruct((N_DEV, k_len, tn), BF16)
        out_spec = pl.BlockSpec((None, tk, tn), lambda g, i, j, t: (j, i, 0))
    else:
        out_shape = jax.ShapeDtypeStruct((groups, k_len, n_len), BF16)
        out_spec = pl.BlockSpec((None, tk, tn), lambda g, i, j, t: (g, i, j))
    out = pl.pallas_call(
        body, name=name, grid=(groups, k_len // tk, n_len // tn, nt), out_shape=out_shape,
        in_specs=[pl.BlockSpec((None, tk, tt), lambda g, i, j, t: (a_group(g), i, t)) if lhs_t
                  else pl.BlockSpec((None, tt, tk), lambda g, i, j, t: (a_group(g), t, i)),
                  pl.BlockSpec((None, tt, tn), lambda g, i, j, t: (b_group(g), t, j))],
        out_specs=out_spec,
        scratch_shapes=[pltpu.VMEM((tk, tn), F32)],
        compiler_params=pltpu.CompilerParams(
            dimension_semantics=("parallel", "parallel", "parallel", "arbitrary"), vmem_limit_bytes=VMEM_LIMIT),
    )(a3, b3)
    return out if (slabs or a_groups or b_groups) else out[0]


def _matmul_tn_once(name, lhs_list, rhs, relu2=False, slabs=False, lhs_t=False):
    t_len, n_len = rhs.shape
    tt = min(t_len, 256)
    nt = t_len // tt
    n_lhs = len(lhs_list)
    assert not (lhs_t or slabs) or (n_lhs == 1 and lhs_list[0].ndim == 2)
    k_shapes = [(a.shape[0], n_len) if lhs_t else a.shape[:-2] + (a.shape[-1], n_len) for a in lhs_list]
    tn = n_len // N_DEV

    def body(*refs):
        a_refs, b_ref = refs[:n_lhs], refs[n_lhs]
        o_refs, acc_refs = refs[n_lhs + 1:2 * n_lhs + 1], refs[2 * n_lhs + 1:]
        step = pl.program_id(0)

        @pl.when(step == 0)
        def _():
            for acc in acc_refs:
                acc[...] = jnp.zeros_like(acc)

        bv = b_ref[...].astype(BF16)

        def piece(av):
            if relu2:
                av = jnp.square(jnp.maximum(av.astype(F32), 0.0))
            return (_dot if lhs_t else _dot_tn)(av.astype(BF16), bv)

        for a_ref, acc in zip(a_refs, acc_refs):
            if len(acc.shape) == 3:
                for g in range(acc.shape[0]):
                    acc[g] += piece(a_ref[g])
            else:
                acc[...] += piece(a_ref[...])

        @pl.when(step == nt - 1)
        def _():
            for o_ref, acc in zip(o_refs, acc_refs):
                if slabs:
                    for j in range(N_DEV):
                        o_ref[j] = acc[:, j * tn:(j + 1) * tn].astype(BF16)
                else:
                    o_ref[...] = acc[...].astype(BF16)

    def lhs_spec(a):
        if lhs_t:
            return pl.BlockSpec((a.shape[0], tt), lambda t: (0, t))
        if a.ndim == 3:
            return pl.BlockSpec((a.shape[0], tt, a.shape[2]), lambda t: (0, t, 0))
        return pl.BlockSpec((tt, a.shape[1]), lambda t: (t, 0))

    out_shapes = [(N_DEV, k_shapes[0][0], tn)] if slabs else k_shapes
    return pl.pallas_call(
        body, name=name, grid=(nt,),
        out_shape=[jax.ShapeDtypeStruct(s, BF16) for s in out_shapes],
        in_specs=[lhs_spec(a) for a in lhs_list] + [pl.BlockSpec((tt, n_len), lambda t: (t, 0))],
        out_specs=[_acc_spec(s) for s in out_shapes],
        scratch_shapes=[pltpu.VMEM(s, F32) for s in k_shapes],
        compiler_params=_seq_params(),
    )(*lhs_list, rhs)


def _pad_layout(d):
    names = ("qf", "kf", "vf", "qb", "kb", "vb", "gates", "forget")
    sizes = (N_HEADS * HEAD_SLOT, N_HEADS * HEAD_SLOT, D_BRANCH, D_BRANCH, D_BRANCH, D_BRANCH, 2 * d, LANES)
    out, off = {}, 0
    for n, s in zip(names, sizes):
        out[n] = (off, off + s)
        off += s
    return out, off


def _slot_rows(xt, extra):
    parts = []
    for h in range(N_HEADS):
        parts += [xt[h * HEAD_DIM:(h + 1) * HEAD_DIM, :], extra]
    return jnp.concatenate(parts, axis=0)


def _inproj_fwd(x, g_mix, w_pad, bf_pad, place_q, place_k, ones_q, ones_k, seq):
    t_len, d = x.shape
    lay, _ = _pad_layout(d)
    tiles_per_seq = seq // TM
    slot_w = N_HEADS * HEAD_SLOT

    def body(x_ref, g_ref, w_ref, bf_ref, pq_ref, pk_ref, oq_ref, ok_ref,
             qf_ref, kf_ref, kft_ref, vf_ref, vft_ref, qkvb_ref, kbt_ref, vbt_ref, gl_ref, fpre_ref, h_ref,
             carry_ref):
        @pl.when(pl.program_id(0) % tiles_per_seq == 0)
        def _():
            carry_ref[...] = jnp.zeros_like(carry_ref)

        def proj(name):
            lo, hi = lay[name]
            return _dot_nt(h, w_ref[lo:hi, :])

        xn, _ = _rms(x_ref[...])
        h = (xn * g_ref[...]).astype(BF16)
        fpre = proj("forget") + bf_ref[...]
        fpre_ref[...] = fpre
        logf = -_softplus(-fpre)
        lower = _tri(TM, lambda r, c: c <= r)
        hi, mid, lo = _split3(logf)
        c_val = carry_ref[...] + _dot(lower, hi) + _dot(lower, mid) + _dot(lower, lo)
        carry_ref[...] = carry_ref[...] + jnp.sum(logf, axis=0, keepdims=True)
        c3 = _split3(c_val)
        qf_ref[...] = (proj("qf") + sum(_dot(c3[j], pq_ref[j]) for j in range(3)) + oq_ref[...]).astype(BF16)
        kf = proj("kf") - sum(_dot(c3[j], pk_ref[j]) for j in range(3)) + ok_ref[...]
        kf_ref[...] = kf.astype(BF16)
        kft_ref[0] = kf.T.astype(BF16)
        row0 = (lax.broadcasted_iota(jnp.int32, (HEAD_DIM, TM), 0) == 0).astype(F32)
        zeros = jnp.zeros((HEAD_DIM, TM), F32)
        vf = proj("vf")
        vf_ref[...] = vf.astype(BF16)
        vft_ref[0] = _slot_rows(vf.T, row0).astype(BF16)
        qkvb_ref[0] = proj("qb").astype(BF16)
        kb = proj("kb")
        qkvb_ref[1] = kb.astype(BF16)
        kbt_ref[0] = _slot_rows(kb.T, zeros).astype(BF16)
        vb = proj("vb")
        qkvb_ref[2] = vb.astype(BF16)
        vbt_ref[0] = _slot_rows(vb.T, row0).astype(BF16)
        gl_ref[...] = proj("gates").astype(BF16)
        h_ref[...] = h

    n_tiles = t_len // TM
    slot_shape = jax.ShapeDtypeStruct((t_len, slot_w), BF16)
    t_shape = jax.ShapeDtypeStruct((n_tiles, slot_w, TM), BF16)
    t_spec = pl.BlockSpec((1, slot_w, TM), lambda i: (i, 0, 0))
    return pl.pallas_call(
        body, name="inproj_fwd", grid=(n_tiles,),
        out_shape=(slot_shape, slot_shape, t_shape, jax.ShapeDtypeStruct((t_len, D_BRANCH), BF16), t_shape,
                   jax.ShapeDtypeStruct((3, t_len, D_BRANCH), BF16), t_shape, t_shape,
                   jax.ShapeDtypeStruct((t_len, 2 * d), BF16), jax.ShapeDtypeStruct((t_len, LANES), F32),
                   jax.ShapeDtypeStruct((t_len, d), BF16)),
        in_specs=[_row_spec(TM, d), _const_spec((1, d)), _const_spec(w_pad.shape), _const_spec((1, LANES)),
                  _const_spec(place_q.shape), _const_spec(place_k.shape), _const_spec((1, slot_w)),
                  _const_spec((1, slot_w))],
        out_specs=(_row_spec(TM, slot_w), _row_spec(TM, slot_w), t_spec, _row_spec(TM, D_BRANCH), t_spec,
                   _row3_spec(3, TM, D_BRANCH), t_spec, t_spec, _row_spec(TM, 2 * d), _row_spec(TM, LANES),
                   _row_spec(TM, d)),
        scratch_shapes=[pltpu.VMEM((1, LANES), F32)],
        compiler_params=_seq_params(),
    )(x, g_mix, w_pad, bf_pad, place_q, place_k, ones_q, ones_k)


def _slot_spec(seq):
    return pl.BlockSpec((seq, NH * HEAD_SLOT), lambda b, g: (b, g))


def _slot2_spec(seq):
    return pl.BlockSpec((2, seq, NH * HEAD_SLOT), lambda b, g: (0, b, g))


def _group_spec(seq):
    return pl.BlockSpec((seq, NH * HEAD_DIM), lambda b, g: (b, g))


def _group3_spec(which, seq):
    return pl.BlockSpec((None, seq, NH * HEAD_DIM), lambda b, g: (which, b, g))


def _tblock_spec(seq):
    return pl.BlockSpec((seq // TK, NH * HEAD_SLOT, TK), lambda b, g: (b, g, 0))


def _qrow_spec(seq):
    return pl.BlockSpec((None, NH, seq // TQ, TQ), lambda b, g: (b, g, 0, 0))


def _stat_spec(seq):
    return pl.BlockSpec((None, None, seq, LANES), lambda b, g: (b, g, 0, 0))


def _attn_params():
    return pltpu.CompilerParams(dimension_semantics=("parallel", "parallel"), vmem_limit_bytes=VMEM_LIMIT)


def _serial_attn_params():
    return pltpu.CompilerParams(dimension_semantics=("arbitrary", "arbitrary"), vmem_limit_bytes=VMEM_LIMIT)


def _hcols(hh):
    return slice(hh * HEAD_DIM, (hh + 1) * HEAD_DIM)


def _hslot(hh):
    return slice(hh * HEAD_SLOT, (hh + 1) * HEAD_SLOT)


def _lane(hh):
    return slice(hh, hh + 1)


def _key_query_mask(rel):
    r = lax.broadcasted_iota(jnp.int32, (TK, TQ), 0)
    c = lax.broadcasted_iota(jnp.int32, (TK, TQ), 1)
    return rel(r, c)


def _heads_cat(vals):
    return jnp.concatenate(vals, axis=1)


def _untranspose(acc_t):
    return acc_t.T[:, :HEAD_DIM]


def _fox_fwd(qf, kf, vft, batch, seq):
    def body(q_ref, k_ref, vt_ref, o_ref, lse_ref, m_s, acc_s):
        causal = _key_query_mask(lambda r, c: r <= c)

        def tile(q0, kj, masked, n_k=1):
            krows = pl.ds(pl.multiple_of(kj * TK, TK), n_k * TK)
            heads = range(NH)
            sts = [_dot_nt(k_ref[krows, _hslot(hh)], q_ref[pl.ds(q0, TQ), _hslot(hh)]) for hh in heads]
            if masked:
                sts = [jnp.where(causal, st, NEG) for st in sts]
            m_olds = [m_s[hh] for hh in heads]
            m_news = [jnp.maximum(m_olds[hh], jnp.max(sts[hh], axis=0, keepdims=True)) for hh in heads]
            pts = [jnp.exp(sts[hh] - m_news[hh]).astype(BF16) for hh in heads]
            pvs = [sum(_dot(vt_ref[kj + i, _hslot(hh), :], pts[hh][i * TK:(i + 1) * TK]) for i in range(n_k))
                   for hh in heads]
            for hh in heads:
                acc_s[hh] = jnp.exp(m_olds[hh] - m_news[hh]) * acc_s[hh] + pvs[hh]
                m_s[hh] = m_news[hh]

        def q_loop(qi, _):
            q0 = pl.multiple_of(qi * TQ, TQ)
            m_s[...] = jnp.full(m_s.shape, NEG, F32)
            acc_s[...] = jnp.zeros_like(acc_s)

            def pair_loop(i, _):
                tile(q0, 2 * i, False, n_k=2)
                return 0

            lax.fori_loop(0, qi // 2, pair_loop, 0)
            pl.when(qi % 2 == 1)(lambda: tile(q0, qi - 1, False))
            tile(q0, qi, True)
            outs = []
            for hh in range(NH):
                total = acc_s[hh, HEAD_DIM:HEAD_DIM + 1, :]
                outs.append(_untranspose(acc_s[hh] / total))
                lse_ref[hh, pl.ds(qi, 1), :] = m_s[hh] + jnp.log(total)
            o_ref[pl.ds(q0, TQ), :] = _heads_cat(outs).astype(BF16)
            return 0

        lax.fori_loop(0, seq // TQ, q_loop, 0)

    return pl.pallas_call(
        body, name="fox_fwd", grid=(batch, N_HEADS // NH),
        out_shape=(jax.ShapeDtypeStruct((batch * seq, D_BRANCH), BF16),
                   jax.ShapeDtypeStruct((batch, N_HEADS, seq // TQ, TQ), F32)),
        in_specs=[_slot_spec(seq), _slot_spec(seq), _tblock_spec(seq)],
        out_specs=(_group_spec(seq), _qrow_spec(seq)),
        scratch_shapes=[pltpu.VMEM((NH, 1, TQ), F32), pltpu.VMEM((NH, HEAD_SLOT, TQ), F32)],
        compiler_params=_attn_params(),
    )(qf, kf, vft)


def _fox_bwd(qf, kf, kft, vf, o, do, lse, batch, seq, partials):
    n_q = seq // TQ
    n = len(partials)

    def body(q_ref, k_ref, kt_ref, v_ref, o_ref, do_ref, lse_ref, *rest):
        p_refs, (dqk_ref, dv_ref, dcq_ref, dck_ref), recv_refs = rest[:n], rest[n:n + 4], rest[n + 4:2 * n + 4]
        delta_s, dqt_acc, dk_s, dv_s = rest[2 * n + 4:2 * n + 8]
        pair_start, pair_finish = _pair_plan(p_refs, recv_refs, *rest[2 * n + 8:])
        first_step, last_step = _first_last_step()
        pl.when(first_step)(pair_start)
        causal = _key_query_mask(lambda r, c: r <= c)
        ones8 = jnp.ones((8, HEAD_DIM), BF16)
        dqt_acc[...] = jnp.zeros_like(dqt_acc)

        def prep(qi, _):
            rows = pl.ds(pl.multiple_of(qi * TQ, TQ), TQ)
            for hh in range(NH):
                hi, lo = _split2(do_ref[rows, _hcols(hh)].astype(F32) * o_ref[rows, _hcols(hh)].astype(F32))
                delta_s[hh, pl.ds(qi, 1), :] = (_dot_nt(ones8, hi) + _dot_nt(ones8, lo))[0:1, :]
            return 0

        lax.fori_loop(0, n_q, prep, 0)

        def tile(qi, kj, masked):
            rows = pl.ds(pl.multiple_of(qi * TQ, TQ), TQ)
            krows = pl.ds(pl.multiple_of(kj * TK, TK), TK)
            heads = range(NH)
            qs = [q_ref[rows, _hslot(hh)] for hh in heads]
            douts = [do_ref[rows, _hcols(hh)] for hh in heads]
            sts = [_dot_nt(k_ref[krows, _hslot(hh)], qs[hh]) for hh in heads]
            dps = [_dot_nt(v_ref[krows, _hcols(hh)], douts[hh]) for hh in heads]
            pts = [jnp.exp(sts[hh] - lse_ref[hh, pl.ds(qi, 1), :]) for hh in heads]
            if masked:
                pts = [jnp.where(causal, pt, 0.0) for pt in pts]
            dsts = [(pts[hh] * (dps[hh] - delta_s[hh, pl.ds(qi, 1), :])).astype(BF16) for hh in heads]
            for hh in heads:
                dv_s[hh] += _dot(pts[hh].astype(BF16), douts[hh])
                dk_s[hh] += _dot(dsts[hh], qs[hh])
                dqt_acc[hh, qi] += _dot(kt_ref[kj, _hslot(hh), :], dsts[hh])

        def k_loop(kj, _):
            krows = pl.ds(pl.multiple_of(kj * TK, TK), TK)
            dk_s[...] = jnp.zeros_like(dk_s)
            dv_s[...] = jnp.zeros_like(dv_s)
            tile(kj, kj, True)

            def q_loop(qi, _):
                tile(qi, kj, False)
                return 0

            lax.fori_loop(kj + 1, n_q, q_loop, 0)
            dqk_ref[1, krows, :] = _heads_cat([dk_s[hh] for hh in range(NH)]).astype(BF16)
            dv_ref[krows, :] = _heads_cat([dv_s[hh] for hh in range(NH)]).astype(BF16)
            for hh in range(NH):
                dck_ref[krows, _lane(hh)] = dk_s[hh, :, C_ONES_Q:C_ONES_Q + 1]
            return 0

        lax.fori_loop(0, seq // TK, k_loop, 0)

        def finish(qi, _):
            rows = pl.ds(pl.multiple_of(qi * TQ, TQ), TQ)
            dqk_ref[0, rows, :] = _heads_cat([dqt_acc[hh, qi].T for hh in range(NH)]).astype(BF16)
            for hh in range(NH):
                dcq_ref[hh, pl.ds(qi, 1), :] = dqt_acc[hh, qi, C_ONES_K:C_ONES_K + 1, :]
            return 0

        lax.fori_loop(0, n_q, finish, 0)
        pl.when(last_step)(pair_finish)

    out = pl.pallas_call(
        body, name="fox_bwd", grid=(batch, N_HEADS // NH),
        out_shape=[jax.ShapeDtypeStruct((2, batch * seq, N_HEADS * HEAD_SLOT), BF16),
                   jax.ShapeDtypeStruct((batch * seq, D_BRANCH), BF16),
                   jax.ShapeDtypeStruct((batch, N_HEADS, seq // TQ, TQ), F32),
                   jax.ShapeDtypeStruct((batch, N_HEADS // NH, seq, LANES), F32)] + _pair_shapes(partials),
        in_specs=[_slot_spec(seq), _slot_spec(seq), _tblock_spec(seq), _group_spec(seq), _group_spec(seq),
                  _group_spec(seq), _qrow_spec(seq)] + _hbm_specs(n),
        out_specs=[_slot2_spec(seq), _group_spec(seq), _qrow_spec(seq), _stat_spec(seq)] + _hbm_specs(n),
        scratch_shapes=[pltpu.VMEM((NH, n_q, TQ), F32), pltpu.VMEM((NH, n_q, HEAD_SLOT, TQ), F32),
                        pltpu.VMEM((NH, TK, HEAD_SLOT), F32), pltpu.VMEM((NH, TK, HEAD_DIM), F32)] + _pair_sems(n),
        compiler_params=_serial_attn_params(),
    )(qf, kf, kft, vf, o, do, lse, *partials)
    return out[0], out[1], out[2], out[3], out[4:]


def _first_last_step():
    step = pl.program_id(0) * pl.num_programs(1) + pl.program_id(1)
    return step == 0, step == pl.num_programs(0) * pl.num_programs(1) - 1


def _sb_fwd(qkvb, vbt, batch, seq, shards):
    n = len(shards)

    def body(q_ref, k_ref, vt_ref, *rest):
        x_refs, (o_ref, lt_ref), out_refs = rest[:n], rest[n:n + 2], rest[n + 2:2 * n + 2]
        run_s, acc_s = rest[2 * n + 2:2 * n + 4]
        gather_start, gather_finish = _gather_plan(x_refs, out_refs, *rest[2 * n + 4:])
        first_step, last_step = _first_last_step()
        pl.when(first_step)(gather_start)
        strict = _key_query_mask(lambda r, c: r < c)
        later = _tri(TK, lambda r, c: c > r)

        def tile(q0, kj, masked):
            krows = pl.ds(pl.multiple_of(kj * TK, TK), TK)
            heads = range(NH)
            zts = [_dot_nt(k_ref[krows, _hcols(hh)], q_ref[pl.ds(q0, TQ), _hcols(hh)]) for hh in heads]
            lgs = [-_softplus(zt) for zt in zts]
            if masked:
                lgs = [jnp.where(strict, lg, 0.0) for lg in lgs]
            parts = [_split2(lg) for lg in lgs]
            sufs = [_dot(later, hi) + _dot(later, lo) for hi, lo in parts]
            ats = [jnp.exp(zts[hh] + lgs[hh] + run_s[hh] + sufs[hh]) for hh in heads]
            if masked:
                ats = [jnp.where(strict, at, 0.0) for at in ats]
            for hh in heads:
                acc_s[hh] += _dot(vt_ref[kj, _hslot(hh), :], ats[hh].astype(BF16))
                run_s[hh] += jnp.sum(lgs[hh], axis=0, keepdims=True)

        def q_loop(qi, _):
            q0 = pl.multiple_of(qi * TQ, TQ)
            run_s[...] = jnp.zeros_like(run_s)
            acc_s[...] = jnp.zeros_like(acc_s)
            tile(q0, qi, True)

            def k_loop(kk, _):
                tile(q0, qi - 1 - kk, False)
                return 0

            lax.fori_loop(0, qi, k_loop, 0)
            o_ref[pl.ds(q0, TQ), :] = _heads_cat([_untranspose(acc_s[hh]) for hh in range(NH)]).astype(BF16)
            for hh in range(NH):
                lt_ref[hh, pl.ds(qi, 1), :] = run_s[hh]
            return 0

        lax.fori_loop(0, seq // TQ, q_loop, 0)
        pl.when(last_step)(gather_finish)

    out = pl.pallas_call(
        body, name="sb_fwd", grid=(batch, N_HEADS // NH),
        out_shape=[jax.ShapeDtypeStruct((batch * seq, D_BRANCH), BF16),
                   jax.ShapeDtypeStruct((batch, N_HEADS, seq // TQ, TQ), F32)] + _gather_shapes(shards),
        in_specs=[_group3_spec(0, seq), _group3_spec(1, seq), _tblock_spec(seq)] + _hbm_specs(n),
        out_specs=[_group_spec(seq), _qrow_spec(seq)] + _hbm_specs(n),
        scratch_shapes=[pltpu.VMEM((NH, 1, TQ), F32), pltpu.VMEM((NH, HEAD_SLOT, TQ), F32)] + _gather_sems(n),
        compiler_params=_serial_attn_params(),
    )(qkvb, qkvb, vbt, *shards)
    return out[0], out[1], out[2:]


def _sb_bwd(qkvb, kbt, do, ltot, batch, seq, chip_sums):
    n = len(chip_sums)

    def body(q_ref, k_ref, v_ref, kt_ref, do_ref, lt_ref, *rest):
        cs_refs, dqkv_ref, out_refs = rest[:n], rest[n], rest[n + 1:2 * n + 1]
        dk_acc, dv_acc, ls_s, gs_s, dqt_s = rest[2 * n + 1:2 * n + 6]
        chips_start, chips_finish = _chips_plan(cs_refs, out_refs, *rest[2 * n + 6:])
        first_step, last_step = _first_last_step()
        pl.when(first_step)(chips_start)
        strict = _key_query_mask(lambda r, c: r < c)
        upto = _tri(TK, lambda r, c: c <= r)
        before = _tri(TK, lambda r, c: c < r)
        dk_acc[...] = jnp.zeros_like(dk_acc)
        dv_acc[...] = jnp.zeros_like(dv_acc)

        def tile(qi, kj, masked):
            rows = pl.ds(pl.multiple_of(qi * TQ, TQ), TQ)
            krows = pl.ds(pl.multiple_of(kj * TK, TK), TK)
            heads = range(NH)
            qs = [q_ref[rows, _hcols(hh)] for hh in heads]
            douts = [do_ref[rows, _hcols(hh)] for hh in heads]
            zts = [_dot_nt(k_ref[krows, _hcols(hh)], qs[hh]) for hh in heads]
            das = [_dot_nt(v_ref[krows, _hcols(hh)], douts[hh]) for hh in heads]
            lgs = [-_softplus(zt) for zt in zts]
            if masked:
                lgs = [jnp.where(strict, lg, 0.0) for lg in lgs]
            parts = [_split2(lg) for lg in lgs]
            prefs = [_dot(upto, hi) + _dot(upto, lo) for hi, lo in parts]
            ats = [jnp.exp(zts[hh] + lgs[hh] + (lt_ref[hh, pl.ds(qi, 1), :] - ls_s[hh]) - prefs[hh]) for hh in heads]
            if masked:
                ats = [jnp.where(strict, at, 0.0) for at in ats]
            gts = [das[hh] * ats[hh] for hh in heads]
            us = [gs_s[hh] + _dot(before, gts[hh].astype(BF16)) for hh in heads]
            dzts = [(jnp.exp(lgs[hh]) * (gts[hh] + us[hh]) - us[hh]).astype(BF16) for hh in heads]
            for hh in heads:
                dk_acc[hh, krows, :] += _dot(dzts[hh], qs[hh])
                dv_acc[hh, krows, :] += _dot(ats[hh].astype(BF16), douts[hh])
                dqt_s[hh] += _dot(kt_ref[kj, _hslot(hh), :], dzts[hh])
                ls_s[hh] += jnp.sum(lgs[hh], axis=0, keepdims=True)
                gs_s[hh] += jnp.sum(gts[hh], axis=0, keepdims=True)

        def q_loop(qi, _):
            ls_s[...] = jnp.zeros_like(ls_s)
            gs_s[...] = jnp.zeros_like(gs_s)
            dqt_s[...] = jnp.zeros_like(dqt_s)

            def k_loop(kj, _):
                tile(qi, kj, False)
                return 0

            lax.fori_loop(0, qi, k_loop, 0)
            tile(qi, qi, True)
            dqkv_ref[0, pl.ds(pl.multiple_of(qi * TQ, TQ), TQ), :] = _heads_cat(
                [_untranspose(dqt_s[hh]) for hh in range(NH)]).astype(BF16)
            return 0

        lax.fori_loop(0, seq // TQ, q_loop, 0)
        dqkv_ref[1] = _heads_cat([dk_acc[hh] for hh in range(NH)]).astype(BF16)
        dqkv_ref[2] = _heads_cat([dv_acc[hh] for hh in range(NH)]).astype(BF16)
        pl.when(last_step)(chips_finish)

    out = pl.pallas_call(
        body, name="sb_bwd", grid=(batch, N_HEADS // NH),
        out_shape=[jax.ShapeDtypeStruct((3, batch * seq, D_BRANCH), BF16)]
        + [jax.ShapeDtypeStruct(s.shape, s.dtype) for s in chip_sums],
        in_specs=[_group3_spec(0, seq), _group3_spec(1, seq), _group3_spec(2, seq), _tblock_spec(seq),
                  _group_spec(seq), _qrow_spec(seq)] + _hbm_specs(n),
        out_specs=[pl.BlockSpec((3, seq, NH * HEAD_DIM), lambda b, g: (0, b, g))] + _hbm_specs(n),
        scratch_shapes=[pltpu.VMEM((NH, seq, HEAD_DIM), F32), pltpu.VMEM((NH, seq, HEAD_DIM), F32),
                        pltpu.VMEM((NH, 1, TQ), F32), pltpu.VMEM((NH, 1, TQ), F32),
                        pltpu.VMEM((NH, HEAD_SLOT, TQ), F32)] + _chips_sems(n),
        compiler_params=_serial_attn_params(),
    )(qkvb, qkvb, qkvb, kbt, do, ltot, *chip_sums)
    return out[0], out[1:]


def _forget_bwd(dcq_tok, dck_tok, fpre, batch, seq):
    t_len = batch * seq
    tiles = seq // TM

    def rev(i):
        return ((i // tiles) * tiles + (tiles - 1 - i % tiles), 0)

    def body(dcq_ref, dck_ref, f_ref, df_ref, db_ref, carry_ref):
        i = pl.program_id(0)

        @pl.when(i == 0)
        def _():
            db_ref[...] = jnp.zeros_like(db_ref)

        @pl.when(i % tiles == 0)
        def _():
            carry_ref[...] = jnp.zeros_like(carry_ref)

        dc = dcq_ref[...] - dck_ref[...]
        upper = _tri(TM, lambda r, c: c >= r)
        hi, mid, lo = _split3(dc)
        dlogf = carry_ref[...] + _dot(upper, hi) + _dot(upper, mid) + _dot(upper, lo)
        carry_ref[...] = carry_ref[...] + jnp.sum(dc, axis=0, keepdims=True)
        df = dlogf * _sigmoid(-f_ref[...])
        df_ref[...] = df.astype(BF16)
        db_ref[...] += jnp.sum(df, axis=0, keepdims=True)

    return pl.pallas_call(
        body, name="forget_bwd", grid=(t_len // TM,),
        out_shape=(jax.ShapeDtypeStruct((t_len, LANES), BF16), jax.ShapeDtypeStruct((1, LANES), F32)),
        in_specs=[pl.BlockSpec((TM, LANES), rev)] * 3,
        out_specs=(pl.BlockSpec((TM, LANES), rev), _acc_spec((1, LANES))),
        scratch_shapes=[pltpu.VMEM((1, LANES), F32)],
        compiler_params=_seq_params(),
    )(dcq_tok, dck_tok, fpre)


def _mix_fwd(o_fox, o_sb, gl, x, w_bf, w_bs, w_out, b_gate):
    t_len, d = x.shape

    def body(of_ref, os_ref, gl_ref, x_ref, wbf_ref, wbs_ref, wo_ref, bg_ref, x1_ref):
        br_f = _dot(of_ref[...], wbf_ref[...])
        br_s = _dot(os_ref[...], wbs_ref[...])
        ga = _sigmoid(gl_ref[:, :d].astype(F32) + bg_ref[0:1, :])
        gb = _sigmoid(gl_ref[:, d:].astype(F32) + bg_ref[1:2, :])
        merged = ga * br_f + gb * br_s
        x1_ref[...] = x_ref[...] + _dot(merged.astype(BF16), wo_ref[...])

    return pl.pallas_call(
        body, name="mix_fwd", grid=(t_len // TM,),
        out_shape=jax.ShapeDtypeStruct((t_len, d), F32),
        in_specs=[_row_spec(TM, D_BRANCH), _row_spec(TM, D_BRANCH), _row_spec(TM, 2 * d), _row_spec(TM, d),
                  _const_spec(w_bf.shape), _const_spec(w_bs.shape), _const_spec(w_out.shape), _const_spec(b_gate.shape)],
        out_specs=_row_spec(TM, d),
        compiler_params=_seq_params(),
    )(o_fox, o_sb, gl, x, w_bf, w_bs, w_out, b_gate)


def _ff_chunk(d_ff):
    return min(d_ff, 1024)


def _mlp_fwd(x1, g_mlp, w_up, w_down):
    t_len, d = x1.shape
    d_ff = w_up.shape[1]
    ch = _ff_chunk(d_ff)

    def body(x1_ref, g_ref, wu_ref, wd_ref, a_ref, x2_ref):
        x1v = x1_ref[...]
        xn, _ = _rms(x1v)
        h = (xn * g_ref[...]).astype(BF16)
        acc = x1v
        for j in range(d_ff // ch):
            a = _dot(h, wu_ref[:, j * ch:(j + 1) * ch])
            a_ref[:, j * ch:(j + 1) * ch] = a.astype(BF16)
            acc = acc + _dot(jnp.square(jnp.maximum(a, 0.0)).astype(BF16), wd_ref[j * ch:(j + 1) * ch, :])
        x2_ref[...] = acc

    return pl.pallas_call(
        body, name="mlp_fwd", grid=(t_len // TM,),
        out_shape=(jax.ShapeDtypeStruct((t_len, d_ff), BF16), jax.ShapeDtypeStruct((t_len, d), F32)),
        in_specs=[_row_spec(TM, d), _const_spec((1, d)), _const_spec(w_up.shape), _const_spec(w_down.shape)],
        out_specs=(_row_spec(TM, d_ff), _row_spec(TM, d)),
        compiler_params=_seq_params(),
    )(x1, g_mlp, w_up, w_down)


def _head_fwd_bwd(x2, p, target, g_ple, g_final, w_pg, w_ple):
    t_len, d = x2.shape
    d_ple = p.shape[1]

    def body(x2_ref, p_ref, t_ref, gp_ref, gf_ref, wpg_ref, wple_ref,
             dx2_ref, h3_ref, dpre_ref, dpe_ref, loss_ref, dgp_ref, dgf_ref):
        @pl.when(pl.program_id(0) == 0)
        def _():
            loss_ref[...] = jnp.zeros_like(loss_ref)
            dgp_ref[...] = jnp.zeros_like(dgp_ref)
            dgf_ref[...] = jnp.zeros_like(dgf_ref)

        x2v = x2_ref[...]
        x2n, r3 = _rms(x2v)
        h3 = (x2n * gp_ref[...]).astype(BF16)
        h3_ref[...] = h3
        gate = _sigmoid(_dot(h3, wpg_ref[...]))
        pe = _dot(p_ref[...].astype(BF16), wple_ref[...])
        x3n, r4 = _rms(x2v + gate * pe)
        err = x3n * gf_ref[...] - t_ref[...]
        loss_ref[...] += jnp.full(loss_ref.shape, (0.5 / d) * jnp.sum(err * err), F32)
        dx3, dgf = _rms_bwd(err * (1.0 / d), x3n, r4, gf_ref[...])
        dgf_ref[...] += dgf
        dpe_ref[...] = (dx3 * gate).astype(BF16)
        dpre = (dx3 * pe * gate * (1.0 - gate)).astype(BF16)
        dpre_ref[...] = dpre
        dres, dgp = _rms_bwd(_dot_nt(dpre, wpg_ref[...]), x2n, r3, gp_ref[...])
        dgp_ref[...] += dgp
        dx2_ref[...] = dx3 + dres

    shp_b = jax.ShapeDtypeStruct((t_len, d), BF16)
    return pl.pallas_call(
        body, name="head_fwd_bwd", grid=(t_len // TM,),
        out_shape=(jax.ShapeDtypeStruct((t_len, d), F32), shp_b, shp_b, shp_b,
                   jax.ShapeDtypeStruct((1, LANES), F32), jax.ShapeDtypeStruct((1, d), F32),
                   jax.ShapeDtypeStruct((1, d), F32)),
        in_specs=[_row_spec(TM, d), _row_spec(TM, d_ple), _row_spec(TM, d), _const_spec((1, d)), _const_spec((1, d)),
                  _const_spec(w_pg.shape), _const_spec(w_ple.shape)],
        out_specs=(_row_spec(TM, d), _row_spec(TM, d), _row_spec(TM, d), _row_spec(TM, d),
                   _acc_spec((1, LANES)), _acc_spec((1, d)), _acc_spec((1, d))),
        compiler_params=_seq_params(),
    )(x2, p, target, g_ple, g_final, w_pg, w_ple)


def _mlp_bwd(dx2, a, x1, g_mlp, w_up, w_down):
    t_len, d = x1.shape
    d_ff = w_up.shape[1]
    ch = _ff_chunk(d_ff)

    def body(dx2_ref, a_ref, x1_ref, g_ref, wu_ref, wd_ref, dx1_ref, da_ref, h2_ref, dg_ref):
        @pl.when(pl.program_id(0) == 0)
        def _():
            dg_ref[...] = jnp.zeros_like(dg_ref)

        dx2v = dx2_ref[...]
        dx2b = dx2v.astype(BF16)
        xn, r = _rms(x1_ref[...])
        h2_ref[...] = (xn * g_ref[...]).T.astype(BF16)
        dh = jnp.zeros((TM, d), F32)
        for j in range(d_ff // ch):
            dact = _dot_nt(dx2b, wd_ref[j * ch:(j + 1) * ch, :])
            da = (dact * 2.0 * jnp.maximum(a_ref[:, j * ch:(j + 1) * ch].astype(F32), 0.0)).astype(BF16)
            da_ref[:, j * ch:(j + 1) * ch] = da
            dh = dh + _dot_nt(da, wu_ref[:, j * ch:(j + 1) * ch])
        dres, dg = _rms_bwd(dh, xn, r, g_ref[...])
        dg_ref[...] += dg
        dx1_ref[...] = dx2v + dres

    return pl.pallas_call(
        body, name="mlp_bwd", grid=(t_len // TM,),
        out_shape=(jax.ShapeDtypeStruct((t_len, d), F32), jax.ShapeDtypeStruct((t_len, d_ff), BF16),
                   jax.ShapeDtypeStruct((d, t_len), BF16), jax.ShapeDtypeStruct((1, d), F32)),
        in_specs=[_row_spec(TM, d), _row_spec(TM, d_ff), _row_spec(TM, d), _const_spec((1, d)),
                  _const_spec(w_up.shape), _const_spec(w_down.shape)],
        out_specs=(_row_spec(TM, d), _row_spec(TM, d_ff), _col_spec(d, TM), _acc_spec((1, d))),
        compiler_params=_seq_params(),
    )(dx2, a, x1, g_mlp, w_up, w_down)


def _mix_bwd(dx1, o_fox, o_sb, gl, w_bf, w_bs, w_out, b_gate):
    t_len, d = dx1.shape

    def body(dx1_ref, of_ref, os_ref, gl_ref, wbf_ref, wbs_ref, wo_ref, bg_ref,
             mg_ref, dbf_ref, dbs_ref, dgl_ref, dof_ref, dos_ref, dbg_ref):
        @pl.when(pl.program_id(0) == 0)
        def _():
            dbg_ref[...] = jnp.zeros_like(dbg_ref)

        dmerged = _dot_nt(dx1_ref[...].astype(BF16), wo_ref[...])
        br_f = _dot(of_ref[...], wbf_ref[...])
        br_s = _dot(os_ref[...], wbs_ref[...])
        ga = _sigmoid(gl_ref[:, :d].astype(F32) + bg_ref[0:1, :])
        gb = _sigmoid(gl_ref[:, d:].astype(F32) + bg_ref[1:2, :])
        mg_ref[...] = (ga * br_f + gb * br_s).astype(BF16)
        dbf = (dmerged * ga).astype(BF16)
        dbs = (dmerged * gb).astype(BF16)
        dbf_ref[...] = dbf
        dbs_ref[...] = dbs
        dla = dmerged * br_f * ga * (1.0 - ga)
        dlb = dmerged * br_s * gb * (1.0 - gb)
        dgl_ref[:, :d] = dla.astype(BF16)
        dgl_ref[:, d:] = dlb.astype(BF16)
        dbg_ref[0:1, :] += jnp.sum(dla, axis=0, keepdims=True)
        dbg_ref[1:2, :] += jnp.sum(dlb, axis=0, keepdims=True)
        dof_ref[...] = _dot_nt(dbf, wbf_ref[...]).astype(BF16)
        dos_ref[...] = _dot_nt(dbs, wbs_ref[...]).astype(BF16)

    shp_d = jax.ShapeDtypeStruct((t_len, d), BF16)
    shp_h = jax.ShapeDtypeStruct((t_len, D_BRANCH), BF16)
    return pl.pallas_call(
        body, name="mix_bwd", grid=(t_len // TM,),
        out_shape=(shp_d, shp_d, shp_d, jax.ShapeDtypeStruct((t_len, 2 * d), BF16), shp_h, shp_h,
                   jax.ShapeDtypeStruct((2, d), F32)),
        in_specs=[_row_spec(TM, d), _row_spec(TM, D_BRANCH), _row_spec(TM, D_BRANCH), _row_spec(TM, 2 * d),
                  _const_spec(w_bf.shape), _const_spec(w_bs.shape), _const_spec(w_out.shape), _const_spec(b_gate.shape)],
        out_specs=(_row_spec(TM, d), _row_spec(TM, d), _row_spec(TM, d), _row_spec(TM, 2 * d),
                   _row_spec(TM, D_BRANCH), _row_spec(TM, D_BRANCH), _acc_spec((2, d))),
        compiler_params=_seq_params(),
    )(dx1, o_fox, o_sb, gl, w_bf, w_bs, w_out, b_gate)


def _inproj_bwd(dqk_f, dv_f, dqkv_b, dgl, df, dx1, x, g_mix, w_pad, chip_sums):
    t_len, d = x.shape
    lay, _ = _pad_layout(d)
    slot_w = N_HEADS * HEAD_SLOT
    n = len(chip_sums)
    n_tiles = t_len // TM

    def body(dqk_ref, dvf_ref, db_ref, dgl_ref, df_ref, dx1_ref, x_ref, g_ref, w_ref, *rest):
        cs_refs, (dx_ref, dg_ref), out_refs = rest[:n], rest[n:n + 2], rest[n + 2:2 * n + 2]
        chips_start, chips_finish = _chips_plan(cs_refs, out_refs, *rest[2 * n + 2:])

        @pl.when(pl.program_id(0) == 0)
        def _():
            dg_ref[...] = jnp.zeros_like(dg_ref)
            chips_start()

        def back(piece, name):
            lo, hi = lay[name]
            return _dot(piece, w_ref[lo:hi, :])

        xn, r = _rms(x_ref[...])
        dh = (back(df_ref[...], "forget") + back(dgl_ref[...], "gates") + back(dqk_ref[0], "qf")
              + back(dqk_ref[1], "kf") + back(dvf_ref[...], "vf") + back(db_ref[0], "qb") + back(db_ref[1], "kb")
              + back(db_ref[2], "vb"))
        dres, dg = _rms_bwd(dh, xn, r, g_ref[...])
        dg_ref[...] += dg
        dx_ref[...] = dx1_ref[...] + dres
        pl.when(pl.program_id(0) == n_tiles - 1)(chips_finish)

    out = pl.pallas_call(
        body, name="inproj_bwd", grid=(n_tiles,),
        out_shape=[jax.ShapeDtypeStruct((t_len, d), F32), jax.ShapeDtypeStruct((1, d), F32)]
        + [jax.ShapeDtypeStruct(s.shape, s.dtype) for s in chip_sums],
        in_specs=[_row3_spec(2, TM, slot_w), _row_spec(TM, D_BRANCH), _row3_spec(3, TM, D_BRANCH),
                  _row_spec(TM, 2 * d), _row_spec(TM, LANES), _row_spec(TM, d), _row_spec(TM, d), _const_spec((1, d)),
                  _const_spec(w_pad.shape)] + _hbm_specs(n),
        out_specs=[_row_spec(TM, d), _acc_spec((1, d))] + _hbm_specs(n),
        scratch_shapes=_chips_sems(n),
        compiler_params=_seq_params(),
    )(dqk_f, dv_f, dqkv_b, dgl, df, dx1, x, g_mix, w_pad, *chip_sums)
    return out[0], out[1], out[2:]


def _cols_to_slabs(full):
    r, c8 = full.shape
    return full.reshape(r, N_DEV, c8 // N_DEV).transpose(1, 0, 2)


def _slabs_to_cols(slabs):
    n, r, c = slabs.shape
    return slabs.transpose(1, 0, 2).reshape(r, n * c)


def _win_sizes(d):
    return (D_BRANCH, D_BRANCH, D_BRANCH, N_HEADS, D_BRANCH, D_BRANCH, D_BRANCH, d, d)


def _split_win(w_t, d):
    out, off = [], 0
    for s in _win_sizes(d):
        out.append(w_t[off:off + s])
        off += s
    return out


def _to_slots(w_t):
    c = w_t.shape[1]
    return jnp.pad(w_t.reshape(N_HEADS, HEAD_DIM, c), ((0, 0), (0, HEAD_SLOT - HEAD_DIM), (0, 0))).reshape(-1, c)


def _from_slots(w_t):
    c = w_t.shape[1]
    return w_t.reshape(N_HEADS, HEAD_SLOT, c)[:, :HEAD_DIM].reshape(N_HEADS * HEAD_DIM, c)


def _pad_win(w_full_t, d):
    qa, ka, va, fa, qb, kb, vb, ga, gb = _split_win(w_full_t, d)
    scale = HEAD_DIM ** -0.5
    fpad = jnp.pad(fa, ((0, LANES - N_HEADS), (0, 0)))
    return jnp.concatenate([_to_slots(qa * scale), _to_slots(ka), va, qb * scale, kb, vb, ga, gb, fpad], axis=0)


def _unpad_dwin(dqk_f, dv_f, dqkv_b, dgates, dforget, d):
    scale = HEAD_DIM ** -0.5
    return jnp.concatenate([_from_slots(dqk_f[0]) * scale, _from_slots(dqk_f[1]), dv_f, dforget[:N_HEADS],
                            dqkv_b[0] * scale, dqkv_b[1], dqkv_b[2], dgates], axis=0)


def _c_lane_constants():
    head = jnp.arange(LANES)[:, None]
    lane = jnp.arange(N_HEADS * HEAD_SLOT)[None, :]
    in_head = (lane // HEAD_SLOT == head) & (head < N_HEADS)

    def place(first):
        return jnp.stack([(in_head & (lane % HEAD_SLOT == first + j)) for j in range(3)]).astype(BF16)

    def ones(first):
        off = lane % HEAD_SLOT
        return ((off >= first) & (off < first + 3)).astype(F32)

    return place(C_TERMS_Q), place(C_TERMS_K), ones(C_ONES_Q), ones(C_ONES_K)


def _pad_rows(a, rows):
    return jnp.pad(a, [(0, 0)] * (a.ndim - 2) + [(0, rows - a.shape[-2]), (0, 0)])


def kernel(x, p, g_mix, w_in, b_forget, b_gate, w_branch_fox, w_branch_sb, w_out, g_mlp, w_up, w_down, g_ple, w_ple_gate, w_ple, g_final, loss_target, m_g_mix, m_w_in, m_b_forget, m_b_gate, m_w_branch_fox, m_w_branch_sb, m_w_out, m_g_mlp, m_w_up, m_w_down, m_g_ple, m_w_ple_gate, m_w_ple, m_g_final, v_g_mix, v_w_in, v_b_forget, v_b_gate, v_w_branch_fox, v_w_branch_sb, v_w_out, v_g_mlp, v_w_up, v_w_down, v_g_ple, v_w_ple_gate, v_w_ple, v_g_final):
    batch, seq, d = x.shape
    t_len = batch * seq
    d_ple = p.shape[-1]
    d_ff = w_up.shape[-1] * N_DEV
    dn = d // N_DEV
    fn = d_ff // N_DEV
    my_c = lax.axis_index("c")
    my_dev = 4 * lax.axis_index("x") + 2 * lax.axis_index("y") + my_c

    bg_hi = b_gate[0].astype(BF16)
    bg_r = b_gate[0] - bg_hi.astype(F32)
    bg_mid = bg_r.astype(BF16)
    bg_lo = (bg_r - bg_mid.astype(F32)).astype(BF16)
    narrow_rows = 2 * D_BRANCH + d_ple + 6
    narrow_rows_pad = -(-narrow_rows // 16) * 16
    narrow = _pad_rows(jnp.concatenate(
        [w_branch_fox[0].astype(BF16), w_branch_sb[0].astype(BF16), w_ple[0].astype(BF16), bg_hi, bg_mid, bg_lo],
        axis=0), narrow_rows_pad)
    g_in, = _all_gather([w_in[0].T.astype(BF16)])
    w_pad = _pad_win(g_in.reshape(-1, d), d)
    bf_pad = jnp.pad(b_forget, ((0, 0), (0, LANES - N_HEADS)))
    place_q, place_k, ones_q, ones_k = _c_lane_constants()

    x2d = x.reshape(t_len, d)
    p2d = p.reshape(t_len, d_ple)
    tgt2d = loss_target.reshape(t_len, d)
    qf, kf, kft, vf, vft, qkvb, kbt, vbt, gl, fpre, h1 = _inproj_fwd(
        x2d, g_mix, w_pad, bf_pad, place_q, place_k, ones_q, ones_k, seq)
    o_sb, ltot, (g_up, g_out, g_down, g_pg, g_narrow) = _sb_fwd(qkvb, vbt, batch, seq, [
        w_up[0].astype(BF16), w_out[0].astype(BF16), w_down[0].astype(BF16), w_ple_gate[0].astype(BF16), narrow])
    o_fox, lse = _fox_fwd(qf, kf, vft, batch, seq)
    w_up_full = _slabs_to_cols(g_up)
    w_out_full = g_out.reshape(d, d)
    w_down_full = g_down.reshape(d_ff, d)
    w_pg_full = g_pg.reshape(d, d)
    w_bf_full = _slabs_to_cols(g_narrow[:, :D_BRANCH])
    w_bs_full = _slabs_to_cols(g_narrow[:, D_BRANCH:2 * D_BRANCH])
    w_ple_full = _slabs_to_cols(g_narrow[:, 2 * D_BRANCH:2 * D_BRANCH + d_ple])
    bg_terms = g_narrow[:, 2 * D_BRANCH + d_ple:narrow_rows].astype(F32)
    b_gate_full = _slabs_to_cols(bg_terms[:, 0:2] + bg_terms[:, 2:4] + bg_terms[:, 4:6])
    x1 = _mix_fwd(o_fox, o_sb, gl, x2d, w_bf_full, w_bs_full, w_out_full, b_gate_full)
    a_up, x2 = _mlp_fwd(x1, g_mlp, w_up_full, w_down_full)

    dx2, h3, dpre, dpe, loss_acc, dg_ple, dg_final = _head_fwd_bwd(
        x2, p2d, tgt2d, g_ple, g_final.reshape(1, d), w_pg_full, w_ple_full)
    dx1, da_up, h2t, dg_mlp = _mlp_bwd(dx2, a_up, x1, g_mlp, w_up_full, w_down_full)
    merged, dbr_f, dbr_s, dgl, do_fox, do_sb, dbg = _mix_bwd(
        dx1, o_fox, o_sb, gl, w_bf_full, w_bs_full, w_out_full, b_gate_full)

    def column_shards(name, lhs, rhs, lhs_t=False):
        if (rhs.shape[-1] // N_DEV) % (4 * LANES) == 0:
            return _matmul_tn(name, lhs, rhs, slabs=True, lhs_t=lhs_t)
        return _cols_to_slabs(_matmul_tn(name, lhs, rhs, lhs_t=lhs_t))

    if fn % (4 * LANES) == 0:
        part_up, = _matmul_tn_once("dw_up", [h2t], da_up, slabs=True, lhs_t=True)
    else:
        part_up = column_shards("dw_up", h2t, da_up, lhs_t=True)
    part_out = _matmul_tn("dw_out", merged, dx1).reshape(N_DEV, dn, d)
    part_down = _matmul_tn_once("dw_down", [a_up], dx2, relu2=True)[0].reshape(N_DEV, fn, d)
    part_pg = _matmul_tn("dw_ple_gate", h3, dpre).reshape(N_DEV, dn, d)
    part_narrow = _pad_rows(jnp.concatenate(
        [column_shards("dw_branch_fox", o_fox, dbr_f), column_shards("dw_branch_sb", o_sb, dbr_s),
         column_shards("dw_ple", p2d, dpe)], axis=1), narrow_rows_pad)
    early = [part_up, part_out, part_down, part_pg, lax.optimization_barrier(part_narrow)]

    dqk_f, dv_f, dc_queries, dc_keys, early_recv = _fox_bwd(
        qf, kf, kft, vf, o_fox, do_fox, lse, batch, seq, early)
    early_sums = [_pair_add("pair_add_%d" % i, pt, rc, my_c) for i, (pt, rc) in enumerate(zip(early, early_recv))]
    dqkv_b, (s_up, s_out, s_down, s_pg, s_narrow) = _sb_bwd(qkvb, kbt, do_sb, ltot, batch, seq, early_sums)
    dcq_tok = dc_queries.reshape(batch, N_HEADS, seq).transpose(0, 2, 1).reshape(t_len, N_HEADS)
    dck_tok = dc_keys[..., :NH].transpose(0, 2, 1, 3).reshape(t_len, N_HEADS)
    lane_pad = ((0, 0), (0, LANES - N_HEADS))
    df, db_forget = _forget_bwd(jnp.pad(dcq_tok, lane_pad), jnp.pad(dck_tok, lane_pad), fpre, batch, seq)

    gw_in = _unpad_dwin(*_matmul_tn_once("dw_in_fox_qk", [dqk_f], h1),
                        *_matmul_tn_once("dw_in_rest", [dv_f, dqkv_b, dgl, df], h1), d)
    part_in = lax.optimization_barrier(gw_in.reshape(N_DEV, -1, d))
    recv_in, = _rs_core_pair("reduce_scatter_core_pair_w_in", [part_in])
    grad_x, dg_mix, (s_in,) = _inproj_bwd(dqk_f, dv_f, dqkv_b, dgl, df, dx1, x2d, g_mix, w_pad,
                                          [_pair_add("pair_add_w_in", part_in, recv_in, my_c)])

    small = jnp.concatenate([
        dg_mix, dg_mlp, dg_ple, dg_final, jnp.pad(db_forget[:, :N_HEADS], ((0, 0), (0, d - N_HEADS))), dbg,
        jnp.pad(loss_acc[:, :1], ((0, 0), (0, d - 1)))], axis=0)
    small = _all_reduce_small(small)
    loss = small[7, 0]
    small_grads = {
        "g_mix": small[0:1], "g_mlp": small[1:2], "g_ple": small[2:3], "g_final": small[3:4],
        "b_forget": small[4:5, :N_HEADS],
        "b_gate": lax.dynamic_slice_in_dim(small[5:7], my_dev * dn, dn, axis=1),
    }

    weights = {"g_mix": g_mix, "w_in": w_in, "b_forget": b_forget, "b_gate": b_gate, "w_branch_fox": w_branch_fox,
               "w_branch_sb": w_branch_sb, "w_out": w_out, "g_mlp": g_mlp, "w_up": w_up, "w_down": w_down,
               "g_ple": g_ple, "w_ple_gate": w_ple_gate, "w_ple": w_ple, "g_final": g_final}
    m_in = {"g_mix": m_g_mix, "w_in": m_w_in, "b_forget": m_b_forget, "b_gate": m_b_gate,
            "w_branch_fox": m_w_branch_fox, "w_branch_sb": m_w_branch_sb, "w_out": m_w_out, "g_mlp": m_g_mlp,
            "w_up": m_w_up, "w_down": m_w_down, "g_ple": m_g_ple, "w_ple_gate": m_w_ple_gate, "w_ple": m_w_ple,
            "g_final": m_g_final}
    v_in = {"g_mix": v_g_mix, "w_in": v_w_in, "b_forget": v_b_forget, "b_gate": v_b_gate,
            "w_branch_fox": v_w_branch_fox, "w_branch_sb": v_w_branch_sb, "w_out": v_w_out, "g_mlp": v_g_mlp,
            "w_up": v_w_up, "w_down": v_w_down, "g_ple": v_g_ple, "w_ple_gate": v_w_ple_gate, "w_ple": v_w_ple,
            "g_final": v_g_final}
    names = list(weights)

    def as2d(a):
        return a.reshape(-1, a.shape[-1])

    result = {}
    big = {"w_up": (s_up, 0), "w_out": (s_out, 0), "w_down": (s_down, 0), "w_ple_gate": (s_pg, 0),
           "w_branch_fox": (s_narrow, 0), "w_branch_sb": (s_narrow, D_BRANCH), "w_ple": (s_narrow, 2 * D_BRANCH)}
    for n, (parts, off) in big.items():
        result[n] = _adamw_parts("adamw_" + n, as2d(weights[n]), parts, off, as2d(m_in[n]), as2d(v_in[n]))
    result["w_in"] = tuple(r.T for r in _adamw_parts("adamw_w_in", w_in[0].T, s_in, 0, m_w_in[0].T, v_w_in[0].T))
    small_names = list(small_grads)
    small_out = _adamw_small([(as2d(weights[n]), small_grads[n], as2d(m_in[n]), as2d(v_in[n])) for n in small_names])
    for n, (dlt, nm, nv) in zip(small_names, small_out):
        result[n] = (small_grads[n], dlt, nm, nv)
    outs = [[result[n][k].reshape(weights[n].shape) for n in names] for k in range(4)]
    return (loss, grad_x.reshape(x.shape), *outs[0], *outs[1], *outs[2], *outs[3])
```

```python
import jax
import jax.numpy as jnp
from jax import lax
from jax.experimental import pallas as pl
from jax.experimental.pallas import tpu as pltpu

F32 = jnp.float32
BF16 = jnp.bfloat16

HEAD_DIM = 64
N_HEADS = 8
D_BRANCH = N_HEADS * HEAD_DIM
EPS = 1e-6
ADAM_LR = 0.001
ADAM_B1 = 0.9
ADAM_B2 = 0.999
ADAM_EPS = 1e-08
ADAM_WD = 0.01
ADAM_STEP = 10

N_DEV = 8
LANES = 128
TM = 256
TQ = 256
TK = 256
NH = 4
HEAD_SLOT = 128
C_TERMS_Q = 64
C_ONES_K = 64
C_TERMS_K = 67
C_ONES_Q = 67
NEG = -1e30
VMEM_LIMIT = 56 * 1024 * 1024
MESH = pl.DeviceIdType.MESH


def _dot(a, b):
    return jnp.dot(a, b, preferred_element_type=F32)


def _dot_nt(a, b):
    return lax.dot_general(a, b, (((1,), (1,)), ((), ())), preferred_element_type=F32)


def _dot_tn(a, b):
    return lax.dot_general(a, b, (((0,), (0,)), ((), ())), preferred_element_type=F32)


def _sigmoid(x):
    return 1.0 / (1.0 + jnp.exp(-x))


def _softplus(x):
    return jnp.maximum(x, 0.0) + jnp.log(1.0 + jnp.exp(-jnp.abs(x)))


def _split2(x):
    hi = x.astype(BF16)
    lo = (x - hi.astype(F32)).astype(BF16)
    return hi, lo


def _split3(x):
    hi = x.astype(BF16)
    r = x - hi.astype(F32)
    mid = r.astype(BF16)
    lo = (r - mid.astype(F32)).astype(BF16)
    return hi, mid, lo


def _rows_dot_mask(x, mask_bf16):
    hi, lo = _split2(x)
    return _dot(hi, mask_bf16) + _dot(lo, mask_bf16)


def _tri(n, rel):
    r = lax.broadcasted_iota(jnp.int32, (n, n), 0)
    c = lax.broadcasted_iota(jnp.int32, (n, n), 1)
    return rel(r, c).astype(BF16)


def _rms(x):
    r = lax.rsqrt(jnp.mean(x * x, axis=-1, keepdims=True) + EPS)
    return x * r, r


def _rms_bwd(dh, xn, r, g):
    dxn = dh * g
    dx = r * (dxn - xn * jnp.mean(dxn * xn, axis=-1, keepdims=True))
    return dx, jnp.sum(dh * xn, axis=0, keepdims=True)


def _row_spec(tm, cols):
    return pl.BlockSpec((tm, cols), lambda i: (i, 0))


def _row3_spec(g, tm, cols):
    return pl.BlockSpec((g, tm, cols), lambda i: (0, i, 0))


def _col_spec(rows, tm):
    return pl.BlockSpec((rows, tm), lambda i: (0, i))


def _const_spec(shape):
    nd = len(shape)
    return pl.BlockSpec(shape, lambda i: (0,) * nd, pipeline_mode=pl.Buffered(1))


def _acc_spec(shape):
    nd = len(shape)
    return pl.BlockSpec(shape, lambda i: (0,) * nd)


def _seq_params():
    return pltpu.CompilerParams(dimension_semantics=("arbitrary",), vmem_limit_bytes=VMEM_LIMIT)


def _mesh_pos():
    return lax.axis_index("x"), lax.axis_index("y"), lax.axis_index("c")


def _other_chips(x, y):
    return [(1 - x, y), (x, 1 - y), (1 - x, 1 - y)]


def _hbm_specs(n):
    return [pl.BlockSpec(memory_space=pl.ANY)] * n


def _gather_plan(x_refs, out_refs, send_sems, recv_sems, local_sems):
    n = len(x_refs)
    x, y, c = _mesh_pos()
    me, sibling = (x, y, c), (x, y, 1 - c)
    chips = _other_chips(x, y)

    def index(px, py, pc):
        return 4 * px + 2 * py + pc

    def copy(a, k, block, to, src=None):
        slab = out_refs[a].at[index(*block)]
        return pltpu.make_async_remote_copy(
            src_ref=slab if src is None else src, dst_ref=slab,
            send_sem=send_sems.at[7 * a + k], recv_sem=recv_sems.at[7 * a + k], device_id=to, device_id_type=MESH)

    mine = [pltpu.make_async_copy(x_refs[a], out_refs[a].at[index(*me)], local_sems.at[a]) for a in range(n)]
    first = []
    for a in range(n):
        first.append(copy(a, 0, me, sibling, src=x_refs[a]))
        first += [copy(a, 1 + j, me, (cx, cy, c), src=x_refs[a]) for j, (cx, cy) in enumerate(chips)]

    def start():
        for cp in mine + first:
            cp.start()

    def finish():
        passed = []
        for j, (cx, cy) in enumerate(chips):
            for a in range(n):
                copy(a, 1 + j, (cx, cy, c), me).wait_recv()
                passed.append(copy(a, 4 + j, (cx, cy, c), sibling))
                passed[-1].start()
        for a in range(n):
            copy(a, 0, sibling, me).wait_recv()
            for j, (cx, cy) in enumerate(chips):
                copy(a, 4 + j, (cx, cy, 1 - c), me).wait_recv()
        for cp in first + passed:
            cp.wait_send()
        for cp in mine:
            cp.wait()

    return start, finish


def _gather_shapes(shards):
    return [jax.ShapeDtypeStruct((N_DEV,) + s.shape, s.dtype) for s in shards]


def _gather_sems(n):
    return [pltpu.SemaphoreType.DMA((7 * n,)), pltpu.SemaphoreType.DMA((7 * n,)), pltpu.SemaphoreType.DMA((n,))]


def _all_gather(shards):
    n = len(shards)

    def body(*refs):
        start, finish = _gather_plan(refs[:n], refs[n:2 * n], *refs[2 * n:])
        start()
        finish()

    return pl.pallas_call(
        body, name="all_gather_weights", out_shape=_gather_shapes(shards),
        in_specs=_hbm_specs(n), out_specs=_hbm_specs(n), scratch_shapes=_gather_sems(n),
    )(*shards)


def _pair_plan(p_refs, recv_refs, send_sems, recv_sems):
    n = len(p_refs)
    x, y, c = _mesh_pos()
    sibling = (x, y, 1 - c)

    def start():
        for a in range(n):
            for chip in range(4):
                pltpu.make_async_remote_copy(
                    src_ref=p_refs[a].at[2 * chip + (1 - c)], dst_ref=recv_refs[a].at[chip],
                    send_sem=send_sems.at[a], recv_sem=recv_sems.at[a], device_id=sibling, device_id_type=MESH).start()

    def finish():
        for a in range(n):
            pltpu.make_async_remote_copy(
                src_ref=recv_refs[a], dst_ref=recv_refs[a], send_sem=send_sems.at[a], recv_sem=recv_sems.at[a],
                device_id=sibling, device_id_type=MESH).wait()

    return start, finish


def _pair_shapes(partials):
    return [jax.ShapeDtypeStruct((4,) + s.shape[1:], s.dtype) for s in partials]


def _pair_sems(n):
    return [pltpu.SemaphoreType.DMA((n,)), pltpu.SemaphoreType.DMA((n,))]


def _rs_core_pair(name, partials):
    n = len(partials)

    def body(*refs):
        start, finish = _pair_plan(refs[:n], refs[n:2 * n], *refs[2 * n:])
        start()
        finish()

    return pl.pallas_call(
        body, name=name, out_shape=_pair_shapes(partials),
        in_specs=_hbm_specs(n), out_specs=_hbm_specs(n), scratch_shapes=_pair_sems(n),
    )(*partials)


def _chips_plan(cs_refs, out_refs, send_sems, recv_sems, local_sems):
    n = len(cs_refs)
    x, y, c = _mesh_pos()
    chip = 2 * x + y
    chips = _other_chips(x, y)
    mine = [pltpu.make_async_copy(cs_refs[a].at[chip], out_refs[a].at[chip], local_sems.at[a]) for a in range(n)]
    sends = [pltpu.make_async_remote_copy(
        src_ref=cs_refs[a].at[2 * cx + cy], dst_ref=out_refs[a].at[chip],
        send_sem=send_sems.at[3 * a + j], recv_sem=recv_sems.at[3 * a + j],
        device_id=(cx, cy, c), device_id_type=MESH) for a in range(n) for j, (cx, cy) in enumerate(chips)]

    def start():
        for cp in mine + sends:
            cp.start()

    def finish():
        for a in range(n):
            for j, (cx, cy) in enumerate(chips):
                pltpu.make_async_remote_copy(
                    src_ref=cs_refs[a].at[chip], dst_ref=out_refs[a].at[2 * cx + cy],
                    send_sem=send_sems.at[3 * a + j], recv_sem=recv_sems.at[3 * a + j],
                    device_id=(x, y, c), device_id_type=MESH).wait_recv()
        for cp in sends:
            cp.wait_send()
        for cp in mine:
            cp.wait()

    return start, finish


def _chips_sems(n):
    return [pltpu.SemaphoreType.DMA((3 * n,)), pltpu.SemaphoreType.DMA((3 * n,)), pltpu.SemaphoreType.DMA((n,))]


def _all_reduce_small(vec):
    rows, cols = vec.shape

    def body(x_ref, land_ref, sum_ref, send_sems, recv_sems):
        x, y, c = _mesh_pos()
        me = 4 * x + 2 * y + c
        land_ref[me] = x_ref[...]
        flips = [(fx, fy, fc) for fx in (0, 1) for fy in (0, 1) for fc in (0, 1)][1:]

        def flipped(f):
            return tuple((1 - v) if b else v for v, b in zip((x, y, c), f))

        sends = []
        for k, f in enumerate(flips):
            sends.append(pltpu.make_async_remote_copy(
                src_ref=x_ref, dst_ref=land_ref.at[me], send_sem=send_sems.at[k], recv_sem=recv_sems.at[k],
                device_id=flipped(f), device_id_type=MESH))
            sends[-1].start()
        for k, f in enumerate(flips):
            px, py, pc = flipped(f)
            pltpu.make_async_remote_copy(
                src_ref=x_ref, dst_ref=land_ref.at[4 * px + 2 * py + pc], send_sem=send_sems.at[k],
                recv_sem=recv_sems.at[k], device_id=(x, y, c), device_id_type=MESH).wait_recv()
        for cp in sends:
            cp.wait_send()
        total = land_ref[0]
        for d in range(1, N_DEV):
            total = total + land_ref[d]
        sum_ref[...] = total

    vm = pl.BlockSpec(memory_space=pltpu.VMEM)
    return pl.pallas_call(
        body, name="all_reduce_small",
        out_shape=(jax.ShapeDtypeStruct((N_DEV, rows, cols), F32), jax.ShapeDtypeStruct((rows, cols), F32)),
        in_specs=[vm], out_specs=(vm, vm),
        scratch_shapes=[pltpu.SemaphoreType.DMA((7,)), pltpu.SemaphoreType.DMA((7,))],
    )(vec)[1]


def _block_rows(rows, cols, itemsize, align, row_off=0):
    best = None
    for t in range(align, rows + 1, align):
        if rows % t == 0 and row_off % t == 0 and t * cols * itemsize <= (1 << 20):
            best = t
    return rows if best is None else best


def _pair_add(name, partial, recv, my_c):
    _, rows, cols = partial.shape
    br = _block_rows(rows, cols, 2, 16)

    def body(c_ref, a_ref, b_ref, o_ref):
        o_ref[...] = (a_ref[...].astype(F32) + b_ref[...].astype(F32)).astype(BF16)

    return pl.pallas_call(
        body, name=name,
        grid_spec=pltpu.PrefetchScalarGridSpec(
            num_scalar_prefetch=1, grid=(4, rows // br),
            in_specs=[pl.BlockSpec((None, None, br, cols), lambda j, i, c_ref: (j, c_ref[0], i, 0)),
                      pl.BlockSpec((None, br, cols), lambda j, i, c_ref: (j, i, 0))],
            out_specs=pl.BlockSpec((None, br, cols), lambda j, i, c_ref: (j, i, 0))),
        out_shape=jax.ShapeDtypeStruct((4, rows, cols), BF16),
    )(my_c.reshape(1).astype(jnp.int32), partial.reshape(4, 2, rows, cols), recv)


def _adam_update(w, g, m, v):
    nm = ADAM_B1 * m + (1.0 - ADAM_B1) * g
    nv = ADAM_B2 * v + (1.0 - ADAM_B2) * (g * g)
    m_hat = nm / (1.0 - ADAM_B1 ** ADAM_STEP)
    v_hat = nv / (1.0 - ADAM_B2 ** ADAM_STEP)
    return -ADAM_LR * (m_hat / (jnp.sqrt(v_hat) + ADAM_EPS) + ADAM_WD * w), nm, nv


def _adamw_parts(name, w, parts, row_off, m, v):
    rows, cols = w.shape
    tr = _block_rows(rows, cols, 4, 16, row_off)
    tc = cols
    if tr == rows and rows % 16 != 0 and cols % (2 * LANES) == 0:
        tc = 2 * LANES
    assert rows % tr == 0 and row_off % tr == 0 and (tc == cols or row_off == 0)
    off = row_off // tr

    def body(w_ref, p_ref, m_ref, v_ref, g_ref, d_ref, nm_ref, nv_ref):
        g = p_ref[0].astype(F32)
        for j in range(1, 4):
            g = g + p_ref[j].astype(F32)
        g_ref[...] = g
        d_ref[...], nm_ref[...], nv_ref[...] = _adam_update(w_ref[...], g, m_ref[...], v_ref[...])

    spec = pl.BlockSpec((tr, tc), lambda i, j: (i, j))
    shp = jax.ShapeDtypeStruct((rows, cols), F32)
    return pl.pallas_call(
        body, name=name, grid=(rows // tr, cols // tc), out_shape=(shp,) * 4,
        in_specs=[spec, pl.BlockSpec((4, tr, tc), lambda i, j: (0, off + i, j)), spec, spec], out_specs=(spec,) * 4,
    )(w, parts, m, v)


def _adamw_small(tensors):
    n = len(tensors)

    def body(*refs):
        ins, outs = refs[:4 * n], refs[4 * n:]
        for t in range(n):
            w_ref, g_ref, m_ref, v_ref = ins[4 * t:4 * t + 4]
            d, nm, nv = _adam_update(w_ref[...], g_ref[...], m_ref[...], v_ref[...])
            outs[3 * t][...], outs[3 * t + 1][...], outs[3 * t + 2][...] = d, nm, nv

    vm = pl.BlockSpec(memory_space=pltpu.VMEM)
    out = pl.pallas_call(
        body, name="adamw_small",
        out_shape=[jax.ShapeDtypeStruct(t[0].shape, F32) for t in tensors for _ in range(3)],
        in_specs=[vm] * (4 * n), out_specs=[vm] * (3 * n),
    )(*[a for t in tensors for a in t])
    return [tuple(out[3 * t:3 * t + 3]) for t in range(n)]


def _matmul_tn(name, a, b, relu2=False, slabs=False, lhs_t=False):
    a_groups = a.shape[0] if a.ndim == 3 else 0
    b_groups = b.shape[0] if b.ndim == 3 else 0
    groups = max(a_groups, b_groups, 1)
    assert not (a_groups and b_groups) and not (a_groups and lhs_t)
    a3 = a if a_groups else a[None]
    b3 = b if b_groups else b[None]
    t_len, k_len = a3.shape[1:][::-1] if lhs_t else a3.shape[1:]
    n_len = b3.shape[2]
    tt = min(t_len, 512)
    tk = min(k_len, 1024)
    tn = n_len // N_DEV if slabs else min(n_len, 1024)
    nt = t_len // tt
    assert not slabs or (groups == 1 and tn <= 1024)

    def body(a_ref, b_ref, o_ref, acc_ref):
        @pl.when(pl.program_id(3) == 0)
        def _():
            acc_ref[...] = jnp.zeros_like(acc_ref)

        av = a_ref[...]
        if relu2:
            av = jnp.square(jnp.maximum(av.astype(F32), 0.0))
        product = _dot if lhs_t else _dot_tn
        acc_ref[...] += product(av.astype(BF16), b_ref[...].astype(BF16))

        @pl.when(pl.program_id(3) == nt - 1)
        def _():
            o_ref[...] = acc_ref[...].astype(BF16)

    def a_group(g):
        return g if a_groups else 0

    def b_group(g):
        return g if b_groups else 0

    if slabs:
        out_shape = jax.ShapeDtypeStruct((N_DEV, k_len, tn), BF16)
        out_spec = pl.BlockSpec((None, tk, tn), lambda g, i, j, t: (j, i, 0))
    else:
        out_shape = jax.ShapeDtypeStruct((groups, k_len, n_len), BF16)
        out_spec = pl.BlockSpec((None, tk, tn), lambda g, i, j, t: (g, i, j))
    out = pl.pallas_call(
        body, name=name, grid=(groups, k_len // tk, n_len // tn, nt), out_shape=out_shape,
        in_specs=[pl.BlockSpec((None, tk, tt), lambda g, i, j, t: (a_group(g), i, t)) if lhs_t
                  else pl.BlockSpec((None, tt, tk), lambda g, i, j, t: (a_group(g), t, i)),
                  pl.BlockSpec((None, tt, tn), lambda g, i, j, t: (b_group(g), t, j))],
        out_specs=out_spec,
        scratch_shapes=[pltpu.VMEM((tk, tn), F32)],
        compiler_params=pltpu.CompilerParams(
            dimension_semantics=("parallel", "parallel", "parallel", "arbitrary"), vmem_limit_bytes=VMEM_LIMIT),
    )(a3, b3)
    return out if (slabs or a_groups or b_groups) else out[0]


def _matmul_tn_once(name, lhs_list, rhs, relu2=False, slabs=False, lhs_t=False):
    t_len, n_len = rhs.shape
    tt = min(t_len, 256)
    nt = t_len // tt
    n_lhs = len(lhs_list)
    assert not (lhs_t or slabs) or (n_lhs == 1 and lhs_list[0].ndim == 2)
    k_shapes = [(a.shape[0], n_len) if lhs_t else a.shape[:-2] + (a.shape[-1], n_len) for a in lhs_list]
    tn = n_len // N_DEV

    def body(*refs):
        a_refs, b_ref = refs[:n_lhs], refs[n_lhs]
        o_refs, acc_refs = refs[n_lhs + 1:2 * n_lhs + 1], refs[2 * n_lhs + 1:]
        step = pl.program_id(0)

        @pl.when(step == 0)
        def _():
            for acc in acc_refs:
                acc[...] = jnp.zeros_like(acc)

        bv = b_ref[...].astype(BF16)

        def piece(av):
            if relu2:
                av = jnp.square(jnp.maximum(av.astype(F32), 0.0))
            return (_dot if lhs_t else _dot_tn)(av.astype(BF16), bv)

        for a_ref, acc in zip(a_refs, acc_refs):
            if len(acc.shape) == 3:
                for g in range(acc.shape[0]):
                    acc[g] += piece(a_ref[g])
            else:
                acc[...] += piece(a_ref[...])

        @pl.when(step == nt - 1)
        def _():
            for o_ref, acc in zip(o_refs, acc_refs):
                if slabs:
                    for j in range(N_DEV):
                        o_ref[j] = acc[:, j * tn:(j + 1) * tn].astype(BF16)
                else:
                    o_ref[...] = acc[...].astype(BF16)

    def lhs_spec(a):
        if lhs_t:
            return pl.BlockSpec((a.shape[0], tt), lambda t: (0, t))
        if a.ndim == 3:
            return pl.BlockSpec((a.shape[0], tt, a.shape[2]), lambda t: (0, t, 0))
        return pl.BlockSpec((tt, a.shape[1]), lambda t: (t, 0))

    out_shapes = [(N_DEV, k_shapes[0][0], tn)] if slabs else k_shapes
    return pl.pallas_call(
        body, name=name, grid=(nt,),
        out_shape=[jax.ShapeDtypeStruct(s, BF16) for s in out_shapes],
        in_specs=[lhs_spec(a) for a in lhs_list] + [pl.BlockSpec((tt, n_len), lambda t: (t, 0))],
        out_specs=[_acc_spec(s) for s in out_shapes],
        scratch_shapes=[pltpu.VMEM(s, F32) for s in k_shapes],
        compiler_params=_seq_params(),
    )(*lhs_list, rhs)


def _pad_layout(d):
    names = ("qf", "kf", "vf", "qb", "kb", "vb", "gates", "forget")
    sizes = (N_HEADS * HEAD_SLOT, N_HEADS * HEAD_SLOT, D_BRANCH, D_BRANCH, D_BRANCH, D_BRANCH, 2 * d, LANES)
    out, off = {}, 0
    for n, s in zip(names, sizes):
        out[n] = (off, off + s)
        off += s
    return out, off


def _slot_rows(xt, extra):
    parts = []
    for h in range(N_HEADS):
        parts += [xt[h * HEAD_DIM:(h + 1) * HEAD_DIM, :], extra]
    return jnp.concatenate(parts, axis=0)


def _inproj_fwd(x, g_mix, w_pad, bf_pad, place_q, place_k, ones_q, ones_k, seq):
    t_len, d = x.shape
    lay, _ = _pad_layout(d)
    tiles_per_seq = seq // TM
    slot_w = N_HEADS * HEAD_SLOT

    def body(x_ref, g_ref, w_ref, bf_ref, pq_ref, pk_ref, oq_ref, ok_ref,
             qf_ref, kf_ref, kft_ref, vf_ref, vft_ref, qkvb_ref, kbt_ref, vbt_ref, gl_ref, fpre_ref, h_ref,
             carry_ref):
        @pl.when(pl.program_id(0) % tiles_per_seq == 0)
        def _():
            carry_ref[...] = jnp.zeros_like(carry_ref)

        def proj(name):
            lo, hi = lay[name]
            return _dot_nt(h, w_ref[lo:hi, :])

        xn, _ = _rms(x_ref[...])
        h = (xn * g_ref[...]).astype(BF16)
        fpre = proj("forget") + bf_ref[...]
        fpre_ref[...] = fpre
        logf = -_softplus(-fpre)
        lower = _tri(TM, lambda r, c: c <= r)
        hi, mid, lo = _split3(logf)
        c_val = carry_ref[...] + _dot(lower, hi) + _dot(lower, mid) + _dot(lower, lo)
        carry_ref[...] = carry_ref[...] + jnp.sum(logf, axis=0, keepdims=True)
        c3 = _split3(c_val)
        qf_ref[...] = (proj("qf") + sum(_dot(c3[j], pq_ref[j]) for j in range(3)) + oq_ref[...]).astype(BF16)
        kf = proj("kf") - sum(_dot(c3[j], pk_ref[j]) for j in range(3)) + ok_ref[...]
        kf_ref[...] = kf.astype(BF16)
        kft_ref[0] = kf.T.astype(BF16)
        row0 = (lax.broadcasted_iota(jnp.int32, (HEAD_DIM, TM), 0) == 0).astype(F32)
        zeros = jnp.zeros((HEAD_DIM, TM), F32)
        vf = proj("vf")
        vf_ref[...] = vf.astype(BF16)
        vft_ref[0] = _slot_rows(vf.T, row0).astype(BF16)
        qkvb_ref[0] = proj("qb").astype(BF16)
        kb = proj("kb")
        qkvb_ref[1] = kb.astype(BF16)
        kbt_ref[0] = _slot_rows(kb.T, zeros).astype(BF16)
        vb = proj("vb")
        qkvb_ref[2] = vb.astype(BF16)
        vbt_ref[0] = _slot_rows(vb.T, row0).astype(BF16)
        gl_ref[...] = proj("gates").astype(BF16)
        h_ref[...] = h

    n_tiles = t_len // TM
    slot_shape = jax.ShapeDtypeStruct((t_len, slot_w), BF16)
    t_shape = jax.ShapeDtypeStruct((n_tiles, slot_w, TM), BF16)
    t_spec = pl.BlockSpec((1, slot_w, TM), lambda i: (i, 0, 0))
    return pl.pallas_call(
        body, name="inproj_fwd", grid=(n_tiles,),
        out_shape=(slot_shape, slot_shape, t_shape, jax.ShapeDtypeStruct((t_len, D_BRANCH), BF16), t_shape,
                   jax.ShapeDtypeStruct((3, t_len, D_BRANCH), BF16), t_shape, t_shape,
                   jax.ShapeDtypeStruct((t_len, 2 * d), BF16), jax.ShapeDtypeStruct((t_len, LANES), F32),
                   jax.ShapeDtypeStruct((t_len, d), BF16)),
        in_specs=[_row_spec(TM, d), _const_spec((1, d)), _const_spec(w_pad.shape), _const_spec((1, LANES)),
                  _const_spec(place_q.shape), _const_spec(place_k.shape), _const_spec((1, slot_w)),
                  _const_spec((1, slot_w))],
        out_specs=(_row_spec(TM, slot_w), _row_spec(TM, slot_w), t_spec, _row_spec(TM, D_BRANCH), t_spec,
                   _row3_spec(3, TM, D_BRANCH), t_spec, t_spec, _row_spec(TM, 2 * d), _row_spec(TM, LANES),
                   _row_spec(TM, d)),
        scratch_shapes=[pltpu.VMEM((1, LANES), F32)],
        compiler_params=_seq_params(),
    )(x, g_mix, w_pad, bf_pad, place_q, place_k, ones_q, ones_k)


def _slot_spec(seq):
    return pl.BlockSpec((seq, NH * HEAD_SLOT), lambda b, g: (b, g))


def _slot2_spec(seq):
    return pl.BlockSpec((2, seq, NH * HEAD_SLOT), lambda b, g: (0, b, g))


def _group_spec(seq):
    return pl.BlockSpec((seq, NH * HEAD_DIM), lambda b, g: (b, g))


def _group3_spec(which, seq):
    return pl.BlockSpec((None, seq, NH * HEAD_DIM), lambda b, g: (which, b, g))


def _tblock_spec(seq):
    return pl.BlockSpec((seq // TK, NH * HEAD_SLOT, TK), lambda b, g: (b, g, 0))


def _qrow_spec(seq):
    return pl.BlockSpec((None, NH, seq // TQ, TQ), lambda b, g: (b, g, 0, 0))


def _stat_spec(seq):
    return pl.BlockSpec((None, None, seq, LANES), lambda b, g: (b, g, 0, 0))


def _attn_params():
    return pltpu.CompilerParams(dimension_semantics=("parallel", "parallel"), vmem_limit_bytes=VMEM_LIMIT)


def _serial_attn_params():
    return pltpu.CompilerParams(dimension_semantics=("arbitrary", "arbitrary"), vmem_limit_bytes=VMEM_LIMIT)


def _hcols(hh):
    return slice(hh * HEAD_DIM, (hh + 1) * HEAD_DIM)


def _hslot(hh):
    return slice(hh * HEAD_SLOT, (hh + 1) * HEAD_SLOT)


def _lane(hh):
    return slice(hh, hh + 1)


def _key_query_mask(rel):
    r = lax.broadcasted_iota(jnp.int32, (TK, TQ), 0)
    c = lax.broadcasted_iota(jnp.int32, (TK, TQ), 1)
    return rel(r, c)


def _heads_cat(vals):
    return jnp.concatenate(vals, axis=1)


def _untranspose(acc_t):
    return acc_t.T[:, :HEAD_DIM]


def _fox_fwd(qf, kf, vft, batch, seq):
    def body(q_ref, k_ref, vt_ref, o_ref, lse_ref, m_s, acc_s):
        causal = _key_query_mask(lambda r, c: r <= c)

        def tile(q0, kj, masked, n_k=1):
            krows = pl.ds(pl.multiple_of(kj * TK, TK), n_k * TK)
            heads = range(NH)
            sts = [_dot_nt(k_ref[krows, _hslot(hh)], q_ref[pl.ds(q0, TQ), _hslot(hh)]) for hh in heads]
            if masked:
                sts = [jnp.where(causal, st, NEG) for st in sts]
            m_olds = [m_s[hh] for hh in heads]
            m_news = [jnp.maximum(m_olds[hh], jnp.max(sts[hh], axis=0, keepdims=True)) for hh in heads]
            pts = [jnp.exp(sts[hh] - m_news[hh]).astype(BF16) for hh in heads]
            pvs = [sum(_dot(vt_ref[kj + i, _hslot(hh), :], pts[hh][i * TK:(i + 1) * TK]) for i in range(n_k))
                   for hh in heads]
            for hh in heads:
                acc_s[hh] = jnp.exp(m_olds[hh] - m_news[hh]) * acc_s[hh] + pvs[hh]
                m_s[hh] = m_news[hh]

        def q_loop(qi, _):
            q0 = pl.multiple_of(qi * TQ, TQ)
            m_s[...] = jnp.full(m_s.shape, NEG, F32)
            acc_s[...] = jnp.zeros_like(acc_s)

            def pair_loop(i, _):
                tile(q0, 2 * i, False, n_k=2)
                return 0

            lax.fori_loop(0, qi // 2, pair_loop, 0)
            pl.when(qi % 2 == 1)(lambda: tile(q0, qi - 1, False))
            tile(q0, qi, True)
            outs = []
            for hh in range(NH):
                total = acc_s[hh, HEAD_DIM:HEAD_DIM + 1, :]
                outs.append(_untranspose(acc_s[hh] / total))
                lse_ref[hh, pl.ds(qi, 1), :] = m_s[hh] + jnp.log(total)
            o_ref[pl.ds(q0, TQ), :] = _heads_cat(outs).astype(BF16)
            return 0

        lax.fori_loop(0, seq // TQ, q_loop, 0)

    return pl.pallas_call(
        body, name="fox_fwd", grid=(batch, N_HEADS // NH),
        out_shape=(jax.ShapeDtypeStruct((batch * seq, D_BRANCH), BF16),
                   jax.ShapeDtypeStruct((batch, N_HEADS, seq // TQ, TQ), F32)),
        in_specs=[_slot_spec(seq), _slot_spec(seq), _tblock_spec(seq)],
        out_specs=(_group_spec(seq), _qrow_spec(seq)),
        scratch_shapes=[pltpu.VMEM((NH, 1, TQ), F32), pltpu.VMEM((NH, HEAD_SLOT, TQ), F32)],
        compiler_params=_attn_params(),
    )(qf, kf, vft)


def _fox_bwd(qf, kf, kft, vf, o, do, lse, batch, seq, partials):
    n_q = seq // TQ
    n = len(partials)

    def body(q_ref, k_ref, kt_ref, v_ref, o_ref, do_ref, lse_ref, *rest):
        p_refs, (dqk_ref, dv_ref, dcq_ref, dck_ref), recv_refs = rest[:n], rest[n:n + 4], rest[n + 4:2 * n + 4]
        delta_s, dqt_acc, dk_s, dv_s = rest[2 * n + 4:2 * n + 8]
        pair_start, pair_finish = _pair_plan(p_refs, recv_refs, *rest[2 * n + 8:])
        first_step, last_step = _first_last_step()
        pl.when(first_step)(pair_start)
        causal = _key_query_mask(lambda r, c: r <= c)
        ones8 = jnp.ones((8, HEAD_DIM), BF16)
        dqt_acc[...] = jnp.zeros_like(dqt_acc)

        def prep(qi, _):
            rows = pl.ds(pl.multiple_of(qi * TQ, TQ), TQ)
            for hh in range(NH):
                hi, lo = _split2(do_ref[rows, _hcols(hh)].astype(F32) * o_ref[rows, _hcols(hh)].astype(F32))
                delta_s[hh, pl.ds(qi, 1), :] = (_dot_nt(ones8, hi) + _dot_nt(ones8, lo))[0:1, :]
            return 0

        lax.fori_loop(0, n_q, prep, 0)

        def tile(qi, kj, masked):
            rows = pl.ds(pl.multiple_of(qi * TQ, TQ), TQ)
            krows = pl.ds(pl.multiple_of(kj * TK, TK), TK)
            heads = range(NH)
            qs = [q_ref[rows, _hslot(hh)] for hh in heads]
            douts = [do_ref[rows, _hcols(hh)] for hh in heads]
            sts = [_dot_nt(k_ref[krows, _hslot(hh)], qs[hh]) for hh in heads]
            dps = [_dot_nt(v_ref[krows, _hcols(hh)], douts[hh]) for hh in heads]
            pts = [jnp.exp(sts[hh] - lse_ref[hh, pl.ds(qi, 1), :]) for hh in heads]
            if masked:
                pts = [jnp.where(causal, pt, 0.0) for pt in pts]
            dsts = [(pts[hh] * (dps[hh] - delta_s[hh, pl.ds(qi, 1), :])).astype(BF16) for hh in heads]
            for hh in heads:
                dv_s[hh] += _dot(pts[hh].astype(BF16), douts[hh])
                dk_s[hh] += _dot(dsts[hh], qs[hh])
                dqt_acc[hh, qi] += _dot(kt_ref[kj, _hslot(hh), :], dsts[hh])

        def k_loop(kj, _):
            krows = pl.ds(pl.multiple_of(kj * TK, TK), TK)
            dk_s[...] = jnp.zeros_like(dk_s)
            dv_s[...] = jnp.zeros_like(dv_s)
            tile(kj, kj, True)

            def q_loop(qi, _):
                tile(qi, kj, False)
                return 0

            lax.fori_loop(kj + 1, n_q, q_loop, 0)
            dqk_ref[1, krows, :] = _heads_cat([dk_s[hh] for hh in range(NH)]).astype(BF16)
            dv_ref[krows, :] = _heads_cat([dv_s[hh] for hh in range(NH)]).astype(BF16)
            for hh in range(NH):
                dck_ref[krows, _lane(hh)] = dk_s[hh, :, C_ONES_Q:C_ONES_Q + 1]
            return 0

        lax.fori_loop(0, seq // TK, k_loop, 0)

        def finish(qi, _):
            rows = pl.ds(pl.multiple_of(qi * TQ, TQ), TQ)
            dqk_ref[0, rows, :] = _heads_cat([dqt_acc[hh, qi].T for hh in range(NH)]).astype(BF16)
            for hh in range(NH):
                dcq_ref[hh, pl.ds(qi, 1), :] = dqt_acc[hh, qi, C_ONES_K:C_ONES_K + 1, :]
            return 0

        lax.fori_loop(0, n_q, finish, 0)
        pl.when(last_step)(pair_finish)

    out = pl.pallas_call(
        body, name="fox_bwd", grid=(batch, N_HEADS // NH),
        out_shape=[jax.ShapeDtypeStruct((2, batch * seq, N_HEADS * HEAD_SLOT), BF16),
                   jax.ShapeDtypeStruct((batch * seq, D_BRANCH), BF16),
                   jax.ShapeDtypeStruct((batch, N_HEADS, seq // TQ, TQ), F32),
                   jax.ShapeDtypeStruct((batch, N_HEADS // NH, seq, LANES), F32)] + _pair_shapes(partials),
        in_specs=[_slot_spec(seq), _slot_spec(seq), _tblock_spec(seq), _group_spec(seq), _group_spec(seq),
                  _group_spec(seq), _qrow_spec(seq)] + _hbm_specs(n),
        out_specs=[_slot2_spec(seq), _group_spec(seq), _qrow_spec(seq), _stat_spec(seq)] + _hbm_specs(n),
        scratch_shapes=[pltpu.VMEM((NH, n_q, TQ), F32), pltpu.VMEM((NH, n_q, HEAD_SLOT, TQ), F32),
                        pltpu.VMEM((NH, TK, HEAD_SLOT), F32), pltpu.VMEM((NH, TK, HEAD_DIM), F32)] + _pair_sems(n),
        compiler_params=_serial_attn_params(),
    )(qf, kf, kft, vf, o, do, lse, *partials)
    return out[0], out[1], out[2], out[3], out[4:]


def _first_last_step():
    step = pl.program_id(0) * pl.num_programs(1) + pl.program_id(1)
    return step == 0, step == pl.num_programs(0) * pl.num_programs(1) - 1


def _sb_fwd(qkvb, vbt, batch, seq, shards):
    n = len(shards)

    def body(q_ref, k_ref, vt_ref, *rest):
        x_refs, (o_ref, lt_ref), out_refs = rest[:n], rest[n:n + 2], rest[n + 2:2 * n + 2]
        run_s, acc_s = rest[2 * n + 2:2 * n + 4]
        gather_start, gather_finish = _gather_plan(x_refs, out_refs, *rest[2 * n + 4:])
        first_step, last_step = _first_last_step()
        pl.when(first_step)(gather_start)
        strict = _key_query_mask(lambda r, c: r < c)
        later = _tri(TK, lambda r, c: c > r)

        def tile(q0, kjs, masked):
            heads = range(NH)
            items = [(t, hh) for t in range(len(kjs)) for hh in heads]
            krows = [pl.ds(kj * TK if isinstance(kj, int) else pl.multiple_of(kj * TK, TK), TK) for kj in kjs]
            zts = [_dot_nt(k_ref[krows[t], _hcols(hh)], q_ref[pl.ds(q0, TQ), _hcols(hh)]) for t, hh in items]
            lgs = [-_softplus(zt) for zt in zts]
            if masked:
                lgs = [jnp.where(strict, lg, 0.0) for lg in lgs]
            parts = [_split2(lg) for lg in lgs]
            sufs = [_dot(later, hi) + _dot(later, lo) for hi, lo in parts]
            sums = [jnp.sum(lg, axis=0, keepdims=True) for lg in lgs]
            runs = {}
            for hh in heads:
                run = run_s[hh]
                for t in range(len(kjs)):
                    runs[t, hh] = run
                    run = run + sums[t * NH + hh]
                run_s[hh] = run
            ats = [jnp.exp(zts[i] + lgs[i] + runs[item] + sufs[i]) for i, item in enumerate(items)]
            if masked:
                ats = [jnp.where(strict, at, 0.0) for at in ats]
            for hh in heads:
                acc_s[hh] += sum(_dot(vt_ref[kjs[t], _hslot(hh), :], ats[t * NH + hh].astype(BF16))
                                 for t in range(len(kjs)))

        def q_loop(qi, _):
            q0 = pl.multiple_of(qi * TQ, TQ)
            run_s[...] = jnp.zeros_like(run_s)
            acc_s[...] = jnp.zeros_like(acc_s)
            tile(q0, [qi], True)

            def pair_loop(i, _):
                tile(q0, [qi - 1 - 2 * i, qi - 2 - 2 * i], False)
                return 0

            lax.fori_loop(0, qi // 2, pair_loop, 0)
            pl.when(qi % 2 == 1)(lambda: tile(q0, [0], False))
            o_ref[pl.ds(q0, TQ), :] = _heads_cat([_untranspose(acc_s[hh]) for hh in range(NH)]).astype(BF16)
            for hh in range(NH):
                lt_ref[hh, pl.ds(qi, 1), :] = run_s[hh]
            return 0

        lax.fori_loop(0, seq // TQ, q_loop, 0)
        pl.when(last_step)(gather_finish)

    out = pl.pallas_call(
        body, name="sb_fwd", grid=(batch, N_HEADS // NH),
        out_shape=[jax.ShapeDtypeStruct((batch * seq, D_BRANCH), BF16),
                   jax.ShapeDtypeStruct((batch, N_HEADS, seq // TQ, TQ), F32)] + _gather_shapes(shards),
        in_specs=[_group3_spec(0, seq), _group3_spec(1, seq), _tblock_spec(seq)] + _hbm_specs(n),
        out_specs=[_group_spec(seq), _qrow_spec(seq)] + _hbm_specs(n),
        scratch_shapes=[pltpu.VMEM((NH, 1, TQ), F32), pltpu.VMEM((NH, HEAD_SLOT, TQ), F32)] + _gather_sems(n),
        compiler_params=_serial_attn_params(),
    )(qkvb, qkvb, vbt, *shards)
    return out[0], out[1], out[2:]


def _sb_bwd(qkvb, kbt, do, ltot, batch, seq, chip_sums):
    n = len(chip_sums)

    def body(q_ref, k_ref, v_ref, kt_ref, do_ref, lt_ref, *rest):
        cs_refs, dqkv_ref, out_refs = rest[:n], rest[n], rest[n + 1:2 * n + 1]
        dk_acc, dv_acc, ls_s, gs_s, dqt_s = rest[2 * n + 1:2 * n + 6]
        chips_start, chips_finish = _chips_plan(cs_refs, out_refs, *rest[2 * n + 6:])
        first_step, last_step = _first_last_step()
        pl.when(first_step)(chips_start)
        strict = _key_query_mask(lambda r, c: r < c)
        upto = _tri(TK, lambda r, c: c <= r)
        before = _tri(TK, lambda r, c: c < r)
        dk_acc[...] = jnp.zeros_like(dk_acc)
        dv_acc[...] = jnp.zeros_like(dv_acc)

        def tile(qi, kj, masked):
            rows = pl.ds(pl.multiple_of(qi * TQ, TQ), TQ)
            krows = pl.ds(pl.multiple_of(kj * TK, TK), TK)
            heads = range(NH)
            qs = [q_ref[rows, _hcols(hh)] for hh in heads]
            douts = [do_ref[rows, _hcols(hh)] for hh in heads]
            zts = [_dot_nt(k_ref[krows, _hcols(hh)], qs[hh]) for hh in heads]
            das = [_dot_nt(v_ref[krows, _hcols(hh)], douts[hh]) for hh in heads]
            lgs = [-_softplus(zt) for zt in zts]
            if masked:
                lgs = [jnp.where(strict, lg, 0.0) for lg in lgs]
            parts = [_split2(lg) for lg in lgs]
            prefs = [_dot(upto, hi) + _dot(upto, lo) for hi, lo in parts]
            ats = [jnp.exp(zts[hh] + lgs[hh] + (lt_ref[hh, pl.ds(qi, 1), :] - ls_s[hh]) - prefs[hh]) for hh in heads]
            if masked:
                ats = [jnp.where(strict, at, 0.0) for at in ats]
            gts = [das[hh] * ats[hh] for hh in heads]
            us = [gs_s[hh] + _dot(before, gts[hh].astype(BF16)) for hh in heads]
            dzts = [(jnp.exp(lgs[hh]) * (gts[hh] + us[hh]) - us[hh]).astype(BF16) for hh in heads]
            for hh in heads:
                dk_acc[hh, krows, :] += _dot(dzts[hh], qs[hh])
                dv_acc[hh, krows, :] += _dot(ats[hh].astype(BF16), douts[hh])
                dqt_s[hh] += _dot(kt_ref[kj, _hslot(hh), :], dzts[hh])
                ls_s[hh] += jnp.sum(lgs[hh], axis=0, keepdims=True)
                gs_s[hh] += jnp.sum(gts[hh], axis=0, keepdims=True)

        def q_loop(qi, _):
            ls_s[...] = jnp.zeros_like(ls_s)
            gs_s[...] = jnp.zeros_like(gs_s)
            dqt_s[...] = jnp.zeros_like(dqt_s)

            def k_loop(kj, _):
                tile(qi, kj, False)
                return 0

            lax.fori_loop(0, qi, k_loop, 0)
            tile(qi, qi, True)
            dqkv_ref[0, pl.ds(pl.multiple_of(qi * TQ, TQ), TQ), :] = _heads_cat(
                [_untranspose(dqt_s[hh]) for hh in range(NH)]).astype(BF16)
            return 0

        lax.fori_loop(0, seq // TQ, q_loop, 0)
        dqkv_ref[1] = _heads_cat([dk_acc[hh] for hh in range(NH)]).astype(BF16)
        dqkv_ref[2] = _heads_cat([dv_acc[hh] for hh in range(NH)]).astype(BF16)
        pl.when(last_step)(chips_finish)

    out = pl.pallas_call(
        body, name="sb_bwd", grid=(batch, N_HEADS // NH),
        out_shape=[jax.ShapeDtypeStruct((3, batch * seq, D_BRANCH), BF16)]
        + [jax.ShapeDtypeStruct(s.shape, s.dtype) for s in chip_sums],
        in_specs=[_group3_spec(0, seq), _group3_spec(1, seq), _group3_spec(2, seq), _tblock_spec(seq),
                  _group_spec(seq), _qrow_spec(seq)] + _hbm_specs(n),
        out_specs=[pl.BlockSpec((3, seq, NH * HEAD_DIM), lambda b, g: (0, b, g))] + _hbm_specs(n),
        scratch_shapes=[pltpu.VMEM((NH, seq, HEAD_DIM), F32), pltpu.VMEM((NH, seq, HEAD_DIM), F32),
                        pltpu.VMEM((NH, 1, TQ), F32), pltpu.VMEM((NH, 1, TQ), F32),
                        pltpu.VMEM((NH, HEAD_SLOT, TQ), F32)] + _chips_sems(n),
        compiler_params=_serial_attn_params(),
    )(qkvb, qkvb, qkvb, kbt, do, ltot, *chip_sums)
    return out[0], out[1:]


def _forget_bwd(dcq_tok, dck_tok, fpre, batch, seq):
    t_len = batch * seq
    tiles = seq // TM

    def rev(i):
        return ((i // tiles) * tiles + (tiles - 1 - i % tiles), 0)

    def body(dcq_ref, dck_ref, f_ref, df_ref, db_ref, carry_ref):
        i = pl.program_id(0)

        @pl.when(i == 0)
        def _():
            db_ref[...] = jnp.zeros_like(db_ref)

        @pl.when(i % tiles == 0)
        def _():
            carry_ref[...] = jnp.zeros_like(carry_ref)

        dc = dcq_ref[...] - dck_ref[...]
        upper = _tri(TM, lambda r, c: c >= r)
        hi, mid, lo = _split3(dc)
        dlogf = carry_ref[...] + _dot(upper, hi) + _dot(upper, mid) + _dot(upper, lo)
        carry_ref[...] = carry_ref[...] + jnp.sum(dc, axis=0, keepdims=True)
        df = dlogf * _sigmoid(-f_ref[...])
        df_ref[...] = df.astype(BF16)
        db_ref[...] += jnp.sum(df, axis=0, keepdims=True)

    return pl.pallas_call(
        body, name="forget_bwd", grid=(t_len // TM,),
        out_shape=(jax.ShapeDtypeStruct((t_len, LANES), BF16), jax.ShapeDtypeStruct((1, LANES), F32)),
        in_specs=[pl.BlockSpec((TM, LANES), rev)] * 3,
        out_specs=(pl.BlockSpec((TM, LANES), rev), _acc_spec((1, LANES))),
        scratch_shapes=[pltpu.VMEM((1, LANES), F32)],
        compiler_params=_seq_params(),
    )(dcq_tok, dck_tok, fpre)


def _mix_fwd(o_fox, o_sb, gl, x, w_bf, w_bs, w_out, b_gate):
    t_len, d = x.shape

    def body(of_ref, os_ref, gl_ref, x_ref, wbf_ref, wbs_ref, wo_ref, bg_ref, x1_ref):
        br_f = _dot(of_ref[...], wbf_ref[...])
        br_s = _dot(os_ref[...], wbs_ref[...])
        ga = _sigmoid(gl_ref[:, :d].astype(F32) + bg_ref[0:1, :])
        gb = _sigmoid(gl_ref[:, d:].astype(F32) + bg_ref[1:2, :])
        merged = ga * br_f + gb * br_s
        x1_ref[...] = x_ref[...] + _dot(merged.astype(BF16), wo_ref[...])

    return pl.pallas_call(
        body, name="mix_fwd", grid=(t_len // TM,),
        out_shape=jax.ShapeDtypeStruct((t_len, d), F32),
        in_specs=[_row_spec(TM, D_BRANCH), _row_spec(TM, D_BRANCH), _row_spec(TM, 2 * d), _row_spec(TM, d),
                  _const_spec(w_bf.shape), _const_spec(w_bs.shape), _const_spec(w_out.shape), _const_spec(b_gate.shape)],
        out_specs=_row_spec(TM, d),
        compiler_params=_seq_params(),
    )(o_fox, o_sb, gl, x, w_bf, w_bs, w_out, b_gate)


def _ff_chunk(d_ff):
    return min(d_ff, 1024)


def _mlp_fwd(x1, g_mlp, w_up, w_down):
    t_len, d = x1.shape
    d_ff = w_up.shape[1]
    ch = _ff_chunk(d_ff)

    def body(x1_ref, g_ref, wu_ref, wd_ref, a_ref, x2_ref):
        x1v = x1_ref[...]
        xn, _ = _rms(x1v)
        h = (xn * g_ref[...]).astype(BF16)
        acc = x1v
        for j in range(d_ff // ch):
            a = _dot(h, wu_ref[:, j * ch:(j + 1) * ch])
            a_ref[:, j * ch:(j + 1) * ch] = a.astype(BF16)
            acc = acc + _dot(jnp.square(jnp.maximum(a, 0.0)).astype(BF16), wd_ref[j * ch:(j + 1) * ch, :])
        x2_ref[...] = acc

    return pl.pallas_call(
        body, name="mlp_fwd", grid=(t_len // TM,),
        out_shape=(jax.ShapeDtypeStruct((t_len, d_ff), BF16), jax.ShapeDtypeStruct((t_len, d), F32)),
        in_specs=[_row_spec(TM, d), _const_spec((1, d)), _const_spec(w_up.shape), _const_spec(w_down.shape)],
        out_specs=(_row_spec(TM, d_ff), _row_spec(TM, d)),
        compiler_params=_seq_params(),
    )(x1, g_mlp, w_up, w_down)


def _head_fwd_bwd(x2, p, target, g_ple, g_final, w_pg, w_ple):
    t_len, d = x2.shape
    d_ple = p.shape[1]

    def body(x2_ref, p_ref, t_ref, gp_ref, gf_ref, wpg_ref, wple_ref,
             dx2_ref, h3_ref, dpre_ref, dpe_ref, loss_ref, dgp_ref, dgf_ref):
        @pl.when(pl.program_id(0) == 0)
        def _():
            loss_ref[...] = jnp.zeros_like(loss_ref)
            dgp_ref[...] = jnp.zeros_like(dgp_ref)
            dgf_ref[...] = jnp.zeros_like(dgf_ref)

        x2v = x2_ref[...]
        x2n, r3 = _rms(x2v)
        h3 = (x2n * gp_ref[...]).astype(BF16)
        h3_ref[...] = h3
        gate = _sigmoid(_dot(h3, wpg_ref[...]))
        pe = _dot(p_ref[...].astype(BF16), wple_ref[...])
        x3n, r4 = _rms(x2v + gate * pe)
        err = x3n * gf_ref[...] - t_ref[...]
        loss_ref[...] += jnp.full(loss_ref.shape, (0.5 / d) * jnp.sum(err * err), F32)
        dx3, dgf = _rms_bwd(err * (1.0 / d), x3n, r4, gf_ref[...])
        dgf_ref[...] += dgf
        dpe_ref[...] = (dx3 * gate).astype(BF16)
        dpre = (dx3 * pe * gate * (1.0 - gate)).astype(BF16)
        dpre_ref[...] = dpre
        dres, dgp = _rms_bwd(_dot_nt(dpre, wpg_ref[...]), x2n, r3, gp_ref[...])
        dgp_ref[...] += dgp
        dx2_ref[...] = dx3 + dres

    shp_b = jax.ShapeDtypeStruct((t_len, d), BF16)
    return pl.pallas_call(
        body, name="head_fwd_bwd", grid=(t_len // TM,),
        out_shape=(jax.ShapeDtypeStruct((t_len, d), F32), shp_b, shp_b, shp_b,
                   jax.ShapeDtypeStruct((1, LANES), F32), jax.ShapeDtypeStruct((1, d), F32),
                   jax.ShapeDtypeStruct((1, d), F32)),
        in_specs=[_row_spec(TM, d), _row_spec(TM, d_ple), _row_spec(TM, d), _const_spec((1, d)), _const_spec((1, d)),
                  _const_spec(w_pg.shape), _const_spec(w_ple.shape)],
        out_specs=(_row_spec(TM, d), _row_spec(TM, d), _row_spec(TM, d), _row_spec(TM, d),
                   _acc_spec((1, LANES)), _acc_spec((1, d)), _acc_spec((1, d))),
        compiler_params=_seq_params(),
    )(x2, p, target, g_ple, g_final, w_pg, w_ple)


def _mlp_bwd(dx2, a, x1, g_mlp, w_up, w_down):
    t_len, d = x1.shape
    d_ff = w_up.shape[1]
    ch = _ff_chunk(d_ff)

    def body(dx2_ref, a_ref, x1_ref, g_ref, wu_ref, wd_ref, dx1_ref, da_ref, h2_ref, dg_ref):
        @pl.when(pl.program_id(0) == 0)
        def _():
            dg_ref[...] = jnp.zeros_like(dg_ref)

        dx2v = dx2_ref[...]
        dx2b = dx2v.astype(BF16)
        xn, r = _rms(x1_ref[...])
        h2_ref[...] = (xn * g_ref[...]).T.astype(BF16)
        dh = jnp.zeros((TM, d), F32)
        for j in range(d_ff // ch):
            dact = _dot_nt(dx2b, wd_ref[j * ch:(j + 1) * ch, :])
            da = (dact * 2.0 * jnp.maximum(a_ref[:, j * ch:(j + 1) * ch].astype(F32), 0.0)).astype(BF16)
            da_ref[:, j * ch:(j + 1) * ch] = da
            dh = dh + _dot_nt(da, wu_ref[:, j * ch:(j + 1) * ch])
        dres, dg = _rms_bwd(dh, xn, r, g_ref[...])
        dg_ref[...] += dg
        dx1_ref[...] = dx2v + dres

    return pl.pallas_call(
        body, name="mlp_bwd", grid=(t_len // TM,),
        out_shape=(jax.ShapeDtypeStruct((t_len, d), F32), jax.ShapeDtypeStruct((t_len, d_ff), BF16),
                   jax.ShapeDtypeStruct((d, t_len), BF16), jax.ShapeDtypeStruct((1, d), F32)),
        in_specs=[_row_spec(TM, d), _row_spec(TM, d_ff), _row_spec(TM, d), _const_spec((1, d)),
                  _const_spec(w_up.shape), _const_spec(w_down.shape)],
        out_specs=(_row_spec(TM, d), _row_spec(TM, d_ff), _col_spec(d, TM), _acc_spec((1, d))),
        compiler_params=_seq_params(),
    )(dx2, a, x1, g_mlp, w_up, w_down)


def _mix_bwd(dx1, o_fox, o_sb, gl, w_bf, w_bs, w_out, b_gate):
    t_len, d = dx1.shape

    def body(dx1_ref, of_ref, os_ref, gl_ref, wbf_ref, wbs_ref, wo_ref, bg_ref,
             mg_ref, dbf_ref, dbs_ref, dgl_ref, dof_ref, dos_ref, dbg_ref):
        @pl.when(pl.program_id(0) == 0)
        def _():
            dbg_ref[...] = jnp.zeros_like(dbg_ref)

        dmerged = _dot_nt(dx1_ref[...].astype(BF16), wo_ref[...])
        br_f = _dot(of_ref[...], wbf_ref[...])
        br_s = _dot(os_ref[...], wbs_ref[...])
        ga = _sigmoid(gl_ref[:, :d].astype(F32) + bg_ref[0:1, :])
        gb = _sigmoid(gl_ref[:, d:].astype(F32) + bg_ref[1:2, :])
        mg_ref[...] = (ga * br_f + gb * br_s).astype(BF16)
        dbf = (dmerged * ga).astype(BF16)
        dbs = (dmerged * gb).astype(BF16)
        dbf_ref[...] = dbf
        dbs_ref[...] = dbs
        dla = dmerged * br_f * ga * (1.0 - ga)
        dlb = dmerged * br_s * gb * (1.0 - gb)
        dgl_ref[:, :d] = dla.astype(BF16)
        dgl_ref[:, d:] = dlb.astype(BF16)
        dbg_ref[0:1, :] += jnp.sum(dla, axis=0, keepdims=True)
        dbg_ref[1:2, :] += jnp.sum(dlb, axis=0, keepdims=True)
        dof_ref[...] = _dot_nt(dbf, wbf_ref[...]).astype(BF16)
        dos_ref[...] = _dot_nt(dbs, wbs_ref[...]).astype(BF16)

    shp_d = jax.ShapeDtypeStruct((t_len, d), BF16)
    shp_h = jax.ShapeDtypeStruct((t_len, D_BRANCH), BF16)
    return pl.pallas_call(
        body, name="mix_bwd", grid=(t_len // TM,),
        out_shape=(shp_d, shp_d, shp_d, jax.ShapeDtypeStruct((t_len, 2 * d), BF16), shp_h, shp_h,
                   jax.ShapeDtypeStruct((2, d), F32)),
        in_specs=[_row_spec(TM, d), _row_spec(TM, D_BRANCH), _row_spec(TM, D_BRANCH), _row_spec(TM, 2 * d),
                  _const_spec(w_bf.shape), _const_spec(w_bs.shape), _const_spec(w_out.shape), _const_spec(b_gate.shape)],
        out_specs=(_row_spec(TM, d), _row_spec(TM, d), _row_spec(TM, d), _row_spec(TM, 2 * d),
                   _row_spec(TM, D_BRANCH), _row_spec(TM, D_BRANCH), _acc_spec((2, d))),
        compiler_params=_seq_params(),
    )(dx1, o_fox, o_sb, gl, w_bf, w_bs, w_out, b_gate)


def _inproj_bwd(dqk_f, dv_f, dqkv_b, dgl, df, dx1, x, g_mix, w_pad, chip_sums):
    t_len, d = x.shape
    lay, _ = _pad_layout(d)
    slot_w = N_HEADS * HEAD_SLOT
    n = len(chip_sums)
    n_tiles = t_len // TM

    def body(dqk_ref, dvf_ref, db_ref, dgl_ref, df_ref, dx1_ref, x_ref, g_ref, w_ref, *rest):
        cs_refs, (dx_ref, dg_ref), out_refs = rest[:n], rest[n:n + 2], rest[n + 2:2 * n + 2]
        chips_start, chips_finish = _chips_plan(cs_refs, out_refs, *rest[2 * n + 2:])

        @pl.when(pl.program_id(0) == 0)
        def _():
            dg_ref[...] = jnp.zeros_like(dg_ref)
            chips_start()

        def back(piece, name):
            lo, hi = lay[name]
            return _dot(piece, w_ref[lo:hi, :])

        xn, r = _rms(x_ref[...])
        dh = (back(df_ref[...], "forget") + back(dgl_ref[...], "gates") + back(dqk_ref[0], "qf")
              + back(dqk_ref[1], "kf") + back(dvf_ref[...], "vf") + back(db_ref[0], "qb") + back(db_ref[1], "kb")
              + back(db_ref[2], "vb"))
        dres, dg = _rms_bwd(dh, xn, r, g_ref[...])
        dg_ref[...] += dg
        dx_ref[...] = dx1_ref[...] + dres
        pl.when(pl.program_id(0) == n_tiles - 1)(chips_finish)

    out = pl.pallas_call(
        body, name="inproj_bwd", grid=(n_tiles,),
        out_shape=[jax.ShapeDtypeStruct((t_len, d), F32), jax.ShapeDtypeStruct((1, d), F32)]
        + [jax.ShapeDtypeStruct(s.shape, s.dtype) for s in chip_sums],
        in_specs=[_row3_spec(2, TM, slot_w), _row_spec(TM, D_BRANCH), _row3_spec(3, TM, D_BRANCH),
                  _row_spec(TM, 2 * d), _row_spec(TM, LANES), _row_spec(TM, d), _row_spec(TM, d), _const_spec((1, d)),
                  _const_spec(w_pad.shape)] + _hbm_specs(n),
        out_specs=[_row_spec(TM, d), _acc_spec((1, d))] + _hbm_specs(n),
        scratch_shapes=_chips_sems(n),
        compiler_params=_seq_params(),
    )(dqk_f, dv_f, dqkv_b, dgl, df, dx1, x, g_mix, w_pad, *chip_sums)
    return out[0], out[1], out[2:]


def _cols_to_slabs(full):
    r, c8 = full.shape
    return full.reshape(r, N_DEV, c8 // N_DEV).transpose(1, 0, 2)


def _slabs_to_cols(slabs):
    n, r, c = slabs.shape
    return slabs.transpose(1, 0, 2).reshape(r, n * c)


def _win_sizes(d):
    return (D_BRANCH, D_BRANCH, D_BRANCH, N_HEADS, D_BRANCH, D_BRANCH, D_BRANCH, d, d)


def _global_rows(slabs, lo, hi):
    r = slabs.shape[1]
    parts = []
    for s in range(slabs.shape[0]):
        a, b = max(lo, s * r), min(hi, (s + 1) * r)
        if a < b:
            parts.append(slabs[s, a - s * r:b - s * r])
    return parts[0] if len(parts) == 1 else jnp.concatenate(parts, axis=0)


def _split_win(slabs, d):
    out, off = [], 0
    for s in _win_sizes(d):
        out.append(_global_rows(slabs, off, off + s))
        off += s
    return out


def _to_slots(w_t):
    c = w_t.shape[1]
    return jnp.pad(w_t.reshape(N_HEADS, HEAD_DIM, c), ((0, 0), (0, HEAD_SLOT - HEAD_DIM), (0, 0))).reshape(-1, c)


def _from_slots(w_t):
    c = w_t.shape[1]
    return w_t.reshape(N_HEADS, HEAD_SLOT, c)[:, :HEAD_DIM].reshape(N_HEADS * HEAD_DIM, c)


def _pad_win(slabs, d):
    qa, ka, va, fa, qb, kb, vb, ga, gb = _split_win(slabs, d)
    scale = HEAD_DIM ** -0.5
    fpad = jnp.pad(fa, ((0, LANES - N_HEADS), (0, 0)))
    return jnp.concatenate([_to_slots(qa * scale), _to_slots(ka), va, qb * scale, kb, vb, ga, gb, fpad], axis=0)


def _unpad_dwin(dqk_f, dv_f, dqkv_b, dgates, dforget, d):
    scale = HEAD_DIM ** -0.5
    return jnp.concatenate([_from_slots(dqk_f[0]) * scale, _from_slots(dqk_f[1]), dv_f, dforget[:N_HEADS],
                            dqkv_b[0] * scale, dqkv_b[1], dqkv_b[2], dgates], axis=0)


def _c_lane_constants():
    head = jnp.arange(LANES)[:, None]
    lane = jnp.arange(N_HEADS * HEAD_SLOT)[None, :]
    in_head = (lane // HEAD_SLOT == head) & (head < N_HEADS)

    def place(first):
        return jnp.stack([(in_head & (lane % HEAD_SLOT == first + j)) for j in range(3)]).astype(BF16)

    def ones(first):
        off = lane % HEAD_SLOT
        return ((off >= first) & (off < first + 3)).astype(F32)

    return place(C_TERMS_Q), place(C_TERMS_K), ones(C_ONES_Q), ones(C_ONES_K)


def _pad_rows(a, rows):
    return jnp.pad(a, [(0, 0)] * (a.ndim - 2) + [(0, rows - a.shape[-2]), (0, 0)])


def kernel(x, p, g_mix, w_in, b_forget, b_gate, w_branch_fox, w_branch_sb, w_out, g_mlp, w_up, w_down, g_ple, w_ple_gate, w_ple, g_final, loss_target, m_g_mix, m_w_in, m_b_forget, m_b_gate, m_w_branch_fox, m_w_branch_sb, m_w_out, m_g_mlp, m_w_up, m_w_down, m_g_ple, m_w_ple_gate, m_w_ple, m_g_final, v_g_mix, v_w_in, v_b_forget, v_b_gate, v_w_branch_fox, v_w_branch_sb, v_w_out, v_g_mlp, v_w_up, v_w_down, v_g_ple, v_w_ple_gate, v_w_ple, v_g_final):
    batch, seq, d = x.shape
    t_len = batch * seq
    d_ple = p.shape[-1]
    d_ff = w_up.shape[-1] * N_DEV
    dn = d // N_DEV
    fn = d_ff // N_DEV
    my_c = lax.axis_index("c")
    my_dev = 4 * lax.axis_index("x") + 2 * lax.axis_index("y") + my_c

    bg_hi = b_gate[0].astype(BF16)
    bg_r = b_gate[0] - bg_hi.astype(F32)
    bg_mid = bg_r.astype(BF16)
    bg_lo = (bg_r - bg_mid.astype(F32)).astype(BF16)
    narrow_rows = 2 * D_BRANCH + d_ple + 6
    narrow_rows_pad = -(-narrow_rows // 16) * 16
    narrow = _pad_rows(jnp.concatenate(
        [w_branch_fox[0].astype(BF16), w_branch_sb[0].astype(BF16), w_ple[0].astype(BF16), bg_hi, bg_mid, bg_lo],
        axis=0), narrow_rows_pad)
    g_in, = _all_gather([w_in[0].T.astype(BF16)])
    w_pad = _pad_win(g_in, d)
    bf_pad = jnp.pad(b_forget, ((0, 0), (0, LANES - N_HEADS)))
    place_q, place_k, ones_q, ones_k = _c_lane_constants()

    x2d = x.reshape(t_len, d)
    p2d = p.reshape(t_len, d_ple)
    tgt2d = loss_target.reshape(t_len, d)
    qf, kf, kft, vf, vft, qkvb, kbt, vbt, gl, fpre, h1 = _inproj_fwd(
        x2d, g_mix, w_pad, bf_pad, place_q, place_k, ones_q, ones_k, seq)
    o_sb, ltot, (g_up, g_out, g_down, g_pg, g_narrow) = _sb_fwd(qkvb, vbt, batch, seq, [
        w_up[0].astype(BF16), w_out[0].astype(BF16), w_down[0].astype(BF16), w_ple_gate[0].astype(BF16), narrow])
    o_fox, lse = _fox_fwd(qf, kf, vft, batch, seq)
    w_up_full = _slabs_to_cols(g_up)
    w_out_full = g_out.reshape(d, d)
    w_down_full = g_down.reshape(d_ff, d)
    w_pg_full = g_pg.reshape(d, d)
    w_bf_full = _slabs_to_cols(g_narrow[:, :D_BRANCH])
    w_bs_full = _slabs_to_cols(g_narrow[:, D_BRANCH:2 * D_BRANCH])
    w_ple_full = _slabs_to_cols(g_narrow[:, 2 * D_BRANCH:2 * D_BRANCH + d_ple])
    bg_terms = g_narrow[:, 2 * D_BRANCH + d_ple:narrow_rows].astype(F32)
    b_gate_full = _slabs_to_cols(bg_terms[:, 0:2] + bg_terms[:, 2:4] + bg_terms[:, 4:6])
    x1 = _mix_fwd(o_fox, o_sb, gl, x2d, w_bf_full, w_bs_full, w_out_full, b_gate_full)
    a_up, x2 = _mlp_fwd(x1, g_mlp, w_up_full, w_down_full)

    dx2, h3, dpre, dpe, loss_acc, dg_ple, dg_final = _head_fwd_bwd(
        x2, p2d, tgt2d, g_ple, g_final.reshape(1, d), w_pg_full, w_ple_full)
    dx1, da_up, h2t, dg_mlp = _mlp_bwd(dx2, a_up, x1, g_mlp, w_up_full, w_down_full)
    merged, dbr_f, dbr_s, dgl, do_fox, do_sb, dbg = _mix_bwd(
        dx1, o_fox, o_sb, gl, w_bf_full, w_bs_full, w_out_full, b_gate_full)

    def column_shards(name, lhs, rhs, lhs_t=False):
        if (rhs.shape[-1] // N_DEV) % (4 * LANES) == 0:
            return _matmul_tn(name, lhs, rhs, slabs=True, lhs_t=lhs_t)
        return _cols_to_slabs(_matmul_tn(name, lhs, rhs, lhs_t=lhs_t))

    if fn % (4 * LANES) == 0:
        part_up, = _matmul_tn_once("dw_up", [h2t], da_up, slabs=True, lhs_t=True)
    else:
        part_up = column_shards("dw_up", h2t, da_up, lhs_t=True)
    part_out = _matmul_tn("dw_out", merged, dx1).reshape(N_DEV, dn, d)
    part_down = _matmul_tn_once("dw_down", [a_up], dx2, relu2=True)[0].reshape(N_DEV, fn, d)
    part_pg = _matmul_tn("dw_ple_gate", h3, dpre).reshape(N_DEV, dn, d)
    part_narrow = _pad_rows(jnp.concatenate(
        [column_shards("dw_branch_fox", o_fox, dbr_f), column_shards("dw_branch_sb", o_sb, dbr_s),
         column_shards("dw_ple", p2d, dpe)], axis=1), narrow_rows_pad)
    early = [part_up, part_out, part_down, part_pg, lax.optimization_barrier(part_narrow)]

    dqk_f, dv_f, dc_queries, dc_keys, early_recv = _fox_bwd(
        qf, kf, kft, vf, o_fox, do_fox, lse, batch, seq, early)
    early_sums = [_pair_add("pair_add_%d" % i, pt, rc, my_c) for i, (pt, rc) in enumerate(zip(early, early_recv))]
    dqkv_b, (s_up, s_out, s_down, s_pg, s_narrow) = _sb_bwd(qkvb, kbt, do_sb, ltot, batch, seq, early_sums)
    dcq_tok = dc_queries.reshape(batch, N_HEADS, seq).transpose(0, 2, 1).reshape(t_len, N_HEADS)
    dck_tok = dc_keys[..., :NH].transpose(0, 2, 1, 3).reshape(t_len, N_HEADS)
    lane_pad = ((0, 0), (0, LANES - N_HEADS))
    df, db_forget = _forget_bwd(jnp.pad(dcq_tok, lane_pad), jnp.pad(dck_tok, lane_pad), fpre, batch, seq)

    gw_in = _unpad_dwin(*_matmul_tn_once("dw_in_fox_qk", [dqk_f], h1),
                        *_matmul_tn_once("dw_in_rest", [dv_f, dqkv_b, dgl, df], h1), d)
    part_in = lax.optimization_barrier(gw_in.reshape(N_DEV, -1, d))
    recv_in, = _rs_core_pair("reduce_scatter_core_pair_w_in", [part_in])
    grad_x, dg_mix, (s_in,) = _inproj_bwd(dqk_f, dv_f, dqkv_b, dgl, df, dx1, x2d, g_mix, w_pad,
                                          [_pair_add("pair_add_w_in", part_in, recv_in, my_c)])

    small = jnp.concatenate([
        dg_mix, dg_mlp, dg_ple, dg_final, jnp.pad(db_forget[:, :N_HEADS], ((0, 0), (0, d - N_HEADS))), dbg,
        jnp.pad(loss_acc[:, :1], ((0, 0), (0, d - 1)))], axis=0)
    small = _all_reduce_small(small)
    loss = small[7, 0]
    small_grads = {
        "g_mix": small[0:1], "g_mlp": small[1:2], "g_ple": small[2:3], "g_final": small[3:4],
        "b_forget": small[4:5, :N_HEADS],
        "b_gate": lax.dynamic_slice_in_dim(small[5:7], my_dev * dn, dn, axis=1),
    }

    weights = {"g_mix": g_mix, "w_in": w_in, "b_forget": b_forget, "b_gate": b_gate, "w_branch_fox": w_branch_fox,
               "w_branch_sb": w_branch_sb, "w_out": w_out, "g_mlp": g_mlp, "w_up": w_up, "w_down": w_down,
               "g_ple": g_ple, "w_ple_gate": w_ple_gate, "w_ple": w_ple, "g_final": g_final}
    m_in = {"g_mix": m_g_mix, "w_in": m_w_in, "b_forget": m_b_forget, "b_gate": m_b_gate,
            "w_branch_fox": m_w_branch_fox, "w_branch_sb": m_w_branch_sb, "w_out": m_w_out, "g_mlp": m_g_mlp,
            "w_up": m_w_up, "w_down": m_w_down, "g_ple": m_g_ple, "w_ple_gate": m_w_ple_gate, "w_ple": m_w_ple,
            "g_final": m_g_final}
    v_in = {"g_mix": v_g_mix, "w_in": v_w_in, "b_forget": v_b_forget, "b_gate": v_b_gate,
            "w_branch_fox": v_w_branch_fox, "w_branch_sb": v_w_branch_sb, "w_out": v_w_out, "g_mlp": v_g_mlp,
            "w_up": v_w_up, "w_down": v_w_down, "g_ple": v_g_ple, "w_ple_gate": v_w_ple_gate, "w_ple": v_w_ple,
            "g_final": v_g_final}
    names = list(weights)

    def as2d(a):
        return a.reshape(-1, a.shape[-1])

    result = {}
    big = {"w_up": (s_up, 0), "w_out": (s_out, 0), "w_down": (s_down, 0), "w_ple_gate": (s_pg, 0),
           "w_branch_fox": (s_narrow, 0), "w_branch_sb": (s_narrow, D_BRANCH), "w_ple": (s_narrow, 2 * D_BRANCH)}
    for n, (parts, off) in big.items():
        result[n] = _adamw_parts("adamw_" + n, as2d(weights[n]), parts, off, as2d(m_in[n]), as2d(v_in[n]))
    result["w_in"] = tuple(r.T for r in _adamw_parts("adamw_w_in", w_in[0].T, s_in, 0, m_w_in[0].T, v_w_in[0].T))
    small_names = list(small_grads)
    small_out = _adamw_small([(as2d(weights[n]), small_grads[n], as2d(m_in[n]), as2d(v_in[n])) for n in small_names])
    for n, (dlt, nm, nv) in zip(small_names, small_out):
        result[n] = (small_grads[n], dlt, nm, nv)
    outs = [[result[n][k].reshape(weights[n].shape) for n in names] for k in range(4)]
    return (loss, grad_x.reshape(x.shape), *outs[0], *outs[1], *outs[2], *outs[3])
```

```python
import jax
import jax.numpy as jnp
from jax import lax
from jax.experimental import pallas as pl
from jax.experimental.pallas import tpu as pltpu

F32 = jnp.float32
BF16 = jnp.bfloat16

HEAD_DIM = 64
N_HEADS = 8
D_BRANCH = N_HEADS * HEAD_DIM
EPS = 1e-6
ADAM_LR = 0.001
ADAM_B1 = 0.9
ADAM_B2 = 0.999
ADAM_EPS = 1e-08
ADAM_WD = 0.01
ADAM_STEP = 10

N_DEV = 8
LANES = 128
TM = 256
TQ = 256
TK = 256
NH = 4
HEAD_SLOT = 128
C_TERMS_Q = 64
C_ONES_K = 64
C_TERMS_K = 67
C_ONES_Q = 67
NEG = -1e30
VMEM_LIMIT = 56 * 1024 * 1024
MESH = pl.DeviceIdType.MESH


def _dot(a, b):
    return jnp.dot(a, b, preferred_element_type=F32)


def _dot_nt(a, b):
    return lax.dot_general(a, b, (((1,), (1,)), ((), ())), preferred_element_type=F32)


def _dot_tn(a, b):
    return lax.dot_general(a, b, (((0,), (0,)), ((), ())), preferred_element_type=F32)


def _sigmoid(x):
    return 1.0 / (1.0 + jnp.exp(-x))


def _softplus(x):
    return jnp.maximum(x, 0.0) + jnp.log(1.0 + jnp.exp(-jnp.abs(x)))


def _split2(x):
    hi = x.astype(BF16)
    lo = (x - hi.astype(F32)).astype(BF16)
    return hi, lo


def _split3(x):
    hi = x.astype(BF16)
    r = x - hi.astype(F32)
    mid = r.astype(BF16)
    lo = (r - mid.astype(F32)).astype(BF16)
    return hi, mid, lo


def _rows_dot_mask(x, mask_bf16):
    hi, lo = _split2(x)
    return _dot(hi, mask_bf16) + _dot(lo, mask_bf16)


def _tri(n, rel):
    r = lax.broadcasted_iota(jnp.int32, (n, n), 0)
    c = lax.broadcasted_iota(jnp.int32, (n, n), 1)
    return rel(r, c).astype(BF16)


def _rms(x):
    r = lax.rsqrt(jnp.mean(x * x, axis=-1, keepdims=True) + EPS)
    return x * r, r


def _rms_bwd(dh, xn, r, g):
    dxn = dh * g
    dx = r * (dxn - xn * jnp.mean(dxn * xn, axis=-1, keepdims=True))
    return dx, jnp.sum(dh * xn, axis=0, keepdims=True)


def _row_spec(tm, cols):
    return pl.BlockSpec((tm, cols), lambda i: (i, 0))


def _row3_spec(g, tm, cols):
    return pl.BlockSpec((g, tm, cols), lambda i: (0, i, 0))


def _col_spec(rows, tm):
    return pl.BlockSpec((rows, tm), lambda i: (0, i))


def _const_spec(shape):
    nd = len(shape)
    return pl.BlockSpec(shape, lambda i: (0,) * nd, pipeline_mode=pl.Buffered(1))


def _acc_spec(shape):
    nd = len(shape)
    return pl.BlockSpec(shape, lambda i: (0,) * nd)


def _seq_params():
    return pltpu.CompilerParams(dimension_semantics=("arbitrary",), vmem_limit_bytes=VMEM_LIMIT)


def _mesh_pos():
    return lax.axis_index("x"), lax.axis_index("y"), lax.axis_index("c")


def _other_chips(x, y):
    return [(1 - x, y), (x, 1 - y), (1 - x, 1 - y)]


def _hbm_specs(n):
    return [pl.BlockSpec(memory_space=pl.ANY)] * n


def _gather_plan(x_refs, out_refs, send_sems, recv_sems, local_sems):
    n = len(x_refs)
    x, y, c = _mesh_pos()
    me, sibling = (x, y, c), (x, y, 1 - c)
    chips = _other_chips(x, y)

    def index(px, py, pc):
        return 4 * px + 2 * py + pc

    def copy(a, k, block, to, src=None):
        slab = out_refs[a].at[index(*block)]
        return pltpu.make_async_remote_copy(
            src_ref=slab if src is None else src, dst_ref=slab,
            send_sem=send_sems.at[7 * a + k], recv_sem=recv_sems.at[7 * a + k], device_id=to, device_id_type=MESH)

    mine = [pltpu.make_async_copy(x_refs[a], out_refs[a].at[index(*me)], local_sems.at[a]) for a in range(n)]
    first = []
    for a in range(n):
        first.append(copy(a, 0, me, sibling, src=x_refs[a]))
        first += [copy(a, 1 + j, me, (cx, cy, c), src=x_refs[a]) for j, (cx, cy) in enumerate(chips)]

    def start():
        for cp in mine + first:
            cp.start()

    def finish():
        passed = []
        for j, (cx, cy) in enumerate(chips):
            for a in range(n):
                copy(a, 1 + j, (cx, cy, c), me).wait_recv()
                passed.append(copy(a, 4 + j, (cx, cy, c), sibling))
                passed[-1].start()
        for a in range(n):
            copy(a, 0, sibling, me).wait_recv()
            for j, (cx, cy) in enumerate(chips):
                copy(a, 4 + j, (cx, cy, 1 - c), me).wait_recv()
        for cp in first + passed:
            cp.wait_send()
        for cp in mine:
            cp.wait()

    return start, finish


def _gather_shapes(shards):
    return [jax.ShapeDtypeStruct((N_DEV,) + s.shape, s.dtype) for s in shards]


def _gather_sems(n):
    return [pltpu.SemaphoreType.DMA((7 * n,)), pltpu.SemaphoreType.DMA((7 * n,)), pltpu.SemaphoreType.DMA((n,))]


def _all_gather(shards):
    n = len(shards)

    def body(*refs):
        start, finish = _gather_plan(refs[:n], refs[n:2 * n], *refs[2 * n:])
        start()
        finish()

    return pl.pallas_call(
        body, name="all_gather_weights", out_shape=_gather_shapes(shards),
        in_specs=_hbm_specs(n), out_specs=_hbm_specs(n), scratch_shapes=_gather_sems(n),
    )(*shards)


def _pair_plan(p_refs, recv_refs, send_sems, recv_sems):
    n = len(p_refs)
    x, y, c = _mesh_pos()
    sibling = (x, y, 1 - c)

    def start():
        for a in range(n):
            for chip in range(4):
                pltpu.make_async_remote_copy(
                    src_ref=p_refs[a].at[2 * chip + (1 - c)], dst_ref=recv_refs[a].at[chip],
                    send_sem=send_sems.at[a], recv_sem=recv_sems.at[a], device_id=sibling, device_id_type=MESH).start()

    def finish():
        for a in range(n):
            pltpu.make_async_remote_copy(
                src_ref=recv_refs[a], dst_ref=recv_refs[a], send_sem=send_sems.at[a], recv_sem=recv_sems.at[a],
                device_id=sibling, device_id_type=MESH).wait()

    return start, finish


def _pair_shapes(partials):
    return [jax.ShapeDtypeStruct((4,) + s.shape[1:], s.dtype) for s in partials]


def _pair_sems(n):
    return [pltpu.SemaphoreType.DMA((n,)), pltpu.SemaphoreType.DMA((n,))]


def _rs_core_pair(name, partials):
    n = len(partials)

    def body(*refs):
        start, finish = _pair_plan(refs[:n], refs[n:2 * n], *refs[2 * n:])
        start()
        finish()

    return pl.pallas_call(
        body, name=name, out_shape=_pair_shapes(partials),
        in_specs=_hbm_specs(n), out_specs=_hbm_specs(n), scratch_shapes=_pair_sems(n),
    )(*partials)


def _chips_plan(cs_refs, out_refs, send_sems, recv_sems, local_sems):
    n = len(cs_refs)
    x, y, c = _mesh_pos()
    chip = 2 * x + y
    chips = _other_chips(x, y)
    mine = [pltpu.make_async_copy(cs_refs[a].at[chip], out_refs[a].at[chip], local_sems.at[a]) for a in range(n)]
    sends = [pltpu.make_async_remote_copy(
        src_ref=cs_refs[a].at[2 * cx + cy], dst_ref=out_refs[a].at[chip],
        send_sem=send_sems.at[3 * a + j], recv_sem=recv_sems.at[3 * a + j],
        device_id=(cx, cy, c), device_id_type=MESH) for a in range(n) for j, (cx, cy) in enumerate(chips)]

    def start():
        for cp in mine + sends:
            cp.start()

    def finish():
        for a in range(n):
            for j, (cx, cy) in enumerate(chips):
                pltpu.make_async_remote_copy(
                    src_ref=cs_refs[a].at[chip], dst_ref=out_refs[a].at[2 * cx + cy],
                    send_sem=send_sems.at[3 * a + j], recv_sem=recv_sems.at[3 * a + j],
                    device_id=(x, y, c), device_id_type=MESH).wait_recv()
        for cp in sends:
            cp.wait_send()
        for cp in mine:
            cp.wait()

    return start, finish


def _chips_sems(n):
    return [pltpu.SemaphoreType.DMA((3 * n,)), pltpu.SemaphoreType.DMA((3 * n,)), pltpu.SemaphoreType.DMA((n,))]


def _all_reduce_small(vec):
    rows, cols = vec.shape

    def body(x_ref, land_ref, sum_ref, send_sems, recv_sems):
        x, y, c = _mesh_pos()
        me = 4 * x + 2 * y + c
        land_ref[me] = x_ref[...]
        flips = [(fx, fy, fc) for fx in (0, 1) for fy in (0, 1) for fc in (0, 1)][1:]

        def flipped(f):
            return tuple((1 - v) if b else v for v, b in zip((x, y, c), f))

        sends = []
        for k, f in enumerate(flips):
            sends.append(pltpu.make_async_remote_copy(
                src_ref=x_ref, dst_ref=land_ref.at[me], send_sem=send_sems.at[k], recv_sem=recv_sems.at[k],
                device_id=flipped(f), device_id_type=MESH))
            sends[-1].start()
        for k, f in enumerate(flips):
            px, py, pc = flipped(f)
            pltpu.make_async_remote_copy(
                src_ref=x_ref, dst_ref=land_ref.at[4 * px + 2 * py + pc], send_sem=send_sems.at[k],
                recv_sem=recv_sems.at[k], device_id=(x, y, c), device_id_type=MESH).wait_recv()
        for cp in sends:
            cp.wait_send()
        total = land_ref[0]
        for d in range(1, N_DEV):
            total = total + land_ref[d]
        sum_ref[...] = total

    vm = pl.BlockSpec(memory_space=pltpu.VMEM)
    return pl.pallas_call(
        body, name="all_reduce_small",
        out_shape=(jax.ShapeDtypeStruct((N_DEV, rows, cols), F32), jax.ShapeDtypeStruct((rows, cols), F32)),
        in_specs=[vm], out_specs=(vm, vm),
        scratch_shapes=[pltpu.SemaphoreType.DMA((7,)), pltpu.SemaphoreType.DMA((7,))],
    )(vec)[1]


def _block_rows(rows, cols, itemsize, align, row_off=0):
    best = None
    for t in range(align, rows + 1, align):
        if rows % t == 0 and row_off % t == 0 and t * cols * itemsize <= (1 << 20):
            best = t
    return rows if best is None else best


def _pair_add(name, partial, recv, my_c):
    _, rows, cols = partial.shape
    br = _block_rows(rows, cols, 2, 16)

    def body(c_ref, a_ref, b_ref, o_ref):
        o_ref[...] = (a_ref[...].astype(F32) + b_ref[...].astype(F32)).astype(BF16)

    return pl.pallas_call(
        body, name=name,
        grid_spec=pltpu.PrefetchScalarGridSpec(
            num_scalar_prefetch=1, grid=(4, rows // br),
            in_specs=[pl.BlockSpec((None, None, br, cols), lambda j, i, c_ref: (j, c_ref[0], i, 0)),
                      pl.BlockSpec((None, br, cols), lambda j, i, c_ref: (j, i, 0))],
            out_specs=pl.BlockSpec((None, br, cols), lambda j, i, c_ref: (j, i, 0))),
        out_shape=jax.ShapeDtypeStruct((4, rows, cols), BF16),
    )(my_c.reshape(1).astype(jnp.int32), partial.reshape(4, 2, rows, cols), recv)


def _adam_update(w, g, m, v):
    nm = ADAM_B1 * m + (1.0 - ADAM_B1) * g
    nv = ADAM_B2 * v + (1.0 - ADAM_B2) * (g * g)
    m_hat = nm / (1.0 - ADAM_B1 ** ADAM_STEP)
    v_hat = nv / (1.0 - ADAM_B2 ** ADAM_STEP)
    return -ADAM_LR * (m_hat / (jnp.sqrt(v_hat) + ADAM_EPS) + ADAM_WD * w), nm, nv


def _adamw_parts(name, w, parts, row_off, m, v):
    rows, cols = w.shape
    tr = _block_rows(rows, cols, 4, 16, row_off)
    tc = cols
    if tr == rows and rows % 16 != 0 and cols % (2 * LANES) == 0:
        tc = 2 * LANES
    assert rows % tr == 0 and row_off % tr == 0 and (tc == cols or row_off == 0)
    off = row_off // tr

    def body(w_ref, p_ref, m_ref, v_ref, g_ref, d_ref, nm_ref, nv_ref):
        g = p_ref[0].astype(F32)
        for j in range(1, 4):
            g = g + p_ref[j].astype(F32)
        g_ref[...] = g
        d_ref[...], nm_ref[...], nv_ref[...] = _adam_update(w_ref[...], g, m_ref[...], v_ref[...])

    spec = pl.BlockSpec((tr, tc), lambda i, j: (i, j))
    shp = jax.ShapeDtypeStruct((rows, cols), F32)
    return pl.pallas_call(
        body, name=name, grid=(rows // tr, cols // tc), out_shape=(shp,) * 4,
        in_specs=[spec, pl.BlockSpec((4, tr, tc), lambda i, j: (0, off + i, j)), spec, spec], out_specs=(spec,) * 4,
    )(w, parts, m, v)


def _adamw_small(tensors):
    n = len(tensors)

    def body(*refs):
        ins, outs = refs[:4 * n], refs[4 * n:]
        for t in range(n):
            w_ref, g_ref, m_ref, v_ref = ins[4 * t:4 * t + 4]
            d, nm, nv = _adam_update(w_ref[...], g_ref[...], m_ref[...], v_ref[...])
            outs[3 * t][...], outs[3 * t + 1][...], outs[3 * t + 2][...] = d, nm, nv

    vm = pl.BlockSpec(memory_space=pltpu.VMEM)
    out = pl.pallas_call(
        body, name="adamw_small",
        out_shape=[jax.ShapeDtypeStruct(t[0].shape, F32) for t in tensors for _ in range(3)],
        in_specs=[vm] * (4 * n), out_specs=[vm] * (3 * n),
    )(*[a for t in tensors for a in t])
    return [tuple(out[3 * t:3 * t + 3]) for t in range(n)]


def _matmul_tn(name, a, b, relu2=False, slabs=False, lhs_t=False):
    a_groups = a.shape[0] if a.ndim == 3 else 0
    b_groups = b.shape[0] if b.ndim == 3 else 0
    groups = max(a_groups, b_groups, 1)
    assert not (a_groups and b_groups) and not (a_groups and lhs_t)
    a3 = a if a_groups else a[None]
    b3 = b if b_groups else b[None]
    t_len, k_len = a3.shape[1:][::-1] if lhs_t else a3.shape[1:]
    n_len = b3.shape[2]
    tt = min(t_len, 512)
    tk = min(k_len, 1024)
    tn = n_len // N_DEV if slabs else min(n_len, 1024)
    nt = t_len // tt
    assert not slabs or (groups == 1 and tn <= 1024)

    def body(a_ref, b_ref, o_ref, acc_ref):
        @pl.when(pl.program_id(3) == 0)
        def _():
            acc_ref[...] = jnp.zeros_like(acc_ref)

        av = a_ref[...]
        if relu2:
            av = jnp.square(jnp.maximum(av.astype(F32), 0.0))
        product = _dot if lhs_t else _dot_tn
        acc_ref[...] += product(av.astype(BF16), b_ref[...].astype(BF16))

        @pl.when(pl.program_id(3) == nt - 1)
        def _():
            o_ref[...] = acc_ref[...].astype(BF16)

    def a_group(g):
        return g if a_groups else 0

    def b_group(g):
        return g if b_groups else 0

    if slabs:
        out_shape = jax.ShapeDtypeStruct((N_DEV, k_len, tn), BF16)
        out_spec = pl.BlockSpec((None, tk, tn), lambda g, i, j, t: (j, i, 0))
    else:
        out_shape = jax.ShapeDtypeStruct((groups, k_len, n_len), BF16)
        out_spec = pl.BlockSpec((None, tk, tn), lambda g, i, j, t: (g, i, j))
    out = pl.pallas_call(
        body, name=name, grid=(groups, k_len // tk, n_len // tn, nt), out_shape=out_shape,
        in_specs=[pl.BlockSpec((None, tk, tt), lambda g, i, j, t: (a_group(g), i, t)) if lhs_t
                  else pl.BlockSpec((None, tt, tk), lambda g, i, j, t: (a_group(g), t, i)),
                  pl.BlockSpec((None, tt, tn), lambda g, i, j, t: (b_group(g), t, j))],
        out_specs=out_spec,
        scratch_shapes=[pltpu.VMEM((tk, tn), F32)],
        compiler_params=pltpu.CompilerParams(
            dimension_semantics=("parallel", "parallel", "parallel", "arbitrary"), vmem_limit_bytes=VMEM_LIMIT),
    )(a3, b3)
    return out if (slabs or a_groups or b_groups) else out[0]


def _matmul_tn_once(name, lhs_list, rhs, relu2=False, slabs=False, lhs_t=False):
    t_len, n_len = rhs.shape
    tt = min(t_len, 256)
    nt = t_len // tt
    n_lhs = len(lhs_list)
    assert not (lhs_t or slabs) or (n_lhs == 1 and lhs_list[0].ndim == 2)
    k_shapes = [(a.shape[0], n_len) if lhs_t else a.shape[:-2] + (a.shape[-1], n_len) for a in lhs_list]
    tn = n_len // N_DEV

    def body(*refs):
        a_refs, b_ref = refs[:n_lhs], refs[n_lhs]
        o_refs, acc_refs = refs[n_lhs + 1:2 * n_lhs + 1], refs[2 * n_lhs + 1:]
        step = pl.program_id(0)

        @pl.when(step == 0)
        def _():
            for acc in acc_refs:
                acc[...] = jnp.zeros_like(acc)

        bv = b_ref[...].astype(BF16)

        def piece(av):
            if relu2:
                av = jnp.square(jnp.maximum(av.astype(F32), 0.0))
            return (_dot if lhs_t else _dot_tn)(av.astype(BF16), bv)

        for a_ref, acc in zip(a_refs, acc_refs):
            if len(acc.shape) == 3:
                for g in range(acc.shape[0]):
                    acc[g] += piece(a_ref[g])
            else:
                acc[...] += piece(a_ref[...])

        @pl.when(step == nt - 1)
        def _():
            for o_ref, acc in zip(o_refs, acc_refs):
                if slabs:
                    for j in range(N_DEV):
                        o_ref[j] = acc[:, j * tn:(j + 1) * tn].astype(BF16)
                else:
                    o_ref[...] = acc[...].astype(BF16)

    def lhs_spec(a):
        if lhs_t:
            return pl.BlockSpec((a.shape[0], tt), lambda t: (0, t))
        if a.ndim == 3:
            return pl.BlockSpec((a.shape[0], tt, a.shape[2]), lambda t: (0, t, 0))
        return pl.BlockSpec((tt, a.shape[1]), lambda t: (t, 0))

    out_shapes = [(N_DEV, k_shapes[0][0], tn)] if slabs else k_shapes
    return pl.pallas_call(
        body, name=name, grid=(nt,),
        out_shape=[jax.ShapeDtypeStruct(s, BF16) for s in out_shapes],
        in_specs=[lhs_spec(a) for a in lhs_list] + [pl.BlockSpec((tt, n_len), lambda t: (t, 0))],
        out_specs=[_acc_spec(s) for s in out_shapes],
        scratch_shapes=[pltpu.VMEM(s, F32) for s in k_shapes],
        compiler_params=_seq_params(),
    )(*lhs_list, rhs)


def _pad_layout(d):
    names = ("qf", "kf", "vf", "qb", "kb", "vb", "gates", "forget")
    sizes = (N_HEADS * HEAD_SLOT, N_HEADS * HEAD_SLOT, D_BRANCH, D_BRANCH, D_BRANCH, D_BRANCH, 2 * d, LANES)
    out, off = {}, 0
    for n, s in zip(names, sizes):
        out[n] = (off, off + s)
        off += s
    return out, off


def _slot_rows(xt, extra):
    parts = []
    for h in range(N_HEADS):
        parts += [xt[h * HEAD_DIM:(h + 1) * HEAD_DIM, :], extra]
    return jnp.concatenate(parts, axis=0)


def _inproj_fwd(x, g_mix, w_pad, bf_pad, place_q, place_k, ones_q, ones_k, seq):
    t_len, d = x.shape
    lay, _ = _pad_layout(d)
    tiles_per_seq = seq // TM
    slot_w = N_HEADS * HEAD_SLOT

    def body(x_ref, g_ref, w_ref, bf_ref, pq_ref, pk_ref, oq_ref, ok_ref,
             qf_ref, kf_ref, kft_ref, vf_ref, vft_ref, qkvb_ref, kbt_ref, vbt_ref, gl_ref, fpre_ref, h_ref,
             carry_ref):
        @pl.when(pl.program_id(0) % tiles_per_seq == 0)
        def _():
            carry_ref[...] = jnp.zeros_like(carry_ref)

        def proj(name):
            lo, hi = lay[name]
            return _dot_nt(h, w_ref[lo:hi, :])

        xn, _ = _rms(x_ref[...])
        h = (xn * g_ref[...]).astype(BF16)
        fpre = proj("forget") + bf_ref[...]
        fpre_ref[...] = fpre
        logf = -_softplus(-fpre)
        lower = _tri(TM, lambda r, c: c <= r)
        hi, mid, lo = _split3(logf)
        c_val = carry_ref[...] + _dot(lower, hi) + _dot(lower, mid) + _dot(lower, lo)
        carry_ref[...] = carry_ref[...] + jnp.sum(logf, axis=0, keepdims=True)
        c3 = _split3(c_val)
        qf_ref[...] = (proj("qf") + sum(_dot(c3[j], pq_ref[j]) for j in range(3)) + oq_ref[...]).astype(BF16)
        kf = proj("kf") - sum(_dot(c3[j], pk_ref[j]) for j in range(3)) + ok_ref[...]
        kf_ref[...] = kf.astype(BF16)
        kft_ref[0] = kf.T.astype(BF16)
        row0 = (lax.broadcasted_iota(jnp.int32, (HEAD_DIM, TM), 0) == 0).astype(F32)
        zeros = jnp.zeros((HEAD_DIM, TM), F32)
        vf = proj("vf")
        vf_ref[...] = vf.astype(BF16)
        vft_ref[0] = _slot_rows(vf.T, row0).astype(BF16)
        qkvb_ref[0] = proj("qb").astype(BF16)
        kb = proj("kb")
        qkvb_ref[1] = kb.astype(BF16)
        kbt_ref[0] = _slot_rows(kb.T, zeros).astype(BF16)
        vb = proj("vb")
        qkvb_ref[2] = vb.astype(BF16)
        vbt_ref[0] = _slot_rows(vb.T, row0).astype(BF16)
        gl_ref[...] = proj("gates").astype(BF16)
        h_ref[...] = h

    n_tiles = t_len // TM
    slot_shape = jax.ShapeDtypeStruct((t_len, slot_w), BF16)
    t_shape = jax.ShapeDtypeStruct((n_tiles, slot_w, TM), BF16)
    t_spec = pl.BlockSpec((1, slot_w, TM), lambda i: (i, 0, 0))
    return pl.pallas_call(
        body, name="inproj_fwd", grid=(n_tiles,),
        out_shape=(slot_shape, slot_shape, t_shape, jax.ShapeDtypeStruct((t_len, D_BRANCH), BF16), t_shape,
                   jax.ShapeDtypeStruct((3, t_len, D_BRANCH), BF16), t_shape, t_shape,
                   jax.ShapeDtypeStruct((t_len, 2 * d), BF16), jax.ShapeDtypeStruct((t_len, LANES), F32),
                   jax.ShapeDtypeStruct((t_len, d), BF16)),
        in_specs=[_row_spec(TM, d), _const_spec((1, d)), _const_spec(w_pad.shape), _const_spec((1, LANES)),
                  _const_spec(place_q.shape), _const_spec(place_k.shape), _const_spec((1, slot_w)),
                  _const_spec((1, slot_w))],
        out_specs=(_row_spec(TM, slot_w), _row_spec(TM, slot_w), t_spec, _row_spec(TM, D_BRANCH), t_spec,
                   _row3_spec(3, TM, D_BRANCH), t_spec, t_spec, _row_spec(TM, 2 * d), _row_spec(TM, LANES),
                   _row_spec(TM, d)),
        scratch_shapes=[pltpu.VMEM((1, LANES), F32)],
        compiler_params=_seq_params(),
    )(x, g_mix, w_pad, bf_pad, place_q, place_k, ones_q, ones_k)


def _slot_spec(seq):
    return pl.BlockSpec((seq, NH * HEAD_SLOT), lambda b, g: (b, g))


def _slot2_spec(seq):
    return pl.BlockSpec((2, seq, NH * HEAD_SLOT), lambda b, g: (0, b, g))


def _group_spec(seq):
    return pl.BlockSpec((seq, NH * HEAD_DIM), lambda b, g: (b, g))


def _group3_spec(which, seq):
    return pl.BlockSpec((None, seq, NH * HEAD_DIM), lambda b, g: (which, b, g))


def _tblock_spec(seq):
    return pl.BlockSpec((seq // TK, NH * HEAD_SLOT, TK), lambda b, g: (b, g, 0))


def _qrow_spec(seq):
    return pl.BlockSpec((None, NH, seq // TQ, TQ), lambda b, g: (b, g, 0, 0))


def _stat_spec(seq):
    return pl.BlockSpec((None, None, seq, LANES), lambda b, g: (b, g, 0, 0))


def _attn_params():
    return pltpu.CompilerParams(dimension_semantics=("parallel", "parallel"), vmem_limit_bytes=VMEM_LIMIT)


def _serial_attn_params():
    return pltpu.CompilerParams(dimension_semantics=("arbitrary", "arbitrary"), vmem_limit_bytes=VMEM_LIMIT)


def _hcols(hh):
    return slice(hh * HEAD_DIM, (hh + 1) * HEAD_DIM)


def _hslot(hh):
    return slice(hh * HEAD_SLOT, (hh + 1) * HEAD_SLOT)


def _lane(hh):
    return slice(hh, hh + 1)


def _key_query_mask(rel):
    r = lax.broadcasted_iota(jnp.int32, (TK, TQ), 0)
    c = lax.broadcasted_iota(jnp.int32, (TK, TQ), 1)
    return rel(r, c)


def _heads_cat(vals):
    return jnp.concatenate(vals, axis=1)


def _untranspose(acc_t):
    return acc_t.T[:, :HEAD_DIM]


def _fox_fwd(qf, kf, vft, batch, seq):
    def body(q_ref, k_ref, vt_ref, o_ref, lse_ref, m_s, acc_s):
        causal = _key_query_mask(lambda r, c: r <= c)

        def tile(q0, kj, masked, n_k=1):
            krows = pl.ds(pl.multiple_of(kj * TK, TK), n_k * TK)
            heads = range(NH)
            sts = [_dot_nt(k_ref[krows, _hslot(hh)], q_ref[pl.ds(q0, TQ), _hslot(hh)]) for hh in heads]
            if masked:
                sts = [jnp.where(causal, st, NEG) for st in sts]
            m_olds = [m_s[hh] for hh in heads]
            m_news = [jnp.maximum(m_olds[hh], jnp.max(sts[hh], axis=0, keepdims=True)) for hh in heads]
            pts = [jnp.exp(sts[hh] - m_news[hh]).astype(BF16) for hh in heads]
            pvs = [sum(_dot(vt_ref[kj + i, _hslot(hh), :], pts[hh][i * TK:(i + 1) * TK]) for i in range(n_k))
                   for hh in heads]
            for hh in heads:
                acc_s[hh] = jnp.exp(m_olds[hh] - m_news[hh]) * acc_s[hh] + pvs[hh]
                m_s[hh] = m_news[hh]

        def q_loop(qi, _):
            q0 = pl.multiple_of(qi * TQ, TQ)
            m_s[...] = jnp.full(m_s.shape, NEG, F32)
            acc_s[...] = jnp.zeros_like(acc_s)

            def pair_loop(i, _):
                tile(q0, 2 * i, False, n_k=2)
                return 0

            lax.fori_loop(0, qi // 2, pair_loop, 0)
            pl.when(qi % 2 == 1)(lambda: tile(q0, qi - 1, False))
            tile(q0, qi, True)
            outs = []
            for hh in range(NH):
                total = acc_s[hh, HEAD_DIM:HEAD_DIM + 1, :]
                outs.append(_untranspose(acc_s[hh] / total))
                lse_ref[hh, pl.ds(qi, 1), :] = m_s[hh] + jnp.log(total)
            o_ref[pl.ds(q0, TQ), :] = _heads_cat(outs).astype(BF16)
            return 0

        lax.fori_loop(0, seq // TQ, q_loop, 0)

    return pl.pallas_call(
        body, name="fox_fwd", grid=(batch, N_HEADS // NH),
        out_shape=(jax.ShapeDtypeStruct((batch * seq, D_BRANCH), BF16),
                   jax.ShapeDtypeStruct((batch, N_HEADS, seq // TQ, TQ), F32)),
        in_specs=[_slot_spec(seq), _slot_spec(seq), _tblock_spec(seq)],
        out_specs=(_group_spec(seq), _qrow_spec(seq)),
        scratch_shapes=[pltpu.VMEM((NH, 1, TQ), F32), pltpu.VMEM((NH, HEAD_SLOT, TQ), F32)],
        compiler_params=_attn_params(),
    )(qf, kf, vft)


def _fox_bwd(qf, kf, kft, vf, o, do, lse, batch, seq, partials):
    n_q = seq // TQ
    n = len(partials)

    def body(q_ref, k_ref, kt_ref, v_ref, o_ref, do_ref, lse_ref, *rest):
        p_refs, (dqk_ref, dv_ref, dcq_ref, dck_ref), recv_refs = rest[:n], rest[n:n + 4], rest[n + 4:2 * n + 4]
        delta_s, dqt_acc, dk_s, dv_s = rest[2 * n + 4:2 * n + 8]
        pair_start, pair_finish = _pair_plan(p_refs, recv_refs, *rest[2 * n + 8:])
        first_step, last_step = _first_last_step()
        pl.when(first_step)(pair_start)
        causal = _key_query_mask(lambda r, c: r <= c)
        ones8 = jnp.ones((8, HEAD_DIM), BF16)
        dqt_acc[...] = jnp.zeros_like(dqt_acc)

        def prep(qi, _):
            rows = pl.ds(pl.multiple_of(qi * TQ, TQ), TQ)
            for hh in range(NH):
                hi, lo = _split2(do_ref[rows, _hcols(hh)].astype(F32) * o_ref[rows, _hcols(hh)].astype(F32))
                delta_s[hh, pl.ds(qi, 1), :] = (_dot_nt(ones8, hi) + _dot_nt(ones8, lo))[0:1, :]
            return 0

        lax.fori_loop(0, n_q, prep, 0)

        def tile(qi, kj, masked):
            rows = pl.ds(pl.multiple_of(qi * TQ, TQ), TQ)
            krows = pl.ds(pl.multiple_of(kj * TK, TK), TK)
            heads = range(NH)
            qs = [q_ref[rows, _hslot(hh)] for hh in heads]
            douts = [do_ref[rows, _hcols(hh)] for hh in heads]
            sts = [_dot_nt(k_ref[krows, _hslot(hh)], qs[hh]) for hh in heads]
            dps = [_dot_nt(v_ref[krows, _hcols(hh)], douts[hh]) for hh in heads]
            pts = [jnp.exp(sts[hh] - lse_ref[hh, pl.ds(qi, 1), :]) for hh in heads]
            if masked:
                pts = [jnp.where(causal, pt, 0.0) for pt in pts]
            dsts = [(pts[hh] * (dps[hh] - delta_s[hh, pl.ds(qi, 1), :])).astype(BF16) for hh in heads]
            for hh in heads:
                dv_s[hh] += _dot(pts[hh].astype(BF16), douts[hh])
                dk_s[hh] += _dot(dsts[hh], qs[hh])
                dqt_acc[hh, qi] += _dot(kt_ref[kj, _hslot(hh), :], dsts[hh])

        def k_loop(kj, _):
            krows = pl.ds(pl.multiple_of(kj * TK, TK), TK)
            dk_s[...] = jnp.zeros_like(dk_s)
            dv_s[...] = jnp.zeros_like(dv_s)
            tile(kj, kj, True)

            def q_loop(qi, _):
                tile(qi, kj, False)
                return 0

            lax.fori_loop(kj + 1, n_q, q_loop, 0)
            dqk_ref[1, krows, :] = _heads_cat([dk_s[hh] for hh in range(NH)]).astype(BF16)
            dv_ref[krows, :] = _heads_cat([dv_s[hh] for hh in range(NH)]).astype(BF16)
            for hh in range(NH):
                dck_ref[krows, _lane(hh)] = dk_s[hh, :, C_ONES_Q:C_ONES_Q + 1]
            return 0

        lax.fori_loop(0, seq // TK, k_loop, 0)

        def finish(qi, _):
            rows = pl.ds(pl.multiple_of(qi * TQ, TQ), TQ)
            dqk_ref[0, rows, :] = _heads_cat([dqt_acc[hh, qi].T for hh in range(NH)]).astype(BF16)
            for hh in range(NH):
                dcq_ref[hh, pl.ds(qi, 1), :] = dqt_acc[hh, qi, C_ONES_K:C_ONES_K + 1, :]
            return 0

        lax.fori_loop(0, n_q, finish, 0)
        pl.when(last_step)(pair_finish)

    out = pl.pallas_call(
        body, name="fox_bwd", grid=(batch, N_HEADS // NH),
        out_shape=[jax.ShapeDtypeStruct((2, batch * seq, N_HEADS * HEAD_SLOT), BF16),
                   jax.ShapeDtypeStruct((batch * seq, D_BRANCH), BF16),
                   jax.ShapeDtypeStruct((batch, N_HEADS, seq // TQ, TQ), F32),
                   jax.ShapeDtypeStruct((batch, N_HEADS // NH, seq, LANES), F32)] + _pair_shapes(partials),
        in_specs=[_slot_spec(seq), _slot_spec(seq), _tblock_spec(seq), _group_spec(seq), _group_spec(seq),
                  _group_spec(seq), _qrow_spec(seq)] + _hbm_specs(n),
        out_specs=[_slot2_spec(seq), _group_spec(seq), _qrow_spec(seq), _stat_spec(seq)] + _hbm_specs(n),
        scratch_shapes=[pltpu.VMEM((NH, n_q, TQ), F32), pltpu.VMEM((NH, n_q, HEAD_SLOT, TQ), F32),
                        pltpu.VMEM((NH, TK, HEAD_SLOT), F32), pltpu.VMEM((NH, TK, HEAD_DIM), F32)] + _pair_sems(n),
        compiler_params=_serial_attn_params(),
    )(qf, kf, kft, vf, o, do, lse, *partials)
    return out[0], out[1], out[2], out[3], out[4:]


def _first_last_step():
    step = pl.program_id(0) * pl.num_programs(1) + pl.program_id(1)
    return step == 0, step == pl.num_programs(0) * pl.num_programs(1) - 1


def _sb_fwd(qkvb, vbt, batch, seq, shards):
    n = len(shards)

    def body(q_ref, k_ref, vt_ref, *rest):
        x_refs, (o_ref, lt_ref), out_refs = rest[:n], rest[n:n + 2], rest[n + 2:2 * n + 2]
        run_s, acc_s = rest[2 * n + 2:2 * n + 4]
        gather_start, gather_finish = _gather_plan(x_refs, out_refs, *rest[2 * n + 4:])
        first_step, last_step = _first_last_step()
        pl.when(first_step)(gather_start)
        strict = _key_query_mask(lambda r, c: r < c)
        later = _tri(TK, lambda r, c: c > r)

        def tile(q0, kjs, masked):
            heads = range(NH)
            items = [(t, hh) for t in range(len(kjs)) for hh in heads]
            krows = [pl.ds(kj * TK if isinstance(kj, int) else pl.multiple_of(kj * TK, TK), TK) for kj in kjs]
            zts = [_dot_nt(k_ref[krows[t], _hcols(hh)], q_ref[pl.ds(q0, TQ), _hcols(hh)]) for t, hh in items]
            lgs = [-_softplus(zt) for zt in zts]
            if masked:
                lgs = [jnp.where(strict, lg, 0.0) for lg in lgs]
            parts = [_split2(lg) for lg in lgs]
            sufs = [_dot(later, hi) + _dot(later, lo) for hi, lo in parts]
            sums = [jnp.sum(lg, axis=0, keepdims=True) for lg in lgs]
            runs = {}
            for hh in heads:
                run = run_s[hh]
                for t in range(len(kjs)):
                    runs[t, hh] = run
                    run = run + sums[t * NH + hh]
                run_s[hh] = run
            ats = [jnp.exp(zts[i] + lgs[i] + runs[item] + sufs[i]) for i, item in enumerate(items)]
            if masked:
                ats = [jnp.where(strict, at, 0.0) for at in ats]
            for hh in heads:
                acc_s[hh] += sum(_dot(vt_ref[kjs[t], _hslot(hh), :], ats[t * NH + hh].astype(BF16))
                                 for t in range(len(kjs)))

        def q_loop(qi, _):
            q0 = pl.multiple_of(qi * TQ, TQ)
            run_s[...] = jnp.zeros_like(run_s)
            acc_s[...] = jnp.zeros_like(acc_s)
            tile(q0, [qi], True)

            def pair_loop(i, _):
                tile(q0, [qi - 1 - 2 * i, qi - 2 - 2 * i], False)
                return 0

            lax.fori_loop(0, qi // 2, pair_loop, 0)
            pl.when(qi % 2 == 1)(lambda: tile(q0, [0], False))
            o_ref[pl.ds(q0, TQ), :] = _heads_cat([_untranspose(acc_s[hh]) for hh in range(NH)]).astype(BF16)
            for hh in range(NH):
                lt_ref[hh, pl.ds(qi, 1), :] = run_s[hh]
            return 0

        lax.fori_loop(0, seq // TQ, q_loop, 0)
        pl.when(last_step)(gather_finish)

    out = pl.pallas_call(
        body, name="sb_fwd", grid=(batch, N_HEADS // NH),
        out_shape=[jax.ShapeDtypeStruct((batch * seq, D_BRANCH), BF16),
                   jax.ShapeDtypeStruct((batch, N_HEADS, seq // TQ, TQ), F32)] + _gather_shapes(shards),
        in_specs=[_group3_spec(0, seq), _group3_spec(1, seq), _tblock_spec(seq)] + _hbm_specs(n),
        out_specs=[_group_spec(seq), _qrow_spec(seq)] + _hbm_specs(n),
        scratch_shapes=[pltpu.VMEM((NH, 1, TQ), F32), pltpu.VMEM((NH, HEAD_SLOT, TQ), F32)] + _gather_sems(n),
        compiler_params=_serial_attn_params(),
    )(qkvb, qkvb, vbt, *shards)
    return out[0], out[1], out[2:]


def _sb_bwd(qkvb, kbt, do, ltot, batch, seq, chip_sums):
    n = len(chip_sums)

    def body(q_ref, k_ref, v_ref, kt_ref, do_ref, lt_ref, *rest):
        cs_refs, dqkv_ref, out_refs = rest[:n], rest[n], rest[n + 1:2 * n + 1]
        dk_acc, dv_acc, ls_s, gs_s, dqt_s = rest[2 * n + 1:2 * n + 6]
        chips_start, chips_finish = _chips_plan(cs_refs, out_refs, *rest[2 * n + 6:])
        first_step, last_step = _first_last_step()
        pl.when(first_step)(chips_start)
        strict = _key_query_mask(lambda r, c: r < c)
        upto = _tri(TK, lambda r, c: c <= r)
        before = _tri(TK, lambda r, c: c < r)
        dk_acc[...] = jnp.zeros_like(dk_acc)
        dv_acc[...] = jnp.zeros_like(dv_acc)

        def tile(qi, kj, masked):
            rows = pl.ds(pl.multiple_of(qi * TQ, TQ), TQ)
            krows = pl.ds(pl.multiple_of(kj * TK, TK), TK)
            heads = range(NH)
            qs = [q_ref[rows, _hcols(hh)] for hh in heads]
            douts = [do_ref[rows, _hcols(hh)] for hh in heads]
            zts = [_dot_nt(k_ref[krows, _hcols(hh)], qs[hh]) for hh in heads]
            das = [_dot_nt(v_ref[krows, _hcols(hh)], douts[hh]) for hh in heads]
            lgs = [-_softplus(zt) for zt in zts]
            if masked:
                lgs = [jnp.where(strict, lg, 0.0) for lg in lgs]
            parts = [_split2(lg) for lg in lgs]
            prefs = [_dot(upto, hi) + _dot(upto, lo) for hi, lo in parts]
            ats = [jnp.exp(zts[hh] + lgs[hh] + (lt_ref[hh, pl.ds(qi, 1), :] - ls_s[hh]) - prefs[hh]) for hh in heads]
            if masked:
                ats = [jnp.where(strict, at, 0.0) for at in ats]
            gts = [das[hh] * ats[hh] for hh in heads]
            us = [gs_s[hh] + _dot(before, gts[hh].astype(BF16)) for hh in heads]
            dzts = [(jnp.exp(lgs[hh]) * (gts[hh] + us[hh]) - us[hh]).astype(BF16) for hh in heads]
            for hh in heads:
                dk_acc[hh, krows, :] += _dot(dzts[hh], qs[hh])
                dv_acc[hh, krows, :] += _dot(ats[hh].astype(BF16), douts[hh])
                dqt_s[hh] += _dot(kt_ref[kj, _hslot(hh), :], dzts[hh])
                ls_s[hh] += jnp.sum(lgs[hh], axis=0, keepdims=True)
                gs_s[hh] += jnp.sum(gts[hh], axis=0, keepdims=True)

        def q_loop(qi, _):
            ls_s[...] = jnp.zeros_like(ls_s)
            gs_s[...] = jnp.zeros_like(gs_s)
            dqt_s[...] = jnp.zeros_like(dqt_s)

            def k_loop(kj, _):
                tile(qi, kj, False)
                return 0

            lax.fori_loop(0, qi, k_loop, 0)
            tile(qi, qi, True)
            dqkv_ref[0, pl.ds(pl.multiple_of(qi * TQ, TQ), TQ), :] = _heads_cat(
                [_untranspose(dqt_s[hh]) for hh in range(NH)]).astype(BF16)
            return 0

        lax.fori_loop(0, seq // TQ, q_loop, 0)
        dqkv_ref[1] = _heads_cat([dk_acc[hh] for hh in range(NH)]).astype(BF16)
        dqkv_ref[2] = _heads_cat([dv_acc[hh] for hh in range(NH)]).astype(BF16)
        pl.when(last_step)(chips_finish)

    out = pl.pallas_call(
        body, name="sb_bwd", grid=(batch, N_HEADS // NH),
        out_shape=[jax.ShapeDtypeStruct((3, batch * seq, D_BRANCH), BF16)]
        + [jax.ShapeDtypeStruct(s.shape, s.dtype) for s in chip_sums],
        in_specs=[_group3_spec(0, seq), _group3_spec(1, seq), _group3_spec(2, seq), _tblock_spec(seq),
                  _group_spec(seq), _qrow_spec(seq)] + _hbm_specs(n),
        out_specs=[pl.BlockSpec((3, seq, NH * HEAD_DIM), lambda b, g: (0, b, g))] + _hbm_specs(n),
        scratch_shapes=[pltpu.VMEM((NH, seq, HEAD_DIM), F32), pltpu.VMEM((NH, seq, HEAD_DIM), F32),
                        pltpu.VMEM((NH, 1, TQ), F32), pltpu.VMEM((NH, 1, TQ), F32),
                        pltpu.VMEM((NH, HEAD_SLOT, TQ), F32)] + _chips_sems(n),
        compiler_params=_serial_attn_params(),
    )(qkvb, qkvb, qkvb, kbt, do, ltot, *chip_sums)
    return out[0], out[1:]


def _forget_bwd(dcq_tok, dck_tok, fpre, batch, seq):
    t_len = batch * seq
    tiles = seq // TM

    def rev(i):
        return ((i // tiles) * tiles + (tiles - 1 - i % tiles), 0)

    def body(dcq_ref, dck_ref, f_ref, df_ref, db_ref, carry_ref):
        i = pl.program_id(0)

        @pl.when(i == 0)
        def _():
            db_ref[...] = jnp.zeros_like(db_ref)

        @pl.when(i % tiles == 0)
        def _():
            carry_ref[...] = jnp.zeros_like(carry_ref)

        dc = dcq_ref[...] - dck_ref[...]
        upper = _tri(TM, lambda r, c: c >= r)
        hi, mid, lo = _split3(dc)
        dlogf = carry_ref[...] + _dot(upper, hi) + _dot(upper, mid) + _dot(upper, lo)
        carry_ref[...] = carry_ref[...] + jnp.sum(dc, axis=0, keepdims=True)
        df = dlogf * _sigmoid(-f_ref[...])
        df_ref[...] = df.astype(BF16)
        db_ref[...] += jnp.sum(df, axis=0, keepdims=True)

    return pl.pallas_call(
        body, name="forget_bwd", grid=(t_len // TM,),
        out_shape=(jax.ShapeDtypeStruct((t_len, LANES), BF16), jax.ShapeDtypeStruct((1, LANES), F32)),
        in_specs=[pl.BlockSpec((TM, LANES), rev)] * 3,
        out_specs=(pl.BlockSpec((TM, LANES), rev), _acc_spec((1, LANES))),
        scratch_shapes=[pltpu.VMEM((1, LANES), F32)],
        compiler_params=_seq_params(),
    )(dcq_tok, dck_tok, fpre)


def _mix_fwd(o_fox, o_sb, gl, x, w_bf, w_bs, w_out, b_gate):
    t_len, d = x.shape

    def body(of_ref, os_ref, gl_ref, x_ref, wbf_ref, wbs_ref, wo_ref, bg_ref, x1_ref):
        br_f = _dot(of_ref[...], wbf_ref[...])
        br_s = _dot(os_ref[...], wbs_ref[...])
        ga = _sigmoid(gl_ref[:, :d].astype(F32) + bg_ref[0:1, :])
        gb = _sigmoid(gl_ref[:, d:].astype(F32) + bg_ref[1:2, :])
        merged = ga * br_f + gb * br_s
        x1_ref[...] = x_ref[...] + _dot(merged.astype(BF16), wo_ref[...])

    return pl.pallas_call(
        body, name="mix_fwd", grid=(t_len // TM,),
        out_shape=jax.ShapeDtypeStruct((t_len, d), F32),
        in_specs=[_row_spec(TM, D_BRANCH), _row_spec(TM, D_BRANCH), _row_spec(TM, 2 * d), _row_spec(TM, d),
                  _const_spec(w_bf.shape), _const_spec(w_bs.shape), _const_spec(w_out.shape), _const_spec(b_gate.shape)],
        out_specs=_row_spec(TM, d),
        compiler_params=_seq_params(),
    )(o_fox, o_sb, gl, x, w_bf, w_bs, w_out, b_gate)


def _ff_chunk(d_ff):
    return min(d_ff, 1024)


def _mlp_fwd(x1, g_mlp, w_up, w_down):
    t_len, d = x1.shape
    d_ff = w_up.shape[1]
    ch = _ff_chunk(d_ff)

    def body(x1_ref, g_ref, wu_ref, wd_ref, a_ref, x2_ref):
        x1v = x1_ref[...]
        xn, _ = _rms(x1v)
        h = (xn * g_ref[...]).astype(BF16)
        acc = x1v
        for j in range(d_ff // ch):
            a = _dot(h, wu_ref[:, j * ch:(j + 1) * ch])
            a_ref[:, j * ch:(j + 1) * ch] = a.astype(BF16)
            acc = acc + _dot(jnp.square(jnp.maximum(a, 0.0)).astype(BF16), wd_ref[j * ch:(j + 1) * ch, :])
        x2_ref[...] = acc

    return pl.pallas_call(
        body, name="mlp_fwd", grid=(t_len // TM,),
        out_shape=(jax.ShapeDtypeStruct((t_len, d_ff), BF16), jax.ShapeDtypeStruct((t_len, d), F32)),
        in_specs=[_row_spec(TM, d), _const_spec((1, d)), _const_spec(w_up.shape), _const_spec(w_down.shape)],
        out_specs=(_row_spec(TM, d_ff), _row_spec(TM, d)),
        compiler_params=_seq_params(),
    )(x1, g_mlp, w_up, w_down)


def _head_fwd_bwd(x2, p, target, g_ple, g_final, w_pg, w_ple):
    t_len, d = x2.shape
    d_ple = p.shape[1]

    def body(x2_ref, p_ref, t_ref, gp_ref, gf_ref, wpg_ref, wple_ref,
             dx2_ref, h3_ref, dpre_ref, dpe_ref, loss_ref, dgp_ref, dgf_ref):
        @pl.when(pl.program_id(0) == 0)
        def _():
            loss_ref[...] = jnp.zeros_like(loss_ref)
            dgp_ref[...] = jnp.zeros_like(dgp_ref)
            dgf_ref[...] = jnp.zeros_like(dgf_ref)

        x2v = x2_ref[...]
        x2n, r3 = _rms(x2v)
        h3 = (x2n * gp_ref[...]).astype(BF16)
        h3_ref[...] = h3
        gate = _sigmoid(_dot(h3, wpg_ref[...]))
        pe = _dot(p_ref[...].astype(BF16), wple_ref[...])
        x3n, r4 = _rms(x2v + gate * pe)
        err = x3n * gf_ref[...] - t_ref[...]
        loss_ref[...] += jnp.full(loss_ref.shape, (0.5 / d) * jnp.sum(err * err), F32)
        dx3, dgf = _rms_bwd(err * (1.0 / d), x3n, r4, gf_ref[...])
        dgf_ref[...] += dgf
        dpe_ref[...] = (dx3 * gate).astype(BF16)
        dpre = (dx3 * pe * gate * (1.0 - gate)).astype(BF16)
        dpre_ref[...] = dpre
        dres, dgp = _rms_bwd(_dot_nt(dpre, wpg_ref[...]), x2n, r3, gp_ref[...])
        dgp_ref[...] += dgp
        dx2_ref[...] = dx3 + dres

    shp_b = jax.ShapeDtypeStruct((t_len, d), BF16)
    return pl.pallas_call(
        body, name="head_fwd_bwd", grid=(t_len // TM,),
        out_shape=(jax.ShapeDtypeStruct((t_len, d), F32), shp_b, shp_b, shp_b,
                   jax.ShapeDtypeStruct((1, LANES), F32), jax.ShapeDtypeStruct((1, d), F32),
                   jax.ShapeDtypeStruct((1, d), F32)),
        in_specs=[_row_spec(TM, d), _row_spec(TM, d_ple), _row_spec(TM, d), _const_spec((1, d)), _const_spec((1, d)),
                  _const_spec(w_pg.shape), _const_spec(w_ple.shape)],
        out_specs=(_row_spec(TM, d), _row_spec(TM, d), _row_spec(TM, d), _row_spec(TM, d),
                   _acc_spec((1, LANES)), _acc_spec((1, d)), _acc_spec((1, d))),
        compiler_params=_seq_params(),
    )(x2, p, target, g_ple, g_final, w_pg, w_ple)


def _mlp_bwd(dx2, a, x1, g_mlp, w_up, w_down):
    t_len, d = x1.shape
    d_ff = w_up.shape[1]
    ch = _ff_chunk(d_ff)

    def body(dx2_ref, a_ref, x1_ref, g_ref, wu_ref, wd_ref, dx1_ref, da_ref, h2_ref, dg_ref):
        @pl.when(pl.program_id(0) == 0)
        def _():
            dg_ref[...] = jnp.zeros_like(dg_ref)

        dx2v = dx2_ref[...]
        dx2b = dx2v.astype(BF16)
        xn, r = _rms(x1_ref[...])
        h2_ref[...] = (xn * g_ref[...]).T.astype(BF16)
        dh = jnp.zeros((TM, d), F32)
        for j in range(d_ff // ch):
            dact = _dot_nt(dx2b, wd_ref[j * ch:(j + 1) * ch, :])
            da = (dact * 2.0 * jnp.maximum(a_ref[:, j * ch:(j + 1) * ch].astype(F32), 0.0)).astype(BF16)
            da_ref[:, j * ch:(j + 1) * ch] = da
            dh = dh + _dot_nt(da, wu_ref[:, j * ch:(j + 1) * ch])
        dres, dg = _rms_bwd(dh, xn, r, g_ref[...])
        dg_ref[...] += dg
        dx1_ref[...] = dx2v + dres

    return pl.pallas_call(
        body, name="mlp_bwd", grid=(t_len // TM,),
        out_shape=(jax.ShapeDtypeStruct((t_len, d), F32), jax.ShapeDtypeStruct((t_len, d_ff), BF16),
                   jax.ShapeDtypeStruct((d, t_len), BF16), jax.ShapeDtypeStruct((1, d), F32)),
        in_specs=[_row_spec(TM, d), _row_spec(TM, d_ff), _row_spec(TM, d), _const_spec((1, d)),
                  _const_spec(w_up.shape), _const_spec(w_down.shape)],
        out_specs=(_row_spec(TM, d), _row_spec(TM, d_ff), _col_spec(d, TM), _acc_spec((1, d))),
        compiler_params=_seq_params(),
    )(dx2, a, x1, g_mlp, w_up, w_down)


def _mix_bwd(dx1, o_fox, o_sb, gl, w_bf, w_bs, w_out, b_gate):
    t_len, d = dx1.shape

    def body(dx1_ref, of_ref, os_ref, gl_ref, wbf_ref, wbs_ref, wo_ref, bg_ref,
             mg_ref, dbf_ref, dbs_ref, dgl_ref, dof_ref, dos_ref, dbg_ref):
        @pl.when(pl.program_id(0) == 0)
        def _():
            dbg_ref[...] = jnp.zeros_like(dbg_ref)

        dmerged = _dot_nt(dx1_ref[...].astype(BF16), wo_ref[...])
        br_f = _dot(of_ref[...], wbf_ref[...])
        br_s = _dot(os_ref[...], wbs_ref[...])
        ga = _sigmoid(gl_ref[:, :d].astype(F32) + bg_ref[0:1, :])
        gb = _sigmoid(gl_ref[:, d:].astype(F32) + bg_ref[1:2, :])
        mg_ref[...] = (ga * br_f + gb * br_s).astype(BF16)
        dbf = (dmerged * ga).astype(BF16)
        dbs = (dmerged * gb).astype(BF16)
        dbf_ref[...] = dbf
        dbs_ref[...] = dbs
        dla = dmerged * br_f * ga * (1.0 - ga)
        dlb = dmerged * br_s * gb * (1.0 - gb)
        dgl_ref[:, :d] = dla.astype(BF16)
        dgl_ref[:, d:] = dlb.astype(BF16)
        dbg_ref[0:1, :] += jnp.sum(dla, axis=0, keepdims=True)
        dbg_ref[1:2, :] += jnp.sum(dlb, axis=0, keepdims=True)
        dof_ref[...] = _dot_nt(dbf, wbf_ref[...]).astype(BF16)
        dos_ref[...] = _dot_nt(dbs, wbs_ref[...]).astype(BF16)

    shp_d = jax.ShapeDtypeStruct((t_len, d), BF16)
    shp_h = jax.ShapeDtypeStruct((t_len, D_BRANCH), BF16)
    return pl.pallas_call(
        body, name="mix_bwd", grid=(t_len // TM,),
        out_shape=(shp_d, shp_d, shp_d, jax.ShapeDtypeStruct((t_len, 2 * d), BF16), shp_h, shp_h,
                   jax.ShapeDtypeStruct((2, d), F32)),
        in_specs=[_row_spec(TM, d), _row_spec(TM, D_BRANCH), _row_spec(TM, D_BRANCH), _row_spec(TM, 2 * d),
                  _const_spec(w_bf.shape), _const_spec(w_bs.shape), _const_spec(w_out.shape), _const_spec(b_gate.shape)],
        out_specs=(_row_spec(TM, d), _row_spec(TM, d), _row_spec(TM, d), _row_spec(TM, 2 * d),
                   _row_spec(TM, D_BRANCH), _row_spec(TM, D_BRANCH), _acc_spec((2, d))),
        compiler_params=_seq_params(),
    )(dx1, o_fox, o_sb, gl, w_bf, w_bs, w_out, b_gate)


def _inproj_bwd(dqk_f, dv_f, dqkv_b, dgl, df, dx1, x, g_mix, w_pad, chip_sums):
    t_len, d = x.shape
    lay, _ = _pad_layout(d)
    slot_w = N_HEADS * HEAD_SLOT
    n = len(chip_sums)
    n_tiles = t_len // TM

    def body(dqk_ref, dvf_ref, db_ref, dgl_ref, df_ref, dx1_ref, x_ref, g_ref, w_ref, *rest):
        cs_refs, (dx_ref, dg_ref), out_refs = rest[:n], rest[n:n + 2], rest[n + 2:2 * n + 2]
        chips_start, chips_finish = _chips_plan(cs_refs, out_refs, *rest[2 * n + 2:])

        @pl.when(pl.program_id(0) == 0)
        def _():
            dg_ref[...] = jnp.zeros_like(dg_ref)
            chips_start()

        def back(piece, name):
            lo, hi = lay[name]
            return _dot(piece, w_ref[lo:hi, :])

        xn, r = _rms(x_ref[...])
        dh = (back(df_ref[...], "forget") + back(dgl_ref[...], "gates") + back(dqk_ref[0], "qf")
              + back(dqk_ref[1], "kf") + back(dvf_ref[...], "vf") + back(db_ref[0], "qb") + back(db_ref[1], "kb")
              + back(db_ref[2], "vb"))
        dres, dg = _rms_bwd(dh, xn, r, g_ref[...])
        dg_ref[...] += dg
        dx_ref[...] = dx1_ref[...] + dres
        pl.when(pl.program_id(0) == n_tiles - 1)(chips_finish)

    out = pl.pallas_call(
        body, name="inproj_bwd", grid=(n_tiles,),
        out_shape=[jax.ShapeDtypeStruct((t_len, d), F32), jax.ShapeDtypeStruct((1, d), F32)]
        + [jax.ShapeDtypeStruct(s.shape, s.dtype) for s in chip_sums],
        in_specs=[_row3_spec(2, TM, slot_w), _row_spec(TM, D_BRANCH), _row3_spec(3, TM, D_BRANCH),
                  _row_spec(TM, 2 * d), _row_spec(TM, LANES), _row_spec(TM, d), _row_spec(TM, d), _const_spec((1, d)),
                  _const_spec(w_pad.shape)] + _hbm_specs(n),
        out_specs=[_row_spec(TM, d), _acc_spec((1, d))] + _hbm_specs(n),
        scratch_shapes=_chips_sems(n),
        compiler_params=_seq_params(),
    )(dqk_f, dv_f, dqkv_b, dgl, df, dx1, x, g_mix, w_pad, *chip_sums)
    return out[0], out[1], out[2:]


def _cols_to_slabs(full):
    r, c8 = full.shape
    return full.reshape(r, N_DEV, c8 // N_DEV).transpose(1, 0, 2)


def _slabs_to_cols(slabs):
    n, r, c = slabs.shape
    return slabs.transpose(1, 0, 2).reshape(r, n * c)


def _win_sizes(d):
    return (D_BRANCH, D_BRANCH, D_BRANCH, N_HEADS, D_BRANCH, D_BRANCH, D_BRANCH, d, d)


def _split_win(w_t, d):
    out, off = [], 0
    for s in _win_sizes(d):
        out.append(w_t[off:off + s])
        off += s
    return out


def _to_slots(w_t):
    c = w_t.shape[1]
    return jnp.pad(w_t.reshape(N_HEADS, HEAD_DIM, c), ((0, 0), (0, HEAD_SLOT - HEAD_DIM), (0, 0))).reshape(-1, c)


def _from_slots(w_t):
    c = w_t.shape[1]
    return w_t.reshape(N_HEADS, HEAD_SLOT, c)[:, :HEAD_DIM].reshape(N_HEADS * HEAD_DIM, c)


def _pad_win(w_full_t, d):
    qa, ka, va, fa, qb, kb, vb, ga, gb = _split_win(w_full_t, d)
    scale = HEAD_DIM ** -0.5
    fpad = jnp.pad(fa, ((0, LANES - N_HEADS), (0, 0)))
    return jnp.concatenate([_to_slots(qa * scale), _to_slots(ka), va, qb * scale, kb, vb, ga, gb, fpad], axis=0)


def _unpad_dwin(dqk_f, dv_f, dqkv_b, dgates, dforget, d):
    scale = HEAD_DIM ** -0.5
    return jnp.concatenate([_from_slots(dqk_f[0]) * scale, _from_slots(dqk_f[1]), dv_f, dforget[:N_HEADS],
                            dqkv_b[0] * scale, dqkv_b[1], dqkv_b[2], dgates], axis=0)


def _c_lane_constants():
    head = jnp.arange(LANES)[:, None]
    lane = jnp.arange(N_HEADS * HEAD_SLOT)[None, :]
    in_head = (lane // HEAD_SLOT == head) & (head < N_HEADS)

    def place(first):
        return jnp.stack([(in_head & (lane % HEAD_SLOT == first + j)) for j in range(3)]).astype(BF16)

    def ones(first):
        off = lane % HEAD_SLOT
        return ((off >= first) & (off < first + 3)).astype(F32)

    return place(C_TERMS_Q), place(C_TERMS_K), ones(C_ONES_Q), ones(C_ONES_K)


def _pad_rows(a, rows):
    return jnp.pad(a, [(0, 0)] * (a.ndim - 2) + [(0, rows - a.shape[-2]), (0, 0)])


def kernel(x, p, g_mix, w_in, b_forget, b_gate, w_branch_fox, w_branch_sb, w_out, g_mlp, w_up, w_down, g_ple, w_ple_gate, w_ple, g_final, loss_target, m_g_mix, m_w_in, m_b_forget, m_b_gate, m_w_branch_fox, m_w_branch_sb, m_w_out, m_g_mlp, m_w_up, m_w_down, m_g_ple, m_w_ple_gate, m_w_ple, m_g_final, v_g_mix, v_w_in, v_b_forget, v_b_gate, v_w_branch_fox, v_w_branch_sb, v_w_out, v_g_mlp, v_w_up, v_w_down, v_g_ple, v_w_ple_gate, v_w_ple, v_g_final):
    batch, seq, d = x.shape
    t_len = batch * seq
    d_ple = p.shape[-1]
    d_ff = w_up.shape[-1] * N_DEV
    dn = d // N_DEV
    fn = d_ff // N_DEV
    my_c = lax.axis_index("c")
    my_dev = 4 * lax.axis_index("x") + 2 * lax.axis_index("y") + my_c

    bg_hi = b_gate[0].astype(BF16)
    bg_r = b_gate[0] - bg_hi.astype(F32)
    bg_mid = bg_r.astype(BF16)
    bg_lo = (bg_r - bg_mid.astype(F32)).astype(BF16)
    narrow_rows = 2 * D_BRANCH + d_ple + 6
    narrow_rows_pad = -(-narrow_rows // 16) * 16
    narrow = _pad_rows(jnp.concatenate(
        [w_branch_fox[0].astype(BF16), w_branch_sb[0].astype(BF16), w_ple[0].astype(BF16), bg_hi, bg_mid, bg_lo],
        axis=0), narrow_rows_pad)
    g_in, = _all_gather([w_in[0].T.astype(BF16)])
    w_pad = _pad_win(g_in.reshape(-1, d), d)
    bf_pad = jnp.pad(b_forget, ((0, 0), (0, LANES - N_HEADS)))
    place_q, place_k, ones_q, ones_k = _c_lane_constants()

    x2d = x.reshape(t_len, d)
    p2d = p.reshape(t_len, d_ple)
    tgt2d = loss_target.reshape(t_len, d)
    qf, kf, kft, vf, vft, qkvb, kbt, vbt, gl, fpre, h1 = _inproj_fwd(
        x2d, g_mix, w_pad, bf_pad, place_q, place_k, ones_q, ones_k, seq)
    o_sb, ltot, (g_up, g_out, g_down, g_pg, g_narrow) = _sb_fwd(qkvb, vbt, batch, seq, [
        w_up[0].astype(BF16), w_out[0].astype(BF16), w_down[0].astype(BF16), w_ple_gate[0].astype(BF16), narrow])
    o_fox, lse = _fox_fwd(qf, kf, vft, batch, seq)
    w_up_full = _slabs_to_cols(g_up)
    w_out_full = g_out.reshape(d, d)
    w_down_full = g_down.reshape(d_ff, d)
    w_pg_full = g_pg.reshape(d, d)
    w_bf_full = _slabs_to_cols(g_narrow[:, :D_BRANCH])
    w_bs_full = _slabs_to_cols(g_narrow[:, D_BRANCH:2 * D_BRANCH])
    w_ple_full = _slabs_to_cols(g_narrow[:, 2 * D_BRANCH:2 * D_BRANCH + d_ple])
    bg_terms = g_narrow[:, 2 * D_BRANCH + d_ple:narrow_rows].astype(F32)
    b_gate_full = _slabs_to_cols(bg_terms[:, 0:2] + bg_terms[:, 2:4] + bg_terms[:, 4:6])
    x1 = _mix_fwd(o_fox, o_sb, gl, x2d, w_bf_full, w_bs_full, w_out_full, b_gate_full)
    a_up, x2 = _mlp_fwd(x1, g_mlp, w_up_full, w_down_full)

    dx2, h3, dpre, dpe, loss_acc, dg_ple, dg_final = _head_fwd_bwd(
        x2, p2d, tgt2d, g_ple, g_final.reshape(1, d), w_pg_full, w_ple_full)
    dx1, da_up, h2t, dg_mlp = _mlp_bwd(dx2, a_up, x1, g_mlp, w_up_full, w_down_full)
    merged, dbr_f, dbr_s, dgl, do_fox, do_sb, dbg = _mix_bwd(
        dx1, o_fox, o_sb, gl, w_bf_full, w_bs_full, w_out_full, b_gate_full)

    def column_shards(name, lhs, rhs, lhs_t=False):
        if (rhs.shape[-1] // N_DEV) % (4 * LANES) == 0:
            return _matmul_tn(name, lhs, rhs, slabs=True, lhs_t=lhs_t)
        return _cols_to_slabs(_matmul_tn(name, lhs, rhs, lhs_t=lhs_t))

    if fn % (4 * LANES) == 0:
        part_up, = _matmul_tn_once("dw_up", [h2t], da_up, slabs=True, lhs_t=True)
    else:
        part_up = column_shards("dw_up", h2t, da_up, lhs_t=True)
    part_out = _matmul_tn("dw_out", merged, dx1).reshape(N_DEV, dn, d)
    part_down = _matmul_tn_once("dw_down", [a_up], dx2, relu2=True)[0].reshape(N_DEV, fn, d)
    part_pg = _matmul_tn("dw_ple_gate", h3, dpre).reshape(N_DEV, dn, d)
    part_narrow = _pad_rows(jnp.concatenate(
        [column_shards("dw_branch_fox", o_fox, dbr_f), column_shards("dw_branch_sb", o_sb, dbr_s),
         column_shards("dw_ple", p2d, dpe)], axis=1), narrow_rows_pad)
    early = [part_up, part_out, part_down, part_pg, lax.optimization_barrier(part_narrow)]

    dqk_f, dv_f, dc_queries, dc_keys, early_recv = _fox_bwd(
        qf, kf, kft, vf, o_fox, do_fox, lse, batch, seq, early)
    early_sums = [_pair_add("pair_add_%d" % i, pt, rc, my_c) for i, (pt, rc) in enumerate(zip(early, early_recv))]
    dqkv_b, (s_up, s_out, s_down, s_pg, s_narrow) = _sb_bwd(qkvb, kbt, do_sb, ltot, batch, seq, early_sums)
    dcq_tok = dc_queries.reshape(batch, N_HEADS, seq).transpose(0, 2, 1).reshape(t_len, N_HEADS)
    dck_tok = dc_keys[..., :NH].transpose(0, 2, 1, 3).reshape(t_len, N_HEADS)
    lane_pad = ((0, 0), (0, LANES - N_HEADS))
    df, db_forget = _forget_bwd(jnp.pad(dcq_tok, lane_pad), jnp.pad(dck_tok, lane_pad), fpre, batch, seq)

    gw_in = _unpad_dwin(*_matmul_tn_once("dw_in_fox_qk", [dqk_f], h1),
                        *_matmul_tn_once("dw_in_rest", [dv_f, dqkv_b, dgl, df], h1), d)
    part_in = lax.optimization_barrier(gw_in.reshape(N_DEV, -1, d))
    recv_in, = _rs_core_pair("reduce_scatter_core_pair_w_in", [part_in])
    grad_x, dg_mix, (s_in,) = _inproj_bwd(dqk_f, dv_f, dqkv_b, dgl, df, dx1, x2d, g_mix, w_pad,
                                          [_pair_add("pair_add_w_in", part_in, recv_in, my_c)])

    small = jnp.concatenate([
        dg_mix, dg_mlp, dg_ple, dg_final, jnp.pad(db_forget[:, :N_HEADS], ((0, 0), (0, d - N_HEADS))), dbg,
        jnp.pad(loss_acc[:, :1], ((0, 0), (0, d - 1)))], axis=0)
    small = _all_reduce_small(small)
    loss = small[7, 0]
    small_grads = {
        "g_mix": small[0:1], "g_mlp": small[1:2], "g_ple": small[2:3], "g_final": small[3:4],
        "b_forget": small[4:5, :N_HEADS],
        "b_gate": lax.dynamic_slice_in_dim(small[5:7], my_dev * dn, dn, axis=1),
    }

    weights = {"g_mix": g_mix, "w_in": w_in, "b_forget": b_forget, "b_gate": b_gate, "w_branch_fox": w_branch_fox,
               "w_branch_sb": w_branch_sb, "w_out": w_out, "g_mlp": g_mlp, "w_up": w_up, "w_down": w_down,
               "g_ple": g_ple, "w_ple_gate": w_ple_gate, "w_ple": w_ple, "g_final": g_final}
    m_in = {"g_mix": m_g_mix, "w_in": m_w_in, "b_forget": m_b_forget, "b_gate": m_b_gate,
            "w_branch_fox": m_w_branch_fox, "w_branch_sb": m_w_branch_sb, "w_out": m_w_out, "g_mlp": m_g_mlp,
            "w_up": m_w_up, "w_down": m_w_down, "g_ple": m_g_ple, "w_ple_gate": m_w_ple_gate, "w_ple": m_w_ple,
            "g_final": m_g_final}
    v_in = {"g_mix": v_g_mix, "w_in": v_w_in, "b_forget": v_b_forget, "b_gate": v_b_gate,
            "w_branch_fox": v_w_branch_fox, "w_branch_sb": v_w_branch_sb, "w_out": v_w_out, "g_mlp": v_g_mlp,
            "w_up": v_w_up, "w_down": v_w_down, "g_ple": v_g_ple, "w_ple_gate": v_w_ple_gate, "w_ple": v_w_ple,
            "g_final": v_g_final}
    names = list(weights)

    def as2d(a):
        return a.reshape(-1, a.shape[-1])

    result = {}
    big = {"w_up": (s_up, 0), "w_out": (s_out, 0), "w_down": (s_down, 0), "w_ple_gate": (s_pg, 0),
           "w_branch_fox": (s_narrow, 0), "w_branch_sb": (s_narrow, D_BRANCH), "w_ple": (s_narrow, 2 * D_BRANCH)}
    for n, (parts, off) in big.items():
        result[n] = _adamw_parts("adamw_" + n, as2d(weights[n]), parts, off, as2d(m_in[n]), as2d(v_in[n]))
    result["w_in"] = tuple(r.T for r in _adamw_parts("adamw_w_in", w_in[0].T, s_in, 0, m_w_in[0].T, v_w_in[0].T))
    small_names = list(small_grads)
    small_out = _adamw_small([(as2d(weights[n]), small_grads[n], as2d(m_in[n]), as2d(v_in[n])) for n in small_names])
    for n, (dlt, nm, nv) in zip(small_names, small_out):
        result[n] = (small_grads[n], dlt, nm, nv)
    outs = [[result[n][k].reshape(weights[n].shape) for n in names] for k in range(4)]
    return (loss, grad_x.reshape(x.shape), *outs[0], *outs[1], *outs[2], *outs[3])
```

```python
import jax
import jax.numpy as jnp
from jax import lax
from jax.experimental import pallas as pl
from jax.experimental.pallas import tpu as pltpu

F32 = jnp.float32
BF16 = jnp.bfloat16

HEAD_DIM = 64
N_HEADS = 8
D_BRANCH = N_HEADS * HEAD_DIM
EPS = 1e-6
ADAM_LR = 0.001
ADAM_B1 = 0.9
ADAM_B2 = 0.999
ADAM_EPS = 1e-08
ADAM_WD = 0.01
ADAM_STEP = 10

N_DEV = 8
LANES = 128
TM = 256
TQ = 256
TK = 256
NH = 4
HEAD_SLOT = 128
C_TERMS_Q = 64
C_ONES_K = 64
C_TERMS_K = 67
C_ONES_Q = 67
NEG = -1e30
VMEM_LIMIT = 56 * 1024 * 1024
MESH = pl.DeviceIdType.MESH


def _dot(a, b):
    return jnp.dot(a, b, preferred_element_type=F32)


def _dot_nt(a, b):
    return lax.dot_general(a, b, (((1,), (1,)), ((), ())), preferred_element_type=F32)


def _dot_tn(a, b):
    return lax.dot_general(a, b, (((0,), (0,)), ((), ())), preferred_element_type=F32)


def _sigmoid(x):
    return 1.0 / (1.0 + jnp.exp(-x))


def _softplus(x):
    return jnp.maximum(x, 0.0) + jnp.log(1.0 + jnp.exp(-jnp.abs(x)))


def _split2(x):
    hi = x.astype(BF16)
    lo = (x - hi.astype(F32)).astype(BF16)
    return hi, lo


def _split3(x):
    hi = x.astype(BF16)
    r = x - hi.astype(F32)
    mid = r.astype(BF16)
    lo = (r - mid.astype(F32)).astype(BF16)
    return hi, mid, lo


def _tri(n, rel):
    r = lax.broadcasted_iota(jnp.int32, (n, n), 0)
    c = lax.broadcasted_iota(jnp.int32, (n, n), 1)
    return rel(r, c).astype(BF16)


def _rms(x):
    r = lax.rsqrt(jnp.mean(x * x, axis=-1, keepdims=True) + EPS)
    return x * r, r


def _rms_bwd(dh, xn, r, g):
    dxn = dh * g
    dx = r * (dxn - xn * jnp.mean(dxn * xn, axis=-1, keepdims=True))
    return dx, jnp.sum(dh * xn, axis=0, keepdims=True)


def _row_spec(tm, cols):
    return pl.BlockSpec((tm, cols), lambda i: (i, 0))


def _row3_spec(g, tm, cols):
    return pl.BlockSpec((g, tm, cols), lambda i: (0, i, 0))


def _col_spec(rows, tm):
    return pl.BlockSpec((rows, tm), lambda i: (0, i))


def _const_spec(shape):
    nd = len(shape)
    return pl.BlockSpec(shape, lambda i: (0,) * nd, pipeline_mode=pl.Buffered(1))


def _acc_spec(shape):
    nd = len(shape)
    return pl.BlockSpec(shape, lambda i: (0,) * nd)


def _seq_params():
    return pltpu.CompilerParams(dimension_semantics=("arbitrary",), vmem_limit_bytes=VMEM_LIMIT)


def _mesh_pos():
    return lax.axis_index("x"), lax.axis_index("y"), lax.axis_index("c")


def _other_chips(x, y):
    return [(1 - x, y), (x, 1 - y), (1 - x, 1 - y)]


def _hbm_specs(n):
    return [pl.BlockSpec(memory_space=pl.ANY)] * n


def _gather_plan(x_refs, out_refs, send_sems, recv_sems, local_sems):
    n = len(x_refs)
    x, y, c = _mesh_pos()
    me, sibling = (x, y, c), (x, y, 1 - c)
    chips = _other_chips(x, y)

    def index(px, py, pc):
        return 4 * px + 2 * py + pc

    def copy(a, k, block, to, src=None):
        slab = out_refs[a].at[index(*block)]
        return pltpu.make_async_remote_copy(
            src_ref=slab if src is None else src, dst_ref=slab,
            send_sem=send_sems.at[7 * a + k], recv_sem=recv_sems.at[7 * a + k], device_id=to, device_id_type=MESH)

    mine = [pltpu.make_async_copy(x_refs[a], out_refs[a].at[index(*me)], local_sems.at[a]) for a in range(n)]
    first = []
    for a in range(n):
        first.append(copy(a, 0, me, sibling, src=x_refs[a]))
        first += [copy(a, 1 + j, me, (cx, cy, c), src=x_refs[a]) for j, (cx, cy) in enumerate(chips)]

    def start():
        for cp in mine + first:
            cp.start()

    def finish():
        passed = []
        for j, (cx, cy) in enumerate(chips):
            for a in range(n):
                copy(a, 1 + j, (cx, cy, c), me).wait_recv()
                passed.append(copy(a, 4 + j, (cx, cy, c), sibling))
                passed[-1].start()
        for a in range(n):
            copy(a, 0, sibling, me).wait_recv()
            for j, (cx, cy) in enumerate(chips):
                copy(a, 4 + j, (cx, cy, 1 - c), me).wait_recv()
        for cp in first + passed:
            cp.wait_send()
        for cp in mine:
            cp.wait()

    return start, finish


def _gather_shapes(shards):
    return [jax.ShapeDtypeStruct((N_DEV,) + s.shape, s.dtype) for s in shards]


def _gather_sems(n):
    return [pltpu.SemaphoreType.DMA((7 * n,)), pltpu.SemaphoreType.DMA((7 * n,)), pltpu.SemaphoreType.DMA((n,))]


def _all_gather(shards):
    n = len(shards)

    def body(*refs):
        start, finish = _gather_plan(refs[:n], refs[n:2 * n], *refs[2 * n:])
        start()
        finish()

    return pl.pallas_call(
        body, name="all_gather_weights", out_shape=_gather_shapes(shards),
        in_specs=_hbm_specs(n), out_specs=_hbm_specs(n), scratch_shapes=_gather_sems(n),
    )(*shards)


def _pair_plan(p_refs, recv_refs, send_sems, recv_sems):
    n = len(p_refs)
    x, y, c = _mesh_pos()
    sibling = (x, y, 1 - c)

    def start():
        for a in range(n):
            for chip in range(4):
                pltpu.make_async_remote_copy(
                    src_ref=p_refs[a].at[2 * chip + (1 - c)], dst_ref=recv_refs[a].at[chip],
                    send_sem=send_sems.at[a], recv_sem=recv_sems.at[a], device_id=sibling, device_id_type=MESH).start()

    def finish():
        for a in range(n):
            pltpu.make_async_remote_copy(
                src_ref=recv_refs[a], dst_ref=recv_refs[a], send_sem=send_sems.at[a], recv_sem=recv_sems.at[a],
                device_id=sibling, device_id_type=MESH).wait()

    return start, finish


def _pair_shapes(partials):
    return [jax.ShapeDtypeStruct((4,) + s.shape[1:], s.dtype) for s in partials]


def _pair_sems(n):
    return [pltpu.SemaphoreType.DMA((n,)), pltpu.SemaphoreType.DMA((n,))]


def _rs_core_pair(name, partials):
    n = len(partials)

    def body(*refs):
        start, finish = _pair_plan(refs[:n], refs[n:2 * n], *refs[2 * n:])
        start()
        finish()

    return pl.pallas_call(
        body, name=name, out_shape=_pair_shapes(partials),
        in_specs=_hbm_specs(n), out_specs=_hbm_specs(n), scratch_shapes=_pair_sems(n),
    )(*partials)


def _chips_plan(cs_refs, out_refs, send_sems, recv_sems, local_sems):
    n = len(cs_refs)
    x, y, c = _mesh_pos()
    chip = 2 * x + y
    chips = _other_chips(x, y)
    mine = [pltpu.make_async_copy(cs_refs[a].at[chip], out_refs[a].at[chip], local_sems.at[a]) for a in range(n)]
    sends = [pltpu.make_async_remote_copy(
        src_ref=cs_refs[a].at[2 * cx + cy], dst_ref=out_refs[a].at[chip],
        send_sem=send_sems.at[3 * a + j], recv_sem=recv_sems.at[3 * a + j],
        device_id=(cx, cy, c), device_id_type=MESH) for a in range(n) for j, (cx, cy) in enumerate(chips)]

    def start():
        for cp in mine + sends:
            cp.start()

    def finish():
        for a in range(n):
            for j, (cx, cy) in enumerate(chips):
                pltpu.make_async_remote_copy(
                    src_ref=cs_refs[a].at[chip], dst_ref=out_refs[a].at[2 * cx + cy],
                    send_sem=send_sems.at[3 * a + j], recv_sem=recv_sems.at[3 * a + j],
                    device_id=(x, y, c), device_id_type=MESH).wait_recv()
        for cp in sends:
            cp.wait_send()
        for cp in mine:
            cp.wait()

    return start, finish


def _chips_sems(n):
    return [pltpu.SemaphoreType.DMA((3 * n,)), pltpu.SemaphoreType.DMA((3 * n,)), pltpu.SemaphoreType.DMA((n,))]


def _all_reduce_small(vec):
    rows, cols = vec.shape

    def body(x_ref, land_ref, sum_ref, send_sems, recv_sems):
        x, y, c = _mesh_pos()
        me = 4 * x + 2 * y + c
        land_ref[me] = x_ref[...]
        flips = [(fx, fy, fc) for fx in (0, 1) for fy in (0, 1) for fc in (0, 1)][1:]

        def flipped(f):
            return tuple((1 - v) if b else v for v, b in zip((x, y, c), f))

        sends = []
        for k, f in enumerate(flips):
            sends.append(pltpu.make_async_remote_copy(
                src_ref=x_ref, dst_ref=land_ref.at[me], send_sem=send_sems.at[k], recv_sem=recv_sems.at[k],
                device_id=flipped(f), device_id_type=MESH))
            sends[-1].start()
        for k, f in enumerate(flips):
            px, py, pc = flipped(f)
            pltpu.make_async_remote_copy(
                src_ref=x_ref, dst_ref=land_ref.at[4 * px + 2 * py + pc], send_sem=send_sems.at[k],
                recv_sem=recv_sems.at[k], device_id=(x, y, c), device_id_type=MESH).wait_recv()
        for cp in sends:
            cp.wait_send()
        total = land_ref[0]
        for d in range(1, N_DEV):
            total = total + land_ref[d]
        sum_ref[...] = total

    vm = pl.BlockSpec(memory_space=pltpu.VMEM)
    return pl.pallas_call(
        body, name="all_reduce_small",
        out_shape=(jax.ShapeDtypeStruct((N_DEV, rows, cols), F32), jax.ShapeDtypeStruct((rows, cols), F32)),
        in_specs=[vm], out_specs=(vm, vm),
        scratch_shapes=[pltpu.SemaphoreType.DMA((7,)), pltpu.SemaphoreType.DMA((7,))],
    )(vec)[1]


def _block_rows(rows, cols, itemsize, align, row_off=0):
    best = None
    for t in range(align, rows + 1, align):
        if rows % t == 0 and row_off % t == 0 and t * cols * itemsize <= (1 << 20):
            best = t
    return rows if best is None else best


def _pair_add(name, partial, recv, my_c):
    _, rows, cols = partial.shape
    br = _block_rows(rows, cols, 2, 16)

    def body(c_ref, a_ref, b_ref, o_ref):
        o_ref[...] = (a_ref[...].astype(F32) + b_ref[...].astype(F32)).astype(BF16)

    return pl.pallas_call(
        body, name=name,
        grid_spec=pltpu.PrefetchScalarGridSpec(
            num_scalar_prefetch=1, grid=(4, rows // br),
            in_specs=[pl.BlockSpec((None, None, br, cols), lambda j, i, c_ref: (j, c_ref[0], i, 0)),
                      pl.BlockSpec((None, br, cols), lambda j, i, c_ref: (j, i, 0))],
            out_specs=pl.BlockSpec((None, br, cols), lambda j, i, c_ref: (j, i, 0))),
        out_shape=jax.ShapeDtypeStruct((4, rows, cols), BF16),
    )(my_c.reshape(1).astype(jnp.int32), partial.reshape(4, 2, rows, cols), recv)


def _adam_update(w, g, m, v):
    nm = ADAM_B1 * m + (1.0 - ADAM_B1) * g
    nv = ADAM_B2 * v + (1.0 - ADAM_B2) * (g * g)
    m_hat = nm / (1.0 - ADAM_B1 ** ADAM_STEP)
    v_hat = nv / (1.0 - ADAM_B2 ** ADAM_STEP)
    return -ADAM_LR * (m_hat / (jnp.sqrt(v_hat) + ADAM_EPS) + ADAM_WD * w), nm, nv


def _adamw_parts(name, w, parts, row_off, m, v):
    rows, cols = w.shape
    tr = _block_rows(rows, cols, 4, 16, row_off)
    tc = cols
    if tr == rows and rows % 16 != 0 and cols % (2 * LANES) == 0:
        tc = 2 * LANES
    assert rows % tr == 0 and row_off % tr == 0 and (tc == cols or row_off == 0)
    off = row_off // tr

    def body(w_ref, p_ref, m_ref, v_ref, g_ref, d_ref, nm_ref, nv_ref):
        g = p_ref[0].astype(F32)
        for j in range(1, 4):
            g = g + p_ref[j].astype(F32)
        g_ref[...] = g
        d_ref[...], nm_ref[...], nv_ref[...] = _adam_update(w_ref[...], g, m_ref[...], v_ref[...])

    spec = pl.BlockSpec((tr, tc), lambda i, j: (i, j))
    shp = jax.ShapeDtypeStruct((rows, cols), F32)
    return pl.pallas_call(
        body, name=name, grid=(rows // tr, cols // tc), out_shape=(shp,) * 4,
        in_specs=[spec, pl.BlockSpec((4, tr, tc), lambda i, j: (0, off + i, j)), spec, spec], out_specs=(spec,) * 4,
    )(w, parts, m, v)


def _adamw_small(tensors):
    n = len(tensors)

    def body(*refs):
        ins, outs = refs[:4 * n], refs[4 * n:]
        for t in range(n):
            w_ref, g_ref, m_ref, v_ref = ins[4 * t:4 * t + 4]
            d, nm, nv = _adam_update(w_ref[...], g_ref[...], m_ref[...], v_ref[...])
            outs[3 * t][...], outs[3 * t + 1][...], outs[3 * t + 2][...] = d, nm, nv

    vm = pl.BlockSpec(memory_space=pltpu.VMEM)
    out = pl.pallas_call(
        body, name="adamw_small",
        out_shape=[jax.ShapeDtypeStruct(t[0].shape, F32) for t in tensors for _ in range(3)],
        in_specs=[vm] * (4 * n), out_specs=[vm] * (3 * n),
    )(*[a for t in tensors for a in t])
    return [tuple(out[3 * t:3 * t + 3]) for t in range(n)]


def _matmul_tn(name, a, b, relu2=False, slabs=False, lhs_t=False):
    a_groups = a.shape[0] if a.ndim == 3 else 0
    b_groups = b.shape[0] if b.ndim == 3 else 0
    groups = max(a_groups, b_groups, 1)
    assert not (a_groups and b_groups) and not (a_groups and lhs_t)
    a3 = a if a_groups else a[None]
    b3 = b if b_groups else b[None]
    t_len, k_len = a3.shape[1:][::-1] if lhs_t else a3.shape[1:]
    n_len = b3.shape[2]
    tt = min(t_len, 512)
    tk = min(k_len, 1024)
    tn = n_len // N_DEV if slabs else min(n_len, 1024)
    nt = t_len // tt
    assert not slabs or (groups == 1 and tn <= 1024)

    def body(a_ref, b_ref, o_ref, acc_ref):
        @pl.when(pl.program_id(3) == 0)
        def _():
            acc_ref[...] = jnp.zeros_like(acc_ref)

        av = a_ref[...]
        if relu2:
            av = jnp.square(jnp.maximum(av.astype(F32), 0.0))
        product = _dot if lhs_t else _dot_tn
        acc_ref[...] += product(av.astype(BF16), b_ref[...].astype(BF16))

        @pl.when(pl.program_id(3) == nt - 1)
        def _():
            o_ref[...] = acc_ref[...].astype(BF16)

    def a_group(g):
        return g if a_groups else 0

    def b_group(g):
        return g if b_groups else 0

    if slabs:
        out_shape = jax.ShapeDtypeStruct((N_DEV, k_len, tn), BF16)
        out_spec = pl.BlockSpec((None, tk, tn), lambda g, i, j, t: (j, i, 0))
    else:
        out_shape = jax.ShapeDtypeStruct((groups, k_len, n_len), BF16)
        out_spec = pl.BlockSpec((None, tk, tn), lambda g, i, j, t: (g, i, j))
    out = pl.pallas_call(
        body, name=name, grid=(groups, k_len // tk, n_len // tn, nt), out_shape=out_shape,
        in_specs=[pl.BlockSpec((None, tk, tt), lambda g, i, j, t: (a_group(g), i, t)) if lhs_t
                  else pl.BlockSpec((None, tt, tk), lambda g, i, j, t: (a_group(g), t, i)),
                  pl.BlockSpec((None, tt, tn), lambda g, i, j, t: (b_group(g), t, j))],
        out_specs=out_spec,
        scratch_shapes=[pltpu.VMEM((tk, tn), F32)],
        compiler_params=pltpu.CompilerParams(
            dimension_semantics=("parallel", "parallel", "parallel", "arbitrary"), vmem_limit_bytes=VMEM_LIMIT),
    )(a3, b3)
    return out if (slabs or a_groups or b_groups) else out[0]


def _matmul_tn_once(name, lhs_list, rhs, relu2=False, slabs=False, lhs_t=False):
    t_len, n_len = rhs.shape
    tt = min(t_len, 256)
    nt = t_len // tt
    n_lhs = len(lhs_list)
    assert not (lhs_t or slabs) or (n_lhs == 1 and lhs_list[0].ndim == 2)
    k_shapes = [(a.shape[0], n_len) if lhs_t else a.shape[:-2] + (a.shape[-1], n_len) for a in lhs_list]
    tn = n_len // N_DEV

    def body(*refs):
        a_refs, b_ref = refs[:n_lhs], refs[n_lhs]
        o_refs, acc_refs = refs[n_lhs + 1:2 * n_lhs + 1], refs[2 * n_lhs + 1:]
        step = pl.program_id(0)

        @pl.when(step == 0)
        def _():
            for acc in acc_refs:
                acc[...] = jnp.zeros_like(acc)

        bv = b_ref[...].astype(BF16)

        def piece(av):
            if relu2:
                av = jnp.square(jnp.maximum(av.astype(F32), 0.0))
            return (_dot if lhs_t else _dot_tn)(av.astype(BF16), bv)

        for a_ref, acc in zip(a_refs, acc_refs):
            if len(acc.shape) == 3:
                for g in range(acc.shape[0]):
                    acc[g] += piece(a_ref[g])
            else:
                acc[...] += piece(a_ref[...])

        @pl.when(step == nt - 1)
        def _():
            for o_ref, acc in zip(o_refs, acc_refs):
                if slabs:
                    for j in range(N_DEV):
                        o_ref[j] = acc[:, j * tn:(j + 1) * tn].astype(BF16)
                else:
                    o_ref[...] = acc[...].astype(BF16)

    def lhs_spec(a):
        if lhs_t:
            return pl.BlockSpec((a.shape[0], tt), lambda t: (0, t))
        if a.ndim == 3:
            return pl.BlockSpec((a.shape[0], tt, a.shape[2]), lambda t: (0, t, 0))
        return pl.BlockSpec((tt, a.shape[1]), lambda t: (t, 0))

    out_shapes = [(N_DEV, k_shapes[0][0], tn)] if slabs else k_shapes
    return pl.pallas_call(
        body, name=name, grid=(nt,),
        out_shape=[jax.ShapeDtypeStruct(s, BF16) for s in out_shapes],
        in_specs=[lhs_spec(a) for a in lhs_list] + [pl.BlockSpec((tt, n_len), lambda t: (t, 0))],
        out_specs=[_acc_spec(s) for s in out_shapes],
        scratch_shapes=[pltpu.VMEM(s, F32) for s in k_shapes],
        compiler_params=_seq_params(),
    )(*lhs_list, rhs)


def _pad_layout(d):
    names = ("qf", "kf", "vf", "qb", "kb", "vb", "gates", "forget")
    sizes = (N_HEADS * HEAD_SLOT, N_HEADS * HEAD_SLOT, D_BRANCH, D_BRANCH, D_BRANCH, D_BRANCH, 2 * d, LANES)
    out, off = {}, 0
    for n, s in zip(names, sizes):
        out[n] = (off, off + s)
        off += s
    return out, off


def _slot_rows(xt, extra):
    parts = []
    for h in range(N_HEADS):
        parts += [xt[h * HEAD_DIM:(h + 1) * HEAD_DIM, :], extra]
    return jnp.concatenate(parts, axis=0)


def _inproj_fwd(x, g_mix, w_pad, bf_pad, place_q, place_k, ones_q, ones_k, seq):
    t_len, d = x.shape
    lay, _ = _pad_layout(d)
    tiles_per_seq = seq // TM
    slot_w = N_HEADS * HEAD_SLOT

    def body(x_ref, g_ref, w_ref, bf_ref, pq_ref, pk_ref, oq_ref, ok_ref,
             qf_ref, kf_ref, kft_ref, vf_ref, vft_ref, qkvb_ref, kbt_ref, vbt_ref, gl_ref, fpre_ref, h_ref, qft_ref,
             carry_ref):
        @pl.when(pl.program_id(0) % tiles_per_seq == 0)
        def _():
            carry_ref[...] = jnp.zeros_like(carry_ref)

        def proj(name):
            lo, hi = lay[name]
            return _dot_nt(h, w_ref[lo:hi, :])

        xn, _ = _rms(x_ref[...])
        h = (xn * g_ref[...]).astype(BF16)
        fpre = proj("forget") + bf_ref[...]
        fpre_ref[...] = fpre
        logf = -_softplus(-fpre)
        lower = _tri(TM, lambda r, c: c <= r)
        hi, mid, lo = _split3(logf)
        c_val = carry_ref[...] + _dot(lower, hi) + _dot(lower, mid) + _dot(lower, lo)
        carry_ref[...] = carry_ref[...] + jnp.sum(logf, axis=0, keepdims=True)
        c3 = _split3(c_val)
        qf = proj("qf") + sum(_dot(c3[j], pq_ref[j]) for j in range(3)) + oq_ref[...]
        qf_ref[...] = qf.astype(BF16)
        qft_ref[0] = qf.T.astype(BF16)
        kf = proj("kf") - sum(_dot(c3[j], pk_ref[j]) for j in range(3)) + ok_ref[...]
        kf_ref[...] = kf.astype(BF16)
        kft_ref[0] = kf.T.astype(BF16)
        row0 = (lax.broadcasted_iota(jnp.int32, (HEAD_DIM, TM), 0) == 0).astype(F32)
        zeros = jnp.zeros((HEAD_DIM, TM), F32)
        vf = proj("vf")
        vf_ref[...] = vf.astype(BF16)
        vft_ref[0] = _slot_rows(vf.T, row0).astype(BF16)
        qkvb_ref[0] = proj("qb").astype(BF16)
        kb = proj("kb")
        qkvb_ref[1] = kb.astype(BF16)
        kbt_ref[0] = _slot_rows(kb.T, zeros).astype(BF16)
        vb = proj("vb")
        qkvb_ref[2] = vb.astype(BF16)
        vbt_ref[0] = _slot_rows(vb.T, row0).astype(BF16)
        gl_ref[...] = proj("gates").astype(BF16)
        h_ref[...] = h

    n_tiles = t_len // TM
    slot_shape = jax.ShapeDtypeStruct((t_len, slot_w), BF16)
    t_shape = jax.ShapeDtypeStruct((n_tiles, slot_w, TM), BF16)
    t_spec = pl.BlockSpec((1, slot_w, TM), lambda i: (i, 0, 0))
    return pl.pallas_call(
        body, name="inproj_fwd", grid=(n_tiles,),
        out_shape=(slot_shape, slot_shape, t_shape, jax.ShapeDtypeStruct((t_len, D_BRANCH), BF16), t_shape,
                   jax.ShapeDtypeStruct((3, t_len, D_BRANCH), BF16), t_shape, t_shape,
                   jax.ShapeDtypeStruct((t_len, 2 * d), BF16), jax.ShapeDtypeStruct((t_len, LANES), F32),
                   jax.ShapeDtypeStruct((t_len, d), BF16), t_shape),
        in_specs=[_row_spec(TM, d), _const_spec((1, d)), _const_spec(w_pad.shape), _const_spec((1, LANES)),
                  _const_spec(place_q.shape), _const_spec(place_k.shape), _const_spec((1, slot_w)),
                  _const_spec((1, slot_w))],
        out_specs=(_row_spec(TM, slot_w), _row_spec(TM, slot_w), t_spec, _row_spec(TM, D_BRANCH), t_spec,
                   _row3_spec(3, TM, D_BRANCH), t_spec, t_spec, _row_spec(TM, 2 * d), _row_spec(TM, LANES),
                   _row_spec(TM, d), t_spec),
        scratch_shapes=[pltpu.VMEM((1, LANES), F32)],
        compiler_params=_seq_params(),
    )(x, g_mix, w_pad, bf_pad, place_q, place_k, ones_q, ones_k)


def _slot_spec(seq):
    return pl.BlockSpec((seq, NH * HEAD_SLOT), lambda b, g: (b, g))


def _slot2_spec(seq):
    return pl.BlockSpec((2, seq, NH * HEAD_SLOT), lambda b, g: (0, b, g))


def _group_spec(seq):
    return pl.BlockSpec((seq, NH * HEAD_DIM), lambda b, g: (b, g))


def _group3_spec(which, seq):
    return pl.BlockSpec((None, seq, NH * HEAD_DIM), lambda b, g: (which, b, g))


def _tblock_spec(seq):
    return pl.BlockSpec((seq // TK, NH * HEAD_SLOT, TK), lambda b, g: (b, g, 0))


def _qrow_spec(seq):
    return pl.BlockSpec((None, NH, seq // TQ, TQ), lambda b, g: (b, g, 0, 0))


def _attn_params():
    return pltpu.CompilerParams(dimension_semantics=("parallel", "parallel"), vmem_limit_bytes=VMEM_LIMIT)


def _serial_attn_params():
    return pltpu.CompilerParams(dimension_semantics=("arbitrary", "arbitrary"), vmem_limit_bytes=VMEM_LIMIT)


def _hcols(hh):
    return slice(hh * HEAD_DIM, (hh + 1) * HEAD_DIM)


def _hslot(hh):
    return slice(hh * HEAD_SLOT, (hh + 1) * HEAD_SLOT)


def _key_query_mask(rel):
    r = lax.broadcasted_iota(jnp.int32, (TK, TQ), 0)
    c = lax.broadcasted_iota(jnp.int32, (TK, TQ), 1)
    return rel(r, c)


def _heads_cat(vals):
    return jnp.concatenate(vals, axis=1)


def _untranspose(acc_t):
    return acc_t.T[:, :HEAD_DIM]


def _fox_fwd(qf, kf, vft, batch, seq):
    def body(q_ref, k_ref, vt_ref, o_ref, lse_ref, m_s, acc_s):
        causal = _key_query_mask(lambda r, c: r <= c)

        def tile(q0, kj, masked, n_k=1):
            krows = pl.ds(pl.multiple_of(kj * TK, TK), n_k * TK)
            heads = range(NH)
            sts = [_dot_nt(k_ref[krows, _hslot(hh)], q_ref[pl.ds(q0, TQ), _hslot(hh)]) for hh in heads]
            if masked:
                sts = [jnp.where(causal, st, NEG) for st in sts]
            m_olds = [m_s[hh] for hh in heads]
            m_news = [jnp.maximum(m_olds[hh], jnp.max(sts[hh], axis=0, keepdims=True)) for hh in heads]
            pts = [jnp.exp(sts[hh] - m_news[hh]).astype(BF16) for hh in heads]
            pvs = [sum(_dot(vt_ref[kj + i, _hslot(hh), :], pts[hh][i * TK:(i + 1) * TK]) for i in range(n_k))
                   for hh in heads]
            for hh in heads:
                acc_s[hh] = jnp.exp(m_olds[hh] - m_news[hh]) * acc_s[hh] + pvs[hh]
                m_s[hh] = m_news[hh]

        def q_loop(qi, _):
            q0 = pl.multiple_of(qi * TQ, TQ)
            m_s[...] = jnp.full(m_s.shape, NEG, F32)
            acc_s[...] = jnp.zeros_like(acc_s)

            def pair_loop(i, _):
                tile(q0, 2 * i, False, n_k=2)
                return 0

            lax.fori_loop(0, qi // 2, pair_loop, 0)
            pl.when(qi % 2 == 1)(lambda: tile(q0, qi - 1, False))
            tile(q0, qi, True)
            outs = []
            for hh in range(NH):
                total = acc_s[hh, HEAD_DIM:HEAD_DIM + 1, :]
                outs.append(_untranspose(acc_s[hh] / total))
                lse_ref[hh, pl.ds(qi, 1), :] = m_s[hh] + jnp.log(total)
            o_ref[pl.ds(q0, TQ), :] = _heads_cat(outs).astype(BF16)
            return 0

        lax.fori_loop(0, seq // TQ, q_loop, 0)

    return pl.pallas_call(
        body, name="fox_fwd", grid=(batch, N_HEADS // NH),
        out_shape=(jax.ShapeDtypeStruct((batch * seq, D_BRANCH), BF16),
                   jax.ShapeDtypeStruct((batch, N_HEADS, seq // TQ, TQ), F32)),
        in_specs=[_slot_spec(seq), _slot_spec(seq), _tblock_spec(seq)],
        out_specs=(_group_spec(seq), _qrow_spec(seq)),
        scratch_shapes=[pltpu.VMEM((NH, 1, TQ), F32), pltpu.VMEM((NH, HEAD_SLOT, TQ), F32)],
        compiler_params=_attn_params(),
    )(qf, kf, vft)


def _fox_bwd(qf, qft, kf, kft, vf, o, do, dot, lse, batch, seq, partials):
    n_q = seq // TQ
    n = len(partials)

    def body(q_ref, qt_ref, k_ref, kt_ref, v_ref, o_ref, do_ref, dot_ref, lse_ref, *rest):
        p_refs, (dqk_ref, dv_ref, dcq_ref, dck_ref), recv_refs = rest[:n], rest[n:n + 4], rest[n + 4:2 * n + 4]
        delta_s, dqt_acc, dk_s, dv_s = rest[2 * n + 4:2 * n + 8]
        pair_start, pair_finish = _pair_plan(p_refs, recv_refs, *rest[2 * n + 8:])
        first_step, last_step = _first_last_step()
        pl.when(first_step)(pair_start)
        causal = _key_query_mask(lambda r, c: r <= c)
        ones8 = jnp.ones((8, HEAD_DIM), BF16)
        dqt_acc[...] = jnp.zeros_like(dqt_acc)

        def prep(qi, _):
            rows = pl.ds(pl.multiple_of(qi * TQ, TQ), TQ)
            for hh in range(NH):
                hi, lo = _split2(do_ref[rows, _hcols(hh)].astype(F32) * o_ref[rows, _hcols(hh)].astype(F32))
                delta_s[hh, pl.ds(qi, 1), :] = (_dot_nt(ones8, hi) + _dot_nt(ones8, lo))[0:1, :]
            return 0

        lax.fori_loop(0, n_q, prep, 0)

        def tile(qis, kj, masked):
            krows = pl.ds(pl.multiple_of(kj * TK, TK), TK)
            heads = range(NH)
            items = [(t, hh) for t in range(len(qis)) for hh in heads]
            rows = [pl.ds(qi * TQ if isinstance(qi, int) else pl.multiple_of(qi * TQ, TQ), TQ) for qi in qis]
            sts = [_dot_nt(k_ref[krows, _hslot(hh)], q_ref[rows[t], _hslot(hh)]) for t, hh in items]
            dps = [_dot_nt(v_ref[krows, _hcols(hh)], do_ref[rows[t], _hcols(hh)]) for t, hh in items]
            pts = [jnp.exp(sts[i] - lse_ref[hh, pl.ds(qis[t], 1), :]) for i, (t, hh) in enumerate(items)]
            if masked:
                pts = [jnp.where(causal, pt, 0.0) for pt in pts]
            dsts = [(pts[i] * (dps[i] - delta_s[hh, pl.ds(qis[t], 1), :])).astype(BF16)
                    for i, (t, hh) in enumerate(items)]
            for i, (t, hh) in enumerate(items):
                dv_s[hh] += _dot_nt(dot_ref[qis[t], _hslot(hh), :], pts[i].astype(BF16))
                dk_s[hh] += _dot_nt(qt_ref[qis[t], _hslot(hh), :], dsts[i])
                dqt_acc[hh, qis[t]] += _dot(kt_ref[kj, _hslot(hh), :], dsts[i])

        def k_loop(kj, _):
            krows = pl.ds(pl.multiple_of(kj * TK, TK), TK)
            dk_s[...] = jnp.zeros_like(dk_s)
            dv_s[...] = jnp.zeros_like(dv_s)
            tile([kj], kj, True)
            left = n_q - 1 - kj

            def pair_loop(i, _):
                tile([kj + 1 + 2 * i, kj + 2 + 2 * i], kj, False)
                return 0

            lax.fori_loop(0, left // 2, pair_loop, 0)
            pl.when(left % 2 == 1)(lambda: tile([n_q - 1], kj, False))
            dqk_ref[1, krows, :] = _heads_cat([dk_s[hh].T for hh in range(NH)]).astype(BF16)
            dv_ref[krows, :] = _heads_cat([_untranspose(dv_s[hh]) for hh in range(NH)]).astype(BF16)
            for hh in range(NH):
                dck_ref[hh, pl.ds(kj, 1), :] = dk_s[hh, C_ONES_Q:C_ONES_Q + 1, :]
            return 0

        lax.fori_loop(0, seq // TK, k_loop, 0)

        def finish(qi, _):
            rows = pl.ds(pl.multiple_of(qi * TQ, TQ), TQ)
            dqk_ref[0, rows, :] = _heads_cat([dqt_acc[hh, qi].T for hh in range(NH)]).astype(BF16)
            for hh in range(NH):
                dcq_ref[hh, pl.ds(qi, 1), :] = dqt_acc[hh, qi, C_ONES_K:C_ONES_K + 1, :]
            return 0

        lax.fori_loop(0, n_q, finish, 0)
        pl.when(last_step)(pair_finish)

    out = pl.pallas_call(
        body, name="fox_bwd", grid=(batch, N_HEADS // NH),
        out_shape=[jax.ShapeDtypeStruct((2, batch * seq, N_HEADS * HEAD_SLOT), BF16),
                   jax.ShapeDtypeStruct((batch * seq, D_BRANCH), BF16),
                   jax.ShapeDtypeStruct((batch, N_HEADS, seq // TQ, TQ), F32),
                   jax.ShapeDtypeStruct((batch, N_HEADS, seq // TK, TK), F32)] + _pair_shapes(partials),
        in_specs=[_slot_spec(seq), _tblock_spec(seq), _slot_spec(seq), _tblock_spec(seq), _group_spec(seq),
                  _group_spec(seq), _group_spec(seq), _tblock_spec(seq), _qrow_spec(seq)] + _hbm_specs(n),
        out_specs=[_slot2_spec(seq), _group_spec(seq), _qrow_spec(seq), _qrow_spec(seq)] + _hbm_specs(n),
        scratch_shapes=[pltpu.VMEM((NH, n_q, TQ), F32), pltpu.VMEM((NH, n_q, HEAD_SLOT, TQ), F32),
                        pltpu.VMEM((NH, HEAD_SLOT, TK), F32), pltpu.VMEM((NH, HEAD_SLOT, TK), F32)] + _pair_sems(n),
        compiler_params=_serial_attn_params(),
    )(qf, qft, kf, kft, vf, o, do, dot, lse, *partials)
    return out[0], out[1], out[2], out[3], out[4:]


def _first_last_step():
    step = pl.program_id(0) * pl.num_programs(1) + pl.program_id(1)
    return step == 0, step == pl.num_programs(0) * pl.num_programs(1) - 1


def _sb_fwd(qkvb, vbt, batch, seq, shards):
    n = len(shards)

    def body(q_ref, k_ref, vt_ref, *rest):
        x_refs, (o_ref, lt_ref), out_refs = rest[:n], rest[n:n + 2], rest[n + 2:2 * n + 2]
        run_s, acc_s = rest[2 * n + 2:2 * n + 4]
        gather_start, gather_finish = _gather_plan(x_refs, out_refs, *rest[2 * n + 4:])
        first_step, last_step = _first_last_step()
        pl.when(first_step)(gather_start)
        strict = _key_query_mask(lambda r, c: r < c)
        later = _tri(TK, lambda r, c: c > r)

        def tile(q0, kjs, masked):
            heads = range(NH)
            items = [(t, hh) for t in range(len(kjs)) for hh in heads]
            krows = [pl.ds(kj * TK if isinstance(kj, int) else pl.multiple_of(kj * TK, TK), TK) for kj in kjs]
            zts = [_dot_nt(k_ref[krows[t], _hcols(hh)], q_ref[pl.ds(q0, TQ), _hcols(hh)]) for t, hh in items]
            lgs = [-_softplus(zt) for zt in zts]
            if masked:
                lgs = [jnp.where(strict, lg, 0.0) for lg in lgs]
            parts = [_split2(lg) for lg in lgs]
            sufs = [_dot(later, hi) + _dot(later, lo) for hi, lo in parts]
            sums = [jnp.sum(lg, axis=0, keepdims=True) for lg in lgs]
            runs = {}
            for hh in heads:
                run = run_s[hh]
                for t in range(len(kjs)):
                    runs[t, hh] = run
                    run = run + sums[t * NH + hh]
                run_s[hh] = run
            ats = [jnp.exp(zts[i] + lgs[i] + runs[item] + sufs[i]) for i, item in enumerate(items)]
            if masked:
                ats = [jnp.where(strict, at, 0.0) for at in ats]
            for hh in heads:
                acc_s[hh] += sum(_dot(vt_ref[kjs[t], _hslot(hh), :], ats[t * NH + hh].astype(BF16))
                                 for t in range(len(kjs)))

        def q_loop(qi, _):
            q0 = pl.multiple_of(qi * TQ, TQ)
            run_s[...] = jnp.zeros_like(run_s)
            acc_s[...] = jnp.zeros_like(acc_s)
            tile(q0, [qi], True)

            def pair_loop(i, _):
                tile(q0, [qi - 1 - 2 * i, qi - 2 - 2 * i], False)
                return 0

            lax.fori_loop(0, qi // 2, pair_loop, 0)
            pl.when(qi % 2 == 1)(lambda: tile(q0, [0], False))
            o_ref[pl.ds(q0, TQ), :] = _heads_cat([_untranspose(acc_s[hh]) for hh in range(NH)]).astype(BF16)
            for hh in range(NH):
                lt_ref[hh, pl.ds(qi, 1), :] = run_s[hh]
            return 0

        lax.fori_loop(0, seq // TQ, q_loop, 0)
        pl.when(last_step)(gather_finish)

    out = pl.pallas_call(
        body, name="sb_fwd", grid=(batch, N_HEADS // NH),
        out_shape=[jax.ShapeDtypeStruct((batch * seq, D_BRANCH), BF16),
                   jax.ShapeDtypeStruct((batch, N_HEADS, seq // TQ, TQ), F32)] + _gather_shapes(shards),
        in_specs=[_group3_spec(0, seq), _group3_spec(1, seq), _tblock_spec(seq)] + _hbm_specs(n),
        out_specs=[_group_spec(seq), _qrow_spec(seq)] + _hbm_specs(n),
        scratch_shapes=[pltpu.VMEM((NH, 1, TQ), F32), pltpu.VMEM((NH, HEAD_SLOT, TQ), F32)] + _gather_sems(n),
        compiler_params=_serial_attn_params(),
    )(qkvb, qkvb, vbt, *shards)
    return out[0], out[1], out[2:]


def _sb_bwd(qkvb, kbt, do, ltot, batch, seq, chip_sums):
    n = len(chip_sums)

    def body(q_ref, k_ref, v_ref, kt_ref, do_ref, lt_ref, *rest):
        cs_refs, dqkv_ref, out_refs = rest[:n], rest[n], rest[n + 1:2 * n + 1]
        dk_acc, dv_acc, ls_s, gs_s, dqt_s = rest[2 * n + 1:2 * n + 6]
        chips_start, chips_finish = _chips_plan(cs_refs, out_refs, *rest[2 * n + 6:])
        first_step, last_step = _first_last_step()
        pl.when(first_step)(chips_start)
        strict = _key_query_mask(lambda r, c: r < c)
        upto = _tri(TK, lambda r, c: c <= r)
        before = _tri(TK, lambda r, c: c < r)
        dk_acc[...] = jnp.zeros_like(dk_acc)
        dv_acc[...] = jnp.zeros_like(dv_acc)

        def tile(qi, kj, masked):
            rows = pl.ds(pl.multiple_of(qi * TQ, TQ), TQ)
            krows = pl.ds(pl.multiple_of(kj * TK, TK), TK)
            heads = range(NH)
            qs = [q_ref[rows, _hcols(hh)] for hh in heads]
            douts = [do_ref[rows, _hcols(hh)] for hh in heads]
            zts = [_dot_nt(k_ref[krows, _hcols(hh)], qs[hh]) for hh in heads]
            das = [_dot_nt(v_ref[krows, _hcols(hh)], douts[hh]) for hh in heads]
            lgs = [-_softplus(zt) for zt in zts]
            if masked:
                lgs = [jnp.where(strict, lg, 0.0) for lg in lgs]
            parts = [_split2(lg) for lg in lgs]
            prefs = [_dot(upto, hi) + _dot(upto, lo) for hi, lo in parts]
            ats = [jnp.exp(zts[hh] + lgs[hh] + (lt_ref[hh, pl.ds(qi, 1), :] - ls_s[hh]) - prefs[hh]) for hh in heads]
            if masked:
                ats = [jnp.where(strict, at, 0.0) for at in ats]
            gts = [das[hh] * ats[hh] for hh in heads]
            us = [gs_s[hh] + _dot(before, gts[hh].astype(BF16)) for hh in heads]
            dzts = [(jnp.exp(lgs[hh]) * (gts[hh] + us[hh]) - us[hh]).astype(BF16) for hh in heads]
            for hh in heads:
                dk_acc[hh, krows, :] += _dot(dzts[hh], qs[hh])
                dv_acc[hh, krows, :] += _dot(ats[hh].astype(BF16), douts[hh])
                dqt_s[hh] += _dot(kt_ref[kj, _hslot(hh), :], dzts[hh])
                ls_s[hh] += jnp.sum(lgs[hh], axis=0, keepdims=True)
                gs_s[hh] += jnp.sum(gts[hh], axis=0, keepdims=True)

        def q_loop(qi, _):
            ls_s[...] = jnp.zeros_like(ls_s)
            gs_s[...] = jnp.zeros_like(gs_s)
            dqt_s[...] = jnp.zeros_like(dqt_s)

            def k_loop(kj, _):
                tile(qi, kj, False)
                return 0

            lax.fori_loop(0, qi, k_loop, 0)
            tile(qi, qi, True)
            dqkv_ref[0, pl.ds(pl.multiple_of(qi * TQ, TQ), TQ), :] = _heads_cat(
                [_untranspose(dqt_s[hh]) for hh in range(NH)]).astype(BF16)
            return 0

        lax.fori_loop(0, seq // TQ, q_loop, 0)
        dqkv_ref[1] = _heads_cat([dk_acc[hh] for hh in range(NH)]).astype(BF16)
        dqkv_ref[2] = _heads_cat([dv_acc[hh] for hh in range(NH)]).astype(BF16)
        pl.when(last_step)(chips_finish)

    out = pl.pallas_call(
        body, name="sb_bwd", grid=(batch, N_HEADS // NH),
        out_shape=[jax.ShapeDtypeStruct((3, batch * seq, D_BRANCH), BF16)]
        + [jax.ShapeDtypeStruct(s.shape, s.dtype) for s in chip_sums],
        in_specs=[_group3_spec(0, seq), _group3_spec(1, seq), _group3_spec(2, seq), _tblock_spec(seq),
                  _group_spec(seq), _qrow_spec(seq)] + _hbm_specs(n),
        out_specs=[pl.BlockSpec((3, seq, NH * HEAD_DIM), lambda b, g: (0, b, g))] + _hbm_specs(n),
        scratch_shapes=[pltpu.VMEM((NH, seq, HEAD_DIM), F32), pltpu.VMEM((NH, seq, HEAD_DIM), F32),
                        pltpu.VMEM((NH, 1, TQ), F32), pltpu.VMEM((NH, 1, TQ), F32),
                        pltpu.VMEM((NH, HEAD_SLOT, TQ), F32)] + _chips_sems(n),
        compiler_params=_serial_attn_params(),
    )(qkvb, qkvb, qkvb, kbt, do, ltot, *chip_sums)
    return out[0], out[1:]


def _forget_bwd(dcq_tok, dck_tok, fpre, batch, seq):
    t_len = batch * seq
    tiles = seq // TM

    def rev(i):
        return ((i // tiles) * tiles + (tiles - 1 - i % tiles), 0)

    def body(dcq_ref, dck_ref, f_ref, df_ref, db_ref, carry_ref):
        i = pl.program_id(0)

        @pl.when(i == 0)
        def _():
            db_ref[...] = jnp.zeros_like(db_ref)

        @pl.when(i % tiles == 0)
        def _():
            carry_ref[...] = jnp.zeros_like(carry_ref)

        dc = dcq_ref[...] - dck_ref[...]
        upper = _tri(TM, lambda r, c: c >= r)
        hi, mid, lo = _split3(dc)
        dlogf = carry_ref[...] + _dot(upper, hi) + _dot(upper, mid) + _dot(upper, lo)
        carry_ref[...] = carry_ref[...] + jnp.sum(dc, axis=0, keepdims=True)
        df = dlogf * _sigmoid(-f_ref[...])
        df_ref[...] = df.astype(BF16)
        db_ref[...] += jnp.sum(df, axis=0, keepdims=True)

    return pl.pallas_call(
        body, name="forget_bwd", grid=(t_len // TM,),
        out_shape=(jax.ShapeDtypeStruct((t_len, LANES), BF16), jax.ShapeDtypeStruct((1, LANES), F32)),
        in_specs=[pl.BlockSpec((TM, LANES), rev)] * 3,
        out_specs=(pl.BlockSpec((TM, LANES), rev), _acc_spec((1, LANES))),
        scratch_shapes=[pltpu.VMEM((1, LANES), F32)],
        compiler_params=_seq_params(),
    )(dcq_tok, dck_tok, fpre)


def _mix_fwd(o_fox, o_sb, gl, x, w_bf, w_bs, w_out, b_gate):
    t_len, d = x.shape

    def body(of_ref, os_ref, gl_ref, x_ref, wbf_ref, wbs_ref, wo_ref, bg_ref, x1_ref):
        br_f = _dot(of_ref[...], wbf_ref[...])
        br_s = _dot(os_ref[...], wbs_ref[...])
        ga = _sigmoid(gl_ref[:, :d].astype(F32) + bg_ref[0:1, :])
        gb = _sigmoid(gl_ref[:, d:].astype(F32) + bg_ref[1:2, :])
        merged = ga * br_f + gb * br_s
        x1_ref[...] = x_ref[...] + _dot(merged.astype(BF16), wo_ref[...])

    return pl.pallas_call(
        body, name="mix_fwd", grid=(t_len // TM,),
        out_shape=jax.ShapeDtypeStruct((t_len, d), F32),
        in_specs=[_row_spec(TM, D_BRANCH), _row_spec(TM, D_BRANCH), _row_spec(TM, 2 * d), _row_spec(TM, d),
                  _const_spec(w_bf.shape), _const_spec(w_bs.shape), _const_spec(w_out.shape), _const_spec(b_gate.shape)],
        out_specs=_row_spec(TM, d),
        compiler_params=_seq_params(),
    )(o_fox, o_sb, gl, x, w_bf, w_bs, w_out, b_gate)


def _ff_chunk(d_ff):
    return min(d_ff, 1024)


def _mlp_head_fwd_bwd(x1, p, target, g_mlp, w_up, w_down, g_ple, g_final, w_pg, w_ple):
    t_len, d = x1.shape
    d_ple = p.shape[1]
    d_ff = w_up.shape[1]
    ch = _ff_chunk(d_ff)

    def body(x1_ref, p_ref, t_ref, gm_ref, wu_ref, wd_ref, gp_ref, gf_ref, wpg_ref, wple_ref,
             a_ref, dx2_ref, h3_ref, dpre_ref, dpe_ref, loss_ref, dgp_ref, dgf_ref):
        @pl.when(pl.program_id(0) == 0)
        def _():
            loss_ref[...] = jnp.zeros_like(loss_ref)
            dgp_ref[...] = jnp.zeros_like(dgp_ref)
            dgf_ref[...] = jnp.zeros_like(dgf_ref)

        x1v = x1_ref[...]
        x1n, _ = _rms(x1v)
        h2 = (x1n * gm_ref[...]).astype(BF16)
        x2v = x1v
        for j in range(d_ff // ch):
            a = _dot(h2, wu_ref[:, j * ch:(j + 1) * ch])
            a_ref[:, j * ch:(j + 1) * ch] = a.astype(BF16)
            x2v = x2v + _dot(jnp.square(jnp.maximum(a, 0.0)).astype(BF16), wd_ref[j * ch:(j + 1) * ch, :])
        x2n, r3 = _rms(x2v)
        h3 = (x2n * gp_ref[...]).astype(BF16)
        h3_ref[...] = h3
        gate = _sigmoid(_dot(h3, wpg_ref[...]))
        pe = _dot(p_ref[...].astype(BF16), wple_ref[...])
        x3n, r4 = _rms(x2v + gate * pe)
        err = x3n * gf_ref[...] - t_ref[...]
        loss_ref[...] += jnp.full(loss_ref.shape, (0.5 / d) * jnp.sum(err * err), F32)
        dx3, dgf = _rms_bwd(err * (1.0 / d), x3n, r4, gf_ref[...])
        dgf_ref[...] += dgf
        dpe_ref[...] = (dx3 * gate).astype(BF16)
        dpre = (dx3 * pe * gate * (1.0 - gate)).astype(BF16)
        dpre_ref[...] = dpre
        dres, dgp = _rms_bwd(_dot_nt(dpre, wpg_ref[...]), x2n, r3, gp_ref[...])
        dgp_ref[...] += dgp
        dx2_ref[...] = dx3 + dres

    shp_b = jax.ShapeDtypeStruct((t_len, d), BF16)
    return pl.pallas_call(
        body, name="mlp_head_fwd_bwd", grid=(t_len // TM,),
        out_shape=(jax.ShapeDtypeStruct((t_len, d_ff), BF16), jax.ShapeDtypeStruct((t_len, d), F32), shp_b, shp_b, shp_b,
                   jax.ShapeDtypeStruct((1, LANES), F32), jax.ShapeDtypeStruct((1, d), F32),
                   jax.ShapeDtypeStruct((1, d), F32)),
        in_specs=[_row_spec(TM, d), _row_spec(TM, d_ple), _row_spec(TM, d), _const_spec((1, d)),
                  _const_spec(w_up.shape), _const_spec(w_down.shape), _const_spec((1, d)), _const_spec((1, d)),
                  _const_spec(w_pg.shape), _const_spec(w_ple.shape)],
        out_specs=(_row_spec(TM, d_ff), _row_spec(TM, d), _row_spec(TM, d), _row_spec(TM, d), _row_spec(TM, d),
                   _acc_spec((1, LANES)), _acc_spec((1, d)), _acc_spec((1, d))),
        compiler_params=_seq_params(),
    )(x1, p, target, g_mlp, w_up, w_down, g_ple, g_final, w_pg, w_ple)


def _mlp_bwd(dx2, a, x1, g_mlp, w_up, w_down):
    t_len, d = x1.shape
    d_ff = w_up.shape[1]
    ch = _ff_chunk(d_ff)

    def body(dx2_ref, a_ref, x1_ref, g_ref, wu_ref, wd_ref, dx1_ref, da_ref, h2_ref, dg_ref):
        @pl.when(pl.program_id(0) == 0)
        def _():
            dg_ref[...] = jnp.zeros_like(dg_ref)

        dx2v = dx2_ref[...]
        dx2b = dx2v.astype(BF16)
        xn, r = _rms(x1_ref[...])
        h2_ref[...] = (xn * g_ref[...]).T.astype(BF16)
        dh = jnp.zeros((TM, d), F32)
        for j in range(d_ff // ch):
            dact = _dot_nt(dx2b, wd_ref[j * ch:(j + 1) * ch, :])
            da = (dact * 2.0 * jnp.maximum(a_ref[:, j * ch:(j + 1) * ch].astype(F32), 0.0)).astype(BF16)
            da_ref[:, j * ch:(j + 1) * ch] = da
            dh = dh + _dot_nt(da, wu_ref[:, j * ch:(j + 1) * ch])
        dres, dg = _rms_bwd(dh, xn, r, g_ref[...])
        dg_ref[...] += dg
        dx1_ref[...] = dx2v + dres

    return pl.pallas_call(
        body, name="mlp_bwd", grid=(t_len // TM,),
        out_shape=(jax.ShapeDtypeStruct((t_len, d), F32), jax.ShapeDtypeStruct((t_len, d_ff), BF16),
                   jax.ShapeDtypeStruct((d, t_len), BF16), jax.ShapeDtypeStruct((1, d), F32)),
        in_specs=[_row_spec(TM, d), _row_spec(TM, d_ff), _row_spec(TM, d), _const_spec((1, d)),
                  _const_spec(w_up.shape), _const_spec(w_down.shape)],
        out_specs=(_row_spec(TM, d), _row_spec(TM, d_ff), _col_spec(d, TM), _acc_spec((1, d))),
        compiler_params=_seq_params(),
    )(dx2, a, x1, g_mlp, w_up, w_down)


def _mix_bwd(dx1, o_fox, o_sb, gl, w_bf, w_bs, w_out, b_gate):
    t_len, d = dx1.shape

    def body(dx1_ref, of_ref, os_ref, gl_ref, wbf_ref, wbs_ref, wo_ref, bg_ref,
             mg_ref, dbf_ref, dbs_ref, dgl_ref, dof_ref, dos_ref, dbg_ref, doft_ref):
        @pl.when(pl.program_id(0) == 0)
        def _():
            dbg_ref[...] = jnp.zeros_like(dbg_ref)

        dmerged = _dot_nt(dx1_ref[...].astype(BF16), wo_ref[...])
        br_f = _dot(of_ref[...], wbf_ref[...])
        br_s = _dot(os_ref[...], wbs_ref[...])
        ga = _sigmoid(gl_ref[:, :d].astype(F32) + bg_ref[0:1, :])
        gb = _sigmoid(gl_ref[:, d:].astype(F32) + bg_ref[1:2, :])
        mg_ref[...] = (ga * br_f + gb * br_s).astype(BF16)
        dbf = (dmerged * ga).astype(BF16)
        dbs = (dmerged * gb).astype(BF16)
        dbf_ref[...] = dbf
        dbs_ref[...] = dbs
        dla = dmerged * br_f * ga * (1.0 - ga)
        dlb = dmerged * br_s * gb * (1.0 - gb)
        dgl_ref[:, :d] = dla.astype(BF16)
        dgl_ref[:, d:] = dlb.astype(BF16)
        dbg_ref[0:1, :] += jnp.sum(dla, axis=0, keepdims=True)
        dbg_ref[1:2, :] += jnp.sum(dlb, axis=0, keepdims=True)
        dof = _dot_nt(dbf, wbf_ref[...])
        dof_ref[...] = dof.astype(BF16)
        doft_ref[0] = _slot_rows(dof.T, jnp.zeros((HEAD_DIM, TM), F32)).astype(BF16)
        dos_ref[...] = _dot_nt(dbs, wbs_ref[...]).astype(BF16)

    shp_d = jax.ShapeDtypeStruct((t_len, d), BF16)
    shp_h = jax.ShapeDtypeStruct((t_len, D_BRANCH), BF16)
    return pl.pallas_call(
        body, name="mix_bwd", grid=(t_len // TM,),
        out_shape=(shp_d, shp_d, shp_d, jax.ShapeDtypeStruct((t_len, 2 * d), BF16), shp_h, shp_h,
                   jax.ShapeDtypeStruct((2, d), F32),
                   jax.ShapeDtypeStruct((t_len // TM, N_HEADS * HEAD_SLOT, TM), BF16)),
        in_specs=[_row_spec(TM, d), _row_spec(TM, D_BRANCH), _row_spec(TM, D_BRANCH), _row_spec(TM, 2 * d),
                  _const_spec(w_bf.shape), _const_spec(w_bs.shape), _const_spec(w_out.shape), _const_spec(b_gate.shape)],
        out_specs=(_row_spec(TM, d), _row_spec(TM, d), _row_spec(TM, d), _row_spec(TM, 2 * d),
                   _row_spec(TM, D_BRANCH), _row_spec(TM, D_BRANCH), _acc_spec((2, d)),
                   pl.BlockSpec((1, N_HEADS * HEAD_SLOT, TM), lambda i: (i, 0, 0))),
        compiler_params=_seq_params(),
    )(dx1, o_fox, o_sb, gl, w_bf, w_bs, w_out, b_gate)


def _inproj_bwd(dqk_f, dv_f, dqkv_b, dgl, df, dx1, x, g_mix, w_pad, chip_sums):
    t_len, d = x.shape
    lay, _ = _pad_layout(d)
    slot_w = N_HEADS * HEAD_SLOT
    n = len(chip_sums)
    n_tiles = t_len // TM

    def body(dqk_ref, dvf_ref, db_ref, dgl_ref, df_ref, dx1_ref, x_ref, g_ref, w_ref, *rest):
        cs_refs, (dx_ref, dg_ref), out_refs = rest[:n], rest[n:n + 2], rest[n + 2:2 * n + 2]
        chips_start, chips_finish = _chips_plan(cs_refs, out_refs, *rest[2 * n + 2:])

        @pl.when(pl.program_id(0) == 0)
        def _():
            dg_ref[...] = jnp.zeros_like(dg_ref)
            chips_start()

        def back(piece, name):
            lo, hi = lay[name]
            return _dot(piece, w_ref[lo:hi, :])

        xn, r = _rms(x_ref[...])
        dh = (back(df_ref[...], "forget") + back(dgl_ref[...], "gates") + back(dqk_ref[0], "qf")
              + back(dqk_ref[1], "kf") + back(dvf_ref[...], "vf") + back(db_ref[0], "qb") + back(db_ref[1], "kb")
              + back(db_ref[2], "vb"))
        dres, dg = _rms_bwd(dh, xn, r, g_ref[...])
        dg_ref[...] += dg
        dx_ref[...] = dx1_ref[...] + dres
        pl.when(pl.program_id(0) == n_tiles - 1)(chips_finish)

    out = pl.pallas_call(
        body, name="inproj_bwd", grid=(n_tiles,),
        out_shape=[jax.ShapeDtypeStruct((t_len, d), F32), jax.ShapeDtypeStruct((1, d), F32)]
        + [jax.ShapeDtypeStruct(s.shape, s.dtype) for s in chip_sums],
        in_specs=[_row3_spec(2, TM, slot_w), _row_spec(TM, D_BRANCH), _row3_spec(3, TM, D_BRANCH),
                  _row_spec(TM, 2 * d), _row_spec(TM, LANES), _row_spec(TM, d), _row_spec(TM, d), _const_spec((1, d)),
                  _const_spec(w_pad.shape)] + _hbm_specs(n),
        out_specs=[_row_spec(TM, d), _acc_spec((1, d))] + _hbm_specs(n),
        scratch_shapes=_chips_sems(n),
        compiler_params=_seq_params(),
    )(dqk_f, dv_f, dqkv_b, dgl, df, dx1, x, g_mix, w_pad, *chip_sums)
    return out[0], out[1], out[2:]


def _cols_to_slabs(full):
    r, c8 = full.shape
    return full.reshape(r, N_DEV, c8 // N_DEV).transpose(1, 0, 2)


def _slabs_to_cols(slabs):
    n, r, c = slabs.shape
    return slabs.transpose(1, 0, 2).reshape(r, n * c)


def _win_sizes(d):
    return (D_BRANCH, D_BRANCH, D_BRANCH, N_HEADS, D_BRANCH, D_BRANCH, D_BRANCH, d, d)


def _split_win(w_t, d):
    out, off = [], 0
    for s in _win_sizes(d):
        out.append(w_t[off:off + s])
        off += s
    return out


def _to_slots(w_t):
    c = w_t.shape[1]
    return jnp.pad(w_t.reshape(N_HEADS, HEAD_DIM, c), ((0, 0), (0, HEAD_SLOT - HEAD_DIM), (0, 0))).reshape(-1, c)


def _from_slots(w_t):
    c = w_t.shape[1]
    return w_t.reshape(N_HEADS, HEAD_SLOT, c)[:, :HEAD_DIM].reshape(N_HEADS * HEAD_DIM, c)


def _pad_win(w_full_t, d):
    qa, ka, va, fa, qb, kb, vb, ga, gb = _split_win(w_full_t, d)
    scale = HEAD_DIM ** -0.5
    fpad = jnp.pad(fa, ((0, LANES - N_HEADS), (0, 0)))
    return jnp.concatenate([_to_slots(qa * scale), _to_slots(ka), va, qb * scale, kb, vb, ga, gb, fpad], axis=0)


def _unpad_dwin(dqk_f, dv_f, dqkv_b, dgates, dforget, d):
    scale = HEAD_DIM ** -0.5
    return jnp.concatenate([_from_slots(dqk_f[0]) * scale, _from_slots(dqk_f[1]), dv_f, dforget[:N_HEADS],
                            dqkv_b[0] * scale, dqkv_b[1], dqkv_b[2], dgates], axis=0)


def _c_lane_constants():
    head = jnp.arange(LANES)[:, None]
    lane = jnp.arange(N_HEADS * HEAD_SLOT)[None, :]
    in_head = (lane // HEAD_SLOT == head) & (head < N_HEADS)

    def place(first):
        return jnp.stack([(in_head & (lane % HEAD_SLOT == first + j)) for j in range(3)]).astype(BF16)

    def ones(first):
        off = lane % HEAD_SLOT
        return ((off >= first) & (off < first + 3)).astype(F32)

    return place(C_TERMS_Q), place(C_TERMS_K), ones(C_ONES_Q), ones(C_ONES_K)


def _pad_rows(a, rows):
    return jnp.pad(a, [(0, 0)] * (a.ndim - 2) + [(0, rows - a.shape[-2]), (0, 0)])


def kernel(x, p, g_mix, w_in, b_forget, b_gate, w_branch_fox, w_branch_sb, w_out, g_mlp, w_up, w_down, g_ple, w_ple_gate, w_ple, g_final, loss_target, m_g_mix, m_w_in, m_b_forget, m_b_gate, m_w_branch_fox, m_w_branch_sb, m_w_out, m_g_mlp, m_w_up, m_w_down, m_g_ple, m_w_ple_gate, m_w_ple, m_g_final, v_g_mix, v_w_in, v_b_forget, v_b_gate, v_w_branch_fox, v_w_branch_sb, v_w_out, v_g_mlp, v_w_up, v_w_down, v_g_ple, v_w_ple_gate, v_w_ple, v_g_final):
    batch, seq, d = x.shape
    t_len = batch * seq
    d_ple = p.shape[-1]
    d_ff = w_up.shape[-1] * N_DEV
    dn = d // N_DEV
    fn = d_ff // N_DEV
    my_c = lax.axis_index("c")
    my_dev = 4 * lax.axis_index("x") + 2 * lax.axis_index("y") + my_c

    bg_hi = b_gate[0].astype(BF16)
    bg_r = b_gate[0] - bg_hi.astype(F32)
    bg_mid = bg_r.astype(BF16)
    bg_lo = (bg_r - bg_mid.astype(F32)).astype(BF16)
    narrow_rows = 2 * D_BRANCH + d_ple + 6
    narrow_rows_pad = -(-narrow_rows // 16) * 16
    narrow = _pad_rows(jnp.concatenate(
        [w_branch_fox[0].astype(BF16), w_branch_sb[0].astype(BF16), w_ple[0].astype(BF16), bg_hi, bg_mid, bg_lo],
        axis=0), narrow_rows_pad)
    g_in, = _all_gather([w_in[0].T.astype(BF16)])
    w_pad = _pad_win(g_in.reshape(-1, d), d)
    bf_pad = jnp.pad(b_forget, ((0, 0), (0, LANES - N_HEADS)))
    place_q, place_k, ones_q, ones_k = _c_lane_constants()

    x2d = x.reshape(t_len, d)
    p2d = p.reshape(t_len, d_ple)
    tgt2d = loss_target.reshape(t_len, d)
    qf, kf, kft, vf, vft, qkvb, kbt, vbt, gl, fpre, h1, qft = _inproj_fwd(
        x2d, g_mix, w_pad, bf_pad, place_q, place_k, ones_q, ones_k, seq)
    o_sb, ltot, (g_up, g_out, g_down, g_pg, g_narrow) = _sb_fwd(qkvb, vbt, batch, seq, [
        w_up[0].astype(BF16), w_out[0].astype(BF16), w_down[0].astype(BF16), w_ple_gate[0].astype(BF16), narrow])
    o_fox, lse = _fox_fwd(qf, kf, vft, batch, seq)
    w_up_full = _slabs_to_cols(g_up)
    w_out_full = g_out.reshape(d, d)
    w_down_full = g_down.reshape(d_ff, d)
    w_pg_full = g_pg.reshape(d, d)
    w_bf_full = _slabs_to_cols(g_narrow[:, :D_BRANCH])
    w_bs_full = _slabs_to_cols(g_narrow[:, D_BRANCH:2 * D_BRANCH])
    w_ple_full = _slabs_to_cols(g_narrow[:, 2 * D_BRANCH:2 * D_BRANCH + d_ple])
    bg_terms = g_narrow[:, 2 * D_BRANCH + d_ple:narrow_rows].astype(F32)
    b_gate_full = _slabs_to_cols(bg_terms[:, 0:2] + bg_terms[:, 2:4] + bg_terms[:, 4:6])
    x1 = _mix_fwd(o_fox, o_sb, gl, x2d, w_bf_full, w_bs_full, w_out_full, b_gate_full)

    a_up, dx2, h3, dpre, dpe, loss_acc, dg_ple, dg_final = _mlp_head_fwd_bwd(
        x1, p2d, tgt2d, g_mlp, w_up_full, w_down_full, g_ple, g_final.reshape(1, d), w_pg_full, w_ple_full)
    dx1, da_up, h2t, dg_mlp = _mlp_bwd(dx2, a_up, x1, g_mlp, w_up_full, w_down_full)
    merged, dbr_f, dbr_s, dgl, do_fox, do_sb, dbg, do_fox_t = _mix_bwd(
        dx1, o_fox, o_sb, gl, w_bf_full, w_bs_full, w_out_full, b_gate_full)

    def column_shards(name, lhs, rhs, lhs_t=False):
        if (rhs.shape[-1] // N_DEV) % (4 * LANES) == 0:
            return _matmul_tn(name, lhs, rhs, slabs=True, lhs_t=lhs_t)
        return _cols_to_slabs(_matmul_tn(name, lhs, rhs, lhs_t=lhs_t))

    if fn % (4 * LANES) == 0:
        part_up, = _matmul_tn_once("dw_up", [h2t], da_up, slabs=True, lhs_t=True)
    else:
        part_up = column_shards("dw_up", h2t, da_up, lhs_t=True)
    part_out = _matmul_tn("dw_out", merged, dx1).reshape(N_DEV, dn, d)
    part_down = _matmul_tn_once("dw_down", [a_up], dx2, relu2=True)[0].reshape(N_DEV, fn, d)
    part_pg = _matmul_tn("dw_ple_gate", h3, dpre).reshape(N_DEV, dn, d)
    part_narrow = _pad_rows(jnp.concatenate(
        [column_shards("dw_branch_fox", o_fox, dbr_f), column_shards("dw_branch_sb", o_sb, dbr_s),
         column_shards("dw_ple", p2d, dpe)], axis=1), narrow_rows_pad)
    early = [part_up, part_out, part_down, part_pg, lax.optimization_barrier(part_narrow)]

    dqk_f, dv_f, dc_queries, dc_keys, early_recv = _fox_bwd(
        qf, qft, kf, kft, vf, o_fox, do_fox, do_fox_t, lse, batch, seq, early)
    early_sums = [_pair_add("pair_add_%d" % i, pt, rc, my_c) for i, (pt, rc) in enumerate(zip(early, early_recv))]
    dqkv_b, (s_up, s_out, s_down, s_pg, s_narrow) = _sb_bwd(qkvb, kbt, do_sb, ltot, batch, seq, early_sums)
    dcq_tok = dc_queries.reshape(batch, N_HEADS, seq).transpose(0, 2, 1).reshape(t_len, N_HEADS)
    dck_tok = dc_keys.reshape(batch, N_HEADS, seq).transpose(0, 2, 1).reshape(t_len, N_HEADS)
    lane_pad = ((0, 0), (0, LANES - N_HEADS))
    df, db_forget = _forget_bwd(jnp.pad(dcq_tok, lane_pad), jnp.pad(dck_tok, lane_pad), fpre, batch, seq)

    gw_in = _unpad_dwin(*_matmul_tn_once("dw_in_fox_qk", [dqk_f], h1),
                        *_matmul_tn_once("dw_in_rest", [dv_f, dqkv_b, dgl, df], h1), d)
    part_in = lax.optimization_barrier(gw_in.reshape(N_DEV, -1, d))
    recv_in, = _rs_core_pair("reduce_scatter_core_pair_w_in", [part_in])
    grad_x, dg_mix, (s_in,) = _inproj_bwd(dqk_f, dv_f, dqkv_b, dgl, df, dx1, x2d, g_mix, w_pad,
                                          [_pair_add("pair_add_w_in", part_in, recv_in, my_c)])

    small = jnp.concatenate([
        dg_mix, dg_mlp, dg_ple, dg_final, jnp.pad(db_forget[:, :N_HEADS], ((0, 0), (0, d - N_HEADS))), dbg,
        jnp.pad(loss_acc[:, :1], ((0, 0), (0, d - 1)))], axis=0)
    small = _all_reduce_small(small)
    loss = small[7, 0]
    small_grads = {
        "g_mix": small[0:1], "g_mlp": small[1:2], "g_ple": small[2:3], "g_final": small[3:4],
        "b_forget": small[4:5, :N_HEADS],
        "b_gate": lax.dynamic_slice_in_dim(small[5:7], my_dev * dn, dn, axis=1),
    }

    weights = {"g_mix": g_mix, "w_in": w_in, "b_forget": b_forget, "b_gate": b_gate, "w_branch_fox": w_branch_fox,
               "w_branch_sb": w_branch_sb, "w_out": w_out, "g_mlp": g_mlp, "w_up": w_up, "w_down": w_down,
               "g_ple": g_ple, "w_ple_gate": w_ple_gate, "w_ple": w_ple, "g_final": g_final}
    m_in = {"g_mix": m_g_mix, "w_in": m_w_in, "b_forget": m_b_forget, "b_gate": m_b_gate,
            "w_branch_fox": m_w_branch_fox, "w_branch_sb": m_w_branch_sb, "w_out": m_w_out, "g_mlp": m_g_mlp,
            "w_up": m_w_up, "w_down": m_w_down, "g_ple": m_g_ple, "w_ple_gate": m_w_ple_gate, "w_ple": m_w_ple,
            "g_final": m_g_final}
    v_in = {"g_mix": v_g_mix, "w_in": v_w_in, "b_forget": v_b_forget, "b_gate": v_b_gate,
            "w_branch_fox": v_w_branch_fox, "w_branch_sb": v_w_branch_sb, "w_out": v_w_out, "g_mlp": v_g_mlp,
            "w_up": v_w_up, "w_down": v_w_down, "g_ple": v_g_ple, "w_ple_gate": v_w_ple_gate, "w_ple": v_w_ple,
            "g_final": v_g_final}
    names = list(weights)

    def as2d(a):
        return a.reshape(-1, a.shape[-1])

    result = {}
    big = {"w_up": (s_up, 0), "w_out": (s_out, 0), "w_down": (s_down, 0), "w_ple_gate": (s_pg, 0),
           "w_branch_fox": (s_narrow, 0), "w_branch_sb": (s_narrow, D_BRANCH), "w_ple": (s_narrow, 2 * D_BRANCH)}
    for n, (parts, off) in big.items():
        result[n] = _adamw_parts("adamw_" + n, as2d(weights[n]), parts, off, as2d(m_in[n]), as2d(v_in[n]))
    result["w_in"] = tuple(r.T for r in _adamw_parts("adamw_w_in", w_in[0].T, s_in, 0, m_w_in[0].T, v_w_in[0].T))
    small_names = list(small_grads)
    small_out = _adamw_small([(as2d(weights[n]), small_grads[n], as2d(m_in[n]), as2d(v_in[n])) for n in small_names])
    for n, (dlt, nm, nv) in zip(small_names, small_out):
        result[n] = (small_grads[n], dlt, nm, nv)
    outs = [[result[n][k].reshape(weights[n].shape) for n in names] for k in range(4)]
    return (loss, grad_x.reshape(x.shape), *outs[0], *outs[1], *outs[2], *outs[3])
```

```python
import jax
import jax.numpy as jnp
from jax import lax
from jax.experimental import pallas as pl
from jax.experimental.pallas import tpu as pltpu

F32 = jnp.float32
BF16 = jnp.bfloat16

HEAD_DIM = 64
N_HEADS = 8
D_BRANCH = N_HEADS * HEAD_DIM
EPS = 1e-6
ADAM_LR = 0.001
ADAM_B1 = 0.9
ADAM_B2 = 0.999
ADAM_EPS = 1e-08
ADAM_WD = 0.01
ADAM_STEP = 10

N_DEV = 8
LANES = 128
TM = 256
TQ = 256
TK = 256
NH = 4
HEAD_SLOT = 128
C_TERMS_Q = 64
C_ONES_K = 64
C_TERMS_K = 67
C_ONES_Q = 67
NEG = -1e30
VMEM_LIMIT = 56 * 1024 * 1024
MESH = pl.DeviceIdType.MESH


def _dot(a, b):
    return jnp.dot(a, b, preferred_element_type=F32)


def _dot_nt(a, b):
    return lax.dot_general(a, b, (((1,), (1,)), ((), ())), preferred_element_type=F32)


def _dot_tn(a, b):
    return lax.dot_general(a, b, (((0,), (0,)), ((), ())), preferred_element_type=F32)


def _sigmoid(x):
    return 1.0 / (1.0 + jnp.exp(-x))


def _softplus(x):
    return jnp.maximum(x, 0.0) + jnp.log(1.0 + jnp.exp(-jnp.abs(x)))


def _split2(x):
    hi = x.astype(BF16)
    lo = (x - hi.astype(F32)).astype(BF16)
    return hi, lo


def _split3(x):
    hi = x.astype(BF16)
    r = x - hi.astype(F32)
    mid = r.astype(BF16)
    lo = (r - mid.astype(F32)).astype(BF16)
    return hi, mid, lo


def _tri(n, rel):
    r = lax.broadcasted_iota(jnp.int32, (n, n), 0)
    c = lax.broadcasted_iota(jnp.int32, (n, n), 1)
    return rel(r, c).astype(BF16)


def _rms(x):
    r = lax.rsqrt(jnp.mean(x * x, axis=-1, keepdims=True) + EPS)
    return x * r, r


def _rms_bwd(dh, xn, r, g):
    dxn = dh * g
    dx = r * (dxn - xn * jnp.mean(dxn * xn, axis=-1, keepdims=True))
    return dx, jnp.sum(dh * xn, axis=0, keepdims=True)


def _row_spec(tm, cols):
    return pl.BlockSpec((tm, cols), lambda i: (i, 0))


def _row3_spec(g, tm, cols):
    return pl.BlockSpec((g, tm, cols), lambda i: (0, i, 0))


def _col_spec(rows, tm):
    return pl.BlockSpec((rows, tm), lambda i: (0, i))


def _const_spec(shape):
    nd = len(shape)
    return pl.BlockSpec(shape, lambda i: (0,) * nd, pipeline_mode=pl.Buffered(1))


def _acc_spec(shape):
    nd = len(shape)
    return pl.BlockSpec(shape, lambda i: (0,) * nd)


def _seq_params():
    return pltpu.CompilerParams(dimension_semantics=("arbitrary",), vmem_limit_bytes=VMEM_LIMIT)


def _mesh_pos():
    return lax.axis_index("x"), lax.axis_index("y"), lax.axis_index("c")


def _other_chips(x, y):
    return [(1 - x, y), (x, 1 - y), (1 - x, 1 - y)]


def _hbm_specs(n):
    return [pl.BlockSpec(memory_space=pl.ANY)] * n


def _gather_plan(x_refs, out_refs, send_sems, recv_sems, local_sems):
    n = len(x_refs)
    x, y, c = _mesh_pos()
    me, sibling = (x, y, c), (x, y, 1 - c)
    chips = _other_chips(x, y)

    def index(px, py, pc):
        return 4 * px + 2 * py + pc

    def copy(a, k, block, to, src=None):
        slab = out_refs[a].at[index(*block)]
        return pltpu.make_async_remote_copy(
            src_ref=slab if src is None else src, dst_ref=slab,
            send_sem=send_sems.at[7 * a + k], recv_sem=recv_sems.at[7 * a + k], device_id=to, device_id_type=MESH)

    mine = [pltpu.make_async_copy(x_refs[a], out_refs[a].at[index(*me)], local_sems.at[a]) for a in range(n)]
    first = []
    for a in range(n):
        first.append(copy(a, 0, me, sibling, src=x_refs[a]))
        first += [copy(a, 1 + j, me, (cx, cy, c), src=x_refs[a]) for j, (cx, cy) in enumerate(chips)]

    def start():
        for cp in mine + first:
            cp.start()

    def finish():
        passed = []
        for j, (cx, cy) in enumerate(chips):
            for a in range(n):
                copy(a, 1 + j, (cx, cy, c), me).wait_recv()
                passed.append(copy(a, 4 + j, (cx, cy, c), sibling))
                passed[-1].start()
        for a in range(n):
            copy(a, 0, sibling, me).wait_recv()
            for j, (cx, cy) in enumerate(chips):
                copy(a, 4 + j, (cx, cy, 1 - c), me).wait_recv()
        for cp in first + passed:
            cp.wait_send()
        for cp in mine:
            cp.wait()

    return start, finish


def _gather_shapes(shards):
    return [jax.ShapeDtypeStruct((N_DEV,) + s.shape, s.dtype) for s in shards]


def _gather_sems(n):
    return [pltpu.SemaphoreType.DMA((7 * n,)), pltpu.SemaphoreType.DMA((7 * n,)), pltpu.SemaphoreType.DMA((n,))]


def _all_gather(shards):
    n = len(shards)

    def body(*refs):
        start, finish = _gather_plan(refs[:n], refs[n:2 * n], *refs[2 * n:])
        start()
        finish()

    return pl.pallas_call(
        body, name="all_gather_weights", out_shape=_gather_shapes(shards),
        in_specs=_hbm_specs(n), out_specs=_hbm_specs(n), scratch_shapes=_gather_sems(n),
    )(*shards)


def _pair_plan(p_refs, recv_refs, send_sems, recv_sems):
    n = len(p_refs)
    x, y, c = _mesh_pos()
    sibling = (x, y, 1 - c)

    def start():
        for a in range(n):
            for chip in range(4):
                pltpu.make_async_remote_copy(
                    src_ref=p_refs[a].at[2 * chip + (1 - c)], dst_ref=recv_refs[a].at[chip],
                    send_sem=send_sems.at[a], recv_sem=recv_sems.at[a], device_id=sibling, device_id_type=MESH).start()

    def finish():
        for a in range(n):
            pltpu.make_async_remote_copy(
                src_ref=recv_refs[a], dst_ref=recv_refs[a], send_sem=send_sems.at[a], recv_sem=recv_sems.at[a],
                device_id=sibling, device_id_type=MESH).wait()

    return start, finish


def _pair_shapes(partials):
    return [jax.ShapeDtypeStruct((4,) + s.shape[1:], s.dtype) for s in partials]


def _pair_sems(n):
    return [pltpu.SemaphoreType.DMA((n,)), pltpu.SemaphoreType.DMA((n,))]


def _rs_core_pair(name, partials):
    n = len(partials)

    def body(*refs):
        start, finish = _pair_plan(refs[:n], refs[n:2 * n], *refs[2 * n:])
        start()
        finish()

    return pl.pallas_call(
        body, name=name, out_shape=_pair_shapes(partials),
        in_specs=_hbm_specs(n), out_specs=_hbm_specs(n), scratch_shapes=_pair_sems(n),
    )(*partials)


def _chips_plan(cs_refs, out_refs, send_sems, recv_sems, local_sems):
    n = len(cs_refs)
    x, y, c = _mesh_pos()
    chip = 2 * x + y
    chips = _other_chips(x, y)
    mine = [pltpu.make_async_copy(cs_refs[a].at[chip], out_refs[a].at[chip], local_sems.at[a]) for a in range(n)]
    sends = [pltpu.make_async_remote_copy(
        src_ref=cs_refs[a].at[2 * cx + cy], dst_ref=out_refs[a].at[chip],
        send_sem=send_sems.at[3 * a + j], recv_sem=recv_sems.at[3 * a + j],
        device_id=(cx, cy, c), device_id_type=MESH) for a in range(n) for j, (cx, cy) in enumerate(chips)]

    def start():
        for cp in mine + sends:
            cp.start()

    def finish():
        for a in range(n):
            for j, (cx, cy) in enumerate(chips):
                pltpu.make_async_remote_copy(
                    src_ref=cs_refs[a].at[chip], dst_ref=out_refs[a].at[2 * cx + cy],
                    send_sem=send_sems.at[3 * a + j], recv_sem=recv_sems.at[3 * a + j],
                    device_id=(x, y, c), device_id_type=MESH).wait_recv()
        for cp in sends:
            cp.wait_send()
        for cp in mine:
            cp.wait()

    return start, finish


def _chips_sems(n):
    return [pltpu.SemaphoreType.DMA((3 * n,)), pltpu.SemaphoreType.DMA((3 * n,)), pltpu.SemaphoreType.DMA((n,))]


def _all_reduce_small(vec):
    rows, cols = vec.shape

    def body(x_ref, land_ref, sum_ref, send_sems, recv_sems):
        x, y, c = _mesh_pos()
        me = 4 * x + 2 * y + c
        land_ref[me] = x_ref[...]
        flips = [(fx, fy, fc) for fx in (0, 1) for fy in (0, 1) for fc in (0, 1)][1:]

        def flipped(f):
            return tuple((1 - v) if b else v for v, b in zip((x, y, c), f))

        sends = []
        for k, f in enumerate(flips):
            sends.append(pltpu.make_async_remote_copy(
                src_ref=x_ref, dst_ref=land_ref.at[me], send_sem=send_sems.at[k], recv_sem=recv_sems.at[k],
                device_id=flipped(f), device_id_type=MESH))
            sends[-1].start()
        for k, f in enumerate(flips):
            px, py, pc = flipped(f)
            pltpu.make_async_remote_copy(
                src_ref=x_ref, dst_ref=land_ref.at[4 * px + 2 * py + pc], send_sem=send_sems.at[k],
                recv_sem=recv_sems.at[k], device_id=(x, y, c), device_id_type=MESH).wait_recv()
        for cp in sends:
            cp.wait_send()
        total = land_ref[0]
        for d in range(1, N_DEV):
            total = total + land_ref[d]
        sum_ref[...] = total

    vm = pl.BlockSpec(memory_space=pltpu.VMEM)
    return pl.pallas_call(
        body, name="all_reduce_small",
        out_shape=(jax.ShapeDtypeStruct((N_DEV, rows, cols), F32), jax.ShapeDtypeStruct((rows, cols), F32)),
        in_specs=[vm], out_specs=(vm, vm),
        scratch_shapes=[pltpu.SemaphoreType.DMA((7,)), pltpu.SemaphoreType.DMA((7,))],
    )(vec)[1]


def _block_rows(rows, cols, itemsize, align, row_off=0):
    best = None
    for t in range(align, rows + 1, align):
        if rows % t == 0 and row_off % t == 0 and t * cols * itemsize <= (1 << 20):
            best = t
    return rows if best is None else best


def _pair_add(name, partial, recv, my_c):
    _, rows, cols = partial.shape
    br = _block_rows(rows, cols, 2, 16)

    def body(c_ref, a_ref, b_ref, o_ref):
        o_ref[...] = (a_ref[...].astype(F32) + b_ref[...].astype(F32)).astype(BF16)

    return pl.pallas_call(
        body, name=name,
        grid_spec=pltpu.PrefetchScalarGridSpec(
            num_scalar_prefetch=1, grid=(4, rows // br),
            in_specs=[pl.BlockSpec((None, None, br, cols), lambda j, i, c_ref: (j, c_ref[0], i, 0)),
                      pl.BlockSpec((None, br, cols), lambda j, i, c_ref: (j, i, 0))],
            out_specs=pl.BlockSpec((None, br, cols), lambda j, i, c_ref: (j, i, 0))),
        out_shape=jax.ShapeDtypeStruct((4, rows, cols), BF16),
    )(my_c.reshape(1).astype(jnp.int32), partial.reshape(4, 2, rows, cols), recv)


def _adam_update(w, g, m, v):
    nm = ADAM_B1 * m + (1.0 - ADAM_B1) * g
    nv = ADAM_B2 * v + (1.0 - ADAM_B2) * (g * g)
    m_hat = nm / (1.0 - ADAM_B1 ** ADAM_STEP)
    v_hat = nv / (1.0 - ADAM_B2 ** ADAM_STEP)
    return -ADAM_LR * (m_hat / (jnp.sqrt(v_hat) + ADAM_EPS) + ADAM_WD * w), nm, nv


def _adamw_parts(name, w, parts, row_off, m, v):
    rows, cols = w.shape
    tr = _block_rows(rows, cols, 4, 16, row_off)
    tc = cols
    if tr == rows and rows % 16 != 0 and cols % (2 * LANES) == 0:
        tc = 2 * LANES
    assert rows % tr == 0 and row_off % tr == 0 and (tc == cols or row_off == 0)
    off = row_off // tr

    def body(w_ref, p_ref, m_ref, v_ref, g_ref, d_ref, nm_ref, nv_ref):
        g = p_ref[0].astype(F32)
        for j in range(1, 4):
            g = g + p_ref[j].astype(F32)
        g_ref[...] = g
        d_ref[...], nm_ref[...], nv_ref[...] = _adam_update(w_ref[...], g, m_ref[...], v_ref[...])

    spec = pl.BlockSpec((tr, tc), lambda i, j: (i, j))
    shp = jax.ShapeDtypeStruct((rows, cols), F32)
    return pl.pallas_call(
        body, name=name, grid=(rows // tr, cols // tc), out_shape=(shp,) * 4,
        in_specs=[spec, pl.BlockSpec((4, tr, tc), lambda i, j: (0, off + i, j)), spec, spec], out_specs=(spec,) * 4,
    )(w, parts, m, v)


def _adamw_small(tensors):
    n = len(tensors)

    def body(*refs):
        ins, outs = refs[:4 * n], refs[4 * n:]
        for t in range(n):
            w_ref, g_ref, m_ref, v_ref = ins[4 * t:4 * t + 4]
            d, nm, nv = _adam_update(w_ref[...], g_ref[...], m_ref[...], v_ref[...])
            outs[3 * t][...], outs[3 * t + 1][...], outs[3 * t + 2][...] = d, nm, nv

    vm = pl.BlockSpec(memory_space=pltpu.VMEM)
    out = pl.pallas_call(
        body, name="adamw_small",
        out_shape=[jax.ShapeDtypeStruct(t[0].shape, F32) for t in tensors for _ in range(3)],
        in_specs=[vm] * (4 * n), out_specs=[vm] * (3 * n),
    )(*[a for t in tensors for a in t])
    return [tuple(out[3 * t:3 * t + 3]) for t in range(n)]


def _matmul_tn(name, a, b, relu2=False, slabs=False, lhs_t=False):
    a_groups = a.shape[0] if a.ndim == 3 else 0
    b_groups = b.shape[0] if b.ndim == 3 else 0
    groups = max(a_groups, b_groups, 1)
    assert not (a_groups and b_groups) and not (a_groups and lhs_t)
    a3 = a if a_groups else a[None]
    b3 = b if b_groups else b[None]
    t_len, k_len = a3.shape[1:][::-1] if lhs_t else a3.shape[1:]
    n_len = b3.shape[2]
    tt = min(t_len, 512)
    tk = min(k_len, 1024)
    tn = n_len // N_DEV if slabs else min(n_len, 1024)
    nt = t_len // tt
    assert not slabs or (groups == 1 and tn <= 1024)

    def body(a_ref, b_ref, o_ref, acc_ref):
        @pl.when(pl.program_id(3) == 0)
        def _():
            acc_ref[...] = jnp.zeros_like(acc_ref)

        av = a_ref[...]
        if relu2:
            av = jnp.square(jnp.maximum(av.astype(F32), 0.0))
        product = _dot if lhs_t else _dot_tn
        acc_ref[...] += product(av.astype(BF16), b_ref[...].astype(BF16))

        @pl.when(pl.program_id(3) == nt - 1)
        def _():
            o_ref[...] = acc_ref[...].astype(BF16)

    def a_group(g):
        return g if a_groups else 0

    def b_group(g):
        return g if b_groups else 0

    if slabs:
        out_shape = jax.ShapeDtypeStruct((N_DEV, k_len, tn), BF16)
        out_spec = pl.BlockSpec((None, tk, tn), lambda g, i, j, t: (j, i, 0))
    else:
        out_shape = jax.ShapeDtypeStruct((groups, k_len, n_len), BF16)
        out_spec = pl.BlockSpec((None, tk, tn), lambda g, i, j, t: (g, i, j))
    out = pl.pallas_call(
        body, name=name, grid=(groups, k_len // tk, n_len // tn, nt), out_shape=out_shape,
        in_specs=[pl.BlockSpec((None, tk, tt), lambda g, i, j, t: (a_group(g), i, t)) if lhs_t
                  else pl.BlockSpec((None, tt, tk), lambda g, i, j, t: (a_group(g), t, i)),
                  pl.BlockSpec((None, tt, tn), lambda g, i, j, t: (b_group(g), t, j))],
        out_specs=out_spec,
        scratch_shapes=[pltpu.VMEM((tk, tn), F32)],
        compiler_params=pltpu.CompilerParams(
            dimension_semantics=("parallel", "parallel", "parallel", "arbitrary"), vmem_limit_bytes=VMEM_LIMIT),
    )(a3, b3)
    return out if (slabs or a_groups or b_groups) else out[0]


def _matmul_tn_once(name, lhs_list, rhs, relu2=False, slabs=False, lhs_t=False):
    t_len, n_len = rhs.shape
    tt = min(t_len, 256)
    nt = t_len // tt
    n_lhs = len(lhs_list)
    assert not (lhs_t or slabs) or (n_lhs == 1 and lhs_list[0].ndim == 2)
    k_shapes = [(a.shape[0], n_len) if lhs_t else a.shape[:-2] + (a.shape[-1], n_len) for a in lhs_list]
    tn = n_len // N_DEV

    def body(*refs):
        a_refs, b_ref = refs[:n_lhs], refs[n_lhs]
        o_refs, acc_refs = refs[n_lhs + 1:2 * n_lhs + 1], refs[2 * n_lhs + 1:]
        step = pl.program_id(0)

        @pl.when(step == 0)
        def _():
            for acc in acc_refs:
                acc[...] = jnp.zeros_like(acc)

        bv = b_ref[...].astype(BF16)

        def piece(av):
            if relu2:
                av = jnp.square(jnp.maximum(av.astype(F32), 0.0))
            return (_dot if lhs_t else _dot_tn)(av.astype(BF16), bv)

        for a_ref, acc in zip(a_refs, acc_refs):
            if len(acc.shape) == 3:
                for g in range(acc.shape[0]):
                    acc[g] += piece(a_ref[g])
            else:
                acc[...] += piece(a_ref[...])

        @pl.when(step == nt - 1)
        def _():
            for o_ref, acc in zip(o_refs, acc_refs):
                if slabs:
                    for j in range(N_DEV):
                        o_ref[j] = acc[:, j * tn:(j + 1) * tn].astype(BF16)
                else:
                    o_ref[...] = acc[...].astype(BF16)

    def lhs_spec(a):
        if lhs_t:
            return pl.BlockSpec((a.shape[0], tt), lambda t: (0, t))
        if a.ndim == 3:
            return pl.BlockSpec((a.shape[0], tt, a.shape[2]), lambda t: (0, t, 0))
        return pl.BlockSpec((tt, a.shape[1]), lambda t: (t, 0))

    out_shapes = [(N_DEV, k_shapes[0][0], tn)] if slabs else k_shapes
    return pl.pallas_call(
        body, name=name, grid=(nt,),
        out_shape=[jax.ShapeDtypeStruct(s, BF16) for s in out_shapes],
        in_specs=[lhs_spec(a) for a in lhs_list] + [pl.BlockSpec((tt, n_len), lambda t: (t, 0))],
        out_specs=[_acc_spec(s) for s in out_shapes],
        scratch_shapes=[pltpu.VMEM(s, F32) for s in k_shapes],
        compiler_params=_seq_params(),
    )(*lhs_list, rhs)


def _pad_layout(d):
    names = ("qf", "kf", "vf", "qb", "kb", "vb", "gates", "forget")
    sizes = (N_HEADS * HEAD_SLOT, N_HEADS * HEAD_SLOT, D_BRANCH, D_BRANCH, D_BRANCH, D_BRANCH, 2 * d, LANES)
    out, off = {}, 0
    for n, s in zip(names, sizes):
        out[n] = (off, off + s)
        off += s
    return out, off


def _slot_rows(xt, extra):
    parts = []
    for h in range(N_HEADS):
        parts += [xt[h * HEAD_DIM:(h + 1) * HEAD_DIM, :], extra]
    return jnp.concatenate(parts, axis=0)


def _inproj_fwd(x, g_mix, w_pad, bf_pad, place_q, place_k, ones_q, ones_k, seq):
    t_len, d = x.shape
    lay, _ = _pad_layout(d)
    tiles_per_seq = seq // TM
    slot_w = N_HEADS * HEAD_SLOT

    def body(x_ref, g_ref, w_ref, bf_ref, pq_ref, pk_ref, oq_ref, ok_ref,
             qf_ref, kf_ref, kft_ref, vf_ref, vft_ref, qkvb_ref, kbt_ref, vbt_ref, gl_ref, fpre_ref, h_ref, qft_ref,
             carry_ref):
        @pl.when(pl.program_id(0) % tiles_per_seq == 0)
        def _():
            carry_ref[...] = jnp.zeros_like(carry_ref)

        def proj(name):
            lo, hi = lay[name]
            return _dot_nt(h, w_ref[lo:hi, :])

        xn, _ = _rms(x_ref[...])
        h = (xn * g_ref[...]).astype(BF16)
        fpre = proj("forget") + bf_ref[...]
        fpre_ref[...] = fpre
        logf = -_softplus(-fpre)
        lower = _tri(TM, lambda r, c: c <= r)
        hi, mid, lo = _split3(logf)
        c_val = carry_ref[...] + _dot(lower, hi) + _dot(lower, mid) + _dot(lower, lo)
        carry_ref[...] = carry_ref[...] + jnp.sum(logf, axis=0, keepdims=True)
        head_lanes = lax.broadcasted_iota(jnp.int32, (TM, LANES), 1) < N_HEADS
        terms = [jnp.where(head_lanes, t.astype(F32), 0.0) for t in _split3(c_val)]
        c_packed = (terms[0] + pltpu.roll(terms[1], N_HEADS, 1) + pltpu.roll(terms[2], 2 * N_HEADS, 1)).astype(BF16)
        qf = proj("qf") + _dot(c_packed, pq_ref[...]) + oq_ref[...]
        qf_ref[...] = qf.astype(BF16)
        qft_ref[0] = qf.T.astype(BF16)
        kf = proj("kf") - _dot(c_packed, pk_ref[...]) + ok_ref[...]
        kf_ref[...] = kf.astype(BF16)
        kft_ref[0] = kf.T.astype(BF16)
        row0 = (lax.broadcasted_iota(jnp.int32, (HEAD_DIM, TM), 0) == 0).astype(F32)
        zeros = jnp.zeros((HEAD_DIM, TM), F32)
        vf = proj("vf")
        vf_ref[...] = vf.astype(BF16)
        vft_ref[0] = _slot_rows(vf.T, row0).astype(BF16)
        qkvb_ref[0] = proj("qb").astype(BF16)
        kb = proj("kb")
        qkvb_ref[1] = kb.astype(BF16)
        kbt_ref[0] = _slot_rows(kb.T, zeros).astype(BF16)
        vb = proj("vb")
        qkvb_ref[2] = vb.astype(BF16)
        vbt_ref[0] = _slot_rows(vb.T, row0).astype(BF16)
        gl_ref[...] = proj("gates").astype(BF16)
        h_ref[...] = h

    n_tiles = t_len // TM
    slot_shape = jax.ShapeDtypeStruct((t_len, slot_w), BF16)
    t_shape = jax.ShapeDtypeStruct((n_tiles, slot_w, TM), BF16)
    t_spec = pl.BlockSpec((1, slot_w, TM), lambda i: (i, 0, 0))
    return pl.pallas_call(
        body, name="inproj_fwd", grid=(n_tiles,),
        out_shape=(slot_shape, slot_shape, t_shape, jax.ShapeDtypeStruct((t_len, D_BRANCH), BF16), t_shape,
                   jax.ShapeDtypeStruct((3, t_len, D_BRANCH), BF16), t_shape, t_shape,
                   jax.ShapeDtypeStruct((t_len, 2 * d), BF16), jax.ShapeDtypeStruct((t_len, LANES), F32),
                   jax.ShapeDtypeStruct((t_len, d), BF16), t_shape),
        in_specs=[_row_spec(TM, d), _const_spec((1, d)), _const_spec(w_pad.shape), _const_spec((1, LANES)),
                  _const_spec(place_q.shape), _const_spec(place_k.shape), _const_spec((1, slot_w)),
                  _const_spec((1, slot_w))],
        out_specs=(_row_spec(TM, slot_w), _row_spec(TM, slot_w), t_spec, _row_spec(TM, D_BRANCH), t_spec,
                   _row3_spec(3, TM, D_BRANCH), t_spec, t_spec, _row_spec(TM, 2 * d), _row_spec(TM, LANES),
                   _row_spec(TM, d), t_spec),
        scratch_shapes=[pltpu.VMEM((1, LANES), F32)],
        compiler_params=_seq_params(),
    )(x, g_mix, w_pad, bf_pad, place_q, place_k, ones_q, ones_k)


def _slot_spec(seq):
    return pl.BlockSpec((seq, NH * HEAD_SLOT), lambda b, g: (b, g))


def _slot2_spec(seq):
    return pl.BlockSpec((2, seq, NH * HEAD_SLOT), lambda b, g: (0, b, g))


def _group_spec(seq):
    return pl.BlockSpec((seq, NH * HEAD_DIM), lambda b, g: (b, g))


def _group3_spec(which, seq):
    return pl.BlockSpec((None, seq, NH * HEAD_DIM), lambda b, g: (which, b, g))


def _tblock_spec(seq):
    return pl.BlockSpec((seq // TK, NH * HEAD_SLOT, TK), lambda b, g: (b, g, 0))


def _qrow_spec(seq):
    return pl.BlockSpec((None, NH, seq // TQ, TQ), lambda b, g: (b, g, 0, 0))


def _attn_params():
    return pltpu.CompilerParams(dimension_semantics=("parallel", "parallel"), vmem_limit_bytes=VMEM_LIMIT)


def _serial_attn_params():
    return pltpu.CompilerParams(dimension_semantics=("arbitrary", "arbitrary"), vmem_limit_bytes=VMEM_LIMIT)


def _hcols(hh):
    return slice(hh * HEAD_DIM, (hh + 1) * HEAD_DIM)


def _hslot(hh):
    return slice(hh * HEAD_SLOT, (hh + 1) * HEAD_SLOT)


def _key_query_mask(rel):
    r = lax.broadcasted_iota(jnp.int32, (TK, TQ), 0)
    c = lax.broadcasted_iota(jnp.int32, (TK, TQ), 1)
    return rel(r, c)


def _heads_cat(vals):
    return jnp.concatenate(vals, axis=1)


def _untranspose(acc_t):
    return acc_t.T[:, :HEAD_DIM]


def _fox_fwd(qf, kf, vft, batch, seq):
    def body(q_ref, k_ref, vt_ref, o_ref, lse_ref, m_s, acc_s):
        causal = _key_query_mask(lambda r, c: r <= c)

        def tile(q0, kj, masked, n_k=1):
            krows = pl.ds(pl.multiple_of(kj * TK, TK), n_k * TK)
            heads = range(NH)
            sts = [_dot_nt(k_ref[krows, _hslot(hh)], q_ref[pl.ds(q0, TQ), _hslot(hh)]) for hh in heads]
            if masked:
                sts = [jnp.where(causal, st, NEG) for st in sts]
            m_olds = [m_s[hh] for hh in heads]
            m_news = [jnp.maximum(m_olds[hh], jnp.max(sts[hh], axis=0, keepdims=True)) for hh in heads]
            pts = [jnp.exp(sts[hh] - m_news[hh]).astype(BF16) for hh in heads]
            pvs = [sum(_dot(vt_ref[kj + i, _hslot(hh), :], pts[hh][i * TK:(i + 1) * TK]) for i in range(n_k))
                   for hh in heads]
            for hh in heads:
                acc_s[hh] = jnp.exp(m_olds[hh] - m_news[hh]) * acc_s[hh] + pvs[hh]
                m_s[hh] = m_news[hh]

        def q_loop(qi, _):
            q0 = pl.multiple_of(qi * TQ, TQ)
            m_s[...] = jnp.full(m_s.shape, NEG, F32)
            acc_s[...] = jnp.zeros_like(acc_s)

            def pair_loop(i, _):
                tile(q0, 2 * i, False, n_k=2)
                return 0

            lax.fori_loop(0, qi // 2, pair_loop, 0)
            pl.when(qi % 2 == 1)(lambda: tile(q0, qi - 1, False))
            tile(q0, qi, True)
            outs = []
            for hh in range(NH):
                total = acc_s[hh, HEAD_DIM:HEAD_DIM + 1, :]
                outs.append(_untranspose(acc_s[hh] / total))
                lse_ref[hh, pl.ds(qi, 1), :] = m_s[hh] + jnp.log(total)
            o_ref[pl.ds(q0, TQ), :] = _heads_cat(outs).astype(BF16)
            return 0

        lax.fori_loop(0, seq // TQ, q_loop, 0)

    return pl.pallas_call(
        body, name="fox_fwd", grid=(batch, N_HEADS // NH),
        out_shape=(jax.ShapeDtypeStruct((batch * seq, D_BRANCH), BF16),
                   jax.ShapeDtypeStruct((batch, N_HEADS, seq // TQ, TQ), F32)),
        in_specs=[_slot_spec(seq), _slot_spec(seq), _tblock_spec(seq)],
        out_specs=(_group_spec(seq), _qrow_spec(seq)),
        scratch_shapes=[pltpu.VMEM((NH, 1, TQ), F32), pltpu.VMEM((NH, HEAD_SLOT, TQ), F32)],
        compiler_params=_attn_params(),
    )(qf, kf, vft)


def _fox_bwd(qf, qft, kf, kft, vf, o, do, dot, lse, batch, seq, partials):
    n_q = seq // TQ
    n = len(partials)

    def body(q_ref, qt_ref, k_ref, kt_ref, v_ref, o_ref, do_ref, dot_ref, lse_ref, *rest):
        p_refs, (dqk_ref, dv_ref, dcq_ref, dck_ref), recv_refs = rest[:n], rest[n:n + 4], rest[n + 4:2 * n + 4]
        delta_s, dqt_acc, dk_s, dv_s = rest[2 * n + 4:2 * n + 8]
        pair_start, pair_finish = _pair_plan(p_refs, recv_refs, *rest[2 * n + 8:])
        first_step, last_step = _first_last_step()
        pl.when(first_step)(pair_start)
        causal = _key_query_mask(lambda r, c: r <= c)
        ones8 = jnp.ones((8, HEAD_DIM), BF16)
        dqt_acc[...] = jnp.zeros_like(dqt_acc)

        def prep(qi, _):
            rows = pl.ds(pl.multiple_of(qi * TQ, TQ), TQ)
            for hh in range(NH):
                hi, lo = _split2(do_ref[rows, _hcols(hh)].astype(F32) * o_ref[rows, _hcols(hh)].astype(F32))
                delta_s[hh, pl.ds(qi, 1), :] = (_dot_nt(ones8, hi) + _dot_nt(ones8, lo))[0:1, :]
            return 0

        lax.fori_loop(0, n_q, prep, 0)

        def tile(qis, kj, masked):
            krows = pl.ds(pl.multiple_of(kj * TK, TK), TK)
            heads = range(NH)
            items = [(t, hh) for t in range(len(qis)) for hh in heads]
            rows = [pl.ds(qi * TQ if isinstance(qi, int) else pl.multiple_of(qi * TQ, TQ), TQ) for qi in qis]
            sts = [_dot_nt(k_ref[krows, _hslot(hh)], q_ref[rows[t], _hslot(hh)]) for t, hh in items]
            dps = [_dot_nt(v_ref[krows, _hcols(hh)], do_ref[rows[t], _hcols(hh)]) for t, hh in items]
            pts = [jnp.exp(sts[i] - lse_ref[hh, pl.ds(qis[t], 1), :]) for i, (t, hh) in enumerate(items)]
            if masked:
                pts = [jnp.where(causal, pt, 0.0) for pt in pts]
            dsts = [(pts[i] * (dps[i] - delta_s[hh, pl.ds(qis[t], 1), :])).astype(BF16)
                    for i, (t, hh) in enumerate(items)]
            for i, (t, hh) in enumerate(items):
                dv_s[hh] += _dot_nt(dot_ref[qis[t], _hslot(hh), :], pts[i].astype(BF16))
                dk_s[hh] += _dot_nt(qt_ref[qis[t], _hslot(hh), :], dsts[i])
                dqt_acc[hh, qis[t]] += _dot(kt_ref[kj, _hslot(hh), :], dsts[i])

        def k_loop(kj, _):
            krows = pl.ds(pl.multiple_of(kj * TK, TK), TK)
            dk_s[...] = jnp.zeros_like(dk_s)
            dv_s[...] = jnp.zeros_like(dv_s)
            tile([kj], kj, True)
            left = n_q - 1 - kj

            def pair_loop(i, _):
                tile([kj + 1 + 2 * i, kj + 2 + 2 * i], kj, False)
                return 0

            lax.fori_loop(0, left // 2, pair_loop, 0)
            pl.when(left % 2 == 1)(lambda: tile([n_q - 1], kj, False))
            dqk_ref[1, krows, :] = _heads_cat([dk_s[hh].T for hh in range(NH)]).astype(BF16)
            dv_ref[krows, :] = _heads_cat([_untranspose(dv_s[hh]) for hh in range(NH)]).astype(BF16)
            for hh in range(NH):
                dck_ref[hh, pl.ds(kj, 1), :] = dk_s[hh, C_ONES_Q:C_ONES_Q + 1, :]
            return 0

        lax.fori_loop(0, seq // TK, k_loop, 0)

        def finish(qi, _):
            rows = pl.ds(pl.multiple_of(qi * TQ, TQ), TQ)
            dqk_ref[0, rows, :] = _heads_cat([dqt_acc[hh, qi].T for hh in range(NH)]).astype(BF16)
            for hh in range(NH):
                dcq_ref[hh, pl.ds(qi, 1), :] = dqt_acc[hh, qi, C_ONES_K:C_ONES_K + 1, :]
            return 0

        lax.fori_loop(0, n_q, finish, 0)
        pl.when(last_step)(pair_finish)

    out = pl.pallas_call(
        body, name="fox_bwd", grid=(batch, N_HEADS // NH),
        out_shape=[jax.ShapeDtypeStruct((2, batch * seq, N_HEADS * HEAD_SLOT), BF16),
                   jax.ShapeDtypeStruct((batch * seq, D_BRANCH), BF16),
                   jax.ShapeDtypeStruct((batch, N_HEADS, seq // TQ, TQ), F32),
                   jax.ShapeDtypeStruct((batch, N_HEADS, seq // TK, TK), F32)] + _pair_shapes(partials),
        in_specs=[_slot_spec(seq), _tblock_spec(seq), _slot_spec(seq), _tblock_spec(seq), _group_spec(seq),
                  _group_spec(seq), _group_spec(seq), _tblock_spec(seq), _qrow_spec(seq)] + _hbm_specs(n),
        out_specs=[_slot2_spec(seq), _group_spec(seq), _qrow_spec(seq), _qrow_spec(seq)] + _hbm_specs(n),
        scratch_shapes=[pltpu.VMEM((NH, n_q, TQ), F32), pltpu.VMEM((NH, n_q, HEAD_SLOT, TQ), F32),
                        pltpu.VMEM((NH, HEAD_SLOT, TK), F32), pltpu.VMEM((NH, HEAD_SLOT, TK), F32)] + _pair_sems(n),
        compiler_params=_serial_attn_params(),
    )(qf, qft, kf, kft, vf, o, do, dot, lse, *partials)
    return out[0], out[1], out[2], out[3], out[4:]


def _first_last_step():
    step = pl.program_id(0) * pl.num_programs(1) + pl.program_id(1)
    return step == 0, step == pl.num_programs(0) * pl.num_programs(1) - 1


def _sb_fwd(qkvb, vbt, batch, seq, shards):
    n = len(shards)

    def body(q_ref, k_ref, vt_ref, *rest):
        x_refs, (o_ref, lt_ref), out_refs = rest[:n], rest[n:n + 2], rest[n + 2:2 * n + 2]
        run_s, acc_s = rest[2 * n + 2:2 * n + 4]
        gather_start, gather_finish = _gather_plan(x_refs, out_refs, *rest[2 * n + 4:])
        first_step, last_step = _first_last_step()
        pl.when(first_step)(gather_start)
        strict = _key_query_mask(lambda r, c: r < c)
        later = _tri(TK, lambda r, c: c > r)

        def tile(q0, kjs, masked):
            heads = range(NH)
            items = [(t, hh) for t in range(len(kjs)) for hh in heads]
            krows = [pl.ds(kj * TK if isinstance(kj, int) else pl.multiple_of(kj * TK, TK), TK) for kj in kjs]
            zts = [_dot_nt(k_ref[krows[t], _hcols(hh)], q_ref[pl.ds(q0, TQ), _hcols(hh)]) for t, hh in items]
            lgs = [-_softplus(zt) for zt in zts]
            if masked:
                lgs = [jnp.where(strict, lg, 0.0) for lg in lgs]
            parts = [_split2(lg) for lg in lgs]
            sufs = [_dot(later, hi) + _dot(later, lo) for hi, lo in parts]
            sums = [jnp.sum(lg, axis=0, keepdims=True) for lg in lgs]
            runs = {}
            for hh in heads:
                run = run_s[hh]
                for t in range(len(kjs)):
                    runs[t, hh] = run
                    run = run + sums[t * NH + hh]
                run_s[hh] = run
            ats = [jnp.exp(zts[i] + lgs[i] + runs[item] + sufs[i]) for i, item in enumerate(items)]
            if masked:
                ats = [jnp.where(strict, at, 0.0) for at in ats]
            for hh in heads:
                acc_s[hh] += sum(_dot(vt_ref[kjs[t], _hslot(hh), :], ats[t * NH + hh].astype(BF16))
                                 for t in range(len(kjs)))

        def q_loop(qi, _):
            q0 = pl.multiple_of(qi * TQ, TQ)
            run_s[...] = jnp.zeros_like(run_s)
            acc_s[...] = jnp.zeros_like(acc_s)
            tile(q0, [qi], True)

            def pair_loop(i, _):
                tile(q0, [qi - 1 - 2 * i, qi - 2 - 2 * i], False)
                return 0

            lax.fori_loop(0, qi // 2, pair_loop, 0)
            pl.when(qi % 2 == 1)(lambda: tile(q0, [0], False))
            o_ref[pl.ds(q0, TQ), :] = _heads_cat([_untranspose(acc_s[hh]) for hh in range(NH)]).astype(BF16)
            for hh in range(NH):
                lt_ref[hh, pl.ds(qi, 1), :] = run_s[hh]
            return 0

        lax.fori_loop(0, seq // TQ, q_loop, 0)
        pl.when(last_step)(gather_finish)

    out = pl.pallas_call(
        body, name="sb_fwd", grid=(batch, N_HEADS // NH),
        out_shape=[jax.ShapeDtypeStruct((batch * seq, D_BRANCH), BF16),
                   jax.ShapeDtypeStruct((batch, N_HEADS, seq // TQ, TQ), F32)] + _gather_shapes(shards),
        in_specs=[_group3_spec(0, seq), _group3_spec(1, seq), _tblock_spec(seq)] + _hbm_specs(n),
        out_specs=[_group_spec(seq), _qrow_spec(seq)] + _hbm_specs(n),
        scratch_shapes=[pltpu.VMEM((NH, 1, TQ), F32), pltpu.VMEM((NH, HEAD_SLOT, TQ), F32)] + _gather_sems(n),
        compiler_params=_serial_attn_params(),
    )(qkvb, qkvb, vbt, *shards)
    return out[0], out[1], out[2:]


def _sb_bwd(qkvb, kbt, do, ltot, batch, seq, chip_sums):
    n = len(chip_sums)

    def body(q_ref, k_ref, v_ref, kt_ref, do_ref, lt_ref, *rest):
        cs_refs, dqkv_ref, out_refs = rest[:n], rest[n], rest[n + 1:2 * n + 1]
        dk_acc, dv_acc, ls_s, gs_s, dqt_s = rest[2 * n + 1:2 * n + 6]
        chips_start, chips_finish = _chips_plan(cs_refs, out_refs, *rest[2 * n + 6:])
        first_step, last_step = _first_last_step()
        pl.when(first_step)(chips_start)
        strict = _key_query_mask(lambda r, c: r < c)
        upto = _tri(TK, lambda r, c: c <= r)
        before = _tri(TK, lambda r, c: c < r)
        dk_acc[...] = jnp.zeros_like(dk_acc)
        dv_acc[...] = jnp.zeros_like(dv_acc)

        def tile(qi, kj, masked):
            rows = pl.ds(pl.multiple_of(qi * TQ, TQ), TQ)
            krows = pl.ds(pl.multiple_of(kj * TK, TK), TK)
            heads = range(NH)
            qs = [q_ref[rows, _hcols(hh)] for hh in heads]
            douts = [do_ref[rows, _hcols(hh)] for hh in heads]
            zts = [_dot_nt(k_ref[krows, _hcols(hh)], qs[hh]) for hh in heads]
            das = [_dot_nt(v_ref[krows, _hcols(hh)], douts[hh]) for hh in heads]
            lgs = [-_softplus(zt) for zt in zts]
            if masked:
                lgs = [jnp.where(strict, lg, 0.0) for lg in lgs]
            parts = [_split2(lg) for lg in lgs]
            prefs = [_dot(upto, hi) + _dot(upto, lo) for hi, lo in parts]
            ats = [jnp.exp(zts[hh] + lgs[hh] + (lt_ref[hh, pl.ds(qi, 1), :] - ls_s[hh]) - prefs[hh]) for hh in heads]
            if masked:
                ats = [jnp.where(strict, at, 0.0) for at in ats]
            gts = [das[hh] * ats[hh] for hh in heads]
            us = [gs_s[hh] + _dot(before, gts[hh].astype(BF16)) for hh in heads]
            dzts = [(jnp.exp(lgs[hh]) * (gts[hh] + us[hh]) - us[hh]).astype(BF16) for hh in heads]
            for hh in heads:
                dk_acc[hh, krows, :] += _dot(dzts[hh], qs[hh])
                dv_acc[hh, krows, :] += _dot(ats[hh].astype(BF16), douts[hh])
                dqt_s[hh] += _dot(kt_ref[kj, _hslot(hh), :], dzts[hh])
                ls_s[hh] += jnp.sum(lgs[hh], axis=0, keepdims=True)
                gs_s[hh] += jnp.sum(gts[hh], axis=0, keepdims=True)

        def q_loop(qi, _):
            ls_s[...] = jnp.zeros_like(ls_s)
            gs_s[...] = jnp.zeros_like(gs_s)
            dqt_s[...] = jnp.zeros_like(dqt_s)

            def k_loop(kj, _):
                tile(qi, kj, False)
                return 0

            lax.fori_loop(0, qi, k_loop, 0)
            tile(qi, qi, True)
            dqkv_ref[0, pl.ds(pl.multiple_of(qi * TQ, TQ), TQ), :] = _heads_cat(
                [_untranspose(dqt_s[hh]) for hh in range(NH)]).astype(BF16)
            return 0

        lax.fori_loop(0, seq // TQ, q_loop, 0)
        dqkv_ref[1] = _heads_cat([dk_acc[hh] for hh in range(NH)]).astype(BF16)
        dqkv_ref[2] = _heads_cat([dv_acc[hh] for hh in range(NH)]).astype(BF16)
        pl.when(last_step)(chips_finish)

    out = pl.pallas_call(
        body, name="sb_bwd", grid=(batch, N_HEADS // NH),
        out_shape=[jax.ShapeDtypeStruct((3, batch * seq, D_BRANCH), BF16)]
        + [jax.ShapeDtypeStruct(s.shape, s.dtype) for s in chip_sums],
        in_specs=[_group3_spec(0, seq), _group3_spec(1, seq), _group3_spec(2, seq), _tblock_spec(seq),
                  _group_spec(seq), _qrow_spec(seq)] + _hbm_specs(n),
        out_specs=[pl.BlockSpec((3, seq, NH * HEAD_DIM), lambda b, g: (0, b, g))] + _hbm_specs(n),
        scratch_shapes=[pltpu.VMEM((NH, seq, HEAD_DIM), F32), pltpu.VMEM((NH, seq, HEAD_DIM), F32),
                        pltpu.VMEM((NH, 1, TQ), F32), pltpu.VMEM((NH, 1, TQ), F32),
                        pltpu.VMEM((NH, HEAD_SLOT, TQ), F32)] + _chips_sems(n),
        compiler_params=_serial_attn_params(),
    )(qkvb, qkvb, qkvb, kbt, do, ltot, *chip_sums)
    return out[0], out[1:]


def _forget_bwd(dcq_tok, dck_tok, fpre, batch, seq):
    t_len = batch * seq
    tiles = seq // TM

    def rev(i):
        return ((i // tiles) * tiles + (tiles - 1 - i % tiles), 0)

    def body(dcq_ref, dck_ref, f_ref, df_ref, db_ref, carry_ref):
        i = pl.program_id(0)

        @pl.when(i == 0)
        def _():
            db_ref[...] = jnp.zeros_like(db_ref)

        @pl.when(i % tiles == 0)
        def _():
            carry_ref[...] = jnp.zeros_like(carry_ref)

        dc = dcq_ref[...] - dck_ref[...]
        upper = _tri(TM, lambda r, c: c >= r)
        hi, mid, lo = _split3(dc)
        dlogf = carry_ref[...] + _dot(upper, hi) + _dot(upper, mid) + _dot(upper, lo)
        carry_ref[...] = carry_ref[...] + jnp.sum(dc, axis=0, keepdims=True)
        df = dlogf * _sigmoid(-f_ref[...])
        df_ref[...] = df.astype(BF16)
        db_ref[...] += jnp.sum(df, axis=0, keepdims=True)

    return pl.pallas_call(
        body, name="forget_bwd", grid=(t_len // TM,),
        out_shape=(jax.ShapeDtypeStruct((t_len, LANES), BF16), jax.ShapeDtypeStruct((1, LANES), F32)),
        in_specs=[pl.BlockSpec((TM, LANES), rev)] * 3,
        out_specs=(pl.BlockSpec((TM, LANES), rev), _acc_spec((1, LANES))),
        scratch_shapes=[pltpu.VMEM((1, LANES), F32)],
        compiler_params=_seq_params(),
    )(dcq_tok, dck_tok, fpre)


def _mix_fwd(o_fox, o_sb, gl, x, w_bf, w_bs, w_out, b_gate):
    t_len, d = x.shape

    def body(of_ref, os_ref, gl_ref, x_ref, wbf_ref, wbs_ref, wo_ref, bg_ref, x1_ref):
        br_f = _dot(of_ref[...], wbf_ref[...])
        br_s = _dot(os_ref[...], wbs_ref[...])
        ga = _sigmoid(gl_ref[:, :d].astype(F32) + bg_ref[0:1, :])
        gb = _sigmoid(gl_ref[:, d:].astype(F32) + bg_ref[1:2, :])
        merged = ga * br_f + gb * br_s
        x1_ref[...] = x_ref[...] + _dot(merged.astype(BF16), wo_ref[...])

    return pl.pallas_call(
        body, name="mix_fwd", grid=(t_len // TM,),
        out_shape=jax.ShapeDtypeStruct((t_len, d), F32),
        in_specs=[_row_spec(TM, D_BRANCH), _row_spec(TM, D_BRANCH), _row_spec(TM, 2 * d), _row_spec(TM, d),
                  _const_spec(w_bf.shape), _const_spec(w_bs.shape), _const_spec(w_out.shape), _const_spec(b_gate.shape)],
        out_specs=_row_spec(TM, d),
        compiler_params=_seq_params(),
    )(o_fox, o_sb, gl, x, w_bf, w_bs, w_out, b_gate)


def _ff_chunk(d_ff):
    return min(d_ff, 1024)


def _mlp_head_fwd_bwd(x1, p, target, g_mlp, w_up, w_down, g_ple, g_final, w_pg, w_ple):
    t_len, d = x1.shape
    d_ple = p.shape[1]
    d_ff = w_up.shape[1]
    ch = _ff_chunk(d_ff)

    def body(x1_ref, p_ref, t_ref, gm_ref, wu_ref, wd_ref, gp_ref, gf_ref, wpg_ref, wple_ref,
             a_ref, dx2_ref, h3_ref, dpre_ref, dpe_ref, loss_ref, dgp_ref, dgf_ref):
        @pl.when(pl.program_id(0) == 0)
        def _():
            loss_ref[...] = jnp.zeros_like(loss_ref)
            dgp_ref[...] = jnp.zeros_like(dgp_ref)
            dgf_ref[...] = jnp.zeros_like(dgf_ref)

        x1v = x1_ref[...]
        x1n, _ = _rms(x1v)
        h2 = (x1n * gm_ref[...]).astype(BF16)
        x2v = x1v
        for j in range(d_ff // ch):
            a = _dot(h2, wu_ref[:, j * ch:(j + 1) * ch])
            a_ref[:, j * ch:(j + 1) * ch] = a.astype(BF16)
            x2v = x2v + _dot(jnp.square(jnp.maximum(a, 0.0)).astype(BF16), wd_ref[j * ch:(j + 1) * ch, :])
        x2n, r3 = _rms(x2v)
        h3 = (x2n * gp_ref[...]).astype(BF16)
        h3_ref[...] = h3
        gate = _sigmoid(_dot(h3, wpg_ref[...]))
        pe = _dot(p_ref[...].astype(BF16), wple_ref[...])
        x3n, r4 = _rms(x2v + gate * pe)
        err = x3n * gf_ref[...] - t_ref[...]
        loss_ref[...] += jnp.full(loss_ref.shape, (0.5 / d) * jnp.sum(err * err), F32)
        dx3, dgf = _rms_bwd(err * (1.0 / d), x3n, r4, gf_ref[...])
        dgf_ref[...] += dgf
        dpe_ref[...] = (dx3 * gate).astype(BF16)
        dpre = (dx3 * pe * gate * (1.0 - gate)).astype(BF16)
        dpre_ref[...] = dpre
        dres, dgp = _rms_bwd(_dot_nt(dpre, wpg_ref[...]), x2n, r3, gp_ref[...])
        dgp_ref[...] += dgp
        dx2_ref[...] = dx3 + dres

    shp_b = jax.ShapeDtypeStruct((t_len, d), BF16)
    return pl.pallas_call(
        body, name="mlp_head_fwd_bwd", grid=(t_len // TM,),
        out_shape=(jax.ShapeDtypeStruct((t_len, d_ff), BF16), jax.ShapeDtypeStruct((t_len, d), F32), shp_b, shp_b, shp_b,
                   jax.ShapeDtypeStruct((1, LANES), F32), jax.ShapeDtypeStruct((1, d), F32),
                   jax.ShapeDtypeStruct((1, d), F32)),
        in_specs=[_row_spec(TM, d), _row_spec(TM, d_ple), _row_spec(TM, d), _const_spec((1, d)),
                  _const_spec(w_up.shape), _const_spec(w_down.shape), _const_spec((1, d)), _const_spec((1, d)),
                  _const_spec(w_pg.shape), _const_spec(w_ple.shape)],
        out_specs=(_row_spec(TM, d_ff), _row_spec(TM, d), _row_spec(TM, d), _row_spec(TM, d), _row_spec(TM, d),
                   _acc_spec((1, LANES)), _acc_spec((1, d)), _acc_spec((1, d))),
        compiler_params=_seq_params(),
    )(x1, p, target, g_mlp, w_up, w_down, g_ple, g_final, w_pg, w_ple)


def _mlp_bwd(dx2, a, x1, g_mlp, w_up, w_down):
    t_len, d = x1.shape
    d_ff = w_up.shape[1]
    ch = _ff_chunk(d_ff)

    def body(dx2_ref, a_ref, x1_ref, g_ref, wu_ref, wd_ref, dx1_ref, da_ref, h2_ref, dg_ref):
        @pl.when(pl.program_id(0) == 0)
        def _():
            dg_ref[...] = jnp.zeros_like(dg_ref)

        dx2v = dx2_ref[...]
        dx2b = dx2v.astype(BF16)
        xn, r = _rms(x1_ref[...])
        h2_ref[...] = (xn * g_ref[...]).T.astype(BF16)
        dh = jnp.zeros((TM, d), F32)
        for j in range(d_ff // ch):
            dact = _dot_nt(dx2b, wd_ref[j * ch:(j + 1) * ch, :])
            da = (dact * 2.0 * jnp.maximum(a_ref[:, j * ch:(j + 1) * ch].astype(F32), 0.0)).astype(BF16)
            da_ref[:, j * ch:(j + 1) * ch] = da
            dh = dh + _dot_nt(da, wu_ref[:, j * ch:(j + 1) * ch])
        dres, dg = _rms_bwd(dh, xn, r, g_ref[...])
        dg_ref[...] += dg
        dx1_ref[...] = dx2v + dres

    return pl.pallas_call(
        body, name="mlp_bwd", grid=(t_len // TM,),
        out_shape=(jax.ShapeDtypeStruct((t_len, d), F32), jax.ShapeDtypeStruct((t_len, d_ff), BF16),
                   jax.ShapeDtypeStruct((d, t_len), BF16), jax.ShapeDtypeStruct((1, d), F32)),
        in_specs=[_row_spec(TM, d), _row_spec(TM, d_ff), _row_spec(TM, d), _const_spec((1, d)),
                  _const_spec(w_up.shape), _const_spec(w_down.shape)],
        out_specs=(_row_spec(TM, d), _row_spec(TM, d_ff), _col_spec(d, TM), _acc_spec((1, d))),
        compiler_params=_seq_params(),
    )(dx2, a, x1, g_mlp, w_up, w_down)


def _mix_bwd(dx1, o_fox, o_sb, gl, w_bf, w_bs, w_out, b_gate):
    t_len, d = dx1.shape

    def body(dx1_ref, of_ref, os_ref, gl_ref, wbf_ref, wbs_ref, wo_ref, bg_ref,
             mg_ref, dbf_ref, dbs_ref, dgl_ref, dof_ref, dos_ref, dbg_ref, doft_ref):
        @pl.when(pl.program_id(0) == 0)
        def _():
            dbg_ref[...] = jnp.zeros_like(dbg_ref)

        dmerged = _dot_nt(dx1_ref[...].astype(BF16), wo_ref[...])
        br_f = _dot(of_ref[...], wbf_ref[...])
        br_s = _dot(os_ref[...], wbs_ref[...])
        ga = _sigmoid(gl_ref[:, :d].astype(F32) + bg_ref[0:1, :])
        gb = _sigmoid(gl_ref[:, d:].astype(F32) + bg_ref[1:2, :])
        mg_ref[...] = (ga * br_f + gb * br_s).astype(BF16)
        dbf = (dmerged * ga).astype(BF16)
        dbs = (dmerged * gb).astype(BF16)
        dbf_ref[...] = dbf
        dbs_ref[...] = dbs
        dla = dmerged * br_f * ga * (1.0 - ga)
        dlb = dmerged * br_s * gb * (1.0 - gb)
        dgl_ref[:, :d] = dla.astype(BF16)
        dgl_ref[:, d:] = dlb.astype(BF16)
        dbg_ref[0:1, :] += jnp.sum(dla, axis=0, keepdims=True)
        dbg_ref[1:2, :] += jnp.sum(dlb, axis=0, keepdims=True)
        dof = _dot_nt(dbf, wbf_ref[...])
        dof_ref[...] = dof.astype(BF16)
        doft_ref[0] = _slot_rows(dof.T, jnp.zeros((HEAD_DIM, TM), F32)).astype(BF16)
        dos_ref[...] = _dot_nt(dbs, wbs_ref[...]).astype(BF16)

    shp_d = jax.ShapeDtypeStruct((t_len, d), BF16)
    shp_h = jax.ShapeDtypeStruct((t_len, D_BRANCH), BF16)
    return pl.pallas_call(
        body, name="mix_bwd", grid=(t_len // TM,),
        out_shape=(shp_d, shp_d, shp_d, jax.ShapeDtypeStruct((t_len, 2 * d), BF16), shp_h, shp_h,
                   jax.ShapeDtypeStruct((2, d), F32),
                   jax.ShapeDtypeStruct((t_len // TM, N_HEADS * HEAD_SLOT, TM), BF16)),
        in_specs=[_row_spec(TM, d), _row_spec(TM, D_BRANCH), _row_spec(TM, D_BRANCH), _row_spec(TM, 2 * d),
                  _const_spec(w_bf.shape), _const_spec(w_bs.shape), _const_spec(w_out.shape), _const_spec(b_gate.shape)],
        out_specs=(_row_spec(TM, d), _row_spec(TM, d), _row_spec(TM, d), _row_spec(TM, 2 * d),
                   _row_spec(TM, D_BRANCH), _row_spec(TM, D_BRANCH), _acc_spec((2, d)),
                   pl.BlockSpec((1, N_HEADS * HEAD_SLOT, TM), lambda i: (i, 0, 0))),
        compiler_params=_seq_params(),
    )(dx1, o_fox, o_sb, gl, w_bf, w_bs, w_out, b_gate)


def _inproj_bwd(dqk_f, dv_f, dqkv_b, dgl, df, dx1, x, g_mix, w_pad, chip_sums):
    t_len, d = x.shape
    lay, _ = _pad_layout(d)
    slot_w = N_HEADS * HEAD_SLOT
    n = len(chip_sums)
    n_tiles = t_len // TM

    def body(dqk_ref, dvf_ref, db_ref, dgl_ref, df_ref, dx1_ref, x_ref, g_ref, w_ref, *rest):
        cs_refs, (dx_ref, dg_ref), out_refs = rest[:n], rest[n:n + 2], rest[n + 2:2 * n + 2]
        chips_start, chips_finish = _chips_plan(cs_refs, out_refs, *rest[2 * n + 2:])

        @pl.when(pl.program_id(0) == 0)
        def _():
            dg_ref[...] = jnp.zeros_like(dg_ref)
            chips_start()

        def back(piece, name):
            lo, hi = lay[name]
            return _dot(piece, w_ref[lo:hi, :])

        xn, r = _rms(x_ref[...])
        dh = (back(df_ref[...], "forget") + back(dgl_ref[...], "gates") + back(dqk_ref[0], "qf")
              + back(dqk_ref[1], "kf") + back(dvf_ref[...], "vf") + back(db_ref[0], "qb") + back(db_ref[1], "kb")
              + back(db_ref[2], "vb"))
        dres, dg = _rms_bwd(dh, xn, r, g_ref[...])
        dg_ref[...] += dg
        dx_ref[...] = dx1_ref[...] + dres
        pl.when(pl.program_id(0) == n_tiles - 1)(chips_finish)

    out = pl.pallas_call(
        body, name="inproj_bwd", grid=(n_tiles,),
        out_shape=[jax.ShapeDtypeStruct((t_len, d), F32), jax.ShapeDtypeStruct((1, d), F32)]
        + [jax.ShapeDtypeStruct(s.shape, s.dtype) for s in chip_sums],
        in_specs=[_row3_spec(2, TM, slot_w), _row_spec(TM, D_BRANCH), _row3_spec(3, TM, D_BRANCH),
                  _row_spec(TM, 2 * d), _row_spec(TM, LANES), _row_spec(TM, d), _row_spec(TM, d), _const_spec((1, d)),
                  _const_spec(w_pad.shape)] + _hbm_specs(n),
        out_specs=[_row_spec(TM, d), _acc_spec((1, d))] + _hbm_specs(n),
        scratch_shapes=_chips_sems(n),
        compiler_params=_seq_params(),
    )(dqk_f, dv_f, dqkv_b, dgl, df, dx1, x, g_mix, w_pad, *chip_sums)
    return out[0], out[1], out[2:]


def _cols_to_slabs(full):
    r, c8 = full.shape
    return full.reshape(r, N_DEV, c8 // N_DEV).transpose(1, 0, 2)


def _slabs_to_cols(slabs):
    n, r, c = slabs.shape
    return slabs.transpose(1, 0, 2).reshape(r, n * c)


def _win_sizes(d):
    return (D_BRANCH, D_BRANCH, D_BRANCH, N_HEADS, D_BRANCH, D_BRANCH, D_BRANCH, d, d)


def _split_win(w_t, d):
    out, off = [], 0
    for s in _win_sizes(d):
        out.append(w_t[off:off + s])
        off += s
    return out


def _to_slots(w_t):
    c = w_t.shape[1]
    return jnp.pad(w_t.reshape(N_HEADS, HEAD_DIM, c), ((0, 0), (0, HEAD_SLOT - HEAD_DIM), (0, 0))).reshape(-1, c)


def _from_slots(w_t):
    c = w_t.shape[1]
    return w_t.reshape(N_HEADS, HEAD_SLOT, c)[:, :HEAD_DIM].reshape(N_HEADS * HEAD_DIM, c)


def _pad_win(w_full_t, d):
    qa, ka, va, fa, qb, kb, vb, ga, gb = _split_win(w_full_t, d)
    scale = HEAD_DIM ** -0.5
    fpad = jnp.pad(fa, ((0, LANES - N_HEADS), (0, 0)))
    return jnp.concatenate([_to_slots(qa * scale), _to_slots(ka), va, qb * scale, kb, vb, ga, gb, fpad], axis=0)


def _unpad_dwin(dqk_f, dv_f, dqkv_b, dgates, dforget, d):
    scale = HEAD_DIM ** -0.5
    return jnp.concatenate([_from_slots(dqk_f[0]) * scale, _from_slots(dqk_f[1]), dv_f, dforget[:N_HEADS],
                            dqkv_b[0] * scale, dqkv_b[1], dqkv_b[2], dgates], axis=0)


def _c_lane_constants():
    row = jnp.arange(LANES)[:, None]
    lane = jnp.arange(N_HEADS * HEAD_SLOT)[None, :]

    def place(first):
        return ((lane // HEAD_SLOT == row % N_HEADS) & (lane % HEAD_SLOT == first + row // N_HEADS)
                & (row < 3 * N_HEADS)).astype(BF16)

    def ones(first):
        off = lane % HEAD_SLOT
        return ((off >= first) & (off < first + 3)).astype(F32)

    return place(C_TERMS_Q), place(C_TERMS_K), ones(C_ONES_Q), ones(C_ONES_K)


def _pad_rows(a, rows):
    return jnp.pad(a, [(0, 0)] * (a.ndim - 2) + [(0, rows - a.shape[-2]), (0, 0)])


def kernel(x, p, g_mix, w_in, b_forget, b_gate, w_branch_fox, w_branch_sb, w_out, g_mlp, w_up, w_down, g_ple, w_ple_gate, w_ple, g_final, loss_target, m_g_mix, m_w_in, m_b_forget, m_b_gate, m_w_branch_fox, m_w_branch_sb, m_w_out, m_g_mlp, m_w_up, m_w_down, m_g_ple, m_w_ple_gate, m_w_ple, m_g_final, v_g_mix, v_w_in, v_b_forget, v_b_gate, v_w_branch_fox, v_w_branch_sb, v_w_out, v_g_mlp, v_w_up, v_w_down, v_g_ple, v_w_ple_gate, v_w_ple, v_g_final):
    batch, seq, d = x.shape
    t_len = batch * seq
    d_ple = p.shape[-1]
    d_ff = w_up.shape[-1] * N_DEV
    dn = d // N_DEV
    fn = d_ff // N_DEV
    my_c = lax.axis_index("c")
    my_dev = 4 * lax.axis_index("x") + 2 * lax.axis_index("y") + my_c

    bg_hi = b_gate[0].astype(BF16)
    bg_r = b_gate[0] - bg_hi.astype(F32)
    bg_mid = bg_r.astype(BF16)
    bg_lo = (bg_r - bg_mid.astype(F32)).astype(BF16)
    narrow_rows = 2 * D_BRANCH + d_ple + 6
    narrow_rows_pad = -(-narrow_rows // 16) * 16
    narrow = _pad_rows(jnp.concatenate(
        [w_branch_fox[0].astype(BF16), w_branch_sb[0].astype(BF16), w_ple[0].astype(BF16), bg_hi, bg_mid, bg_lo],
        axis=0), narrow_rows_pad)
    g_in, = _all_gather([w_in[0].T.astype(BF16)])
    w_pad = _pad_win(g_in.reshape(-1, d), d)
    bf_pad = jnp.pad(b_forget, ((0, 0), (0, LANES - N_HEADS)))
    place_q, place_k, ones_q, ones_k = _c_lane_constants()

    x2d = x.reshape(t_len, d)
    p2d = p.reshape(t_len, d_ple)
    tgt2d = loss_target.reshape(t_len, d)
    qf, kf, kft, vf, vft, qkvb, kbt, vbt, gl, fpre, h1, qft = _inproj_fwd(
        x2d, g_mix, w_pad, bf_pad, place_q, place_k, ones_q, ones_k, seq)
    o_sb, ltot, (g_up, g_out, g_down, g_pg, g_narrow) = _sb_fwd(qkvb, vbt, batch, seq, [
        w_up[0].astype(BF16), w_out[0].astype(BF16), w_down[0].astype(BF16), w_ple_gate[0].astype(BF16), narrow])
    o_fox, lse = _fox_fwd(qf, kf, vft, batch, seq)
    w_up_full = _slabs_to_cols(g_up)
    w_out_full = g_out.reshape(d, d)
    w_down_full = g_down.reshape(d_ff, d)
    w_pg_full = g_pg.reshape(d, d)
    w_bf_full = _slabs_to_cols(g_narrow[:, :D_BRANCH])
    w_bs_full = _slabs_to_cols(g_narrow[:, D_BRANCH:2 * D_BRANCH])
    w_ple_full = _slabs_to_cols(g_narrow[:, 2 * D_BRANCH:2 * D_BRANCH + d_ple])
    bg_terms = g_narrow[:, 2 * D_BRANCH + d_ple:narrow_rows].astype(F32)
    b_gate_full = _slabs_to_cols(bg_terms[:, 0:2] + bg_terms[:, 2:4] + bg_terms[:, 4:6])
    x1 = _mix_fwd(o_fox, o_sb, gl, x2d, w_bf_full, w_bs_full, w_out_full, b_gate_full)

    a_up, dx2, h3, dpre, dpe, loss_acc, dg_ple, dg_final = _mlp_head_fwd_bwd(
        x1, p2d, tgt2d, g_mlp, w_up_full, w_down_full, g_ple, g_final.reshape(1, d), w_pg_full, w_ple_full)
    dx1, da_up, h2t, dg_mlp = _mlp_bwd(dx2, a_up, x1, g_mlp, w_up_full, w_down_full)
    merged, dbr_f, dbr_s, dgl, do_fox, do_sb, dbg, do_fox_t = _mix_bwd(
        dx1, o_fox, o_sb, gl, w_bf_full, w_bs_full, w_out_full, b_gate_full)

    def column_shards(name, lhs, rhs, lhs_t=False):
        if (rhs.shape[-1] // N_DEV) % (4 * LANES) == 0:
            return _matmul_tn(name, lhs, rhs, slabs=True, lhs_t=lhs_t)
        return _cols_to_slabs(_matmul_tn(name, lhs, rhs, lhs_t=lhs_t))

    if fn % (4 * LANES) == 0:
        part_up, = _matmul_tn_once("dw_up", [h2t], da_up, slabs=True, lhs_t=True)
    else:
        part_up = column_shards("dw_up", h2t, da_up, lhs_t=True)
    part_out = _matmul_tn("dw_out", merged, dx1).reshape(N_DEV, dn, d)
    part_down = _matmul_tn_once("dw_down", [a_up], dx2, relu2=True)[0].reshape(N_DEV, fn, d)
    part_pg = _matmul_tn("dw_ple_gate", h3, dpre).reshape(N_DEV, dn, d)
    part_narrow = _pad_rows(jnp.concatenate(
        [column_shards("dw_branch_fox", o_fox, dbr_f), column_shards("dw_branch_sb", o_sb, dbr_s),
         column_shards("dw_ple", p2d, dpe)], axis=1), narrow_rows_pad)
    early = [part_up, part_out, part_down, part_pg, lax.optimization_barrier(part_narrow)]

    dqk_f, dv_f, dc_queries, dc_keys, early_recv = _fox_bwd(
        qf, qft, kf, kft, vf, o_fox, do_fox, do_fox_t, lse, batch, seq, early)
    early_sums = [_pair_add("pair_add_%d" % i, pt, rc, my_c) for i, (pt, rc) in enumerate(zip(early, early_recv))]
    dqkv_b, (s_up, s_out, s_down, s_pg, s_narrow) = _sb_bwd(qkvb, kbt, do_sb, ltot, batch, seq, early_sums)
    dcq_tok = dc_queries.reshape(batch, N_HEADS, seq).transpose(0, 2, 1).reshape(t_len, N_HEADS)
    dck_tok = dc_keys.reshape(batch, N_HEADS, seq).transpose(0, 2, 1).reshape(t_len, N_HEADS)
    lane_pad = ((0, 0), (0, LANES - N_HEADS))
    df, db_forget = _forget_bwd(jnp.pad(dcq_tok, lane_pad), jnp.pad(dck_tok, lane_pad), fpre, batch, seq)

    gw_in = _unpad_dwin(*_matmul_tn_once("dw_in_fox_qk", [dqk_f], h1),
                        *_matmul_tn_once("dw_in_rest", [dv_f, dqkv_b, dgl, df], h1), d)
    part_in = lax.optimization_barrier(gw_in.reshape(N_DEV, -1, d))
    recv_in, = _rs_core_pair("reduce_scatter_core_pair_w_in", [part_in])
    grad_x, dg_mix, (s_in,) = _inproj_bwd(dqk_f, dv_f, dqkv_b, dgl, df, dx1, x2d, g_mix, w_pad,
                                          [_pair_add("pair_add_w_in", part_in, recv_in, my_c)])

    small = jnp.concatenate([
        dg_mix, dg_mlp, dg_ple, dg_final, jnp.pad(db_forget[:, :N_HEADS], ((0, 0), (0, d - N_HEADS))), dbg,
        jnp.pad(loss_acc[:, :1], ((0, 0), (0, d - 1)))], axis=0)
    small = _all_reduce_small(small)
    loss = small[7, 0]
    small_grads = {
        "g_mix": small[0:1], "g_mlp": small[1:2], "g_ple": small[2:3], "g_final": small[3:4],
        "b_forget": small[4:5, :N_HEADS],
        "b_gate": lax.dynamic_slice_in_dim(small[5:7], my_dev * dn, dn, axis=1),
    }

    weights = {"g_mix": g_mix, "w_in": w_in, "b_forget": b_forget, "b_gate": b_gate, "w_branch_fox": w_branch_fox,
               "w_branch_sb": w_branch_sb, "w_out": w_out, "g_mlp": g_mlp, "w_up": w_up, "w_down": w_down,
               "g_ple": g_ple, "w_ple_gate": w_ple_gate, "w_ple": w_ple, "g_final": g_final}
    m_in = {"g_mix": m_g_mix, "w_in": m_w_in, "b_forget": m_b_forget, "b_gate": m_b_gate,
            "w_branch_fox": m_w_branch_fox, "w_branch_sb": m_w_branch_sb, "w_out": m_w_out, "g_mlp": m_g_mlp,
            "w_up": m_w_up, "w_down": m_w_down, "g_ple": m_g_ple, "w_ple_gate": m_w_ple_gate, "w_ple": m_w_ple,
            "g_final": m_g_final}
    v_in = {"g_mix": v_g_mix, "w_in": v_w_in, "b_forget": v_b_forget, "b_gate": v_b_gate,
            "w_branch_fox": v_w_branch_fox, "w_branch_sb": v_w_branch_sb, "w_out": v_w_out, "g_mlp": v_g_mlp,
            "w_up": v_w_up, "w_down": v_w_down, "g_ple": v_g_ple, "w_ple_gate": v_w_ple_gate, "w_ple": v_w_ple,
            "g_final": v_g_final}
    names = list(weights)

    def as2d(a):
        return a.reshape(-1, a.shape[-1])

    result = {}
    big = {"w_up": (s_up, 0), "w_out": (s_out, 0), "w_down": (s_down, 0), "w_ple_gate": (s_pg, 0),
           "w_branch_fox": (s_narrow, 0), "w_branch_sb": (s_narrow, D_BRANCH), "w_ple": (s_narrow, 2 * D_BRANCH)}
    for n, (parts, off) in big.items():
        result[n] = _adamw_parts("adamw_" + n, as2d(weights[n]), parts, off, as2d(m_in[n]), as2d(v_in[n]))
    result["w_in"] = tuple(r.T for r in _adamw_parts("adamw_w_in", w_in[0].T, s_in, 0, m_w_in[0].T, v_w_in[0].T))
    small_names = list(small_grads)
    small_out = _adamw_small([(as2d(weights[n]), small_grads[n], as2d(m_in[n]), as2d(v_in[n])) for n in small_names])
    for n, (dlt, nm, nv) in zip(small_names, small_out):
        result[n] = (small_grads[n], dlt, nm, nv)
    outs = [[result[n][k].reshape(weights[n].shape) for n in names] for k in range(4)]
    return (loss, grad_x.reshape(x.shape), *outs[0], *outs[1], *outs[2], *outs[3])
```

```python
import jax
import jax.numpy as jnp
from jax import lax
from jax.experimental import pallas as pl
from jax.experimental.pallas import tpu as pltpu

F32 = jnp.float32
BF16 = jnp.bfloat16

HEAD_DIM = 64
N_HEADS = 8
D_BRANCH = N_HEADS * HEAD_DIM
EPS = 1e-6
ADAM_LR = 0.001
ADAM_B1 = 0.9
ADAM_B2 = 0.999
ADAM_EPS = 1e-08
ADAM_WD = 0.01
ADAM_STEP = 10

N_DEV = 8
LANES = 128
TM = 256
TQ = 256
TK = 256
NH = 4
HEAD_SLOT = 128
C_TERMS_Q = 64
C_ONES_K = 64
C_TERMS_K = 67
C_ONES_Q = 67
NEG = -1e30
VMEM_LIMIT = 56 * 1024 * 1024
MESH = pl.DeviceIdType.MESH


def _dot(a, b):
    return jnp.dot(a, b, preferred_element_type=F32)


def _dot_nt(a, b):
    return lax.dot_general(a, b, (((1,), (1,)), ((), ())), preferred_element_type=F32)


def _dot_tn(a, b):
    return lax.dot_general(a, b, (((0,), (0,)), ((), ())), preferred_element_type=F32)


def _sigmoid(x):
    return 1.0 / (1.0 + jnp.exp(-x))


def _softplus(x):
    return jnp.maximum(x, 0.0) + jnp.log(1.0 + jnp.exp(-jnp.abs(x)))


def _split2(x):
    hi = x.astype(BF16)
    lo = (x - hi.astype(F32)).astype(BF16)
    return hi, lo


def _split3(x):
    hi = x.astype(BF16)
    r = x - hi.astype(F32)
    mid = r.astype(BF16)
    lo = (r - mid.astype(F32)).astype(BF16)
    return hi, mid, lo


def _tri(n, rel):
    r = lax.broadcasted_iota(jnp.int32, (n, n), 0)
    c = lax.broadcasted_iota(jnp.int32, (n, n), 1)
    return rel(r, c).astype(BF16)


def _rms(x):
    r = lax.rsqrt(jnp.mean(x * x, axis=-1, keepdims=True) + EPS)
    return x * r, r


def _rms_bwd(dh, xn, r, g):
    dxn = dh * g
    dx = r * (dxn - xn * jnp.mean(dxn * xn, axis=-1, keepdims=True))
    return dx, jnp.sum(dh * xn, axis=0, keepdims=True)


def _row_spec(tm, cols):
    return pl.BlockSpec((tm, cols), lambda i: (i, 0))


def _row3_spec(g, tm, cols):
    return pl.BlockSpec((g, tm, cols), lambda i: (0, i, 0))


def _col_spec(rows, tm):
    return pl.BlockSpec((rows, tm), lambda i: (0, i))


def _const_spec(shape):
    nd = len(shape)
    return pl.BlockSpec(shape, lambda i: (0,) * nd, pipeline_mode=pl.Buffered(1))


def _acc_spec(shape):
    nd = len(shape)
    return pl.BlockSpec(shape, lambda i: (0,) * nd)


def _seq_params():
    return pltpu.CompilerParams(dimension_semantics=("arbitrary",), vmem_limit_bytes=VMEM_LIMIT)


def _mesh_pos():
    return lax.axis_index("x"), lax.axis_index("y"), lax.axis_index("c")


def _other_chips(x, y):
    return [(1 - x, y), (x, 1 - y), (1 - x, 1 - y)]


def _hbm_specs(n):
    return [pl.BlockSpec(memory_space=pl.ANY)] * n


def _gather_plan(x_refs, out_refs, send_sems, recv_sems, local_sems):
    n = len(x_refs)
    x, y, c = _mesh_pos()
    me, sibling = (x, y, c), (x, y, 1 - c)
    chips = _other_chips(x, y)

    def index(px, py, pc):
        return 4 * px + 2 * py + pc

    def copy(a, k, block, to, src=None):
        slab = out_refs[a].at[index(*block)]
        return pltpu.make_async_remote_copy(
            src_ref=slab if src is None else src, dst_ref=slab,
            send_sem=send_sems.at[7 * a + k], recv_sem=recv_sems.at[7 * a + k], device_id=to, device_id_type=MESH)

    mine = [pltpu.make_async_copy(x_refs[a], out_refs[a].at[index(*me)], local_sems.at[a]) for a in range(n)]
    first = []
    for a in range(n):
        first.append(copy(a, 0, me, sibling, src=x_refs[a]))
        first += [copy(a, 1 + j, me, (cx, cy, c), src=x_refs[a]) for j, (cx, cy) in enumerate(chips)]

    def start():
        for cp in mine + first:
            cp.start()

    def finish():
        passed = []
        for j, (cx, cy) in enumerate(chips):
            for a in range(n):
                copy(a, 1 + j, (cx, cy, c), me).wait_recv()
                passed.append(copy(a, 4 + j, (cx, cy, c), sibling))
                passed[-1].start()
        for a in range(n):
            copy(a, 0, sibling, me).wait_recv()
            for j, (cx, cy) in enumerate(chips):
                copy(a, 4 + j, (cx, cy, 1 - c), me).wait_recv()
        for cp in first + passed:
            cp.wait_send()
        for cp in mine:
            cp.wait()

    return start, finish


def _gather_shapes(shards):
    return [jax.ShapeDtypeStruct((N_DEV,) + s.shape, s.dtype) for s in shards]


def _gather_sems(n):
    return [pltpu.SemaphoreType.DMA((7 * n,)), pltpu.SemaphoreType.DMA((7 * n,)), pltpu.SemaphoreType.DMA((n,))]


def _all_gather(shards):
    n = len(shards)

    def body(*refs):
        start, finish = _gather_plan(refs[:n], refs[n:2 * n], *refs[2 * n:])
        start()
        finish()

    return pl.pallas_call(
        body, name="all_gather_weights", out_shape=_gather_shapes(shards),
        in_specs=_hbm_specs(n), out_specs=_hbm_specs(n), scratch_shapes=_gather_sems(n),
    )(*shards)


def _pair_plan(p_refs, recv_refs, send_sems, recv_sems):
    n = len(p_refs)
    x, y, c = _mesh_pos()
    sibling = (x, y, 1 - c)

    def start():
        for a in range(n):
            for chip in range(4):
                pltpu.make_async_remote_copy(
                    src_ref=p_refs[a].at[2 * chip + (1 - c)], dst_ref=recv_refs[a].at[chip],
                    send_sem=send_sems.at[a], recv_sem=recv_sems.at[a], device_id=sibling, device_id_type=MESH).start()

    def finish():
        for a in range(n):
            pltpu.make_async_remote_copy(
                src_ref=recv_refs[a], dst_ref=recv_refs[a], send_sem=send_sems.at[a], recv_sem=recv_sems.at[a],
                device_id=sibling, device_id_type=MESH).wait()

    return start, finish


def _pair_shapes(partials):
    return [jax.ShapeDtypeStruct((4,) + s.shape[1:], s.dtype) for s in partials]


def _pair_sems(n):
    return [pltpu.SemaphoreType.DMA((n,)), pltpu.SemaphoreType.DMA((n,))]


def _rs_core_pair(name, partials):
    n = len(partials)

    def body(*refs):
        start, finish = _pair_plan(refs[:n], refs[n:2 * n], *refs[2 * n:])
        start()
        finish()

    return pl.pallas_call(
        body, name=name, out_shape=_pair_shapes(partials),
        in_specs=_hbm_specs(n), out_specs=_hbm_specs(n), scratch_shapes=_pair_sems(n),
    )(*partials)


def _chips_plan(cs_refs, out_refs, send_sems, recv_sems, local_sems):
    n = len(cs_refs)
    x, y, c = _mesh_pos()
    chip = 2 * x + y
    chips = _other_chips(x, y)
    mine = [pltpu.make_async_copy(cs_refs[a].at[chip], out_refs[a].at[chip], local_sems.at[a]) for a in range(n)]
    sends = [pltpu.make_async_remote_copy(
        src_ref=cs_refs[a].at[2 * cx + cy], dst_ref=out_refs[a].at[chip],
        send_sem=send_sems.at[3 * a + j], recv_sem=recv_sems.at[3 * a + j],
        device_id=(cx, cy, c), device_id_type=MESH) for a in range(n) for j, (cx, cy) in enumerate(chips)]

    def start():
        for cp in mine + sends:
            cp.start()

    def finish():
        for a in range(n):
            for j, (cx, cy) in enumerate(chips):
                pltpu.make_async_remote_copy(
                    src_ref=cs_refs[a].at[chip], dst_ref=out_refs[a].at[2 * cx + cy],
                    send_sem=send_sems.at[3 * a + j], recv_sem=recv_sems.at[3 * a + j],
                    device_id=(x, y, c), device_id_type=MESH).wait_recv()
        for cp in sends:
            cp.wait_send()
        for cp in mine:
            cp.wait()

    return start, finish


def _chips_sems(n):
    return [pltpu.SemaphoreType.DMA((3 * n,)), pltpu.SemaphoreType.DMA((3 * n,)), pltpu.SemaphoreType.DMA((n,))]


def _all_reduce_small(vec):
    rows, cols = vec.shape

    def body(x_ref, land_ref, sum_ref, send_sems, recv_sems):
        x, y, c = _mesh_pos()
        me = 4 * x + 2 * y + c
        land_ref[me] = x_ref[...]
        flips = [(fx, fy, fc) for fx in (0, 1) for fy in (0, 1) for fc in (0, 1)][1:]

        def flipped(f):
            return tuple((1 - v) if b else v for v, b in zip((x, y, c), f))

        sends = []
        for k, f in enumerate(flips):
            sends.append(pltpu.make_async_remote_copy(
                src_ref=x_ref, dst_ref=land_ref.at[me], send_sem=send_sems.at[k], recv_sem=recv_sems.at[k],
                device_id=flipped(f), device_id_type=MESH))
            sends[-1].start()
        for k, f in enumerate(flips):
            px, py, pc = flipped(f)
            pltpu.make_async_remote_copy(
                src_ref=x_ref, dst_ref=land_ref.at[4 * px + 2 * py + pc], send_sem=send_sems.at[k],
                recv_sem=recv_sems.at[k], device_id=(x, y, c), device_id_type=MESH).wait_recv()
        for cp in sends:
            cp.wait_send()
        total = land_ref[0]
        for d in range(1, N_DEV):
            total = total + land_ref[d]
        sum_ref[...] = total

    vm = pl.BlockSpec(memory_space=pltpu.VMEM)
    return pl.pallas_call(
        body, name="all_reduce_small",
        out_shape=(jax.ShapeDtypeStruct((N_DEV, rows, cols), F32), jax.ShapeDtypeStruct((rows, cols), F32)),
        in_specs=[vm], out_specs=(vm, vm),
        scratch_shapes=[pltpu.SemaphoreType.DMA((7,)), pltpu.SemaphoreType.DMA((7,))],
    )(vec)[1]


def _block_rows(rows, cols, itemsize, align, row_off=0):
    best = None
    for t in range(align, rows + 1, align):
        if rows % t == 0 and row_off % t == 0 and t * cols * itemsize <= (1 << 20):
            best = t
    return rows if best is None else best


def _pair_add(name, partial, recv, my_c):
    _, rows, cols = partial.shape
    br = _block_rows(rows, cols, 2, 16)

    def body(c_ref, a_ref, b_ref, o_ref):
        o_ref[...] = (a_ref[...].astype(F32) + b_ref[...].astype(F32)).astype(BF16)

    return pl.pallas_call(
        body, name=name,
        grid_spec=pltpu.PrefetchScalarGridSpec(
            num_scalar_prefetch=1, grid=(4, rows // br),
            in_specs=[pl.BlockSpec((None, None, br, cols), lambda j, i, c_ref: (j, c_ref[0], i, 0)),
                      pl.BlockSpec((None, br, cols), lambda j, i, c_ref: (j, i, 0))],
            out_specs=pl.BlockSpec((None, br, cols), lambda j, i, c_ref: (j, i, 0))),
        out_shape=jax.ShapeDtypeStruct((4, rows, cols), BF16),
    )(my_c.reshape(1).astype(jnp.int32), partial.reshape(4, 2, rows, cols), recv)


def _adam_update(w, g, m, v):
    nm = ADAM_B1 * m + (1.0 - ADAM_B1) * g
    nv = ADAM_B2 * v + (1.0 - ADAM_B2) * (g * g)
    m_hat = nm / (1.0 - ADAM_B1 ** ADAM_STEP)
    v_hat = nv / (1.0 - ADAM_B2 ** ADAM_STEP)
    return -ADAM_LR * (m_hat / (jnp.sqrt(v_hat) + ADAM_EPS) + ADAM_WD * w), nm, nv


def _adamw_parts(name, w, parts, row_off, m, v):
    rows, cols = w.shape
    tr = _block_rows(rows, cols, 4, 16, row_off)
    tc = cols
    if tr == rows and rows % 16 != 0 and cols % (2 * LANES) == 0:
        tc = 2 * LANES
    assert rows % tr == 0 and row_off % tr == 0 and (tc == cols or row_off == 0)
    off = row_off // tr

    def body(w_ref, p_ref, m_ref, v_ref, g_ref, d_ref, nm_ref, nv_ref):
        g = p_ref[0].astype(F32)
        for j in range(1, 4):
            g = g + p_ref[j].astype(F32)
        g_ref[...] = g
        d_ref[...], nm_ref[...], nv_ref[...] = _adam_update(w_ref[...], g, m_ref[...], v_ref[...])

    spec = pl.BlockSpec((tr, tc), lambda i, j: (i, j))
    shp = jax.ShapeDtypeStruct((rows, cols), F32)
    return pl.pallas_call(
        body, name=name, grid=(rows // tr, cols // tc), out_shape=(shp,) * 4,
        in_specs=[spec, pl.BlockSpec((4, tr, tc), lambda i, j: (0, off + i, j)), spec, spec], out_specs=(spec,) * 4,
    )(w, parts, m, v)


def _adamw_small(tensors):
    n = len(tensors)

    def body(*refs):
        ins, outs = refs[:4 * n], refs[4 * n:]
        for t in range(n):
            w_ref, g_ref, m_ref, v_ref = ins[4 * t:4 * t + 4]
            d, nm, nv = _adam_update(w_ref[...], g_ref[...], m_ref[...], v_ref[...])
            outs[3 * t][...], outs[3 * t + 1][...], outs[3 * t + 2][...] = d, nm, nv

    vm = pl.BlockSpec(memory_space=pltpu.VMEM)
    out = pl.pallas_call(
        body, name="adamw_small",
        out_shape=[jax.ShapeDtypeStruct(t[0].shape, F32) for t in tensors for _ in range(3)],
        in_specs=[vm] * (4 * n), out_specs=[vm] * (3 * n),
    )(*[a for t in tensors for a in t])
    return [tuple(out[3 * t:3 * t + 3]) for t in range(n)]


def _matmul_tn(name, a, b, relu2=False, slabs=False, lhs_t=False):
    a_groups = a.shape[0] if a.ndim == 3 else 0
    b_groups = b.shape[0] if b.ndim == 3 else 0
    groups = max(a_groups, b_groups, 1)
    assert not (a_groups and b_groups) and not (a_groups and lhs_t)
    a3 = a if a_groups else a[None]
    b3 = b if b_groups else b[None]
    t_len, k_len = a3.shape[1:][::-1] if lhs_t else a3.shape[1:]
    n_len = b3.shape[2]
    tt = min(t_len, 512)
    tk = min(k_len, 1024)
    tn = n_len // N_DEV if slabs else min(n_len, 1024)
    nt = t_len // tt
    assert not slabs or (groups == 1 and tn <= 1024)

    def body(a_ref, b_ref, o_ref, acc_ref):
        @pl.when(pl.program_id(3) == 0)
        def _():
            acc_ref[...] = jnp.zeros_like(acc_ref)

        av = a_ref[...]
        if relu2:
            av = jnp.square(jnp.maximum(av.astype(F32), 0.0))
        product = _dot if lhs_t else _dot_tn
        acc_ref[...] += product(av.astype(BF16), b_ref[...].astype(BF16))

        @pl.when(pl.program_id(3) == nt - 1)
        def _():
            o_ref[...] = acc_ref[...].astype(BF16)

    def a_group(g):
        return g if a_groups else 0

    def b_group(g):
        return g if b_groups else 0

    if slabs:
        out_shape = jax.ShapeDtypeStruct((N_DEV, k_len, tn), BF16)
        out_spec = pl.BlockSpec((None, tk, tn), lambda g, i, j, t: (j, i, 0))
    else:
        out_shape = jax.ShapeDtypeStruct((groups, k_len, n_len), BF16)
        out_spec = pl.BlockSpec((None, tk, tn), lambda g, i, j, t: (g, i, j))
    out = pl.pallas_call(
        body, name=name, grid=(groups, k_len // tk, n_len // tn, nt), out_shape=out_shape,
        in_specs=[pl.BlockSpec((None, tk, tt), lambda g, i, j, t: (a_group(g), i, t)) if lhs_t
                  else pl.BlockSpec((None, tt, tk), lambda g, i, j, t: (a_group(g), t, i)),
                  pl.BlockSpec((None, tt, tn), lambda g, i, j, t: (b_group(g), t, j))],
        out_specs=out_spec,
        scratch_shapes=[pltpu.VMEM((tk, tn), F32)],
        compiler_params=pltpu.CompilerParams(
            dimension_semantics=("parallel", "parallel", "parallel", "arbitrary"), vmem_limit_bytes=VMEM_LIMIT),
    )(a3, b3)
    return out if (slabs or a_groups or b_groups) else out[0]


def _matmul_tn_once(name, lhs_list, rhs, relu2=False, slabs=False, lhs_t=False):
    t_len, n_len = rhs.shape
    tt = min(t_len, 256)
    nt = t_len // tt
    n_lhs = len(lhs_list)
    assert not (lhs_t or slabs) or (n_lhs == 1 and lhs_list[0].ndim == 2)
    k_shapes = [(a.shape[0], n_len) if lhs_t else a.shape[:-2] + (a.shape[-1], n_len) for a in lhs_list]
    tn = n_len // N_DEV

    def body(*refs):
        a_refs, b_ref = refs[:n_lhs], refs[n_lhs]
        o_refs, acc_refs = refs[n_lhs + 1:2 * n_lhs + 1], refs[2 * n_lhs + 1:]
        step = pl.program_id(0)

        @pl.when(step == 0)
        def _():
            for acc in acc_refs:
                acc[...] = jnp.zeros_like(acc)

        bv = b_ref[...].astype(BF16)

        def piece(av):
            if relu2:
                av = jnp.square(jnp.maximum(av.astype(F32), 0.0))
            return (_dot if lhs_t else _dot_tn)(av.astype(BF16), bv)

        for a_ref, acc in zip(a_refs, acc_refs):
            if len(acc.shape) == 3:
                for g in range(acc.shape[0]):
                    acc[g] += piece(a_ref[g])
            else:
                acc[...] += piece(a_ref[...])

        @pl.when(step == nt - 1)
        def _():
            for o_ref, acc in zip(o_refs, acc_refs):
                if slabs:
                    for j in range(N_DEV):
                        o_ref[j] = acc[:, j * tn:(j + 1) * tn].astype(BF16)
                else:
                    o_ref[...] = acc[...].astype(BF16)

    def lhs_spec(a):
        if lhs_t:
            return pl.BlockSpec((a.shape[0], tt), lambda t: (0, t))
        if a.ndim == 3:
            return pl.BlockSpec((a.shape[0], tt, a.shape[2]), lambda t: (0, t, 0))
        return pl.BlockSpec((tt, a.shape[1]), lambda t: (t, 0))

    out_shapes = [(N_DEV, k_shapes[0][0], tn)] if slabs else k_shapes
    return pl.pallas_call(
        body, name=name, grid=(nt,),
        out_shape=[jax.ShapeDtypeStruct(s, BF16) for s in out_shapes],
        in_specs=[lhs_spec(a) for a in lhs_list] + [pl.BlockSpec((tt, n_len), lambda t: (t, 0))],
        out_specs=[_acc_spec(s) for s in out_shapes],
        scratch_shapes=[pltpu.VMEM(s, F32) for s in k_shapes],
        compiler_params=_seq_params(),
    )(*lhs_list, rhs)


def _pad_layout(d):
    names = ("qf", "kf", "vf", "qb", "kb", "vb", "gates", "forget")
    sizes = (N_HEADS * HEAD_SLOT, N_HEADS * HEAD_SLOT, D_BRANCH, D_BRANCH, D_BRANCH, D_BRANCH, 2 * d, LANES)
    out, off = {}, 0
    for n, s in zip(names, sizes):
        out[n] = (off, off + s)
        off += s
    return out, off


def _slot_rows(xt, extra):
    parts = []
    for h in range(N_HEADS):
        parts += [xt[h * HEAD_DIM:(h + 1) * HEAD_DIM, :], extra]
    return jnp.concatenate(parts, axis=0)


def _inproj_fwd(x, g_mix, w_pad, bf_pad, place_q, place_k, ones_q, ones_k, seq):
    t_len, d = x.shape
    lay, _ = _pad_layout(d)
    tiles_per_seq = seq // TM
    slot_w = N_HEADS * HEAD_SLOT

    def body(x_ref, g_ref, w_ref, bf_ref, pq_ref, pk_ref, oq_ref, ok_ref,
             qf_ref, kf_ref, kft_ref, vf_ref, vft_ref, qkvb_ref, kbt_ref, vbt_ref, gl_ref, fpre_ref, h_ref, qft_ref,
             carry_ref):
        @pl.when(pl.program_id(0) % tiles_per_seq == 0)
        def _():
            carry_ref[...] = jnp.zeros_like(carry_ref)

        def proj(name):
            lo, hi = lay[name]
            return _dot_nt(h, w_ref[lo:hi, :])

        xn, _ = _rms(x_ref[...])
        h = (xn * g_ref[...]).astype(BF16)
        fpre = proj("forget") + bf_ref[...]
        fpre_ref[...] = fpre
        logf = -_softplus(-fpre)
        lower = _tri(TM, lambda r, c: c <= r)
        hi, mid, lo = _split3(logf)
        c_val = carry_ref[...] + _dot(lower, hi) + _dot(lower, mid) + _dot(lower, lo)
        carry_ref[...] = carry_ref[...] + jnp.sum(logf, axis=0, keepdims=True)
        head_lanes = lax.broadcasted_iota(jnp.int32, (TM, LANES), 1) < N_HEADS
        terms = [jnp.where(head_lanes, t.astype(F32), 0.0) for t in _split3(c_val)]
        c_packed = (terms[0] + pltpu.roll(terms[1], N_HEADS, 1) + pltpu.roll(terms[2], 2 * N_HEADS, 1)).astype(BF16)
        qf = proj("qf") + _dot(c_packed, pq_ref[...]) + oq_ref[...]
        qf_ref[...] = qf.astype(BF16)
        qft_ref[0] = qf.T.astype(BF16)
        kf = proj("kf") - _dot(c_packed, pk_ref[...]) + ok_ref[...]
        kf_ref[...] = kf.astype(BF16)
        kft_ref[0] = kf.T.astype(BF16)
        row0 = (lax.broadcasted_iota(jnp.int32, (HEAD_DIM, TM), 0) == 0).astype(F32)
        zeros = jnp.zeros((HEAD_DIM, TM), F32)
        vf = proj("vf")
        vf_ref[...] = vf.astype(BF16)
        vft_ref[0] = _slot_rows(vf.T, row0).astype(BF16)
        qkvb_ref[0] = proj("qb").astype(BF16)
        kb = proj("kb")
        qkvb_ref[1] = kb.astype(BF16)
        kbt_ref[0] = _slot_rows(kb.T, zeros).astype(BF16)
        vb = proj("vb")
        qkvb_ref[2] = vb.astype(BF16)
        vbt_ref[0] = _slot_rows(vb.T, row0).astype(BF16)
        gl_ref[...] = proj("gates").astype(BF16)
        h_ref[...] = h

    n_tiles = t_len // TM
    slot_shape = jax.ShapeDtypeStruct((t_len, slot_w), BF16)
    t_shape = jax.ShapeDtypeStruct((n_tiles, slot_w, TM), BF16)
    t_spec = pl.BlockSpec((1, slot_w, TM), lambda i: (i, 0, 0))
    return pl.pallas_call(
        body, name="inproj_fwd", grid=(n_tiles,),
        out_shape=(slot_shape, slot_shape, t_shape, jax.ShapeDtypeStruct((t_len, D_BRANCH), BF16), t_shape,
                   jax.ShapeDtypeStruct((3, t_len, D_BRANCH), BF16), t_shape, t_shape,
                   jax.ShapeDtypeStruct((t_len, 2 * d), BF16), jax.ShapeDtypeStruct((t_len, LANES), F32),
                   jax.ShapeDtypeStruct((t_len, d), BF16), t_shape),
        in_specs=[_row_spec(TM, d), _const_spec((1, d)), _const_spec(w_pad.shape), _const_spec((1, LANES)),
                  _const_spec(place_q.shape), _const_spec(place_k.shape), _const_spec((1, slot_w)),
                  _const_spec((1, slot_w))],
        out_specs=(_row_spec(TM, slot_w), _row_spec(TM, slot_w), t_spec, _row_spec(TM, D_BRANCH), t_spec,
                   _row3_spec(3, TM, D_BRANCH), t_spec, t_spec, _row_spec(TM, 2 * d), _row_spec(TM, LANES),
                   _row_spec(TM, d), t_spec),
        scratch_shapes=[pltpu.VMEM((1, LANES), F32)],
        compiler_params=_seq_params(),
    )(x, g_mix, w_pad, bf_pad, place_q, place_k, ones_q, ones_k)


def _slot_spec(seq):
    return pl.BlockSpec((seq, NH * HEAD_SLOT), lambda b, g: (b, g))


def _group2_spec(seq):
    return pl.BlockSpec((2, seq, NH * HEAD_DIM), lambda b, g: (0, b, g))


def _group_spec(seq):
    return pl.BlockSpec((seq, NH * HEAD_DIM), lambda b, g: (b, g))


def _group3_spec(which, seq):
    return pl.BlockSpec((None, seq, NH * HEAD_DIM), lambda b, g: (which, b, g))


def _tblock_spec(seq):
    return pl.BlockSpec((seq // TK, NH * HEAD_SLOT, TK), lambda b, g: (b, g, 0))


def _qrow_spec(seq):
    return pl.BlockSpec((None, NH, seq // TQ, TQ), lambda b, g: (b, g, 0, 0))


def _attn_params():
    return pltpu.CompilerParams(dimension_semantics=("parallel", "parallel"), vmem_limit_bytes=VMEM_LIMIT)


def _serial_attn_params():
    return pltpu.CompilerParams(dimension_semantics=("arbitrary", "arbitrary"), vmem_limit_bytes=VMEM_LIMIT)


def _hcols(hh):
    return slice(hh * HEAD_DIM, (hh + 1) * HEAD_DIM)


def _hslot(hh):
    return slice(hh * HEAD_SLOT, (hh + 1) * HEAD_SLOT)


def _key_query_mask(rel):
    r = lax.broadcasted_iota(jnp.int32, (TK, TQ), 0)
    c = lax.broadcasted_iota(jnp.int32, (TK, TQ), 1)
    return rel(r, c)


def _heads_cat(vals):
    return jnp.concatenate(vals, axis=1)


def _untranspose(acc_t):
    return acc_t.T[:, :HEAD_DIM]


def _fox_fwd(qf, kf, vft, batch, seq):
    def body(q_ref, k_ref, vt_ref, o_ref, lse_ref, m_s, acc_s):
        causal = _key_query_mask(lambda r, c: r <= c)

        def tile(q0, kj, masked, n_k=1):
            krows = pl.ds(pl.multiple_of(kj * TK, TK), n_k * TK)
            heads = range(NH)
            sts = [_dot_nt(k_ref[krows, _hslot(hh)], q_ref[pl.ds(q0, TQ), _hslot(hh)]) for hh in heads]
            if masked:
                sts = [jnp.where(causal, st, NEG) for st in sts]
            m_olds = [m_s[hh] for hh in heads]
            m_news = [jnp.maximum(m_olds[hh], jnp.max(sts[hh], axis=0, keepdims=True)) for hh in heads]
            pts = [jnp.exp(sts[hh] - m_news[hh]).astype(BF16) for hh in heads]
            pvs = [sum(_dot(vt_ref[kj + i, _hslot(hh), :], pts[hh][i * TK:(i + 1) * TK]) for i in range(n_k))
                   for hh in heads]
            for hh in heads:
                acc_s[hh] = jnp.exp(m_olds[hh] - m_news[hh]) * acc_s[hh] + pvs[hh]
                m_s[hh] = m_news[hh]

        def q_loop(qi, _):
            q0 = pl.multiple_of(qi * TQ, TQ)
            m_s[...] = jnp.full(m_s.shape, NEG, F32)
            acc_s[...] = jnp.zeros_like(acc_s)

            def pair_loop(i, _):
                tile(q0, 2 * i, False, n_k=2)
                return 0

            lax.fori_loop(0, qi // 2, pair_loop, 0)
            pl.when(qi % 2 == 1)(lambda: tile(q0, qi - 1, False))
            tile(q0, qi, True)
            outs = []
            for hh in range(NH):
                total = acc_s[hh, HEAD_DIM:HEAD_DIM + 1, :]
                outs.append(_untranspose(acc_s[hh] / total))
                lse_ref[hh, pl.ds(qi, 1), :] = m_s[hh] + jnp.log(total)
            o_ref[pl.ds(q0, TQ), :] = _heads_cat(outs).astype(BF16)
            return 0

        lax.fori_loop(0, seq // TQ, q_loop, 0)

    return pl.pallas_call(
        body, name="fox_fwd", grid=(batch, N_HEADS // NH),
        out_shape=(jax.ShapeDtypeStruct((batch * seq, D_BRANCH), BF16),
                   jax.ShapeDtypeStruct((batch, N_HEADS, seq // TQ, TQ), F32)),
        in_specs=[_slot_spec(seq), _slot_spec(seq), _tblock_spec(seq)],
        out_specs=(_group_spec(seq), _qrow_spec(seq)),
        scratch_shapes=[pltpu.VMEM((NH, 1, TQ), F32), pltpu.VMEM((NH, HEAD_SLOT, TQ), F32)],
        compiler_params=_attn_params(),
    )(qf, kf, vft)


def _fox_bwd(qf, qft, kf, kft, vf, o, do, dot, lse, batch, seq, partials):
    n_q = seq // TQ
    n = len(partials)

    def body(q_ref, qt_ref, k_ref, kt_ref, v_ref, o_ref, do_ref, dot_ref, lse_ref, *rest):
        p_refs, (dqk_ref, dv_ref, dcq_ref, dck_ref), recv_refs = rest[:n], rest[n:n + 4], rest[n + 4:2 * n + 4]
        delta_s, dqt_acc, dk_s, dv_s = rest[2 * n + 4:2 * n + 8]
        pair_start, pair_finish = _pair_plan(p_refs, recv_refs, *rest[2 * n + 8:])
        first_step, last_step = _first_last_step()
        pl.when(first_step)(pair_start)
        causal = _key_query_mask(lambda r, c: r <= c)
        ones8 = jnp.ones((8, HEAD_DIM), BF16)
        dqt_acc[...] = jnp.zeros_like(dqt_acc)

        def prep(qi, _):
            rows = pl.ds(pl.multiple_of(qi * TQ, TQ), TQ)
            for hh in range(NH):
                hi, lo = _split2(do_ref[rows, _hcols(hh)].astype(F32) * o_ref[rows, _hcols(hh)].astype(F32))
                delta_s[hh, pl.ds(qi, 1), :] = (_dot_nt(ones8, hi) + _dot_nt(ones8, lo))[0:1, :]
            return 0

        lax.fori_loop(0, n_q, prep, 0)

        def tile(qis, kj, masked):
            krows = pl.ds(pl.multiple_of(kj * TK, TK), TK)
            heads = range(NH)
            items = [(t, hh) for t in range(len(qis)) for hh in heads]
            rows = [pl.ds(qi * TQ if isinstance(qi, int) else pl.multiple_of(qi * TQ, TQ), TQ) for qi in qis]
            sts = [_dot_nt(k_ref[krows, _hslot(hh)], q_ref[rows[t], _hslot(hh)]) for t, hh in items]
            dps = [_dot_nt(v_ref[krows, _hcols(hh)], do_ref[rows[t], _hcols(hh)]) for t, hh in items]
            pts = [jnp.exp(sts[i] - lse_ref[hh, pl.ds(qis[t], 1), :]) for i, (t, hh) in enumerate(items)]
            if masked:
                pts = [jnp.where(causal, pt, 0.0) for pt in pts]
            dsts = [(pts[i] * (dps[i] - delta_s[hh, pl.ds(qis[t], 1), :])).astype(BF16)
                    for i, (t, hh) in enumerate(items)]
            for i, (t, hh) in enumerate(items):
                dv_s[hh] += _dot_nt(dot_ref[qis[t], _hslot(hh), :], pts[i].astype(BF16))
                dk_s[hh] += _dot_nt(qt_ref[qis[t], _hslot(hh), :], dsts[i])
                dqt_acc[hh, qis[t]] += _dot(kt_ref[kj, _hslot(hh), :], dsts[i])

        def k_loop(kj, _):
            krows = pl.ds(pl.multiple_of(kj * TK, TK), TK)
            dk_s[...] = jnp.zeros_like(dk_s)
            dv_s[...] = jnp.zeros_like(dv_s)
            tile([kj], kj, True)
            left = n_q - 1 - kj

            def pair_loop(i, _):
                tile([kj + 1 + 2 * i, kj + 2 + 2 * i], kj, False)
                return 0

            lax.fori_loop(0, left // 2, pair_loop, 0)
            pl.when(left % 2 == 1)(lambda: tile([n_q - 1], kj, False))
            dqk_ref[1, krows, :] = _heads_cat([_untranspose(dk_s[hh]) for hh in range(NH)]).astype(BF16)
            dv_ref[krows, :] = _heads_cat([_untranspose(dv_s[hh]) for hh in range(NH)]).astype(BF16)
            for hh in range(NH):
                dck_ref[hh, pl.ds(kj, 1), :] = dk_s[hh, C_ONES_Q:C_ONES_Q + 1, :]
            return 0

        lax.fori_loop(0, seq // TK, k_loop, 0)

        def finish(qi, _):
            rows = pl.ds(pl.multiple_of(qi * TQ, TQ), TQ)
            dqk_ref[0, rows, :] = _heads_cat([_untranspose(dqt_acc[hh, qi]) for hh in range(NH)]).astype(BF16)
            for hh in range(NH):
                dcq_ref[hh, pl.ds(qi, 1), :] = dqt_acc[hh, qi, C_ONES_K:C_ONES_K + 1, :]
            return 0

        lax.fori_loop(0, n_q, finish, 0)
        pl.when(last_step)(pair_finish)

    out = pl.pallas_call(
        body, name="fox_bwd", grid=(batch, N_HEADS // NH),
        out_shape=[jax.ShapeDtypeStruct((2, batch * seq, D_BRANCH), BF16),
                   jax.ShapeDtypeStruct((batch * seq, D_BRANCH), BF16),
                   jax.ShapeDtypeStruct((batch, N_HEADS, seq // TQ, TQ), F32),
                   jax.ShapeDtypeStruct((batch, N_HEADS, seq // TK, TK), F32)] + _pair_shapes(partials),
        in_specs=[_slot_spec(seq), _tblock_spec(seq), _slot_spec(seq), _tblock_spec(seq), _group_spec(seq),
                  _group_spec(seq), _group_spec(seq), _tblock_spec(seq), _qrow_spec(seq)] + _hbm_specs(n),
        out_specs=[_group2_spec(seq), _group_spec(seq), _qrow_spec(seq), _qrow_spec(seq)] + _hbm_specs(n),
        scratch_shapes=[pltpu.VMEM((NH, n_q, TQ), F32), pltpu.VMEM((NH, n_q, HEAD_SLOT, TQ), F32),
                        pltpu.VMEM((NH, HEAD_SLOT, TK), F32), pltpu.VMEM((NH, HEAD_SLOT, TK), F32)] + _pair_sems(n),
        compiler_params=_serial_attn_params(),
    )(qf, qft, kf, kft, vf, o, do, dot, lse, *partials)
    return out[0], out[1], out[2], out[3], out[4:]


def _first_last_step():
    step = pl.program_id(0) * pl.num_programs(1) + pl.program_id(1)
    return step == 0, step == pl.num_programs(0) * pl.num_programs(1) - 1


def _sb_fwd(qkvb, vbt, batch, seq, shards):
    n = len(shards)

    def body(q_ref, k_ref, vt_ref, *rest):
        x_refs, (o_ref, lt_ref), out_refs = rest[:n], rest[n:n + 2], rest[n + 2:2 * n + 2]
        run_s, acc_s = rest[2 * n + 2:2 * n + 4]
        gather_start, gather_finish = _gather_plan(x_refs, out_refs, *rest[2 * n + 4:])
        first_step, last_step = _first_last_step()
        pl.when(first_step)(gather_start)
        strict = _key_query_mask(lambda r, c: r < c)
        later = _tri(TK, lambda r, c: c > r)

        def tile(q0, kjs, masked):
            heads = range(NH)
            items = [(t, hh) for t in range(len(kjs)) for hh in heads]
            krows = [pl.ds(kj * TK if isinstance(kj, int) else pl.multiple_of(kj * TK, TK), TK) for kj in kjs]
            zts = [_dot_nt(k_ref[krows[t], _hcols(hh)], q_ref[pl.ds(q0, TQ), _hcols(hh)]) for t, hh in items]
            lgs = [-_softplus(zt) for zt in zts]
            if masked:
                lgs = [jnp.where(strict, lg, 0.0) for lg in lgs]
            parts = [_split2(lg) for lg in lgs]
            sufs = [_dot(later, hi) + _dot(later, lo) for hi, lo in parts]
            sums = [jnp.sum(lg, axis=0, keepdims=True) for lg in lgs]
            runs = {}
            for hh in heads:
                run = run_s[hh]
                for t in range(len(kjs)):
                    runs[t, hh] = run
                    run = run + sums[t * NH + hh]
                run_s[hh] = run
            ats = [jnp.exp(zts[i] + lgs[i] + runs[item] + sufs[i]) for i, item in enumerate(items)]
            if masked:
                ats = [jnp.where(strict, at, 0.0) for at in ats]
            for hh in heads:
                acc_s[hh] += sum(_dot(vt_ref[kjs[t], _hslot(hh), :], ats[t * NH + hh].astype(BF16))
                                 for t in range(len(kjs)))

        def q_loop(qi, _):
            q0 = pl.multiple_of(qi * TQ, TQ)
            run_s[...] = jnp.zeros_like(run_s)
            acc_s[...] = jnp.zeros_like(acc_s)
            tile(q0, [qi], True)

            def pair_loop(i, _):
                tile(q0, [qi - 1 - 2 * i, qi - 2 - 2 * i], False)
                return 0

            lax.fori_loop(0, qi // 2, pair_loop, 0)
            pl.when(qi % 2 == 1)(lambda: tile(q0, [0], False))
            o_ref[pl.ds(q0, TQ), :] = _heads_cat([_untranspose(acc_s[hh]) for hh in range(NH)]).astype(BF16)
            for hh in range(NH):
                lt_ref[hh, pl.ds(qi, 1), :] = run_s[hh]
            return 0

        lax.fori_loop(0, seq // TQ, q_loop, 0)
        pl.when(last_step)(gather_finish)

    out = pl.pallas_call(
        body, name="sb_fwd", grid=(batch, N_HEADS // NH),
        out_shape=[jax.ShapeDtypeStruct((batch * seq, D_BRANCH), BF16),
                   jax.ShapeDtypeStruct((batch, N_HEADS, seq // TQ, TQ), F32)] + _gather_shapes(shards),
        in_specs=[_group3_spec(0, seq), _group3_spec(1, seq), _tblock_spec(seq)] + _hbm_specs(n),
        out_specs=[_group_spec(seq), _qrow_spec(seq)] + _hbm_specs(n),
        scratch_shapes=[pltpu.VMEM((NH, 1, TQ), F32), pltpu.VMEM((NH, HEAD_SLOT, TQ), F32)] + _gather_sems(n),
        compiler_params=_serial_attn_params(),
    )(qkvb, qkvb, vbt, *shards)
    return out[0], out[1], out[2:]


def _sb_bwd(qkvb, kbt, do, ltot, batch, seq, chip_sums):
    n = len(chip_sums)

    def body(q_ref, k_ref, v_ref, kt_ref, do_ref, lt_ref, *rest):
        cs_refs, dqkv_ref, out_refs = rest[:n], rest[n], rest[n + 1:2 * n + 1]
        dk_acc, dv_acc, ls_s, gs_s, dqt_s = rest[2 * n + 1:2 * n + 6]
        chips_start, chips_finish = _chips_plan(cs_refs, out_refs, *rest[2 * n + 6:])
        first_step, last_step = _first_last_step()
        pl.when(first_step)(chips_start)
        strict = _key_query_mask(lambda r, c: r < c)
        upto = _tri(TK, lambda r, c: c <= r)
        before = _tri(TK, lambda r, c: c < r)
        dk_acc[...] = jnp.zeros_like(dk_acc)
        dv_acc[...] = jnp.zeros_like(dv_acc)

        def tile(qi, kj, masked):
            rows = pl.ds(pl.multiple_of(qi * TQ, TQ), TQ)
            krows = pl.ds(pl.multiple_of(kj * TK, TK), TK)
            heads = range(NH)
            qs = [q_ref[rows, _hcols(hh)] for hh in heads]
            douts = [do_ref[rows, _hcols(hh)] for hh in heads]
            zts = [_dot_nt(k_ref[krows, _hcols(hh)], qs[hh]) for hh in heads]
            das = [_dot_nt(v_ref[krows, _hcols(hh)], douts[hh]) for hh in heads]
            lgs = [-_softplus(zt) for zt in zts]
            if masked:
                lgs = [jnp.where(strict, lg, 0.0) for lg in lgs]
            parts = [_split2(lg) for lg in lgs]
            prefs = [_dot(upto, hi) + _dot(upto, lo) for hi, lo in parts]
            ats = [jnp.exp(zts[hh] + lgs[hh] + (lt_ref[hh, pl.ds(qi, 1), :] - ls_s[hh]) - prefs[hh]) for hh in heads]
            if masked:
                ats = [jnp.where(strict, at, 0.0) for at in ats]
            gts = [das[hh] * ats[hh] for hh in heads]
            us = [gs_s[hh] + _dot(before, gts[hh].astype(BF16)) for hh in heads]
            dzts = [(jnp.exp(lgs[hh]) * (gts[hh] + us[hh]) - us[hh]).astype(BF16) for hh in heads]
            for hh in heads:
                dk_acc[hh, krows, :] += _dot(dzts[hh], qs[hh])
                dv_acc[hh, krows, :] += _dot(ats[hh].astype(BF16), douts[hh])
                dqt_s[hh] += _dot(kt_ref[kj, _hslot(hh), :], dzts[hh])
                ls_s[hh] += jnp.sum(lgs[hh], axis=0, keepdims=True)
                gs_s[hh] += jnp.sum(gts[hh], axis=0, keepdims=True)

        def q_loop(qi, _):
            ls_s[...] = jnp.zeros_like(ls_s)
            gs_s[...] = jnp.zeros_like(gs_s)
            dqt_s[...] = jnp.zeros_like(dqt_s)

            def k_loop(kj, _):
                tile(qi, kj, False)
                return 0

            lax.fori_loop(0, qi, k_loop, 0)
            tile(qi, qi, True)
            dqkv_ref[0, pl.ds(pl.multiple_of(qi * TQ, TQ), TQ), :] = _heads_cat(
                [_untranspose(dqt_s[hh]) for hh in range(NH)]).astype(BF16)
            return 0

        lax.fori_loop(0, seq // TQ, q_loop, 0)
        dqkv_ref[1] = _heads_cat([dk_acc[hh] for hh in range(NH)]).astype(BF16)
        dqkv_ref[2] = _heads_cat([dv_acc[hh] for hh in range(NH)]).astype(BF16)
        pl.when(last_step)(chips_finish)

    out = pl.pallas_call(
        body, name="sb_bwd", grid=(batch, N_HEADS // NH),
        out_shape=[jax.ShapeDtypeStruct((3, batch * seq, D_BRANCH), BF16)]
        + [jax.ShapeDtypeStruct(s.shape, s.dtype) for s in chip_sums],
        in_specs=[_group3_spec(0, seq), _group3_spec(1, seq), _group3_spec(2, seq), _tblock_spec(seq),
                  _group_spec(seq), _qrow_spec(seq)] + _hbm_specs(n),
        out_specs=[pl.BlockSpec((3, seq, NH * HEAD_DIM), lambda b, g: (0, b, g))] + _hbm_specs(n),
        scratch_shapes=[pltpu.VMEM((NH, seq, HEAD_DIM), F32), pltpu.VMEM((NH, seq, HEAD_DIM), F32),
                        pltpu.VMEM((NH, 1, TQ), F32), pltpu.VMEM((NH, 1, TQ), F32),
                        pltpu.VMEM((NH, HEAD_SLOT, TQ), F32)] + _chips_sems(n),
        compiler_params=_serial_attn_params(),
    )(qkvb, qkvb, qkvb, kbt, do, ltot, *chip_sums)
    return out[0], out[1:]


def _forget_bwd(dcq_tok, dck_tok, fpre, batch, seq):
    t_len = batch * seq
    tiles = seq // TM

    def rev(i):
        return ((i // tiles) * tiles + (tiles - 1 - i % tiles), 0)

    def body(dcq_ref, dck_ref, f_ref, df_ref, db_ref, carry_ref):
        i = pl.program_id(0)

        @pl.when(i == 0)
        def _():
            db_ref[...] = jnp.zeros_like(db_ref)

        @pl.when(i % tiles == 0)
        def _():
            carry_ref[...] = jnp.zeros_like(carry_ref)

        dc = dcq_ref[...] - dck_ref[...]
        upper = _tri(TM, lambda r, c: c >= r)
        hi, mid, lo = _split3(dc)
        dlogf = carry_ref[...] + _dot(upper, hi) + _dot(upper, mid) + _dot(upper, lo)
        carry_ref[...] = carry_ref[...] + jnp.sum(dc, axis=0, keepdims=True)
        df = dlogf * _sigmoid(-f_ref[...])
        df_ref[...] = df.astype(BF16)
        db_ref[...] += jnp.sum(df, axis=0, keepdims=True)

    return pl.pallas_call(
        body, name="forget_bwd", grid=(t_len // TM,),
        out_shape=(jax.ShapeDtypeStruct((t_len, LANES), BF16), jax.ShapeDtypeStruct((1, LANES), F32)),
        in_specs=[pl.BlockSpec((TM, LANES), rev)] * 3,
        out_specs=(pl.BlockSpec((TM, LANES), rev), _acc_spec((1, LANES))),
        scratch_shapes=[pltpu.VMEM((1, LANES), F32)],
        compiler_params=_seq_params(),
    )(dcq_tok, dck_tok, fpre)


def _mix_fwd(o_fox, o_sb, gl, x, w_bf, w_bs, w_out, b_gate):
    t_len, d = x.shape

    def body(of_ref, os_ref, gl_ref, x_ref, wbf_ref, wbs_ref, wo_ref, bg_ref, x1_ref):
        br_f = _dot(of_ref[...], wbf_ref[...])
        br_s = _dot(os_ref[...], wbs_ref[...])
        ga = _sigmoid(gl_ref[:, :d].astype(F32) + bg_ref[0:1, :])
        gb = _sigmoid(gl_ref[:, d:].astype(F32) + bg_ref[1:2, :])
        merged = ga * br_f + gb * br_s
        x1_ref[...] = x_ref[...] + _dot(merged.astype(BF16), wo_ref[...])

    return pl.pallas_call(
        body, name="mix_fwd", grid=(t_len // TM,),
        out_shape=jax.ShapeDtypeStruct((t_len, d), F32),
        in_specs=[_row_spec(TM, D_BRANCH), _row_spec(TM, D_BRANCH), _row_spec(TM, 2 * d), _row_spec(TM, d),
                  _const_spec(w_bf.shape), _const_spec(w_bs.shape), _const_spec(w_out.shape), _const_spec(b_gate.shape)],
        out_specs=_row_spec(TM, d),
        compiler_params=_seq_params(),
    )(o_fox, o_sb, gl, x, w_bf, w_bs, w_out, b_gate)


def _ff_chunk(d_ff):
    return min(d_ff, 1024)


def _mlp_head_fwd_bwd(x1, p, target, g_mlp, w_up, w_down, g_ple, g_final, w_pg, w_ple):
    t_len, d = x1.shape
    d_ple = p.shape[1]
    d_ff = w_up.shape[1]
    ch = _ff_chunk(d_ff)

    def body(x1_ref, p_ref, t_ref, gm_ref, wu_ref, wd_ref, gp_ref, gf_ref, wpg_ref, wple_ref,
             a_ref, dx2_ref, h3_ref, dpre_ref, dpe_ref, loss_ref, dgp_ref, dgf_ref):
        @pl.when(pl.program_id(0) == 0)
        def _():
            loss_ref[...] = jnp.zeros_like(loss_ref)
            dgp_ref[...] = jnp.zeros_like(dgp_ref)
            dgf_ref[...] = jnp.zeros_like(dgf_ref)

        x1v = x1_ref[...]
        x1n, _ = _rms(x1v)
        h2 = (x1n * gm_ref[...]).astype(BF16)
        x2v = x1v
        for j in range(d_ff // ch):
            a = _dot(h2, wu_ref[:, j * ch:(j + 1) * ch])
            a_ref[:, j * ch:(j + 1) * ch] = a.astype(BF16)
            x2v = x2v + _dot(jnp.square(jnp.maximum(a, 0.0)).astype(BF16), wd_ref[j * ch:(j + 1) * ch, :])
        x2n, r3 = _rms(x2v)
        h3 = (x2n * gp_ref[...]).astype(BF16)
        h3_ref[...] = h3
        gate = _sigmoid(_dot(h3, wpg_ref[...]))
        pe = _dot(p_ref[...].astype(BF16), wple_ref[...])
        x3n, r4 = _rms(x2v + gate * pe)
        err = x3n * gf_ref[...] - t_ref[...]
        loss_ref[...] += jnp.full(loss_ref.shape, (0.5 / d) * jnp.sum(err * err), F32)
        dx3, dgf = _rms_bwd(err * (1.0 / d), x3n, r4, gf_ref[...])
        dgf_ref[...] += dgf
        dpe_ref[...] = (dx3 * gate).astype(BF16)
        dpre = (dx3 * pe * gate * (1.0 - gate)).astype(BF16)
        dpre_ref[...] = dpre
        dres, dgp = _rms_bwd(_dot_nt(dpre, wpg_ref[...]), x2n, r3, gp_ref[...])
        dgp_ref[...] += dgp
        dx2_ref[...] = dx3 + dres

    shp_b = jax.ShapeDtypeStruct((t_len, d), BF16)
    return pl.pallas_call(
        body, name="mlp_head_fwd_bwd", grid=(t_len // TM,),
        out_shape=(jax.ShapeDtypeStruct((t_len, d_ff), BF16), jax.ShapeDtypeStruct((t_len, d), F32), shp_b, shp_b, shp_b,
                   jax.ShapeDtypeStruct((1, LANES), F32), jax.ShapeDtypeStruct((1, d), F32),
                   jax.ShapeDtypeStruct((1, d), F32)),
        in_specs=[_row_spec(TM, d), _row_spec(TM, d_ple), _row_spec(TM, d), _const_spec((1, d)),
                  _const_spec(w_up.shape), _const_spec(w_down.shape), _const_spec((1, d)), _const_spec((1, d)),
                  _const_spec(w_pg.shape), _const_spec(w_ple.shape)],
        out_specs=(_row_spec(TM, d_ff), _row_spec(TM, d), _row_spec(TM, d), _row_spec(TM, d), _row_spec(TM, d),
                   _acc_spec((1, LANES)), _acc_spec((1, d)), _acc_spec((1, d))),
        compiler_params=_seq_params(),
    )(x1, p, target, g_mlp, w_up, w_down, g_ple, g_final, w_pg, w_ple)


def _mlp_bwd(dx2, a, x1, g_mlp, w_up, w_down):
    t_len, d = x1.shape
    d_ff = w_up.shape[1]
    ch = _ff_chunk(d_ff)

    def body(dx2_ref, a_ref, x1_ref, g_ref, wu_ref, wd_ref, dx1_ref, da_ref, h2_ref, dg_ref):
        @pl.when(pl.program_id(0) == 0)
        def _():
            dg_ref[...] = jnp.zeros_like(dg_ref)

        dx2v = dx2_ref[...]
        dx2b = dx2v.astype(BF16)
        xn, r = _rms(x1_ref[...])
        h2_ref[...] = (xn * g_ref[...]).T.astype(BF16)
        dh = jnp.zeros((TM, d), F32)
        for j in range(d_ff // ch):
            dact = _dot_nt(dx2b, wd_ref[j * ch:(j + 1) * ch, :])
            da = (dact * 2.0 * jnp.maximum(a_ref[:, j * ch:(j + 1) * ch].astype(F32), 0.0)).astype(BF16)
            da_ref[:, j * ch:(j + 1) * ch] = da
            dh = dh + _dot_nt(da, wu_ref[:, j * ch:(j + 1) * ch])
        dres, dg = _rms_bwd(dh, xn, r, g_ref[...])
        dg_ref[...] += dg
        dx1_ref[...] = dx2v + dres

    return pl.pallas_call(
        body, name="mlp_bwd", grid=(t_len // TM,),
        out_shape=(jax.ShapeDtypeStruct((t_len, d), F32), jax.ShapeDtypeStruct((t_len, d_ff), BF16),
                   jax.ShapeDtypeStruct((d, t_len), BF16), jax.ShapeDtypeStruct((1, d), F32)),
        in_specs=[_row_spec(TM, d), _row_spec(TM, d_ff), _row_spec(TM, d), _const_spec((1, d)),
                  _const_spec(w_up.shape), _const_spec(w_down.shape)],
        out_specs=(_row_spec(TM, d), _row_spec(TM, d_ff), _col_spec(d, TM), _acc_spec((1, d))),
        compiler_params=_seq_params(),
    )(dx2, a, x1, g_mlp, w_up, w_down)


def _mix_bwd(dx1, o_fox, o_sb, gl, w_bf, w_bs, w_out, b_gate):
    t_len, d = dx1.shape

    def body(dx1_ref, of_ref, os_ref, gl_ref, wbf_ref, wbs_ref, wo_ref, bg_ref,
             mg_ref, dbf_ref, dbs_ref, dgl_ref, dof_ref, dos_ref, dbg_ref, doft_ref):
        @pl.when(pl.program_id(0) == 0)
        def _():
            dbg_ref[...] = jnp.zeros_like(dbg_ref)

        dmerged = _dot_nt(dx1_ref[...].astype(BF16), wo_ref[...])
        br_f = _dot(of_ref[...], wbf_ref[...])
        br_s = _dot(os_ref[...], wbs_ref[...])
        ga = _sigmoid(gl_ref[:, :d].astype(F32) + bg_ref[0:1, :])
        gb = _sigmoid(gl_ref[:, d:].astype(F32) + bg_ref[1:2, :])
        mg_ref[...] = (ga * br_f + gb * br_s).astype(BF16)
        dbf = (dmerged * ga).astype(BF16)
        dbs = (dmerged * gb).astype(BF16)
        dbf_ref[...] = dbf
        dbs_ref[...] = dbs
        dla = dmerged * br_f * ga * (1.0 - ga)
        dlb = dmerged * br_s * gb * (1.0 - gb)
        dgl_ref[:, :d] = dla.astype(BF16)
        dgl_ref[:, d:] = dlb.astype(BF16)
        dbg_ref[0:1, :] += jnp.sum(dla, axis=0, keepdims=True)
        dbg_ref[1:2, :] += jnp.sum(dlb, axis=0, keepdims=True)
        dof = _dot_nt(dbf, wbf_ref[...])
        dof_ref[...] = dof.astype(BF16)
        doft_ref[0] = _slot_rows(dof.T, jnp.zeros((HEAD_DIM, TM), F32)).astype(BF16)
        dos_ref[...] = _dot_nt(dbs, wbs_ref[...]).astype(BF16)

    shp_d = jax.ShapeDtypeStruct((t_len, d), BF16)
    shp_h = jax.ShapeDtypeStruct((t_len, D_BRANCH), BF16)
    return pl.pallas_call(
        body, name="mix_bwd", grid=(t_len // TM,),
        out_shape=(shp_d, shp_d, shp_d, jax.ShapeDtypeStruct((t_len, 2 * d), BF16), shp_h, shp_h,
                   jax.ShapeDtypeStruct((2, d), F32),
                   jax.ShapeDtypeStruct((t_len // TM, N_HEADS * HEAD_SLOT, TM), BF16)),
        in_specs=[_row_spec(TM, d), _row_spec(TM, D_BRANCH), _row_spec(TM, D_BRANCH), _row_spec(TM, 2 * d),
                  _const_spec(w_bf.shape), _const_spec(w_bs.shape), _const_spec(w_out.shape), _const_spec(b_gate.shape)],
        out_specs=(_row_spec(TM, d), _row_spec(TM, d), _row_spec(TM, d), _row_spec(TM, 2 * d),
                   _row_spec(TM, D_BRANCH), _row_spec(TM, D_BRANCH), _acc_spec((2, d)),
                   pl.BlockSpec((1, N_HEADS * HEAD_SLOT, TM), lambda i: (i, 0, 0))),
        compiler_params=_seq_params(),
    )(dx1, o_fox, o_sb, gl, w_bf, w_bs, w_out, b_gate)


def _inproj_bwd(dqk_f, dv_f, dqkv_b, dgl, df, dx1, x, g_mix, w_pad, w_qk, chip_sums):
    t_len, d = x.shape
    lay, _ = _pad_layout(d)
    n = len(chip_sums)
    n_tiles = t_len // TM

    def body(dqk_ref, dvf_ref, db_ref, dgl_ref, df_ref, dx1_ref, x_ref, g_ref, w_ref, wqk_ref, *rest):
        cs_refs, (dx_ref, dg_ref), out_refs = rest[:n], rest[n:n + 2], rest[n + 2:2 * n + 2]
        chips_start, chips_finish = _chips_plan(cs_refs, out_refs, *rest[2 * n + 2:])

        @pl.when(pl.program_id(0) == 0)
        def _():
            dg_ref[...] = jnp.zeros_like(dg_ref)
            chips_start()

        def back(piece, name):
            lo, hi = lay[name]
            return _dot(piece, w_ref[lo:hi, :])

        xn, r = _rms(x_ref[...])
        dh = (back(df_ref[...], "forget") + back(dgl_ref[...], "gates") + _dot(dqk_ref[0], wqk_ref[:D_BRANCH, :])
              + _dot(dqk_ref[1], wqk_ref[D_BRANCH:, :]) + back(dvf_ref[...], "vf") + back(db_ref[0], "qb")
              + back(db_ref[1], "kb") + back(db_ref[2], "vb"))
        dres, dg = _rms_bwd(dh, xn, r, g_ref[...])
        dg_ref[...] += dg
        dx_ref[...] = dx1_ref[...] + dres
        pl.when(pl.program_id(0) == n_tiles - 1)(chips_finish)

    out = pl.pallas_call(
        body, name="inproj_bwd", grid=(n_tiles,),
        out_shape=[jax.ShapeDtypeStruct((t_len, d), F32), jax.ShapeDtypeStruct((1, d), F32)]
        + [jax.ShapeDtypeStruct(s.shape, s.dtype) for s in chip_sums],
        in_specs=[_row3_spec(2, TM, D_BRANCH), _row_spec(TM, D_BRANCH), _row3_spec(3, TM, D_BRANCH),
                  _row_spec(TM, 2 * d), _row_spec(TM, LANES), _row_spec(TM, d), _row_spec(TM, d), _const_spec((1, d)),
                  _const_spec(w_pad.shape), _const_spec(w_qk.shape)] + _hbm_specs(n),
        out_specs=[_row_spec(TM, d), _acc_spec((1, d))] + _hbm_specs(n),
        scratch_shapes=_chips_sems(n),
        compiler_params=_seq_params(),
    )(dqk_f, dv_f, dqkv_b, dgl, df, dx1, x, g_mix, w_pad, w_qk, *chip_sums)
    return out[0], out[1], out[2:]


def _cols_to_slabs(full):
    r, c8 = full.shape
    return full.reshape(r, N_DEV, c8 // N_DEV).transpose(1, 0, 2)


def _slabs_to_cols(slabs):
    n, r, c = slabs.shape
    return slabs.transpose(1, 0, 2).reshape(r, n * c)


def _win_sizes(d):
    return (D_BRANCH, D_BRANCH, D_BRANCH, N_HEADS, D_BRANCH, D_BRANCH, D_BRANCH, d, d)


def _split_win(w_t, d):
    out, off = [], 0
    for s in _win_sizes(d):
        out.append(w_t[off:off + s])
        off += s
    return out


def _to_slots(w_t):
    c = w_t.shape[1]
    return jnp.pad(w_t.reshape(N_HEADS, HEAD_DIM, c), ((0, 0), (0, HEAD_SLOT - HEAD_DIM), (0, 0))).reshape(-1, c)


def _pad_win(w_full_t, d):
    qa, ka, va, fa, qb, kb, vb, ga, gb = _split_win(w_full_t, d)
    scale = HEAD_DIM ** -0.5
    fpad = jnp.pad(fa, ((0, LANES - N_HEADS), (0, 0)))
    w_pad = jnp.concatenate([_to_slots(qa * scale), _to_slots(ka), va, qb * scale, kb, vb, ga, gb, fpad], axis=0)
    return w_pad, jnp.concatenate([qa * scale, ka], axis=0)


def _unpad_dwin(dqk_f, dv_f, dqkv_b, dgates, dforget, d):
    scale = HEAD_DIM ** -0.5
    return jnp.concatenate([dqk_f[0] * scale, dqk_f[1], dv_f, dforget[:N_HEADS],
                            dqkv_b[0] * scale, dqkv_b[1], dqkv_b[2], dgates], axis=0)


def _c_lane_constants():
    row = jnp.arange(LANES)[:, None]
    lane = jnp.arange(N_HEADS * HEAD_SLOT)[None, :]

    def place(first):
        return ((lane // HEAD_SLOT == row % N_HEADS) & (lane % HEAD_SLOT == first + row // N_HEADS)
                & (row < 3 * N_HEADS)).astype(BF16)

    def ones(first):
        off = lane % HEAD_SLOT
        return ((off >= first) & (off < first + 3)).astype(F32)

    return place(C_TERMS_Q), place(C_TERMS_K), ones(C_ONES_Q), ones(C_ONES_K)


def _pad_rows(a, rows):
    return jnp.pad(a, [(0, 0)] * (a.ndim - 2) + [(0, rows - a.shape[-2]), (0, 0)])


def kernel(x, p, g_mix, w_in, b_forget, b_gate, w_branch_fox, w_branch_sb, w_out, g_mlp, w_up, w_down, g_ple, w_ple_gate, w_ple, g_final, loss_target, m_g_mix, m_w_in, m_b_forget, m_b_gate, m_w_branch_fox, m_w_branch_sb, m_w_out, m_g_mlp, m_w_up, m_w_down, m_g_ple, m_w_ple_gate, m_w_ple, m_g_final, v_g_mix, v_w_in, v_b_forget, v_b_gate, v_w_branch_fox, v_w_branch_sb, v_w_out, v_g_mlp, v_w_up, v_w_down, v_g_ple, v_w_ple_gate, v_w_ple, v_g_final):
    batch, seq, d = x.shape
    t_len = batch * seq
    d_ple = p.shape[-1]
    d_ff = w_up.shape[-1] * N_DEV
    dn = d // N_DEV
    fn = d_ff // N_DEV
    my_c = lax.axis_index("c")
    my_dev = 4 * lax.axis_index("x") + 2 * lax.axis_index("y") + my_c

    bg_hi = b_gate[0].astype(BF16)
    bg_r = b_gate[0] - bg_hi.astype(F32)
    bg_mid = bg_r.astype(BF16)
    bg_lo = (bg_r - bg_mid.astype(F32)).astype(BF16)
    narrow_rows = 2 * D_BRANCH + d_ple + 6
    narrow_rows_pad = -(-narrow_rows // 16) * 16
    narrow = _pad_rows(jnp.concatenate(
        [w_branch_fox[0].astype(BF16), w_branch_sb[0].astype(BF16), w_ple[0].astype(BF16), bg_hi, bg_mid, bg_lo],
        axis=0), narrow_rows_pad)
    g_in, = _all_gather([w_in[0].T.astype(BF16)])
    w_pad, w_qk = _pad_win(g_in.reshape(-1, d), d)
    bf_pad = jnp.pad(b_forget, ((0, 0), (0, LANES - N_HEADS)))
    place_q, place_k, ones_q, ones_k = _c_lane_constants()

    x2d = x.reshape(t_len, d)
    p2d = p.reshape(t_len, d_ple)
    tgt2d = loss_target.reshape(t_len, d)
    qf, kf, kft, vf, vft, qkvb, kbt, vbt, gl, fpre, h1, qft = _inproj_fwd(
        x2d, g_mix, w_pad, bf_pad, place_q, place_k, ones_q, ones_k, seq)
    o_sb, ltot, (g_up, g_out, g_down, g_pg, g_narrow) = _sb_fwd(qkvb, vbt, batch, seq, [
        w_up[0].astype(BF16), w_out[0].astype(BF16), w_down[0].astype(BF16), w_ple_gate[0].astype(BF16), narrow])
    o_fox, lse = _fox_fwd(qf, kf, vft, batch, seq)
    w_up_full = _slabs_to_cols(g_up)
    w_out_full = g_out.reshape(d, d)
    w_down_full = g_down.reshape(d_ff, d)
    w_pg_full = g_pg.reshape(d, d)
    w_bf_full = _slabs_to_cols(g_narrow[:, :D_BRANCH])
    w_bs_full = _slabs_to_cols(g_narrow[:, D_BRANCH:2 * D_BRANCH])
    w_ple_full = _slabs_to_cols(g_narrow[:, 2 * D_BRANCH:2 * D_BRANCH + d_ple])
    bg_terms = g_narrow[:, 2 * D_BRANCH + d_ple:narrow_rows].astype(F32)
    b_gate_full = _slabs_to_cols(bg_terms[:, 0:2] + bg_terms[:, 2:4] + bg_terms[:, 4:6])
    x1 = _mix_fwd(o_fox, o_sb, gl, x2d, w_bf_full, w_bs_full, w_out_full, b_gate_full)

    a_up, dx2, h3, dpre, dpe, loss_acc, dg_ple, dg_final = _mlp_head_fwd_bwd(
        x1, p2d, tgt2d, g_mlp, w_up_full, w_down_full, g_ple, g_final.reshape(1, d), w_pg_full, w_ple_full)
    dx1, da_up, h2t, dg_mlp = _mlp_bwd(dx2, a_up, x1, g_mlp, w_up_full, w_down_full)
    merged, dbr_f, dbr_s, dgl, do_fox, do_sb, dbg, do_fox_t = _mix_bwd(
        dx1, o_fox, o_sb, gl, w_bf_full, w_bs_full, w_out_full, b_gate_full)

    def column_shards(name, lhs, rhs, lhs_t=False):
        if (rhs.shape[-1] // N_DEV) % (4 * LANES) == 0:
            return _matmul_tn(name, lhs, rhs, slabs=True, lhs_t=lhs_t)
        return _cols_to_slabs(_matmul_tn(name, lhs, rhs, lhs_t=lhs_t))

    if fn % (4 * LANES) == 0:
        part_up, = _matmul_tn_once("dw_up", [h2t], da_up, slabs=True, lhs_t=True)
    else:
        part_up = column_shards("dw_up", h2t, da_up, lhs_t=True)
    part_out = _matmul_tn("dw_out", merged, dx1).reshape(N_DEV, dn, d)
    part_down = _matmul_tn_once("dw_down", [a_up], dx2, relu2=True)[0].reshape(N_DEV, fn, d)
    part_pg = _matmul_tn("dw_ple_gate", h3, dpre).reshape(N_DEV, dn, d)
    part_narrow = _pad_rows(jnp.concatenate(
        [column_shards("dw_branch_fox", o_fox, dbr_f), column_shards("dw_branch_sb", o_sb, dbr_s),
         column_shards("dw_ple", p2d, dpe)], axis=1), narrow_rows_pad)
    early = [part_up, part_out, part_down, part_pg, lax.optimization_barrier(part_narrow)]

    dqk_f, dv_f, dc_queries, dc_keys, early_recv = _fox_bwd(
        qf, qft, kf, kft, vf, o_fox, do_fox, do_fox_t, lse, batch, seq, early)
    early_sums = [_pair_add("pair_add_%d" % i, pt, rc, my_c) for i, (pt, rc) in enumerate(zip(early, early_recv))]
    dqkv_b, (s_up, s_out, s_down, s_pg, s_narrow) = _sb_bwd(qkvb, kbt, do_sb, ltot, batch, seq, early_sums)
    dcq_tok = dc_queries.reshape(batch, N_HEADS, seq).transpose(0, 2, 1).reshape(t_len, N_HEADS)
    dck_tok = dc_keys.reshape(batch, N_HEADS, seq).transpose(0, 2, 1).reshape(t_len, N_HEADS)
    lane_pad = ((0, 0), (0, LANES - N_HEADS))
    df, db_forget = _forget_bwd(jnp.pad(dcq_tok, lane_pad), jnp.pad(dck_tok, lane_pad), fpre, batch, seq)

    gw_in = _unpad_dwin(*_matmul_tn_once("dw_in_fox_qk", [dqk_f], h1),
                        *_matmul_tn_once("dw_in_rest", [dv_f, dqkv_b, dgl, df], h1), d)
    part_in = lax.optimization_barrier(gw_in.reshape(N_DEV, -1, d))
    recv_in, = _rs_core_pair("reduce_scatter_core_pair_w_in", [part_in])
    grad_x, dg_mix, (s_in,) = _inproj_bwd(dqk_f, dv_f, dqkv_b, dgl, df, dx1, x2d, g_mix, w_pad, w_qk,
                                          [_pair_add("pair_add_w_in", part_in, recv_in, my_c)])

    small = jnp.concatenate([
        dg_mix, dg_mlp, dg_ple, dg_final, jnp.pad(db_forget[:, :N_HEADS], ((0, 0), (0, d - N_HEADS))), dbg,
        jnp.pad(loss_acc[:, :1], ((0, 0), (0, d - 1)))], axis=0)
    small = _all_reduce_small(small)
    loss = small[7, 0]
    small_grads = {
        "g_mix": small[0:1], "g_mlp": small[1:2], "g_ple": small[2:3], "g_final": small[3:4],
        "b_forget": small[4:5, :N_HEADS],
        "b_gate": lax.dynamic_slice_in_dim(small[5:7], my_dev * dn, dn, axis=1),
    }

    weights = {"g_mix": g_mix, "w_in": w_in, "b_forget": b_forget, "b_gate": b_gate, "w_branch_fox": w_branch_fox,
               "w_branch_sb": w_branch_sb, "w_out": w_out, "g_mlp": g_mlp, "w_up": w_up, "w_down": w_down,
               "g_ple": g_ple, "w_ple_gate": w_ple_gate, "w_ple": w_ple, "g_final": g_final}
    m_in = {"g_mix": m_g_mix, "w_in": m_w_in, "b_forget": m_b_forget, "b_gate": m_b_gate,
            "w_branch_fox": m_w_branch_fox, "w_branch_sb": m_w_branch_sb, "w_out": m_w_out, "g_mlp": m_g_mlp,
            "w_up": m_w_up, "w_down": m_w_down, "g_ple": m_g_ple, "w_ple_gate": m_w_ple_gate, "w_ple": m_w_ple,
            "g_final": m_g_final}
    v_in = {"g_mix": v_g_mix, "w_in": v_w_in, "b_forget": v_b_forget, "b_gate": v_b_gate,
            "w_branch_fox": v_w_branch_fox, "w_branch_sb": v_w_branch_sb, "w_out": v_w_out, "g_mlp": v_g_mlp,
            "w_up": v_w_up, "w_down": v_w_down, "g_ple": v_g_ple, "w_ple_gate": v_w_ple_gate, "w_ple": v_w_ple,
            "g_final": v_g_final}
    names = list(weights)

    def as2d(a):
        return a.reshape(-1, a.shape[-1])

    result = {}
    big = {"w_up": (s_up, 0), "w_out": (s_out, 0), "w_down": (s_down, 0), "w_ple_gate": (s_pg, 0),
           "w_branch_fox": (s_narrow, 0), "w_branch_sb": (s_narrow, D_BRANCH), "w_ple": (s_narrow, 2 * D_BRANCH)}
    for n, (parts, off) in big.items():
        result[n] = _adamw_parts("adamw_" + n, as2d(weights[n]), parts, off, as2d(m_in[n]), as2d(v_in[n]))
    result["w_in"] = tuple(r.T for r in _adamw_parts("adamw_w_in", w_in[0].T, s_in, 0, m_w_in[0].T, v_w_in[0].T))
    small_names = list(small_grads)
    small_out = _adamw_small([(as2d(weights[n]), small_grads[n], as2d(m_in[n]), as2d(v_in[n])) for n in small_names])
    for n, (dlt, nm, nv) in zip(small_names, small_out):
        result[n] = (small_grads[n], dlt, nm, nv)
    outs = [[result[n][k].reshape(weights[n].shape) for n in names] for k in range(4)]
    return (loss, grad_x.reshape(x.shape), *outs[0], *outs[1], *outs[2], *outs[3])
```

```python
import jax
import jax.numpy as jnp
from jax import lax
from jax.experimental import pallas as pl
from jax.experimental.pallas import tpu as pltpu

F32 = jnp.float32
BF16 = jnp.bfloat16

HEAD_DIM = 64
N_HEADS = 8
D_BRANCH = N_HEADS * HEAD_DIM
EPS = 1e-6
ADAM_LR = 0.001
ADAM_B1 = 0.9
ADAM_B2 = 0.999
ADAM_EPS = 1e-08
ADAM_WD = 0.01
ADAM_STEP = 10

N_DEV = 8
LANES = 128
TM = 256
TQ = 256
TK = 256
NH = 4
HEAD_SLOT = 128
C_TERMS_Q = 64
C_ONES_K = 64
C_TERMS_K = 67
C_ONES_Q = 67
NEG = -1e30
VMEM_LIMIT = 56 * 1024 * 1024
MESH = pl.DeviceIdType.MESH


def _dot(a, b):
    return jnp.dot(a, b, preferred_element_type=F32)


def _dot_nt(a, b):
    return lax.dot_general(a, b, (((1,), (1,)), ((), ())), preferred_element_type=F32)


def _dot_tn(a, b):
    return lax.dot_general(a, b, (((0,), (0,)), ((), ())), preferred_element_type=F32)


def _sigmoid(x):
    return 1.0 / (1.0 + jnp.exp(-x))


def _softplus(x):
    return jnp.maximum(x, 0.0) + jnp.log(1.0 + jnp.exp(-jnp.abs(x)))


def _split2(x):
    hi = x.astype(BF16)
    lo = (x - hi.astype(F32)).astype(BF16)
    return hi, lo


def _split3(x):
    hi = x.astype(BF16)
    r = x - hi.astype(F32)
    mid = r.astype(BF16)
    lo = (r - mid.astype(F32)).astype(BF16)
    return hi, mid, lo


def _tri(n, rel):
    r = lax.broadcasted_iota(jnp.int32, (n, n), 0)
    c = lax.broadcasted_iota(jnp.int32, (n, n), 1)
    return rel(r, c).astype(BF16)


def _rms(x):
    r = lax.rsqrt(jnp.mean(x * x, axis=-1, keepdims=True) + EPS)
    return x * r, r


def _rms_bwd(dh, xn, r, g):
    dxn = dh * g
    dx = r * (dxn - xn * jnp.mean(dxn * xn, axis=-1, keepdims=True))
    return dx, jnp.sum(dh * xn, axis=0, keepdims=True)


def _row_spec(tm, cols):
    return pl.BlockSpec((tm, cols), lambda i: (i, 0))


def _row3_spec(g, tm, cols):
    return pl.BlockSpec((g, tm, cols), lambda i: (0, i, 0))


def _col_spec(rows, tm):
    return pl.BlockSpec((rows, tm), lambda i: (0, i))


def _const_spec(shape):
    nd = len(shape)
    return pl.BlockSpec(shape, lambda i: (0,) * nd, pipeline_mode=pl.Buffered(1))


def _acc_spec(shape):
    nd = len(shape)
    return pl.BlockSpec(shape, lambda i: (0,) * nd)


def _seq_params():
    return pltpu.CompilerParams(dimension_semantics=("arbitrary",), vmem_limit_bytes=VMEM_LIMIT)


def _mesh_pos():
    return lax.axis_index("x"), lax.axis_index("y"), lax.axis_index("c")


def _other_chips(x, y):
    return [(1 - x, y), (x, 1 - y), (1 - x, 1 - y)]


def _hbm_specs(n):
    return [pl.BlockSpec(memory_space=pl.ANY)] * n


def _gather_plan(x_refs, out_refs, send_sems, recv_sems, local_sems):
    n = len(x_refs)
    x, y, c = _mesh_pos()
    me, sibling = (x, y, c), (x, y, 1 - c)
    x_nb, y_nb, diag = (1 - x, y), (x, 1 - y), (1 - x, 1 - y)

    def slab(a, chip, core):
        return out_refs[a].at[4 * chip[0] + 2 * chip[1] + core]

    def half(ref, upper):
        rows = ref.shape[0]
        cut = (rows // 2) // 16 * 16
        return ref.at[pl.ds(cut, rows - cut)] if upper else ref.at[pl.ds(0, cut)]

    def copy(a, k, ref, to, src=None):
        return pltpu.make_async_remote_copy(
            src_ref=ref if src is None else src, dst_ref=ref,
            send_sem=send_sems.at[8 * a + k], recv_sem=recv_sems.at[8 * a + k], device_id=to, device_id_type=MESH)

    mine = [pltpu.make_async_copy(x_refs[a], slab(a, (x, y), c), local_sems.at[a]) for a in range(n)]
    first = []
    for a in range(n):
        own = slab(a, (x, y), c)
        first += [copy(a, 0, own, sibling, src=x_refs[a]), copy(a, 1, own, (*x_nb, c), src=x_refs[a]),
                  copy(a, 2, own, (*y_nb, c), src=x_refs[a])]

    def start():
        for cp in mine + first:
            cp.start()

    def finish():
        passed = []

        def pass_on(cp):
            passed.append(cp)
            cp.start()

        for a in range(n):
            got = slab(a, x_nb, c)
            copy(a, 1, got, me).wait_recv()
            pass_on(copy(a, 5, got, sibling))
            pass_on(copy(a, 3, half(got, False), (*y_nb, c)))
        for a in range(n):
            got = slab(a, y_nb, c)
            copy(a, 2, got, me).wait_recv()
            pass_on(copy(a, 6, got, sibling))
            pass_on(copy(a, 4, half(got, True), (*x_nb, c)))
        for a in range(n):
            got = slab(a, diag, c)
            copy(a, 3, half(got, False), me).wait_recv()
            copy(a, 4, half(got, True), me).wait_recv()
            pass_on(copy(a, 7, got, sibling))
        for a in range(n):
            copy(a, 0, slab(a, (x, y), 1 - c), me).wait_recv()
            for k, chip in ((5, x_nb), (6, y_nb), (7, diag)):
                copy(a, k, slab(a, chip, 1 - c), me).wait_recv()
        for cp in first + passed:
            cp.wait_send()
        for cp in mine:
            cp.wait()

    return start, finish


def _gather_plan_direct(x_refs, out_refs, send_sems, recv_sems, local_sems):
    n = len(x_refs)
    x, y, c = _mesh_pos()
    me, sibling = (x, y, c), (x, y, 1 - c)
    chips = _other_chips(x, y)

    def index(px, py, pc):
        return 4 * px + 2 * py + pc

    def copy(a, k, block, to, src=None):
        slab = out_refs[a].at[index(*block)]
        return pltpu.make_async_remote_copy(
            src_ref=slab if src is None else src, dst_ref=slab,
            send_sem=send_sems.at[8 * a + k], recv_sem=recv_sems.at[8 * a + k], device_id=to, device_id_type=MESH)

    mine = [pltpu.make_async_copy(x_refs[a], out_refs[a].at[index(*me)], local_sems.at[a]) for a in range(n)]
    first = []
    for a in range(n):
        first.append(copy(a, 0, me, sibling, src=x_refs[a]))
        first += [copy(a, 1 + j, me, (cx, cy, c), src=x_refs[a]) for j, (cx, cy) in enumerate(chips)]

    def start():
        for cp in mine + first:
            cp.start()

    def finish():
        passed = []
        for j, (cx, cy) in enumerate(chips):
            for a in range(n):
                copy(a, 1 + j, (cx, cy, c), me).wait_recv()
                passed.append(copy(a, 4 + j, (cx, cy, c), sibling))
                passed[-1].start()
        for a in range(n):
            copy(a, 0, sibling, me).wait_recv()
            for j, (cx, cy) in enumerate(chips):
                copy(a, 4 + j, (cx, cy, 1 - c), me).wait_recv()
        for cp in first + passed:
            cp.wait_send()
        for cp in mine:
            cp.wait()

    return start, finish


def _gather_shapes(shards):
    return [jax.ShapeDtypeStruct((N_DEV,) + s.shape, s.dtype) for s in shards]


def _gather_sems(n):
    return [pltpu.SemaphoreType.DMA((8 * n,)), pltpu.SemaphoreType.DMA((8 * n,)), pltpu.SemaphoreType.DMA((n,))]


def _all_gather(shards):
    n = len(shards)

    def body(*refs):
        start, finish = _gather_plan(refs[:n], refs[n:2 * n], *refs[2 * n:])
        start()
        finish()

    return pl.pallas_call(
        body, name="all_gather_weights", out_shape=_gather_shapes(shards),
        in_specs=_hbm_specs(n), out_specs=_hbm_specs(n), scratch_shapes=_gather_sems(n),
    )(*shards)


def _pair_plan(p_refs, recv_refs, send_sems, recv_sems):
    n = len(p_refs)
    x, y, c = _mesh_pos()
    sibling = (x, y, 1 - c)

    def start():
        for a in range(n):
            for chip in range(4):
                pltpu.make_async_remote_copy(
                    src_ref=p_refs[a].at[2 * chip + (1 - c)], dst_ref=recv_refs[a].at[chip],
                    send_sem=send_sems.at[a], recv_sem=recv_sems.at[a], device_id=sibling, device_id_type=MESH).start()

    def finish():
        for a in range(n):
            pltpu.make_async_remote_copy(
                src_ref=recv_refs[a], dst_ref=recv_refs[a], send_sem=send_sems.at[a], recv_sem=recv_sems.at[a],
                device_id=sibling, device_id_type=MESH).wait()

    return start, finish


def _pair_shapes(partials):
    return [jax.ShapeDtypeStruct((4,) + s.shape[1:], s.dtype) for s in partials]


def _pair_sems(n):
    return [pltpu.SemaphoreType.DMA((n,)), pltpu.SemaphoreType.DMA((n,))]


def _rs_core_pair(name, partials):
    n = len(partials)

    def body(*refs):
        start, finish = _pair_plan(refs[:n], refs[n:2 * n], *refs[2 * n:])
        start()
        finish()

    return pl.pallas_call(
        body, name=name, out_shape=_pair_shapes(partials),
        in_specs=_hbm_specs(n), out_specs=_hbm_specs(n), scratch_shapes=_pair_sems(n),
    )(*partials)


def _chips_plan(cs_refs, out_refs, send_sems, recv_sems, local_sems):
    n = len(cs_refs)
    x, y, c = _mesh_pos()
    chip = 2 * x + y
    chips = _other_chips(x, y)
    mine = [pltpu.make_async_copy(cs_refs[a].at[chip], out_refs[a].at[chip], local_sems.at[a]) for a in range(n)]
    sends = [pltpu.make_async_remote_copy(
        src_ref=cs_refs[a].at[2 * cx + cy], dst_ref=out_refs[a].at[chip],
        send_sem=send_sems.at[3 * a + j], recv_sem=recv_sems.at[3 * a + j],
        device_id=(cx, cy, c), device_id_type=MESH) for a in range(n) for j, (cx, cy) in enumerate(chips)]

    def start():
        for cp in mine + sends:
            cp.start()

    def finish():
        for a in range(n):
            for j, (cx, cy) in enumerate(chips):
                pltpu.make_async_remote_copy(
                    src_ref=cs_refs[a].at[chip], dst_ref=out_refs[a].at[2 * cx + cy],
                    send_sem=send_sems.at[3 * a + j], recv_sem=recv_sems.at[3 * a + j],
                    device_id=(x, y, c), device_id_type=MESH).wait_recv()
        for cp in sends:
            cp.wait_send()
        for cp in mine:
            cp.wait()

    return start, finish


def _chips_sems(n):
    return [pltpu.SemaphoreType.DMA((3 * n,)), pltpu.SemaphoreType.DMA((3 * n,)), pltpu.SemaphoreType.DMA((n,))]


def _all_reduce_small(vec):
    rows, cols = vec.shape

    def body(x_ref, land_ref, sum_ref, send_sems, recv_sems):
        x, y, c = _mesh_pos()
        me = 4 * x + 2 * y + c
        land_ref[me] = x_ref[...]
        flips = [(fx, fy, fc) for fx in (0, 1) for fy in (0, 1) for fc in (0, 1)][1:]

        def flipped(f):
            return tuple((1 - v) if b else v for v, b in zip((x, y, c), f))

        sends = []
        for k, f in enumerate(flips):
            sends.append(pltpu.make_async_remote_copy(
                src_ref=x_ref, dst_ref=land_ref.at[me], send_sem=send_sems.at[k], recv_sem=recv_sems.at[k],
                device_id=flipped(f), device_id_type=MESH))
            sends[-1].start()
        for k, f in enumerate(flips):
            px, py, pc = flipped(f)
            pltpu.make_async_remote_copy(
                src_ref=x_ref, dst_ref=land_ref.at[4 * px + 2 * py + pc], send_sem=send_sems.at[k],
                recv_sem=recv_sems.at[k], device_id=(x, y, c), device_id_type=MESH).wait_recv()
        for cp in sends:
            cp.wait_send()
        total = land_ref[0]
        for d in range(1, N_DEV):
            total = total + land_ref[d]
        sum_ref[...] = total

    vm = pl.BlockSpec(memory_space=pltpu.VMEM)
    return pl.pallas_call(
        body, name="all_reduce_small",
        out_shape=(jax.ShapeDtypeStruct((N_DEV, rows, cols), F32), jax.ShapeDtypeStruct((rows, cols), F32)),
        in_specs=[vm], out_specs=(vm, vm),
        scratch_shapes=[pltpu.SemaphoreType.DMA((7,)), pltpu.SemaphoreType.DMA((7,))],
    )(vec)[1]


def _block_rows(rows, cols, itemsize, align, row_off=0):
    best = None
    for t in range(align, rows + 1, align):
        if rows % t == 0 and row_off % t == 0 and t * cols * itemsize <= (1 << 20):
            best = t
    return rows if best is None else best


def _pair_add(name, partial, recv, my_c):
    _, rows, cols = partial.shape
    br = _block_rows(rows, cols, 2, 16)

    def body(c_ref, a_ref, b_ref, o_ref):
        o_ref[...] = (a_ref[...].astype(F32) + b_ref[...].astype(F32)).astype(BF16)

    return pl.pallas_call(
        body, name=name,
        grid_spec=pltpu.PrefetchScalarGridSpec(
            num_scalar_prefetch=1, grid=(4, rows // br),
            in_specs=[pl.BlockSpec((None, None, br, cols), lambda j, i, c_ref: (j, c_ref[0], i, 0)),
                      pl.BlockSpec((None, br, cols), lambda j, i, c_ref: (j, i, 0))],
            out_specs=pl.BlockSpec((None, br, cols), lambda j, i, c_ref: (j, i, 0))),
        out_shape=jax.ShapeDtypeStruct((4, rows, cols), BF16),
    )(my_c.reshape(1).astype(jnp.int32), partial.reshape(4, 2, rows, cols), recv)


def _adam_update(w, g, m, v):
    nm = ADAM_B1 * m + (1.0 - ADAM_B1) * g
    nv = ADAM_B2 * v + (1.0 - ADAM_B2) * (g * g)
    m_hat = nm / (1.0 - ADAM_B1 ** ADAM_STEP)
    v_hat = nv / (1.0 - ADAM_B2 ** ADAM_STEP)
    return -ADAM_LR * (m_hat / (jnp.sqrt(v_hat) + ADAM_EPS) + ADAM_WD * w), nm, nv


def _adamw_parts(name, w, parts, row_off, m, v):
    rows, cols = w.shape
    tr = _block_rows(rows, cols, 4, 16, row_off)
    tc = cols
    if tr == rows and rows % 16 != 0 and cols % (2 * LANES) == 0:
        tc = 2 * LANES
    assert rows % tr == 0 and row_off % tr == 0 and (tc == cols or row_off == 0)
    off = row_off // tr

    def body(w_ref, p_ref, m_ref, v_ref, g_ref, d_ref, nm_ref, nv_ref):
        g = p_ref[0].astype(F32)
        for j in range(1, 4):
            g = g + p_ref[j].astype(F32)
        g_ref[...] = g
        d_ref[...], nm_ref[...], nv_ref[...] = _adam_update(w_ref[...], g, m_ref[...], v_ref[...])

    spec = pl.BlockSpec((tr, tc), lambda i, j: (i, j))
    shp = jax.ShapeDtypeStruct((rows, cols), F32)
    return pl.pallas_call(
        body, name=name, grid=(rows // tr, cols // tc), out_shape=(shp,) * 4,
        in_specs=[spec, pl.BlockSpec((4, tr, tc), lambda i, j: (0, off + i, j)), spec, spec], out_specs=(spec,) * 4,
    )(w, parts, m, v)


def _adamw_small(tensors):
    n = len(tensors)

    def body(*refs):
        ins, outs = refs[:4 * n], refs[4 * n:]
        for t in range(n):
            w_ref, g_ref, m_ref, v_ref = ins[4 * t:4 * t + 4]
            d, nm, nv = _adam_update(w_ref[...], g_ref[...], m_ref[...], v_ref[...])
            outs[3 * t][...], outs[3 * t + 1][...], outs[3 * t + 2][...] = d, nm, nv

    vm = pl.BlockSpec(memory_space=pltpu.VMEM)
    out = pl.pallas_call(
        body, name="adamw_small",
        out_shape=[jax.ShapeDtypeStruct(t[0].shape, F32) for t in tensors for _ in range(3)],
        in_specs=[vm] * (4 * n), out_specs=[vm] * (3 * n),
    )(*[a for t in tensors for a in t])
    return [tuple(out[3 * t:3 * t + 3]) for t in range(n)]


def _matmul_tn(name, a, b, relu2=False, slabs=False, lhs_t=False):
    a_groups = a.shape[0] if a.ndim == 3 else 0
    b_groups = b.shape[0] if b.ndim == 3 else 0
    groups = max(a_groups, b_groups, 1)
    assert not (a_groups and b_groups) and not (a_groups and lhs_t)
    a3 = a if a_groups else a[None]
    b3 = b if b_groups else b[None]
    t_len, k_len = a3.shape[1:][::-1] if lhs_t else a3.shape[1:]
    n_len = b3.shape[2]
    tt = min(t_len, 512)
    tk = min(k_len, 1024)
    tn = n_len // N_DEV if slabs else min(n_len, 1024)
    nt = t_len // tt
    assert not slabs or (groups == 1 and tn <= 1024)

    def body(a_ref, b_ref, o_ref, acc_ref):
        @pl.when(pl.program_id(3) == 0)
        def _():
            acc_ref[...] = jnp.zeros_like(acc_ref)

        av = a_ref[...]
        if relu2:
            av = jnp.square(jnp.maximum(av.astype(F32), 0.0))
        product = _dot if lhs_t else _dot_tn
        acc_ref[...] += product(av.astype(BF16), b_ref[...].astype(BF16))

        @pl.when(pl.program_id(3) == nt - 1)
        def _():
            o_ref[...] = acc_ref[...].astype(BF16)

    def a_group(g):
        return g if a_groups else 0

    def b_group(g):
        return g if b_groups else 0

    if slabs:
        out_shape = jax.ShapeDtypeStruct((N_DEV, k_len, tn), BF16)
        out_spec = pl.BlockSpec((None, tk, tn), lambda g, i, j, t: (j, i, 0))
    else:
        out_shape = jax.ShapeDtypeStruct((groups, k_len, n_len), BF16)
        out_spec = pl.BlockSpec((None, tk, tn), lambda g, i, j, t: (g, i, j))
    out = pl.pallas_call(
        body, name=name, grid=(groups, k_len // tk, n_len // tn, nt), out_shape=out_shape,
        in_specs=[pl.BlockSpec((None, tk, tt), lambda g, i, j, t: (a_group(g), i, t)) if lhs_t
                  else pl.BlockSpec((None, tt, tk), lambda g, i, j, t: (a_group(g), t, i)),
                  pl.BlockSpec((None, tt, tn), lambda g, i, j, t: (b_group(g), t, j))],
        out_specs=out_spec,
        scratch_shapes=[pltpu.VMEM((tk, tn), F32)],
        compiler_params=pltpu.CompilerParams(
            dimension_semantics=("parallel", "parallel", "parallel", "arbitrary"), vmem_limit_bytes=VMEM_LIMIT),
    )(a3, b3)
    return out if (slabs or a_groups or b_groups) else out[0]


def _matmul_tn_once(name, lhs_list, rhs, relu2=False, slabs=False, lhs_t=False):
    t_len, n_len = rhs.shape
    tt = min(t_len, 256)
    nt = t_len // tt
    n_lhs = len(lhs_list)
    assert not (lhs_t or slabs) or (n_lhs == 1 and lhs_list[0].ndim == 2)
    k_shapes = [(a.shape[0], n_len) if lhs_t else a.shape[:-2] + (a.shape[-1], n_len) for a in lhs_list]
    tn = n_len // N_DEV

    def body(*refs):
        a_refs, b_ref = refs[:n_lhs], refs[n_lhs]
        o_refs, acc_refs = refs[n_lhs + 1:2 * n_lhs + 1], refs[2 * n_lhs + 1:]
        step = pl.program_id(0)

        @pl.when(step == 0)
        def _():
            for acc in acc_refs:
                acc[...] = jnp.zeros_like(acc)

        bv = b_ref[...].astype(BF16)

        def piece(av):
            if relu2:
                av = jnp.square(jnp.maximum(av.astype(F32), 0.0))
            return (_dot if lhs_t else _dot_tn)(av.astype(BF16), bv)

        for a_ref, acc in zip(a_refs, acc_refs):
            if len(acc.shape) == 3:
                for g in range(acc.shape[0]):
                    acc[g] += piece(a_ref[g])
            else:
                acc[...] += piece(a_ref[...])

        @pl.when(step == nt - 1)
        def _():
            for o_ref, acc in zip(o_refs, acc_refs):
                if slabs:
                    for j in range(N_DEV):
                        o_ref[j] = acc[:, j * tn:(j + 1) * tn].astype(BF16)
                else:
                    o_ref[...] = acc[...].astype(BF16)

    def lhs_spec(a):
        if lhs_t:
            return pl.BlockSpec((a.shape[0], tt), lambda t: (0, t))
        if a.ndim == 3:
            return pl.BlockSpec((a.shape[0], tt, a.shape[2]), lambda t: (0, t, 0))
        return pl.BlockSpec((tt, a.shape[1]), lambda t: (t, 0))

    out_shapes = [(N_DEV, k_shapes[0][0], tn)] if slabs else k_shapes
    return pl.pallas_call(
        body, name=name, grid=(nt,),
        out_shape=[jax.ShapeDtypeStruct(s, BF16) for s in out_shapes],
        in_specs=[lhs_spec(a) for a in lhs_list] + [pl.BlockSpec((tt, n_len), lambda t: (t, 0))],
        out_specs=[_acc_spec(s) for s in out_shapes],
        scratch_shapes=[pltpu.VMEM(s, F32) for s in k_shapes],
        compiler_params=_seq_params(),
    )(*lhs_list, rhs)


def _pad_layout(d):
    names = ("qf", "kf", "vf", "qb", "kb", "vb", "gates", "forget")
    sizes = (N_HEADS * HEAD_SLOT, N_HEADS * HEAD_SLOT, D_BRANCH, D_BRANCH, D_BRANCH, D_BRANCH, 2 * d, LANES)
    out, off = {}, 0
    for n, s in zip(names, sizes):
        out[n] = (off, off + s)
        off += s
    return out, off


def _slot_rows(xt, extra):
    parts = []
    for h in range(N_HEADS):
        parts += [xt[h * HEAD_DIM:(h + 1) * HEAD_DIM, :], extra]
    return jnp.concatenate(parts, axis=0)


def _inproj_fwd(x, g_mix, w_pad, bf_pad, place_q, place_k, ones_q, ones_k, seq):
    t_len, d = x.shape
    lay, _ = _pad_layout(d)
    tiles_per_seq = seq // TM
    slot_w = N_HEADS * HEAD_SLOT

    def body(x_ref, g_ref, w_ref, bf_ref, pq_ref, pk_ref, oq_ref, ok_ref,
             qf_ref, kf_ref, kft_ref, vf_ref, vft_ref, qkvb_ref, kbt_ref, vbt_ref, gl_ref, fpre_ref, h_ref, qft_ref,
             carry_ref):
        @pl.when(pl.program_id(0) % tiles_per_seq == 0)
        def _():
            carry_ref[...] = jnp.zeros_like(carry_ref)

        def proj(name):
            lo, hi = lay[name]
            return _dot_nt(h, w_ref[lo:hi, :])

        xn, _ = _rms(x_ref[...])
        h = (xn * g_ref[...]).astype(BF16)
        fpre = proj("forget") + bf_ref[...]
        fpre_ref[...] = fpre
        logf = -_softplus(-fpre)
        lower = _tri(TM, lambda r, c: c <= r)
        hi, mid, lo = _split3(logf)
        c_val = carry_ref[...] + _dot(lower, hi) + _dot(lower, mid) + _dot(lower, lo)
        carry_ref[...] = carry_ref[...] + jnp.sum(logf, axis=0, keepdims=True)
        head_lanes = lax.broadcasted_iota(jnp.int32, (TM, LANES), 1) < N_HEADS
        terms = [jnp.where(head_lanes, t.astype(F32), 0.0) for t in _split3(c_val)]
        c_packed = (terms[0] + pltpu.roll(terms[1], N_HEADS, 1) + pltpu.roll(terms[2], 2 * N_HEADS, 1)).astype(BF16)
        qf = proj("qf") + _dot(c_packed, pq_ref[...]) + oq_ref[...]
        qf_ref[...] = qf.astype(BF16)
        qft_ref[0] = qf.T.astype(BF16)
        kf = proj("kf") - _dot(c_packed, pk_ref[...]) + ok_ref[...]
        kf_ref[...] = kf.astype(BF16)
        kft_ref[0] = kf.T.astype(BF16)
        row0 = (lax.broadcasted_iota(jnp.int32, (HEAD_DIM, TM), 0) == 0).astype(F32)
        zeros = jnp.zeros((HEAD_DIM, TM), F32)
        vf = proj("vf")
        vf_ref[...] = vf.astype(BF16)
        vft_ref[0] = _slot_rows(vf.T, row0).astype(BF16)
        qkvb_ref[0] = proj("qb").astype(BF16)
        kb = proj("kb")
        qkvb_ref[1] = kb.astype(BF16)
        kbt_ref[0] = _slot_rows(kb.T, zeros).astype(BF16)
        vb = proj("vb")
        qkvb_ref[2] = vb.astype(BF16)
        vbt_ref[0] = _slot_rows(vb.T, row0).astype(BF16)
        gl_ref[...] = proj("gates").astype(BF16)
        h_ref[...] = h

    n_tiles = t_len // TM
    slot_shape = jax.ShapeDtypeStruct((t_len, slot_w), BF16)
    t_shape = jax.ShapeDtypeStruct((n_tiles, slot_w, TM), BF16)
    t_spec = pl.BlockSpec((1, slot_w, TM), lambda i: (i, 0, 0))
    return pl.pallas_call(
        body, name="inproj_fwd", grid=(n_tiles,),
        out_shape=(slot_shape, slot_shape, t_shape, jax.ShapeDtypeStruct((t_len, D_BRANCH), BF16), t_shape,
                   jax.ShapeDtypeStruct((3, t_len, D_BRANCH), BF16), t_shape, t_shape,
                   jax.ShapeDtypeStruct((t_len, 2 * d), BF16), jax.ShapeDtypeStruct((t_len, LANES), F32),
                   jax.ShapeDtypeStruct((t_len, d), BF16), t_shape),
        in_specs=[_row_spec(TM, d), _const_spec((1, d)), _const_spec(w_pad.shape), _const_spec((1, LANES)),
                  _const_spec(place_q.shape), _const_spec(place_k.shape), _const_spec((1, slot_w)),
                  _const_spec((1, slot_w))],
        out_specs=(_row_spec(TM, slot_w), _row_spec(TM, slot_w), t_spec, _row_spec(TM, D_BRANCH), t_spec,
                   _row3_spec(3, TM, D_BRANCH), t_spec, t_spec, _row_spec(TM, 2 * d), _row_spec(TM, LANES),
                   _row_spec(TM, d), t_spec),
        scratch_shapes=[pltpu.VMEM((1, LANES), F32)],
        compiler_params=_seq_params(),
    )(x, g_mix, w_pad, bf_pad, place_q, place_k, ones_q, ones_k)


def _slot_spec(seq):
    return pl.BlockSpec((seq, NH * HEAD_SLOT), lambda b, g: (b, g))


def _group2_spec(seq):
    return pl.BlockSpec((2, seq, NH * HEAD_DIM), lambda b, g: (0, b, g))


def _group_spec(seq):
    return pl.BlockSpec((seq, NH * HEAD_DIM), lambda b, g: (b, g))


def _group3_spec(which, seq):
    return pl.BlockSpec((None, seq, NH * HEAD_DIM), lambda b, g: (which, b, g))


def _tblock_spec(seq):
    return pl.BlockSpec((seq // TK, NH * HEAD_SLOT, TK), lambda b, g: (b, g, 0))


def _qrow_spec(seq):
    return pl.BlockSpec((None, NH, seq // TQ, TQ), lambda b, g: (b, g, 0, 0))


def _attn_params():
    return pltpu.CompilerParams(dimension_semantics=("parallel", "parallel"), vmem_limit_bytes=VMEM_LIMIT)


def _serial_attn_params():
    return pltpu.CompilerParams(dimension_semantics=("arbitrary", "arbitrary"), vmem_limit_bytes=VMEM_LIMIT)


def _hcols(hh):
    return slice(hh * HEAD_DIM, (hh + 1) * HEAD_DIM)


def _hslot(hh):
    return slice(hh * HEAD_SLOT, (hh + 1) * HEAD_SLOT)


def _key_query_mask(rel):
    r = lax.broadcasted_iota(jnp.int32, (TK, TQ), 0)
    c = lax.broadcasted_iota(jnp.int32, (TK, TQ), 1)
    return rel(r, c)


def _heads_cat(vals):
    return jnp.concatenate(vals, axis=1)


def _untranspose(acc_t):
    return acc_t.T[:, :HEAD_DIM]


def _fox_fwd(qf, kf, vft, batch, seq):
    def body(q_ref, k_ref, vt_ref, o_ref, lse_ref, m_s, acc_s):
        causal = _key_query_mask(lambda r, c: r <= c)

        def tile(q0, kj, masked, n_k=1):
            krows = pl.ds(pl.multiple_of(kj * TK, TK), n_k * TK)
            heads = range(NH)
            sts = [_dot_nt(k_ref[krows, _hslot(hh)], q_ref[pl.ds(q0, TQ), _hslot(hh)]) for hh in heads]
            if masked:
                sts = [jnp.where(causal, st, NEG) for st in sts]
            m_olds = [m_s[hh] for hh in heads]
            m_news = [jnp.maximum(m_olds[hh], jnp.max(sts[hh], axis=0, keepdims=True)) for hh in heads]
            pts = [jnp.exp(sts[hh] - m_news[hh]).astype(BF16) for hh in heads]
            pvs = [sum(_dot(vt_ref[kj + i, _hslot(hh), :], pts[hh][i * TK:(i + 1) * TK]) for i in range(n_k))
                   for hh in heads]
            for hh in heads:
                acc_s[hh] = jnp.exp(m_olds[hh] - m_news[hh]) * acc_s[hh] + pvs[hh]
                m_s[hh] = m_news[hh]

        def q_loop(qi, _):
            q0 = pl.multiple_of(qi * TQ, TQ)
            m_s[...] = jnp.full(m_s.shape, NEG, F32)
            acc_s[...] = jnp.zeros_like(acc_s)

            def pair_loop(i, _):
                tile(q0, 2 * i, False, n_k=2)
                return 0

            lax.fori_loop(0, qi // 2, pair_loop, 0)
            pl.when(qi % 2 == 1)(lambda: tile(q0, qi - 1, False))
            tile(q0, qi, True)
            outs = []
            for hh in range(NH):
                total = acc_s[hh, HEAD_DIM:HEAD_DIM + 1, :]
                outs.append(_untranspose(acc_s[hh] / total))
                lse_ref[hh, pl.ds(qi, 1), :] = m_s[hh] + jnp.log(total)
            o_ref[pl.ds(q0, TQ), :] = _heads_cat(outs).astype(BF16)
            return 0

        lax.fori_loop(0, seq // TQ, q_loop, 0)

    return pl.pallas_call(
        body, name="fox_fwd", grid=(batch, N_HEADS // NH),
        out_shape=(jax.ShapeDtypeStruct((batch * seq, D_BRANCH), BF16),
                   jax.ShapeDtypeStruct((batch, N_HEADS, seq // TQ, TQ), F32)),
        in_specs=[_slot_spec(seq), _slot_spec(seq), _tblock_spec(seq)],
        out_specs=(_group_spec(seq), _qrow_spec(seq)),
        scratch_shapes=[pltpu.VMEM((NH, 1, TQ), F32), pltpu.VMEM((NH, HEAD_SLOT, TQ), F32)],
        compiler_params=_attn_params(),
    )(qf, kf, vft)


def _fox_bwd(qf, qft, kf, kft, vf, o, do, dot, lse, batch, seq, partials):
    n_q = seq // TQ
    n = len(partials)

    def body(q_ref, qt_ref, k_ref, kt_ref, v_ref, o_ref, do_ref, dot_ref, lse_ref, *rest):
        p_refs, (dqk_ref, dv_ref, dcq_ref, dck_ref), recv_refs = rest[:n], rest[n:n + 4], rest[n + 4:2 * n + 4]
        delta_s, dqt_acc, dk_s, dv_s = rest[2 * n + 4:2 * n + 8]
        pair_start, pair_finish = _pair_plan(p_refs, recv_refs, *rest[2 * n + 8:])
        first_step, last_step = _first_last_step()
        pl.when(first_step)(pair_start)
        causal = _key_query_mask(lambda r, c: r <= c)
        ones8 = jnp.ones((8, HEAD_DIM), BF16)
        dqt_acc[...] = jnp.zeros_like(dqt_acc)

        def prep(qi, _):
            rows = pl.ds(pl.multiple_of(qi * TQ, TQ), TQ)
            for hh in range(NH):
                hi, lo = _split2(do_ref[rows, _hcols(hh)].astype(F32) * o_ref[rows, _hcols(hh)].astype(F32))
                delta_s[hh, pl.ds(qi, 1), :] = (_dot_nt(ones8, hi) + _dot_nt(ones8, lo))[0:1, :]
            return 0

        lax.fori_loop(0, n_q, prep, 0)

        def tile(qis, kj, masked):
            krows = pl.ds(pl.multiple_of(kj * TK, TK), TK)
            heads = range(NH)
            items = [(t, hh) for t in range(len(qis)) for hh in heads]
            rows = [pl.ds(qi * TQ if isinstance(qi, int) else pl.multiple_of(qi * TQ, TQ), TQ) for qi in qis]
            sts = [_dot_nt(k_ref[krows, _hslot(hh)], q_ref[rows[t], _hslot(hh)]) for t, hh in items]
            dps = [_dot_nt(v_ref[krows, _hcols(hh)], do_ref[rows[t], _hcols(hh)]) for t, hh in items]
            pts = [jnp.exp(sts[i] - lse_ref[hh, pl.ds(qis[t], 1), :]) for i, (t, hh) in enumerate(items)]
            if masked:
                pts = [jnp.where(causal, pt, 0.0) for pt in pts]
            dsts = [(pts[i] * (dps[i] - delta_s[hh, pl.ds(qis[t], 1), :])).astype(BF16)
                    for i, (t, hh) in enumerate(items)]
            for i, (t, hh) in enumerate(items):
                dv_s[hh] += _dot_nt(dot_ref[qis[t], _hslot(hh), :], pts[i].astype(BF16))
                dk_s[hh] += _dot_nt(qt_ref[qis[t], _hslot(hh), :], dsts[i])
                dqt_acc[hh, qis[t]] += _dot(kt_ref[kj, _hslot(hh), :], dsts[i])

        def k_loop(kj, _):
            krows = pl.ds(pl.multiple_of(kj * TK, TK), TK)
            dk_s[...] = jnp.zeros_like(dk_s)
            dv_s[...] = jnp.zeros_like(dv_s)
            tile([kj], kj, True)
            left = n_q - 1 - kj

            def pair_loop(i, _):
                tile([kj + 1 + 2 * i, kj + 2 + 2 * i], kj, False)
                return 0

            lax.fori_loop(0, left // 2, pair_loop, 0)
            pl.when(left % 2 == 1)(lambda: tile([n_q - 1], kj, False))
            dqk_ref[1, krows, :] = _heads_cat([_untranspose(dk_s[hh]) for hh in range(NH)]).astype(BF16)
            dv_ref[krows, :] = _heads_cat([_untranspose(dv_s[hh]) for hh in range(NH)]).astype(BF16)
            for hh in range(NH):
                dck_ref[hh, pl.ds(kj, 1), :] = dk_s[hh, C_ONES_Q:C_ONES_Q + 1, :]
            return 0

        lax.fori_loop(0, seq // TK, k_loop, 0)

        def finish(qi, _):
            rows = pl.ds(pl.multiple_of(qi * TQ, TQ), TQ)
            dqk_ref[0, rows, :] = _heads_cat([_untranspose(dqt_acc[hh, qi]) for hh in range(NH)]).astype(BF16)
            for hh in range(NH):
                dcq_ref[hh, pl.ds(qi, 1), :] = dqt_acc[hh, qi, C_ONES_K:C_ONES_K + 1, :]
            return 0

        lax.fori_loop(0, n_q, finish, 0)
        pl.when(last_step)(pair_finish)

    out = pl.pallas_call(
        body, name="fox_bwd", grid=(batch, N_HEADS // NH),
        out_shape=[jax.ShapeDtypeStruct((2, batch * seq, D_BRANCH), BF16),
                   jax.ShapeDtypeStruct((batch * seq, D_BRANCH), BF16),
                   jax.ShapeDtypeStruct((batch, N_HEADS, seq // TQ, TQ), F32),
                   jax.ShapeDtypeStruct((batch, N_HEADS, seq // TK, TK), F32)] + _pair_shapes(partials),
        in_specs=[_slot_spec(seq), _tblock_spec(seq), _slot_spec(seq), _tblock_spec(seq), _group_spec(seq),
                  _group_spec(seq), _group_spec(seq), _tblock_spec(seq), _qrow_spec(seq)] + _hbm_specs(n),
        out_specs=[_group2_spec(seq), _group_spec(seq), _qrow_spec(seq), _qrow_spec(seq)] + _hbm_specs(n),
        scratch_shapes=[pltpu.VMEM((NH, n_q, TQ), F32), pltpu.VMEM((NH, n_q, HEAD_SLOT, TQ), F32),
                        pltpu.VMEM((NH, HEAD_SLOT, TK), F32), pltpu.VMEM((NH, HEAD_SLOT, TK), F32)] + _pair_sems(n),
        compiler_params=_serial_attn_params(),
    )(qf, qft, kf, kft, vf, o, do, dot, lse, *partials)
    return out[0], out[1], out[2], out[3], out[4:]


def _first_last_step():
    step = pl.program_id(0) * pl.num_programs(1) + pl.program_id(1)
    return step == 0, step == pl.num_programs(0) * pl.num_programs(1) - 1


def _sb_fwd(qkvb, vbt, batch, seq, shards):
    n = len(shards)

    def body(q_ref, k_ref, vt_ref, *rest):
        x_refs, (o_ref, lt_ref), out_refs = rest[:n], rest[n:n + 2], rest[n + 2:2 * n + 2]
        run_s, acc_s = rest[2 * n + 2:2 * n + 4]
        gather_start, gather_finish = _gather_plan_direct(x_refs, out_refs, *rest[2 * n + 4:])
        first_step, last_step = _first_last_step()
        pl.when(first_step)(gather_start)
        strict = _key_query_mask(lambda r, c: r < c)
        later = _tri(TK, lambda r, c: c > r)

        def tile(q0, kjs, masked):
            heads = range(NH)
            items = [(t, hh) for t in range(len(kjs)) for hh in heads]
            krows = [pl.ds(kj * TK if isinstance(kj, int) else pl.multiple_of(kj * TK, TK), TK) for kj in kjs]
            zts = [_dot_nt(k_ref[krows[t], _hcols(hh)], q_ref[pl.ds(q0, TQ), _hcols(hh)]) for t, hh in items]
            lgs = [-_softplus(zt) for zt in zts]
            if masked:
                lgs = [jnp.where(strict, lg, 0.0) for lg in lgs]
            parts = [_split2(lg) for lg in lgs]
            sufs = [_dot(later, hi) + _dot(later, lo) for hi, lo in parts]
            sums = [jnp.sum(lg, axis=0, keepdims=True) for lg in lgs]
            runs = {}
            for hh in heads:
                run = run_s[hh]
                for t in range(len(kjs)):
                    runs[t, hh] = run
                    run = run + sums[t * NH + hh]
                run_s[hh] = run
            ats = [jnp.exp(zts[i] + lgs[i] + runs[item] + sufs[i]) for i, item in enumerate(items)]
            if masked:
                ats = [jnp.where(strict, at, 0.0) for at in ats]
            for hh in heads:
                acc_s[hh] += sum(_dot(vt_ref[kjs[t], _hslot(hh), :], ats[t * NH + hh].astype(BF16))
                                 for t in range(len(kjs)))

        def q_loop(qi, _):
            q0 = pl.multiple_of(qi * TQ, TQ)
            run_s[...] = jnp.zeros_like(run_s)
            acc_s[...] = jnp.zeros_like(acc_s)
            tile(q0, [qi], True)

            def pair_loop(i, _):
                tile(q0, [qi - 1 - 2 * i, qi - 2 - 2 * i], False)
                return 0

            lax.fori_loop(0, qi // 2, pair_loop, 0)
            pl.when(qi % 2 == 1)(lambda: tile(q0, [0], False))
            o_ref[pl.ds(q0, TQ), :] = _heads_cat([_untranspose(acc_s[hh]) for hh in range(NH)]).astype(BF16)
            for hh in range(NH):
                lt_ref[hh, pl.ds(qi, 1), :] = run_s[hh]
            return 0

        lax.fori_loop(0, seq // TQ, q_loop, 0)
        pl.when(last_step)(gather_finish)

    out = pl.pallas_call(
        body, name="sb_fwd", grid=(batch, N_HEADS // NH),
        out_shape=[jax.ShapeDtypeStruct((batch * seq, D_BRANCH), BF16),
                   jax.ShapeDtypeStruct((batch, N_HEADS, seq // TQ, TQ), F32)] + _gather_shapes(shards),
        in_specs=[_group3_spec(0, seq), _group3_spec(1, seq), _tblock_spec(seq)] + _hbm_specs(n),
        out_specs=[_group_spec(seq), _qrow_spec(seq)] + _hbm_specs(n),
        scratch_shapes=[pltpu.VMEM((NH, 1, TQ), F32), pltpu.VMEM((NH, HEAD_SLOT, TQ), F32)] + _gather_sems(n),
        compiler_params=_serial_attn_params(),
    )(qkvb, qkvb, vbt, *shards)
    return out[0], out[1], out[2:]


def _sb_bwd(qkvb, kbt, do, ltot, batch, seq, chip_sums):
    n = len(chip_sums)

    def body(q_ref, k_ref, v_ref, kt_ref, do_ref, lt_ref, *rest):
        cs_refs, dqkv_ref, out_refs = rest[:n], rest[n], rest[n + 1:2 * n + 1]
        dk_acc, dv_acc, ls_s, gs_s, dqt_s = rest[2 * n + 1:2 * n + 6]
        chips_start, chips_finish = _chips_plan(cs_refs, out_refs, *rest[2 * n + 6:])
        first_step, last_step = _first_last_step()
        pl.when(first_step)(chips_start)
        strict = _key_query_mask(lambda r, c: r < c)
        upto = _tri(TK, lambda r, c: c <= r)
        before = _tri(TK, lambda r, c: c < r)
        dk_acc[...] = jnp.zeros_like(dk_acc)
        dv_acc[...] = jnp.zeros_like(dv_acc)

        def tile(qi, kj, masked):
            rows = pl.ds(pl.multiple_of(qi * TQ, TQ), TQ)
            krows = pl.ds(pl.multiple_of(kj * TK, TK), TK)
            heads = range(NH)
            qs = [q_ref[rows, _hcols(hh)] for hh in heads]
            douts = [do_ref[rows, _hcols(hh)] for hh in heads]
            zts = [_dot_nt(k_ref[krows, _hcols(hh)], qs[hh]) for hh in heads]
            das = [_dot_nt(v_ref[krows, _hcols(hh)], douts[hh]) for hh in heads]
            lgs = [-_softplus(zt) for zt in zts]
            if masked:
                lgs = [jnp.where(strict, lg, 0.0) for lg in lgs]
            parts = [_split2(lg) for lg in lgs]
            prefs = [_dot(upto, hi) + _dot(upto, lo) for hi, lo in parts]
            ats = [jnp.exp(zts[hh] + lgs[hh] + (lt_ref[hh, pl.ds(qi, 1), :] - ls_s[hh]) - prefs[hh]) for hh in heads]
            if masked:
                ats = [jnp.where(strict, at, 0.0) for at in ats]
            gts = [das[hh] * ats[hh] for hh in heads]
            us = [gs_s[hh] + _dot(before, gts[hh].astype(BF16)) for hh in heads]
            dzts = [(jnp.exp(lgs[hh]) * (gts[hh] + us[hh]) - us[hh]).astype(BF16) for hh in heads]
            for hh in heads:
                dk_acc[hh, krows, :] += _dot(dzts[hh], qs[hh])
                dv_acc[hh, krows, :] += _dot(ats[hh].astype(BF16), douts[hh])
                dqt_s[hh] += _dot(kt_ref[kj, _hslot(hh), :], dzts[hh])
                ls_s[hh] += jnp.sum(lgs[hh], axis=0, keepdims=True)
                gs_s[hh] += jnp.sum(gts[hh], axis=0, keepdims=True)

        def q_loop(qi, _):
            ls_s[...] = jnp.zeros_like(ls_s)
            gs_s[...] = jnp.zeros_like(gs_s)
            dqt_s[...] = jnp.zeros_like(dqt_s)

            def k_loop(kj, _):
                tile(qi, kj, False)
                return 0

            lax.fori_loop(0, qi, k_loop, 0)
            tile(qi, qi, True)
            dqkv_ref[0, pl.ds(pl.multiple_of(qi * TQ, TQ), TQ), :] = _heads_cat(
                [_untranspose(dqt_s[hh]) for hh in range(NH)]).astype(BF16)
            return 0

        lax.fori_loop(0, seq // TQ, q_loop, 0)
        dqkv_ref[1] = _heads_cat([dk_acc[hh] for hh in range(NH)]).astype(BF16)
        dqkv_ref[2] = _heads_cat([dv_acc[hh] for hh in range(NH)]).astype(BF16)
        pl.when(last_step)(chips_finish)

    out = pl.pallas_call(
        body, name="sb_bwd", grid=(batch, N_HEADS // NH),
        out_shape=[jax.ShapeDtypeStruct((3, batch * seq, D_BRANCH), BF16)]
        + [jax.ShapeDtypeStruct(s.shape, s.dtype) for s in chip_sums],
        in_specs=[_group3_spec(0, seq), _group3_spec(1, seq), _group3_spec(2, seq), _tblock_spec(seq),
                  _group_spec(seq), _qrow_spec(seq)] + _hbm_specs(n),
        out_specs=[pl.BlockSpec((3, seq, NH * HEAD_DIM), lambda b, g: (0, b, g))] + _hbm_specs(n),
        scratch_shapes=[pltpu.VMEM((NH, seq, HEAD_DIM), F32), pltpu.VMEM((NH, seq, HEAD_DIM), F32),
                        pltpu.VMEM((NH, 1, TQ), F32), pltpu.VMEM((NH, 1, TQ), F32),
                        pltpu.VMEM((NH, HEAD_SLOT, TQ), F32)] + _chips_sems(n),
        compiler_params=_serial_attn_params(),
    )(qkvb, qkvb, qkvb, kbt, do, ltot, *chip_sums)
    return out[0], out[1:]


def _forget_bwd(dcq_tok, dck_tok, fpre, batch, seq):
    t_len = batch * seq
    tiles = seq // TM

    def rev(i):
        return ((i // tiles) * tiles + (tiles - 1 - i % tiles), 0)

    def body(dcq_ref, dck_ref, f_ref, df_ref, db_ref, carry_ref):
        i = pl.program_id(0)

        @pl.when(i == 0)
        def _():
            db_ref[...] = jnp.zeros_like(db_ref)

        @pl.when(i % tiles == 0)
        def _():
            carry_ref[...] = jnp.zeros_like(carry_ref)

        dc = dcq_ref[...] - dck_ref[...]
        upper = _tri(TM, lambda r, c: c >= r)
        hi, mid, lo = _split3(dc)
        dlogf = carry_ref[...] + _dot(upper, hi) + _dot(upper, mid) + _dot(upper, lo)
        carry_ref[...] = carry_ref[...] + jnp.sum(dc, axis=0, keepdims=True)
        df = dlogf * _sigmoid(-f_ref[...])
        df_ref[...] = df.astype(BF16)
        db_ref[...] += jnp.sum(df, axis=0, keepdims=True)

    return pl.pallas_call(
        body, name="forget_bwd", grid=(t_len // TM,),
        out_shape=(jax.ShapeDtypeStruct((t_len, LANES), BF16), jax.ShapeDtypeStruct((1, LANES), F32)),
        in_specs=[pl.BlockSpec((TM, LANES), rev)] * 3,
        out_specs=(pl.BlockSpec((TM, LANES), rev), _acc_spec((1, LANES))),
        scratch_shapes=[pltpu.VMEM((1, LANES), F32)],
        compiler_params=_seq_params(),
    )(dcq_tok, dck_tok, fpre)


def _mix_fwd(o_fox, o_sb, gl, x, w_bf, w_bs, w_out, b_gate):
    t_len, d = x.shape

    def body(of_ref, os_ref, gl_ref, x_ref, wbf_ref, wbs_ref, wo_ref, bg_ref, x1_ref):
        br_f = _dot(of_ref[...], wbf_ref[...])
        br_s = _dot(os_ref[...], wbs_ref[...])
        ga = _sigmoid(gl_ref[:, :d].astype(F32) + bg_ref[0:1, :])
        gb = _sigmoid(gl_ref[:, d:].astype(F32) + bg_ref[1:2, :])
        merged = ga * br_f + gb * br_s
        x1_ref[...] = x_ref[...] + _dot(merged.astype(BF16), wo_ref[...])

    return pl.pallas_call(
        body, name="mix_fwd", grid=(t_len // TM,),
        out_shape=jax.ShapeDtypeStruct((t_len, d), F32),
        in_specs=[_row_spec(TM, D_BRANCH), _row_spec(TM, D_BRANCH), _row_spec(TM, 2 * d), _row_spec(TM, d),
                  _const_spec(w_bf.shape), _const_spec(w_bs.shape), _const_spec(w_out.shape), _const_spec(b_gate.shape)],
        out_specs=_row_spec(TM, d),
        compiler_params=_seq_params(),
    )(o_fox, o_sb, gl, x, w_bf, w_bs, w_out, b_gate)


def _ff_chunk(d_ff):
    return min(d_ff, 1024)


def _mlp_head_fwd_bwd(x1, p, target, g_mlp, w_up, w_down, g_ple, g_final, w_pg, w_ple):
    t_len, d = x1.shape
    d_ple = p.shape[1]
    d_ff = w_up.shape[1]
    ch = _ff_chunk(d_ff)

    def body(x1_ref, p_ref, t_ref, gm_ref, wu_ref, wd_ref, gp_ref, gf_ref, wpg_ref, wple_ref,
             a_ref, dx2_ref, h3_ref, dpre_ref, dpe_ref, loss_ref, dgp_ref, dgf_ref):
        @pl.when(pl.program_id(0) == 0)
        def _():
            loss_ref[...] = jnp.zeros_like(loss_ref)
            dgp_ref[...] = jnp.zeros_like(dgp_ref)
            dgf_ref[...] = jnp.zeros_like(dgf_ref)

        x1v = x1_ref[...]
        x1n, _ = _rms(x1v)
        h2 = (x1n * gm_ref[...]).astype(BF16)
        x2v = x1v
        for j in range(d_ff // ch):
            a = _dot(h2, wu_ref[:, j * ch:(j + 1) * ch])
            a_ref[:, j * ch:(j + 1) * ch] = a.astype(BF16)
            x2v = x2v + _dot(jnp.square(jnp.maximum(a, 0.0)).astype(BF16), wd_ref[j * ch:(j + 1) * ch, :])
        x2n, r3 = _rms(x2v)
        h3 = (x2n * gp_ref[...]).astype(BF16)
        h3_ref[...] = h3
        gate = _sigmoid(_dot(h3, wpg_ref[...]))
        pe = _dot(p_ref[...].astype(BF16), wple_ref[...])
        x3n, r4 = _rms(x2v + gate * pe)
        err = x3n * gf_ref[...] - t_ref[...]
        loss_ref[...] += jnp.full(loss_ref.shape, (0.5 / d) * jnp.sum(err * err), F32)
        dx3, dgf = _rms_bwd(err * (1.0 / d), x3n, r4, gf_ref[...])
        dgf_ref[...] += dgf
        dpe_ref[...] = (dx3 * gate).astype(BF16)
        dpre = (dx3 * pe * gate * (1.0 - gate)).astype(BF16)
        dpre_ref[...] = dpre
        dres, dgp = _rms_bwd(_dot_nt(dpre, wpg_ref[...]), x2n, r3, gp_ref[...])
        dgp_ref[...] += dgp
        dx2_ref[...] = dx3 + dres

    shp_b = jax.ShapeDtypeStruct((t_len, d), BF16)
    return pl.pallas_call(
        body, name="mlp_head_fwd_bwd", grid=(t_len // TM,),
        out_shape=(jax.ShapeDtypeStruct((t_len, d_ff), BF16), jax.ShapeDtypeStruct((t_len, d), F32), shp_b, shp_b, shp_b,
                   jax.ShapeDtypeStruct((1, LANES), F32), jax.ShapeDtypeStruct((1, d), F32),
                   jax.ShapeDtypeStruct((1, d), F32)),
        in_specs=[_row_spec(TM, d), _row_spec(TM, d_ple), _row_spec(TM, d), _const_spec((1, d)),
                  _const_spec(w_up.shape), _const_spec(w_down.shape), _const_spec((1, d)), _const_spec((1, d)),
                  _const_spec(w_pg.shape), _const_spec(w_ple.shape)],
        out_specs=(_row_spec(TM, d_ff), _row_spec(TM, d), _row_spec(TM, d), _row_spec(TM, d), _row_spec(TM, d),
                   _acc_spec((1, LANES)), _acc_spec((1, d)), _acc_spec((1, d))),
        compiler_params=_seq_params(),
    )(x1, p, target, g_mlp, w_up, w_down, g_ple, g_final, w_pg, w_ple)


def _mlp_bwd(dx2, a, x1, g_mlp, w_up, w_down):
    t_len, d = x1.shape
    d_ff = w_up.shape[1]
    ch = _ff_chunk(d_ff)

    def body(dx2_ref, a_ref, x1_ref, g_ref, wu_ref, wd_ref, dx1_ref, da_ref, h2_ref, dg_ref):
        @pl.when(pl.program_id(0) == 0)
        def _():
            dg_ref[...] = jnp.zeros_like(dg_ref)

        dx2v = dx2_ref[...]
        dx2b = dx2v.astype(BF16)
        xn, r = _rms(x1_ref[...])
        h2_ref[...] = (xn * g_ref[...]).T.astype(BF16)
        dh = jnp.zeros((TM, d), F32)
        for j in range(d_ff // ch):
            dact = _dot_nt(dx2b, wd_ref[j * ch:(j + 1) * ch, :])
            da = (dact * 2.0 * jnp.maximum(a_ref[:, j * ch:(j + 1) * ch].astype(F32), 0.0)).astype(BF16)
            da_ref[:, j * ch:(j + 1) * ch] = da
            dh = dh + _dot_nt(da, wu_ref[:, j * ch:(j + 1) * ch])
        dres, dg = _rms_bwd(dh, xn, r, g_ref[...])
        dg_ref[...] += dg
        dx1_ref[...] = dx2v + dres

    return pl.pallas_call(
        body, name="mlp_bwd", grid=(t_len // TM,),
        out_shape=(jax.ShapeDtypeStruct((t_len, d), F32), jax.ShapeDtypeStruct((t_len, d_ff), BF16),
                   jax.ShapeDtypeStruct((d, t_len), BF16), jax.ShapeDtypeStruct((1, d), F32)),
        in_specs=[_row_spec(TM, d), _row_spec(TM, d_ff), _row_spec(TM, d), _const_spec((1, d)),
                  _const_spec(w_up.shape), _const_spec(w_down.shape)],
        out_specs=(_row_spec(TM, d), _row_spec(TM, d_ff), _col_spec(d, TM), _acc_spec((1, d))),
        compiler_params=_seq_params(),
    )(dx2, a, x1, g_mlp, w_up, w_down)


def _mix_bwd(dx1, o_fox, o_sb, gl, w_bf, w_bs, w_out, b_gate):
    t_len, d = dx1.shape

    def body(dx1_ref, of_ref, os_ref, gl_ref, wbf_ref, wbs_ref, wo_ref, bg_ref,
             mg_ref, dbf_ref, dbs_ref, dgl_ref, dof_ref, dos_ref, dbg_ref, doft_ref):
        @pl.when(pl.program_id(0) == 0)
        def _():
            dbg_ref[...] = jnp.zeros_like(dbg_ref)

        dmerged = _dot_nt(dx1_ref[...].astype(BF16), wo_ref[...])
        br_f = _dot(of_ref[...], wbf_ref[...])
        br_s = _dot(os_ref[...], wbs_ref[...])
        ga = _sigmoid(gl_ref[:, :d].astype(F32) + bg_ref[0:1, :])
        gb = _sigmoid(gl_ref[:, d:].astype(F32) + bg_ref[1:2, :])
        mg_ref[...] = (ga * br_f + gb * br_s).astype(BF16)
        dbf = (dmerged * ga).astype(BF16)
        dbs = (dmerged * gb).astype(BF16)
        dbf_ref[...] = dbf
        dbs_ref[...] = dbs
        dla = dmerged * br_f * ga * (1.0 - ga)
        dlb = dmerged * br_s * gb * (1.0 - gb)
        dgl_ref[:, :d] = dla.astype(BF16)
        dgl_ref[:, d:] = dlb.astype(BF16)
        dbg_ref[0:1, :] += jnp.sum(dla, axis=0, keepdims=True)
        dbg_ref[1:2, :] += jnp.sum(dlb, axis=0, keepdims=True)
        dof = _dot_nt(dbf, wbf_ref[...])
        dof_ref[...] = dof.astype(BF16)
        doft_ref[0] = _slot_rows(dof.T, jnp.zeros((HEAD_DIM, TM), F32)).astype(BF16)
        dos_ref[...] = _dot_nt(dbs, wbs_ref[...]).astype(BF16)

    shp_d = jax.ShapeDtypeStruct((t_len, d), BF16)
    shp_h = jax.ShapeDtypeStruct((t_len, D_BRANCH), BF16)
    return pl.pallas_call(
        body, name="mix_bwd", grid=(t_len // TM,),
        out_shape=(shp_d, shp_d, shp_d, jax.ShapeDtypeStruct((t_len, 2 * d), BF16), shp_h, shp_h,
                   jax.ShapeDtypeStruct((2, d), F32),
                   jax.ShapeDtypeStruct((t_len // TM, N_HEADS * HEAD_SLOT, TM), BF16)),
        in_specs=[_row_spec(TM, d), _row_spec(TM, D_BRANCH), _row_spec(TM, D_BRANCH), _row_spec(TM, 2 * d),
                  _const_spec(w_bf.shape), _const_spec(w_bs.shape), _const_spec(w_out.shape), _const_spec(b_gate.shape)],
        out_specs=(_row_spec(TM, d), _row_spec(TM, d), _row_spec(TM, d), _row_spec(TM, 2 * d),
                   _row_spec(TM, D_BRANCH), _row_spec(TM, D_BRANCH), _acc_spec((2, d)),
                   pl.BlockSpec((1, N_HEADS * HEAD_SLOT, TM), lambda i: (i, 0, 0))),
        compiler_params=_seq_params(),
    )(dx1, o_fox, o_sb, gl, w_bf, w_bs, w_out, b_gate)


def _inproj_bwd(dqk_f, dv_f, dqkv_b, dgl, df, dx1, x, g_mix, w_pad, w_qk, chip_sums):
    t_len, d = x.shape
    lay, _ = _pad_layout(d)
    n = len(chip_sums)
    n_tiles = t_len // TM

    def body(dqk_ref, dvf_ref, db_ref, dgl_ref, df_ref, dx1_ref, x_ref, g_ref, w_ref, wqk_ref, *rest):
        cs_refs, (dx_ref, dg_ref), out_refs = rest[:n], rest[n:n + 2], rest[n + 2:2 * n + 2]
        chips_start, chips_finish = _chips_plan(cs_refs, out_refs, *rest[2 * n + 2:])

        @pl.when(pl.program_id(0) == 0)
        def _():
            dg_ref[...] = jnp.zeros_like(dg_ref)
            chips_start()

        def back(piece, name):
            lo, hi = lay[name]
            return _dot(piece, w_ref[lo:hi, :])

        xn, r = _rms(x_ref[...])
        dh = (back(df_ref[...], "forget") + back(dgl_ref[...], "gates") + _dot(dqk_ref[0], wqk_ref[:D_BRANCH, :])
              + _dot(dqk_ref[1], wqk_ref[D_BRANCH:, :]) + back(dvf_ref[...], "vf") + back(db_ref[0], "qb")
              + back(db_ref[1], "kb") + back(db_ref[2], "vb"))
        dres, dg = _rms_bwd(dh, xn, r, g_ref[...])
        dg_ref[...] += dg
        dx_ref[...] = dx1_ref[...] + dres
        pl.when(pl.program_id(0) == n_tiles - 1)(chips_finish)

    out = pl.pallas_call(
        body, name="inproj_bwd", grid=(n_tiles,),
        out_shape=[jax.ShapeDtypeStruct((t_len, d), F32), jax.ShapeDtypeStruct((1, d), F32)]
        + [jax.ShapeDtypeStruct(s.shape, s.dtype) for s in chip_sums],
        in_specs=[_row3_spec(2, TM, D_BRANCH), _row_spec(TM, D_BRANCH), _row3_spec(3, TM, D_BRANCH),
                  _row_spec(TM, 2 * d), _row_spec(TM, LANES), _row_spec(TM, d), _row_spec(TM, d), _const_spec((1, d)),
                  _const_spec(w_pad.shape), _const_spec(w_qk.shape)] + _hbm_specs(n),
        out_specs=[_row_spec(TM, d), _acc_spec((1, d))] + _hbm_specs(n),
        scratch_shapes=_chips_sems(n),
        compiler_params=_seq_params(),
    )(dqk_f, dv_f, dqkv_b, dgl, df, dx1, x, g_mix, w_pad, w_qk, *chip_sums)
    return out[0], out[1], out[2:]


def _cols_to_slabs(full):
    r, c8 = full.shape
    return full.reshape(r, N_DEV, c8 // N_DEV).transpose(1, 0, 2)


def _slabs_to_cols(slabs):
    n, r, c = slabs.shape
    return slabs.transpose(1, 0, 2).reshape(r, n * c)


def _win_sizes(d):
    return (D_BRANCH, D_BRANCH, D_BRANCH, N_HEADS, D_BRANCH, D_BRANCH, D_BRANCH, d, d)


def _split_win(w_t, d):
    out, off = [], 0
    for s in _win_sizes(d):
        out.append(w_t[off:off + s])
        off += s
    return out


def _to_slots(w_t):
    c = w_t.shape[1]
    return jnp.pad(w_t.reshape(N_HEADS, HEAD_DIM, c), ((0, 0), (0, HEAD_SLOT - HEAD_DIM), (0, 0))).reshape(-1, c)


def _pad_win(w_full_t, d):
    qa, ka, va, fa, qb, kb, vb, ga, gb = _split_win(w_full_t, d)
    scale = HEAD_DIM ** -0.5
    fpad = jnp.pad(fa, ((0, LANES - N_HEADS), (0, 0)))
    w_pad = jnp.concatenate([_to_slots(qa * scale), _to_slots(ka), va, qb * scale, kb, vb, ga, gb, fpad], axis=0)
    return w_pad, jnp.concatenate([qa * scale, ka], axis=0)


def _unpad_dwin(dqk_f, dv_f, dqkv_b, dgates, dforget, d):
    scale = HEAD_DIM ** -0.5
    return jnp.concatenate([dqk_f[0] * scale, dqk_f[1], dv_f, dforget[:N_HEADS],
                            dqkv_b[0] * scale, dqkv_b[1], dqkv_b[2], dgates], axis=0)


def _c_lane_constants():
    row = jnp.arange(LANES)[:, None]
    lane = jnp.arange(N_HEADS * HEAD_SLOT)[None, :]

    def place(first):
        return ((lane // HEAD_SLOT == row % N_HEADS) & (lane % HEAD_SLOT == first + row // N_HEADS)
                & (row < 3 * N_HEADS)).astype(BF16)

    def ones(first):
        off = lane % HEAD_SLOT
        return ((off >= first) & (off < first + 3)).astype(F32)

    return place(C_TERMS_Q), place(C_TERMS_K), ones(C_ONES_Q), ones(C_ONES_K)


def _pad_rows(a, rows):
    return jnp.pad(a, [(0, 0)] * (a.ndim - 2) + [(0, rows - a.shape[-2]), (0, 0)])


def kernel(x, p, g_mix, w_in, b_forget, b_gate, w_branch_fox, w_branch_sb, w_out, g_mlp, w_up, w_down, g_ple, w_ple_gate, w_ple, g_final, loss_target, m_g_mix, m_w_in, m_b_forget, m_b_gate, m_w_branch_fox, m_w_branch_sb, m_w_out, m_g_mlp, m_w_up, m_w_down, m_g_ple, m_w_ple_gate, m_w_ple, m_g_final, v_g_mix, v_w_in, v_b_forget, v_b_gate, v_w_branch_fox, v_w_branch_sb, v_w_out, v_g_mlp, v_w_up, v_w_down, v_g_ple, v_w_ple_gate, v_w_ple, v_g_final):
    batch, seq, d = x.shape
    t_len = batch * seq
    d_ple = p.shape[-1]
    d_ff = w_up.shape[-1] * N_DEV
    dn = d // N_DEV
    fn = d_ff // N_DEV
    my_c = lax.axis_index("c")
    my_dev = 4 * lax.axis_index("x") + 2 * lax.axis_index("y") + my_c

    bg_hi = b_gate[0].astype(BF16)
    bg_r = b_gate[0] - bg_hi.astype(F32)
    bg_mid = bg_r.astype(BF16)
    bg_lo = (bg_r - bg_mid.astype(F32)).astype(BF16)
    narrow_rows = 2 * D_BRANCH + d_ple + 6
    narrow_rows_pad = -(-narrow_rows // 16) * 16
    narrow = _pad_rows(jnp.concatenate(
        [w_branch_fox[0].astype(BF16), w_branch_sb[0].astype(BF16), w_ple[0].astype(BF16), bg_hi, bg_mid, bg_lo],
        axis=0), narrow_rows_pad)
    g_in, = _all_gather([w_in[0].T.astype(BF16)])
    w_pad, w_qk = _pad_win(g_in.reshape(-1, d), d)
    bf_pad = jnp.pad(b_forget, ((0, 0), (0, LANES - N_HEADS)))
    place_q, place_k, ones_q, ones_k = _c_lane_constants()

    x2d = x.reshape(t_len, d)
    p2d = p.reshape(t_len, d_ple)
    tgt2d = loss_target.reshape(t_len, d)
    qf, kf, kft, vf, vft, qkvb, kbt, vbt, gl, fpre, h1, qft = _inproj_fwd(
        x2d, g_mix, w_pad, bf_pad, place_q, place_k, ones_q, ones_k, seq)
    o_sb, ltot, (g_up, g_out, g_down, g_pg, g_narrow) = _sb_fwd(qkvb, vbt, batch, seq, [
        w_up[0].astype(BF16), w_out[0].astype(BF16), w_down[0].astype(BF16), w_ple_gate[0].astype(BF16), narrow])
    o_fox, lse = _fox_fwd(qf, kf, vft, batch, seq)
    w_up_full = _slabs_to_cols(g_up)
    w_out_full = g_out.reshape(d, d)
    w_down_full = g_down.reshape(d_ff, d)
    w_pg_full = g_pg.reshape(d, d)
    w_bf_full = _slabs_to_cols(g_narrow[:, :D_BRANCH])
    w_bs_full = _slabs_to_cols(g_narrow[:, D_BRANCH:2 * D_BRANCH])
    w_ple_full = _slabs_to_cols(g_narrow[:, 2 * D_BRANCH:2 * D_BRANCH + d_ple])
    bg_terms = g_narrow[:, 2 * D_BRANCH + d_ple:narrow_rows].astype(F32)
    b_gate_full = _slabs_to_cols(bg_terms[:, 0:2] + bg_terms[:, 2:4] + bg_terms[:, 4:6])
    x1 = _mix_fwd(o_fox, o_sb, gl, x2d, w_bf_full, w_bs_full, w_out_full, b_gate_full)

    a_up, dx2, h3, dpre, dpe, loss_acc, dg_ple, dg_final = _mlp_head_fwd_bwd(
        x1, p2d, tgt2d, g_mlp, w_up_full, w_down_full, g_ple, g_final.reshape(1, d), w_pg_full, w_ple_full)
    dx1, da_up, h2t, dg_mlp = _mlp_bwd(dx2, a_up, x1, g_mlp, w_up_full, w_down_full)
    merged, dbr_f, dbr_s, dgl, do_fox, do_sb, dbg, do_fox_t = _mix_bwd(
        dx1, o_fox, o_sb, gl, w_bf_full, w_bs_full, w_out_full, b_gate_full)

    def column_shards(name, lhs, rhs, lhs_t=False):
        if (rhs.shape[-1] // N_DEV) % (4 * LANES) == 0:
            return _matmul_tn(name, lhs, rhs, slabs=True, lhs_t=lhs_t)
        return _cols_to_slabs(_matmul_tn(name, lhs, rhs, lhs_t=lhs_t))

    if fn % (4 * LANES) == 0:
        part_up, = _matmul_tn_once("dw_up", [h2t], da_up, slabs=True, lhs_t=True)
    else:
        part_up = column_shards("dw_up", h2t, da_up, lhs_t=True)
    part_out = _matmul_tn("dw_out", merged, dx1).reshape(N_DEV, dn, d)
    part_down = _matmul_tn_once("dw_down", [a_up], dx2, relu2=True)[0].reshape(N_DEV, fn, d)
    part_pg = _matmul_tn("dw_ple_gate", h3, dpre).reshape(N_DEV, dn, d)
    part_narrow = _pad_rows(jnp.concatenate(
        [column_shards("dw_branch_fox", o_fox, dbr_f), column_shards("dw_branch_sb", o_sb, dbr_s),
         column_shards("dw_ple", p2d, dpe)], axis=1), narrow_rows_pad)
    early = [part_up, part_out, part_down, part_pg, lax.optimization_barrier(part_narrow)]

    dqk_f, dv_f, dc_queries, dc_keys, early_recv = _fox_bwd(
        qf, qft, kf, kft, vf, o_fox, do_fox, do_fox_t, lse, batch, seq, early)
    early_sums = [_pair_add("pair_add_%d" % i, pt, rc, my_c) for i, (pt, rc) in enumerate(zip(early, early_recv))]
    dqkv_b, (s_up, s_out, s_down, s_pg, s_narrow) = _sb_bwd(qkvb, kbt, do_sb, ltot, batch, seq, early_sums)
    dcq_tok = dc_queries.reshape(batch, N_HEADS, seq).transpose(0, 2, 1).reshape(t_len, N_HEADS)
    dck_tok = dc_keys.reshape(batch, N_HEADS, seq).transpose(0, 2, 1).reshape(t_len, N_HEADS)
    lane_pad = ((0, 0), (0, LANES - N_HEADS))
    df, db_forget = _forget_bwd(jnp.pad(dcq_tok, lane_pad), jnp.pad(dck_tok, lane_pad), fpre, batch, seq)

    gw_in = _unpad_dwin(*_matmul_tn_once("dw_in_fox_qk", [dqk_f], h1),
                        *_matmul_tn_once("dw_in_rest", [dv_f, dqkv_b, dgl, df], h1), d)
    part_in = lax.optimization_barrier(gw_in.reshape(N_DEV, -1, d))
    recv_in, = _rs_core_pair("reduce_scatter_core_pair_w_in", [part_in])
    grad_x, dg_mix, (s_in,) = _inproj_bwd(dqk_f, dv_f, dqkv_b, dgl, df, dx1, x2d, g_mix, w_pad, w_qk,
                                          [_pair_add("pair_add_w_in", part_in, recv_in, my_c)])

    small = jnp.concatenate([
        dg_mix, dg_mlp, dg_ple, dg_final, jnp.pad(db_forget[:, :N_HEADS], ((0, 0), (0, d - N_HEADS))), dbg,
        jnp.pad(loss_acc[:, :1], ((0, 0), (0, d - 1)))], axis=0)
    small = _all_reduce_small(small)
    loss = small[7, 0]
    small_grads = {
        "g_mix": small[0:1], "g_mlp": small[1:2], "g_ple": small[2:3], "g_final": small[3:4],
        "b_forget": small[4:5, :N_HEADS],
        "b_gate": lax.dynamic_slice_in_dim(small[5:7], my_dev * dn, dn, axis=1),
    }

    weights = {"g_mix": g_mix, "w_in": w_in, "b_forget": b_forget, "b_gate": b_gate, "w_branch_fox": w_branch_fox,
               "w_branch_sb": w_branch_sb, "w_out": w_out, "g_mlp": g_mlp, "w_up": w_up, "w_down": w_down,
               "g_ple": g_ple, "w_ple_gate": w_ple_gate, "w_ple": w_ple, "g_final": g_final}
    m_in = {"g_mix": m_g_mix, "w_in": m_w_in, "b_forget": m_b_forget, "b_gate": m_b_gate,
            "w_branch_fox": m_w_branch_fox, "w_branch_sb": m_w_branch_sb, "w_out": m_w_out, "g_mlp": m_g_mlp,
            "w_up": m_w_up, "w_down": m_w_down, "g_ple": m_g_ple, "w_ple_gate": m_w_ple_gate, "w_ple": m_w_ple,
            "g_final": m_g_final}
    v_in = {"g_mix": v_g_mix, "w_in": v_w_in, "b_forget": v_b_forget, "b_gate": v_b_gate,
            "w_branch_fox": v_w_branch_fox, "w_branch_sb": v_w_branch_sb, "w_out": v_w_out, "g_mlp": v_g_mlp,
            "w_up": v_w_up, "w_down": v_w_down, "g_ple": v_g_ple, "w_ple_gate": v_w_ple_gate, "w_ple": v_w_ple,
            "g_final": v_g_final}
    names = list(weights)

    def as2d(a):
        return a.reshape(-1, a.shape[-1])

    result = {}
    big = {"w_up": (s_up, 0), "w_out": (s_out, 0), "w_down": (s_down, 0), "w_ple_gate": (s_pg, 0),
           "w_branch_fox": (s_narrow, 0), "w_branch_sb": (s_narrow, D_BRANCH), "w_ple": (s_narrow, 2 * D_BRANCH)}
    for n, (parts, off) in big.items():
        result[n] = _adamw_parts("adamw_" + n, as2d(weights[n]), parts, off, as2d(m_in[n]), as2d(v_in[n]))
    result["w_in"] = tuple(r.T for r in _adamw_parts("adamw_w_in", w_in[0].T, s_in, 0, m_w_in[0].T, v_w_in[0].T))
    small_names = list(small_grads)
    small_out = _adamw_small([(as2d(weights[n]), small_grads[n], as2d(m_in[n]), as2d(v_in[n])) for n in small_names])
    for n, (dlt, nm, nv) in zip(small_names, small_out):
        result[n] = (small_grads[n], dlt, nm, nv)
    outs = [[result[n][k].reshape(weights[n].shape) for n in names] for k in range(4)]
    return (loss, grad_x.reshape(x.shape), *outs[0], *outs[1], *outs[2], *outs[3])
```

```python
import jax
import jax.numpy as jnp
from jax import lax
from jax.experimental import pallas as pl
from jax.experimental.pallas import tpu as pltpu

F32 = jnp.float32
BF16 = jnp.bfloat16

HEAD_DIM = 64
N_HEADS = 8
D_BRANCH = N_HEADS * HEAD_DIM
EPS = 1e-6
ADAM_LR = 0.001
ADAM_B1 = 0.9
ADAM_B2 = 0.999
ADAM_EPS = 1e-08
ADAM_WD = 0.01
ADAM_STEP = 10

N_DEV = 8
LANES = 128
TM = 256
TQ = 256
TK = 256
NH = 4
HEAD_SLOT = 128
C_TERMS_Q = 64
C_ONES_K = 64
C_TERMS_K = 67
C_ONES_Q = 67
NEG = -1e30
VMEM_LIMIT = 56 * 1024 * 1024
MESH = pl.DeviceIdType.MESH


def _dot(a, b):
    return jnp.dot(a, b, preferred_element_type=F32)


def _dot_nt(a, b):
    return lax.dot_general(a, b, (((1,), (1,)), ((), ())), preferred_element_type=F32)


def _dot_tn(a, b):
    return lax.dot_general(a, b, (((0,), (0,)), ((), ())), preferred_element_type=F32)


def _sigmoid(x):
    return 1.0 / (1.0 + jnp.exp(-x))


def _softplus(x):
    return jnp.maximum(x, 0.0) + jnp.log(1.0 + jnp.exp(-jnp.abs(x)))


def _split2(x):
    hi = x.astype(BF16)
    lo = (x - hi.astype(F32)).astype(BF16)
    return hi, lo


def _split3(x):
    hi = x.astype(BF16)
    r = x - hi.astype(F32)
    mid = r.astype(BF16)
    lo = (r - mid.astype(F32)).astype(BF16)
    return hi, mid, lo


def _tri(n, rel):
    r = lax.broadcasted_iota(jnp.int32, (n, n), 0)
    c = lax.broadcasted_iota(jnp.int32, (n, n), 1)
    return rel(r, c).astype(BF16)


def _rms(x):
    r = lax.rsqrt(jnp.mean(x * x, axis=-1, keepdims=True) + EPS)
    return x * r, r


def _rms_bwd(dh, xn, r, g):
    dxn = dh * g
    dx = r * (dxn - xn * jnp.mean(dxn * xn, axis=-1, keepdims=True))
    return dx, jnp.sum(dh * xn, axis=0, keepdims=True)


def _row_spec(tm, cols):
    return pl.BlockSpec((tm, cols), lambda i: (i, 0))


def _row3_spec(g, tm, cols):
    return pl.BlockSpec((g, tm, cols), lambda i: (0, i, 0))


def _col_spec(rows, tm):
    return pl.BlockSpec((rows, tm), lambda i: (0, i))


def _const_spec(shape):
    nd = len(shape)
    return pl.BlockSpec(shape, lambda i: (0,) * nd, pipeline_mode=pl.Buffered(1))


def _acc_spec(shape):
    nd = len(shape)
    return pl.BlockSpec(shape, lambda i: (0,) * nd)


def _seq_params():
    return pltpu.CompilerParams(dimension_semantics=("arbitrary",), vmem_limit_bytes=VMEM_LIMIT)


def _mesh_pos():
    return lax.axis_index("x"), lax.axis_index("y"), lax.axis_index("c")


def _other_chips(x, y):
    return [(1 - x, y), (x, 1 - y), (1 - x, 1 - y)]


def _hbm_specs(n):
    return [pl.BlockSpec(memory_space=pl.ANY)] * n


def _gather_plan(x_refs, out_refs, send_sems, recv_sems, local_sems):
    n = len(x_refs)
    x, y, c = _mesh_pos()
    me, sibling = (x, y, c), (x, y, 1 - c)
    x_nb, y_nb, diag = (1 - x, y), (x, 1 - y), (1 - x, 1 - y)

    def slab(a, chip, core):
        return out_refs[a].at[4 * chip[0] + 2 * chip[1] + core]

    def half(ref, upper):
        rows = ref.shape[0]
        cut = (rows // 2) // 16 * 16
        return ref.at[pl.ds(cut, rows - cut)] if upper else ref.at[pl.ds(0, cut)]

    def copy(a, k, ref, to, src=None):
        return pltpu.make_async_remote_copy(
            src_ref=ref if src is None else src, dst_ref=ref,
            send_sem=send_sems.at[8 * a + k], recv_sem=recv_sems.at[8 * a + k], device_id=to, device_id_type=MESH)

    mine = [pltpu.make_async_copy(x_refs[a], slab(a, (x, y), c), local_sems.at[a]) for a in range(n)]
    first = []
    for a in range(n):
        own = slab(a, (x, y), c)
        first += [copy(a, 0, own, sibling, src=x_refs[a]), copy(a, 1, own, (*x_nb, c), src=x_refs[a]),
                  copy(a, 2, own, (*y_nb, c), src=x_refs[a])]

    def start():
        for cp in mine + first:
            cp.start()

    def finish():
        passed = []

        def pass_on(cp):
            passed.append(cp)
            cp.start()

        for a in range(n):
            got = slab(a, x_nb, c)
            copy(a, 1, got, me).wait_recv()
            pass_on(copy(a, 5, got, sibling))
            pass_on(copy(a, 3, half(got, False), (*y_nb, c)))
        for a in range(n):
            got = slab(a, y_nb, c)
            copy(a, 2, got, me).wait_recv()
            pass_on(copy(a, 6, got, sibling))
            pass_on(copy(a, 4, half(got, True), (*x_nb, c)))
        for a in range(n):
            got = slab(a, diag, c)
            copy(a, 3, half(got, False), me).wait_recv()
            copy(a, 4, half(got, True), me).wait_recv()
            pass_on(copy(a, 7, got, sibling))
        for a in range(n):
            copy(a, 0, slab(a, (x, y), 1 - c), me).wait_recv()
            for k, chip in ((5, x_nb), (6, y_nb), (7, diag)):
                copy(a, k, slab(a, chip, 1 - c), me).wait_recv()
        for cp in first + passed:
            cp.wait_send()
        for cp in mine:
            cp.wait()

    return start, finish


def _gather_plan_direct(x_refs, out_refs, send_sems, recv_sems, local_sems):
    n = len(x_refs)
    x, y, c = _mesh_pos()
    me, sibling = (x, y, c), (x, y, 1 - c)
    chips = _other_chips(x, y)

    def index(px, py, pc):
        return 4 * px + 2 * py + pc

    def copy(a, k, block, to, src=None):
        slab = out_refs[a].at[index(*block)]
        return pltpu.make_async_remote_copy(
            src_ref=slab if src is None else src, dst_ref=slab,
            send_sem=send_sems.at[8 * a + k], recv_sem=recv_sems.at[8 * a + k], device_id=to, device_id_type=MESH)

    mine = [pltpu.make_async_copy(x_refs[a], out_refs[a].at[index(*me)], local_sems.at[a]) for a in range(n)]
    first = []
    for a in range(n):
        first.append(copy(a, 0, me, sibling, src=x_refs[a]))
        first += [copy(a, 1 + j, me, (cx, cy, c), src=x_refs[a]) for j, (cx, cy) in enumerate(chips)]

    def start():
        for cp in mine + first:
            cp.start()

    def finish():
        passed = []
        for j, (cx, cy) in enumerate(chips):
            for a in range(n):
                copy(a, 1 + j, (cx, cy, c), me).wait_recv()
                passed.append(copy(a, 4 + j, (cx, cy, c), sibling))
                passed[-1].start()
        for a in range(n):
            copy(a, 0, sibling, me).wait_recv()
            for j, (cx, cy) in enumerate(chips):
                copy(a, 4 + j, (cx, cy, 1 - c), me).wait_recv()
        for cp in first + passed:
            cp.wait_send()
        for cp in mine:
            cp.wait()

    return start, finish


def _gather_shapes(shards):
    return [jax.ShapeDtypeStruct((N_DEV,) + s.shape, s.dtype) for s in shards]


def _gather_sems(n):
    return [pltpu.SemaphoreType.DMA((8 * n,)), pltpu.SemaphoreType.DMA((8 * n,)), pltpu.SemaphoreType.DMA((n,))]


def _all_gather(shards):
    n = len(shards)

    def body(*refs):
        start, finish = _gather_plan(refs[:n], refs[n:2 * n], *refs[2 * n:])
        start()
        finish()

    return pl.pallas_call(
        body, name="all_gather_weights", out_shape=_gather_shapes(shards),
        in_specs=_hbm_specs(n), out_specs=_hbm_specs(n), scratch_shapes=_gather_sems(n),
    )(*shards)


def _pair_plan(p_refs, recv_refs, send_sems, recv_sems):
    n = len(p_refs)
    x, y, c = _mesh_pos()
    sibling = (x, y, 1 - c)

    def start():
        for a in range(n):
            for chip in range(4):
                pltpu.make_async_remote_copy(
                    src_ref=p_refs[a].at[2 * chip + (1 - c)], dst_ref=recv_refs[a].at[chip],
                    send_sem=send_sems.at[a], recv_sem=recv_sems.at[a], device_id=sibling, device_id_type=MESH).start()

    def finish():
        for a in range(n):
            pltpu.make_async_remote_copy(
                src_ref=recv_refs[a], dst_ref=recv_refs[a], send_sem=send_sems.at[a], recv_sem=recv_sems.at[a],
                device_id=sibling, device_id_type=MESH).wait()

    return start, finish


def _pair_shapes(partials):
    return [jax.ShapeDtypeStruct((4,) + s.shape[1:], s.dtype) for s in partials]


def _pair_sems(n):
    return [pltpu.SemaphoreType.DMA((n,)), pltpu.SemaphoreType.DMA((n,))]


def _rs_core_pair(name, partials):
    n = len(partials)

    def body(*refs):
        start, finish = _pair_plan(refs[:n], refs[n:2 * n], *refs[2 * n:])
        start()
        finish()

    return pl.pallas_call(
        body, name=name, out_shape=_pair_shapes(partials),
        in_specs=_hbm_specs(n), out_specs=_hbm_specs(n), scratch_shapes=_pair_sems(n),
    )(*partials)


def _chips_plan(cs_refs, out_refs, send_sems, recv_sems, local_sems):
    n = len(cs_refs)
    x, y, c = _mesh_pos()
    chip = 2 * x + y
    chips = _other_chips(x, y)
    mine = [pltpu.make_async_copy(cs_refs[a].at[chip], out_refs[a].at[chip], local_sems.at[a]) for a in range(n)]
    sends = [pltpu.make_async_remote_copy(
        src_ref=cs_refs[a].at[2 * cx + cy], dst_ref=out_refs[a].at[chip],
        send_sem=send_sems.at[3 * a + j], recv_sem=recv_sems.at[3 * a + j],
        device_id=(cx, cy, c), device_id_type=MESH) for a in range(n) for j, (cx, cy) in enumerate(chips)]

    def start():
        for cp in mine + sends:
            cp.start()

    def finish():
        for a in range(n):
            for j, (cx, cy) in enumerate(chips):
                pltpu.make_async_remote_copy(
                    src_ref=cs_refs[a].at[chip], dst_ref=out_refs[a].at[2 * cx + cy],
                    send_sem=send_sems.at[3 * a + j], recv_sem=recv_sems.at[3 * a + j],
                    device_id=(x, y, c), device_id_type=MESH).wait_recv()
        for cp in sends:
            cp.wait_send()
        for cp in mine:
            cp.wait()

    return start, finish


def _chips_sems(n):
    return [pltpu.SemaphoreType.DMA((3 * n,)), pltpu.SemaphoreType.DMA((3 * n,)), pltpu.SemaphoreType.DMA((n,))]


def _all_reduce_small(vec):
    rows, cols = vec.shape

    def body(x_ref, land_ref, sum_ref, send_sems, recv_sems):
        x, y, c = _mesh_pos()
        me = 4 * x + 2 * y + c
        land_ref[me] = x_ref[...]
        flips = [(fx, fy, fc) for fx in (0, 1) for fy in (0, 1) for fc in (0, 1)][1:]

        def flipped(f):
            return tuple((1 - v) if b else v for v, b in zip((x, y, c), f))

        sends = []
        for k, f in enumerate(flips):
            sends.append(pltpu.make_async_remote_copy(
                src_ref=x_ref, dst_ref=land_ref.at[me], send_sem=send_sems.at[k], recv_sem=recv_sems.at[k],
                device_id=flipped(f), device_id_type=MESH))
            sends[-1].start()
        for k, f in enumerate(flips):
            px, py, pc = flipped(f)
            pltpu.make_async_remote_copy(
                src_ref=x_ref, dst_ref=land_ref.at[4 * px + 2 * py + pc], send_sem=send_sems.at[k],
                recv_sem=recv_sems.at[k], device_id=(x, y, c), device_id_type=MESH).wait_recv()
        for cp in sends:
            cp.wait_send()
        total = land_ref[0]
        for d in range(1, N_DEV):
            total = total + land_ref[d]
        sum_ref[...] = total

    vm = pl.BlockSpec(memory_space=pltpu.VMEM)
    return pl.pallas_call(
        body, name="all_reduce_small",
        out_shape=(jax.ShapeDtypeStruct((N_DEV, rows, cols), F32), jax.ShapeDtypeStruct((rows, cols), F32)),
        in_specs=[vm], out_specs=(vm, vm),
        scratch_shapes=[pltpu.SemaphoreType.DMA((7,)), pltpu.SemaphoreType.DMA((7,))],
    )(vec)[1]


def _block_rows(rows, cols, itemsize, align, row_off=0):
    best = None
    for t in range(align, rows + 1, align):
        if rows % t == 0 and row_off % t == 0 and t * cols * itemsize <= (1 << 20):
            best = t
    return rows if best is None else best


def _pair_add(name, partial, recv, my_c):
    _, rows, cols = partial.shape
    br = _block_rows(rows, cols, 2, 16)

    def body(c_ref, a_ref, b_ref, o_ref):
        o_ref[...] = (a_ref[...].astype(F32) + b_ref[...].astype(F32)).astype(BF16)

    return pl.pallas_call(
        body, name=name,
        grid_spec=pltpu.PrefetchScalarGridSpec(
            num_scalar_prefetch=1, grid=(4, rows // br),
            in_specs=[pl.BlockSpec((None, None, br, cols), lambda j, i, c_ref: (j, c_ref[0], i, 0)),
                      pl.BlockSpec((None, br, cols), lambda j, i, c_ref: (j, i, 0))],
            out_specs=pl.BlockSpec((None, br, cols), lambda j, i, c_ref: (j, i, 0))),
        out_shape=jax.ShapeDtypeStruct((4, rows, cols), BF16),
    )(my_c.reshape(1).astype(jnp.int32), partial.reshape(4, 2, rows, cols), recv)


def _adam_update(w, g, m, v):
    nm = ADAM_B1 * m + (1.0 - ADAM_B1) * g
    nv = ADAM_B2 * v + (1.0 - ADAM_B2) * (g * g)
    m_hat = nm / (1.0 - ADAM_B1 ** ADAM_STEP)
    v_hat = nv / (1.0 - ADAM_B2 ** ADAM_STEP)
    return -ADAM_LR * (m_hat / (jnp.sqrt(v_hat) + ADAM_EPS) + ADAM_WD * w), nm, nv


def _adamw_parts(name, w, parts, row_off, m, v):
    rows, cols = w.shape
    tr = _block_rows(rows, cols, 4, 16, row_off)
    tc = cols
    if tr == rows and rows % 16 != 0 and cols % (2 * LANES) == 0:
        tc = 2 * LANES
    assert rows % tr == 0 and row_off % tr == 0 and (tc == cols or row_off == 0)
    off = row_off // tr

    def body(w_ref, p_ref, m_ref, v_ref, g_ref, d_ref, nm_ref, nv_ref):
        g = p_ref[0].astype(F32)
        for j in range(1, 4):
            g = g + p_ref[j].astype(F32)
        g_ref[...] = g
        d_ref[...], nm_ref[...], nv_ref[...] = _adam_update(w_ref[...], g, m_ref[...], v_ref[...])

    spec = pl.BlockSpec((tr, tc), lambda i, j: (i, j))
    shp = jax.ShapeDtypeStruct((rows, cols), F32)
    return pl.pallas_call(
        body, name=name, grid=(rows // tr, cols // tc), out_shape=(shp,) * 4,
        in_specs=[spec, pl.BlockSpec((4, tr, tc), lambda i, j: (0, off + i, j)), spec, spec], out_specs=(spec,) * 4,
    )(w, parts, m, v)


def _adamw_small(tensors):
    n = len(tensors)

    def body(*refs):
        ins, outs = refs[:4 * n], refs[4 * n:]
        for t in range(n):
            w_ref, g_ref, m_ref, v_ref = ins[4 * t:4 * t + 4]
            d, nm, nv = _adam_update(w_ref[...], g_ref[...], m_ref[...], v_ref[...])
            outs[3 * t][...], outs[3 * t + 1][...], outs[3 * t + 2][...] = d, nm, nv

    vm = pl.BlockSpec(memory_space=pltpu.VMEM)
    out = pl.pallas_call(
        body, name="adamw_small",
        out_shape=[jax.ShapeDtypeStruct(t[0].shape, F32) for t in tensors for _ in range(3)],
        in_specs=[vm] * (4 * n), out_specs=[vm] * (3 * n),
    )(*[a for t in tensors for a in t])
    return [tuple(out[3 * t:3 * t + 3]) for t in range(n)]


def _matmul_tn(name, a, b, relu2=False, slabs=False, lhs_t=False):
    a_groups = a.shape[0] if a.ndim == 3 else 0
    b_groups = b.shape[0] if b.ndim == 3 else 0
    groups = max(a_groups, b_groups, 1)
    assert not (a_groups and b_groups) and not (a_groups and lhs_t)
    a3 = a if a_groups else a[None]
    b3 = b if b_groups else b[None]
    t_len, k_len = a3.shape[1:][::-1] if lhs_t else a3.shape[1:]
    n_len = b3.shape[2]
    tt = min(t_len, 512)
    tk = min(k_len, 1024)
    tn = n_len // N_DEV if slabs else min(n_len, 1024)
    nt = t_len // tt
    assert not slabs or (groups == 1 and tn <= 1024)

    def body(a_ref, b_ref, o_ref, acc_ref):
        @pl.when(pl.program_id(3) == 0)
        def _():
            acc_ref[...] = jnp.zeros_like(acc_ref)

        av = a_ref[...]
        if relu2:
            av = jnp.square(jnp.maximum(av.astype(F32), 0.0))
        product = _dot if lhs_t else _dot_tn
        acc_ref[...] += product(av.astype(BF16), b_ref[...].astype(BF16))

        @pl.when(pl.program_id(3) == nt - 1)
        def _():
            o_ref[...] = acc_ref[...].astype(BF16)

    def a_group(g):
        return g if a_groups else 0

    def b_group(g):
        return g if b_groups else 0

    if slabs:
        out_shape = jax.ShapeDtypeStruct((N_DEV, k_len, tn), BF16)
        out_spec = pl.BlockSpec((None, tk, tn), lambda g, i, j, t: (j, i, 0))
    else:
        out_shape = jax.ShapeDtypeStruct((groups, k_len, n_len), BF16)
        out_spec = pl.BlockSpec((None, tk, tn), lambda g, i, j, t: (g, i, j))
    out = pl.pallas_call(
        body, name=name, grid=(groups, k_len // tk, n_len // tn, nt), out_shape=out_shape,
        in_specs=[pl.BlockSpec((None, tk, tt), lambda g, i, j, t: (a_group(g), i, t)) if lhs_t
                  else pl.BlockSpec((None, tt, tk), lambda g, i, j, t: (a_group(g), t, i)),
                  pl.BlockSpec((None, tt, tn), lambda g, i, j, t: (b_group(g), t, j))],
        out_specs=out_spec,
        scratch_shapes=[pltpu.VMEM((tk, tn), F32)],
        compiler_params=pltpu.CompilerParams(
            dimension_semantics=("parallel", "parallel", "parallel", "arbitrary"), vmem_limit_bytes=VMEM_LIMIT),
    )(a3, b3)
    return out if (slabs or a_groups or b_groups) else out[0]


def _matmul_tn_once(name, lhs_list, rhs, relu2=False, slabs=False, lhs_t=False):
    t_len, n_len = rhs.shape
    tt = min(t_len, 256 if relu2 else 512)
    nt = t_len // tt
    n_lhs = len(lhs_list)
    assert not (lhs_t or slabs) or (n_lhs == 1 and lhs_list[0].ndim == 2)
    k_shapes = [(a.shape[0], n_len) if lhs_t else a.shape[:-2] + (a.shape[-1], n_len) for a in lhs_list]
    tn = n_len // N_DEV

    def body(*refs):
        a_refs, b_ref = refs[:n_lhs], refs[n_lhs]
        o_refs, acc_refs = refs[n_lhs + 1:2 * n_lhs + 1], refs[2 * n_lhs + 1:]
        step = pl.program_id(0)

        @pl.when(step == 0)
        def _():
            for acc in acc_refs:
                acc[...] = jnp.zeros_like(acc)

        bv = b_ref[...].astype(BF16)

        def piece(av):
            if relu2:
                av = jnp.square(jnp.maximum(av.astype(F32), 0.0))
            return (_dot if lhs_t else _dot_tn)(av.astype(BF16), bv)

        for a_ref, acc in zip(a_refs, acc_refs):
            if len(acc.shape) == 3:
                for g in range(acc.shape[0]):
                    acc[g] += piece(a_ref[g])
            else:
                acc[...] += piece(a_ref[...])

        @pl.when(step == nt - 1)
        def _():
            for o_ref, acc in zip(o_refs, acc_refs):
                if slabs:
                    for j in range(N_DEV):
                        o_ref[j] = acc[:, j * tn:(j + 1) * tn].astype(BF16)
                else:
                    o_ref[...] = acc[...].astype(BF16)

    def lhs_spec(a):
        if lhs_t:
            return pl.BlockSpec((a.shape[0], tt), lambda t: (0, t))
        if a.ndim == 3:
            return pl.BlockSpec((a.shape[0], tt, a.shape[2]), lambda t: (0, t, 0))
        return pl.BlockSpec((tt, a.shape[1]), lambda t: (t, 0))

    out_shapes = [(N_DEV, k_shapes[0][0], tn)] if slabs else k_shapes
    return pl.pallas_call(
        body, name=name, grid=(nt,),
        out_shape=[jax.ShapeDtypeStruct(s, BF16) for s in out_shapes],
        in_specs=[lhs_spec(a) for a in lhs_list] + [pl.BlockSpec((tt, n_len), lambda t: (t, 0))],
        out_specs=[_acc_spec(s) for s in out_shapes],
        scratch_shapes=[pltpu.VMEM(s, F32) for s in k_shapes],
        compiler_params=_seq_params(),
    )(*lhs_list, rhs)


def _pad_layout(d):
    names = ("qf", "kf", "vf", "qb", "kb", "vb", "gates", "forget")
    sizes = (N_HEADS * HEAD_SLOT, N_HEADS * HEAD_SLOT, D_BRANCH, D_BRANCH, D_BRANCH, D_BRANCH, 2 * d, LANES)
    out, off = {}, 0
    for n, s in zip(names, sizes):
        out[n] = (off, off + s)
        off += s
    return out, off


def _slot_rows(xt, extra):
    parts = []
    for h in range(N_HEADS):
        parts += [xt[h * HEAD_DIM:(h + 1) * HEAD_DIM, :], extra]
    return jnp.concatenate(parts, axis=0)


def _inproj_fwd(x, g_mix, w_pad, bf_pad, place_q, place_k, ones_q, ones_k, seq):
    t_len, d = x.shape
    lay, _ = _pad_layout(d)
    tiles_per_seq = seq // TM
    slot_w = N_HEADS * HEAD_SLOT

    def body(x_ref, g_ref, w_ref, bf_ref, pq_ref, pk_ref, oq_ref, ok_ref,
             qf_ref, kf_ref, kft_ref, vf_ref, vft_ref, qkvb_ref, kbt_ref, vbt_ref, gl_ref, fpre_ref, h_ref, qft_ref,
             carry_ref):
        @pl.when(pl.program_id(0) % tiles_per_seq == 0)
        def _():
            carry_ref[...] = jnp.zeros_like(carry_ref)

        def proj(name):
            lo, hi = lay[name]
            return _dot_nt(h, w_ref[lo:hi, :])

        xn, _ = _rms(x_ref[...])
        h = (xn * g_ref[...]).astype(BF16)
        fpre = proj("forget") + bf_ref[...]
        fpre_ref[...] = fpre
        logf = -_softplus(-fpre)
        lower = _tri(TM, lambda r, c: c <= r)
        hi, mid, lo = _split3(logf)
        c_val = carry_ref[...] + _dot(lower, hi) + _dot(lower, mid) + _dot(lower, lo)
        carry_ref[...] = carry_ref[...] + jnp.sum(logf, axis=0, keepdims=True)
        head_lanes = lax.broadcasted_iota(jnp.int32, (TM, LANES), 1) < N_HEADS
        terms = [jnp.where(head_lanes, t.astype(F32), 0.0) for t in _split3(c_val)]
        c_packed = (terms[0] + pltpu.roll(terms[1], N_HEADS, 1) + pltpu.roll(terms[2], 2 * N_HEADS, 1)).astype(BF16)
        qf = proj("qf") + _dot(c_packed, pq_ref[...]) + oq_ref[...]
        qf_ref[...] = qf.astype(BF16)
        qft_ref[0] = qf.T.astype(BF16)
        kf = proj("kf") - _dot(c_packed, pk_ref[...]) + ok_ref[...]
        kf_ref[...] = kf.astype(BF16)
        kft_ref[0] = kf.T.astype(BF16)
        row0 = (lax.broadcasted_iota(jnp.int32, (HEAD_DIM, TM), 0) == 0).astype(F32)
        zeros = jnp.zeros((HEAD_DIM, TM), F32)
        vf = proj("vf")
        vf_ref[...] = vf.astype(BF16)
        vft_ref[0] = _slot_rows(vf.T, row0).astype(BF16)
        qkvb_ref[0] = proj("qb").astype(BF16)
        kb = proj("kb")
        qkvb_ref[1] = kb.astype(BF16)
        kbt_ref[0] = _slot_rows(kb.T, zeros).astype(BF16)
        vb = proj("vb")
        qkvb_ref[2] = vb.astype(BF16)
        vbt_ref[0] = _slot_rows(vb.T, row0).astype(BF16)
        gl_ref[...] = proj("gates").astype(BF16)
        h_ref[...] = h

    n_tiles = t_len // TM
    slot_shape = jax.ShapeDtypeStruct((t_len, slot_w), BF16)
    t_shape = jax.ShapeDtypeStruct((n_tiles, slot_w, TM), BF16)
    t_spec = pl.BlockSpec((1, slot_w, TM), lambda i: (i, 0, 0))
    return pl.pallas_call(
        body, name="inproj_fwd", grid=(n_tiles,),
        out_shape=(slot_shape, slot_shape, t_shape, jax.ShapeDtypeStruct((t_len, D_BRANCH), BF16), t_shape,
                   jax.ShapeDtypeStruct((3, t_len, D_BRANCH), BF16), t_shape, t_shape,
                   jax.ShapeDtypeStruct((t_len, 2 * d), BF16), jax.ShapeDtypeStruct((t_len, LANES), F32),
                   jax.ShapeDtypeStruct((t_len, d), BF16), t_shape),
        in_specs=[_row_spec(TM, d), _const_spec((1, d)), _const_spec(w_pad.shape), _const_spec((1, LANES)),
                  _const_spec(place_q.shape), _const_spec(place_k.shape), _const_spec((1, slot_w)),
                  _const_spec((1, slot_w))],
        out_specs=(_row_spec(TM, slot_w), _row_spec(TM, slot_w), t_spec, _row_spec(TM, D_BRANCH), t_spec,
                   _row3_spec(3, TM, D_BRANCH), t_spec, t_spec, _row_spec(TM, 2 * d), _row_spec(TM, LANES),
                   _row_spec(TM, d), t_spec),
        scratch_shapes=[pltpu.VMEM((1, LANES), F32)],
        compiler_params=_seq_params(),
    )(x, g_mix, w_pad, bf_pad, place_q, place_k, ones_q, ones_k)


def _slot_spec(seq):
    return pl.BlockSpec((seq, NH * HEAD_SLOT), lambda b, g: (b, g))


def _group2_spec(seq):
    return pl.BlockSpec((2, seq, NH * HEAD_DIM), lambda b, g: (0, b, g))


def _group_spec(seq):
    return pl.BlockSpec((seq, NH * HEAD_DIM), lambda b, g: (b, g))


def _group3_spec(which, seq):
    return pl.BlockSpec((None, seq, NH * HEAD_DIM), lambda b, g: (which, b, g))


def _tblock_spec(seq):
    return pl.BlockSpec((seq // TK, NH * HEAD_SLOT, TK), lambda b, g: (b, g, 0))


def _qrow_spec(seq):
    return pl.BlockSpec((None, NH, seq // TQ, TQ), lambda b, g: (b, g, 0, 0))


def _attn_params():
    return pltpu.CompilerParams(dimension_semantics=("parallel", "parallel"), vmem_limit_bytes=VMEM_LIMIT)


def _serial_attn_params():
    return pltpu.CompilerParams(dimension_semantics=("arbitrary", "arbitrary"), vmem_limit_bytes=VMEM_LIMIT)


def _hcols(hh):
    return slice(hh * HEAD_DIM, (hh + 1) * HEAD_DIM)


def _hslot(hh):
    return slice(hh * HEAD_SLOT, (hh + 1) * HEAD_SLOT)


def _key_query_mask(rel):
    r = lax.broadcasted_iota(jnp.int32, (TK, TQ), 0)
    c = lax.broadcasted_iota(jnp.int32, (TK, TQ), 1)
    return rel(r, c)


def _heads_cat(vals):
    return jnp.concatenate(vals, axis=1)


def _untranspose(acc_t):
    return acc_t.T[:, :HEAD_DIM]


def _fox_fwd(qf, kf, vft, batch, seq):
    def body(q_ref, k_ref, vt_ref, o_ref, lse_ref, m_s, acc_s):
        causal = _key_query_mask(lambda r, c: r <= c)

        def tile(q0, kj, masked, n_k=1):
            krows = pl.ds(pl.multiple_of(kj * TK, TK), n_k * TK)
            heads = range(NH)
            sts = [_dot_nt(k_ref[krows, _hslot(hh)], q_ref[pl.ds(q0, TQ), _hslot(hh)]) for hh in heads]
            if masked:
                sts = [jnp.where(causal, st, NEG) for st in sts]
            m_olds = [m_s[hh] for hh in heads]
            m_news = [jnp.maximum(m_olds[hh], jnp.max(sts[hh], axis=0, keepdims=True)) for hh in heads]
            pts = [jnp.exp(sts[hh] - m_news[hh]).astype(BF16) for hh in heads]
            pvs = [sum(_dot(vt_ref[kj + i, _hslot(hh), :], pts[hh][i * TK:(i + 1) * TK]) for i in range(n_k))
                   for hh in heads]
            for hh in heads:
                acc_s[hh] = jnp.exp(m_olds[hh] - m_news[hh]) * acc_s[hh] + pvs[hh]
                m_s[hh] = m_news[hh]

        def q_loop(qi, _):
            q0 = pl.multiple_of(qi * TQ, TQ)
            m_s[...] = jnp.full(m_s.shape, NEG, F32)
            acc_s[...] = jnp.zeros_like(acc_s)

            def pair_loop(i, _):
                tile(q0, 2 * i, False, n_k=2)
                return 0

            lax.fori_loop(0, qi // 2, pair_loop, 0)
            pl.when(qi % 2 == 1)(lambda: tile(q0, qi - 1, False))
            tile(q0, qi, True)
            outs = []
            for hh in range(NH):
                total = acc_s[hh, HEAD_DIM:HEAD_DIM + 1, :]
                outs.append(_untranspose(acc_s[hh] / total))
                lse_ref[hh, pl.ds(qi, 1), :] = m_s[hh] + jnp.log(total)
            o_ref[pl.ds(q0, TQ), :] = _heads_cat(outs).astype(BF16)
            return 0

        lax.fori_loop(0, seq // TQ, q_loop, 0)

    return pl.pallas_call(
        body, name="fox_fwd", grid=(batch, N_HEADS // NH),
        out_shape=(jax.ShapeDtypeStruct((batch * seq, D_BRANCH), BF16),
                   jax.ShapeDtypeStruct((batch, N_HEADS, seq // TQ, TQ), F32)),
        in_specs=[_slot_spec(seq), _slot_spec(seq), _tblock_spec(seq)],
        out_specs=(_group_spec(seq), _qrow_spec(seq)),
        scratch_shapes=[pltpu.VMEM((NH, 1, TQ), F32), pltpu.VMEM((NH, HEAD_SLOT, TQ), F32)],
        compiler_params=_attn_params(),
    )(qf, kf, vft)


def _fox_bwd(qf, qft, kf, kft, vf, o, do, dot, lse, batch, seq, partials):
    n_q = seq // TQ
    n = len(partials)

    def body(q_ref, qt_ref, k_ref, kt_ref, v_ref, o_ref, do_ref, dot_ref, lse_ref, *rest):
        p_refs, (dqk_ref, dv_ref, dcq_ref, dck_ref), recv_refs = rest[:n], rest[n:n + 4], rest[n + 4:2 * n + 4]
        delta_s, dqt_acc, dk_s, dv_s = rest[2 * n + 4:2 * n + 8]
        pair_start, pair_finish = _pair_plan(p_refs, recv_refs, *rest[2 * n + 8:])
        first_step, last_step = _first_last_step()
        pl.when(first_step)(pair_start)
        causal = _key_query_mask(lambda r, c: r <= c)
        ones8 = jnp.ones((8, HEAD_DIM), BF16)
        dqt_acc[...] = jnp.zeros_like(dqt_acc)

        def prep(qi, _):
            rows = pl.ds(pl.multiple_of(qi * TQ, TQ), TQ)
            for hh in range(NH):
                hi, lo = _split2(do_ref[rows, _hcols(hh)].astype(F32) * o_ref[rows, _hcols(hh)].astype(F32))
                delta_s[hh, pl.ds(qi, 1), :] = (_dot_nt(ones8, hi) + _dot_nt(ones8, lo))[0:1, :]
            return 0

        lax.fori_loop(0, n_q, prep, 0)

        def tile(qis, kj, masked):
            krows = pl.ds(pl.multiple_of(kj * TK, TK), TK)
            heads = range(NH)
            items = [(t, hh) for t in range(len(qis)) for hh in heads]
            rows = [pl.ds(qi * TQ if isinstance(qi, int) else pl.multiple_of(qi * TQ, TQ), TQ) for qi in qis]
            sts = [_dot_nt(k_ref[krows, _hslot(hh)], q_ref[rows[t], _hslot(hh)]) for t, hh in items]
            dps = [_dot_nt(v_ref[krows, _hcols(hh)], do_ref[rows[t], _hcols(hh)]) for t, hh in items]
            pts = [jnp.exp(sts[i] - lse_ref[hh, pl.ds(qis[t], 1), :]) for i, (t, hh) in enumerate(items)]
            if masked:
                pts = [jnp.where(causal, pt, 0.0) for pt in pts]
            dsts = [(pts[i] * (dps[i] - delta_s[hh, pl.ds(qis[t], 1), :])).astype(BF16)
                    for i, (t, hh) in enumerate(items)]
            for i, (t, hh) in enumerate(items):
                dv_s[hh] += _dot_nt(dot_ref[qis[t], _hslot(hh), :], pts[i].astype(BF16))
                dk_s[hh] += _dot_nt(qt_ref[qis[t], _hslot(hh), :], dsts[i])
                dqt_acc[hh, qis[t]] += _dot(kt_ref[kj, _hslot(hh), :], dsts[i])

        def k_loop(kj, _):
            krows = pl.ds(pl.multiple_of(kj * TK, TK), TK)
            dk_s[...] = jnp.zeros_like(dk_s)
            dv_s[...] = jnp.zeros_like(dv_s)
            tile([kj], kj, True)
            left = n_q - 1 - kj

            def pair_loop(i, _):
                tile([kj + 1 + 2 * i, kj + 2 + 2 * i], kj, False)
                return 0

            lax.fori_loop(0, left // 2, pair_loop, 0)
            pl.when(left % 2 == 1)(lambda: tile([n_q - 1], kj, False))
            dqk_ref[1, krows, :] = _heads_cat([_untranspose(dk_s[hh]) for hh in range(NH)]).astype(BF16)
            dv_ref[krows, :] = _heads_cat([_untranspose(dv_s[hh]) for hh in range(NH)]).astype(BF16)
            for hh in range(NH):
                dck_ref[hh, pl.ds(kj, 1), :] = dk_s[hh, C_ONES_Q:C_ONES_Q + 1, :]
            return 0

        lax.fori_loop(0, seq // TK, k_loop, 0)

        def finish(qi, _):
            rows = pl.ds(pl.multiple_of(qi * TQ, TQ), TQ)
            dqk_ref[0, rows, :] = _heads_cat([_untranspose(dqt_acc[hh, qi]) for hh in range(NH)]).astype(BF16)
            for hh in range(NH):
                dcq_ref[hh, pl.ds(qi, 1), :] = dqt_acc[hh, qi, C_ONES_K:C_ONES_K + 1, :]
            return 0

        lax.fori_loop(0, n_q, finish, 0)
        pl.when(last_step)(pair_finish)

    out = pl.pallas_call(
        body, name="fox_bwd", grid=(batch, N_HEADS // NH),
        out_shape=[jax.ShapeDtypeStruct((2, batch * seq, D_BRANCH), BF16),
                   jax.ShapeDtypeStruct((batch * seq, D_BRANCH), BF16),
                   jax.ShapeDtypeStruct((batch, N_HEADS, seq // TQ, TQ), F32),
                   jax.ShapeDtypeStruct((batch, N_HEADS, seq // TK, TK), F32)] + _pair_shapes(partials),
        in_specs=[_slot_spec(seq), _tblock_spec(seq), _slot_spec(seq), _tblock_spec(seq), _group_spec(seq),
                  _group_spec(seq), _group_spec(seq), _tblock_spec(seq), _qrow_spec(seq)] + _hbm_specs(n),
        out_specs=[_group2_spec(seq), _group_spec(seq), _qrow_spec(seq), _qrow_spec(seq)] + _hbm_specs(n),
        scratch_shapes=[pltpu.VMEM((NH, n_q, TQ), F32), pltpu.VMEM((NH, n_q, HEAD_SLOT, TQ), F32),
                        pltpu.VMEM((NH, HEAD_SLOT, TK), F32), pltpu.VMEM((NH, HEAD_SLOT, TK), F32)] + _pair_sems(n),
        compiler_params=_serial_attn_params(),
    )(qf, qft, kf, kft, vf, o, do, dot, lse, *partials)
    return out[0], out[1], out[2], out[3], out[4:]


def _first_last_step():
    step = pl.program_id(0) * pl.num_programs(1) + pl.program_id(1)
    return step == 0, step == pl.num_programs(0) * pl.num_programs(1) - 1


def _sb_fwd(qkvb, vbt, batch, seq, shards):
    n = len(shards)

    def body(q_ref, k_ref, vt_ref, *rest):
        x_refs, (o_ref, lt_ref), out_refs = rest[:n], rest[n:n + 2], rest[n + 2:2 * n + 2]
        run_s, acc_s = rest[2 * n + 2:2 * n + 4]
        gather_start, gather_finish = _gather_plan_direct(x_refs, out_refs, *rest[2 * n + 4:])
        first_step, last_step = _first_last_step()
        pl.when(first_step)(gather_start)
        strict = _key_query_mask(lambda r, c: r < c)
        later = _tri(TK, lambda r, c: c > r)

        def tile(q0, kjs, masked):
            heads = range(NH)
            items = [(t, hh) for t in range(len(kjs)) for hh in heads]
            krows = [pl.ds(kj * TK if isinstance(kj, int) else pl.multiple_of(kj * TK, TK), TK) for kj in kjs]
            zts = [_dot_nt(k_ref[krows[t], _hcols(hh)], q_ref[pl.ds(q0, TQ), _hcols(hh)]) for t, hh in items]
            lgs = [-_softplus(zt) for zt in zts]
            if masked:
                lgs = [jnp.where(strict, lg, 0.0) for lg in lgs]
            parts = [_split2(lg) for lg in lgs]
            sufs = [_dot(later, hi) + _dot(later, lo) for hi, lo in parts]
            sums = [jnp.sum(lg, axis=0, keepdims=True) for lg in lgs]
            runs = {}
            for hh in heads:
                run = run_s[hh]
                for t in range(len(kjs)):
                    runs[t, hh] = run
                    run = run + sums[t * NH + hh]
                run_s[hh] = run
            ats = [jnp.exp(zts[i] + lgs[i] + runs[item] + sufs[i]) for i, item in enumerate(items)]
            if masked:
                ats = [jnp.where(strict, at, 0.0) for at in ats]
            for hh in heads:
                acc_s[hh] += sum(_dot(vt_ref[kjs[t], _hslot(hh), :], ats[t * NH + hh].astype(BF16))
                                 for t in range(len(kjs)))

        def q_loop(qi, _):
            q0 = pl.multiple_of(qi * TQ, TQ)
            run_s[...] = jnp.zeros_like(run_s)
            acc_s[...] = jnp.zeros_like(acc_s)
            tile(q0, [qi], True)

            def pair_loop(i, _):
                tile(q0, [qi - 1 - 2 * i, qi - 2 - 2 * i], False)
                return 0

            lax.fori_loop(0, qi // 2, pair_loop, 0)
            pl.when(qi % 2 == 1)(lambda: tile(q0, [0], False))
            o_ref[pl.ds(q0, TQ), :] = _heads_cat([_untranspose(acc_s[hh]) for hh in range(NH)]).astype(BF16)
            for hh in range(NH):
                lt_ref[hh, pl.ds(qi, 1), :] = run_s[hh]
            return 0

        lax.fori_loop(0, seq // TQ, q_loop, 0)
        pl.when(last_step)(gather_finish)

    out = pl.pallas_call(
        body, name="sb_fwd", grid=(batch, N_HEADS // NH),
        out_shape=[jax.ShapeDtypeStruct((batch * seq, D_BRANCH), BF16),
                   jax.ShapeDtypeStruct((batch, N_HEADS, seq // TQ, TQ), F32)] + _gather_shapes(shards),
        in_specs=[_group3_spec(0, seq), _group3_spec(1, seq), _tblock_spec(seq)] + _hbm_specs(n),
        out_specs=[_group_spec(seq), _qrow_spec(seq)] + _hbm_specs(n),
        scratch_shapes=[pltpu.VMEM((NH, 1, TQ), F32), pltpu.VMEM((NH, HEAD_SLOT, TQ), F32)] + _gather_sems(n),
        compiler_params=_serial_attn_params(),
    )(qkvb, qkvb, vbt, *shards)
    return out[0], out[1], out[2:]


def _sb_bwd(qkvb, kbt, do, ltot, batch, seq, chip_sums):
    n = len(chip_sums)

    def body(q_ref, k_ref, v_ref, kt_ref, do_ref, lt_ref, *rest):
        cs_refs, dqkv_ref, out_refs = rest[:n], rest[n], rest[n + 1:2 * n + 1]
        dk_acc, dv_acc, ls_s, gs_s, dqt_s = rest[2 * n + 1:2 * n + 6]
        chips_start, chips_finish = _chips_plan(cs_refs, out_refs, *rest[2 * n + 6:])
        first_step, last_step = _first_last_step()
        pl.when(first_step)(chips_start)
        strict = _key_query_mask(lambda r, c: r < c)
        upto = _tri(TK, lambda r, c: c <= r)
        before = _tri(TK, lambda r, c: c < r)
        dk_acc[...] = jnp.zeros_like(dk_acc)
        dv_acc[...] = jnp.zeros_like(dv_acc)

        def tile(qi, kj, masked):
            rows = pl.ds(pl.multiple_of(qi * TQ, TQ), TQ)
            krows = pl.ds(pl.multiple_of(kj * TK, TK), TK)
            heads = range(NH)
            qs = [q_ref[rows, _hcols(hh)] for hh in heads]
            douts = [do_ref[rows, _hcols(hh)] for hh in heads]
            zts = [_dot_nt(k_ref[krows, _hcols(hh)], qs[hh]) for hh in heads]
            das = [_dot_nt(v_ref[krows, _hcols(hh)], douts[hh]) for hh in heads]
            lgs = [-_softplus(zt) for zt in zts]
            if masked:
                lgs = [jnp.where(strict, lg, 0.0) for lg in lgs]
            parts = [_split2(lg) for lg in lgs]
            prefs = [_dot(upto, hi) + _dot(upto, lo) for hi, lo in parts]
            ats = [jnp.exp(zts[hh] + lgs[hh] + (lt_ref[hh, pl.ds(qi, 1), :] - ls_s[hh]) - prefs[hh]) for hh in heads]
            if masked:
                ats = [jnp.where(strict, at, 0.0) for at in ats]
            gts = [das[hh] * ats[hh] for hh in heads]
            us = [gs_s[hh] + _dot(before, gts[hh].astype(BF16)) for hh in heads]
            dzts = [(jnp.exp(lgs[hh]) * (gts[hh] + us[hh]) - us[hh]).astype(BF16) for hh in heads]
            for hh in heads:
                dk_acc[hh, krows, :] += _dot(dzts[hh], qs[hh])
                dv_acc[hh, krows, :] += _dot(ats[hh].astype(BF16), douts[hh])
                dqt_s[hh] += _dot(kt_ref[kj, _hslot(hh), :], dzts[hh])
                ls_s[hh] += jnp.sum(lgs[hh], axis=0, keepdims=True)
                gs_s[hh] += jnp.sum(gts[hh], axis=0, keepdims=True)

        def q_loop(qi, _):
            ls_s[...] = jnp.zeros_like(ls_s)
            gs_s[...] = jnp.zeros_like(gs_s)
            dqt_s[...] = jnp.zeros_like(dqt_s)

            def k_loop(kj, _):
                tile(qi, kj, False)
                return 0

            lax.fori_loop(0, qi, k_loop, 0)
            tile(qi, qi, True)
            dqkv_ref[0, pl.ds(pl.multiple_of(qi * TQ, TQ), TQ), :] = _heads_cat(
                [_untranspose(dqt_s[hh]) for hh in range(NH)]).astype(BF16)
            return 0

        lax.fori_loop(0, seq // TQ, q_loop, 0)
        dqkv_ref[1] = _heads_cat([dk_acc[hh] for hh in range(NH)]).astype(BF16)
        dqkv_ref[2] = _heads_cat([dv_acc[hh] for hh in range(NH)]).astype(BF16)
        pl.when(last_step)(chips_finish)

    out = pl.pallas_call(
        body, name="sb_bwd", grid=(batch, N_HEADS // NH),
        out_shape=[jax.ShapeDtypeStruct((3, batch * seq, D_BRANCH), BF16)]
        + [jax.ShapeDtypeStruct(s.shape, s.dtype) for s in chip_sums],
        in_specs=[_group3_spec(0, seq), _group3_spec(1, seq), _group3_spec(2, seq), _tblock_spec(seq),
                  _group_spec(seq), _qrow_spec(seq)] + _hbm_specs(n),
        out_specs=[pl.BlockSpec((3, seq, NH * HEAD_DIM), lambda b, g: (0, b, g))] + _hbm_specs(n),
        scratch_shapes=[pltpu.VMEM((NH, seq, HEAD_DIM), F32), pltpu.VMEM((NH, seq, HEAD_DIM), F32),
                        pltpu.VMEM((NH, 1, TQ), F32), pltpu.VMEM((NH, 1, TQ), F32),
                        pltpu.VMEM((NH, HEAD_SLOT, TQ), F32)] + _chips_sems(n),
        compiler_params=_serial_attn_params(),
    )(qkvb, qkvb, qkvb, kbt, do, ltot, *chip_sums)
    return out[0], out[1:]


def _forget_bwd(dcq_tok, dck_tok, fpre, batch, seq):
    t_len = batch * seq
    tiles = seq // TM

    def rev(i):
        return ((i // tiles) * tiles + (tiles - 1 - i % tiles), 0)

    def body(dcq_ref, dck_ref, f_ref, df_ref, db_ref, carry_ref):
        i = pl.program_id(0)

        @pl.when(i == 0)
        def _():
            db_ref[...] = jnp.zeros_like(db_ref)

        @pl.when(i % tiles == 0)
        def _():
            carry_ref[...] = jnp.zeros_like(carry_ref)

        dc = dcq_ref[...] - dck_ref[...]
        upper = _tri(TM, lambda r, c: c >= r)
        hi, mid, lo = _split3(dc)
        dlogf = carry_ref[...] + _dot(upper, hi) + _dot(upper, mid) + _dot(upper, lo)
        carry_ref[...] = carry_ref[...] + jnp.sum(dc, axis=0, keepdims=True)
        df = dlogf * _sigmoid(-f_ref[...])
        df_ref[...] = df.astype(BF16)
        db_ref[...] += jnp.sum(df, axis=0, keepdims=True)

    return pl.pallas_call(
        body, name="forget_bwd", grid=(t_len // TM,),
        out_shape=(jax.ShapeDtypeStruct((t_len, LANES), BF16), jax.ShapeDtypeStruct((1, LANES), F32)),
        in_specs=[pl.BlockSpec((TM, LANES), rev)] * 3,
        out_specs=(pl.BlockSpec((TM, LANES), rev), _acc_spec((1, LANES))),
        scratch_shapes=[pltpu.VMEM((1, LANES), F32)],
        compiler_params=_seq_params(),
    )(dcq_tok, dck_tok, fpre)


def _mix_fwd(o_fox, o_sb, gl, x, w_bf, w_bs, w_out, b_gate):
    t_len, d = x.shape

    def body(of_ref, os_ref, gl_ref, x_ref, wbf_ref, wbs_ref, wo_ref, bg_ref, x1_ref):
        br_f = _dot(of_ref[...], wbf_ref[...])
        br_s = _dot(os_ref[...], wbs_ref[...])
        ga = _sigmoid(gl_ref[:, :d].astype(F32) + bg_ref[0:1, :])
        gb = _sigmoid(gl_ref[:, d:].astype(F32) + bg_ref[1:2, :])
        merged = ga * br_f + gb * br_s
        x1_ref[...] = x_ref[...] + _dot(merged.astype(BF16), wo_ref[...])

    return pl.pallas_call(
        body, name="mix_fwd", grid=(t_len // TM,),
        out_shape=jax.ShapeDtypeStruct((t_len, d), F32),
        in_specs=[_row_spec(TM, D_BRANCH), _row_spec(TM, D_BRANCH), _row_spec(TM, 2 * d), _row_spec(TM, d),
                  _const_spec(w_bf.shape), _const_spec(w_bs.shape), _const_spec(w_out.shape), _const_spec(b_gate.shape)],
        out_specs=_row_spec(TM, d),
        compiler_params=_seq_params(),
    )(o_fox, o_sb, gl, x, w_bf, w_bs, w_out, b_gate)


def _ff_chunk(d_ff):
    return min(d_ff, 1024)


def _mlp_head_fwd_bwd(x1, p, target, g_mlp, w_up, w_down, g_ple, g_final, w_pg, w_ple):
    t_len, d = x1.shape
    d_ple = p.shape[1]
    d_ff = w_up.shape[1]
    ch = _ff_chunk(d_ff)

    def body(x1_ref, p_ref, t_ref, gm_ref, wu_ref, wd_ref, gp_ref, gf_ref, wpg_ref, wple_ref,
             a_ref, dx2_ref, h3_ref, dpre_ref, dpe_ref, loss_ref, dgp_ref, dgf_ref):
        @pl.when(pl.program_id(0) == 0)
        def _():
            loss_ref[...] = jnp.zeros_like(loss_ref)
            dgp_ref[...] = jnp.zeros_like(dgp_ref)
            dgf_ref[...] = jnp.zeros_like(dgf_ref)

        x1v = x1_ref[...]
        x1n, _ = _rms(x1v)
        h2 = (x1n * gm_ref[...]).astype(BF16)
        x2v = x1v
        for j in range(d_ff // ch):
            a = _dot(h2, wu_ref[:, j * ch:(j + 1) * ch])
            a_ref[:, j * ch:(j + 1) * ch] = a.astype(BF16)
            x2v = x2v + _dot(jnp.square(jnp.maximum(a, 0.0)).astype(BF16), wd_ref[j * ch:(j + 1) * ch, :])
        x2n, r3 = _rms(x2v)
        h3 = (x2n * gp_ref[...]).astype(BF16)
        h3_ref[...] = h3
        gate = _sigmoid(_dot(h3, wpg_ref[...]))
        pe = _dot(p_ref[...].astype(BF16), wple_ref[...])
        x3n, r4 = _rms(x2v + gate * pe)
        err = x3n * gf_ref[...] - t_ref[...]
        loss_ref[...] += jnp.full(loss_ref.shape, (0.5 / d) * jnp.sum(err * err), F32)
        dx3, dgf = _rms_bwd(err * (1.0 / d), x3n, r4, gf_ref[...])
        dgf_ref[...] += dgf
        dpe_ref[...] = (dx3 * gate).astype(BF16)
        dpre = (dx3 * pe * gate * (1.0 - gate)).astype(BF16)
        dpre_ref[...] = dpre
        dres, dgp = _rms_bwd(_dot_nt(dpre, wpg_ref[...]), x2n, r3, gp_ref[...])
        dgp_ref[...] += dgp
        dx2_ref[...] = dx3 + dres

    shp_b = jax.ShapeDtypeStruct((t_len, d), BF16)
    return pl.pallas_call(
        body, name="mlp_head_fwd_bwd", grid=(t_len // TM,),
        out_shape=(jax.ShapeDtypeStruct((t_len, d_ff), BF16), jax.ShapeDtypeStruct((t_len, d), F32), shp_b, shp_b, shp_b,
                   jax.ShapeDtypeStruct((1, LANES), F32), jax.ShapeDtypeStruct((1, d), F32),
                   jax.ShapeDtypeStruct((1, d), F32)),
        in_specs=[_row_spec(TM, d), _row_spec(TM, d_ple), _row_spec(TM, d), _const_spec((1, d)),
                  _const_spec(w_up.shape), _const_spec(w_down.shape), _const_spec((1, d)), _const_spec((1, d)),
                  _const_spec(w_pg.shape), _const_spec(w_ple.shape)],
        out_specs=(_row_spec(TM, d_ff), _row_spec(TM, d), _row_spec(TM, d), _row_spec(TM, d), _row_spec(TM, d),
                   _acc_spec((1, LANES)), _acc_spec((1, d)), _acc_spec((1, d))),
        compiler_params=_seq_params(),
    )(x1, p, target, g_mlp, w_up, w_down, g_ple, g_final, w_pg, w_ple)


def _mlp_bwd(dx2, a, x1, g_mlp, w_up, w_down):
    t_len, d = x1.shape
    d_ff = w_up.shape[1]
    ch = _ff_chunk(d_ff)

    def body(dx2_ref, a_ref, x1_ref, g_ref, wu_ref, wd_ref, dx1_ref, da_ref, h2_ref, dg_ref):
        @pl.when(pl.program_id(0) == 0)
        def _():
            dg_ref[...] = jnp.zeros_like(dg_ref)

        dx2v = dx2_ref[...]
        dx2b = dx2v.astype(BF16)
        xn, r = _rms(x1_ref[...])
        h2_ref[...] = (xn * g_ref[...]).T.astype(BF16)
        dh = jnp.zeros((TM, d), F32)
        for j in range(d_ff // ch):
            dact = _dot_nt(dx2b, wd_ref[j * ch:(j + 1) * ch, :])
            da = (dact * 2.0 * jnp.maximum(a_ref[:, j * ch:(j + 1) * ch].astype(F32), 0.0)).astype(BF16)
            da_ref[:, j * ch:(j + 1) * ch] = da
            dh = dh + _dot_nt(da, wu_ref[:, j * ch:(j + 1) * ch])
        dres, dg = _rms_bwd(dh, xn, r, g_ref[...])
        dg_ref[...] += dg
        dx1_ref[...] = dx2v + dres

    return pl.pallas_call(
        body, name="mlp_bwd", grid=(t_len // TM,),
        out_shape=(jax.ShapeDtypeStruct((t_len, d), F32), jax.ShapeDtypeStruct((t_len, d_ff), BF16),
                   jax.ShapeDtypeStruct((d, t_len), BF16), jax.ShapeDtypeStruct((1, d), F32)),
        in_specs=[_row_spec(TM, d), _row_spec(TM, d_ff), _row_spec(TM, d), _const_spec((1, d)),
                  _const_spec(w_up.shape), _const_spec(w_down.shape)],
        out_specs=(_row_spec(TM, d), _row_spec(TM, d_ff), _col_spec(d, TM), _acc_spec((1, d))),
        compiler_params=_seq_params(),
    )(dx2, a, x1, g_mlp, w_up, w_down)


def _mix_bwd(dx1, o_fox, o_sb, gl, w_bf, w_bs, w_out, b_gate):
    t_len, d = dx1.shape

    def body(dx1_ref, of_ref, os_ref, gl_ref, wbf_ref, wbs_ref, wo_ref, bg_ref,
             mg_ref, dbf_ref, dbs_ref, dgl_ref, dof_ref, dos_ref, dbg_ref, doft_ref):
        @pl.when(pl.program_id(0) == 0)
        def _():
            dbg_ref[...] = jnp.zeros_like(dbg_ref)

        dmerged = _dot_nt(dx1_ref[...].astype(BF16), wo_ref[...])
        br_f = _dot(of_ref[...], wbf_ref[...])
        br_s = _dot(os_ref[...], wbs_ref[...])
        ga = _sigmoid(gl_ref[:, :d].astype(F32) + bg_ref[0:1, :])
        gb = _sigmoid(gl_ref[:, d:].astype(F32) + bg_ref[1:2, :])
        mg_ref[...] = (ga * br_f + gb * br_s).astype(BF16)
        dbf = (dmerged * ga).astype(BF16)
        dbs = (dmerged * gb).astype(BF16)
        dbf_ref[...] = dbf
        dbs_ref[...] = dbs
        dla = dmerged * br_f * ga * (1.0 - ga)
        dlb = dmerged * br_s * gb * (1.0 - gb)
        dgl_ref[:, :d] = dla.astype(BF16)
        dgl_ref[:, d:] = dlb.astype(BF16)
        dbg_ref[0:1, :] += jnp.sum(dla, axis=0, keepdims=True)
        dbg_ref[1:2, :] += jnp.sum(dlb, axis=0, keepdims=True)
        dof = _dot_nt(dbf, wbf_ref[...])
        dof_ref[...] = dof.astype(BF16)
        doft_ref[0] = _slot_rows(dof.T, jnp.zeros((HEAD_DIM, TM), F32)).astype(BF16)
        dos_ref[...] = _dot_nt(dbs, wbs_ref[...]).astype(BF16)

    shp_d = jax.ShapeDtypeStruct((t_len, d), BF16)
    shp_h = jax.ShapeDtypeStruct((t_len, D_BRANCH), BF16)
    return pl.pallas_call(
        body, name="mix_bwd", grid=(t_len // TM,),
        out_shape=(shp_d, shp_d, shp_d, jax.ShapeDtypeStruct((t_len, 2 * d), BF16), shp_h, shp_h,
                   jax.ShapeDtypeStruct((2, d), F32),
                   jax.ShapeDtypeStruct((t_len // TM, N_HEADS * HEAD_SLOT, TM), BF16)),
        in_specs=[_row_spec(TM, d), _row_spec(TM, D_BRANCH), _row_spec(TM, D_BRANCH), _row_spec(TM, 2 * d),
                  _const_spec(w_bf.shape), _const_spec(w_bs.shape), _const_spec(w_out.shape), _const_spec(b_gate.shape)],
        out_specs=(_row_spec(TM, d), _row_spec(TM, d), _row_spec(TM, d), _row_spec(TM, 2 * d),
                   _row_spec(TM, D_BRANCH), _row_spec(TM, D_BRANCH), _acc_spec((2, d)),
                   pl.BlockSpec((1, N_HEADS * HEAD_SLOT, TM), lambda i: (i, 0, 0))),
        compiler_params=_seq_params(),
    )(dx1, o_fox, o_sb, gl, w_bf, w_bs, w_out, b_gate)


def _inproj_bwd(dqk_f, dv_f, dqkv_b, dgl, df, dx1, x, g_mix, w_pad, w_qk, chip_sums):
    t_len, d = x.shape
    lay, _ = _pad_layout(d)
    n = len(chip_sums)
    n_tiles = t_len // TM

    def body(dqk_ref, dvf_ref, db_ref, dgl_ref, df_ref, dx1_ref, x_ref, g_ref, w_ref, wqk_ref, *rest):
        cs_refs, (dx_ref, dg_ref), out_refs = rest[:n], rest[n:n + 2], rest[n + 2:2 * n + 2]
        chips_start, chips_finish = _chips_plan(cs_refs, out_refs, *rest[2 * n + 2:])

        @pl.when(pl.program_id(0) == 0)
        def _():
            dg_ref[...] = jnp.zeros_like(dg_ref)
            chips_start()

        def back(piece, name):
            lo, hi = lay[name]
            return _dot(piece, w_ref[lo:hi, :])

        xn, r = _rms(x_ref[...])
        dh = (back(df_ref[...], "forget") + back(dgl_ref[...], "gates") + _dot(dqk_ref[0], wqk_ref[:D_BRANCH, :])
              + _dot(dqk_ref[1], wqk_ref[D_BRANCH:, :]) + back(dvf_ref[...], "vf") + back(db_ref[0], "qb")
              + back(db_ref[1], "kb") + back(db_ref[2], "vb"))
        dres, dg = _rms_bwd(dh, xn, r, g_ref[...])
        dg_ref[...] += dg
        dx_ref[...] = dx1_ref[...] + dres
        pl.when(pl.program_id(0) == n_tiles - 1)(chips_finish)

    out = pl.pallas_call(
        body, name="inproj_bwd", grid=(n_tiles,),
        out_shape=[jax.ShapeDtypeStruct((t_len, d), F32), jax.ShapeDtypeStruct((1, d), F32)]
        + [jax.ShapeDtypeStruct(s.shape, s.dtype) for s in chip_sums],
        in_specs=[_row3_spec(2, TM, D_BRANCH), _row_spec(TM, D_BRANCH), _row3_spec(3, TM, D_BRANCH),
                  _row_spec(TM, 2 * d), _row_spec(TM, LANES), _row_spec(TM, d), _row_spec(TM, d), _const_spec((1, d)),
                  _const_spec(w_pad.shape), _const_spec(w_qk.shape)] + _hbm_specs(n),
        out_specs=[_row_spec(TM, d), _acc_spec((1, d))] + _hbm_specs(n),
        scratch_shapes=_chips_sems(n),
        compiler_params=_seq_params(),
    )(dqk_f, dv_f, dqkv_b, dgl, df, dx1, x, g_mix, w_pad, w_qk, *chip_sums)
    return out[0], out[1], out[2:]


def _cols_to_slabs(full):
    r, c8 = full.shape
    return full.reshape(r, N_DEV, c8 // N_DEV).transpose(1, 0, 2)


def _slabs_to_cols(slabs):
    n, r, c = slabs.shape
    return slabs.transpose(1, 0, 2).reshape(r, n * c)


def _win_sizes(d):
    return (D_BRANCH, D_BRANCH, D_BRANCH, N_HEADS, D_BRANCH, D_BRANCH, D_BRANCH, d, d)


def _split_win(w_t, d):
    out, off = [], 0
    for s in _win_sizes(d):
        out.append(w_t[off:off + s])
        off += s
    return out


def _to_slots(w_t):
    c = w_t.shape[1]
    return jnp.pad(w_t.reshape(N_HEADS, HEAD_DIM, c), ((0, 0), (0, HEAD_SLOT - HEAD_DIM), (0, 0))).reshape(-1, c)


def _pad_win(w_full_t, d):
    qa, ka, va, fa, qb, kb, vb, ga, gb = _split_win(w_full_t, d)
    scale = HEAD_DIM ** -0.5
    fpad = jnp.pad(fa, ((0, LANES - N_HEADS), (0, 0)))
    w_pad = jnp.concatenate([_to_slots(qa * scale), _to_slots(ka), va, qb * scale, kb, vb, ga, gb, fpad], axis=0)
    return w_pad, jnp.concatenate([qa * scale, ka], axis=0)


def _unpad_dwin(dqk_f, dv_f, dqkv_b, dgates, dforget, d):
    scale = HEAD_DIM ** -0.5
    return jnp.concatenate([dqk_f[0] * scale, dqk_f[1], dv_f, dforget[:N_HEADS],
                            dqkv_b[0] * scale, dqkv_b[1], dqkv_b[2], dgates], axis=0)


def _c_lane_constants():
    row = jnp.arange(LANES)[:, None]
    lane = jnp.arange(N_HEADS * HEAD_SLOT)[None, :]

    def place(first):
        return ((lane // HEAD_SLOT == row % N_HEADS) & (lane % HEAD_SLOT == first + row // N_HEADS)
                & (row < 3 * N_HEADS)).astype(BF16)

    def ones(first):
        off = lane % HEAD_SLOT
        return ((off >= first) & (off < first + 3)).astype(F32)

    return place(C_TERMS_Q), place(C_TERMS_K), ones(C_ONES_Q), ones(C_ONES_K)


def _pad_rows(a, rows):
    return jnp.pad(a, [(0, 0)] * (a.ndim - 2) + [(0, rows - a.shape[-2]), (0, 0)])


def kernel(x, p, g_mix, w_in, b_forget, b_gate, w_branch_fox, w_branch_sb, w_out, g_mlp, w_up, w_down, g_ple, w_ple_gate, w_ple, g_final, loss_target, m_g_mix, m_w_in, m_b_forget, m_b_gate, m_w_branch_fox, m_w_branch_sb, m_w_out, m_g_mlp, m_w_up, m_w_down, m_g_ple, m_w_ple_gate, m_w_ple, m_g_final, v_g_mix, v_w_in, v_b_forget, v_b_gate, v_w_branch_fox, v_w_branch_sb, v_w_out, v_g_mlp, v_w_up, v_w_down, v_g_ple, v_w_ple_gate, v_w_ple, v_g_final):
    batch, seq, d = x.shape
    t_len = batch * seq
    d_ple = p.shape[-1]
    d_ff = w_up.shape[-1] * N_DEV
    dn = d // N_DEV
    fn = d_ff // N_DEV
    my_c = lax.axis_index("c")
    my_dev = 4 * lax.axis_index("x") + 2 * lax.axis_index("y") + my_c

    bg_hi = b_gate[0].astype(BF16)
    bg_r = b_gate[0] - bg_hi.astype(F32)
    bg_mid = bg_r.astype(BF16)
    bg_lo = (bg_r - bg_mid.astype(F32)).astype(BF16)
    narrow_rows = 2 * D_BRANCH + d_ple + 6
    narrow_rows_pad = -(-narrow_rows // 16) * 16
    narrow = _pad_rows(jnp.concatenate(
        [w_branch_fox[0].astype(BF16), w_branch_sb[0].astype(BF16), w_ple[0].astype(BF16), bg_hi, bg_mid, bg_lo],
        axis=0), narrow_rows_pad)
    g_in, = _all_gather([w_in[0].T.astype(BF16)])
    w_pad, w_qk = _pad_win(g_in.reshape(-1, d), d)
    bf_pad = jnp.pad(b_forget, ((0, 0), (0, LANES - N_HEADS)))
    place_q, place_k, ones_q, ones_k = _c_lane_constants()

    x2d = x.reshape(t_len, d)
    p2d = p.reshape(t_len, d_ple)
    tgt2d = loss_target.reshape(t_len, d)
    qf, kf, kft, vf, vft, qkvb, kbt, vbt, gl, fpre, h1, qft = _inproj_fwd(
        x2d, g_mix, w_pad, bf_pad, place_q, place_k, ones_q, ones_k, seq)
    o_sb, ltot, (g_up, g_out, g_down, g_pg, g_narrow) = _sb_fwd(qkvb, vbt, batch, seq, [
        w_up[0].astype(BF16), w_out[0].astype(BF16), w_down[0].astype(BF16), w_ple_gate[0].astype(BF16), narrow])
    o_fox, lse = _fox_fwd(qf, kf, vft, batch, seq)
    w_up_full = _slabs_to_cols(g_up)
    w_out_full = g_out.reshape(d, d)
    w_down_full = g_down.reshape(d_ff, d)
    w_pg_full = g_pg.reshape(d, d)
    w_bf_full = _slabs_to_cols(g_narrow[:, :D_BRANCH])
    w_bs_full = _slabs_to_cols(g_narrow[:, D_BRANCH:2 * D_BRANCH])
    w_ple_full = _slabs_to_cols(g_narrow[:, 2 * D_BRANCH:2 * D_BRANCH + d_ple])
    bg_terms = g_narrow[:, 2 * D_BRANCH + d_ple:narrow_rows].astype(F32)
    b_gate_full = _slabs_to_cols(bg_terms[:, 0:2] + bg_terms[:, 2:4] + bg_terms[:, 4:6])
    x1 = _mix_fwd(o_fox, o_sb, gl, x2d, w_bf_full, w_bs_full, w_out_full, b_gate_full)

    a_up, dx2, h3, dpre, dpe, loss_acc, dg_ple, dg_final = _mlp_head_fwd_bwd(
        x1, p2d, tgt2d, g_mlp, w_up_full, w_down_full, g_ple, g_final.reshape(1, d), w_pg_full, w_ple_full)
    dx1, da_up, h2t, dg_mlp = _mlp_bwd(dx2, a_up, x1, g_mlp, w_up_full, w_down_full)
    merged, dbr_f, dbr_s, dgl, do_fox, do_sb, dbg, do_fox_t = _mix_bwd(
        dx1, o_fox, o_sb, gl, w_bf_full, w_bs_full, w_out_full, b_gate_full)

    def column_shards(name, lhs, rhs, lhs_t=False):
        if (rhs.shape[-1] // N_DEV) % (4 * LANES) == 0:
            return _matmul_tn(name, lhs, rhs, slabs=True, lhs_t=lhs_t)
        return _cols_to_slabs(_matmul_tn(name, lhs, rhs, lhs_t=lhs_t))

    if fn % (4 * LANES) == 0:
        part_up, = _matmul_tn_once("dw_up", [h2t], da_up, slabs=True, lhs_t=True)
    else:
        part_up = column_shards("dw_up", h2t, da_up, lhs_t=True)
    part_out = _matmul_tn("dw_out", merged, dx1).reshape(N_DEV, dn, d)
    part_down = _matmul_tn_once("dw_down", [a_up], dx2, relu2=True)[0].reshape(N_DEV, fn, d)
    part_pg = _matmul_tn("dw_ple_gate", h3, dpre).reshape(N_DEV, dn, d)
    part_narrow = _pad_rows(jnp.concatenate(
        [column_shards("dw_branch_fox", o_fox, dbr_f), column_shards("dw_branch_sb", o_sb, dbr_s),
         column_shards("dw_ple", p2d, dpe)], axis=1), narrow_rows_pad)
    early = [part_up, part_out, part_down, part_pg, lax.optimization_barrier(part_narrow)]

    dqk_f, dv_f, dc_queries, dc_keys, early_recv = _fox_bwd(
        qf, qft, kf, kft, vf, o_fox, do_fox, do_fox_t, lse, batch, seq, early)
    early_sums = [_pair_add("pair_add_%d" % i, pt, rc, my_c) for i, (pt, rc) in enumerate(zip(early, early_recv))]
    dqkv_b, (s_up, s_out, s_down, s_pg, s_narrow) = _sb_bwd(qkvb, kbt, do_sb, ltot, batch, seq, early_sums)
    dcq_tok = dc_queries.reshape(batch, N_HEADS, seq).transpose(0, 2, 1).reshape(t_len, N_HEADS)
    dck_tok = dc_keys.reshape(batch, N_HEADS, seq).transpose(0, 2, 1).reshape(t_len, N_HEADS)
    lane_pad = ((0, 0), (0, LANES - N_HEADS))
    df, db_forget = _forget_bwd(jnp.pad(dcq_tok, lane_pad), jnp.pad(dck_tok, lane_pad), fpre, batch, seq)

    gw_in = _unpad_dwin(*_matmul_tn_once("dw_in_fox_qk", [dqk_f], h1),
                        *_matmul_tn_once("dw_in_rest", [dv_f, dqkv_b, dgl, df], h1), d)
    part_in = lax.optimization_barrier(gw_in.reshape(N_DEV, -1, d))
    recv_in, = _rs_core_pair("reduce_scatter_core_pair_w_in", [part_in])
    grad_x, dg_mix, (s_in,) = _inproj_bwd(dqk_f, dv_f, dqkv_b, dgl, df, dx1, x2d, g_mix, w_pad, w_qk,
                                          [_pair_add("pair_add_w_in", part_in, recv_in, my_c)])

    small = jnp.concatenate([
        dg_mix, dg_mlp, dg_ple, dg_final, jnp.pad(db_forget[:, :N_HEADS], ((0, 0), (0, d - N_HEADS))), dbg,
        jnp.pad(loss_acc[:, :1], ((0, 0), (0, d - 1)))], axis=0)
    small = _all_reduce_small(small)
    loss = small[7, 0]
    small_grads = {
        "g_mix": small[0:1], "g_mlp": small[1:2], "g_ple": small[2:3], "g_final": small[3:4],
        "b_forget": small[4:5, :N_HEADS],
        "b_gate": lax.dynamic_slice_in_dim(small[5:7], my_dev * dn, dn, axis=1),
    }

    weights = {"g_mix": g_mix, "w_in": w_in, "b_forget": b_forget, "b_gate": b_gate, "w_branch_fox": w_branch_fox,
               "w_branch_sb": w_branch_sb, "w_out": w_out, "g_mlp": g_mlp, "w_up": w_up, "w_down": w_down,
               "g_ple": g_ple, "w_ple_gate": w_ple_gate, "w_ple": w_ple, "g_final": g_final}
    m_in = {"g_mix": m_g_mix, "w_in": m_w_in, "b_forget": m_b_forget, "b_gate": m_b_gate,
            "w_branch_fox": m_w_branch_fox, "w_branch_sb": m_w_branch_sb, "w_out": m_w_out, "g_mlp": m_g_mlp,
            "w_up": m_w_up, "w_down": m_w_down, "g_ple": m_g_ple, "w_ple_gate": m_w_ple_gate, "w_ple": m_w_ple,
            "g_final": m_g_final}
    v_in = {"g_mix": v_g_mix, "w_in": v_w_in, "b_forget": v_b_forget, "b_gate": v_b_gate,
            "w_branch_fox": v_w_branch_fox, "w_branch_sb": v_w_branch_sb, "w_out": v_w_out, "g_mlp": v_g_mlp,
            "w_up": v_w_up, "w_down": v_w_down, "g_ple": v_g_ple, "w_ple_gate": v_w_ple_gate, "w_ple": v_w_ple,
            "g_final": v_g_final}
    names = list(weights)

    def as2d(a):
        return a.reshape(-1, a.shape[-1])

    result = {}
    big = {"w_up": (s_up, 0), "w_out": (s_out, 0), "w_down": (s_down, 0), "w_ple_gate": (s_pg, 0),
           "w_branch_fox": (s_narrow, 0), "w_branch_sb": (s_narrow, D_BRANCH), "w_ple": (s_narrow, 2 * D_BRANCH)}
    for n, (parts, off) in big.items():
        result[n] = _adamw_parts("adamw_" + n, as2d(weights[n]), parts, off, as2d(m_in[n]), as2d(v_in[n]))
    result["w_in"] = tuple(r.T for r in _adamw_parts("adamw_w_in", w_in[0].T, s_in, 0, m_w_in[0].T, v_w_in[0].T))
    small_names = list(small_grads)
    small_out = _adamw_small([(as2d(weights[n]), small_grads[n], as2d(m_in[n]), as2d(v_in[n])) for n in small_names])
    for n, (dlt, nm, nv) in zip(small_names, small_out):
        result[n] = (small_grads[n], dlt, nm, nv)
    outs = [[result[n][k].reshape(weights[n].shape) for n in names] for k in range(4)]
    return (loss, grad_x.reshape(x.shape), *outs[0], *outs[1], *outs[2], *outs[3])
```

```python
import jax
import jax.numpy as jnp
from jax import lax
from jax.experimental import pallas as pl
from jax.experimental.pallas import tpu as pltpu

F32 = jnp.float32
BF16 = jnp.bfloat16

HEAD_DIM = 64
N_HEADS = 8
D_BRANCH = N_HEADS * HEAD_DIM
EPS = 1e-6
ADAM_LR = 0.001
ADAM_B1 = 0.9
ADAM_B2 = 0.999
ADAM_EPS = 1e-08
ADAM_WD = 0.01
ADAM_STEP = 10

N_DEV = 8
LANES = 128
TM = 256
TQ = 256
TK = 256
NH = 4
HEAD_SLOT = 128
C_TERMS_Q = 64
C_ONES_K = 64
C_TERMS_K = 67
C_ONES_Q = 67
NEG = -1e30
VMEM_LIMIT = 56 * 1024 * 1024
MESH = pl.DeviceIdType.MESH


def _dot(a, b):
    return jnp.dot(a, b, preferred_element_type=F32)


def _dot_nt(a, b):
    return lax.dot_general(a, b, (((1,), (1,)), ((), ())), preferred_element_type=F32)


def _dot_tn(a, b):
    return lax.dot_general(a, b, (((0,), (0,)), ((), ())), preferred_element_type=F32)


def _sigmoid(x):
    return 1.0 / (1.0 + jnp.exp(-x))


def _softplus(x):
    return jnp.maximum(x, 0.0) + jnp.log(1.0 + jnp.exp(-jnp.abs(x)))


def _split2(x):
    hi = x.astype(BF16)
    lo = (x - hi.astype(F32)).astype(BF16)
    return hi, lo


def _split3(x):
    hi = x.astype(BF16)
    r = x - hi.astype(F32)
    mid = r.astype(BF16)
    lo = (r - mid.astype(F32)).astype(BF16)
    return hi, mid, lo


def _tri(n, rel):
    r = lax.broadcasted_iota(jnp.int32, (n, n), 0)
    c = lax.broadcasted_iota(jnp.int32, (n, n), 1)
    return rel(r, c).astype(BF16)


def _rms(x):
    r = lax.rsqrt(jnp.mean(x * x, axis=-1, keepdims=True) + EPS)
    return x * r, r


def _rms_bwd(dh, xn, r, g):
    dxn = dh * g
    dx = r * (dxn - xn * jnp.mean(dxn * xn, axis=-1, keepdims=True))
    return dx, jnp.sum(dh * xn, axis=0, keepdims=True)


def _row_spec(tm, cols):
    return pl.BlockSpec((tm, cols), lambda i: (i, 0))


def _row3_spec(g, tm, cols):
    return pl.BlockSpec((g, tm, cols), lambda i: (0, i, 0))


def _col_spec(rows, tm):
    return pl.BlockSpec((rows, tm), lambda i: (0, i))


def _const_spec(shape):
    nd = len(shape)
    return pl.BlockSpec(shape, lambda i: (0,) * nd, pipeline_mode=pl.Buffered(1))


def _acc_spec(shape):
    nd = len(shape)
    return pl.BlockSpec(shape, lambda i: (0,) * nd)


def _seq_params():
    return pltpu.CompilerParams(dimension_semantics=("arbitrary",), vmem_limit_bytes=VMEM_LIMIT)


def _mesh_pos():
    return lax.axis_index("x"), lax.axis_index("y"), lax.axis_index("c")


def _other_chips(x, y):
    return [(1 - x, y), (x, 1 - y), (1 - x, 1 - y)]


def _hbm_specs(n):
    return [pl.BlockSpec(memory_space=pl.ANY)] * n


def _gather_plan(x_refs, out_refs, send_sems, recv_sems, local_sems):
    n = len(x_refs)
    x, y, c = _mesh_pos()
    me, sibling = (x, y, c), (x, y, 1 - c)
    x_nb, y_nb, diag = (1 - x, y), (x, 1 - y), (1 - x, 1 - y)

    def slab(a, chip, core):
        return out_refs[a].at[4 * chip[0] + 2 * chip[1] + core]

    def half(ref, upper):
        rows = ref.shape[0]
        cut = (rows // 2) // 16 * 16
        return ref.at[pl.ds(cut, rows - cut)] if upper else ref.at[pl.ds(0, cut)]

    def copy(a, k, ref, to, src=None):
        return pltpu.make_async_remote_copy(
            src_ref=ref if src is None else src, dst_ref=ref,
            send_sem=send_sems.at[8 * a + k], recv_sem=recv_sems.at[8 * a + k], device_id=to, device_id_type=MESH)

    mine = [pltpu.make_async_copy(x_refs[a], slab(a, (x, y), c), local_sems.at[a]) for a in range(n)]
    first = []
    for a in range(n):
        own = slab(a, (x, y), c)
        first += [copy(a, 0, own, sibling, src=x_refs[a]), copy(a, 1, own, (*x_nb, c), src=x_refs[a]),
                  copy(a, 2, own, (*y_nb, c), src=x_refs[a])]

    def start():
        for cp in mine + first:
            cp.start()

    def finish():
        passed = []

        def pass_on(cp):
            passed.append(cp)
            cp.start()

        for a in range(n):
            got = slab(a, x_nb, c)
            copy(a, 1, got, me).wait_recv()
            pass_on(copy(a, 5, got, sibling))
            pass_on(copy(a, 3, half(got, False), (*y_nb, c)))
        for a in range(n):
            got = slab(a, y_nb, c)
            copy(a, 2, got, me).wait_recv()
            pass_on(copy(a, 6, got, sibling))
            pass_on(copy(a, 4, half(got, True), (*x_nb, c)))
        for a in range(n):
            got = slab(a, diag, c)
            copy(a, 3, half(got, False), me).wait_recv()
            copy(a, 4, half(got, True), me).wait_recv()
            pass_on(copy(a, 7, got, sibling))
        for a in range(n):
            copy(a, 0, slab(a, (x, y), 1 - c), me).wait_recv()
            for k, chip in ((5, x_nb), (6, y_nb), (7, diag)):
                copy(a, k, slab(a, chip, 1 - c), me).wait_recv()
        for cp in first + passed:
            cp.wait_send()
        for cp in mine:
            cp.wait()

    return start, finish


def _gather_plan_direct(x_refs, out_refs, send_sems, recv_sems, local_sems):
    n = len(x_refs)
    x, y, c = _mesh_pos()
    me, sibling = (x, y, c), (x, y, 1 - c)
    chips = _other_chips(x, y)

    def index(px, py, pc):
        return 4 * px + 2 * py + pc

    def copy(a, k, block, to, src=None):
        slab = out_refs[a].at[index(*block)]
        return pltpu.make_async_remote_copy(
            src_ref=slab if src is None else src, dst_ref=slab,
            send_sem=send_sems.at[8 * a + k], recv_sem=recv_sems.at[8 * a + k], device_id=to, device_id_type=MESH)

    mine = [pltpu.make_async_copy(x_refs[a], out_refs[a].at[index(*me)], local_sems.at[a]) for a in range(n)]
    first = []
    for a in range(n):
        first.append(copy(a, 0, me, sibling, src=x_refs[a]))
        first += [copy(a, 1 + j, me, (cx, cy, c), src=x_refs[a]) for j, (cx, cy) in enumerate(chips)]

    def start():
        for cp in mine + first:
            cp.start()

    def finish():
        passed = []
        for j, (cx, cy) in enumerate(chips):
            for a in range(n):
                copy(a, 1 + j, (cx, cy, c), me).wait_recv()
                passed.append(copy(a, 4 + j, (cx, cy, c), sibling))
                passed[-1].start()
        for a in range(n):
            copy(a, 0, sibling, me).wait_recv()
            for j, (cx, cy) in enumerate(chips):
                copy(a, 4 + j, (cx, cy, 1 - c), me).wait_recv()
        for cp in first + passed:
            cp.wait_send()
        for cp in mine:
            cp.wait()

    return start, finish


def _gather_shapes(shards):
    return [jax.ShapeDtypeStruct((N_DEV,) + s.shape, s.dtype) for s in shards]


def _gather_sems(n):
    return [pltpu.SemaphoreType.DMA((8 * n,)), pltpu.SemaphoreType.DMA((8 * n,)), pltpu.SemaphoreType.DMA((n,))]


def _all_gather(shards):
    n = len(shards)

    def body(*refs):
        start, finish = _gather_plan(refs[:n], refs[n:2 * n], *refs[2 * n:])
        start()
        finish()

    return pl.pallas_call(
        body, name="all_gather_weights", out_shape=_gather_shapes(shards),
        in_specs=_hbm_specs(n), out_specs=_hbm_specs(n), scratch_shapes=_gather_sems(n),
    )(*shards)


def _pair_plan(p_refs, recv_refs, send_sems, recv_sems):
    n = len(p_refs)
    x, y, c = _mesh_pos()
    sibling = (x, y, 1 - c)

    def start():
        for a in range(n):
            for chip in range(4):
                pltpu.make_async_remote_copy(
                    src_ref=p_refs[a].at[2 * chip + (1 - c)], dst_ref=recv_refs[a].at[chip],
                    send_sem=send_sems.at[a], recv_sem=recv_sems.at[a], device_id=sibling, device_id_type=MESH).start()

    def finish():
        for a in range(n):
            pltpu.make_async_remote_copy(
                src_ref=recv_refs[a], dst_ref=recv_refs[a], send_sem=send_sems.at[a], recv_sem=recv_sems.at[a],
                device_id=sibling, device_id_type=MESH).wait()

    return start, finish


def _pair_shapes(partials):
    return [jax.ShapeDtypeStruct((4,) + s.shape[1:], s.dtype) for s in partials]


def _pair_sems(n):
    return [pltpu.SemaphoreType.DMA((n,)), pltpu.SemaphoreType.DMA((n,))]


def _rs_core_pair(name, partials):
    n = len(partials)

    def body(*refs):
        start, finish = _pair_plan(refs[:n], refs[n:2 * n], *refs[2 * n:])
        start()
        finish()

    return pl.pallas_call(
        body, name=name, out_shape=_pair_shapes(partials),
        in_specs=_hbm_specs(n), out_specs=_hbm_specs(n), scratch_shapes=_pair_sems(n),
    )(*partials)


def _chips_plan(cs_refs, out_refs, send_sems, recv_sems, local_sems):
    n = len(cs_refs)
    x, y, c = _mesh_pos()
    chip = 2 * x + y
    chips = _other_chips(x, y)
    mine = [pltpu.make_async_copy(cs_refs[a].at[chip], out_refs[a].at[chip], local_sems.at[a]) for a in range(n)]
    sends = [pltpu.make_async_remote_copy(
        src_ref=cs_refs[a].at[2 * cx + cy], dst_ref=out_refs[a].at[chip],
        send_sem=send_sems.at[3 * a + j], recv_sem=recv_sems.at[3 * a + j],
        device_id=(cx, cy, c), device_id_type=MESH) for a in range(n) for j, (cx, cy) in enumerate(chips)]

    def start():
        for cp in mine + sends:
            cp.start()

    def finish():
        for a in range(n):
            for j, (cx, cy) in enumerate(chips):
                pltpu.make_async_remote_copy(
                    src_ref=cs_refs[a].at[chip], dst_ref=out_refs[a].at[2 * cx + cy],
                    send_sem=send_sems.at[3 * a + j], recv_sem=recv_sems.at[3 * a + j],
                    device_id=(x, y, c), device_id_type=MESH).wait_recv()
        for cp in sends:
            cp.wait_send()
        for cp in mine:
            cp.wait()

    return start, finish


def _chips_sems(n):
    return [pltpu.SemaphoreType.DMA((3 * n,)), pltpu.SemaphoreType.DMA((3 * n,)), pltpu.SemaphoreType.DMA((n,))]


def _all_reduce_small(vec):
    rows, cols = vec.shape

    def body(x_ref, land_ref, sum_ref, send_sems, recv_sems):
        x, y, c = _mesh_pos()
        me = 4 * x + 2 * y + c
        land_ref[me] = x_ref[...]
        flips = [(fx, fy, fc) for fx in (0, 1) for fy in (0, 1) for fc in (0, 1)][1:]

        def flipped(f):
            return tuple((1 - v) if b else v for v, b in zip((x, y, c), f))

        sends = []
        for k, f in enumerate(flips):
            sends.append(pltpu.make_async_remote_copy(
                src_ref=x_ref, dst_ref=land_ref.at[me], send_sem=send_sems.at[k], recv_sem=recv_sems.at[k],
                device_id=flipped(f), device_id_type=MESH))
            sends[-1].start()
        for k, f in enumerate(flips):
            px, py, pc = flipped(f)
            pltpu.make_async_remote_copy(
                src_ref=x_ref, dst_ref=land_ref.at[4 * px + 2 * py + pc], send_sem=send_sems.at[k],
                recv_sem=recv_sems.at[k], device_id=(x, y, c), device_id_type=MESH).wait_recv()
        for cp in sends:
            cp.wait_send()
        total = land_ref[0]
        for d in range(1, N_DEV):
            total = total + land_ref[d]
        sum_ref[...] = total

    vm = pl.BlockSpec(memory_space=pltpu.VMEM)
    return pl.pallas_call(
        body, name="all_reduce_small",
        out_shape=(jax.ShapeDtypeStruct((N_DEV, rows, cols), F32), jax.ShapeDtypeStruct((rows, cols), F32)),
        in_specs=[vm], out_specs=(vm, vm),
        scratch_shapes=[pltpu.SemaphoreType.DMA((7,)), pltpu.SemaphoreType.DMA((7,))],
    )(vec)[1]


def _block_rows(rows, cols, itemsize, align, row_off=0):
    best = None
    for t in range(align, rows + 1, align):
        if rows % t == 0 and row_off % t == 0 and t * cols * itemsize <= (1 << 20):
            best = t
    return rows if best is None else best


def _pair_add(name, partial, recv, my_c):
    _, rows, cols = partial.shape
    br = _block_rows(rows, cols, 2, 16)

    def body(c_ref, a_ref, b_ref, o_ref):
        o_ref[...] = (a_ref[...].astype(F32) + b_ref[...].astype(F32)).astype(BF16)

    return pl.pallas_call(
        body, name=name,
        grid_spec=pltpu.PrefetchScalarGridSpec(
            num_scalar_prefetch=1, grid=(4, rows // br),
            in_specs=[pl.BlockSpec((None, None, br, cols), lambda j, i, c_ref: (j, c_ref[0], i, 0)),
                      pl.BlockSpec((None, br, cols), lambda j, i, c_ref: (j, i, 0))],
            out_specs=pl.BlockSpec((None, br, cols), lambda j, i, c_ref: (j, i, 0))),
        out_shape=jax.ShapeDtypeStruct((4, rows, cols), BF16),
    )(my_c.reshape(1).astype(jnp.int32), partial.reshape(4, 2, rows, cols), recv)


def _adam_update(w, g, m, v):
    nm = ADAM_B1 * m + (1.0 - ADAM_B1) * g
    nv = ADAM_B2 * v + (1.0 - ADAM_B2) * (g * g)
    m_hat = nm / (1.0 - ADAM_B1 ** ADAM_STEP)
    v_hat = nv / (1.0 - ADAM_B2 ** ADAM_STEP)
    return -ADAM_LR * (m_hat / (jnp.sqrt(v_hat) + ADAM_EPS) + ADAM_WD * w), nm, nv


def _adamw_parts(name, w, parts, row_off, m, v):
    rows, cols = w.shape
    tr = _block_rows(rows, cols, 4, 16, row_off)
    tc = cols
    if tr == rows and rows % 16 != 0 and cols % (2 * LANES) == 0:
        tc = 2 * LANES
    assert rows % tr == 0 and row_off % tr == 0 and (tc == cols or row_off == 0)
    off = row_off // tr

    def body(w_ref, p_ref, m_ref, v_ref, g_ref, d_ref, nm_ref, nv_ref):
        g = p_ref[0].astype(F32)
        for j in range(1, 4):
            g = g + p_ref[j].astype(F32)
        g_ref[...] = g
        d_ref[...], nm_ref[...], nv_ref[...] = _adam_update(w_ref[...], g, m_ref[...], v_ref[...])

    spec = pl.BlockSpec((tr, tc), lambda i, j: (i, j))
    shp = jax.ShapeDtypeStruct((rows, cols), F32)
    return pl.pallas_call(
        body, name=name, grid=(rows // tr, cols // tc), out_shape=(shp,) * 4,
        in_specs=[spec, pl.BlockSpec((4, tr, tc), lambda i, j: (0, off + i, j)), spec, spec], out_specs=(spec,) * 4,
    )(w, parts, m, v)


def _adamw_small(tensors):
    n = len(tensors)

    def body(*refs):
        ins, outs = refs[:4 * n], refs[4 * n:]
        for t in range(n):
            w_ref, g_ref, m_ref, v_ref = ins[4 * t:4 * t + 4]
            d, nm, nv = _adam_update(w_ref[...], g_ref[...], m_ref[...], v_ref[...])
            outs[3 * t][...], outs[3 * t + 1][...], outs[3 * t + 2][...] = d, nm, nv

    vm = pl.BlockSpec(memory_space=pltpu.VMEM)
    out = pl.pallas_call(
        body, name="adamw_small",
        out_shape=[jax.ShapeDtypeStruct(t[0].shape, F32) for t in tensors for _ in range(3)],
        in_specs=[vm] * (4 * n), out_specs=[vm] * (3 * n),
    )(*[a for t in tensors for a in t])
    return [tuple(out[3 * t:3 * t + 3]) for t in range(n)]


def _matmul_tn(name, a, b, relu2=False, slabs=False, lhs_t=False):
    a_groups = a.shape[0] if a.ndim == 3 else 0
    b_groups = b.shape[0] if b.ndim == 3 else 0
    groups = max(a_groups, b_groups, 1)
    assert not (a_groups and b_groups) and not (a_groups and lhs_t)
    a3 = a if a_groups else a[None]
    b3 = b if b_groups else b[None]
    t_len, k_len = a3.shape[1:][::-1] if lhs_t else a3.shape[1:]
    n_len = b3.shape[2]
    tt = min(t_len, 1024)
    tk = min(k_len, 1024)
    tn = n_len // N_DEV if slabs else min(n_len, 1024)
    nt = t_len // tt
    assert not slabs or (groups == 1 and tn <= 1024)

    def body(a_ref, b_ref, o_ref, acc_ref):
        @pl.when(pl.program_id(3) == 0)
        def _():
            acc_ref[...] = jnp.zeros_like(acc_ref)

        av = a_ref[...]
        if relu2:
            av = jnp.square(jnp.maximum(av.astype(F32), 0.0))
        product = _dot if lhs_t else _dot_tn
        acc_ref[...] += product(av.astype(BF16), b_ref[...].astype(BF16))

        @pl.when(pl.program_id(3) == nt - 1)
        def _():
            o_ref[...] = acc_ref[...].astype(BF16)

    def a_group(g):
        return g if a_groups else 0

    def b_group(g):
        return g if b_groups else 0

    if slabs:
        out_shape = jax.ShapeDtypeStruct((N_DEV, k_len, tn), BF16)
        out_spec = pl.BlockSpec((None, tk, tn), lambda g, i, j, t: (j, i, 0))
    else:
        out_shape = jax.ShapeDtypeStruct((groups, k_len, n_len), BF16)
        out_spec = pl.BlockSpec((None, tk, tn), lambda g, i, j, t: (g, i, j))
    out = pl.pallas_call(
        body, name=name, grid=(groups, k_len // tk, n_len // tn, nt), out_shape=out_shape,
        in_specs=[pl.BlockSpec((None, tk, tt), lambda g, i, j, t: (a_group(g), i, t)) if lhs_t
                  else pl.BlockSpec((None, tt, tk), lambda g, i, j, t: (a_group(g), t, i)),
                  pl.BlockSpec((None, tt, tn), lambda g, i, j, t: (b_group(g), t, j))],
        out_specs=out_spec,
        scratch_shapes=[pltpu.VMEM((tk, tn), F32)],
        compiler_params=pltpu.CompilerParams(
            dimension_semantics=("parallel", "parallel", "parallel", "arbitrary"), vmem_limit_bytes=VMEM_LIMIT),
    )(a3, b3)
    return out if (slabs or a_groups or b_groups) else out[0]


def _matmul_tn_once(name, lhs_list, rhs, relu2=False, slabs=False, lhs_t=False):
    t_len, n_len = rhs.shape
    tt = min(t_len, 256 if relu2 else 512)
    nt = t_len // tt
    n_lhs = len(lhs_list)
    assert not (lhs_t or slabs) or (n_lhs == 1 and lhs_list[0].ndim == 2)
    k_shapes = [(a.shape[0], n_len) if lhs_t else a.shape[:-2] + (a.shape[-1], n_len) for a in lhs_list]
    tn = n_len // N_DEV

    def body(*refs):
        a_refs, b_ref = refs[:n_lhs], refs[n_lhs]
        o_refs, acc_refs = refs[n_lhs + 1:2 * n_lhs + 1], refs[2 * n_lhs + 1:]
        step = pl.program_id(0)

        @pl.when(step == 0)
        def _():
            for acc in acc_refs:
                acc[...] = jnp.zeros_like(acc)

        bv = b_ref[...].astype(BF16)

        def piece(av):
            if relu2:
                av = jnp.square(jnp.maximum(av.astype(F32), 0.0))
            return (_dot if lhs_t else _dot_tn)(av.astype(BF16), bv)

        for a_ref, acc in zip(a_refs, acc_refs):
            if len(acc.shape) == 3:
                for g in range(acc.shape[0]):
                    acc[g] += piece(a_ref[g])
            else:
                acc[...] += piece(a_ref[...])

        @pl.when(step == nt - 1)
        def _():
            for o_ref, acc in zip(o_refs, acc_refs):
                if slabs:
                    for j in range(N_DEV):
                        o_ref[j] = acc[:, j * tn:(j + 1) * tn].astype(BF16)
                else:
                    o_ref[...] = acc[...].astype(BF16)

    def lhs_spec(a):
        if lhs_t:
            return pl.BlockSpec((a.shape[0], tt), lambda t: (0, t))
        if a.ndim == 3:
            return pl.BlockSpec((a.shape[0], tt, a.shape[2]), lambda t: (0, t, 0))
        return pl.BlockSpec((tt, a.shape[1]), lambda t: (t, 0))

    out_shapes = [(N_DEV, k_shapes[0][0], tn)] if slabs else k_shapes
    return pl.pallas_call(
        body, name=name, grid=(nt,),
        out_shape=[jax.ShapeDtypeStruct(s, BF16) for s in out_shapes],
        in_specs=[lhs_spec(a) for a in lhs_list] + [pl.BlockSpec((tt, n_len), lambda t: (t, 0))],
        out_specs=[_acc_spec(s) for s in out_shapes],
        scratch_shapes=[pltpu.VMEM(s, F32) for s in k_shapes],
        compiler_params=_seq_params(),
    )(*lhs_list, rhs)


def _pad_layout(d):
    names = ("qf", "kf", "vf", "qb", "kb", "vb", "gates", "forget")
    sizes = (N_HEADS * HEAD_SLOT, N_HEADS * HEAD_SLOT, D_BRANCH, D_BRANCH, D_BRANCH, D_BRANCH, 2 * d, LANES)
    out, off = {}, 0
    for n, s in zip(names, sizes):
        out[n] = (off, off + s)
        off += s
    return out, off


def _slot_rows(xt, extra):
    parts = []
    for h in range(N_HEADS):
        parts += [xt[h * HEAD_DIM:(h + 1) * HEAD_DIM, :], extra]
    return jnp.concatenate(parts, axis=0)


def _inproj_fwd(x, g_mix, w_pad, bf_pad, place_q, place_k, ones_q, ones_k, seq):
    t_len, d = x.shape
    lay, _ = _pad_layout(d)
    tiles_per_seq = seq // TM
    slot_w = N_HEADS * HEAD_SLOT

    def body(x_ref, g_ref, w_ref, bf_ref, pq_ref, pk_ref, oq_ref, ok_ref,
             qf_ref, kf_ref, kft_ref, vf_ref, vft_ref, qkvb_ref, kbt_ref, vbt_ref, gl_ref, fpre_ref, h_ref, qft_ref,
             carry_ref):
        @pl.when(pl.program_id(0) % tiles_per_seq == 0)
        def _():
            carry_ref[...] = jnp.zeros_like(carry_ref)

        def proj(name):
            lo, hi = lay[name]
            return _dot_nt(h, w_ref[lo:hi, :])

        xn, _ = _rms(x_ref[...])
        h = (xn * g_ref[...]).astype(BF16)
        fpre = proj("forget") + bf_ref[...]
        fpre_ref[...] = fpre
        logf = -_softplus(-fpre)
        lower = _tri(TM, lambda r, c: c <= r)
        hi, mid, lo = _split3(logf)
        c_val = carry_ref[...] + _dot(lower, hi) + _dot(lower, mid) + _dot(lower, lo)
        carry_ref[...] = carry_ref[...] + jnp.sum(logf, axis=0, keepdims=True)
        head_lanes = lax.broadcasted_iota(jnp.int32, (TM, LANES), 1) < N_HEADS
        terms = [jnp.where(head_lanes, t.astype(F32), 0.0) for t in _split3(c_val)]
        c_packed = (terms[0] + pltpu.roll(terms[1], N_HEADS, 1) + pltpu.roll(terms[2], 2 * N_HEADS, 1)).astype(BF16)
        qf = proj("qf") + _dot(c_packed, pq_ref[...]) + oq_ref[...]
        qf_ref[...] = qf.astype(BF16)
        qft_ref[0] = qf.T.astype(BF16)
        kf = proj("kf") - _dot(c_packed, pk_ref[...]) + ok_ref[...]
        kf_ref[...] = kf.astype(BF16)
        kft_ref[0] = kf.T.astype(BF16)
        row0 = (lax.broadcasted_iota(jnp.int32, (HEAD_DIM, TM), 0) == 0).astype(F32)
        zeros = jnp.zeros((HEAD_DIM, TM), F32)
        vf = proj("vf")
        vf_ref[...] = vf.astype(BF16)
        vft_ref[0] = _slot_rows(vf.T, row0).astype(BF16)
        qkvb_ref[0] = proj("qb").astype(BF16)
        kb = proj("kb")
        qkvb_ref[1] = kb.astype(BF16)
        kbt_ref[0] = _slot_rows(kb.T, zeros).astype(BF16)
        vb = proj("vb")
        qkvb_ref[2] = vb.astype(BF16)
        vbt_ref[0] = _slot_rows(vb.T, row0).astype(BF16)
        gl_ref[...] = proj("gates").astype(BF16)
        h_ref[...] = h

    n_tiles = t_len // TM
    slot_shape = jax.ShapeDtypeStruct((t_len, slot_w), BF16)
    t_shape = jax.ShapeDtypeStruct((n_tiles, slot_w, TM), BF16)
    t_spec = pl.BlockSpec((1, slot_w, TM), lambda i: (i, 0, 0))
    return pl.pallas_call(
        body, name="inproj_fwd", grid=(n_tiles,),
        out_shape=(slot_shape, slot_shape, t_shape, jax.ShapeDtypeStruct((t_len, D_BRANCH), BF16), t_shape,
                   jax.ShapeDtypeStruct((3, t_len, D_BRANCH), BF16), t_shape, t_shape,
                   jax.ShapeDtypeStruct((t_len, 2 * d), BF16), jax.ShapeDtypeStruct((t_len, LANES), F32),
                   jax.ShapeDtypeStruct((t_len, d), BF16), t_shape),
        in_specs=[_row_spec(TM, d), _const_spec((1, d)), _const_spec(w_pad.shape), _const_spec((1, LANES)),
                  _const_spec(place_q.shape), _const_spec(place_k.shape), _const_spec((1, slot_w)),
                  _const_spec((1, slot_w))],
        out_specs=(_row_spec(TM, slot_w), _row_spec(TM, slot_w), t_spec, _row_spec(TM, D_BRANCH), t_spec,
                   _row3_spec(3, TM, D_BRANCH), t_spec, t_spec, _row_spec(TM, 2 * d), _row_spec(TM, LANES),
                   _row_spec(TM, d), t_spec),
        scratch_shapes=[pltpu.VMEM((1, LANES), F32)],
        compiler_params=_seq_params(),
    )(x, g_mix, w_pad, bf_pad, place_q, place_k, ones_q, ones_k)


def _slot_spec(seq):
    return pl.BlockSpec((seq, NH * HEAD_SLOT), lambda b, g: (b, g))


def _group2_spec(seq):
    return pl.BlockSpec((2, seq, NH * HEAD_DIM), lambda b, g: (0, b, g))


def _group_spec(seq):
    return pl.BlockSpec((seq, NH * HEAD_DIM), lambda b, g: (b, g))


def _group3_spec(which, seq):
    return pl.BlockSpec((None, seq, NH * HEAD_DIM), lambda b, g: (which, b, g))


def _tblock_spec(seq):
    return pl.BlockSpec((seq // TK, NH * HEAD_SLOT, TK), lambda b, g: (b, g, 0))


def _qrow_spec(seq):
    return pl.BlockSpec((None, NH, seq // TQ, TQ), lambda b, g: (b, g, 0, 0))


def _attn_params():
    return pltpu.CompilerParams(dimension_semantics=("parallel", "parallel"), vmem_limit_bytes=VMEM_LIMIT)


def _serial_attn_params():
    return pltpu.CompilerParams(dimension_semantics=("arbitrary", "arbitrary"), vmem_limit_bytes=VMEM_LIMIT)


def _hcols(hh):
    return slice(hh * HEAD_DIM, (hh + 1) * HEAD_DIM)


def _hslot(hh):
    return slice(hh * HEAD_SLOT, (hh + 1) * HEAD_SLOT)


def _key_query_mask(rel):
    r = lax.broadcasted_iota(jnp.int32, (TK, TQ), 0)
    c = lax.broadcasted_iota(jnp.int32, (TK, TQ), 1)
    return rel(r, c)


def _heads_cat(vals):
    return jnp.concatenate(vals, axis=1)


def _untranspose(acc_t):
    return acc_t.T[:, :HEAD_DIM]


def _fox_fwd(qf, kf, vft, batch, seq):
    def body(q_ref, k_ref, vt_ref, o_ref, lse_ref, m_s, acc_s):
        causal = _key_query_mask(lambda r, c: r <= c)

        def tile(q0, kj, masked, n_k=1):
            krows = pl.ds(pl.multiple_of(kj * TK, TK), n_k * TK)
            heads = range(NH)
            sts = [_dot_nt(k_ref[krows, _hslot(hh)], q_ref[pl.ds(q0, TQ), _hslot(hh)]) for hh in heads]
            if masked:
                sts = [jnp.where(causal, st, NEG) for st in sts]
            m_olds = [m_s[hh] for hh in heads]
            m_news = [jnp.maximum(m_olds[hh], jnp.max(sts[hh], axis=0, keepdims=True)) for hh in heads]
            pts = [jnp.exp(sts[hh] - m_news[hh]).astype(BF16) for hh in heads]
            pvs = [sum(_dot(vt_ref[kj + i, _hslot(hh), :], pts[hh][i * TK:(i + 1) * TK]) for i in range(n_k))
                   for hh in heads]
            for hh in heads:
                acc_s[hh] = jnp.exp(m_olds[hh] - m_news[hh]) * acc_s[hh] + pvs[hh]
                m_s[hh] = m_news[hh]

        def q_loop(qi, _):
            q0 = pl.multiple_of(qi * TQ, TQ)
            m_s[...] = jnp.full(m_s.shape, NEG, F32)
            acc_s[...] = jnp.zeros_like(acc_s)

            def pair_loop(i, _):
                tile(q0, 2 * i, False, n_k=2)
                return 0

            lax.fori_loop(0, qi // 2, pair_loop, 0)
            pl.when(qi % 2 == 1)(lambda: tile(q0, qi - 1, False))
            tile(q0, qi, True)
            outs = []
            for hh in range(NH):
                total = acc_s[hh, HEAD_DIM:HEAD_DIM + 1, :]
                outs.append(_untranspose(acc_s[hh] / total))
                lse_ref[hh, pl.ds(qi, 1), :] = m_s[hh] + jnp.log(total)
            o_ref[pl.ds(q0, TQ), :] = _heads_cat(outs).astype(BF16)
            return 0

        lax.fori_loop(0, seq // TQ, q_loop, 0)

    return pl.pallas_call(
        body, name="fox_fwd", grid=(batch, N_HEADS // NH),
        out_shape=(jax.ShapeDtypeStruct((batch * seq, D_BRANCH), BF16),
                   jax.ShapeDtypeStruct((batch, N_HEADS, seq // TQ, TQ), F32)),
        in_specs=[_slot_spec(seq), _slot_spec(seq), _tblock_spec(seq)],
        out_specs=(_group_spec(seq), _qrow_spec(seq)),
        scratch_shapes=[pltpu.VMEM((NH, 1, TQ), F32), pltpu.VMEM((NH, HEAD_SLOT, TQ), F32)],
        compiler_params=_attn_params(),
    )(qf, kf, vft)


def _fox_bwd(qf, qft, kf, kft, vf, o, do, dot, lse, batch, seq, partials):
    n_q = seq // TQ
    n = len(partials)

    def body(q_ref, qt_ref, k_ref, kt_ref, v_ref, o_ref, do_ref, dot_ref, lse_ref, *rest):
        p_refs, (dqk_ref, dv_ref, dcq_ref, dck_ref), recv_refs = rest[:n], rest[n:n + 4], rest[n + 4:2 * n + 4]
        delta_s, dqt_acc, dk_s, dv_s = rest[2 * n + 4:2 * n + 8]
        pair_start, pair_finish = _pair_plan(p_refs, recv_refs, *rest[2 * n + 8:])
        first_step, last_step = _first_last_step()
        pl.when(first_step)(pair_start)
        causal = _key_query_mask(lambda r, c: r <= c)
        ones8 = jnp.ones((8, HEAD_DIM), BF16)
        dqt_acc[...] = jnp.zeros_like(dqt_acc)

        def prep(qi, _):
            rows = pl.ds(pl.multiple_of(qi * TQ, TQ), TQ)
            for hh in range(NH):
                hi, lo = _split2(do_ref[rows, _hcols(hh)].astype(F32) * o_ref[rows, _hcols(hh)].astype(F32))
                delta_s[hh, pl.ds(qi, 1), :] = (_dot_nt(ones8, hi) + _dot_nt(ones8, lo))[0:1, :]
            return 0

        lax.fori_loop(0, n_q, prep, 0)

        def tile(qis, kj, masked):
            krows = pl.ds(pl.multiple_of(kj * TK, TK), TK)
            heads = range(NH)
            items = [(t, hh) for t in range(len(qis)) for hh in heads]
            rows = [pl.ds(qi * TQ if isinstance(qi, int) else pl.multiple_of(qi * TQ, TQ), TQ) for qi in qis]
            sts = [_dot_nt(k_ref[krows, _hslot(hh)], q_ref[rows[t], _hslot(hh)]) for t, hh in items]
            dps = [_dot_nt(v_ref[krows, _hcols(hh)], do_ref[rows[t], _hcols(hh)]) for t, hh in items]
            pts = [jnp.exp(sts[i] - lse_ref[hh, pl.ds(qis[t], 1), :]) for i, (t, hh) in enumerate(items)]
            if masked:
                pts = [jnp.where(causal, pt, 0.0) for pt in pts]
            dsts = [(pts[i] * (dps[i] - delta_s[hh, pl.ds(qis[t], 1), :])).astype(BF16)
                    for i, (t, hh) in enumerate(items)]
            for i, (t, hh) in enumerate(items):
                dv_s[hh] += _dot_nt(dot_ref[qis[t], _hslot(hh), :], pts[i].astype(BF16))
                dk_s[hh] += _dot_nt(qt_ref[qis[t], _hslot(hh), :], dsts[i])
                dqt_acc[hh, qis[t]] += _dot(kt_ref[kj, _hslot(hh), :], dsts[i])

        def k_loop(kj, _):
            krows = pl.ds(pl.multiple_of(kj * TK, TK), TK)
            dk_s[...] = jnp.zeros_like(dk_s)
            dv_s[...] = jnp.zeros_like(dv_s)
            tile([kj], kj, True)
            left = n_q - 1 - kj

            def pair_loop(i, _):
                tile([kj + 1 + 2 * i, kj + 2 + 2 * i], kj, False)
                return 0

            lax.fori_loop(0, left // 2, pair_loop, 0)
            pl.when(left % 2 == 1)(lambda: tile([n_q - 1], kj, False))
            dqk_ref[1, krows, :] = _heads_cat([_untranspose(dk_s[hh]) for hh in range(NH)]).astype(BF16)
            dv_ref[krows, :] = _heads_cat([_untranspose(dv_s[hh]) for hh in range(NH)]).astype(BF16)
            for hh in range(NH):
                dck_ref[hh, pl.ds(kj, 1), :] = dk_s[hh, C_ONES_Q:C_ONES_Q + 1, :]
            return 0

        lax.fori_loop(0, seq // TK, k_loop, 0)

        def finish(qi, _):
            rows = pl.ds(pl.multiple_of(qi * TQ, TQ), TQ)
            dqk_ref[0, rows, :] = _heads_cat([_untranspose(dqt_acc[hh, qi]) for hh in range(NH)]).astype(BF16)
            for hh in range(NH):
                dcq_ref[hh, pl.ds(qi, 1), :] = dqt_acc[hh, qi, C_ONES_K:C_ONES_K + 1, :]
            return 0

        lax.fori_loop(0, n_q, finish, 0)
        pl.when(last_step)(pair_finish)

    out = pl.pallas_call(
        body, name="fox_bwd", grid=(batch, N_HEADS // NH),
        out_shape=[jax.ShapeDtypeStruct((2, batch * seq, D_BRANCH), BF16),
                   jax.ShapeDtypeStruct((batch * seq, D_BRANCH), BF16),
                   jax.ShapeDtypeStruct((batch, N_HEADS, seq // TQ, TQ), F32),
                   jax.ShapeDtypeStruct((batch, N_HEADS, seq // TK, TK), F32)] + _pair_shapes(partials),
        in_specs=[_slot_spec(seq), _tblock_spec(seq), _slot_spec(seq), _tblock_spec(seq), _group_spec(seq),
                  _group_spec(seq), _group_spec(seq), _tblock_spec(seq), _qrow_spec(seq)] + _hbm_specs(n),
        out_specs=[_group2_spec(seq), _group_spec(seq), _qrow_spec(seq), _qrow_spec(seq)] + _hbm_specs(n),
        scratch_shapes=[pltpu.VMEM((NH, n_q, TQ), F32), pltpu.VMEM((NH, n_q, HEAD_SLOT, TQ), F32),
                        pltpu.VMEM((NH, HEAD_SLOT, TK), F32), pltpu.VMEM((NH, HEAD_SLOT, TK), F32)] + _pair_sems(n),
        compiler_params=_serial_attn_params(),
    )(qf, qft, kf, kft, vf, o, do, dot, lse, *partials)
    return out[0], out[1], out[2], out[3], out[4:]


def _first_last_step():
    step = pl.program_id(0) * pl.num_programs(1) + pl.program_id(1)
    return step == 0, step == pl.num_programs(0) * pl.num_programs(1) - 1


def _sb_fwd(qkvb, vbt, batch, seq, shards):
    n = len(shards)

    def body(q_ref, k_ref, vt_ref, *rest):
        x_refs, (o_ref, lt_ref), out_refs = rest[:n], rest[n:n + 2], rest[n + 2:2 * n + 2]
        run_s, acc_s = rest[2 * n + 2:2 * n + 4]
        gather_start, gather_finish = _gather_plan_direct(x_refs, out_refs, *rest[2 * n + 4:])
        first_step, last_step = _first_last_step()
        pl.when(first_step)(gather_start)
        strict = _key_query_mask(lambda r, c: r < c)
        later = _tri(TK, lambda r, c: c > r)

        def tile(q0, kjs, masked):
            heads = range(NH)
            items = [(t, hh) for t in range(len(kjs)) for hh in heads]
            krows = [pl.ds(kj * TK if isinstance(kj, int) else pl.multiple_of(kj * TK, TK), TK) for kj in kjs]
            zts = [_dot_nt(k_ref[krows[t], _hcols(hh)], q_ref[pl.ds(q0, TQ), _hcols(hh)]) for t, hh in items]
            lgs = [-_softplus(zt) for zt in zts]
            if masked:
                lgs = [jnp.where(strict, lg, 0.0) for lg in lgs]
            parts = [_split2(lg) for lg in lgs]
            sufs = [_dot(later, hi) + _dot(later, lo) for hi, lo in parts]
            sums = [jnp.sum(lg, axis=0, keepdims=True) for lg in lgs]
            runs = {}
            for hh in heads:
                run = run_s[hh]
                for t in range(len(kjs)):
                    runs[t, hh] = run
                    run = run + sums[t * NH + hh]
                run_s[hh] = run
            ats = [jnp.exp(zts[i] + lgs[i] + runs[item] + sufs[i]) for i, item in enumerate(items)]
            if masked:
                ats = [jnp.where(strict, at, 0.0) for at in ats]
            for hh in heads:
                acc_s[hh] += sum(_dot(vt_ref[kjs[t], _hslot(hh), :], ats[t * NH + hh].astype(BF16))
                                 for t in range(len(kjs)))

        def q_loop(qi, _):
            q0 = pl.multiple_of(qi * TQ, TQ)
            run_s[...] = jnp.zeros_like(run_s)
            acc_s[...] = jnp.zeros_like(acc_s)
            tile(q0, [qi], True)

            def pair_loop(i, _):
                tile(q0, [qi - 1 - 2 * i, qi - 2 - 2 * i], False)
                return 0

            lax.fori_loop(0, qi // 2, pair_loop, 0)
            pl.when(qi % 2 == 1)(lambda: tile(q0, [0], False))
            o_ref[pl.ds(q0, TQ), :] = _heads_cat([_untranspose(acc_s[hh]) for hh in range(NH)]).astype(BF16)
            for hh in range(NH):
                lt_ref[hh, pl.ds(qi, 1), :] = run_s[hh]
            return 0

        lax.fori_loop(0, seq // TQ, q_loop, 0)
        pl.when(last_step)(gather_finish)

    out = pl.pallas_call(
        body, name="sb_fwd", grid=(batch, N_HEADS // NH),
        out_shape=[jax.ShapeDtypeStruct((batch * seq, D_BRANCH), BF16),
                   jax.ShapeDtypeStruct((batch, N_HEADS, seq // TQ, TQ), F32)] + _gather_shapes(shards),
        in_specs=[_group3_spec(0, seq), _group3_spec(1, seq), _tblock_spec(seq)] + _hbm_specs(n),
        out_specs=[_group_spec(seq), _qrow_spec(seq)] + _hbm_specs(n),
        scratch_shapes=[pltpu.VMEM((NH, 1, TQ), F32), pltpu.VMEM((NH, HEAD_SLOT, TQ), F32)] + _gather_sems(n),
        compiler_params=_serial_attn_params(),
    )(qkvb, qkvb, vbt, *shards)
    return out[0], out[1], out[2:]


def _sb_bwd(qkvb, kbt, do, ltot, batch, seq, chip_sums):
    n = len(chip_sums)

    def body(q_ref, k_ref, v_ref, kt_ref, do_ref, lt_ref, *rest):
        cs_refs, dqkv_ref, out_refs = rest[:n], rest[n], rest[n + 1:2 * n + 1]
        dk_acc, dv_acc, ls_s, gs_s, dqt_s = rest[2 * n + 1:2 * n + 6]
        chips_start, chips_finish = _chips_plan(cs_refs, out_refs, *rest[2 * n + 6:])
        first_step, last_step = _first_last_step()
        pl.when(first_step)(chips_start)
        strict = _key_query_mask(lambda r, c: r < c)
        upto = _tri(TK, lambda r, c: c <= r)
        before = _tri(TK, lambda r, c: c < r)
        dk_acc[...] = jnp.zeros_like(dk_acc)
        dv_acc[...] = jnp.zeros_like(dv_acc)

        def tile(qi, kj, masked):
            rows = pl.ds(pl.multiple_of(qi * TQ, TQ), TQ)
            krows = pl.ds(pl.multiple_of(kj * TK, TK), TK)
            heads = range(NH)
            qs = [q_ref[rows, _hcols(hh)] for hh in heads]
            douts = [do_ref[rows, _hcols(hh)] for hh in heads]
            zts = [_dot_nt(k_ref[krows, _hcols(hh)], qs[hh]) for hh in heads]
            das = [_dot_nt(v_ref[krows, _hcols(hh)], douts[hh]) for hh in heads]
            lgs = [-_softplus(zt) for zt in zts]
            if masked:
                lgs = [jnp.where(strict, lg, 0.0) for lg in lgs]
            parts = [_split2(lg) for lg in lgs]
            prefs = [_dot(upto, hi) + _dot(upto, lo) for hi, lo in parts]
            ats = [jnp.exp(zts[hh] + lgs[hh] + (lt_ref[hh, pl.ds(qi, 1), :] - ls_s[hh]) - prefs[hh]) for hh in heads]
            if masked:
                ats = [jnp.where(strict, at, 0.0) for at in ats]
            gts = [das[hh] * ats[hh] for hh in heads]
            us = [gs_s[hh] + _dot(before, gts[hh].astype(BF16)) for hh in heads]
            dzts = [(jnp.exp(lgs[hh]) * (gts[hh] + us[hh]) - us[hh]).astype(BF16) for hh in heads]
            for hh in heads:
                dk_acc[hh, krows, :] += _dot(dzts[hh], qs[hh])
                dv_acc[hh, krows, :] += _dot(ats[hh].astype(BF16), douts[hh])
                dqt_s[hh] += _dot(kt_ref[kj, _hslot(hh), :], dzts[hh])
                ls_s[hh] += jnp.sum(lgs[hh], axis=0, keepdims=True)
                gs_s[hh] += jnp.sum(gts[hh], axis=0, keepdims=True)

        def q_loop(qi, _):
            ls_s[...] = jnp.zeros_like(ls_s)
            gs_s[...] = jnp.zeros_like(gs_s)
            dqt_s[...] = jnp.zeros_like(dqt_s)

            def k_loop(kj, _):
                tile(qi, kj, False)
                return 0

            lax.fori_loop(0, qi, k_loop, 0)
            tile(qi, qi, True)
            dqkv_ref[0, pl.ds(pl.multiple_of(qi * TQ, TQ), TQ), :] = _heads_cat(
                [_untranspose(dqt_s[hh]) for hh in range(NH)]).astype(BF16)
            return 0

        lax.fori_loop(0, seq // TQ, q_loop, 0)
        dqkv_ref[1] = _heads_cat([dk_acc[hh] for hh in range(NH)]).astype(BF16)
        dqkv_ref[2] = _heads_cat([dv_acc[hh] for hh in range(NH)]).astype(BF16)
        pl.when(last_step)(chips_finish)

    out = pl.pallas_call(
        body, name="sb_bwd", grid=(batch, N_HEADS // NH),
        out_shape=[jax.ShapeDtypeStruct((3, batch * seq, D_BRANCH), BF16)]
        + [jax.ShapeDtypeStruct(s.shape, s.dtype) for s in chip_sums],
        in_specs=[_group3_spec(0, seq), _group3_spec(1, seq), _group3_spec(2, seq), _tblock_spec(seq),
                  _group_spec(seq), _qrow_spec(seq)] + _hbm_specs(n),
        out_specs=[pl.BlockSpec((3, seq, NH * HEAD_DIM), lambda b, g: (0, b, g))] + _hbm_specs(n),
        scratch_shapes=[pltpu.VMEM((NH, seq, HEAD_DIM), F32), pltpu.VMEM((NH, seq, HEAD_DIM), F32),
                        pltpu.VMEM((NH, 1, TQ), F32), pltpu.VMEM((NH, 1, TQ), F32),
                        pltpu.VMEM((NH, HEAD_SLOT, TQ), F32)] + _chips_sems(n),
        compiler_params=_serial_attn_params(),
    )(qkvb, qkvb, qkvb, kbt, do, ltot, *chip_sums)
    return out[0], out[1:]


def _forget_bwd(dcq_tok, dck_tok, fpre, batch, seq):
    t_len = batch * seq
    tiles = seq // TM

    def rev(i):
        return ((i // tiles) * tiles + (tiles - 1 - i % tiles), 0)

    def body(dcq_ref, dck_ref, f_ref, df_ref, db_ref, carry_ref):
        i = pl.program_id(0)

        @pl.when(i == 0)
        def _():
            db_ref[...] = jnp.zeros_like(db_ref)

        @pl.when(i % tiles == 0)
        def _():
            carry_ref[...] = jnp.zeros_like(carry_ref)

        dc = dcq_ref[...] - dck_ref[...]
        upper = _tri(TM, lambda r, c: c >= r)
        hi, mid, lo = _split3(dc)
        dlogf = carry_ref[...] + _dot(upper, hi) + _dot(upper, mid) + _dot(upper, lo)
        carry_ref[...] = carry_ref[...] + jnp.sum(dc, axis=0, keepdims=True)
        df = dlogf * _sigmoid(-f_ref[...])
        df_ref[...] = df.astype(BF16)
        db_ref[...] += jnp.sum(df, axis=0, keepdims=True)

    return pl.pallas_call(
        body, name="forget_bwd", grid=(t_len // TM,),
        out_shape=(jax.ShapeDtypeStruct((t_len, LANES), BF16), jax.ShapeDtypeStruct((1, LANES), F32)),
        in_specs=[pl.BlockSpec((TM, LANES), rev)] * 3,
        out_specs=(pl.BlockSpec((TM, LANES), rev), _acc_spec((1, LANES))),
        scratch_shapes=[pltpu.VMEM((1, LANES), F32)],
        compiler_params=_seq_params(),
    )(dcq_tok, dck_tok, fpre)


def _mix_fwd(o_fox, o_sb, gl, x, w_bf, w_bs, w_out, b_gate):
    t_len, d = x.shape

    def body(of_ref, os_ref, gl_ref, x_ref, wbf_ref, wbs_ref, wo_ref, bg_ref, x1_ref):
        br_f = _dot(of_ref[...], wbf_ref[...])
        br_s = _dot(os_ref[...], wbs_ref[...])
        ga = _sigmoid(gl_ref[:, :d].astype(F32) + bg_ref[0:1, :])
        gb = _sigmoid(gl_ref[:, d:].astype(F32) + bg_ref[1:2, :])
        merged = ga * br_f + gb * br_s
        x1_ref[...] = x_ref[...] + _dot(merged.astype(BF16), wo_ref[...])

    return pl.pallas_call(
        body, name="mix_fwd", grid=(t_len // TM,),
        out_shape=jax.ShapeDtypeStruct((t_len, d), F32),
        in_specs=[_row_spec(TM, D_BRANCH), _row_spec(TM, D_BRANCH), _row_spec(TM, 2 * d), _row_spec(TM, d),
                  _const_spec(w_bf.shape), _const_spec(w_bs.shape), _const_spec(w_out.shape), _const_spec(b_gate.shape)],
        out_specs=_row_spec(TM, d),
        compiler_params=_seq_params(),
    )(o_fox, o_sb, gl, x, w_bf, w_bs, w_out, b_gate)


def _ff_chunk(d_ff):
    return min(d_ff, 1024)


def _mlp_head_fwd_bwd(x1, p, target, g_mlp, w_up, w_down, g_ple, g_final, w_pg, w_ple):
    t_len, d = x1.shape
    d_ple = p.shape[1]
    d_ff = w_up.shape[1]
    ch = _ff_chunk(d_ff)

    def body(x1_ref, p_ref, t_ref, gm_ref, wu_ref, wd_ref, gp_ref, gf_ref, wpg_ref, wple_ref,
             a_ref, dx2_ref, h3_ref, dpre_ref, dpe_ref, loss_ref, dgp_ref, dgf_ref):
        @pl.when(pl.program_id(0) == 0)
        def _():
            loss_ref[...] = jnp.zeros_like(loss_ref)
            dgp_ref[...] = jnp.zeros_like(dgp_ref)
            dgf_ref[...] = jnp.zeros_like(dgf_ref)

        x1v = x1_ref[...]
        x1n, _ = _rms(x1v)
        h2 = (x1n * gm_ref[...]).astype(BF16)
        x2v = x1v
        for j in range(d_ff // ch):
            a = _dot(h2, wu_ref[:, j * ch:(j + 1) * ch])
            a_ref[:, j * ch:(j + 1) * ch] = a.astype(BF16)
            x2v = x2v + _dot(jnp.square(jnp.maximum(a, 0.0)).astype(BF16), wd_ref[j * ch:(j + 1) * ch, :])
        x2n, r3 = _rms(x2v)
        h3 = (x2n * gp_ref[...]).astype(BF16)
        h3_ref[...] = h3
        gate = _sigmoid(_dot(h3, wpg_ref[...]))
        pe = _dot(p_ref[...].astype(BF16), wple_ref[...])
        x3n, r4 = _rms(x2v + gate * pe)
        err = x3n * gf_ref[...] - t_ref[...]
        loss_ref[...] += jnp.full(loss_ref.shape, (0.5 / d) * jnp.sum(err * err), F32)
        dx3, dgf = _rms_bwd(err * (1.0 / d), x3n, r4, gf_ref[...])
        dgf_ref[...] += dgf
        dpe_ref[...] = (dx3 * gate).astype(BF16)
        dpre = (dx3 * pe * gate * (1.0 - gate)).astype(BF16)
        dpre_ref[...] = dpre
        dres, dgp = _rms_bwd(_dot_nt(dpre, wpg_ref[...]), x2n, r3, gp_ref[...])
        dgp_ref[...] += dgp
        dx2_ref[...] = dx3 + dres

    shp_b = jax.ShapeDtypeStruct((t_len, d), BF16)
    return pl.pallas_call(
        body, name="mlp_head_fwd_bwd", grid=(t_len // TM,),
        out_shape=(jax.ShapeDtypeStruct((t_len, d_ff), BF16), jax.ShapeDtypeStruct((t_len, d), F32), shp_b, shp_b, shp_b,
                   jax.ShapeDtypeStruct((1, LANES), F32), jax.ShapeDtypeStruct((1, d), F32),
                   jax.ShapeDtypeStruct((1, d), F32)),
        in_specs=[_row_spec(TM, d), _row_spec(TM, d_ple), _row_spec(TM, d), _const_spec((1, d)),
                  _const_spec(w_up.shape), _const_spec(w_down.shape), _const_spec((1, d)), _const_spec((1, d)),
                  _const_spec(w_pg.shape), _const_spec(w_ple.shape)],
        out_specs=(_row_spec(TM, d_ff), _row_spec(TM, d), _row_spec(TM, d), _row_spec(TM, d), _row_spec(TM, d),
                   _acc_spec((1, LANES)), _acc_spec((1, d)), _acc_spec((1, d))),
        compiler_params=_seq_params(),
    )(x1, p, target, g_mlp, w_up, w_down, g_ple, g_final, w_pg, w_ple)


def _mlp_bwd(dx2, a, x1, g_mlp, w_up, w_down):
    t_len, d = x1.shape
    d_ff = w_up.shape[1]
    ch = _ff_chunk(d_ff)

    def body(dx2_ref, a_ref, x1_ref, g_ref, wu_ref, wd_ref, dx1_ref, da_ref, h2_ref, dg_ref):
        @pl.when(pl.program_id(0) == 0)
        def _():
            dg_ref[...] = jnp.zeros_like(dg_ref)

        dx2v = dx2_ref[...]
        dx2b = dx2v.astype(BF16)
        xn, r = _rms(x1_ref[...])
        h2_ref[...] = (xn * g_ref[...]).T.astype(BF16)
        dh = jnp.zeros((TM, d), F32)
        for j in range(d_ff // ch):
            dact = _dot_nt(dx2b, wd_ref[j * ch:(j + 1) * ch, :])
            da = (dact * 2.0 * jnp.maximum(a_ref[:, j * ch:(j + 1) * ch].astype(F32), 0.0)).astype(BF16)
            da_ref[:, j * ch:(j + 1) * ch] = da
            dh = dh + _dot_nt(da, wu_ref[:, j * ch:(j + 1) * ch])
        dres, dg = _rms_bwd(dh, xn, r, g_ref[...])
        dg_ref[...] += dg
        dx1_ref[...] = dx2v + dres

    return pl.pallas_call(
        body, name="mlp_bwd", grid=(t_len // TM,),
        out_shape=(jax.ShapeDtypeStruct((t_len, d), F32), jax.ShapeDtypeStruct((t_len, d_ff), BF16),
                   jax.ShapeDtypeStruct((d, t_len), BF16), jax.ShapeDtypeStruct((1, d), F32)),
        in_specs=[_row_spec(TM, d), _row_spec(TM, d_ff), _row_spec(TM, d), _const_spec((1, d)),
                  _const_spec(w_up.shape), _const_spec(w_down.shape)],
        out_specs=(_row_spec(TM, d), _row_spec(TM, d_ff), _col_spec(d, TM), _acc_spec((1, d))),
        compiler_params=_seq_params(),
    )(dx2, a, x1, g_mlp, w_up, w_down)


def _mix_bwd(dx1, o_fox, o_sb, gl, w_bf, w_bs, w_out, b_gate):
    t_len, d = dx1.shape

    def body(dx1_ref, of_ref, os_ref, gl_ref, wbf_ref, wbs_ref, wo_ref, bg_ref,
             mg_ref, dbf_ref, dbs_ref, dgl_ref, dof_ref, dos_ref, dbg_ref, doft_ref):
        @pl.when(pl.program_id(0) == 0)
        def _():
            dbg_ref[...] = jnp.zeros_like(dbg_ref)

        dmerged = _dot_nt(dx1_ref[...].astype(BF16), wo_ref[...])
        br_f = _dot(of_ref[...], wbf_ref[...])
        br_s = _dot(os_ref[...], wbs_ref[...])
        ga = _sigmoid(gl_ref[:, :d].astype(F32) + bg_ref[0:1, :])
        gb = _sigmoid(gl_ref[:, d:].astype(F32) + bg_ref[1:2, :])
        mg_ref[...] = (ga * br_f + gb * br_s).astype(BF16)
        dbf = (dmerged * ga).astype(BF16)
        dbs = (dmerged * gb).astype(BF16)
        dbf_ref[...] = dbf
        dbs_ref[...] = dbs
        dla = dmerged * br_f * ga * (1.0 - ga)
        dlb = dmerged * br_s * gb * (1.0 - gb)
        dgl_ref[:, :d] = dla.astype(BF16)
        dgl_ref[:, d:] = dlb.astype(BF16)
        dbg_ref[0:1, :] += jnp.sum(dla, axis=0, keepdims=True)
        dbg_ref[1:2, :] += jnp.sum(dlb, axis=0, keepdims=True)
        dof = _dot_nt(dbf, wbf_ref[...])
        dof_ref[...] = dof.astype(BF16)
        doft_ref[0] = _slot_rows(dof.T, jnp.zeros((HEAD_DIM, TM), F32)).astype(BF16)
        dos_ref[...] = _dot_nt(dbs, wbs_ref[...]).astype(BF16)

    shp_d = jax.ShapeDtypeStruct((t_len, d), BF16)
    shp_h = jax.ShapeDtypeStruct((t_len, D_BRANCH), BF16)
    return pl.pallas_call(
        body, name="mix_bwd", grid=(t_len // TM,),
        out_shape=(shp_d, shp_d, shp_d, jax.ShapeDtypeStruct((t_len, 2 * d), BF16), shp_h, shp_h,
                   jax.ShapeDtypeStruct((2, d), F32),
                   jax.ShapeDtypeStruct((t_len // TM, N_HEADS * HEAD_SLOT, TM), BF16)),
        in_specs=[_row_spec(TM, d), _row_spec(TM, D_BRANCH), _row_spec(TM, D_BRANCH), _row_spec(TM, 2 * d),
                  _const_spec(w_bf.shape), _const_spec(w_bs.shape), _const_spec(w_out.shape), _const_spec(b_gate.shape)],
        out_specs=(_row_spec(TM, d), _row_spec(TM, d), _row_spec(TM, d), _row_spec(TM, 2 * d),
                   _row_spec(TM, D_BRANCH), _row_spec(TM, D_BRANCH), _acc_spec((2, d)),
                   pl.BlockSpec((1, N_HEADS * HEAD_SLOT, TM), lambda i: (i, 0, 0))),
        compiler_params=_seq_params(),
    )(dx1, o_fox, o_sb, gl, w_bf, w_bs, w_out, b_gate)


def _inproj_bwd(dqk_f, dv_f, dqkv_b, dgl, df, dx1, x, g_mix, w_pad, w_qk, chip_sums):
    t_len, d = x.shape
    lay, _ = _pad_layout(d)
    n = len(chip_sums)
    n_tiles = t_len // TM

    def body(dqk_ref, dvf_ref, db_ref, dgl_ref, df_ref, dx1_ref, x_ref, g_ref, w_ref, wqk_ref, *rest):
        cs_refs, (dx_ref, dg_ref), out_refs = rest[:n], rest[n:n + 2], rest[n + 2:2 * n + 2]
        chips_start, chips_finish = _chips_plan(cs_refs, out_refs, *rest[2 * n + 2:])

        @pl.when(pl.program_id(0) == 0)
        def _():
            dg_ref[...] = jnp.zeros_like(dg_ref)
            chips_start()

        def back(piece, name):
            lo, hi = lay[name]
            return _dot(piece, w_ref[lo:hi, :])

        xn, r = _rms(x_ref[...])
        dh = (back(df_ref[...], "forget") + back(dgl_ref[...], "gates") + _dot(dqk_ref[0], wqk_ref[:D_BRANCH, :])
              + _dot(dqk_ref[1], wqk_ref[D_BRANCH:, :]) + back(dvf_ref[...], "vf") + back(db_ref[0], "qb")
              + back(db_ref[1], "kb") + back(db_ref[2], "vb"))
        dres, dg = _rms_bwd(dh, xn, r, g_ref[...])
        dg_ref[...] += dg
        dx_ref[...] = dx1_ref[...] + dres
        pl.when(pl.program_id(0) == n_tiles - 1)(chips_finish)

    out = pl.pallas_call(
        body, name="inproj_bwd", grid=(n_tiles,),
        out_shape=[jax.ShapeDtypeStruct((t_len, d), F32), jax.ShapeDtypeStruct((1, d), F32)]
        + [jax.ShapeDtypeStruct(s.shape, s.dtype) for s in chip_sums],
        in_specs=[_row3_spec(2, TM, D_BRANCH), _row_spec(TM, D_BRANCH), _row3_spec(3, TM, D_BRANCH),
                  _row_spec(TM, 2 * d), _row_spec(TM, LANES), _row_spec(TM, d), _row_spec(TM, d), _const_spec((1, d)),
                  _const_spec(w_pad.shape), _const_spec(w_qk.shape)] + _hbm_specs(n),
        out_specs=[_row_spec(TM, d), _acc_spec((1, d))] + _hbm_specs(n),
        scratch_shapes=_chips_sems(n),
        compiler_params=_seq_params(),
    )(dqk_f, dv_f, dqkv_b, dgl, df, dx1, x, g_mix, w_pad, w_qk, *chip_sums)
    return out[0], out[1], out[2:]


def _cols_to_slabs(full):
    r, c8 = full.shape
    return full.reshape(r, N_DEV, c8 // N_DEV).transpose(1, 0, 2)


def _slabs_to_cols(slabs):
    n, r, c = slabs.shape
    return slabs.transpose(1, 0, 2).reshape(r, n * c)


def _win_sizes(d):
    return (D_BRANCH, D_BRANCH, D_BRANCH, N_HEADS, D_BRANCH, D_BRANCH, D_BRANCH, d, d)


def _split_win(w_t, d):
    out, off = [], 0
    for s in _win_sizes(d):
        out.append(w_t[off:off + s])
        off += s
    return out


def _to_slots(w_t):
    c = w_t.shape[1]
    return jnp.pad(w_t.reshape(N_HEADS, HEAD_DIM, c), ((0, 0), (0, HEAD_SLOT - HEAD_DIM), (0, 0))).reshape(-1, c)


def _pad_win(w_full_t, d):
    qa, ka, va, fa, qb, kb, vb, ga, gb = _split_win(w_full_t, d)
    scale = HEAD_DIM ** -0.5
    fpad = jnp.pad(fa, ((0, LANES - N_HEADS), (0, 0)))
    w_pad = jnp.concatenate([_to_slots(qa * scale), _to_slots(ka), va, qb * scale, kb, vb, ga, gb, fpad], axis=0)
    return w_pad, jnp.concatenate([qa * scale, ka], axis=0)


def _unpad_dwin(dqk_f, dv_f, dqkv_b, dgates, dforget, d):
    scale = HEAD_DIM ** -0.5
    return jnp.concatenate([dqk_f[0] * scale, dqk_f[1], dv_f, dforget[:N_HEADS],
                            dqkv_b[0] * scale, dqkv_b[1], dqkv_b[2], dgates], axis=0)


def _c_lane_constants():
    row = jnp.arange(LANES)[:, None]
    lane = jnp.arange(N_HEADS * HEAD_SLOT)[None, :]

    def place(first):
        return ((lane // HEAD_SLOT == row % N_HEADS) & (lane % HEAD_SLOT == first + row // N_HEADS)
                & (row < 3 * N_HEADS)).astype(BF16)

    def ones(first):
        off = lane % HEAD_SLOT
        return ((off >= first) & (off < first + 3)).astype(F32)

    return place(C_TERMS_Q), place(C_TERMS_K), ones(C_ONES_Q), ones(C_ONES_K)


def _pad_rows(a, rows):
    return jnp.pad(a, [(0, 0)] * (a.ndim - 2) + [(0, rows - a.shape[-2]), (0, 0)])


def kernel(x, p, g_mix, w_in, b_forget, b_gate, w_branch_fox, w_branch_sb, w_out, g_mlp, w_up, w_down, g_ple, w_ple_gate, w_ple, g_final, loss_target, m_g_mix, m_w_in, m_b_forget, m_b_gate, m_w_branch_fox, m_w_branch_sb, m_w_out, m_g_mlp, m_w_up, m_w_down, m_g_ple, m_w_ple_gate, m_w_ple, m_g_final, v_g_mix, v_w_in, v_b_forget, v_b_gate, v_w_branch_fox, v_w_branch_sb, v_w_out, v_g_mlp, v_w_up, v_w_down, v_g_ple, v_w_ple_gate, v_w_ple, v_g_final):
    batch, seq, d = x.shape
    t_len = batch * seq
    d_ple = p.shape[-1]
    d_ff = w_up.shape[-1] * N_DEV
    dn = d // N_DEV
    fn = d_ff // N_DEV
    my_c = lax.axis_index("c")
    my_dev = 4 * lax.axis_index("x") + 2 * lax.axis_index("y") + my_c

    bg_hi = b_gate[0].astype(BF16)
    bg_r = b_gate[0] - bg_hi.astype(F32)
    bg_mid = bg_r.astype(BF16)
    bg_lo = (bg_r - bg_mid.astype(F32)).astype(BF16)
    narrow_rows = 2 * D_BRANCH + d_ple + 6
    narrow_rows_pad = -(-narrow_rows // 16) * 16
    narrow = _pad_rows(jnp.concatenate(
        [w_branch_fox[0].astype(BF16), w_branch_sb[0].astype(BF16), w_ple[0].astype(BF16), bg_hi, bg_mid, bg_lo],
        axis=0), narrow_rows_pad)
    g_in, = _all_gather([w_in[0].T.astype(BF16)])
    w_pad, w_qk = _pad_win(g_in.reshape(-1, d), d)
    bf_pad = jnp.pad(b_forget, ((0, 0), (0, LANES - N_HEADS)))
    place_q, place_k, ones_q, ones_k = _c_lane_constants()

    x2d = x.reshape(t_len, d)
    p2d = p.reshape(t_len, d_ple)
    tgt2d = loss_target.reshape(t_len, d)
    qf, kf, kft, vf, vft, qkvb, kbt, vbt, gl, fpre, h1, qft = _inproj_fwd(
        x2d, g_mix, w_pad, bf_pad, place_q, place_k, ones_q, ones_k, seq)
    o_sb, ltot, (g_up, g_out, g_down, g_pg, g_narrow) = _sb_fwd(qkvb, vbt, batch, seq, [
        w_up[0].astype(BF16), w_out[0].astype(BF16), w_down[0].astype(BF16), w_ple_gate[0].astype(BF16), narrow])
    o_fox, lse = _fox_fwd(qf, kf, vft, batch, seq)
    w_up_full = _slabs_to_cols(g_up)
    w_out_full = g_out.reshape(d, d)
    w_down_full = g_down.reshape(d_ff, d)
    w_pg_full = g_pg.reshape(d, d)
    w_bf_full = _slabs_to_cols(g_narrow[:, :D_BRANCH])
    w_bs_full = _slabs_to_cols(g_narrow[:, D_BRANCH:2 * D_BRANCH])
    w_ple_full = _slabs_to_cols(g_narrow[:, 2 * D_BRANCH:2 * D_BRANCH + d_ple])
    bg_terms = g_narrow[:, 2 * D_BRANCH + d_ple:narrow_rows].astype(F32)
    b_gate_full = _slabs_to_cols(bg_terms[:, 0:2] + bg_terms[:, 2:4] + bg_terms[:, 4:6])
    x1 = _mix_fwd(o_fox, o_sb, gl, x2d, w_bf_full, w_bs_full, w_out_full, b_gate_full)

    a_up, dx2, h3, dpre, dpe, loss_acc, dg_ple, dg_final = _mlp_head_fwd_bwd(
        x1, p2d, tgt2d, g_mlp, w_up_full, w_down_full, g_ple, g_final.reshape(1, d), w_pg_full, w_ple_full)
    dx1, da_up, h2t, dg_mlp = _mlp_bwd(dx2, a_up, x1, g_mlp, w_up_full, w_down_full)
    merged, dbr_f, dbr_s, dgl, do_fox, do_sb, dbg, do_fox_t = _mix_bwd(
        dx1, o_fox, o_sb, gl, w_bf_full, w_bs_full, w_out_full, b_gate_full)

    def column_shards(name, lhs, rhs, lhs_t=False):
        if (rhs.shape[-1] // N_DEV) % (4 * LANES) == 0:
            return _matmul_tn(name, lhs, rhs, slabs=True, lhs_t=lhs_t)
        return _cols_to_slabs(_matmul_tn(name, lhs, rhs, lhs_t=lhs_t))

    if fn % (4 * LANES) == 0:
        part_up, = _matmul_tn_once("dw_up", [h2t], da_up, slabs=True, lhs_t=True)
    else:
        part_up = column_shards("dw_up", h2t, da_up, lhs_t=True)
    part_out = _matmul_tn("dw_out", merged, dx1).reshape(N_DEV, dn, d)
    part_down = _matmul_tn_once("dw_down", [a_up], dx2, relu2=True)[0].reshape(N_DEV, fn, d)
    part_pg = _matmul_tn("dw_ple_gate", h3, dpre).reshape(N_DEV, dn, d)
    part_narrow = _pad_rows(jnp.concatenate(
        [column_shards("dw_branch_fox", o_fox, dbr_f), column_shards("dw_branch_sb", o_sb, dbr_s),
         column_shards("dw_ple", p2d, dpe)], axis=1), narrow_rows_pad)
    early = [part_up, part_out, part_down, part_pg, lax.optimization_barrier(part_narrow)]

    dqk_f, dv_f, dc_queries, dc_keys, early_recv = _fox_bwd(
        qf, qft, kf, kft, vf, o_fox, do_fox, do_fox_t, lse, batch, seq, early)
    early_sums = [_pair_add("pair_add_%d" % i, pt, rc, my_c) for i, (pt, rc) in enumerate(zip(early, early_recv))]
    dqkv_b, (s_up, s_out, s_down, s_pg, s_narrow) = _sb_bwd(qkvb, kbt, do_sb, ltot, batch, seq, early_sums)
    dcq_tok = dc_queries.reshape(batch, N_HEADS, seq).transpose(0, 2, 1).reshape(t_len, N_HEADS)
    dck_tok = dc_keys.reshape(batch, N_HEADS, seq).transpose(0, 2, 1).reshape(t_len, N_HEADS)
    lane_pad = ((0, 0), (0, LANES - N_HEADS))
    df, db_forget = _forget_bwd(jnp.pad(dcq_tok, lane_pad), jnp.pad(dck_tok, lane_pad), fpre, batch, seq)

    gw_in = _unpad_dwin(*_matmul_tn_once("dw_in_fox_qk", [dqk_f], h1),
                        *_matmul_tn_once("dw_in_rest", [dv_f, dqkv_b, dgl, df], h1), d)
    part_in = lax.optimization_barrier(gw_in.reshape(N_DEV, -1, d))
    recv_in, = _rs_core_pair("reduce_scatter_core_pair_w_in", [part_in])
    grad_x, dg_mix, (s_in,) = _inproj_bwd(dqk_f, dv_f, dqkv_b, dgl, df, dx1, x2d, g_mix, w_pad, w_qk,
                                          [_pair_add("pair_add_w_in", part_in, recv_in, my_c)])

    small = jnp.concatenate([
        dg_mix, dg_mlp, dg_ple, dg_final, jnp.pad(db_forget[:, :N_HEADS], ((0, 0), (0, d - N_HEADS))), dbg,
        jnp.pad(loss_acc[:, :1], ((0, 0), (0, d - 1)))], axis=0)
    small = _all_reduce_small(small)
    loss = small[7, 0]
    small_grads = {
        "g_mix": small[0:1], "g_mlp": small[1:2], "g_ple": small[2:3], "g_final": small[3:4],
        "b_forget": small[4:5, :N_HEADS],
        "b_gate": lax.dynamic_slice_in_dim(small[5:7], my_dev * dn, dn, axis=1),
    }

    weights = {"g_mix": g_mix, "w_in": w_in, "b_forget": b_forget, "b_gate": b_gate, "w_branch_fox": w_branch_fox,
               "w_branch_sb": w_branch_sb, "w_out": w_out, "g_mlp": g_mlp, "w_up": w_up, "w_down": w_down,
               "g_ple": g_ple, "w_ple_gate": w_ple_gate, "w_ple": w_ple, "g_final": g_final}
    m_in = {"g_mix": m_g_mix, "w_in": m_w_in, "b_forget": m_b_forget, "b_gate": m_b_gate,
            "w_branch_fox": m_w_branch_fox, "w_branch_sb": m_w_branch_sb, "w_out": m_w_out, "g_mlp": m_g_mlp,
            "w_up": m_w_up, "w_down": m_w_down, "g_ple": m_g_ple, "w_ple_gate": m_w_ple_gate, "w_ple": m_w_ple,
            "g_final": m_g_final}
    v_in = {"g_mix": v_g_mix, "w_in": v_w_in, "b_forget": v_b_forget, "b_gate": v_b_gate,
            "w_branch_fox": v_w_branch_fox, "w_branch_sb": v_w_branch_sb, "w_out": v_w_out, "g_mlp": v_g_mlp,
            "w_up": v_w_up, "w_down": v_w_down, "g_ple": v_g_ple, "w_ple_gate": v_w_ple_gate, "w_ple": v_w_ple,
            "g_final": v_g_final}
    names = list(weights)

    def as2d(a):
        return a.reshape(-1, a.shape[-1])

    result = {}
    big = {"w_up": (s_up, 0), "w_out": (s_out, 0), "w_down": (s_down, 0), "w_ple_gate": (s_pg, 0),
           "w_branch_fox": (s_narrow, 0), "w_branch_sb": (s_narrow, D_BRANCH), "w_ple": (s_narrow, 2 * D_BRANCH)}
    for n, (parts, off) in big.items():
        result[n] = _adamw_parts("adamw_" + n, as2d(weights[n]), parts, off, as2d(m_in[n]), as2d(v_in[n]))
    result["w_in"] = tuple(r.T for r in _adamw_parts("adamw_w_in", w_in[0].T, s_in, 0, m_w_in[0].T, v_w_in[0].T))
    small_names = list(small_grads)
    small_out = _adamw_small([(as2d(weights[n]), small_grads[n], as2d(m_in[n]), as2d(v_in[n])) for n in small_names])
    for n, (dlt, nm, nv) in zip(small_names, small_out):
        result[n] = (small_grads[n], dlt, nm, nv)
    outs = [[result[n][k].reshape(weights[n].shape) for n in names] for k in range(4)]
    return (loss, grad_x.reshape(x.shape), *outs[0], *outs[1], *outs[2], *outs[3])
```

```python
import jax
import jax.numpy as jnp
from jax import lax
from jax.experimental import pallas as pl
from jax.experimental.pallas import tpu as pltpu

F32 = jnp.float32
BF16 = jnp.bfloat16

HEAD_DIM = 64
N_HEADS = 8
D_BRANCH = N_HEADS * HEAD_DIM
EPS = 1e-6
ADAM_LR = 0.001
ADAM_B1 = 0.9
ADAM_B2 = 0.999
ADAM_EPS = 1e-08
ADAM_WD = 0.01
ADAM_STEP = 10

N_DEV = 8
LANES = 128
TM = 256
TQ = 256
TK = 256
NH = 4
HEAD_SLOT = 128
C_TERMS_Q = 64
C_ONES_K = 64
C_TERMS_K = 67
C_ONES_Q = 67
NEG = -1e30
VMEM_LIMIT = 56 * 1024 * 1024
MESH = pl.DeviceIdType.MESH


def _dot(a, b):
    return jnp.dot(a, b, preferred_element_type=F32)


def _dot_nt(a, b):
    return lax.dot_general(a, b, (((1,), (1,)), ((), ())), preferred_element_type=F32)


def _dot_tn(a, b):
    return lax.dot_general(a, b, (((0,), (0,)), ((), ())), preferred_element_type=F32)


def _sigmoid(x):
    return 1.0 / (1.0 + jnp.exp(-x))


def _softplus(x):
    return jnp.maximum(x, 0.0) + jnp.log(1.0 + jnp.exp(-jnp.abs(x)))


def _split2(x):
    hi = x.astype(BF16)
    lo = (x - hi.astype(F32)).astype(BF16)
    return hi, lo


def _split3(x):
    hi = x.astype(BF16)
    r = x - hi.astype(F32)
    mid = r.astype(BF16)
    lo = (r - mid.astype(F32)).astype(BF16)
    return hi, mid, lo


def _tri(n, rel):
    r = lax.broadcasted_iota(jnp.int32, (n, n), 0)
    c = lax.broadcasted_iota(jnp.int32, (n, n), 1)
    return rel(r, c).astype(BF16)


def _rms(x):
    r = lax.rsqrt(jnp.mean(x * x, axis=-1, keepdims=True) + EPS)
    return x * r, r


def _rms_bwd(dh, xn, r, g):
    dxn = dh * g
    dx = r * (dxn - xn * jnp.mean(dxn * xn, axis=-1, keepdims=True))
    return dx, jnp.sum(dh * xn, axis=0, keepdims=True)


def _row_spec(tm, cols):
    return pl.BlockSpec((tm, cols), lambda i: (i, 0))


def _row3_spec(g, tm, cols):
    return pl.BlockSpec((g, tm, cols), lambda i: (0, i, 0))


def _col_spec(rows, tm):
    return pl.BlockSpec((rows, tm), lambda i: (0, i))


def _const_spec(shape):
    nd = len(shape)
    return pl.BlockSpec(shape, lambda i: (0,) * nd, pipeline_mode=pl.Buffered(1))


def _acc_spec(shape):
    nd = len(shape)
    return pl.BlockSpec(shape, lambda i: (0,) * nd)


def _seq_params():
    return pltpu.CompilerParams(dimension_semantics=("arbitrary",), vmem_limit_bytes=VMEM_LIMIT)


def _mesh_pos():
    return lax.axis_index("x"), lax.axis_index("y"), lax.axis_index("c")


def _other_chips(x, y):
    return [(1 - x, y), (x, 1 - y), (1 - x, 1 - y)]


def _hbm_specs(n):
    return [pl.BlockSpec(memory_space=pl.ANY)] * n


def _gather_plan(x_refs, out_refs, send_sems, recv_sems, local_sems):
    n = len(x_refs)
    x, y, c = _mesh_pos()
    me, sibling = (x, y, c), (x, y, 1 - c)
    x_nb, y_nb, diag = (1 - x, y), (x, 1 - y), (1 - x, 1 - y)

    def slab(a, chip, core):
        return out_refs[a].at[4 * chip[0] + 2 * chip[1] + core]

    def half(ref, upper):
        rows = ref.shape[0]
        cut = (rows // 2) // 16 * 16
        return ref.at[pl.ds(cut, rows - cut)] if upper else ref.at[pl.ds(0, cut)]

    def copy(a, k, ref, to, src=None):
        return pltpu.make_async_remote_copy(
            src_ref=ref if src is None else src, dst_ref=ref,
            send_sem=send_sems.at[8 * a + k], recv_sem=recv_sems.at[8 * a + k], device_id=to, device_id_type=MESH)

    mine = [pltpu.make_async_copy(x_refs[a], slab(a, (x, y), c), local_sems.at[a]) for a in range(n)]
    first = []
    for a in range(n):
        own = slab(a, (x, y), c)
        first += [copy(a, 0, own, sibling, src=x_refs[a]), copy(a, 1, own, (*x_nb, c), src=x_refs[a]),
                  copy(a, 2, own, (*y_nb, c), src=x_refs[a])]

    def start():
        for cp in mine + first:
            cp.start()

    def finish():
        passed = []

        def pass_on(cp):
            passed.append(cp)
            cp.start()

        for a in range(n):
            got = slab(a, x_nb, c)
            copy(a, 1, got, me).wait_recv()
            pass_on(copy(a, 5, got, sibling))
            pass_on(copy(a, 3, half(got, False), (*y_nb, c)))
        for a in range(n):
            got = slab(a, y_nb, c)
            copy(a, 2, got, me).wait_recv()
            pass_on(copy(a, 6, got, sibling))
            pass_on(copy(a, 4, half(got, True), (*x_nb, c)))
        for a in range(n):
            got = slab(a, diag, c)
            copy(a, 3, half(got, False), me).wait_recv()
            copy(a, 4, half(got, True), me).wait_recv()
            pass_on(copy(a, 7, got, sibling))
        for a in range(n):
            copy(a, 0, slab(a, (x, y), 1 - c), me).wait_recv()
            for k, chip in ((5, x_nb), (6, y_nb), (7, diag)):
                copy(a, k, slab(a, chip, 1 - c), me).wait_recv()
        for cp in first + passed:
            cp.wait_send()
        for cp in mine:
            cp.wait()

    return start, finish


def _gather_plan_direct(x_refs, out_refs, send_sems, recv_sems, local_sems):
    n = len(x_refs)
    x, y, c = _mesh_pos()
    me, sibling = (x, y, c), (x, y, 1 - c)
    chips = _other_chips(x, y)

    def index(px, py, pc):
        return 4 * px + 2 * py + pc

    def copy(a, k, block, to, src=None):
        slab = out_refs[a].at[index(*block)]
        return pltpu.make_async_remote_copy(
            src_ref=slab if src is None else src, dst_ref=slab,
            send_sem=send_sems.at[8 * a + k], recv_sem=recv_sems.at[8 * a + k], device_id=to, device_id_type=MESH)

    mine = [pltpu.make_async_copy(x_refs[a], out_refs[a].at[index(*me)], local_sems.at[a]) for a in range(n)]
    first = []
    for a in range(n):
        first.append(copy(a, 0, me, sibling, src=x_refs[a]))
        first += [copy(a, 1 + j, me, (cx, cy, c), src=x_refs[a]) for j, (cx, cy) in enumerate(chips)]

    def start():
        for cp in mine + first:
            cp.start()

    def finish():
        passed = []
        for j, (cx, cy) in enumerate(chips):
            for a in range(n):
                copy(a, 1 + j, (cx, cy, c), me).wait_recv()
                passed.append(copy(a, 4 + j, (cx, cy, c), sibling))
                passed[-1].start()
        for a in range(n):
            copy(a, 0, sibling, me).wait_recv()
            for j, (cx, cy) in enumerate(chips):
                copy(a, 4 + j, (cx, cy, 1 - c), me).wait_recv()
        for cp in first + passed:
            cp.wait_send()
        for cp in mine:
            cp.wait()

    return start, finish


def _gather_shapes(shards):
    return [jax.ShapeDtypeStruct((N_DEV,) + s.shape, s.dtype) for s in shards]


def _gather_sems(n):
    return [pltpu.SemaphoreType.DMA((8 * n,)), pltpu.SemaphoreType.DMA((8 * n,)), pltpu.SemaphoreType.DMA((n,))]


def _all_gather(shards):
    n = len(shards)

    def body(*refs):
        start, finish = _gather_plan(refs[:n], refs[n:2 * n], *refs[2 * n:])
        start()
        finish()

    return pl.pallas_call(
        body, name="all_gather_weights", out_shape=_gather_shapes(shards),
        in_specs=_hbm_specs(n), out_specs=_hbm_specs(n), scratch_shapes=_gather_sems(n),
    )(*shards)


def _pair_plan(p_refs, recv_refs, send_sems, recv_sems):
    n = len(p_refs)
    x, y, c = _mesh_pos()
    sibling = (x, y, 1 - c)

    def start():
        for a in range(n):
            for chip in range(4):
                pltpu.make_async_remote_copy(
                    src_ref=p_refs[a].at[2 * chip + (1 - c)], dst_ref=recv_refs[a].at[chip],
                    send_sem=send_sems.at[a], recv_sem=recv_sems.at[a], device_id=sibling, device_id_type=MESH).start()

    def finish():
        for a in range(n):
            pltpu.make_async_remote_copy(
                src_ref=recv_refs[a], dst_ref=recv_refs[a], send_sem=send_sems.at[a], recv_sem=recv_sems.at[a],
                device_id=sibling, device_id_type=MESH).wait()

    return start, finish


def _pair_shapes(partials):
    return [jax.ShapeDtypeStruct((4,) + s.shape[1:], s.dtype) for s in partials]


def _pair_sems(n):
    return [pltpu.SemaphoreType.DMA((n,)), pltpu.SemaphoreType.DMA((n,))]


def _rs_core_pair(name, partials):
    n = len(partials)

    def body(*refs):
        start, finish = _pair_plan(refs[:n], refs[n:2 * n], *refs[2 * n:])
        start()
        finish()

    return pl.pallas_call(
        body, name=name, out_shape=_pair_shapes(partials),
        in_specs=_hbm_specs(n), out_specs=_hbm_specs(n), scratch_shapes=_pair_sems(n),
    )(*partials)


def _chips_plan(cs_refs, out_refs, send_sems, recv_sems, local_sems):
    n = len(cs_refs)
    x, y, c = _mesh_pos()
    chip = 2 * x + y
    chips = _other_chips(x, y)
    mine = [pltpu.make_async_copy(cs_refs[a].at[chip], out_refs[a].at[chip], local_sems.at[a]) for a in range(n)]
    sends = [pltpu.make_async_remote_copy(
        src_ref=cs_refs[a].at[2 * cx + cy], dst_ref=out_refs[a].at[chip],
        send_sem=send_sems.at[3 * a + j], recv_sem=recv_sems.at[3 * a + j],
        device_id=(cx, cy, c), device_id_type=MESH) for a in range(n) for j, (cx, cy) in enumerate(chips)]

    def start():
        for cp in mine + sends:
            cp.start()

    def finish():
        for a in range(n):
            for j, (cx, cy) in enumerate(chips):
                pltpu.make_async_remote_copy(
                    src_ref=cs_refs[a].at[chip], dst_ref=out_refs[a].at[2 * cx + cy],
                    send_sem=send_sems.at[3 * a + j], recv_sem=recv_sems.at[3 * a + j],
                    device_id=(x, y, c), device_id_type=MESH).wait_recv()
        for cp in sends:
            cp.wait_send()
        for cp in mine:
            cp.wait()

    return start, finish


def _chips_sems(n):
    return [pltpu.SemaphoreType.DMA((3 * n,)), pltpu.SemaphoreType.DMA((3 * n,)), pltpu.SemaphoreType.DMA((n,))]


def _all_reduce_small(vec):
    rows, cols = vec.shape

    def body(x_ref, land_ref, sum_ref, send_sems, recv_sems):
        x, y, c = _mesh_pos()
        me = 4 * x + 2 * y + c
        land_ref[me] = x_ref[...]
        flips = [(fx, fy, fc) for fx in (0, 1) for fy in (0, 1) for fc in (0, 1)][1:]

        def flipped(f):
            return tuple((1 - v) if b else v for v, b in zip((x, y, c), f))

        sends = []
        for k, f in enumerate(flips):
            sends.append(pltpu.make_async_remote_copy(
                src_ref=x_ref, dst_ref=land_ref.at[me], send_sem=send_sems.at[k], recv_sem=recv_sems.at[k],
                device_id=flipped(f), device_id_type=MESH))
            sends[-1].start()
        for k, f in enumerate(flips):
            px, py, pc = flipped(f)
            pltpu.make_async_remote_copy(
                src_ref=x_ref, dst_ref=land_ref.at[4 * px + 2 * py + pc], send_sem=send_sems.at[k],
                recv_sem=recv_sems.at[k], device_id=(x, y, c), device_id_type=MESH).wait_recv()
        for cp in sends:
            cp.wait_send()
        total = land_ref[0]
        for d in range(1, N_DEV):
            total = total + land_ref[d]
        sum_ref[...] = total

    vm = pl.BlockSpec(memory_space=pltpu.VMEM)
    return pl.pallas_call(
        body, name="all_reduce_small",
        out_shape=(jax.ShapeDtypeStruct((N_DEV, rows, cols), F32), jax.ShapeDtypeStruct((rows, cols), F32)),
        in_specs=[vm], out_specs=(vm, vm),
        scratch_shapes=[pltpu.SemaphoreType.DMA((7,)), pltpu.SemaphoreType.DMA((7,))],
    )(vec)[1]


def _block_rows(rows, cols, itemsize, align, row_off=0):
    best = None
    for t in range(align, rows + 1, align):
        if rows % t == 0 and row_off % t == 0 and t * cols * itemsize <= (1 << 20):
            best = t
    return rows if best is None else best


def _pair_add(name, partial, recv, my_c):
    _, rows, cols = partial.shape
    br = _block_rows(rows, cols, 2, 16)

    def body(c_ref, a_ref, b_ref, o_ref):
        o_ref[...] = (a_ref[...].astype(F32) + b_ref[...].astype(F32)).astype(BF16)

    return pl.pallas_call(
        body, name=name,
        grid_spec=pltpu.PrefetchScalarGridSpec(
            num_scalar_prefetch=1, grid=(4, rows // br),
            in_specs=[pl.BlockSpec((None, None, br, cols), lambda j, i, c_ref: (j, c_ref[0], i, 0)),
                      pl.BlockSpec((None, br, cols), lambda j, i, c_ref: (j, i, 0))],
            out_specs=pl.BlockSpec((None, br, cols), lambda j, i, c_ref: (j, i, 0))),
        out_shape=jax.ShapeDtypeStruct((4, rows, cols), BF16),
    )(my_c.reshape(1).astype(jnp.int32), partial.reshape(4, 2, rows, cols), recv)


def _adam_update(w, g, m, v):
    nm = ADAM_B1 * m + (1.0 - ADAM_B1) * g
    nv = ADAM_B2 * v + (1.0 - ADAM_B2) * (g * g)
    m_hat = nm / (1.0 - ADAM_B1 ** ADAM_STEP)
    v_hat = nv / (1.0 - ADAM_B2 ** ADAM_STEP)
    return -ADAM_LR * (m_hat / (jnp.sqrt(v_hat) + ADAM_EPS) + ADAM_WD * w), nm, nv


def _adamw_parts(name, w, parts, row_off, m, v):
    rows, cols = w.shape
    tr = _block_rows(rows, cols, 4, 16, row_off)
    tc = cols
    if tr == rows and rows % 16 != 0 and cols % (2 * LANES) == 0:
        tc = 2 * LANES
    assert rows % tr == 0 and row_off % tr == 0 and (tc == cols or row_off == 0)
    off = row_off // tr

    def body(w_ref, p_ref, m_ref, v_ref, g_ref, d_ref, nm_ref, nv_ref):
        g = p_ref[0].astype(F32)
        for j in range(1, 4):
            g = g + p_ref[j].astype(F32)
        g_ref[...] = g
        d_ref[...], nm_ref[...], nv_ref[...] = _adam_update(w_ref[...], g, m_ref[...], v_ref[...])

    spec = pl.BlockSpec((tr, tc), lambda i, j: (i, j))
    shp = jax.ShapeDtypeStruct((rows, cols), F32)
    return pl.pallas_call(
        body, name=name, grid=(rows // tr, cols // tc), out_shape=(shp,) * 4,
        in_specs=[spec, pl.BlockSpec((4, tr, tc), lambda i, j: (0, off + i, j)), spec, spec], out_specs=(spec,) * 4,
    )(w, parts, m, v)


def _adamw_small(tensors):
    n = len(tensors)

    def body(*refs):
        ins, outs = refs[:4 * n], refs[4 * n:]
        for t in range(n):
            w_ref, g_ref, m_ref, v_ref = ins[4 * t:4 * t + 4]
            d, nm, nv = _adam_update(w_ref[...], g_ref[...], m_ref[...], v_ref[...])
            outs[3 * t][...], outs[3 * t + 1][...], outs[3 * t + 2][...] = d, nm, nv

    vm = pl.BlockSpec(memory_space=pltpu.VMEM)
    out = pl.pallas_call(
        body, name="adamw_small",
        out_shape=[jax.ShapeDtypeStruct(t[0].shape, F32) for t in tensors for _ in range(3)],
        in_specs=[vm] * (4 * n), out_specs=[vm] * (3 * n),
    )(*[a for t in tensors for a in t])
    return [tuple(out[3 * t:3 * t + 3]) for t in range(n)]


def _matmul_tn(name, a, b, relu2=False, slabs=False, lhs_t=False):
    a_groups = a.shape[0] if a.ndim == 3 else 0
    b_groups = b.shape[0] if b.ndim == 3 else 0
    groups = max(a_groups, b_groups, 1)
    assert not (a_groups and b_groups) and not (a_groups and lhs_t)
    a3 = a if a_groups else a[None]
    b3 = b if b_groups else b[None]
    t_len, k_len = a3.shape[1:][::-1] if lhs_t else a3.shape[1:]
    n_len = b3.shape[2]
    tt = min(t_len, 1024)
    tk = min(k_len, 1024)
    tn = n_len // N_DEV if slabs else min(n_len, 1024)
    nt = t_len // tt
    assert not slabs or (groups == 1 and tn <= 1024)

    def body(a_ref, b_ref, o_ref, acc_ref):
        @pl.when(pl.program_id(3) == 0)
        def _():
            acc_ref[...] = jnp.zeros_like(acc_ref)

        av = a_ref[...]
        if relu2:
            av = jnp.square(jnp.maximum(av.astype(F32), 0.0))
        product = _dot if lhs_t else _dot_tn
        acc_ref[...] += product(av.astype(BF16), b_ref[...].astype(BF16))

        @pl.when(pl.program_id(3) == nt - 1)
        def _():
            o_ref[...] = acc_ref[...].astype(BF16)

    def a_group(g):
        return g if a_groups else 0

    def b_group(g):
        return g if b_groups else 0

    if slabs:
        out_shape = jax.ShapeDtypeStruct((N_DEV, k_len, tn), BF16)
        out_spec = pl.BlockSpec((None, tk, tn), lambda g, i, j, t: (j, i, 0))
    else:
        out_shape = jax.ShapeDtypeStruct((groups, k_len, n_len), BF16)
        out_spec = pl.BlockSpec((None, tk, tn), lambda g, i, j, t: (g, i, j))
    out = pl.pallas_call(
        body, name=name, grid=(groups, k_len // tk, n_len // tn, nt), out_shape=out_shape,
        in_specs=[pl.BlockSpec((None, tk, tt), lambda g, i, j, t: (a_group(g), i, t)) if lhs_t
                  else pl.BlockSpec((None, tt, tk), lambda g, i, j, t: (a_group(g), t, i)),
                  pl.BlockSpec((None, tt, tn), lambda g, i, j, t: (b_group(g), t, j))],
        out_specs=out_spec,
        scratch_shapes=[pltpu.VMEM((tk, tn), F32)],
        compiler_params=pltpu.CompilerParams(
            dimension_semantics=("parallel", "parallel", "parallel", "arbitrary"), vmem_limit_bytes=VMEM_LIMIT),
    )(a3, b3)
    return out if (slabs or a_groups or b_groups) else out[0]


def _matmul_tn_once(name, lhs_list, rhs, relu2=False, slabs=False, lhs_t=False):
    t_len, n_len = rhs.shape
    tt = min(t_len, 256 if relu2 else 512)
    nt = t_len // tt
    n_lhs = len(lhs_list)
    assert not (lhs_t or slabs) or (n_lhs == 1 and lhs_list[0].ndim == 2)
    k_shapes = [(a.shape[0], n_len) if lhs_t else a.shape[:-2] + (a.shape[-1], n_len) for a in lhs_list]
    tn = n_len // N_DEV

    def body(*refs):
        a_refs, b_ref = refs[:n_lhs], refs[n_lhs]
        o_refs, acc_refs = refs[n_lhs + 1:2 * n_lhs + 1], refs[2 * n_lhs + 1:]
        step = pl.program_id(0)

        @pl.when(step == 0)
        def _():
            for acc in acc_refs:
                acc[...] = jnp.zeros_like(acc)

        bv = b_ref[...].astype(BF16)

        def piece(av):
            if relu2:
                av = jnp.square(jnp.maximum(av.astype(F32), 0.0))
            return (_dot if lhs_t else _dot_tn)(av.astype(BF16), bv)

        for a_ref, acc in zip(a_refs, acc_refs):
            if len(acc.shape) == 3:
                for g in range(acc.shape[0]):
                    acc[g] += piece(a_ref[g])
            else:
                acc[...] += piece(a_ref[...])

        @pl.when(step == nt - 1)
        def _():
            for o_ref, acc in zip(o_refs, acc_refs):
                if slabs:
                    for j in range(N_DEV):
                        o_ref[j] = acc[:, j * tn:(j + 1) * tn].astype(BF16)
                else:
                    o_ref[...] = acc[...].astype(BF16)

    def lhs_spec(a):
        if lhs_t:
            return pl.BlockSpec((a.shape[0], tt), lambda t: (0, t))
        if a.ndim == 3:
            return pl.BlockSpec((a.shape[0], tt, a.shape[2]), lambda t: (0, t, 0))
        return pl.BlockSpec((tt, a.shape[1]), lambda t: (t, 0))

    out_shapes = [(N_DEV, k_shapes[0][0], tn)] if slabs else k_shapes
    return pl.pallas_call(
        body, name=name, grid=(nt,),
        out_shape=[jax.ShapeDtypeStruct(s, BF16) for s in out_shapes],
        in_specs=[lhs_spec(a) for a in lhs_list] + [pl.BlockSpec((tt, n_len), lambda t: (t, 0))],
        out_specs=[_acc_spec(s) for s in out_shapes],
        scratch_shapes=[pltpu.VMEM(s, F32) for s in k_shapes],
        compiler_params=_seq_params(),
    )(*lhs_list, rhs)


def _pad_layout(d):
    names = ("qf", "kf", "vf", "qb", "kb", "vb", "gates", "forget")
    sizes = (N_HEADS * HEAD_SLOT, N_HEADS * HEAD_SLOT, D_BRANCH, D_BRANCH, D_BRANCH, D_BRANCH, 2 * d, LANES)
    out, off = {}, 0
    for n, s in zip(names, sizes):
        out[n] = (off, off + s)
        off += s
    return out, off


def _slot_rows(xt, extra):
    parts = []
    for h in range(N_HEADS):
        parts += [xt[h * HEAD_DIM:(h + 1) * HEAD_DIM, :], extra]
    return jnp.concatenate(parts, axis=0)


def _inproj_fwd(x, g_mix, w_pad, bf_pad, place_q, place_k, ones_q, ones_k, seq):
    t_len, d = x.shape
    lay, _ = _pad_layout(d)
    tiles_per_seq = seq // TM
    slot_w = N_HEADS * HEAD_SLOT

    def body(x_ref, g_ref, w_ref, bf_ref, pq_ref, pk_ref, oq_ref, ok_ref,
             qf_ref, kf_ref, kft_ref, vf_ref, vft_ref, qkvb_ref, kbt_ref, vbt_ref, gl_ref, fpre_ref, h_ref, qft_ref,
             carry_ref):
        @pl.when(pl.program_id(0) % tiles_per_seq == 0)
        def _():
            carry_ref[...] = jnp.zeros_like(carry_ref)

        def proj(name):
            lo, hi = lay[name]
            return _dot_nt(h, w_ref[lo:hi, :])

        xn, _ = _rms(x_ref[...])
        h = (xn * g_ref[...]).astype(BF16)
        fpre = proj("forget") + bf_ref[...]
        fpre_ref[...] = fpre
        logf = -_softplus(-fpre)
        lower = _tri(TM, lambda r, c: c <= r)
        hi, mid, lo = _split3(logf)
        c_val = carry_ref[...] + _dot(lower, hi) + _dot(lower, mid) + _dot(lower, lo)
        carry_ref[...] = carry_ref[...] + jnp.sum(logf, axis=0, keepdims=True)
        head_lanes = lax.broadcasted_iota(jnp.int32, (TM, LANES), 1) < N_HEADS
        terms = [jnp.where(head_lanes, t.astype(F32), 0.0) for t in _split3(c_val)]
        c_packed = (terms[0] + pltpu.roll(terms[1], N_HEADS, 1) + pltpu.roll(terms[2], 2 * N_HEADS, 1)).astype(BF16)
        qf = proj("qf") + _dot(c_packed, pq_ref[...]) + oq_ref[...]
        qf_ref[...] = qf.astype(BF16)
        qft_ref[0] = qf.T.astype(BF16)
        kf = proj("kf") - _dot(c_packed, pk_ref[...]) + ok_ref[...]
        kf_ref[...] = kf.astype(BF16)
        kft_ref[0] = kf.T.astype(BF16)
        row0 = (lax.broadcasted_iota(jnp.int32, (HEAD_DIM, TM), 0) == 0).astype(F32)
        zeros = jnp.zeros((HEAD_DIM, TM), F32)
        vf = proj("vf")
        vf_ref[...] = vf.astype(BF16)
        vft_ref[0] = _slot_rows(vf.T, row0).astype(BF16)
        qkvb_ref[0] = proj("qb").astype(BF16)
        kb = proj("kb")
        qkvb_ref[1] = kb.astype(BF16)
        kbt_ref[0] = _slot_rows(kb.T, zeros).astype(BF16)
        vb = proj("vb")
        qkvb_ref[2] = vb.astype(BF16)
        vbt_ref[0] = _slot_rows(vb.T, row0).astype(BF16)
        gl_ref[...] = proj("gates").astype(BF16)
        h_ref[...] = h

    n_tiles = t_len // TM
    slot_shape = jax.ShapeDtypeStruct((t_len, slot_w), BF16)
    t_shape = jax.ShapeDtypeStruct((n_tiles, slot_w, TM), BF16)
    t_spec = pl.BlockSpec((1, slot_w, TM), lambda i: (i, 0, 0))
    return pl.pallas_call(
        body, name="inproj_fwd", grid=(n_tiles,),
        out_shape=(slot_shape, slot_shape, t_shape, jax.ShapeDtypeStruct((t_len, D_BRANCH), BF16), t_shape,
                   jax.ShapeDtypeStruct((3, t_len, D_BRANCH), BF16), t_shape, t_shape,
                   jax.ShapeDtypeStruct((t_len, 2 * d), BF16), jax.ShapeDtypeStruct((t_len, LANES), F32),
                   jax.ShapeDtypeStruct((t_len, d), BF16), t_shape),
        in_specs=[_row_spec(TM, d), _const_spec((1, d)), _const_spec(w_pad.shape), _const_spec((1, LANES)),
                  _const_spec(place_q.shape), _const_spec(place_k.shape), _const_spec((1, slot_w)),
                  _const_spec((1, slot_w))],
        out_specs=(_row_spec(TM, slot_w), _row_spec(TM, slot_w), t_spec, _row_spec(TM, D_BRANCH), t_spec,
                   _row3_spec(3, TM, D_BRANCH), t_spec, t_spec, _row_spec(TM, 2 * d), _row_spec(TM, LANES),
                   _row_spec(TM, d), t_spec),
        scratch_shapes=[pltpu.VMEM((1, LANES), F32)],
        compiler_params=_seq_params(),
    )(x, g_mix, w_pad, bf_pad, place_q, place_k, ones_q, ones_k)


def _slot_spec(seq):
    return pl.BlockSpec((seq, NH * HEAD_SLOT), lambda b, g: (b, g))


def _group2_spec(seq):
    return pl.BlockSpec((2, seq, NH * HEAD_DIM), lambda b, g: (0, b, g))


def _group_spec(seq):
    return pl.BlockSpec((seq, NH * HEAD_DIM), lambda b, g: (b, g))


def _group3_spec(which, seq):
    return pl.BlockSpec((None, seq, NH * HEAD_DIM), lambda b, g: (which, b, g))


def _tblock_spec(seq):
    return pl.BlockSpec((seq // TK, NH * HEAD_SLOT, TK), lambda b, g: (b, g, 0))


def _qrow_spec(seq):
    return pl.BlockSpec((None, NH, seq // TQ, TQ), lambda b, g: (b, g, 0, 0))


def _attn_params():
    return pltpu.CompilerParams(dimension_semantics=("parallel", "parallel"), vmem_limit_bytes=VMEM_LIMIT)


def _serial_attn_params():
    return pltpu.CompilerParams(dimension_semantics=("arbitrary", "arbitrary"), vmem_limit_bytes=VMEM_LIMIT)


def _hcols(hh):
    return slice(hh * HEAD_DIM, (hh + 1) * HEAD_DIM)


def _hslot(hh):
    return slice(hh * HEAD_SLOT, (hh + 1) * HEAD_SLOT)


def _key_query_mask(rel):
    r = lax.broadcasted_iota(jnp.int32, (TK, TQ), 0)
    c = lax.broadcasted_iota(jnp.int32, (TK, TQ), 1)
    return rel(r, c)


def _heads_cat(vals):
    return jnp.concatenate(vals, axis=1)


def _untranspose(acc_t):
    return acc_t.T[:, :HEAD_DIM]


def _fox_fwd(qf, kf, vft, batch, seq):
    def body(q_ref, k_ref, vt_ref, o_ref, lse_ref, m_s, acc_s):
        causal = _key_query_mask(lambda r, c: r <= c)

        def tile(q0, kj, masked, n_k=1):
            krows = pl.ds(pl.multiple_of(kj * TK, TK), n_k * TK)
            heads = range(NH)
            sts = [_dot_nt(k_ref[krows, _hslot(hh)], q_ref[pl.ds(q0, TQ), _hslot(hh)]) for hh in heads]
            if masked:
                sts = [jnp.where(causal, st, NEG) for st in sts]
            m_olds = [m_s[hh] for hh in heads]
            m_news = [jnp.maximum(m_olds[hh], jnp.max(sts[hh], axis=0, keepdims=True)) for hh in heads]
            pts = [jnp.exp(sts[hh] - m_news[hh]).astype(BF16) for hh in heads]
            pvs = [sum(_dot(vt_ref[kj + i, _hslot(hh), :], pts[hh][i * TK:(i + 1) * TK]) for i in range(n_k))
                   for hh in heads]
            for hh in heads:
                acc_s[hh] = jnp.exp(m_olds[hh] - m_news[hh]) * acc_s[hh] + pvs[hh]
                m_s[hh] = m_news[hh]

        def q_loop(qi, _):
            q0 = pl.multiple_of(qi * TQ, TQ)
            m_s[...] = jnp.full(m_s.shape, NEG, F32)
            acc_s[...] = jnp.zeros_like(acc_s)

            def pair_loop(i, _):
                tile(q0, 2 * i, False, n_k=2)
                return 0

            lax.fori_loop(0, qi // 2, pair_loop, 0)
            pl.when(qi % 2 == 1)(lambda: tile(q0, qi - 1, False))
            tile(q0, qi, True)
            outs = []
            for hh in range(NH):
                total = acc_s[hh, HEAD_DIM:HEAD_DIM + 1, :]
                outs.append(_untranspose(acc_s[hh] / total))
                lse_ref[hh, pl.ds(qi, 1), :] = m_s[hh] + jnp.log(total)
            o_ref[pl.ds(q0, TQ), :] = _heads_cat(outs).astype(BF16)
            return 0

        lax.fori_loop(0, seq // TQ, q_loop, 0)

    return pl.pallas_call(
        body, name="fox_fwd", grid=(batch, N_HEADS // NH),
        out_shape=(jax.ShapeDtypeStruct((batch * seq, D_BRANCH), BF16),
                   jax.ShapeDtypeStruct((batch, N_HEADS, seq // TQ, TQ), F32)),
        in_specs=[_slot_spec(seq), _slot_spec(seq), _tblock_spec(seq)],
        out_specs=(_group_spec(seq), _qrow_spec(seq)),
        scratch_shapes=[pltpu.VMEM((NH, 1, TQ), F32), pltpu.VMEM((NH, HEAD_SLOT, TQ), F32)],
        compiler_params=_attn_params(),
    )(qf, kf, vft)


def _fox_bwd(qf, qft, kf, kft, vf, o, do, dot, lse, batch, seq, partials):
    n_q = seq // TQ
    n = len(partials)

    def body(q_ref, qt_ref, k_ref, kt_ref, v_ref, o_ref, do_ref, dot_ref, lse_ref, *rest):
        p_refs, (dqk_ref, dv_ref, dcq_ref, dck_ref), recv_refs = rest[:n], rest[n:n + 4], rest[n + 4:2 * n + 4]
        delta_s, dqt_acc, dk_s, dv_s = rest[2 * n + 4:2 * n + 8]
        pair_start, pair_finish = _pair_plan(p_refs, recv_refs, *rest[2 * n + 8:])
        first_step, last_step = _first_last_step()
        pl.when(first_step)(pair_start)
        causal = _key_query_mask(lambda r, c: r <= c)
        ones8 = jnp.ones((8, HEAD_DIM), BF16)
        dqt_acc[...] = jnp.zeros_like(dqt_acc)

        def prep(qi, _):
            rows = pl.ds(pl.multiple_of(qi * TQ, TQ), TQ)
            for hh in range(NH):
                hi, lo = _split2(do_ref[rows, _hcols(hh)].astype(F32) * o_ref[rows, _hcols(hh)].astype(F32))
                delta_s[hh, pl.ds(qi, 1), :] = (_dot_nt(ones8, hi) + _dot_nt(ones8, lo))[0:1, :]
            return 0

        lax.fori_loop(0, n_q, prep, 0)

        def tile(qis, kj, masked):
            krows = pl.ds(pl.multiple_of(kj * TK, TK), TK)
            heads = range(NH)
            items = [(t, hh) for t in range(len(qis)) for hh in heads]
            rows = [pl.ds(qi * TQ if isinstance(qi, int) else pl.multiple_of(qi * TQ, TQ), TQ) for qi in qis]
            sts = [_dot_nt(k_ref[krows, _hslot(hh)], q_ref[rows[t], _hslot(hh)]) for t, hh in items]
            dps = [_dot_nt(v_ref[krows, _hcols(hh)], do_ref[rows[t], _hcols(hh)]) for t, hh in items]
            pts = [jnp.exp(sts[i] - lse_ref[hh, pl.ds(qis[t], 1), :]) for i, (t, hh) in enumerate(items)]
            if masked:
                pts = [jnp.where(causal, pt, 0.0) for pt in pts]
            dsts = [(pts[i] * (dps[i] - delta_s[hh, pl.ds(qis[t], 1), :])).astype(BF16)
                    for i, (t, hh) in enumerate(items)]
            for i, (t, hh) in enumerate(items):
                dv_s[hh] += _dot_nt(dot_ref[qis[t], _hslot(hh), :], pts[i].astype(BF16))
                dk_s[hh] += _dot_nt(qt_ref[qis[t], _hslot(hh), :], dsts[i])
                dqt_acc[hh, qis[t]] += _dot(kt_ref[kj, _hslot(hh), :], dsts[i])

        def k_loop(kj, _):
            krows = pl.ds(pl.multiple_of(kj * TK, TK), TK)
            dk_s[...] = jnp.zeros_like(dk_s)
            dv_s[...] = jnp.zeros_like(dv_s)
            tile([kj], kj, True)
            left = n_q - 1 - kj

            def pair_loop(i, _):
                tile([kj + 1 + 2 * i, kj + 2 + 2 * i], kj, False)
                return 0

            lax.fori_loop(0, left // 2, pair_loop, 0)
            pl.when(left % 2 == 1)(lambda: tile([n_q - 1], kj, False))
            dqk_ref[1, krows, :] = _heads_cat([_untranspose(dk_s[hh]) for hh in range(NH)]).astype(BF16)
            dv_ref[krows, :] = _heads_cat([_untranspose(dv_s[hh]) for hh in range(NH)]).astype(BF16)
            for hh in range(NH):
                dck_ref[hh, pl.ds(kj, 1), :] = dk_s[hh, C_ONES_Q:C_ONES_Q + 1, :]
            return 0

        lax.fori_loop(0, seq // TK, k_loop, 0)

        def finish(qi, _):
            rows = pl.ds(pl.multiple_of(qi * TQ, TQ), TQ)
            dqk_ref[0, rows, :] = _heads_cat([_untranspose(dqt_acc[hh, qi]) for hh in range(NH)]).astype(BF16)
            for hh in range(NH):
                dcq_ref[hh, pl.ds(qi, 1), :] = dqt_acc[hh, qi, C_ONES_K:C_ONES_K + 1, :]
            return 0

        lax.fori_loop(0, n_q, finish, 0)
        pl.when(last_step)(pair_finish)

    out = pl.pallas_call(
        body, name="fox_bwd", grid=(batch, N_HEADS // NH),
        out_shape=[jax.ShapeDtypeStruct((2, batch * seq, D_BRANCH), BF16),
                   jax.ShapeDtypeStruct((batch * seq, D_BRANCH), BF16),
                   jax.ShapeDtypeStruct((batch, N_HEADS, seq // TQ, TQ), F32),
                   jax.ShapeDtypeStruct((batch, N_HEADS, seq // TK, TK), F32)] + _pair_shapes(partials),
        in_specs=[_slot_spec(seq), _tblock_spec(seq), _slot_spec(seq), _tblock_spec(seq), _group_spec(seq),
                  _group_spec(seq), _group_spec(seq), _tblock_spec(seq), _qrow_spec(seq)] + _hbm_specs(n),
        out_specs=[_group2_spec(seq), _group_spec(seq), _qrow_spec(seq), _qrow_spec(seq)] + _hbm_specs(n),
        scratch_shapes=[pltpu.VMEM((NH, n_q, TQ), F32), pltpu.VMEM((NH, n_q, HEAD_SLOT, TQ), F32),
                        pltpu.VMEM((NH, HEAD_SLOT, TK), F32), pltpu.VMEM((NH, HEAD_SLOT, TK), F32)] + _pair_sems(n),
        compiler_params=_serial_attn_params(),
    )(qf, qft, kf, kft, vf, o, do, dot, lse, *partials)
    return out[0], out[1], out[2], out[3], out[4:]


def _first_last_step():
    step = pl.program_id(0) * pl.num_programs(1) + pl.program_id(1)
    return step == 0, step == pl.num_programs(0) * pl.num_programs(1) - 1


def _sb_fwd(qkvb, vbt, batch, seq, shards):
    n = len(shards)

    def body(q_ref, k_ref, vt_ref, *rest):
        x_refs, (o_ref, lt_ref), out_refs = rest[:n], rest[n:n + 2], rest[n + 2:2 * n + 2]
        run_s, acc_s = rest[2 * n + 2:2 * n + 4]
        gather_start, gather_finish = _gather_plan_direct(x_refs, out_refs, *rest[2 * n + 4:])
        first_step, last_step = _first_last_step()
        pl.when(first_step)(gather_start)
        strict = _key_query_mask(lambda r, c: r < c)
        later = _tri(TK, lambda r, c: c > r)

        def tile(q0, kjs, masked):
            heads = range(NH)
            items = [(t, hh) for t in range(len(kjs)) for hh in heads]
            krows = [pl.ds(kj * TK if isinstance(kj, int) else pl.multiple_of(kj * TK, TK), TK) for kj in kjs]
            zts = [_dot_nt(k_ref[krows[t], _hcols(hh)], q_ref[pl.ds(q0, TQ), _hcols(hh)]) for t, hh in items]
            lgs = [-_softplus(zt) for zt in zts]
            if masked:
                lgs = [jnp.where(strict, lg, 0.0) for lg in lgs]
            parts = [_split2(lg) for lg in lgs]
            sufs = [_dot(later, hi) + _dot(later, lo) for hi, lo in parts]
            sums = [jnp.sum(lg, axis=0, keepdims=True) for lg in lgs]
            runs = {}
            for hh in heads:
                run = run_s[hh]
                for t in range(len(kjs)):
                    runs[t, hh] = run
                    run = run + sums[t * NH + hh]
                run_s[hh] = run
            ats = [jnp.exp(zts[i] + lgs[i] + runs[item] + sufs[i]) for i, item in enumerate(items)]
            if masked:
                ats = [jnp.where(strict, at, 0.0) for at in ats]
            for hh in heads:
                acc_s[hh] += sum(_dot(vt_ref[kjs[t], _hslot(hh), :], ats[t * NH + hh].astype(BF16))
                                 for t in range(len(kjs)))

        def q_loop(qi, _):
            q0 = pl.multiple_of(qi * TQ, TQ)
            run_s[...] = jnp.zeros_like(run_s)
            acc_s[...] = jnp.zeros_like(acc_s)
            tile(q0, [qi], True)

            def pair_loop(i, _):
                tile(q0, [qi - 1 - 2 * i, qi - 2 - 2 * i], False)
                return 0

            lax.fori_loop(0, qi // 2, pair_loop, 0)
            pl.when(qi % 2 == 1)(lambda: tile(q0, [0], False))
            o_ref[pl.ds(q0, TQ), :] = _heads_cat([_untranspose(acc_s[hh]) for hh in range(NH)]).astype(BF16)
            for hh in range(NH):
                lt_ref[hh, pl.ds(qi, 1), :] = run_s[hh]
            return 0

        lax.fori_loop(0, seq // TQ, q_loop, 0)
        pl.when(last_step)(gather_finish)

    out = pl.pallas_call(
        body, name="sb_fwd", grid=(batch, N_HEADS // NH),
        out_shape=[jax.ShapeDtypeStruct((batch * seq, D_BRANCH), BF16),
                   jax.ShapeDtypeStruct((batch, N_HEADS, seq // TQ, TQ), F32)] + _gather_shapes(shards),
        in_specs=[_group3_spec(0, seq), _group3_spec(1, seq), _tblock_spec(seq)] + _hbm_specs(n),
        out_specs=[_group_spec(seq), _qrow_spec(seq)] + _hbm_specs(n),
        scratch_shapes=[pltpu.VMEM((NH, 1, TQ), F32), pltpu.VMEM((NH, HEAD_SLOT, TQ), F32)] + _gather_sems(n),
        compiler_params=_serial_attn_params(),
    )(qkvb, qkvb, vbt, *shards)
    return out[0], out[1], out[2:]


def _sb_bwd(qkvb, kbt, do, ltot, batch, seq, chip_sums):
    n = len(chip_sums)

    def body(q_ref, k_ref, v_ref, kt_ref, do_ref, lt_ref, *rest):
        cs_refs, dqkv_ref, out_refs = rest[:n], rest[n], rest[n + 1:2 * n + 1]
        dk_acc, dv_acc, ls_s, gs_s, dqt_s = rest[2 * n + 1:2 * n + 6]
        chips_start, chips_finish = _chips_plan(cs_refs, out_refs, *rest[2 * n + 6:])
        first_step, last_step = _first_last_step()
        pl.when(first_step)(chips_start)
        strict = _key_query_mask(lambda r, c: r < c)
        upto = _tri(TK, lambda r, c: c <= r)
        before = _tri(TK, lambda r, c: c < r)
        dk_acc[...] = jnp.zeros_like(dk_acc)
        dv_acc[...] = jnp.zeros_like(dv_acc)

        def tile(qi, kj, masked):
            for first in range(0, NH, NH // 2):
                tile_heads(qi, kj, masked, first)

        def tile_heads(qi, kj, masked, first):
            rows = pl.ds(pl.multiple_of(qi * TQ, TQ), TQ)
            krows = pl.ds(pl.multiple_of(kj * TK, TK), TK)
            heads = range(NH // 2)
            qs = [q_ref[rows, _hcols(first + hh)] for hh in heads]
            douts = [do_ref[rows, _hcols(first + hh)] for hh in heads]
            zts = [_dot_nt(k_ref[krows, _hcols(first + hh)], qs[hh]) for hh in heads]
            das = [_dot_nt(v_ref[krows, _hcols(first + hh)], douts[hh]) for hh in heads]
            lgs = [-_softplus(zt) for zt in zts]
            if masked:
                lgs = [jnp.where(strict, lg, 0.0) for lg in lgs]
            parts = [_split2(lg) for lg in lgs]
            prefs = [_dot(upto, hi) + _dot(upto, lo) for hi, lo in parts]
            ats = [jnp.exp(zts[hh] + lgs[hh] + (lt_ref[first + hh, pl.ds(qi, 1), :] - ls_s[first + hh]) - prefs[hh])
                   for hh in heads]
            if masked:
                ats = [jnp.where(strict, at, 0.0) for at in ats]
            gts = [das[hh] * ats[hh] for hh in heads]
            us = [gs_s[first + hh] + _dot(before, gts[hh].astype(BF16)) for hh in heads]
            dzts = [(jnp.exp(lgs[hh]) * (gts[hh] + us[hh]) - us[hh]).astype(BF16) for hh in heads]
            for hh in heads:
                dk_acc[first + hh, krows, :] += _dot(dzts[hh], qs[hh])
                dv_acc[first + hh, krows, :] += _dot(ats[hh].astype(BF16), douts[hh])
                dqt_s[first + hh] += _dot(kt_ref[kj, _hslot(first + hh), :], dzts[hh])
                ls_s[first + hh] += jnp.sum(lgs[hh], axis=0, keepdims=True)
                gs_s[first + hh] += jnp.sum(gts[hh], axis=0, keepdims=True)

        def q_loop(qi, _):
            ls_s[...] = jnp.zeros_like(ls_s)
            gs_s[...] = jnp.zeros_like(gs_s)
            dqt_s[...] = jnp.zeros_like(dqt_s)

            def k_loop(kj, _):
                tile(qi, kj, False)
                return 0

            lax.fori_loop(0, qi, k_loop, 0)
            tile(qi, qi, True)
            dqkv_ref[0, pl.ds(pl.multiple_of(qi * TQ, TQ), TQ), :] = _heads_cat(
                [_untranspose(dqt_s[hh]) for hh in range(NH)]).astype(BF16)
            return 0

        lax.fori_loop(0, seq // TQ, q_loop, 0)
        dqkv_ref[1] = _heads_cat([dk_acc[hh] for hh in range(NH)]).astype(BF16)
        dqkv_ref[2] = _heads_cat([dv_acc[hh] for hh in range(NH)]).astype(BF16)
        pl.when(last_step)(chips_finish)

    out = pl.pallas_call(
        body, name="sb_bwd", grid=(batch, N_HEADS // NH),
        out_shape=[jax.ShapeDtypeStruct((3, batch * seq, D_BRANCH), BF16)]
        + [jax.ShapeDtypeStruct(s.shape, s.dtype) for s in chip_sums],
        in_specs=[_group3_spec(0, seq), _group3_spec(1, seq), _group3_spec(2, seq), _tblock_spec(seq),
                  _group_spec(seq), _qrow_spec(seq)] + _hbm_specs(n),
        out_specs=[pl.BlockSpec((3, seq, NH * HEAD_DIM), lambda b, g: (0, b, g))] + _hbm_specs(n),
        scratch_shapes=[pltpu.VMEM((NH, seq, HEAD_DIM), F32), pltpu.VMEM((NH, seq, HEAD_DIM), F32),
                        pltpu.VMEM((NH, 1, TQ), F32), pltpu.VMEM((NH, 1, TQ), F32),
                        pltpu.VMEM((NH, HEAD_SLOT, TQ), F32)] + _chips_sems(n),
        compiler_params=_serial_attn_params(),
    )(qkvb, qkvb, qkvb, kbt, do, ltot, *chip_sums)
    return out[0], out[1:]


def _forget_bwd(dcq_tok, dck_tok, fpre, batch, seq):
    t_len = batch * seq
    tiles = seq // TM

    def rev(i):
        return ((i // tiles) * tiles + (tiles - 1 - i % tiles), 0)

    def body(dcq_ref, dck_ref, f_ref, df_ref, db_ref, carry_ref):
        i = pl.program_id(0)

        @pl.when(i == 0)
        def _():
            db_ref[...] = jnp.zeros_like(db_ref)

        @pl.when(i % tiles == 0)
        def _():
            carry_ref[...] = jnp.zeros_like(carry_ref)

        dc = dcq_ref[...] - dck_ref[...]
        upper = _tri(TM, lambda r, c: c >= r)
        hi, mid, lo = _split3(dc)
        dlogf = carry_ref[...] + _dot(upper, hi) + _dot(upper, mid) + _dot(upper, lo)
        carry_ref[...] = carry_ref[...] + jnp.sum(dc, axis=0, keepdims=True)
        df = dlogf * _sigmoid(-f_ref[...])
        df_ref[...] = df.astype(BF16)
        db_ref[...] += jnp.sum(df, axis=0, keepdims=True)

    return pl.pallas_call(
        body, name="forget_bwd", grid=(t_len // TM,),
        out_shape=(jax.ShapeDtypeStruct((t_len, LANES), BF16), jax.ShapeDtypeStruct((1, LANES), F32)),
        in_specs=[pl.BlockSpec((TM, LANES), rev)] * 3,
        out_specs=(pl.BlockSpec((TM, LANES), rev), _acc_spec((1, LANES))),
        scratch_shapes=[pltpu.VMEM((1, LANES), F32)],
        compiler_params=_seq_params(),
    )(dcq_tok, dck_tok, fpre)


def _mix_fwd(o_fox, o_sb, gl, x, w_bf, w_bs, w_out, b_gate):
    t_len, d = x.shape

    def body(of_ref, os_ref, gl_ref, x_ref, wbf_ref, wbs_ref, wo_ref, bg_ref, x1_ref):
        br_f = _dot(of_ref[...], wbf_ref[...])
        br_s = _dot(os_ref[...], wbs_ref[...])
        ga = _sigmoid(gl_ref[:, :d].astype(F32) + bg_ref[0:1, :])
        gb = _sigmoid(gl_ref[:, d:].astype(F32) + bg_ref[1:2, :])
        merged = ga * br_f + gb * br_s
        x1_ref[...] = x_ref[...] + _dot(merged.astype(BF16), wo_ref[...])

    return pl.pallas_call(
        body, name="mix_fwd", grid=(t_len // TM,),
        out_shape=jax.ShapeDtypeStruct((t_len, d), F32),
        in_specs=[_row_spec(TM, D_BRANCH), _row_spec(TM, D_BRANCH), _row_spec(TM, 2 * d), _row_spec(TM, d),
                  _const_spec(w_bf.shape), _const_spec(w_bs.shape), _const_spec(w_out.shape), _const_spec(b_gate.shape)],
        out_specs=_row_spec(TM, d),
        compiler_params=_seq_params(),
    )(o_fox, o_sb, gl, x, w_bf, w_bs, w_out, b_gate)


def _ff_chunk(d_ff):
    return min(d_ff, 1024)


def _mlp_head_fwd_bwd(x1, p, target, g_mlp, w_up, w_down, g_ple, g_final, w_pg, w_ple):
    t_len, d = x1.shape
    d_ple = p.shape[1]
    d_ff = w_up.shape[1]
    ch = _ff_chunk(d_ff)

    def body(x1_ref, p_ref, t_ref, gm_ref, wu_ref, wd_ref, gp_ref, gf_ref, wpg_ref, wple_ref,
             a_ref, dx2_ref, h3_ref, dpre_ref, dpe_ref, loss_ref, dgp_ref, dgf_ref):
        @pl.when(pl.program_id(0) == 0)
        def _():
            loss_ref[...] = jnp.zeros_like(loss_ref)
            dgp_ref[...] = jnp.zeros_like(dgp_ref)
            dgf_ref[...] = jnp.zeros_like(dgf_ref)

        x1v = x1_ref[...]
        x1n, _ = _rms(x1v)
        h2 = (x1n * gm_ref[...]).astype(BF16)
        x2v = x1v
        for j in range(d_ff // ch):
            a = _dot(h2, wu_ref[:, j * ch:(j + 1) * ch])
            a_ref[:, j * ch:(j + 1) * ch] = a.astype(BF16)
            x2v = x2v + _dot(jnp.square(jnp.maximum(a, 0.0)).astype(BF16), wd_ref[j * ch:(j + 1) * ch, :])
        x2n, r3 = _rms(x2v)
        h3 = (x2n * gp_ref[...]).astype(BF16)
        h3_ref[...] = h3
        gate = _sigmoid(_dot(h3, wpg_ref[...]))
        pe = _dot(p_ref[...].astype(BF16), wple_ref[...])
        x3n, r4 = _rms(x2v + gate * pe)
        err = x3n * gf_ref[...] - t_ref[...]
        loss_ref[...] += jnp.full(loss_ref.shape, (0.5 / d) * jnp.sum(err * err), F32)
        dx3, dgf = _rms_bwd(err * (1.0 / d), x3n, r4, gf_ref[...])
        dgf_ref[...] += dgf
        dpe_ref[...] = (dx3 * gate).astype(BF16)
        dpre = (dx3 * pe * gate * (1.0 - gate)).astype(BF16)
        dpre_ref[...] = dpre
        dres, dgp = _rms_bwd(_dot_nt(dpre, wpg_ref[...]), x2n, r3, gp_ref[...])
        dgp_ref[...] += dgp
        dx2_ref[...] = dx3 + dres

    shp_b = jax.ShapeDtypeStruct((t_len, d), BF16)
    return pl.pallas_call(
        body, name="mlp_head_fwd_bwd", grid=(t_len // TM,),
        out_shape=(jax.ShapeDtypeStruct((t_len, d_ff), BF16), jax.ShapeDtypeStruct((t_len, d), F32), shp_b, shp_b, shp_b,
                   jax.ShapeDtypeStruct((1, LANES), F32), jax.ShapeDtypeStruct((1, d), F32),
                   jax.ShapeDtypeStruct((1, d), F32)),
        in_specs=[_row_spec(TM, d), _row_spec(TM, d_ple), _row_spec(TM, d), _const_spec((1, d)),
                  _const_spec(w_up.shape), _const_spec(w_down.shape), _const_spec((1, d)), _const_spec((1, d)),
                  _const_spec(w_pg.shape), _const_spec(w_ple.shape)],
        out_specs=(_row_spec(TM, d_ff), _row_spec(TM, d), _row_spec(TM, d), _row_spec(TM, d), _row_spec(TM, d),
                   _acc_spec((1, LANES)), _acc_spec((1, d)), _acc_spec((1, d))),
        compiler_params=_seq_params(),
    )(x1, p, target, g_mlp, w_up, w_down, g_ple, g_final, w_pg, w_ple)


def _mlp_bwd(dx2, a, x1, g_mlp, w_up, w_down):
    t_len, d = x1.shape
    d_ff = w_up.shape[1]
    ch = _ff_chunk(d_ff)

    def body(dx2_ref, a_ref, x1_ref, g_ref, wu_ref, wd_ref, dx1_ref, da_ref, h2_ref, dg_ref):
        @pl.when(pl.program_id(0) == 0)
        def _():
            dg_ref[...] = jnp.zeros_like(dg_ref)

        dx2v = dx2_ref[...]
        dx2b = dx2v.astype(BF16)
        xn, r = _rms(x1_ref[...])
        h2_ref[...] = (xn * g_ref[...]).T.astype(BF16)
        dh = jnp.zeros((TM, d), F32)
        for j in range(d_ff // ch):
            dact = _dot_nt(dx2b, wd_ref[j * ch:(j + 1) * ch, :])
            da = (dact * 2.0 * jnp.maximum(a_ref[:, j * ch:(j + 1) * ch].astype(F32), 0.0)).astype(BF16)
            da_ref[:, j * ch:(j + 1) * ch] = da
            dh = dh + _dot_nt(da, wu_ref[:, j * ch:(j + 1) * ch])
        dres, dg = _rms_bwd(dh, xn, r, g_ref[...])
        dg_ref[...] += dg
        dx1_ref[...] = dx2v + dres

    return pl.pallas_call(
        body, name="mlp_bwd", grid=(t_len // TM,),
        out_shape=(jax.ShapeDtypeStruct((t_len, d), F32), jax.ShapeDtypeStruct((t_len, d_ff), BF16),
                   jax.ShapeDtypeStruct((d, t_len), BF16), jax.ShapeDtypeStruct((1, d), F32)),
        in_specs=[_row_spec(TM, d), _row_spec(TM, d_ff), _row_spec(TM, d), _const_spec((1, d)),
                  _const_spec(w_up.shape), _const_spec(w_down.shape)],
        out_specs=(_row_spec(TM, d), _row_spec(TM, d_ff), _col_spec(d, TM), _acc_spec((1, d))),
        compiler_params=_seq_params(),
    )(dx2, a, x1, g_mlp, w_up, w_down)


def _mix_bwd(dx1, o_fox, o_sb, gl, w_bf, w_bs, w_out, b_gate):
    t_len, d = dx1.shape

    def body(dx1_ref, of_ref, os_ref, gl_ref, wbf_ref, wbs_ref, wo_ref, bg_ref,
             mg_ref, dbf_ref, dbs_ref, dgl_ref, dof_ref, dos_ref, dbg_ref, doft_ref):
        @pl.when(pl.program_id(0) == 0)
        def _():
            dbg_ref[...] = jnp.zeros_like(dbg_ref)

        dmerged = _dot_nt(dx1_ref[...].astype(BF16), wo_ref[...])
        br_f = _dot(of_ref[...], wbf_ref[...])
        br_s = _dot(os_ref[...], wbs_ref[...])
        ga = _sigmoid(gl_ref[:, :d].astype(F32) + bg_ref[0:1, :])
        gb = _sigmoid(gl_ref[:, d:].astype(F32) + bg_ref[1:2, :])
        mg_ref[...] = (ga * br_f + gb * br_s).astype(BF16)
        dbf = (dmerged * ga).astype(BF16)
        dbs = (dmerged * gb).astype(BF16)
        dbf_ref[...] = dbf
        dbs_ref[...] = dbs
        dla = dmerged * br_f * ga * (1.0 - ga)
        dlb = dmerged * br_s * gb * (1.0 - gb)
        dgl_ref[:, :d] = dla.astype(BF16)
        dgl_ref[:, d:] = dlb.astype(BF16)
        dbg_ref[0:1, :] += jnp.sum(dla, axis=0, keepdims=True)
        dbg_ref[1:2, :] += jnp.sum(dlb, axis=0, keepdims=True)
        dof = _dot_nt(dbf, wbf_ref[...])
        dof_ref[...] = dof.astype(BF16)
        doft_ref[0] = _slot_rows(dof.T, jnp.zeros((HEAD_DIM, TM), F32)).astype(BF16)
        dos_ref[...] = _dot_nt(dbs, wbs_ref[...]).astype(BF16)

    shp_d = jax.ShapeDtypeStruct((t_len, d), BF16)
    shp_h = jax.ShapeDtypeStruct((t_len, D_BRANCH), BF16)
    return pl.pallas_call(
        body, name="mix_bwd", grid=(t_len // TM,),
        out_shape=(shp_d, shp_d, shp_d, jax.ShapeDtypeStruct((t_len, 2 * d), BF16), shp_h, shp_h,
                   jax.ShapeDtypeStruct((2, d), F32),
                   jax.ShapeDtypeStruct((t_len // TM, N_HEADS * HEAD_SLOT, TM), BF16)),
        in_specs=[_row_spec(TM, d), _row_spec(TM, D_BRANCH), _row_spec(TM, D_BRANCH), _row_spec(TM, 2 * d),
                  _const_spec(w_bf.shape), _const_spec(w_bs.shape), _const_spec(w_out.shape), _const_spec(b_gate.shape)],
        out_specs=(_row_spec(TM, d), _row_spec(TM, d), _row_spec(TM, d), _row_spec(TM, 2 * d),
                   _row_spec(TM, D_BRANCH), _row_spec(TM, D_BRANCH), _acc_spec((2, d)),
                   pl.BlockSpec((1, N_HEADS * HEAD_SLOT, TM), lambda i: (i, 0, 0))),
        compiler_params=_seq_params(),
    )(dx1, o_fox, o_sb, gl, w_bf, w_bs, w_out, b_gate)


def _inproj_bwd(dqk_f, dv_f, dqkv_b, dgl, df, dx1, x, g_mix, w_pad, w_qk, chip_sums):
    t_len, d = x.shape
    lay, _ = _pad_layout(d)
    n = len(chip_sums)
    n_tiles = t_len // TM

    def body(dqk_ref, dvf_ref, db_ref, dgl_ref, df_ref, dx1_ref, x_ref, g_ref, w_ref, wqk_ref, *rest):
        cs_refs, (dx_ref, dg_ref), out_refs = rest[:n], rest[n:n + 2], rest[n + 2:2 * n + 2]
        chips_start, chips_finish = _chips_plan(cs_refs, out_refs, *rest[2 * n + 2:])

        @pl.when(pl.program_id(0) == 0)
        def _():
            dg_ref[...] = jnp.zeros_like(dg_ref)
            chips_start()

        def back(piece, name):
            lo, hi = lay[name]
            return _dot(piece, w_ref[lo:hi, :])

        xn, r = _rms(x_ref[...])
        dh = (back(df_ref[...], "forget") + back(dgl_ref[...], "gates") + _dot(dqk_ref[0], wqk_ref[:D_BRANCH, :])
              + _dot(dqk_ref[1], wqk_ref[D_BRANCH:, :]) + back(dvf_ref[...], "vf") + back(db_ref[0], "qb")
              + back(db_ref[1], "kb") + back(db_ref[2], "vb"))
        dres, dg = _rms_bwd(dh, xn, r, g_ref[...])
        dg_ref[...] += dg
        dx_ref[...] = dx1_ref[...] + dres
        pl.when(pl.program_id(0) == n_tiles - 1)(chips_finish)

    out = pl.pallas_call(
        body, name="inproj_bwd", grid=(n_tiles,),
        out_shape=[jax.ShapeDtypeStruct((t_len, d), F32), jax.ShapeDtypeStruct((1, d), F32)]
        + [jax.ShapeDtypeStruct(s.shape, s.dtype) for s in chip_sums],
        in_specs=[_row3_spec(2, TM, D_BRANCH), _row_spec(TM, D_BRANCH), _row3_spec(3, TM, D_BRANCH),
                  _row_spec(TM, 2 * d), _row_spec(TM, LANES), _row_spec(TM, d), _row_spec(TM, d), _const_spec((1, d)),
                  _const_spec(w_pad.shape), _const_spec(w_qk.shape)] + _hbm_specs(n),
        out_specs=[_row_spec(TM, d), _acc_spec((1, d))] + _hbm_specs(n),
        scratch_shapes=_chips_sems(n),
        compiler_params=_seq_params(),
    )(dqk_f, dv_f, dqkv_b, dgl, df, dx1, x, g_mix, w_pad, w_qk, *chip_sums)
    return out[0], out[1], out[2:]


def _cols_to_slabs(full):
    r, c8 = full.shape
    return full.reshape(r, N_DEV, c8 // N_DEV).transpose(1, 0, 2)


def _slabs_to_cols(slabs):
    n, r, c = slabs.shape
    return slabs.transpose(1, 0, 2).reshape(r, n * c)


def _win_sizes(d):
    return (D_BRANCH, D_BRANCH, D_BRANCH, N_HEADS, D_BRANCH, D_BRANCH, D_BRANCH, d, d)


def _split_win(w_t, d):
    out, off = [], 0
    for s in _win_sizes(d):
        out.append(w_t[off:off + s])
        off += s
    return out


def _to_slots(w_t):
    c = w_t.shape[1]
    return jnp.pad(w_t.reshape(N_HEADS, HEAD_DIM, c), ((0, 0), (0, HEAD_SLOT - HEAD_DIM), (0, 0))).reshape(-1, c)


def _pad_win(w_full_t, d):
    qa, ka, va, fa, qb, kb, vb, ga, gb = _split_win(w_full_t, d)
    scale = HEAD_DIM ** -0.5
    fpad = jnp.pad(fa, ((0, LANES - N_HEADS), (0, 0)))
    w_pad = jnp.concatenate([_to_slots(qa * scale), _to_slots(ka), va, qb * scale, kb, vb, ga, gb, fpad], axis=0)
    return w_pad, jnp.concatenate([qa * scale, ka], axis=0)


def _unpad_dwin(dqk_f, dv_f, dqkv_b, dgates, dforget, d):
    scale = HEAD_DIM ** -0.5
    return jnp.concatenate([dqk_f[0] * scale, dqk_f[1], dv_f, dforget[:N_HEADS],
                            dqkv_b[0] * scale, dqkv_b[1], dqkv_b[2], dgates], axis=0)


def _c_lane_constants():
    row = jnp.arange(LANES)[:, None]
    lane = jnp.arange(N_HEADS * HEAD_SLOT)[None, :]

    def place(first):
        return ((lane // HEAD_SLOT == row % N_HEADS) & (lane % HEAD_SLOT == first + row // N_HEADS)
                & (row < 3 * N_HEADS)).astype(BF16)

    def ones(first):
        off = lane % HEAD_SLOT
        return ((off >= first) & (off < first + 3)).astype(F32)

    return place(C_TERMS_Q), place(C_TERMS_K), ones(C_ONES_Q), ones(C_ONES_K)


def _pad_rows(a, rows):
    return jnp.pad(a, [(0, 0)] * (a.ndim - 2) + [(0, rows - a.shape[-2]), (0, 0)])


def kernel(x, p, g_mix, w_in, b_forget, b_gate, w_branch_fox, w_branch_sb, w_out, g_mlp, w_up, w_down, g_ple, w_ple_gate, w_ple, g_final, loss_target, m_g_mix, m_w_in, m_b_forget, m_b_gate, m_w_branch_fox, m_w_branch_sb, m_w_out, m_g_mlp, m_w_up, m_w_down, m_g_ple, m_w_ple_gate, m_w_ple, m_g_final, v_g_mix, v_w_in, v_b_forget, v_b_gate, v_w_branch_fox, v_w_branch_sb, v_w_out, v_g_mlp, v_w_up, v_w_down, v_g_ple, v_w_ple_gate, v_w_ple, v_g_final):
    batch, seq, d = x.shape
    t_len = batch * seq
    d_ple = p.shape[-1]
    d_ff = w_up.shape[-1] * N_DEV
    dn = d // N_DEV
    fn = d_ff // N_DEV
    my_c = lax.axis_index("c")
    my_dev = 4 * lax.axis_index("x") + 2 * lax.axis_index("y") + my_c

    bg_hi = b_gate[0].astype(BF16)
    bg_r = b_gate[0] - bg_hi.astype(F32)
    bg_mid = bg_r.astype(BF16)
    bg_lo = (bg_r - bg_mid.astype(F32)).astype(BF16)
    narrow_rows = 2 * D_BRANCH + d_ple + 6
    narrow_rows_pad = -(-narrow_rows // 16) * 16
    narrow = _pad_rows(jnp.concatenate(
        [w_branch_fox[0].astype(BF16), w_branch_sb[0].astype(BF16), w_ple[0].astype(BF16), bg_hi, bg_mid, bg_lo],
        axis=0), narrow_rows_pad)
    g_in, = _all_gather([w_in[0].T.astype(BF16)])
    w_pad, w_qk = _pad_win(g_in.reshape(-1, d), d)
    bf_pad = jnp.pad(b_forget, ((0, 0), (0, LANES - N_HEADS)))
    place_q, place_k, ones_q, ones_k = _c_lane_constants()

    x2d = x.reshape(t_len, d)
    p2d = p.reshape(t_len, d_ple)
    tgt2d = loss_target.reshape(t_len, d)
    qf, kf, kft, vf, vft, qkvb, kbt, vbt, gl, fpre, h1, qft = _inproj_fwd(
        x2d, g_mix, w_pad, bf_pad, place_q, place_k, ones_q, ones_k, seq)
    o_sb, ltot, (g_up, g_out, g_down, g_pg, g_narrow) = _sb_fwd(qkvb, vbt, batch, seq, [
        w_up[0].astype(BF16), w_out[0].astype(BF16), w_down[0].astype(BF16), w_ple_gate[0].astype(BF16), narrow])
    o_fox, lse = _fox_fwd(qf, kf, vft, batch, seq)
    w_up_full = _slabs_to_cols(g_up)
    w_out_full = g_out.reshape(d, d)
    w_down_full = g_down.reshape(d_ff, d)
    w_pg_full = g_pg.reshape(d, d)
    w_bf_full = _slabs_to_cols(g_narrow[:, :D_BRANCH])
    w_bs_full = _slabs_to_cols(g_narrow[:, D_BRANCH:2 * D_BRANCH])
    w_ple_full = _slabs_to_cols(g_narrow[:, 2 * D_BRANCH:2 * D_BRANCH + d_ple])
    bg_terms = g_narrow[:, 2 * D_BRANCH + d_ple:narrow_rows].astype(F32)
    b_gate_full = _slabs_to_cols(bg_terms[:, 0:2] + bg_terms[:, 2:4] + bg_terms[:, 4:6])
    x1 = _mix_fwd(o_fox, o_sb, gl, x2d, w_bf_full, w_bs_full, w_out_full, b_gate_full)

    a_up, dx2, h3, dpre, dpe, loss_acc, dg_ple, dg_final = _mlp_head_fwd_bwd(
        x1, p2d, tgt2d, g_mlp, w_up_full, w_down_full, g_ple, g_final.reshape(1, d), w_pg_full, w_ple_full)
    dx1, da_up, h2t, dg_mlp = _mlp_bwd(dx2, a_up, x1, g_mlp, w_up_full, w_down_full)
    merged, dbr_f, dbr_s, dgl, do_fox, do_sb, dbg, do_fox_t = _mix_bwd(
        dx1, o_fox, o_sb, gl, w_bf_full, w_bs_full, w_out_full, b_gate_full)

    def column_shards(name, lhs, rhs, lhs_t=False):
        if (rhs.shape[-1] // N_DEV) % (4 * LANES) == 0:
            return _matmul_tn(name, lhs, rhs, slabs=True, lhs_t=lhs_t)
        return _cols_to_slabs(_matmul_tn(name, lhs, rhs, lhs_t=lhs_t))

    if fn % (4 * LANES) == 0:
        part_up, = _matmul_tn_once("dw_up", [h2t], da_up, slabs=True, lhs_t=True)
    else:
        part_up = column_shards("dw_up", h2t, da_up, lhs_t=True)
    part_out = _matmul_tn("dw_out", merged, dx1).reshape(N_DEV, dn, d)
    part_down = _matmul_tn_once("dw_down", [a_up], dx2, relu2=True)[0].reshape(N_DEV, fn, d)
    part_pg = _matmul_tn("dw_ple_gate", h3, dpre).reshape(N_DEV, dn, d)
    part_narrow = _pad_rows(jnp.concatenate(
        [column_shards("dw_branch_fox", o_fox, dbr_f), column_shards("dw_branch_sb", o_sb, dbr_s),
         column_shards("dw_ple", p2d, dpe)], axis=1), narrow_rows_pad)
    early = [part_up, part_out, part_down, part_pg, lax.optimization_barrier(part_narrow)]

    dqk_f, dv_f, dc_queries, dc_keys, early_recv = _fox_bwd(
        qf, qft, kf, kft, vf, o_fox, do_fox, do_fox_t, lse, batch, seq, early)
    early_sums = [_pair_add("pair_add_%d" % i, pt, rc, my_c) for i, (pt, rc) in enumerate(zip(early, early_recv))]
    dqkv_b, (s_up, s_out, s_down, s_pg, s_narrow) = _sb_bwd(qkvb, kbt, do_sb, ltot, batch, seq, early_sums)
    dcq_tok = dc_queries.reshape(batch, N_HEADS, seq).transpose(0, 2, 1).reshape(t_len, N_HEADS)
    dck_tok = dc_keys.reshape(batch, N_HEADS, seq).transpose(0, 2, 1).reshape(t_len, N_HEADS)
    lane_pad = ((0, 0), (0, LANES - N_HEADS))
    df, db_forget = _forget_bwd(jnp.pad(dcq_tok, lane_pad), jnp.pad(dck_tok, lane_pad), fpre, batch, seq)

    gw_in = _unpad_dwin(*_matmul_tn_once("dw_in_fox_qk", [dqk_f], h1),
                        *_matmul_tn_once("dw_in_rest", [dv_f, dqkv_b, dgl, df], h1), d)
    part_in = lax.optimization_barrier(gw_in.reshape(N_DEV, -1, d))
    recv_in, = _rs_core_pair("reduce_scatter_core_pair_w_in", [part_in])
    grad_x, dg_mix, (s_in,) = _inproj_bwd(dqk_f, dv_f, dqkv_b, dgl, df, dx1, x2d, g_mix, w_pad, w_qk,
                                          [_pair_add("pair_add_w_in", part_in, recv_in, my_c)])

    small = jnp.concatenate([
        dg_mix, dg_mlp, dg_ple, dg_final, jnp.pad(db_forget[:, :N_HEADS], ((0, 0), (0, d - N_HEADS))), dbg,
        jnp.pad(loss_acc[:, :1], ((0, 0), (0, d - 1)))], axis=0)
    small = _all_reduce_small(small)
    loss = small[7, 0]
    small_grads = {
        "g_mix": small[0:1], "g_mlp": small[1:2], "g_ple": small[2:3], "g_final": small[3:4],
        "b_forget": small[4:5, :N_HEADS],
        "b_gate": lax.dynamic_slice_in_dim(small[5:7], my_dev * dn, dn, axis=1),
    }

    weights = {"g_mix": g_mix, "w_in": w_in, "b_forget": b_forget, "b_gate": b_gate, "w_branch_fox": w_branch_fox,
               "w_branch_sb": w_branch_sb, "w_out": w_out, "g_mlp": g_mlp, "w_up": w_up, "w_down": w_down,
               "g_ple": g_ple, "w_ple_gate": w_ple_gate, "w_ple": w_ple, "g_final": g_final}
    m_in = {"g_mix": m_g_mix, "w_in": m_w_in, "b_forget": m_b_forget, "b_gate": m_b_gate,
            "w_branch_fox": m_w_branch_fox, "w_branch_sb": m_w_branch_sb, "w_out": m_w_out, "g_mlp": m_g_mlp,
            "w_up": m_w_up, "w_down": m_w_down, "g_ple": m_g_ple, "w_ple_gate": m_w_ple_gate, "w_ple": m_w_ple,
            "g_final": m_g_final}
    v_in = {"g_mix": v_g_mix, "w_in": v_w_in, "b_forget": v_b_forget, "b_gate": v_b_gate,
            "w_branch_fox": v_w_branch_fox, "w_branch_sb": v_w_branch_sb, "w_out": v_w_out, "g_mlp": v_g_mlp,
            "w_up": v_w_up, "w_down": v_w_down, "g_ple": v_g_ple, "w_ple_gate": v_w_ple_gate, "w_ple": v_w_ple,
            "g_final": v_g_final}
    names = list(weights)

    def as2d(a):
        return a.reshape(-1, a.shape[-1])

    result = {}
    big = {"w_up": (s_up, 0), "w_out": (s_out, 0), "w_down": (s_down, 0), "w_ple_gate": (s_pg, 0),
           "w_branch_fox": (s_narrow, 0), "w_branch_sb": (s_narrow, D_BRANCH), "w_ple": (s_narrow, 2 * D_BRANCH)}
    for n, (parts, off) in big.items():
        result[n] = _adamw_parts("adamw_" + n, as2d(weights[n]), parts, off, as2d(m_in[n]), as2d(v_in[n]))
    result["w_in"] = tuple(r.T for r in _adamw_parts("adamw_w_in", w_in[0].T, s_in, 0, m_w_in[0].T, v_w_in[0].T))
    small_names = list(small_grads)
    small_out = _adamw_small([(as2d(weights[n]), small_grads[n], as2d(m_in[n]), as2d(v_in[n])) for n in small_names])
    for n, (dlt, nm, nv) in zip(small_names, small_out):
        result[n] = (small_grads[n], dlt, nm, nv)
    outs = [[result[n][k].reshape(weights[n].shape) for n in names] for k in range(4)]
    return (loss, grad_x.reshape(x.shape), *outs[0], *outs[1], *outs[2], *outs[3])
```

```python
import jax
import jax.numpy as jnp
from jax import lax
from jax.experimental import pallas as pl
from jax.experimental.pallas import tpu as pltpu

F32 = jnp.float32
BF16 = jnp.bfloat16

HEAD_DIM = 64
N_HEADS = 8
D_BRANCH = N_HEADS * HEAD_DIM
EPS = 1e-6
ADAM_LR = 0.001
ADAM_B1 = 0.9
ADAM_B2 = 0.999
ADAM_EPS = 1e-08
ADAM_WD = 0.01
ADAM_STEP = 10

N_DEV = 8
LANES = 128
TM = 256
TQ = 256
TK = 256
NH = 4
HEAD_SLOT = 128
C_TERMS_Q = 64
C_ONES_K = 64
C_TERMS_K = 67
C_ONES_Q = 67
NEG = -1e30
VMEM_LIMIT = 56 * 1024 * 1024
MESH = pl.DeviceIdType.MESH


def _dot(a, b):
    return jnp.dot(a, b, preferred_element_type=F32)


def _dot_nt(a, b):
    return lax.dot_general(a, b, (((1,), (1,)), ((), ())), preferred_element_type=F32)


def _dot_tn(a, b):
    return lax.dot_general(a, b, (((0,), (0,)), ((), ())), preferred_element_type=F32)


def _sigmoid(x):
    return 1.0 / (1.0 + jnp.exp(-x))


def _softplus(x):
    return jnp.maximum(x, 0.0) + jnp.log(1.0 + jnp.exp(-jnp.abs(x)))


def _split2(x):
    hi = x.astype(BF16)
    lo = (x - hi.astype(F32)).astype(BF16)
    return hi, lo


def _split3(x):
    hi = x.astype(BF16)
    r = x - hi.astype(F32)
    mid = r.astype(BF16)
    lo = (r - mid.astype(F32)).astype(BF16)
    return hi, mid, lo


def _tri(n, rel):
    r = lax.broadcasted_iota(jnp.int32, (n, n), 0)
    c = lax.broadcasted_iota(jnp.int32, (n, n), 1)
    return rel(r, c).astype(BF16)


def _rms(x):
    r = lax.rsqrt(jnp.mean(x * x, axis=-1, keepdims=True) + EPS)
    return x * r, r


def _rms_bwd(dh, xn, r, g):
    dxn = dh * g
    dx = r * (dxn - xn * jnp.mean(dxn * xn, axis=-1, keepdims=True))
    return dx, jnp.sum(dh * xn, axis=0, keepdims=True)


def _row_spec(tm, cols):
    return pl.BlockSpec((tm, cols), lambda i: (i, 0))


def _row3_spec(g, tm, cols):
    return pl.BlockSpec((g, tm, cols), lambda i: (0, i, 0))


def _col_spec(rows, tm):
    return pl.BlockSpec((rows, tm), lambda i: (0, i))


def _const_spec(shape):
    nd = len(shape)
    return pl.BlockSpec(shape, lambda i: (0,) * nd, pipeline_mode=pl.Buffered(1))


def _acc_spec(shape):
    nd = len(shape)
    return pl.BlockSpec(shape, lambda i: (0,) * nd)


def _seq_params():
    return pltpu.CompilerParams(dimension_semantics=("arbitrary",), vmem_limit_bytes=VMEM_LIMIT)


def _mesh_pos():
    return lax.axis_index("x"), lax.axis_index("y"), lax.axis_index("c")


def _other_chips(x, y):
    return [(1 - x, y), (x, 1 - y), (1 - x, 1 - y)]


def _hbm_specs(n):
    return [pl.BlockSpec(memory_space=pl.ANY)] * n


def _gather_plan(x_refs, out_refs, send_sems, recv_sems, local_sems):
    n = len(x_refs)
    x, y, c = _mesh_pos()
    me, sibling = (x, y, c), (x, y, 1 - c)
    x_nb, y_nb, diag = (1 - x, y), (x, 1 - y), (1 - x, 1 - y)

    def slab(a, chip, core):
        return out_refs[a].at[4 * chip[0] + 2 * chip[1] + core]

    def half(ref, upper):
        rows = ref.shape[0]
        cut = (rows // 2) // 16 * 16
        return ref.at[pl.ds(cut, rows - cut)] if upper else ref.at[pl.ds(0, cut)]

    def copy(a, k, ref, to, src=None):
        return pltpu.make_async_remote_copy(
            src_ref=ref if src is None else src, dst_ref=ref,
            send_sem=send_sems.at[8 * a + k], recv_sem=recv_sems.at[8 * a + k], device_id=to, device_id_type=MESH)

    mine = [pltpu.make_async_copy(x_refs[a], slab(a, (x, y), c), local_sems.at[a]) for a in range(n)]
    first = []
    for a in range(n):
        own = slab(a, (x, y), c)
        first += [copy(a, 0, own, sibling, src=x_refs[a]), copy(a, 1, own, (*x_nb, c), src=x_refs[a]),
                  copy(a, 2, own, (*y_nb, c), src=x_refs[a])]

    def start():
        for cp in mine + first:
            cp.start()

    def finish():
        passed = []

        def pass_on(cp):
            passed.append(cp)
            cp.start()

        for a in range(n):
            got = slab(a, x_nb, c)
            copy(a, 1, got, me).wait_recv()
            pass_on(copy(a, 5, got, sibling))
            pass_on(copy(a, 3, half(got, False), (*y_nb, c)))
        for a in range(n):
            got = slab(a, y_nb, c)
            copy(a, 2, got, me).wait_recv()
            pass_on(copy(a, 6, got, sibling))
            pass_on(copy(a, 4, half(got, True), (*x_nb, c)))
        for a in range(n):
            got = slab(a, diag, c)
            copy(a, 3, half(got, False), me).wait_recv()
            copy(a, 4, half(got, True), me).wait_recv()
            pass_on(copy(a, 7, got, sibling))
        for a in range(n):
            copy(a, 0, slab(a, (x, y), 1 - c), me).wait_recv()
            for k, chip in ((5, x_nb), (6, y_nb), (7, diag)):
                copy(a, k, slab(a, chip, 1 - c), me).wait_recv()
        for cp in first + passed:
            cp.wait_send()
        for cp in mine:
            cp.wait()

    return start, finish


def _gather_plan_direct(x_refs, out_refs, send_sems, recv_sems, local_sems):
    n = len(x_refs)
    x, y, c = _mesh_pos()
    me, sibling = (x, y, c), (x, y, 1 - c)
    chips = _other_chips(x, y)

    def index(px, py, pc):
        return 4 * px + 2 * py + pc

    def copy(a, k, block, to, src=None):
        slab = out_refs[a].at[index(*block)]
        return pltpu.make_async_remote_copy(
            src_ref=slab if src is None else src, dst_ref=slab,
            send_sem=send_sems.at[8 * a + k], recv_sem=recv_sems.at[8 * a + k], device_id=to, device_id_type=MESH)

    mine = [pltpu.make_async_copy(x_refs[a], out_refs[a].at[index(*me)], local_sems.at[a]) for a in range(n)]
    first = []
    for a in range(n):
        first.append(copy(a, 0, me, sibling, src=x_refs[a]))
        first += [copy(a, 1 + j, me, (cx, cy, c), src=x_refs[a]) for j, (cx, cy) in enumerate(chips)]

    def start():
        for cp in mine + first:
            cp.start()

    def finish():
        passed = []
        for j, (cx, cy) in enumerate(chips):
            for a in range(n):
                copy(a, 1 + j, (cx, cy, c), me).wait_recv()
                passed.append(copy(a, 4 + j, (cx, cy, c), sibling))
                passed[-1].start()
        for a in range(n):
            copy(a, 0, sibling, me).wait_recv()
            for j, (cx, cy) in enumerate(chips):
                copy(a, 4 + j, (cx, cy, 1 - c), me).wait_recv()
        for cp in first + passed:
            cp.wait_send()
        for cp in mine:
            cp.wait()

    return start, finish


def _gather_shapes(shards):
    return [jax.ShapeDtypeStruct((N_DEV,) + s.shape, s.dtype) for s in shards]


def _gather_sems(n):
    return [pltpu.SemaphoreType.DMA((8 * n,)), pltpu.SemaphoreType.DMA((8 * n,)), pltpu.SemaphoreType.DMA((n,))]


def _all_gather(shards):
    n = len(shards)

    def body(*refs):
        start, finish = _gather_plan(refs[:n], refs[n:2 * n], *refs[2 * n:])
        start()
        finish()

    return pl.pallas_call(
        body, name="all_gather_weights", out_shape=_gather_shapes(shards),
        in_specs=_hbm_specs(n), out_specs=_hbm_specs(n), scratch_shapes=_gather_sems(n),
    )(*shards)


def _pair_plan(p_refs, recv_refs, send_sems, recv_sems):
    n = len(p_refs)
    x, y, c = _mesh_pos()
    sibling = (x, y, 1 - c)

    def start():
        for a in range(n):
            for chip in range(4):
                pltpu.make_async_remote_copy(
                    src_ref=p_refs[a].at[2 * chip + (1 - c)], dst_ref=recv_refs[a].at[chip],
                    send_sem=send_sems.at[a], recv_sem=recv_sems.at[a], device_id=sibling, device_id_type=MESH).start()

    def finish():
        for a in range(n):
            pltpu.make_async_remote_copy(
                src_ref=recv_refs[a], dst_ref=recv_refs[a], send_sem=send_sems.at[a], recv_sem=recv_sems.at[a],
                device_id=sibling, device_id_type=MESH).wait()

    return start, finish


def _pair_shapes(partials):
    return [jax.ShapeDtypeStruct((4,) + s.shape[1:], s.dtype) for s in partials]


def _pair_sems(n):
    return [pltpu.SemaphoreType.DMA((n,)), pltpu.SemaphoreType.DMA((n,))]


def _rs_core_pair(name, partials):
    n = len(partials)

    def body(*refs):
        start, finish = _pair_plan(refs[:n], refs[n:2 * n], *refs[2 * n:])
        start()
        finish()

    return pl.pallas_call(
        body, name=name, out_shape=_pair_shapes(partials),
        in_specs=_hbm_specs(n), out_specs=_hbm_specs(n), scratch_shapes=_pair_sems(n),
    )(*partials)


def _chips_plan(cs_refs, out_refs, send_sems, recv_sems, local_sems):
    n = len(cs_refs)
    x, y, c = _mesh_pos()
    chip = 2 * x + y
    chips = _other_chips(x, y)
    mine = [pltpu.make_async_copy(cs_refs[a].at[chip], out_refs[a].at[chip], local_sems.at[a]) for a in range(n)]
    sends = [pltpu.make_async_remote_copy(
        src_ref=cs_refs[a].at[2 * cx + cy], dst_ref=out_refs[a].at[chip],
        send_sem=send_sems.at[3 * a + j], recv_sem=recv_sems.at[3 * a + j],
        device_id=(cx, cy, c), device_id_type=MESH) for a in range(n) for j, (cx, cy) in enumerate(chips)]

    def start():
        for cp in mine + sends:
            cp.start()

    def finish():
        for a in range(n):
            for j, (cx, cy) in enumerate(chips):
                pltpu.make_async_remote_copy(
                    src_ref=cs_refs[a].at[chip], dst_ref=out_refs[a].at[2 * cx + cy],
                    send_sem=send_sems.at[3 * a + j], recv_sem=recv_sems.at[3 * a + j],
                    device_id=(x, y, c), device_id_type=MESH).wait_recv()
        for cp in sends:
            cp.wait_send()
        for cp in mine:
            cp.wait()

    return start, finish


def _chips_sems(n):
    return [pltpu.SemaphoreType.DMA((3 * n,)), pltpu.SemaphoreType.DMA((3 * n,)), pltpu.SemaphoreType.DMA((n,))]


def _all_reduce_small(vec):
    rows, cols = vec.shape

    def body(x_ref, land_ref, sum_ref, send_sems, recv_sems):
        x, y, c = _mesh_pos()
        me = 4 * x + 2 * y + c
        land_ref[me] = x_ref[...]
        flips = [(fx, fy, fc) for fx in (0, 1) for fy in (0, 1) for fc in (0, 1)][1:]

        def flipped(f):
            return tuple((1 - v) if b else v for v, b in zip((x, y, c), f))

        sends = []
        for k, f in enumerate(flips):
            sends.append(pltpu.make_async_remote_copy(
                src_ref=x_ref, dst_ref=land_ref.at[me], send_sem=send_sems.at[k], recv_sem=recv_sems.at[k],
                device_id=flipped(f), device_id_type=MESH))
            sends[-1].start()
        for k, f in enumerate(flips):
            px, py, pc = flipped(f)
            pltpu.make_async_remote_copy(
                src_ref=x_ref, dst_ref=land_ref.at[4 * px + 2 * py + pc], send_sem=send_sems.at[k],
                recv_sem=recv_sems.at[k], device_id=(x, y, c), device_id_type=MESH).wait_recv()
        for cp in sends:
            cp.wait_send()
        total = land_ref[0]
        for d in range(1, N_DEV):
            total = total + land_ref[d]
        sum_ref[...] = total

    vm = pl.BlockSpec(memory_space=pltpu.VMEM)
    return pl.pallas_call(
        body, name="all_reduce_small",
        out_shape=(jax.ShapeDtypeStruct((N_DEV, rows, cols), F32), jax.ShapeDtypeStruct((rows, cols), F32)),
        in_specs=[vm], out_specs=(vm, vm),
        scratch_shapes=[pltpu.SemaphoreType.DMA((7,)), pltpu.SemaphoreType.DMA((7,))],
    )(vec)[1]


def _block_rows(rows, cols, itemsize, align, row_off=0):
    best = None
    for t in range(align, rows + 1, align):
        if rows % t == 0 and row_off % t == 0 and t * cols * itemsize <= (1 << 20):
            best = t
    return rows if best is None else best


def _pair_add(name, partial, recv, my_c):
    _, rows, cols = partial.shape
    br = _block_rows(rows, cols, 2, 16)

    def body(c_ref, a_ref, b_ref, o_ref):
        o_ref[...] = (a_ref[...].astype(F32) + b_ref[...].astype(F32)).astype(BF16)

    return pl.pallas_call(
        body, name=name,
        grid_spec=pltpu.PrefetchScalarGridSpec(
            num_scalar_prefetch=1, grid=(4, rows // br),
            in_specs=[pl.BlockSpec((None, None, br, cols), lambda j, i, c_ref: (j, c_ref[0], i, 0)),
                      pl.BlockSpec((None, br, cols), lambda j, i, c_ref: (j, i, 0))],
            out_specs=pl.BlockSpec((None, br, cols), lambda j, i, c_ref: (j, i, 0))),
        out_shape=jax.ShapeDtypeStruct((4, rows, cols), BF16),
    )(my_c.reshape(1).astype(jnp.int32), partial.reshape(4, 2, rows, cols), recv)


def _adam_update(w, g, m, v):
    nm = ADAM_B1 * m + (1.0 - ADAM_B1) * g
    nv = ADAM_B2 * v + (1.0 - ADAM_B2) * (g * g)
    m_hat = nm / (1.0 - ADAM_B1 ** ADAM_STEP)
    v_hat = nv / (1.0 - ADAM_B2 ** ADAM_STEP)
    return -ADAM_LR * (m_hat / (jnp.sqrt(v_hat) + ADAM_EPS) + ADAM_WD * w), nm, nv


def _adamw_parts(name, w, parts, row_off, m, v):
    rows, cols = w.shape
    tr = _block_rows(rows, cols, 4, 16, row_off)
    tc = cols
    if tr == rows and rows % 16 != 0 and cols % (2 * LANES) == 0:
        tc = 2 * LANES
    assert rows % tr == 0 and row_off % tr == 0 and (tc == cols or row_off == 0)
    off = row_off // tr

    def body(w_ref, p_ref, m_ref, v_ref, g_ref, d_ref, nm_ref, nv_ref):
        g = p_ref[0].astype(F32)
        for j in range(1, 4):
            g = g + p_ref[j].astype(F32)
        g_ref[...] = g
        d_ref[...], nm_ref[...], nv_ref[...] = _adam_update(w_ref[...], g, m_ref[...], v_ref[...])

    spec = pl.BlockSpec((tr, tc), lambda i, j: (i, j))
    shp = jax.ShapeDtypeStruct((rows, cols), F32)
    return pl.pallas_call(
        body, name=name, grid=(rows // tr, cols // tc), out_shape=(shp,) * 4,
        in_specs=[spec, pl.BlockSpec((4, tr, tc), lambda i, j: (0, off + i, j)), spec, spec], out_specs=(spec,) * 4,
    )(w, parts, m, v)


def _adamw_small(tensors):
    n = len(tensors)

    def body(*refs):
        ins, outs = refs[:4 * n], refs[4 * n:]
        for t in range(n):
            w_ref, g_ref, m_ref, v_ref = ins[4 * t:4 * t + 4]
            d, nm, nv = _adam_update(w_ref[...], g_ref[...], m_ref[...], v_ref[...])
            outs[3 * t][...], outs[3 * t + 1][...], outs[3 * t + 2][...] = d, nm, nv

    vm = pl.BlockSpec(memory_space=pltpu.VMEM)
    out = pl.pallas_call(
        body, name="adamw_small",
        out_shape=[jax.ShapeDtypeStruct(t[0].shape, F32) for t in tensors for _ in range(3)],
        in_specs=[vm] * (4 * n), out_specs=[vm] * (3 * n),
    )(*[a for t in tensors for a in t])
    return [tuple(out[3 * t:3 * t + 3]) for t in range(n)]


def _matmul_tn(name, a, b, relu2=False, slabs=False, lhs_t=False):
    a_groups = a.shape[0] if a.ndim == 3 else 0
    b_groups = b.shape[0] if b.ndim == 3 else 0
    groups = max(a_groups, b_groups, 1)
    assert not (a_groups and b_groups) and not (a_groups and lhs_t)
    a3 = a if a_groups else a[None]
    b3 = b if b_groups else b[None]
    t_len, k_len = a3.shape[1:][::-1] if lhs_t else a3.shape[1:]
    n_len = b3.shape[2]
    tt = min(t_len, 2048)
    tk = min(k_len, 1024)
    tn = n_len // N_DEV if slabs else min(n_len, 1024)
    nt = t_len // tt
    assert not slabs or (groups == 1 and tn <= 1024)

    def body(a_ref, b_ref, o_ref, acc_ref):
        @pl.when(pl.program_id(3) == 0)
        def _():
            acc_ref[...] = jnp.zeros_like(acc_ref)

        av = a_ref[...]
        if relu2:
            av = jnp.square(jnp.maximum(av.astype(F32), 0.0))
        product = _dot if lhs_t else _dot_tn
        acc_ref[...] += product(av.astype(BF16), b_ref[...].astype(BF16))

        @pl.when(pl.program_id(3) == nt - 1)
        def _():
            o_ref[...] = acc_ref[...].astype(BF16)

    def a_group(g):
        return g if a_groups else 0

    def b_group(g):
        return g if b_groups else 0

    if slabs:
        out_shape = jax.ShapeDtypeStruct((N_DEV, k_len, tn), BF16)
        out_spec = pl.BlockSpec((None, tk, tn), lambda g, i, j, t: (j, i, 0))
    else:
        out_shape = jax.ShapeDtypeStruct((groups, k_len, n_len), BF16)
        out_spec = pl.BlockSpec((None, tk, tn), lambda g, i, j, t: (g, i, j))
    out = pl.pallas_call(
        body, name=name, grid=(groups, k_len // tk, n_len // tn, nt), out_shape=out_shape,
        in_specs=[pl.BlockSpec((None, tk, tt), lambda g, i, j, t: (a_group(g), i, t)) if lhs_t
                  else pl.BlockSpec((None, tt, tk), lambda g, i, j, t: (a_group(g), t, i)),
                  pl.BlockSpec((None, tt, tn), lambda g, i, j, t: (b_group(g), t, j))],
        out_specs=out_spec,
        scratch_shapes=[pltpu.VMEM((tk, tn), F32)],
        compiler_params=pltpu.CompilerParams(
            dimension_semantics=("parallel", "parallel", "parallel", "arbitrary"), vmem_limit_bytes=VMEM_LIMIT),
    )(a3, b3)
    return out if (slabs or a_groups or b_groups) else out[0]


def _matmul_tn_once(name, lhs_list, rhs, relu2=False, slabs=False, lhs_t=False):
    t_len, n_len = rhs.shape
    tt = min(t_len, 256 if relu2 else 512)
    nt = t_len // tt
    n_lhs = len(lhs_list)
    assert not (lhs_t or slabs) or (n_lhs == 1 and lhs_list[0].ndim == 2)
    k_shapes = [(a.shape[0], n_len) if lhs_t else a.shape[:-2] + (a.shape[-1], n_len) for a in lhs_list]
    tn = n_len // N_DEV

    def body(*refs):
        a_refs, b_ref = refs[:n_lhs], refs[n_lhs]
        o_refs, acc_refs = refs[n_lhs + 1:2 * n_lhs + 1], refs[2 * n_lhs + 1:]
        step = pl.program_id(0)

        @pl.when(step == 0)
        def _():
            for acc in acc_refs:
                acc[...] = jnp.zeros_like(acc)

        bv = b_ref[...].astype(BF16)

        def piece(av):
            if relu2:
                av = jnp.square(jnp.maximum(av.astype(F32), 0.0))
            return (_dot if lhs_t else _dot_tn)(av.astype(BF16), bv)

        for a_ref, acc in zip(a_refs, acc_refs):
            if len(acc.shape) == 3:
                for g in range(acc.shape[0]):
                    acc[g] += piece(a_ref[g])
            else:
                acc[...] += piece(a_ref[...])

        @pl.when(step == nt - 1)
        def _():
            for o_ref, acc in zip(o_refs, acc_refs):
                if slabs:
                    for j in range(N_DEV):
                        o_ref[j] = acc[:, j * tn:(j + 1) * tn].astype(BF16)
                else:
                    o_ref[...] = acc[...].astype(BF16)

    def lhs_spec(a):
        if lhs_t:
            return pl.BlockSpec((a.shape[0], tt), lambda t: (0, t))
        if a.ndim == 3:
            return pl.BlockSpec((a.shape[0], tt, a.shape[2]), lambda t: (0, t, 0))
        return pl.BlockSpec((tt, a.shape[1]), lambda t: (t, 0))

    out_shapes = [(N_DEV, k_shapes[0][0], tn)] if slabs else k_shapes
    return pl.pallas_call(
        body, name=name, grid=(nt,),
        out_shape=[jax.ShapeDtypeStruct(s, BF16) for s in out_shapes],
        in_specs=[lhs_spec(a) for a in lhs_list] + [pl.BlockSpec((tt, n_len), lambda t: (t, 0))],
        out_specs=[_acc_spec(s) for s in out_shapes],
        scratch_shapes=[pltpu.VMEM(s, F32) for s in k_shapes],
        compiler_params=_seq_params(),
    )(*lhs_list, rhs)


def _pad_layout(d):
    names = ("qf", "kf", "vf", "qb", "kb", "vb", "gates", "forget")
    sizes = (N_HEADS * HEAD_SLOT, N_HEADS * HEAD_SLOT, D_BRANCH, D_BRANCH, D_BRANCH, D_BRANCH, 2 * d, LANES)
    out, off = {}, 0
    for n, s in zip(names, sizes):
        out[n] = (off, off + s)
        off += s
    return out, off


def _slot_rows(xt, extra):
    parts = []
    for h in range(N_HEADS):
        parts += [xt[h * HEAD_DIM:(h + 1) * HEAD_DIM, :], extra]
    return jnp.concatenate(parts, axis=0)


def _inproj_fwd(x, g_mix, w_pad, bf_pad, place_q, place_k, ones_q, ones_k, seq):
    t_len, d = x.shape
    lay, _ = _pad_layout(d)
    tiles_per_seq = seq // TM
    slot_w = N_HEADS * HEAD_SLOT

    def body(x_ref, g_ref, w_ref, bf_ref, pq_ref, pk_ref, oq_ref, ok_ref,
             qf_ref, kf_ref, kft_ref, vf_ref, vft_ref, qkvb_ref, kbt_ref, vbt_ref, gl_ref, fpre_ref, h_ref, qft_ref,
             carry_ref):
        @pl.when(pl.program_id(0) % tiles_per_seq == 0)
        def _():
            carry_ref[...] = jnp.zeros_like(carry_ref)

        def proj(name):
            lo, hi = lay[name]
            return _dot_nt(h, w_ref[lo:hi, :])

        xn, _ = _rms(x_ref[...])
        h = (xn * g_ref[...]).astype(BF16)
        fpre = proj("forget") + bf_ref[...]
        fpre_ref[...] = fpre
        logf = -_softplus(-fpre)
        lower = _tri(TM, lambda r, c: c <= r)
        hi, mid, lo = _split3(logf)
        c_val = carry_ref[...] + _dot(lower, hi) + _dot(lower, mid) + _dot(lower, lo)
        carry_ref[...] = carry_ref[...] + jnp.sum(logf, axis=0, keepdims=True)
        head_lanes = lax.broadcasted_iota(jnp.int32, (TM, LANES), 1) < N_HEADS
        terms = [jnp.where(head_lanes, t.astype(F32), 0.0) for t in _split3(c_val)]
        c_packed = (terms[0] + pltpu.roll(terms[1], N_HEADS, 1) + pltpu.roll(terms[2], 2 * N_HEADS, 1)).astype(BF16)
        qf = proj("qf") + _dot(c_packed, pq_ref[...]) + oq_ref[...]
        qf_ref[...] = qf.astype(BF16)
        qft_ref[0] = qf.T.astype(BF16)
        kf = proj("kf") - _dot(c_packed, pk_ref[...]) + ok_ref[...]
        kf_ref[...] = kf.astype(BF16)
        kft_ref[0] = kf.T.astype(BF16)
        row0 = (lax.broadcasted_iota(jnp.int32, (HEAD_DIM, TM), 0) == 0).astype(F32)
        zeros = jnp.zeros((HEAD_DIM, TM), F32)
        vf = proj("vf")
        vf_ref[...] = vf.astype(BF16)
        vft_ref[0] = _slot_rows(vf.T, row0).astype(BF16)
        qkvb_ref[0] = proj("qb").astype(BF16)
        kb = proj("kb")
        qkvb_ref[1] = kb.astype(BF16)
        kbt_ref[0] = _slot_rows(kb.T, zeros).astype(BF16)
        vb = proj("vb")
        qkvb_ref[2] = vb.astype(BF16)
        vbt_ref[0] = _slot_rows(vb.T, row0).astype(BF16)
        gl_ref[...] = proj("gates").astype(BF16)
        h_ref[...] = h

    n_tiles = t_len // TM
    slot_shape = jax.ShapeDtypeStruct((t_len, slot_w), BF16)
    t_shape = jax.ShapeDtypeStruct((n_tiles, slot_w, TM), BF16)
    t_spec = pl.BlockSpec((1, slot_w, TM), lambda i: (i, 0, 0))
    return pl.pallas_call(
        body, name="inproj_fwd", grid=(n_tiles,),
        out_shape=(slot_shape, slot_shape, t_shape, jax.ShapeDtypeStruct((t_len, D_BRANCH), BF16), t_shape,
                   jax.ShapeDtypeStruct((3, t_len, D_BRANCH), BF16), t_shape, t_shape,
                   jax.ShapeDtypeStruct((t_len, 2 * d), BF16), jax.ShapeDtypeStruct((t_len, LANES), F32),
                   jax.ShapeDtypeStruct((t_len, d), BF16), t_shape),
        in_specs=[_row_spec(TM, d), _const_spec((1, d)), _const_spec(w_pad.shape), _const_spec((1, LANES)),
                  _const_spec(place_q.shape), _const_spec(place_k.shape), _const_spec((1, slot_w)),
                  _const_spec((1, slot_w))],
        out_specs=(_row_spec(TM, slot_w), _row_spec(TM, slot_w), t_spec, _row_spec(TM, D_BRANCH), t_spec,
                   _row3_spec(3, TM, D_BRANCH), t_spec, t_spec, _row_spec(TM, 2 * d), _row_spec(TM, LANES),
                   _row_spec(TM, d), t_spec),
        scratch_shapes=[pltpu.VMEM((1, LANES), F32)],
        compiler_params=_seq_params(),
    )(x, g_mix, w_pad, bf_pad, place_q, place_k, ones_q, ones_k)


def _slot_spec(seq):
    return pl.BlockSpec((seq, NH * HEAD_SLOT), lambda b, g: (b, g))


def _group2_spec(seq):
    return pl.BlockSpec((2, seq, NH * HEAD_DIM), lambda b, g: (0, b, g))


def _group_spec(seq):
    return pl.BlockSpec((seq, NH * HEAD_DIM), lambda b, g: (b, g))


def _group3_spec(which, seq):
    return pl.BlockSpec((None, seq, NH * HEAD_DIM), lambda b, g: (which, b, g))


def _tblock_spec(seq):
    return pl.BlockSpec((seq // TK, NH * HEAD_SLOT, TK), lambda b, g: (b, g, 0))


def _qrow_spec(seq):
    return pl.BlockSpec((None, NH, seq // TQ, TQ), lambda b, g: (b, g, 0, 0))


def _attn_params():
    return pltpu.CompilerParams(dimension_semantics=("parallel", "parallel"), vmem_limit_bytes=VMEM_LIMIT)


def _serial_attn_params():
    return pltpu.CompilerParams(dimension_semantics=("arbitrary", "arbitrary"), vmem_limit_bytes=VMEM_LIMIT)


def _hcols(hh):
    return slice(hh * HEAD_DIM, (hh + 1) * HEAD_DIM)


def _hslot(hh):
    return slice(hh * HEAD_SLOT, (hh + 1) * HEAD_SLOT)


def _key_query_mask(rel):
    r = lax.broadcasted_iota(jnp.int32, (TK, TQ), 0)
    c = lax.broadcasted_iota(jnp.int32, (TK, TQ), 1)
    return rel(r, c)


def _heads_cat(vals):
    return jnp.concatenate(vals, axis=1)


def _untranspose(acc_t):
    return acc_t.T[:, :HEAD_DIM]


def _fox_fwd(qf, kf, vft, batch, seq):
    def body(q_ref, k_ref, vt_ref, o_ref, lse_ref, m_s, acc_s):
        causal = _key_query_mask(lambda r, c: r <= c)

        def tile(q0, kj, masked, n_k=1):
            krows = pl.ds(pl.multiple_of(kj * TK, TK), n_k * TK)
            heads = range(NH)
            sts = [_dot_nt(k_ref[krows, _hslot(hh)], q_ref[pl.ds(q0, TQ), _hslot(hh)]) for hh in heads]
            if masked:
                sts = [jnp.where(causal, st, NEG) for st in sts]
            m_olds = [m_s[hh] for hh in heads]
            m_news = [jnp.maximum(m_olds[hh], jnp.max(sts[hh], axis=0, keepdims=True)) for hh in heads]
            pts = [jnp.exp(sts[hh] - m_news[hh]).astype(BF16) for hh in heads]
            pvs = [sum(_dot(vt_ref[kj + i, _hslot(hh), :], pts[hh][i * TK:(i + 1) * TK]) for i in range(n_k))
                   for hh in heads]
            for hh in heads:
                acc_s[hh] = jnp.exp(m_olds[hh] - m_news[hh]) * acc_s[hh] + pvs[hh]
                m_s[hh] = m_news[hh]

        def q_loop(qi, _):
            q0 = pl.multiple_of(qi * TQ, TQ)
            m_s[...] = jnp.full(m_s.shape, NEG, F32)
            acc_s[...] = jnp.zeros_like(acc_s)

            def pair_loop(i, _):
                tile(q0, 2 * i, False, n_k=2)
                return 0

            lax.fori_loop(0, qi // 2, pair_loop, 0)
            pl.when(qi % 2 == 1)(lambda: tile(q0, qi - 1, False))
            tile(q0, qi, True)
            outs = []
            for hh in range(NH):
                total = acc_s[hh, HEAD_DIM:HEAD_DIM + 1, :]
                outs.append(_untranspose(acc_s[hh] / total))
                lse_ref[hh, pl.ds(qi, 1), :] = m_s[hh] + jnp.log(total)
            o_ref[pl.ds(q0, TQ), :] = _heads_cat(outs).astype(BF16)
            return 0

        lax.fori_loop(0, seq // TQ, q_loop, 0)

    return pl.pallas_call(
        body, name="fox_fwd", grid=(batch, N_HEADS // NH),
        out_shape=(jax.ShapeDtypeStruct((batch * seq, D_BRANCH), BF16),
                   jax.ShapeDtypeStruct((batch, N_HEADS, seq // TQ, TQ), F32)),
        in_specs=[_slot_spec(seq), _slot_spec(seq), _tblock_spec(seq)],
        out_specs=(_group_spec(seq), _qrow_spec(seq)),
        scratch_shapes=[pltpu.VMEM((NH, 1, TQ), F32), pltpu.VMEM((NH, HEAD_SLOT, TQ), F32)],
        compiler_params=_attn_params(),
    )(qf, kf, vft)


def _fox_bwd(qf, qft, kf, kft, vf, o, do, dot, lse, batch, seq, partials):
    n_q = seq // TQ
    n = len(partials)

    def body(q_ref, qt_ref, k_ref, kt_ref, v_ref, o_ref, do_ref, dot_ref, lse_ref, *rest):
        p_refs, (dqk_ref, dv_ref, dcq_ref, dck_ref), recv_refs = rest[:n], rest[n:n + 4], rest[n + 4:2 * n + 4]
        delta_s, dqt_acc, dk_s, dv_s = rest[2 * n + 4:2 * n + 8]
        pair_start, pair_finish = _pair_plan(p_refs, recv_refs, *rest[2 * n + 8:])
        first_step, last_step = _first_last_step()
        pl.when(first_step)(pair_start)
        causal = _key_query_mask(lambda r, c: r <= c)
        ones8 = jnp.ones((8, HEAD_DIM), BF16)
        dqt_acc[...] = jnp.zeros_like(dqt_acc)

        def prep(qi, _):
            rows = pl.ds(pl.multiple_of(qi * TQ, TQ), TQ)
            for hh in range(NH):
                hi, lo = _split2(do_ref[rows, _hcols(hh)].astype(F32) * o_ref[rows, _hcols(hh)].astype(F32))
                delta_s[hh, pl.ds(qi, 1), :] = (_dot_nt(ones8, hi) + _dot_nt(ones8, lo))[0:1, :]
            return 0

        lax.fori_loop(0, n_q, prep, 0)

        def tile(qis, kj, masked):
            krows = pl.ds(pl.multiple_of(kj * TK, TK), TK)
            heads = range(NH)
            items = [(t, hh) for t in range(len(qis)) for hh in heads]
            rows = [pl.ds(qi * TQ if isinstance(qi, int) else pl.multiple_of(qi * TQ, TQ), TQ) for qi in qis]
            sts = [_dot_nt(k_ref[krows, _hslot(hh)], q_ref[rows[t], _hslot(hh)]) for t, hh in items]
            dps = [_dot_nt(v_ref[krows, _hcols(hh)], do_ref[rows[t], _hcols(hh)]) for t, hh in items]
            pts = [jnp.exp(sts[i] - lse_ref[hh, pl.ds(qis[t], 1), :]) for i, (t, hh) in enumerate(items)]
            if masked:
                pts = [jnp.where(causal, pt, 0.0) for pt in pts]
            dsts = [(pts[i] * (dps[i] - delta_s[hh, pl.ds(qis[t], 1), :])).astype(BF16)
                    for i, (t, hh) in enumerate(items)]
            for i, (t, hh) in enumerate(items):
                dv_s[hh] += _dot_nt(dot_ref[qis[t], _hslot(hh), :], pts[i].astype(BF16))
                dk_s[hh] += _dot_nt(qt_ref[qis[t], _hslot(hh), :], dsts[i])
                dqt_acc[hh, qis[t]] += _dot(kt_ref[kj, _hslot(hh), :], dsts[i])

        def k_loop(kj, _):
            krows = pl.ds(pl.multiple_of(kj * TK, TK), TK)
            dk_s[...] = jnp.zeros_like(dk_s)
            dv_s[...] = jnp.zeros_like(dv_s)
            tile([kj], kj, True)
            left = n_q - 1 - kj

            def pair_loop(i, _):
                tile([kj + 1 + 2 * i, kj + 2 + 2 * i], kj, False)
                return 0

            lax.fori_loop(0, left // 2, pair_loop, 0)
            pl.when(left % 2 == 1)(lambda: tile([n_q - 1], kj, False))
            dqk_ref[1, krows, :] = _heads_cat([_untranspose(dk_s[hh]) for hh in range(NH)]).astype(BF16)
            dv_ref[krows, :] = _heads_cat([_untranspose(dv_s[hh]) for hh in range(NH)]).astype(BF16)
            for hh in range(NH):
                dck_ref[hh, pl.ds(kj, 1), :] = dk_s[hh, C_ONES_Q:C_ONES_Q + 1, :]
            return 0

        lax.fori_loop(0, seq // TK, k_loop, 0)

        def finish(qi, _):
            rows = pl.ds(pl.multiple_of(qi * TQ, TQ), TQ)
            dqk_ref[0, rows, :] = _heads_cat([_untranspose(dqt_acc[hh, qi]) for hh in range(NH)]).astype(BF16)
            for hh in range(NH):
                dcq_ref[hh, pl.ds(qi, 1), :] = dqt_acc[hh, qi, C_ONES_K:C_ONES_K + 1, :]
            return 0

        lax.fori_loop(0, n_q, finish, 0)
        pl.when(last_step)(pair_finish)

    out = pl.pallas_call(
        body, name="fox_bwd", grid=(batch, N_HEADS // NH),
        out_shape=[jax.ShapeDtypeStruct((2, batch * seq, D_BRANCH), BF16),
                   jax.ShapeDtypeStruct((batch * seq, D_BRANCH), BF16),
                   jax.ShapeDtypeStruct((batch, N_HEADS, seq // TQ, TQ), F32),
                   jax.ShapeDtypeStruct((batch, N_HEADS, seq // TK, TK), F32)] + _pair_shapes(partials),
        in_specs=[_slot_spec(seq), _tblock_spec(seq), _slot_spec(seq), _tblock_spec(seq), _group_spec(seq),
                  _group_spec(seq), _group_spec(seq), _tblock_spec(seq), _qrow_spec(seq)] + _hbm_specs(n),
        out_specs=[_group2_spec(seq), _group_spec(seq), _qrow_spec(seq), _qrow_spec(seq)] + _hbm_specs(n),
        scratch_shapes=[pltpu.VMEM((NH, n_q, TQ), F32), pltpu.VMEM((NH, n_q, HEAD_SLOT, TQ), F32),
                        pltpu.VMEM((NH, HEAD_SLOT, TK), F32), pltpu.VMEM((NH, HEAD_SLOT, TK), F32)] + _pair_sems(n),
        compiler_params=_serial_attn_params(),
    )(qf, qft, kf, kft, vf, o, do, dot, lse, *partials)
    return out[0], out[1], out[2], out[3], out[4:]


def _first_last_step():
    step = pl.program_id(0) * pl.num_programs(1) + pl.program_id(1)
    return step == 0, step == pl.num_programs(0) * pl.num_programs(1) - 1


def _sb_fwd(qkvb, vbt, batch, seq, shards):
    n = len(shards)

    def body(q_ref, k_ref, vt_ref, *rest):
        x_refs, (o_ref, lt_ref), out_refs = rest[:n], rest[n:n + 2], rest[n + 2:2 * n + 2]
        run_s, acc_s = rest[2 * n + 2:2 * n + 4]
        gather_start, gather_finish = _gather_plan_direct(x_refs, out_refs, *rest[2 * n + 4:])
        first_step, last_step = _first_last_step()
        pl.when(first_step)(gather_start)
        strict = _key_query_mask(lambda r, c: r < c)
        later = _tri(TK, lambda r, c: c > r)

        def tile(q0, kjs, masked):
            heads = range(NH)
            items = [(t, hh) for t in range(len(kjs)) for hh in heads]
            krows = [pl.ds(kj * TK if isinstance(kj, int) else pl.multiple_of(kj * TK, TK), TK) for kj in kjs]
            zts = [_dot_nt(k_ref[krows[t], _hcols(hh)], q_ref[pl.ds(q0, TQ), _hcols(hh)]) for t, hh in items]
            lgs = [-_softplus(zt) for zt in zts]
            if masked:
                lgs = [jnp.where(strict, lg, 0.0) for lg in lgs]
            parts = [_split2(lg) for lg in lgs]
            sufs = [_dot(later, hi) + _dot(later, lo) for hi, lo in parts]
            sums = [jnp.sum(lg, axis=0, keepdims=True) for lg in lgs]
            runs = {}
            for hh in heads:
                run = run_s[hh]
                for t in range(len(kjs)):
                    runs[t, hh] = run
                    run = run + sums[t * NH + hh]
                run_s[hh] = run
            ats = [jnp.exp(zts[i] + lgs[i] + runs[item] + sufs[i]) for i, item in enumerate(items)]
            if masked:
                ats = [jnp.where(strict, at, 0.0) for at in ats]
            for hh in heads:
                acc_s[hh] += sum(_dot(vt_ref[kjs[t], _hslot(hh), :], ats[t * NH + hh].astype(BF16))
                                 for t in range(len(kjs)))

        def q_loop(qi, _):
            q0 = pl.multiple_of(qi * TQ, TQ)
            run_s[...] = jnp.zeros_like(run_s)
            acc_s[...] = jnp.zeros_like(acc_s)
            tile(q0, [qi], True)

            def pair_loop(i, _):
                tile(q0, [qi - 1 - 2 * i, qi - 2 - 2 * i], False)
                return 0

            lax.fori_loop(0, qi // 2, pair_loop, 0)
            pl.when(qi % 2 == 1)(lambda: tile(q0, [0], False))
            o_ref[pl.ds(q0, TQ), :] = _heads_cat([_untranspose(acc_s[hh]) for hh in range(NH)]).astype(BF16)
            for hh in range(NH):
                lt_ref[hh, pl.ds(qi, 1), :] = run_s[hh]
            return 0

        lax.fori_loop(0, seq // TQ, q_loop, 0)
        pl.when(last_step)(gather_finish)

    out = pl.pallas_call(
        body, name="sb_fwd", grid=(batch, N_HEADS // NH),
        out_shape=[jax.ShapeDtypeStruct((batch * seq, D_BRANCH), BF16),
                   jax.ShapeDtypeStruct((batch, N_HEADS, seq // TQ, TQ), F32)] + _gather_shapes(shards),
        in_specs=[_group3_spec(0, seq), _group3_spec(1, seq), _tblock_spec(seq)] + _hbm_specs(n),
        out_specs=[_group_spec(seq), _qrow_spec(seq)] + _hbm_specs(n),
        scratch_shapes=[pltpu.VMEM((NH, 1, TQ), F32), pltpu.VMEM((NH, HEAD_SLOT, TQ), F32)] + _gather_sems(n),
        compiler_params=_serial_attn_params(),
    )(qkvb, qkvb, vbt, *shards)
    return out[0], out[1], out[2:]


def _sb_bwd(qkvb, kbt, do, ltot, batch, seq, chip_sums):
    n = len(chip_sums)

    def body(q_ref, k_ref, v_ref, kt_ref, do_ref, lt_ref, *rest):
        cs_refs, dqkv_ref, out_refs = rest[:n], rest[n], rest[n + 1:2 * n + 1]
        dk_acc, dv_acc, ls_s, gs_s, dqt_s = rest[2 * n + 1:2 * n + 6]
        chips_start, chips_finish = _chips_plan(cs_refs, out_refs, *rest[2 * n + 6:])
        first_step, last_step = _first_last_step()
        pl.when(first_step)(chips_start)
        strict = _key_query_mask(lambda r, c: r < c)
        upto = _tri(TK, lambda r, c: c <= r)
        before = _tri(TK, lambda r, c: c < r)
        dk_acc[...] = jnp.zeros_like(dk_acc)
        dv_acc[...] = jnp.zeros_like(dv_acc)

        def tile(qi, kj, masked):
            rows = pl.ds(pl.multiple_of(qi * TQ, TQ), TQ)
            krows = pl.ds(pl.multiple_of(kj * TK, TK), TK)
            heads = range(NH)
            qs = [q_ref[rows, _hcols(hh)] for hh in heads]
            douts = [do_ref[rows, _hcols(hh)] for hh in heads]
            zts = [_dot_nt(k_ref[krows, _hcols(hh)], qs[hh]) for hh in heads]
            das = [_dot_nt(v_ref[krows, _hcols(hh)], douts[hh]) for hh in heads]
            lgs = [-_softplus(zt) for zt in zts]
            if masked:
                lgs = [jnp.where(strict, lg, 0.0) for lg in lgs]
            parts = [_split2(lg) for lg in lgs]
            prefs = [_dot(upto, hi) + _dot(upto, lo) for hi, lo in parts]
            ats = [jnp.exp(zts[hh] + lgs[hh] + (lt_ref[hh, pl.ds(qi, 1), :] - ls_s[hh]) - prefs[hh]) for hh in heads]
            if masked:
                ats = [jnp.where(strict, at, 0.0) for at in ats]
            gts = [das[hh] * ats[hh] for hh in heads]
            us = [gs_s[hh] + _dot(before, gts[hh].astype(BF16)) for hh in heads]
            dzts = [(jnp.exp(lgs[hh]) * (gts[hh] + us[hh]) - us[hh]).astype(BF16) for hh in heads]
            for hh in heads:
                dk_acc[hh, krows, :] += _dot(dzts[hh], qs[hh])
                dv_acc[hh, krows, :] += _dot(ats[hh].astype(BF16), douts[hh])
                dqt_s[hh] += _dot(kt_ref[kj, _hslot(hh), :], dzts[hh])
                ls_s[hh] += jnp.sum(lgs[hh], axis=0, keepdims=True)
                gs_s[hh] += jnp.sum(gts[hh], axis=0, keepdims=True)

        def q_loop(qi, _):
            ls_s[...] = jnp.zeros_like(ls_s)
            gs_s[...] = jnp.zeros_like(gs_s)
            dqt_s[...] = jnp.zeros_like(dqt_s)

            def k_loop(kj, _):
                tile(qi, kj, False)
                return 0

            lax.fori_loop(0, qi, k_loop, 0)
            tile(qi, qi, True)
            dqkv_ref[0, pl.ds(pl.multiple_of(qi * TQ, TQ), TQ), :] = _heads_cat(
                [_untranspose(dqt_s[hh]) for hh in range(NH)]).astype(BF16)
            return 0

        lax.fori_loop(0, seq // TQ, q_loop, 0)
        dqkv_ref[1] = _heads_cat([dk_acc[hh] for hh in range(NH)]).astype(BF16)
        dqkv_ref[2] = _heads_cat([dv_acc[hh] for hh in range(NH)]).astype(BF16)
        pl.when(last_step)(chips_finish)

    out = pl.pallas_call(
        body, name="sb_bwd", grid=(batch, N_HEADS // NH),
        out_shape=[jax.ShapeDtypeStruct((3, batch * seq, D_BRANCH), BF16)]
        + [jax.ShapeDtypeStruct(s.shape, s.dtype) for s in chip_sums],
        in_specs=[_group3_spec(0, seq), _group3_spec(1, seq), _group3_spec(2, seq), _tblock_spec(seq),
                  _group_spec(seq), _qrow_spec(seq)] + _hbm_specs(n),
        out_specs=[pl.BlockSpec((3, seq, NH * HEAD_DIM), lambda b, g: (0, b, g))] + _hbm_specs(n),
        scratch_shapes=[pltpu.VMEM((NH, seq, HEAD_DIM), F32), pltpu.VMEM((NH, seq, HEAD_DIM), F32),
                        pltpu.VMEM((NH, 1, TQ), F32), pltpu.VMEM((NH, 1, TQ), F32),
                        pltpu.VMEM((NH, HEAD_SLOT, TQ), F32)] + _chips_sems(n),
        compiler_params=_serial_attn_params(),
    )(qkvb, qkvb, qkvb, kbt, do, ltot, *chip_sums)
    return out[0], out[1:]


def _forget_bwd(dcq_tok, dck_tok, fpre, batch, seq):
    t_len = batch * seq
    tiles = seq // TM

    def rev(i):
        return ((i // tiles) * tiles + (tiles - 1 - i % tiles), 0)

    def body(dcq_ref, dck_ref, f_ref, df_ref, db_ref, carry_ref):
        i = pl.program_id(0)

        @pl.when(i == 0)
        def _():
            db_ref[...] = jnp.zeros_like(db_ref)

        @pl.when(i % tiles == 0)
        def _():
            carry_ref[...] = jnp.zeros_like(carry_ref)

        dc = dcq_ref[...] - dck_ref[...]
        upper = _tri(TM, lambda r, c: c >= r)
        hi, mid, lo = _split3(dc)
        dlogf = carry_ref[...] + _dot(upper, hi) + _dot(upper, mid) + _dot(upper, lo)
        carry_ref[...] = carry_ref[...] + jnp.sum(dc, axis=0, keepdims=True)
        df = dlogf * _sigmoid(-f_ref[...])
        df_ref[...] = df.astype(BF16)
        db_ref[...] += jnp.sum(df, axis=0, keepdims=True)

    return pl.pallas_call(
        body, name="forget_bwd", grid=(t_len // TM,),
        out_shape=(jax.ShapeDtypeStruct((t_len, LANES), BF16), jax.ShapeDtypeStruct((1, LANES), F32)),
        in_specs=[pl.BlockSpec((TM, LANES), rev)] * 3,
        out_specs=(pl.BlockSpec((TM, LANES), rev), _acc_spec((1, LANES))),
        scratch_shapes=[pltpu.VMEM((1, LANES), F32)],
        compiler_params=_seq_params(),
    )(dcq_tok, dck_tok, fpre)


def _mix_fwd(o_fox, o_sb, gl, x, w_bf, w_bs, w_out, b_gate):
    t_len, d = x.shape

    def body(of_ref, os_ref, gl_ref, x_ref, wbf_ref, wbs_ref, wo_ref, bg_ref, x1_ref):
        br_f = _dot(of_ref[...], wbf_ref[...])
        br_s = _dot(os_ref[...], wbs_ref[...])
        ga = _sigmoid(gl_ref[:, :d].astype(F32) + bg_ref[0:1, :])
        gb = _sigmoid(gl_ref[:, d:].astype(F32) + bg_ref[1:2, :])
        merged = ga * br_f + gb * br_s
        x1_ref[...] = x_ref[...] + _dot(merged.astype(BF16), wo_ref[...])

    return pl.pallas_call(
        body, name="mix_fwd", grid=(t_len // TM,),
        out_shape=jax.ShapeDtypeStruct((t_len, d), F32),
        in_specs=[_row_spec(TM, D_BRANCH), _row_spec(TM, D_BRANCH), _row_spec(TM, 2 * d), _row_spec(TM, d),
                  _const_spec(w_bf.shape), _const_spec(w_bs.shape), _const_spec(w_out.shape), _const_spec(b_gate.shape)],
        out_specs=_row_spec(TM, d),
        compiler_params=_seq_params(),
    )(o_fox, o_sb, gl, x, w_bf, w_bs, w_out, b_gate)


def _ff_chunk(d_ff):
    return min(d_ff, 1024)


def _mlp_head_fwd_bwd(x1, p, target, g_mlp, w_up, w_down, g_ple, g_final, w_pg, w_ple):
    t_len, d = x1.shape
    d_ple = p.shape[1]
    d_ff = w_up.shape[1]
    ch = _ff_chunk(d_ff)

    def body(x1_ref, p_ref, t_ref, gm_ref, wu_ref, wd_ref, gp_ref, gf_ref, wpg_ref, wple_ref,
             a_ref, dx2_ref, h3_ref, dpre_ref, dpe_ref, loss_ref, dgp_ref, dgf_ref):
        @pl.when(pl.program_id(0) == 0)
        def _():
            loss_ref[...] = jnp.zeros_like(loss_ref)
            dgp_ref[...] = jnp.zeros_like(dgp_ref)
            dgf_ref[...] = jnp.zeros_like(dgf_ref)

        x1v = x1_ref[...]
        x1n, _ = _rms(x1v)
        h2 = (x1n * gm_ref[...]).astype(BF16)
        x2v = x1v
        for j in range(d_ff // ch):
            a = _dot(h2, wu_ref[:, j * ch:(j + 1) * ch])
            a_ref[:, j * ch:(j + 1) * ch] = a.astype(BF16)
            x2v = x2v + _dot(jnp.square(jnp.maximum(a, 0.0)).astype(BF16), wd_ref[j * ch:(j + 1) * ch, :])
        x2n, r3 = _rms(x2v)
        h3 = (x2n * gp_ref[...]).astype(BF16)
        h3_ref[...] = h3
        gate = _sigmoid(_dot(h3, wpg_ref[...]))
        pe = _dot(p_ref[...].astype(BF16), wple_ref[...])
        x3n, r4 = _rms(x2v + gate * pe)
        err = x3n * gf_ref[...] - t_ref[...]
        loss_ref[...] += jnp.full(loss_ref.shape, (0.5 / d) * jnp.sum(err * err), F32)
        dx3, dgf = _rms_bwd(err * (1.0 / d), x3n, r4, gf_ref[...])
        dgf_ref[...] += dgf
        dpe_ref[...] = (dx3 * gate).astype(BF16)
        dpre = (dx3 * pe * gate * (1.0 - gate)).astype(BF16)
        dpre_ref[...] = dpre
        dres, dgp = _rms_bwd(_dot_nt(dpre, wpg_ref[...]), x2n, r3, gp_ref[...])
        dgp_ref[...] += dgp
        dx2_ref[...] = dx3 + dres

    shp_b = jax.ShapeDtypeStruct((t_len, d), BF16)
    return pl.pallas_call(
        body, name="mlp_head_fwd_bwd", grid=(t_len // TM,),
        out_shape=(jax.ShapeDtypeStruct((t_len, d_ff), BF16), jax.ShapeDtypeStruct((t_len, d), F32), shp_b, shp_b, shp_b,
                   jax.ShapeDtypeStruct((1, LANES), F32), jax.ShapeDtypeStruct((1, d), F32),
                   jax.ShapeDtypeStruct((1, d), F32)),
        in_specs=[_row_spec(TM, d), _row_spec(TM, d_ple), _row_spec(TM, d), _const_spec((1, d)),
                  _const_spec(w_up.shape), _const_spec(w_down.shape), _const_spec((1, d)), _const_spec((1, d)),
                  _const_spec(w_pg.shape), _const_spec(w_ple.shape)],
        out_specs=(_row_spec(TM, d_ff), _row_spec(TM, d), _row_spec(TM, d), _row_spec(TM, d), _row_spec(TM, d),
                   _acc_spec((1, LANES)), _acc_spec((1, d)), _acc_spec((1, d))),
        compiler_params=_seq_params(),
    )(x1, p, target, g_mlp, w_up, w_down, g_ple, g_final, w_pg, w_ple)


def _mlp_bwd(dx2, a, x1, g_mlp, w_up, w_down):
    t_len, d = x1.shape
    d_ff = w_up.shape[1]
    ch = _ff_chunk(d_ff)

    def body(dx2_ref, a_ref, x1_ref, g_ref, wu_ref, wd_ref, dx1_ref, da_ref, h2_ref, dg_ref):
        @pl.when(pl.program_id(0) == 0)
        def _():
            dg_ref[...] = jnp.zeros_like(dg_ref)

        dx2v = dx2_ref[...]
        dx2b = dx2v.astype(BF16)
        xn, r = _rms(x1_ref[...])
        h2_ref[...] = (xn * g_ref[...]).T.astype(BF16)
        dh = jnp.zeros((TM, d), F32)
        for j in range(d_ff // ch):
            dact = _dot_nt(dx2b, wd_ref[j * ch:(j + 1) * ch, :])
            da = (dact * 2.0 * jnp.maximum(a_ref[:, j * ch:(j + 1) * ch].astype(F32), 0.0)).astype(BF16)
            da_ref[:, j * ch:(j + 1) * ch] = da
            dh = dh + _dot_nt(da, wu_ref[:, j * ch:(j + 1) * ch])
        dres, dg = _rms_bwd(dh, xn, r, g_ref[...])
        dg_ref[...] += dg
        dx1_ref[...] = dx2v + dres

    return pl.pallas_call(
        body, name="mlp_bwd", grid=(t_len // TM,),
        out_shape=(jax.ShapeDtypeStruct((t_len, d), F32), jax.ShapeDtypeStruct((t_len, d_ff), BF16),
                   jax.ShapeDtypeStruct((d, t_len), BF16), jax.ShapeDtypeStruct((1, d), F32)),
        in_specs=[_row_spec(TM, d), _row_spec(TM, d_ff), _row_spec(TM, d), _const_spec((1, d)),
                  _const_spec(w_up.shape), _const_spec(w_down.shape)],
        out_specs=(_row_spec(TM, d), _row_spec(TM, d_ff), _col_spec(d, TM), _acc_spec((1, d))),
        compiler_params=_seq_params(),
    )(dx2, a, x1, g_mlp, w_up, w_down)


def _mix_bwd(dx1, o_fox, o_sb, gl, w_bf, w_bs, w_out, b_gate):
    t_len, d = dx1.shape

    def body(dx1_ref, of_ref, os_ref, gl_ref, wbf_ref, wbs_ref, wo_ref, bg_ref,
             mg_ref, dbf_ref, dbs_ref, dgl_ref, dof_ref, dos_ref, dbg_ref, doft_ref):
        @pl.when(pl.program_id(0) == 0)
        def _():
            dbg_ref[...] = jnp.zeros_like(dbg_ref)

        dmerged = _dot_nt(dx1_ref[...].astype(BF16), wo_ref[...])
        br_f = _dot(of_ref[...], wbf_ref[...])
        br_s = _dot(os_ref[...], wbs_ref[...])
        ga = _sigmoid(gl_ref[:, :d].astype(F32) + bg_ref[0:1, :])
        gb = _sigmoid(gl_ref[:, d:].astype(F32) + bg_ref[1:2, :])
        mg_ref[...] = (ga * br_f + gb * br_s).astype(BF16)
        dbf = (dmerged * ga).astype(BF16)
        dbs = (dmerged * gb).astype(BF16)
        dbf_ref[...] = dbf
        dbs_ref[...] = dbs
        dla = dmerged * br_f * ga * (1.0 - ga)
        dlb = dmerged * br_s * gb * (1.0 - gb)
        dgl_ref[:, :d] = dla.astype(BF16)
        dgl_ref[:, d:] = dlb.astype(BF16)
        dbg_ref[0:1, :] += jnp.sum(dla, axis=0, keepdims=True)
        dbg_ref[1:2, :] += jnp.sum(dlb, axis=0, keepdims=True)
        dof = _dot_nt(dbf, wbf_ref[...])
        dof_ref[...] = dof.astype(BF16)
        doft_ref[0] = _slot_rows(dof.T, jnp.zeros((HEAD_DIM, TM), F32)).astype(BF16)
        dos_ref[...] = _dot_nt(dbs, wbs_ref[...]).astype(BF16)

    shp_d = jax.ShapeDtypeStruct((t_len, d), BF16)
    shp_h = jax.ShapeDtypeStruct((t_len, D_BRANCH), BF16)
    return pl.pallas_call(
        body, name="mix_bwd", grid=(t_len // TM,),
        out_shape=(shp_d, shp_d, shp_d, jax.ShapeDtypeStruct((t_len, 2 * d), BF16), shp_h, shp_h,
                   jax.ShapeDtypeStruct((2, d), F32),
                   jax.ShapeDtypeStruct((t_len // TM, N_HEADS * HEAD_SLOT, TM), BF16)),
        in_specs=[_row_spec(TM, d), _row_spec(TM, D_BRANCH), _row_spec(TM, D_BRANCH), _row_spec(TM, 2 * d),
                  _const_spec(w_bf.shape), _const_spec(w_bs.shape), _const_spec(w_out.shape), _const_spec(b_gate.shape)],
        out_specs=(_row_spec(TM, d), _row_spec(TM, d), _row_spec(TM, d), _row_spec(TM, 2 * d),
                   _row_spec(TM, D_BRANCH), _row_spec(TM, D_BRANCH), _acc_spec((2, d)),
                   pl.BlockSpec((1, N_HEADS * HEAD_SLOT, TM), lambda i: (i, 0, 0))),
        compiler_params=_seq_params(),
    )(dx1, o_fox, o_sb, gl, w_bf, w_bs, w_out, b_gate)


def _inproj_bwd(dqk_f, dv_f, dqkv_b, dgl, df, dx1, x, g_mix, w_pad, w_qk, chip_sums):
    t_len, d = x.shape
    lay, _ = _pad_layout(d)
    n = len(chip_sums)
    n_tiles = t_len // TM

    def body(dqk_ref, dvf_ref, db_ref, dgl_ref, df_ref, dx1_ref, x_ref, g_ref, w_ref, wqk_ref, *rest):
        cs_refs, (dx_ref, dg_ref), out_refs = rest[:n], rest[n:n + 2], rest[n + 2:2 * n + 2]
        chips_start, chips_finish = _chips_plan(cs_refs, out_refs, *rest[2 * n + 2:])

        @pl.when(pl.program_id(0) == 0)
        def _():
            dg_ref[...] = jnp.zeros_like(dg_ref)
            chips_start()

        def back(piece, name):
            lo, hi = lay[name]
            return _dot(piece, w_ref[lo:hi, :])

        xn, r = _rms(x_ref[...])
        dh = (back(df_ref[...], "forget") + back(dgl_ref[...], "gates") + _dot(dqk_ref[0], wqk_ref[:D_BRANCH, :])
              + _dot(dqk_ref[1], wqk_ref[D_BRANCH:, :]) + back(dvf_ref[...], "vf") + back(db_ref[0], "qb")
              + back(db_ref[1], "kb") + back(db_ref[2], "vb"))
        dres, dg = _rms_bwd(dh, xn, r, g_ref[...])
        dg_ref[...] += dg
        dx_ref[...] = dx1_ref[...] + dres
        pl.when(pl.program_id(0) == n_tiles - 1)(chips_finish)

    out = pl.pallas_call(
        body, name="inproj_bwd", grid=(n_tiles,),
        out_shape=[jax.ShapeDtypeStruct((t_len, d), F32), jax.ShapeDtypeStruct((1, d), F32)]
        + [jax.ShapeDtypeStruct(s.shape, s.dtype) for s in chip_sums],
        in_specs=[_row3_spec(2, TM, D_BRANCH), _row_spec(TM, D_BRANCH), _row3_spec(3, TM, D_BRANCH),
                  _row_spec(TM, 2 * d), _row_spec(TM, LANES), _row_spec(TM, d), _row_spec(TM, d), _const_spec((1, d)),
                  _const_spec(w_pad.shape), _const_spec(w_qk.shape)] + _hbm_specs(n),
        out_specs=[_row_spec(TM, d), _acc_spec((1, d))] + _hbm_specs(n),
        scratch_shapes=_chips_sems(n),
        compiler_params=_seq_params(),
    )(dqk_f, dv_f, dqkv_b, dgl, df, dx1, x, g_mix, w_pad, w_qk, *chip_sums)
    return out[0], out[1], out[2:]


def _cols_to_slabs(full):
    r, c8 = full.shape
    return full.reshape(r, N_DEV, c8 // N_DEV).transpose(1, 0, 2)


def _slabs_to_cols(slabs):
    n, r, c = slabs.shape
    return slabs.transpose(1, 0, 2).reshape(r, n * c)


def _win_sizes(d):
    return (D_BRANCH, D_BRANCH, D_BRANCH, N_HEADS, D_BRANCH, D_BRANCH, D_BRANCH, d, d)


def _split_win(w_t, d):
    out, off = [], 0
    for s in _win_sizes(d):
        out.append(w_t[off:off + s])
        off += s
    return out


def _to_slots(w_t):
    c = w_t.shape[1]
    return jnp.pad(w_t.reshape(N_HEADS, HEAD_DIM, c), ((0, 0), (0, HEAD_SLOT - HEAD_DIM), (0, 0))).reshape(-1, c)


def _pad_win(w_full_t, d):
    qa, ka, va, fa, qb, kb, vb, ga, gb = _split_win(w_full_t, d)
    scale = HEAD_DIM ** -0.5
    fpad = jnp.pad(fa, ((0, LANES - N_HEADS), (0, 0)))
    w_pad = jnp.concatenate([_to_slots(qa * scale), _to_slots(ka), va, qb * scale, kb, vb, ga, gb, fpad], axis=0)
    return w_pad, jnp.concatenate([qa * scale, ka], axis=0)


def _unpad_dwin(dqk_f, dv_f, dqkv_b, dgates, dforget, d):
    scale = HEAD_DIM ** -0.5
    return jnp.concatenate([dqk_f[0] * scale, dqk_f[1], dv_f, dforget[:N_HEADS],
                            dqkv_b[0] * scale, dqkv_b[1], dqkv_b[2], dgates], axis=0)


def _c_lane_constants():
    row = jnp.arange(LANES)[:, None]
    lane = jnp.arange(N_HEADS * HEAD_SLOT)[None, :]

    def place(first):
        return ((lane // HEAD_SLOT == row % N_HEADS) & (lane % HEAD_SLOT == first + row // N_HEADS)
                & (row < 3 * N_HEADS)).astype(BF16)

    def ones(first):
        off = lane % HEAD_SLOT
        return ((off >= first) & (off < first + 3)).astype(F32)

    return place(C_TERMS_Q), place(C_TERMS_K), ones(C_ONES_Q), ones(C_ONES_K)


def _pad_rows(a, rows):
    return jnp.pad(a, [(0, 0)] * (a.ndim - 2) + [(0, rows - a.shape[-2]), (0, 0)])


def kernel(x, p, g_mix, w_in, b_forget, b_gate, w_branch_fox, w_branch_sb, w_out, g_mlp, w_up, w_down, g_ple, w_ple_gate, w_ple, g_final, loss_target, m_g_mix, m_w_in, m_b_forget, m_b_gate, m_w_branch_fox, m_w_branch_sb, m_w_out, m_g_mlp, m_w_up, m_w_down, m_g_ple, m_w_ple_gate, m_w_ple, m_g_final, v_g_mix, v_w_in, v_b_forget, v_b_gate, v_w_branch_fox, v_w_branch_sb, v_w_out, v_g_mlp, v_w_up, v_w_down, v_g_ple, v_w_ple_gate, v_w_ple, v_g_final):
    batch, seq, d = x.shape
    t_len = batch * seq
    d_ple = p.shape[-1]
    d_ff = w_up.shape[-1] * N_DEV
    dn = d // N_DEV
    fn = d_ff // N_DEV
    my_c = lax.axis_index("c")
    my_dev = 4 * lax.axis_index("x") + 2 * lax.axis_index("y") + my_c

    bg_hi = b_gate[0].astype(BF16)
    bg_r = b_gate[0] - bg_hi.astype(F32)
    bg_mid = bg_r.astype(BF16)
    bg_lo = (bg_r - bg_mid.astype(F32)).astype(BF16)
    narrow_rows = 2 * D_BRANCH + d_ple + 6
    narrow_rows_pad = -(-narrow_rows // 16) * 16
    narrow = _pad_rows(jnp.concatenate(
        [w_branch_fox[0].astype(BF16), w_branch_sb[0].astype(BF16), w_ple[0].astype(BF16), bg_hi, bg_mid, bg_lo],
        axis=0), narrow_rows_pad)
    g_in, = _all_gather([w_in[0].T.astype(BF16)])
    w_pad, w_qk = _pad_win(g_in.reshape(-1, d), d)
    bf_pad = jnp.pad(b_forget, ((0, 0), (0, LANES - N_HEADS)))
    place_q, place_k, ones_q, ones_k = _c_lane_constants()

    x2d = x.reshape(t_len, d)
    p2d = p.reshape(t_len, d_ple)
    tgt2d = loss_target.reshape(t_len, d)
    qf, kf, kft, vf, vft, qkvb, kbt, vbt, gl, fpre, h1, qft = _inproj_fwd(
        x2d, g_mix, w_pad, bf_pad, place_q, place_k, ones_q, ones_k, seq)
    o_sb, ltot, (g_up, g_out, g_down, g_pg, g_narrow) = _sb_fwd(qkvb, vbt, batch, seq, [
        w_up[0].astype(BF16), w_out[0].astype(BF16), w_down[0].astype(BF16), w_ple_gate[0].astype(BF16), narrow])
    o_fox, lse = _fox_fwd(qf, kf, vft, batch, seq)
    w_up_full = _slabs_to_cols(g_up)
    w_out_full = g_out.reshape(d, d)
    w_down_full = g_down.reshape(d_ff, d)
    w_pg_full = g_pg.reshape(d, d)
    w_bf_full = _slabs_to_cols(g_narrow[:, :D_BRANCH])
    w_bs_full = _slabs_to_cols(g_narrow[:, D_BRANCH:2 * D_BRANCH])
    w_ple_full = _slabs_to_cols(g_narrow[:, 2 * D_BRANCH:2 * D_BRANCH + d_ple])
    bg_terms = g_narrow[:, 2 * D_BRANCH + d_ple:narrow_rows].astype(F32)
    b_gate_full = _slabs_to_cols(bg_terms[:, 0:2] + bg_terms[:, 2:4] + bg_terms[:, 4:6])
    x1 = _mix_fwd(o_fox, o_sb, gl, x2d, w_bf_full, w_bs_full, w_out_full, b_gate_full)

    a_up, dx2, h3, dpre, dpe, loss_acc, dg_ple, dg_final = _mlp_head_fwd_bwd(
        x1, p2d, tgt2d, g_mlp, w_up_full, w_down_full, g_ple, g_final.reshape(1, d), w_pg_full, w_ple_full)
    dx1, da_up, h2t, dg_mlp = _mlp_bwd(dx2, a_up, x1, g_mlp, w_up_full, w_down_full)
    merged, dbr_f, dbr_s, dgl, do_fox, do_sb, dbg, do_fox_t = _mix_bwd(
        dx1, o_fox, o_sb, gl, w_bf_full, w_bs_full, w_out_full, b_gate_full)

    def column_shards(name, lhs, rhs, lhs_t=False):
        if (rhs.shape[-1] // N_DEV) % (4 * LANES) == 0:
            return _matmul_tn(name, lhs, rhs, slabs=True, lhs_t=lhs_t)
        return _cols_to_slabs(_matmul_tn(name, lhs, rhs, lhs_t=lhs_t))

    if fn % (4 * LANES) == 0:
        part_up, = _matmul_tn_once("dw_up", [h2t], da_up, slabs=True, lhs_t=True)
    else:
        part_up = column_shards("dw_up", h2t, da_up, lhs_t=True)
    part_out = _matmul_tn("dw_out", merged, dx1).reshape(N_DEV, dn, d)
    part_down = _matmul_tn_once("dw_down", [a_up], dx2, relu2=True)[0].reshape(N_DEV, fn, d)
    part_pg = _matmul_tn("dw_ple_gate", h3, dpre).reshape(N_DEV, dn, d)
    part_narrow = _pad_rows(jnp.concatenate(
        [column_shards("dw_branch_fox", o_fox, dbr_f), column_shards("dw_branch_sb", o_sb, dbr_s),
         column_shards("dw_ple", p2d, dpe)], axis=1), narrow_rows_pad)
    early = [part_up, part_out, part_down, part_pg, lax.optimization_barrier(part_narrow)]

    dqk_f, dv_f, dc_queries, dc_keys, early_recv = _fox_bwd(
        qf, qft, kf, kft, vf, o_fox, do_fox, do_fox_t, lse, batch, seq, early)
    early_sums = [_pair_add("pair_add_%d" % i, pt, rc, my_c) for i, (pt, rc) in enumerate(zip(early, early_recv))]
    dqkv_b, (s_up, s_out, s_down, s_pg, s_narrow) = _sb_bwd(qkvb, kbt, do_sb, ltot, batch, seq, early_sums)
    dcq_tok = dc_queries.reshape(batch, N_HEADS, seq).transpose(0, 2, 1).reshape(t_len, N_HEADS)
    dck_tok = dc_keys.reshape(batch, N_HEADS, seq).transpose(0, 2, 1).reshape(t_len, N_HEADS)
    lane_pad = ((0, 0), (0, LANES - N_HEADS))
    df, db_forget = _forget_bwd(jnp.pad(dcq_tok, lane_pad), jnp.pad(dck_tok, lane_pad), fpre, batch, seq)

    gw_in = _unpad_dwin(*_matmul_tn_once("dw_in_fox_qk", [dqk_f], h1),
                        *_matmul_tn_once("dw_in_rest", [dv_f, dqkv_b, dgl, df], h1), d)
    part_in = lax.optimization_barrier(gw_in.reshape(N_DEV, -1, d))
    recv_in, = _rs_core_pair("reduce_scatter_core_pair_w_in", [part_in])
    grad_x, dg_mix, (s_in,) = _inproj_bwd(dqk_f, dv_f, dqkv_b, dgl, df, dx1, x2d, g_mix, w_pad, w_qk,
                                          [_pair_add("pair_add_w_in", part_in, recv_in, my_c)])

    small = jnp.concatenate([
        dg_mix, dg_mlp, dg_ple, dg_final, jnp.pad(db_forget[:, :N_HEADS], ((0, 0), (0, d - N_HEADS))), dbg,
        jnp.pad(loss_acc[:, :1], ((0, 0), (0, d - 1)))], axis=0)
    small = _all_reduce_small(small)
    loss = small[7, 0]
    small_grads = {
        "g_mix": small[0:1], "g_mlp": small[1:2], "g_ple": small[2:3], "g_final": small[3:4],
        "b_forget": small[4:5, :N_HEADS],
        "b_gate": lax.dynamic_slice_in_dim(small[5:7], my_dev * dn, dn, axis=1),
    }

    weights = {"g_mix": g_mix, "w_in": w_in, "b_forget": b_forget, "b_gate": b_gate, "w_branch_fox": w_branch_fox,
               "w_branch_sb": w_branch_sb, "w_out": w_out, "g_mlp": g_mlp, "w_up": w_up, "w_down": w_down,
               "g_ple": g_ple, "w_ple_gate": w_ple_gate, "w_ple": w_ple, "g_final": g_final}
    m_in = {"g_mix": m_g_mix, "w_in": m_w_in, "b_forget": m_b_forget, "b_gate": m_b_gate,
            "w_branch_fox": m_w_branch_fox, "w_branch_sb": m_w_branch_sb, "w_out": m_w_out, "g_mlp": m_g_mlp,
            "w_up": m_w_up, "w_down": m_w_down, "g_ple": m_g_ple, "w_ple_gate": m_w_ple_gate, "w_ple": m_w_ple,
            "g_final": m_g_final}
    v_in = {"g_mix": v_g_mix, "w_in": v_w_in, "b_forget": v_b_forget, "b_gate": v_b_gate,
            "w_branch_fox": v_w_branch_fox, "w_branch_sb": v_w_branch_sb, "w_out": v_w_out, "g_mlp": v_g_mlp,
            "w_up": v_w_up, "w_down": v_w_down, "g_ple": v_g_ple, "w_ple_gate": v_w_ple_gate, "w_ple": v_w_ple,
            "g_final": v_g_final}
    names = list(weights)

    def as2d(a):
        return a.reshape(-1, a.shape[-1])

    result = {}
    big = {"w_up": (s_up, 0), "w_out": (s_out, 0), "w_down": (s_down, 0), "w_ple_gate": (s_pg, 0),
           "w_branch_fox": (s_narrow, 0), "w_branch_sb": (s_narrow, D_BRANCH), "w_ple": (s_narrow, 2 * D_BRANCH)}
    for n, (parts, off) in big.items():
        result[n] = _adamw_parts("adamw_" + n, as2d(weights[n]), parts, off, as2d(m_in[n]), as2d(v_in[n]))
    result["w_in"] = tuple(r.T for r in _adamw_parts("adamw_w_in", w_in[0].T, s_in, 0, m_w_in[0].T, v_w_in[0].T))
    small_names = list(small_grads)
    small_out = _adamw_small([(as2d(weights[n]), small_grads[n], as2d(m_in[n]), as2d(v_in[n])) for n in small_names])
    for n, (dlt, nm, nv) in zip(small_names, small_out):
        result[n] = (small_grads[n], dlt, nm, nv)
    outs = [[result[n][k].reshape(weights[n].shape) for n in names] for k in range(4)]
    return (loss, grad_x.reshape(x.shape), *outs[0], *outs[1], *outs[2], *outs[3])
```
